```python
import jax, jax.numpy as jnp
from jax import lax
import numpy as np

D_MODEL = 1024
BATCH = 8
SEQ = 4096
DEPTH = 2

LRU_WIDTH = D_MODEL
LRU_HEADS = 8
LRU_HEAD_DIM = LRU_WIDTH // LRU_HEADS
LRU_C = 8.0
CONV_WIDTH = 4
DN_HEADS = 8
DN_HEAD_K = 128
DN_HEAD_V = 128
DN_KEY_DIM = DN_HEADS * DN_HEAD_K
DN_VALUE_DIM = DN_HEADS * DN_HEAD_V
DN_CONV_DIM = 2 * DN_KEY_DIM + DN_VALUE_DIM
CHUNK = 64
D_MIX = LRU_WIDTH + DN_VALUE_DIM
D_IN = 2 * LRU_WIDTH + DN_CONV_DIM + DN_VALUE_DIM + 2 * DN_HEADS
SPLITS = [LRU_WIDTH, 2 * LRU_WIDTH, 2 * LRU_WIDTH + DN_CONV_DIM,
          2 * LRU_WIDTH + DN_CONV_DIM + DN_VALUE_DIM,
          2 * LRU_WIDTH + DN_CONV_DIM + DN_VALUE_DIM + DN_HEADS]
EPS = 1e-6

kernel_name = "hymba_rglru_gated_deltanet_trunk"


def rmsnorm(x, w):
    xf = x.astype(jnp.float32)
    y = xf * lax.rsqrt(jnp.mean(xf * xf, axis=-1, keepdims=True) + EPS)
    return (y * w.astype(jnp.float32)).astype(x.dtype)


def gated_rmsnorm(o, z, w):
    of = o.astype(jnp.float32)
    y = of * lax.rsqrt(jnp.mean(of * of, axis=-1, keepdims=True) + EPS)
    return (y * w.astype(jnp.float32) * jax.nn.silu(z.astype(jnp.float32))).astype(o.dtype)


def l2norm(x):
    return x * lax.rsqrt(jnp.sum(x * x, axis=-1, keepdims=True) + EPS)


def causal_dwconv(x, w):
    K = w.shape[0]
    S = x.shape[1]
    xp = jnp.pad(x, ((0, 0), (K - 1, 0), (0, 0)))
    y = xp[:, 0:S] * w[0]
    for j in range(1, K):
        y = y + xp[:, j:j + S] * w[j]
    return y


def rg_lru(x, wa, ba, wx, bx, lam):
    B_, S_, W = x.shape
    xh = x.reshape(B_, S_, LRU_HEADS, LRU_HEAD_DIM)
    r = jax.nn.sigmoid(jnp.einsum('bshi,hij->bshj', xh, wa).reshape(B_, S_, W) + ba)
    i = jax.nn.sigmoid(jnp.einsum('bshi,hij->bshj', xh, wx).reshape(B_, S_, W) + bx)
    log_a = -LRU_C * r * jax.nn.softplus(-lam)
    a = jnp.exp(log_a)
    mult = jnp.sqrt(-jnp.expm1(2.0 * log_a))
    b = mult * (i * x)

    def combine(left, right):
        a_l, b_l = left
        a_r, b_r = right
        return a_l * a_r, a_r * b_l + b_r

    _, h = lax.associative_scan(combine, (a, b), axis=1)
    return h


def gated_delta_rule(q, k, v, g, beta):
    B_, S_, H, dk = q.shape
    dv = v.shape[-1]
    N = S_ // CHUNK

    def to_chunks(t):
        return t.reshape(B_, N, CHUNK, H, t.shape[-1]).transpose(0, 3, 1, 2, 4)

    q, k, v = to_chunks(q), to_chunks(k), to_chunks(v)
    g = g.reshape(B_, N, CHUNK, H).transpose(0, 3, 1, 2)
    beta = beta.reshape(B_, N, CHUNK, H).transpose(0, 3, 1, 2)
    k_beta = k * beta[..., None]
    v_beta = v * beta[..., None]
    gc = jnp.cumsum(g, axis=-1)
    tri = jnp.tril(jnp.ones((CHUNK, CHUNK), dtype=bool))
    strict = jnp.tril(jnp.ones((CHUNK, CHUNK), dtype=bool), -1)
    decay_mask = jnp.exp(jnp.where(tri, gc[..., :, None] - gc[..., None, :], -jnp.inf))
    A = jnp.where(strict, jnp.einsum('bhncd,bhnjd->bhncj', k_beta, k) * decay_mask, 0.0)
    T = A + jnp.eye(CHUNK, dtype=A.dtype)
    u = lax.linalg.triangular_solve(T, v_beta, left_side=True, lower=True, unit_diagonal=True)
    w = lax.linalg.triangular_solve(T, k_beta * jnp.exp(gc)[..., None],
                                    left_side=True, lower=True, unit_diagonal=True)
    attn = jnp.where(tri, jnp.einsum('bhncd,bhnjd->bhncj', q, k) * decay_mask, 0.0)

    def step(state, xs):
        q_i, k_i, u_i, w_i, attn_i, gc_i = xs
        v_new = u_i - jnp.einsum('bhcd,bhde->bhce', w_i, state)
        o_i = (jnp.einsum('bhcd,bhde->bhce', q_i * jnp.exp(gc_i)[..., None], state)
               + jnp.einsum('bhcj,bhje->bhce', attn_i, v_new))
        g_last = gc_i[..., -1]
        state = (state * jnp.exp(g_last)[..., None, None]
                 + jnp.einsum('bhcd,bhce->bhde', k_i * jnp.exp(g_last[..., None] - gc_i)[..., None], v_new))
        return state, o_i

    xs = tuple(jnp.moveaxis(t, 2, 0) for t in (q, k, u, w, attn, gc))
    state0 = jnp.zeros((B_, H, dk, dv), dtype=q.dtype)
    _, o = lax.scan(step, state0, xs)
    return o.transpose(1, 0, 3, 2, 4).reshape(B_, S_, H, dv)


def hybrid_layer(x, norm_w, w_in, lru_conv_w, lru_conv_b, lru_wa, lru_ba, lru_wx, lru_bx,
                 lru_lambda, lru_norm_w, dn_conv_w, dn_A_log, dn_dt_bias, dn_norm_w, w_out):
    B_, S_, _ = x.shape
    f32 = jnp.float32
    h = rmsnorm(x, norm_w)
    proj = jnp.einsum('bsd,de->bse', h, w_in)
    lru_x, lru_z, dn_qkv, dn_z, dn_b, dn_a = jnp.split(proj, SPLITS, axis=-1)

    xc = (causal_dwconv(lru_x, lru_conv_w) + lru_conv_b).astype(f32)
    hl = rg_lru(xc, lru_wa.astype(f32), lru_ba.astype(f32), lru_wx.astype(f32),
                lru_bx.astype(f32), lru_lambda.astype(f32))
    y_lru = gated_rmsnorm(hl.reshape(B_, S_, LRU_HEADS, LRU_HEAD_DIM),
                          lru_z.reshape(B_, S_, LRU_HEADS, LRU_HEAD_DIM),
                          lru_norm_w.reshape(LRU_HEADS, LRU_HEAD_DIM))

    qkv = jax.nn.silu(causal_dwconv(dn_qkv, dn_conv_w)).astype(f32)
    q, k, v = jnp.split(qkv, [DN_KEY_DIM, 2 * DN_KEY_DIM], axis=-1)
    q = l2norm(q.reshape(B_, S_, DN_HEADS, DN_HEAD_K)) * (DN_HEAD_K ** -0.5)
    k = l2norm(k.reshape(B_, S_, DN_HEADS, DN_HEAD_K))
    v = v.reshape(B_, S_, DN_HEADS, DN_HEAD_V)
    beta = jax.nn.sigmoid(dn_b.astype(f32))
    g = -jnp.exp(dn_A_log.astype(f32)) * jax.nn.softplus(dn_a.astype(f32) + dn_dt_bias.astype(f32))
    o = gated_delta_rule(q, k, v, g, beta)
    y_dn = gated_rmsnorm(o, dn_z.reshape(B_, S_, DN_HEADS, DN_HEAD_V), dn_norm_w)

    y = jnp.concatenate([y_lru.reshape(B_, S_, LRU_WIDTH), y_dn.reshape(B_, S_, DN_VALUE_DIM)],
                        axis=-1).astype(x.dtype)
    return x + jnp.einsum('bse,ed->bsd', y, w_out).astype(x.dtype)


def _fwd_setup_inputs(seed: int = 0) -> dict:
    key = jax.random.key(seed)
    ks = jax.random.split(key, 20)
    nrm = jax.random.normal
    L = DEPTH
    x = nrm(ks[0], (BATCH, SEQ, D_MODEL), jnp.float32)
    norm_w = 1.0 + 0.01 * nrm(ks[1], (L, D_MODEL), jnp.float32)
    w_in = nrm(ks[2], (L, D_MODEL, D_IN), jnp.float32) * D_MODEL ** -0.5
    lru_conv_w = nrm(ks[3], (L, CONV_WIDTH, LRU_WIDTH), jnp.float32) * CONV_WIDTH ** -0.5
    lru_conv_b = 0.01 * nrm(ks[4], (L, LRU_WIDTH), jnp.float32)
    lru_wa = nrm(ks[5], (L, LRU_HEADS, LRU_HEAD_DIM, LRU_HEAD_DIM), jnp.float32) * LRU_HEAD_DIM ** -0.5
    lru_ba = 0.01 * nrm(ks[6], (L, LRU_WIDTH), jnp.float32)
    lru_wx = nrm(ks[7], (L, LRU_HEADS, LRU_HEAD_DIM, LRU_HEAD_DIM), jnp.float32) * LRU_HEAD_DIM ** -0.5
    lru_bx = 0.01 * nrm(ks[8], (L, LRU_WIDTH), jnp.float32)
    a0 = jax.random.uniform(ks[9], (L, LRU_WIDTH), jnp.float32, 0.9, 0.999)
    s = a0 ** (1.0 / LRU_C)
    lru_lambda = jnp.log(s) - jnp.log1p(-s)
    lru_norm_w = 1.0 + 0.01 * nrm(ks[10], (L, LRU_WIDTH), jnp.float32)
    dn_conv_w = nrm(ks[11], (L, CONV_WIDTH, DN_CONV_DIM), jnp.float32) * CONV_WIDTH ** -0.5
    dn_A_log = jnp.log(jax.random.uniform(ks[12], (L, DN_HEADS), jnp.float32, 1.0, 16.0))
    dt = jnp.exp(jax.random.uniform(ks[13], (L, DN_HEADS), jnp.float32,
                                    float(np.log(1e-3)), float(np.log(1e-1))))
    dn_dt_bias = dt + jnp.log(-jnp.expm1(-dt))
    dn_norm_w = 1.0 + 0.01 * nrm(ks[14], (L, DN_HEAD_V), jnp.float32)
    w_out = nrm(ks[15], (L, D_MIX, D_MODEL), jnp.float32) * D_MIX ** -0.5
    final_norm_w = 1.0 + 0.01 * nrm(ks[16], (D_MODEL,), jnp.float32)
    return {"x": x, "norm_w": norm_w, "w_in": w_in, "lru_conv_w": lru_conv_w,
            "lru_conv_b": lru_conv_b, "lru_wa": lru_wa, "lru_ba": lru_ba, "lru_wx": lru_wx,
            "lru_bx": lru_bx, "lru_lambda": lru_lambda, "lru_norm_w": lru_norm_w,
            "dn_conv_w": dn_conv_w, "dn_A_log": dn_A_log, "dn_dt_bias": dn_dt_bias,
            "dn_norm_w": dn_norm_w, "w_out": w_out, "final_norm_w": final_norm_w}


def _fwd_reference(x, norm_w, w_in, lru_conv_w, lru_conv_b, lru_wa, lru_ba, lru_wx, lru_bx,
              lru_lambda, lru_norm_w, dn_conv_w, dn_A_log, dn_dt_bias, dn_norm_w, w_out,
              final_norm_w):
    h = x
    for l in range(DEPTH):
        h = hybrid_layer(h, norm_w[l], w_in[l], lru_conv_w[l], lru_conv_b[l], lru_wa[l], lru_ba[l],
                         lru_wx[l], lru_bx[l], lru_lambda[l], lru_norm_w[l], dn_conv_w[l],
                         dn_A_log[l], dn_dt_bias[l], dn_norm_w[l], w_out[l])
    return rmsnorm(h, final_norm_w)


import jax as _jax
import jax.numpy as _jnp

TWIN_FORMAT = 'train_step'
FWD_PARAMS = ['x', 'norm_w', 'w_in', 'lru_conv_w', 'lru_conv_b', 'lru_wa', 'lru_ba', 'lru_wx', 'lru_bx', 'lru_lambda', 'lru_norm_w', 'dn_conv_w', 'dn_A_log', 'dn_dt_bias', 'dn_norm_w', 'w_out', 'final_norm_w']
TWIN_WEIGHTS = ['norm_w', 'w_in', 'lru_conv_w', 'lru_conv_b', 'lru_wa', 'lru_ba', 'lru_wx', 'lru_bx', 'lru_lambda', 'lru_norm_w', 'dn_conv_w', 'dn_A_log', 'dn_dt_bias', 'dn_norm_w', 'w_out', 'final_norm_w']
TWIN_DIFF_INPUT = 'x'
TWIN_INPUTS = ['x', 'norm_w', 'w_in', 'lru_conv_w', 'lru_conv_b', 'lru_wa', 'lru_ba', 'lru_wx', 'lru_bx', 'lru_lambda', 'lru_norm_w', 'dn_conv_w', 'dn_A_log', 'dn_dt_bias', 'dn_norm_w', 'w_out', 'final_norm_w', 'loss_target', 'm_norm_w', 'm_w_in', 'm_lru_conv_w', 'm_lru_conv_b', 'm_lru_wa', 'm_lru_ba', 'm_lru_wx', 'm_lru_bx', 'm_lru_lambda', 'm_lru_norm_w', 'm_dn_conv_w', 'm_dn_A_log', 'm_dn_dt_bias', 'm_dn_norm_w', 'm_w_out', 'm_final_norm_w', 'v_norm_w', 'v_w_in', 'v_lru_conv_w', 'v_lru_conv_b', 'v_lru_wa', 'v_lru_ba', 'v_lru_wx', 'v_lru_bx', 'v_lru_lambda', 'v_lru_norm_w', 'v_dn_conv_w', 'v_dn_A_log', 'v_dn_dt_bias', 'v_dn_norm_w', 'v_w_out', 'v_final_norm_w']
TWIN_OUTPUTS = ['loss', 'grad_x', 'grad_norm_w', 'grad_w_in', 'grad_lru_conv_w', 'grad_lru_conv_b', 'grad_lru_wa', 'grad_lru_ba', 'grad_lru_wx', 'grad_lru_bx', 'grad_lru_lambda', 'grad_lru_norm_w', 'grad_dn_conv_w', 'grad_dn_A_log', 'grad_dn_dt_bias', 'grad_dn_norm_w', 'grad_w_out', 'grad_final_norm_w', 'delta_norm_w', 'delta_w_in', 'delta_lru_conv_w', 'delta_lru_conv_b', 'delta_lru_wa', 'delta_lru_ba', 'delta_lru_wx', 'delta_lru_bx', 'delta_lru_lambda', 'delta_lru_norm_w', 'delta_dn_conv_w', 'delta_dn_A_log', 'delta_dn_dt_bias', 'delta_dn_norm_w', 'delta_w_out', 'delta_final_norm_w', 'new_m_norm_w', 'new_m_w_in', 'new_m_lru_conv_w', 'new_m_lru_conv_b', 'new_m_lru_wa', 'new_m_lru_ba', 'new_m_lru_wx', 'new_m_lru_bx', 'new_m_lru_lambda', 'new_m_lru_norm_w', 'new_m_dn_conv_w', 'new_m_dn_A_log', 'new_m_dn_dt_bias', 'new_m_dn_norm_w', 'new_m_w_out', 'new_m_final_norm_w', 'new_v_norm_w', 'new_v_w_in', 'new_v_lru_conv_w', 'new_v_lru_conv_b', 'new_v_lru_wa', 'new_v_lru_ba', 'new_v_lru_wx', 'new_v_lru_bx', 'new_v_lru_lambda', 'new_v_lru_norm_w', 'new_v_dn_conv_w', 'new_v_dn_A_log', 'new_v_dn_dt_bias', 'new_v_dn_norm_w', 'new_v_w_out', 'new_v_final_norm_w']
TWIN_LEAF_KINDS = {'loss': 'loss', 'grad_x': 'grad_x', 'grad_norm_w': 'grad_w', 'grad_w_in': 'grad_w', 'grad_lru_conv_w': 'grad_w', 'grad_lru_conv_b': 'grad_w', 'grad_lru_wa': 'grad_w', 'grad_lru_ba': 'grad_w', 'grad_lru_wx': 'grad_w', 'grad_lru_bx': 'grad_w', 'grad_lru_lambda': 'grad_w', 'grad_lru_norm_w': 'grad_w', 'grad_dn_conv_w': 'grad_w', 'grad_dn_A_log': 'grad_w', 'grad_dn_dt_bias': 'grad_w', 'grad_dn_norm_w': 'grad_w', 'grad_w_out': 'grad_w', 'grad_final_norm_w': 'grad_w', 'delta_norm_w': 'delta_w', 'delta_w_in': 'delta_w', 'delta_lru_conv_w': 'delta_w', 'delta_lru_conv_b': 'delta_w', 'delta_lru_wa': 'delta_w', 'delta_lru_ba': 'delta_w', 'delta_lru_wx': 'delta_w', 'delta_lru_bx': 'delta_w', 'delta_lru_lambda': 'delta_w', 'delta_lru_norm_w': 'delta_w', 'delta_dn_conv_w': 'delta_w', 'delta_dn_A_log': 'delta_w', 'delta_dn_dt_bias': 'delta_w', 'delta_dn_norm_w': 'delta_w', 'delta_w_out': 'delta_w', 'delta_final_norm_w': 'delta_w', 'new_m_norm_w': 'new_m', 'new_m_w_in': 'new_m', 'new_m_lru_conv_w': 'new_m', 'new_m_lru_conv_b': 'new_m', 'new_m_lru_wa': 'new_m', 'new_m_lru_ba': 'new_m', 'new_m_lru_wx': 'new_m', 'new_m_lru_bx': 'new_m', 'new_m_lru_lambda': 'new_m', 'new_m_lru_norm_w': 'new_m', 'new_m_dn_conv_w': 'new_m', 'new_m_dn_A_log': 'new_m', 'new_m_dn_dt_bias': 'new_m', 'new_m_dn_norm_w': 'new_m', 'new_m_w_out': 'new_m', 'new_m_final_norm_w': 'new_m', 'new_v_norm_w': 'new_v', 'new_v_w_in': 'new_v', 'new_v_lru_conv_w': 'new_v', 'new_v_lru_conv_b': 'new_v', 'new_v_lru_wa': 'new_v', 'new_v_lru_ba': 'new_v', 'new_v_lru_wx': 'new_v', 'new_v_lru_bx': 'new_v', 'new_v_lru_lambda': 'new_v', 'new_v_lru_norm_w': 'new_v', 'new_v_dn_conv_w': 'new_v', 'new_v_dn_A_log': 'new_v', 'new_v_dn_dt_bias': 'new_v', 'new_v_dn_norm_w': 'new_v', 'new_v_w_out': 'new_v', 'new_v_final_norm_w': 'new_v'}


def _forward(args):
    return _fwd_reference(*[args[k] for k in FWD_PARAMS])


def _output_shape():
    out = _jax.eval_shape(lambda: _forward(_fwd_setup_inputs(0)))
    return out.shape, out.dtype

N_MICROBATCH = 1
ADAM_LR = 0.001
ADAM_B1 = 0.9
ADAM_B2 = 0.999
ADAM_EPS = 1e-08
ADAM_WD = 0.01
ADAM_STEP = 10
PER_EXAMPLE_BATCH_AXIS = {'x': 0, 'loss_target': 0}
SHARED_INPUTS = []
_WEIGHT_DTYPES = {'norm_w': _jnp.float32, 'w_in': _jnp.float32, 'lru_conv_w': _jnp.float32, 'lru_conv_b': _jnp.float32, 'lru_wa': _jnp.float32, 'lru_ba': _jnp.float32, 'lru_wx': _jnp.float32, 'lru_bx': _jnp.float32, 'lru_lambda': _jnp.float32, 'lru_norm_w': _jnp.float32, 'dn_conv_w': _jnp.float32, 'dn_A_log': _jnp.float32, 'dn_dt_bias': _jnp.float32, 'dn_norm_w': _jnp.float32, 'w_out': _jnp.float32, 'final_norm_w': _jnp.float32}
MOMENT_SCALE = {'norm_w': 1.509076e-01, 'w_in': 5.929156e-02, 'lru_conv_w': 7.489177e-02, 'lru_conv_b': 8.021660e-01, 'lru_wa': 2.163298e-02, 'lru_ba': 1.930323e-02, 'lru_wx': 3.874120e-02, 'lru_bx': 2.431357e-02, 'lru_lambda': 3.622238e-02, 'lru_norm_w': 7.608926e-02, 'dn_conv_w': 4.908996e-02, 'dn_A_log': 7.500476e-01, 'dn_dt_bias': 7.067760e-01, 'dn_norm_w': 1.878284e-01, 'w_out': 9.491285e-02, 'final_norm_w': 3.198388e+01}


def _to_microbatches(a, axis):
    t = _jnp.moveaxis(a, axis, 0)
    t = t.reshape((N_MICROBATCH, t.shape[0] // N_MICROBATCH) + t.shape[1:])
    return _jnp.moveaxis(t, 1, axis + 1)


def setup_inputs(seed: int = 0) -> dict:
    inp = _fwd_setup_inputs(seed)
    key = _jax.random.fold_in(_jax.random.key(seed), 7919)
    shape, _ = _output_shape()
    out = dict(inp)
    out["loss_target"] = _jax.random.normal(_jax.random.fold_in(key, 0), shape, _jnp.float32)
    for i, name in enumerate(TWIN_WEIGHTS):
        w = inp[name].astype(_jnp.float32)
        if MOMENT_SCALE is None:
            s = _jnp.sqrt(_jnp.mean(_jnp.square(w)) + 1e-30)
        else:
            s = MOMENT_SCALE[name]
        km, kv = _jax.random.split(_jax.random.fold_in(key, i + 1))
        out[name] = w
        out["m_" + name] = s * _jax.random.normal(km, w.shape, _jnp.float32)
        out["v_" + name] = (s * s) * _jax.random.uniform(kv, w.shape, _jnp.float32, 0.5, 1.5)
    if N_MICROBATCH > 1:
        for name, axis in PER_EXAMPLE_BATCH_AXIS.items():
            out[name] = _to_microbatches(out[name], axis)
    return {'x': out['x'], 'norm_w': out['norm_w'], 'w_in': out['w_in'], 'lru_conv_w': out['lru_conv_w'], 'lru_conv_b': out['lru_conv_b'], 'lru_wa': out['lru_wa'], 'lru_ba': out['lru_ba'], 'lru_wx': out['lru_wx'], 'lru_bx': out['lru_bx'], 'lru_lambda': out['lru_lambda'], 'lru_norm_w': out['lru_norm_w'], 'dn_conv_w': out['dn_conv_w'], 'dn_A_log': out['dn_A_log'], 'dn_dt_bias': out['dn_dt_bias'], 'dn_norm_w': out['dn_norm_w'], 'w_out': out['w_out'], 'final_norm_w': out['final_norm_w'], 'loss_target': out['loss_target'], 'm_norm_w': out['m_norm_w'], 'm_w_in': out['m_w_in'], 'm_lru_conv_w': out['m_lru_conv_w'], 'm_lru_conv_b': out['m_lru_conv_b'], 'm_lru_wa': out['m_lru_wa'], 'm_lru_ba': out['m_lru_ba'], 'm_lru_wx': out['m_lru_wx'], 'm_lru_bx': out['m_lru_bx'], 'm_lru_lambda': out['m_lru_lambda'], 'm_lru_norm_w': out['m_lru_norm_w'], 'm_dn_conv_w': out['m_dn_conv_w'], 'm_dn_A_log': out['m_dn_A_log'], 'm_dn_dt_bias': out['m_dn_dt_bias'], 'm_dn_norm_w': out['m_dn_norm_w'], 'm_w_out': out['m_w_out'], 'm_final_norm_w': out['m_final_norm_w'], 'v_norm_w': out['v_norm_w'], 'v_w_in': out['v_w_in'], 'v_lru_conv_w': out['v_lru_conv_w'], 'v_lru_conv_b': out['v_lru_conv_b'], 'v_lru_wa': out['v_lru_wa'], 'v_lru_ba': out['v_lru_ba'], 'v_lru_wx': out['v_lru_wx'], 'v_lru_bx': out['v_lru_bx'], 'v_lru_lambda': out['v_lru_lambda'], 'v_lru_norm_w': out['v_lru_norm_w'], 'v_dn_conv_w': out['v_dn_conv_w'], 'v_dn_A_log': out['v_dn_A_log'], 'v_dn_dt_bias': out['v_dn_dt_bias'], 'v_dn_norm_w': out['v_dn_norm_w'], 'v_w_out': out['v_w_out'], 'v_final_norm_w': out['v_final_norm_w']}


def _loss(weights, diff, rest, loss_target):
    with _jax.named_scope("forward"):
        args = {**rest, TWIN_DIFF_INPUT: diff, **{k: w.astype(_WEIGHT_DTYPES[k]) for k, w in weights.items()}}
        y = _forward(args)
    with _jax.named_scope("loss_head"):
        err = _jnp.square(y.astype(_jnp.float32) - loss_target)
        return 0.5 * _jnp.sum(_jnp.mean(err, axis=-1)) if err.ndim else 0.5 * err


def _adamw(w, g, m, v):
    m = ADAM_B1 * m + (1.0 - ADAM_B1) * g
    v = ADAM_B2 * v + (1.0 - ADAM_B2) * _jnp.square(g)
    m_hat = m / (1.0 - ADAM_B1 ** ADAM_STEP)
    v_hat = v / (1.0 - ADAM_B2 ** ADAM_STEP)
    delta = -ADAM_LR * (m_hat / (_jnp.sqrt(v_hat) + ADAM_EPS) + ADAM_WD * w)
    return delta, m, v


def reference(x, norm_w, w_in, lru_conv_w, lru_conv_b, lru_wa, lru_ba, lru_wx, lru_bx, lru_lambda, lru_norm_w, dn_conv_w, dn_A_log, dn_dt_bias, dn_norm_w, w_out, final_norm_w, loss_target, m_norm_w, m_w_in, m_lru_conv_w, m_lru_conv_b, m_lru_wa, m_lru_ba, m_lru_wx, m_lru_bx, m_lru_lambda, m_lru_norm_w, m_dn_conv_w, m_dn_A_log, m_dn_dt_bias, m_dn_norm_w, m_w_out, m_final_norm_w, v_norm_w, v_w_in, v_lru_conv_w, v_lru_conv_b, v_lru_wa, v_lru_ba, v_lru_wx, v_lru_bx, v_lru_lambda, v_lru_norm_w, v_dn_conv_w, v_dn_A_log, v_dn_dt_bias, v_dn_norm_w, v_w_out, v_final_norm_w):
    given = dict(x=x, norm_w=norm_w, w_in=w_in, lru_conv_w=lru_conv_w, lru_conv_b=lru_conv_b, lru_wa=lru_wa, lru_ba=lru_ba, lru_wx=lru_wx, lru_bx=lru_bx, lru_lambda=lru_lambda, lru_norm_w=lru_norm_w, dn_conv_w=dn_conv_w, dn_A_log=dn_A_log, dn_dt_bias=dn_dt_bias, dn_norm_w=dn_norm_w, w_out=w_out, final_norm_w=final_norm_w, loss_target=loss_target, m_norm_w=m_norm_w, m_w_in=m_w_in, m_lru_conv_w=m_lru_conv_w, m_lru_conv_b=m_lru_conv_b, m_lru_wa=m_lru_wa, m_lru_ba=m_lru_ba, m_lru_wx=m_lru_wx, m_lru_bx=m_lru_bx, m_lru_lambda=m_lru_lambda, m_lru_norm_w=m_lru_norm_w, m_dn_conv_w=m_dn_conv_w, m_dn_A_log=m_dn_A_log, m_dn_dt_bias=m_dn_dt_bias, m_dn_norm_w=m_dn_norm_w, m_w_out=m_w_out, m_final_norm_w=m_final_norm_w, v_norm_w=v_norm_w, v_w_in=v_w_in, v_lru_conv_w=v_lru_conv_w, v_lru_conv_b=v_lru_conv_b, v_lru_wa=v_lru_wa, v_lru_ba=v_lru_ba, v_lru_wx=v_lru_wx, v_lru_bx=v_lru_bx, v_lru_lambda=v_lru_lambda, v_lru_norm_w=v_lru_norm_w, v_dn_conv_w=v_dn_conv_w, v_dn_A_log=v_dn_A_log, v_dn_dt_bias=v_dn_dt_bias, v_dn_norm_w=v_dn_norm_w, v_w_out=v_w_out, v_final_norm_w=v_final_norm_w)
    weights = {n: given[n] for n in TWIN_WEIGHTS}
    shared = {n: given[n] for n in SHARED_INPUTS}
    per_example = {n: given[n] for n in ['x']}
    grad_fn = _jax.value_and_grad(_loss, argnums=(0, 1))

    def one_microbatch(ex, loss_target):
        ex = dict(ex)
        diff = ex.pop(TWIN_DIFF_INPUT)
        return grad_fn(weights, diff, {**shared, **ex}, loss_target)

    if N_MICROBATCH == 1:
        loss, (grad_w, grad_x) = one_microbatch(per_example, given["loss_target"])
    else:
        def body(carry, xs):
            loss_sum, grad_sum = carry
            l_k, (gw_k, gx_k) = one_microbatch(xs[0], xs[1])
            with _jax.named_scope("update"):
                return (loss_sum + l_k, _jax.tree.map(_jnp.add, grad_sum, gw_k)), gx_k

        init = (_jnp.zeros((), _jnp.float32), _jax.tree.map(_jnp.zeros_like, weights))
        (loss, grad_w), grad_x = _jax.lax.scan(body, init, (per_example, given["loss_target"]))
    with _jax.named_scope("update"):
        delta_w, new_m, new_v = {}, {}, {}
        for n in TWIN_WEIGHTS:
            delta_w[n], new_m[n], new_v[n] = _adamw(weights[n], grad_w[n], given["m_" + n], given["v_" + n])
    return (loss, grad_x, *[grad_w[n] for n in TWIN_WEIGHTS], *[delta_w[n] for n in TWIN_WEIGHTS],
            *[new_m[n] for n in TWIN_WEIGHTS], *[new_v[n] for n in TWIN_WEIGHTS])
```

```python
import functools

import jax
import jax.numpy as jnp
from jax import lax
from jax.experimental import pallas as pl
from jax.experimental.pallas import tpu as pltpu

F32 = jnp.float32
BF16 = jnp.bfloat16

N_DEV = 8
D_MODEL = 1024
HEADS = 8
HEAD_DIM = 128
CHUNK = 64
D_IN = 6160
D_IN_PAD = 6272
COL_LRU_X, COL_LRU_Z, COL_Q, COL_K, COL_V, COL_DN_Z, COL_BA = 0, 8, 16, 24, 32, 40, 48
LRU_C = 8.0
EPS = 1e-6
ADAM_LR, ADAM_B1, ADAM_B2, ADAM_EPS, ADAM_WD, ADAM_STEP = 0.001, 0.9, 0.999, 1e-08, 0.01, 10
TIME_BLOCK = 512
PACK_ROWS = 512
VMEM_LIMIT = 56 * 1024 * 1024

NN = (((1,), (0,)), ((), ()))
NT = (((1,), (1,)), ((), ()))
TN = (((0,), (0,)), ((), ()))


def _dot(a, b, dims, prec):
    if prec == "bf16":
        return lax.dot_general(a.astype(BF16), b.astype(BF16), dims, preferred_element_type=F32)
    return lax.dot_general(a, b, dims, precision=lax.Precision.HIGHEST, preferred_element_type=F32)


def _make_mm(prec):
    @jax.custom_vjp
    def nn(a, b):
        return _dot(a, b, NN, prec)

    @jax.custom_vjp
    def nt(a, b):
        return _dot(a, b, NT, prec)

    @jax.custom_vjp
    def tn(a, b):
        return _dot(a, b, TN, prec)

    nn.defvjp(lambda a, b: (_dot(a, b, NN, prec), (a, b)),
              lambda r, g: (_dot(g, r[1], NT, prec), _dot(r[0], g, TN, prec)))
    nt.defvjp(lambda a, b: (_dot(a, b, NT, prec), (a, b)),
              lambda r, g: (_dot(g, r[1], NN, prec), _dot(g, r[0], TN, prec)))
    tn.defvjp(lambda a, b: (_dot(a, b, TN, prec), (a, b)),
              lambda r, g: (_dot(r[1], g, NT, prec), _dot(r[0], g, NN, prec)))
    return nn, nt, tn


_NN_B, _NT_B, _TN_B = _make_mm("bf16")
_NN_F, _NT_F, _TN_F = _make_mm("f32")


@jax.custom_vjp
def _unit_lower_inverse(a):
    n = a.shape[0]
    eye = (lax.broadcasted_iota(jnp.int32, (n, n), 0) == lax.broadcasted_iota(jnp.int32, (n, n), 1)).astype(F32)
    inv = eye - a
    pw = _dot(a, a, NN, "f32")
    steps = n.bit_length() - 2
    for j in range(steps):
        inv = inv + _dot(inv, pw, NN, "f32")
        if j + 1 < steps:
            pw = _dot(pw, pw, NN, "f32")
    return inv


def _uli_fwd(a):
    inv = _unit_lower_inverse(a)
    return inv, inv


def _uli_bwd(inv, g):
    t = _dot(inv, g, TN, "f32")
    return (-_dot(t, inv, NT, "f32"),)


_unit_lower_inverse.defvjp(_uli_fwd, _uli_bwd)


def _expm1(x):
    small = x * (1.0 + x * (0.5 + x * (1.0 / 6 + x * (1.0 / 24 + x * (1.0 / 120 + x * (1.0 / 720))))))
    return jnp.where(jnp.abs(x) < 0.2, small, jnp.exp(x) - 1.0)


def _sigmoid(x):
    return 1.0 / (1.0 + jnp.exp(-x))


def _silu(x):
    return x * _sigmoid(x)


def _softplus(x):
    return jnp.maximum(x, 0.0) + jnp.log(1.0 + jnp.exp(-jnp.abs(x)))


def _rmsnorm(x, w):
    return x * lax.rsqrt(jnp.mean(x * x, axis=-1, keepdims=True) + EPS) * w


def _gated_norm(o, z, w):
    return o * lax.rsqrt(jnp.mean(o * o, axis=-1, keepdims=True) + EPS) * w * _silu(z)


def _lru_gates(xc, wa, ba, wx, bx, lam):
    r = _sigmoid(_NN_B(xc, wa) + ba)
    i = _sigmoid(_NN_B(xc, wx) + bx)
    log_a = -LRU_C * r * _softplus(-lam)
    a = jnp.exp(log_a)
    mult = jnp.sqrt(-_expm1(2.0 * log_a))
    return a, mult * (i * xc)


def _lane_pick(row, lane_index):
    lane = lax.broadcasted_iota(jnp.int32, row.shape, 1)
    return jnp.sum(jnp.where(lane == lane_index, row, 0.0), axis=-1, keepdims=True)


def _dn_prep(qc, kc, vc, ba, a_log_row, dt_row, head):
    q = _silu(qc)
    k = _silu(kc)
    v = _silu(vc)
    q = q * lax.rsqrt(jnp.sum(q * q, axis=-1, keepdims=True) + EPS) * (HEAD_DIM ** -0.5)
    k = k * lax.rsqrt(jnp.sum(k * k, axis=-1, keepdims=True) + EPS)
    beta = _sigmoid(_lane_pick(ba, head))
    g = -jnp.exp(_lane_pick(a_log_row, head)) * _softplus(_lane_pick(ba, HEADS + head) + _lane_pick(dt_row, head))
    return q, k, v, g, beta


def _dn_chunk(q, k, v, gcol, bcol, state):
    c = q.shape[0]
    row = lax.broadcasted_iota(jnp.int32, (c, c), 0)
    col = lax.broadcasted_iota(jnp.int32, (c, c), 1)
    tri = row >= col
    tri_f = tri.astype(F32)
    g_wide = jnp.broadcast_to(gcol, (c, HEAD_DIM))
    b_wide = jnp.broadcast_to(bcol, (c, HEAD_DIM))
    gc = _NN_F(tri_f, g_wide)
    gc_rows = _NN_F(tri_f, jnp.broadcast_to(gcol, (c, c)))
    decay = jnp.exp(jnp.where(tri, gc_rows - gc_rows.T, -1e30))
    kb = k * b_wide
    vb = v * b_wide
    eg = jnp.exp(gc)
    a = jnp.where(row > col, _NT_B(kb, k) * decay, 0.0)
    tinv = _unit_lower_inverse(a)
    u = _NN_B(tinv, vb)
    w = _NN_B(tinv, kb * eg)
    attn = _NT_B(q, k) * decay
    v_new = u - _NN_B(w, state)
    o = _NN_B(q * eg, state) + _NN_B(attn, v_new)
    g_last = jnp.sum(g_wide, axis=0, keepdims=True)
    new_state = state * jnp.exp(g_last) + _TN_B(k * jnp.exp(g_last - gc), v_new)
    return o, new_state


def _conv_taps(buf, head, cw, rows):
    acc = cw[0:1, :] * buf[head, pl.ds(5, rows), :]
    for j in range(1, 4):
        acc = acc + cw[j:j + 1, :] * buf[head, pl.ds(5 + j, rows), :]
    return acc


def _conv_backward(dbuf, dhead, xbuf, xhead, cw, dxc, rows):
    dbuf[dhead, pl.ds(0, rows), :] = dxc
    dx = cw[0:1, :] * dbuf[dhead, pl.ds(3, rows), :]
    for j in range(1, 4):
        dx = dx + cw[j:j + 1, :] * dbuf[dhead, pl.ds(3 - j, rows), :]
    dcw = jnp.concatenate(
        [jnp.sum(dxc * xbuf[xhead, pl.ds(5 + j, rows), :], axis=0, keepdims=True) for j in range(4)], axis=0)
    dbuf[dhead, pl.ds(rows, 8), :] = dbuf[dhead, pl.ds(0, 8), :]
    return dx, dcw


def _params(**kw):
    return pltpu.CompilerParams(vmem_limit_bytes=VMEM_LIMIT, **kw)


def _matmul(a, b, form, tm, tn, tk, name, add=None, out_dtype=F32):
    if form == "nn":
        (m, kdim), (_, n) = a.shape, b.shape
        a_spec = pl.BlockSpec((tm, tk), lambda j, i, k: (i, k))
        b_spec = pl.BlockSpec((tk, tn), lambda j, i, k: (k, j))
        dims = NN
    elif form == "nt":
        (m, kdim), (n, _) = a.shape, b.shape
        a_spec = pl.BlockSpec((tm, tk), lambda j, i, k: (i, k))
        b_spec = pl.BlockSpec((tn, tk), lambda j, i, k: (j, k))
        dims = NT
    else:
        (kdim, m), (_, n) = a.shape, b.shape
        a_spec = pl.BlockSpec((tk, tm), lambda j, i, k: (k, i))
        b_spec = pl.BlockSpec((tk, tn), lambda j, i, k: (k, j))
        dims = TN
    assert m % tm == 0 and n % tn == 0 and kdim % tk == 0, (name, m, n, kdim, tm, tn, tk)
    ksteps = kdim // tk
    o_spec = pl.BlockSpec((tm, tn), lambda j, i, k: (i, j))
    has_add = add is not None

    def body(*refs):
        if has_add:
            a_ref, b_ref, c_ref, o_ref, acc = refs
        else:
            a_ref, b_ref, o_ref, acc = refs
        k = pl.program_id(2)

        @pl.when(k == 0)
        def _():
            acc[...] = c_ref[...] if has_add else jnp.zeros_like(acc)

        acc[...] += lax.dot_general(a_ref[...].astype(BF16), b_ref[...].astype(BF16), dims,
                                    preferred_element_type=F32)

        @pl.when(k == ksteps - 1)
        def _():
            o_ref[...] = acc[...].astype(o_ref.dtype)

    in_specs = [a_spec, b_spec] + ([o_spec] if has_add else [])
    args = (a, b) + ((add,) if has_add else ())
    return pl.pallas_call(
        body, name=name, grid=(n // tn, m // tm, ksteps), in_specs=in_specs, out_specs=o_spec,
        out_shape=jax.ShapeDtypeStruct((m, n), out_dtype), scratch_shapes=[pltpu.VMEM((tm, tn), F32)],
        compiler_params=_params(dimension_semantics=("parallel", "parallel", "arbitrary")),
    )(*args)


def _rmsnorm_fwd(x, w_row, name):
    s = x.shape[0]
    tb = min(TIME_BLOCK, s)

    def body(x_ref, w_ref, o_ref):
        o_ref[...] = _rmsnorm(x_ref[...], w_ref[...]).astype(BF16)

    return pl.pallas_call(
        body, name=name, grid=(s // tb,),
        in_specs=[pl.BlockSpec((tb, D_MODEL), lambda i: (i, 0)), pl.BlockSpec((1, D_MODEL), lambda i: (0, 0))],
        out_specs=pl.BlockSpec((tb, D_MODEL), lambda i: (i, 0)),
        out_shape=jax.ShapeDtypeStruct((s, D_MODEL), BF16), compiler_params=_params(),
    )(x, w_row)


def _rmsnorm_bwd(x, w_row, dh, dres, name):
    s = x.shape[0]
    tb = min(TIME_BLOCK, s)

    def body(x_ref, w_ref, dh_ref, dres_ref, dx_ref, dw_ref):
        _, vjp = jax.vjp(_rmsnorm, x_ref[...], w_ref[...])
        dx, dw = vjp(dh_ref[...])
        dx_ref[...] = dres_ref[...] + dx

        @pl.when(pl.program_id(0) == 0)
        def _():
            dw_ref[...] = jnp.zeros_like(dw_ref)

        dw_ref[...] += dw

    row = pl.BlockSpec((tb, D_MODEL), lambda i: (i, 0))
    vec = pl.BlockSpec((1, D_MODEL), lambda i: (0, 0))
    return pl.pallas_call(
        body, name=name, grid=(s // tb,), in_specs=[row, vec, row, row], out_specs=[row, vec],
        out_shape=[jax.ShapeDtypeStruct((s, D_MODEL), F32), jax.ShapeDtypeStruct((1, D_MODEL), F32)],
        compiler_params=_params(),
    )(x, w_row, dh, dres)


def _final_loss(x, w_row, target, name):
    s = x.shape[0]
    tb = min(TIME_BLOCK, s)

    def loss_fn(xv, wv, tv):
        err = _rmsnorm(xv, wv) - tv
        return 0.5 * jnp.sum(jnp.sum(err * err, axis=-1, keepdims=True), axis=0, keepdims=True) * (1.0 / D_MODEL)

    def body(x_ref, w_ref, t_ref, loss_ref, dx_ref, dw_ref):
        tv = t_ref[...]
        loss, vjp = jax.vjp(lambda xv, wv: loss_fn(xv, wv, tv), x_ref[...], w_ref[...])
        dx, dw = vjp(jnp.ones((1, 1), F32))
        dx_ref[...] = dx

        @pl.when(pl.program_id(0) == 0)
        def _():
            dw_ref[...] = jnp.zeros_like(dw_ref)
            loss_ref[...] = jnp.zeros_like(loss_ref)

        dw_ref[...] += dw
        loss_ref[...] += jnp.broadcast_to(loss, loss_ref.shape)

    row = pl.BlockSpec((tb, D_MODEL), lambda i: (i, 0))
    vec = pl.BlockSpec((1, D_MODEL), lambda i: (0, 0))
    return pl.pallas_call(
        body, name=name, grid=(s // tb,), in_specs=[row, vec, row],
        out_specs=[pl.BlockSpec((1, HEAD_DIM), lambda i: (0, 0)), row, vec],
        out_shape=[jax.ShapeDtypeStruct((1, HEAD_DIM), F32), jax.ShapeDtypeStruct((s, D_MODEL), F32),
                   jax.ShapeDtypeStruct((1, D_MODEL), F32)],
        compiler_params=_params(),
    )(x, w_row, target)


def _head_specs(tb, time_of):
    def col(off):
        return pl.BlockSpec((tb, HEAD_DIM), lambda t, h: (time_of(t), off + h))
    return col


def _vec_spec():
    return pl.BlockSpec((1, HEAD_DIM), lambda t, h: (0, h))


def _lru_fwd(proj, conv_w, conv_b, wa, ba, wx, bx, lam, nw, name):
    s = proj.shape[0]
    tb = min(TIME_BLOCK, s)
    nt = s // tb
    col = _head_specs(tb, lambda t: t)

    def body(x_ref, z_ref, cw_ref, cb_ref, wa_ref, ba_ref, wx_ref, bx_ref, lam_ref, nw_ref,
             y_ref, hs_ref, xbuf, hcar, a_s, b_s):
        t, h = pl.program_id(0), pl.program_id(1)

        @pl.when(t == 0)
        def _():
            xbuf[h, pl.ds(0, 8), :] = jnp.zeros((8, HEAD_DIM), F32)
            hcar[h] = jnp.zeros((8, HEAD_DIM), F32)

        xbuf[h, pl.ds(8, tb), :] = x_ref[...]
        xc = _conv_taps(xbuf, h, cw_ref[...], tb) + cb_ref[...]
        a, b = _lru_gates(xc, wa_ref[...], ba_ref[...], wx_ref[...], bx_ref[...], lam_ref[...])
        a_s[...] = a
        b_s[...] = b

        def step(r, hprev):
            hnew = a_s[pl.ds(r, 1), :] * hprev + b_s[pl.ds(r, 1), :]
            hs_ref[pl.ds(r, 1), :] = hnew
            return hnew

        hlast = lax.fori_loop(0, tb, step, hcar[h, pl.ds(0, 1), :], unroll=8)
        hcar[h, pl.ds(0, 1), :] = hlast
        xbuf[h, pl.ds(0, 8), :] = xbuf[h, pl.ds(tb, 8), :]
        y_ref[...] = _gated_norm(hs_ref[...], z_ref[...], nw_ref[...]).astype(BF16)

    vec = _vec_spec()
    return pl.pallas_call(
        body, name=name, grid=(nt, HEADS),
        in_specs=[col(COL_LRU_X), col(COL_LRU_Z), pl.BlockSpec((4, HEAD_DIM), lambda t, h: (0, h)), vec,
                  pl.BlockSpec((None, HEAD_DIM, HEAD_DIM), lambda t, h: (h, 0, 0)), vec,
                  pl.BlockSpec((None, HEAD_DIM, HEAD_DIM), lambda t, h: (h, 0, 0)), vec, vec, vec],
        out_specs=[col(0), col(0)],
        out_shape=[jax.ShapeDtypeStruct((s, D_MODEL), BF16), jax.ShapeDtypeStruct((s, D_MODEL), F32)],
        scratch_shapes=[pltpu.VMEM((HEADS, tb + 8, HEAD_DIM), F32), pltpu.VMEM((HEADS, 8, HEAD_DIM), F32),
                        pltpu.VMEM((tb, HEAD_DIM), F32), pltpu.VMEM((tb, HEAD_DIM), F32)],
        compiler_params=_params(dimension_semantics=("arbitrary", "arbitrary")),
    )(proj, proj, conv_w, conv_b, wa, ba, wx, bx, lam, nw)


def _halo_spec(tb, nt, off):
    per = tb // 8
    return pl.BlockSpec((8, HEAD_DIM), lambda t, h: (jnp.maximum((nt - 1 - t) * per - 1, 0), off + h))


def _lru_bwd(proj, hs, dy, conv_w, conv_b, wa, ba, wx, bx, lam, nw, name):
    s = proj.shape[0]
    tb = min(TIME_BLOCK, s)
    nt = s // tb
    col = _head_specs(tb, lambda t: nt - 1 - t)

    def body(x_ref, xh_ref, z_ref, hs_ref, hh_ref, dy_ref, cw_ref, cb_ref, wa_ref, ba_ref, wx_ref, bx_ref,
             lam_ref, nw_ref, dx_ref, dz_ref, dcw_ref, dcb_ref, dwa_ref, dba_ref, dwx_ref, dbx_ref, dlam_ref,
             dnw_ref, xbuf, hbuf, dbuf, gcar, a_s, g_s):
        t, h = pl.program_id(0), pl.program_id(1)
        first_block = t == nt - 1

        @pl.when(t == 0)
        def _():
            dbuf[h, pl.ds(tb, 8), :] = jnp.zeros((8, HEAD_DIM), F32)
            gcar[h] = jnp.zeros((8, HEAD_DIM), F32)
            dcw_ref[h] = jnp.zeros((4, HEAD_DIM), F32)
            dwa_ref[h] = jnp.zeros((HEAD_DIM, HEAD_DIM), F32)
            dwx_ref[h] = jnp.zeros((HEAD_DIM, HEAD_DIM), F32)
            for ref in (dcb_ref, dba_ref, dbx_ref, dlam_ref, dnw_ref):
                ref[h] = jnp.zeros((1, HEAD_DIM), F32)

        keep = jnp.where(first_block, 0.0, 1.0)
        xbuf[0, pl.ds(0, 8), :] = xh_ref[...] * keep
        xbuf[0, pl.ds(8, tb), :] = x_ref[...]
        hbuf[pl.ds(0, 8), :] = hh_ref[...] * keep
        hbuf[pl.ds(8, tb), :] = hs_ref[...]
        cw = cw_ref[...]
        xc = _conv_taps(xbuf, 0, cw, tb) + cb_ref[...]
        (a, _), gates_vjp = jax.vjp(_lru_gates, xc, wa_ref[...], ba_ref[...], wx_ref[...], bx_ref[...], lam_ref[...])
        a_s[...] = a
        _, norm_vjp = jax.vjp(_gated_norm, hs_ref[...], z_ref[...], nw_ref[...])
        dh, dz, dnw = norm_vjp(dy_ref[...])
        dz_ref[...] = dz.astype(dz_ref.dtype)
        g_s[...] = dh

        def step(i, car):
            r = tb - 1 - i
            g = g_s[pl.ds(r, 1), :] + car
            g_s[pl.ds(r, 1), :] = g
            return a_s[pl.ds(r, 1), :] * g

        gcar[h, pl.ds(0, 1), :] = lax.fori_loop(0, tb, step, gcar[h, pl.ds(0, 1), :], unroll=8)
        g = g_s[...]
        dxc, dwa, dba, dwx, dbx, dlam = gates_vjp((g * hbuf[pl.ds(7, tb), :], g))
        dx, dcw = _conv_backward(dbuf, h, xbuf, 0, cw, dxc, tb)
        dx_ref[...] = dx.astype(dx_ref.dtype)
        dcw_ref[h] += dcw
        dcb_ref[h] += jnp.sum(dxc, axis=0, keepdims=True)
        dwa_ref[h] += dwa
        dwx_ref[h] += dwx
        dba_ref[h] += dba
        dbx_ref[h] += dbx
        dlam_ref[h] += dlam
        dnw_ref[h] += dnw

    vec = _vec_spec()
    mat = pl.BlockSpec((None, HEAD_DIM, HEAD_DIM), lambda t, h: (h, 0, 0))

    def whole(shape):
        return pl.BlockSpec(shape, lambda t, h: (0,) * len(shape))

    head_vec = jax.ShapeDtypeStruct((HEADS, 1, HEAD_DIM), F32)
    head_mat = jax.ShapeDtypeStruct((HEADS, HEAD_DIM, HEAD_DIM), F32)
    return pl.pallas_call(
        body, name=name, grid=(nt, HEADS),
        in_specs=[col(COL_LRU_X), _halo_spec(tb, nt, COL_LRU_X), col(COL_LRU_Z), col(0), _halo_spec(tb, nt, 0), col(0),
                  pl.BlockSpec((4, HEAD_DIM), lambda t, h: (0, h)), vec, mat, vec, mat, vec, vec, vec],
        out_specs=[col(0), col(0), whole((HEADS, 4, HEAD_DIM)), whole((HEADS, 1, HEAD_DIM)),
                   whole((HEADS, HEAD_DIM, HEAD_DIM)), whole((HEADS, 1, HEAD_DIM)),
                   whole((HEADS, HEAD_DIM, HEAD_DIM)), whole((HEADS, 1, HEAD_DIM)), whole((HEADS, 1, HEAD_DIM)),
                   whole((HEADS, 1, HEAD_DIM))],
        out_shape=[jax.ShapeDtypeStruct((s, D_MODEL), BF16), jax.ShapeDtypeStruct((s, D_MODEL), BF16),
                   jax.ShapeDtypeStruct((HEADS, 4, HEAD_DIM), F32), head_vec, head_mat, head_vec, head_mat, head_vec,
                   head_vec, head_vec],
        scratch_shapes=[pltpu.VMEM((1, tb + 8, HEAD_DIM), F32), pltpu.VMEM((tb + 8, HEAD_DIM), F32),
                        pltpu.VMEM((HEADS, tb + 8, HEAD_DIM), F32), pltpu.VMEM((HEADS, 8, HEAD_DIM), F32),
                        pltpu.VMEM((tb, HEAD_DIM), F32), pltpu.VMEM((tb, HEAD_DIM), F32)],
        compiler_params=_params(dimension_semantics=("arbitrary", "arbitrary")),
    )(proj, proj, proj, hs, hs, dy, conv_w, conv_b, wa, ba, wx, bx, lam, nw)


def _dn_fwd(proj, conv_w, a_log_row, dt_row, nw, name):
    s = proj.shape[0]
    tb = min(TIME_BLOCK, s)
    nt = s // tb
    nchunk = tb // CHUNK
    col = _head_specs(tb, lambda t: t)

    def body(q_ref, k_ref, v_ref, z_ref, ba_ref, cwq_ref, cwk_ref, cwv_ref, al_ref, dt_ref, nw_ref,
             y_ref, o_ref, st_ref, xbuf, state, q_s, k_s, v_s, g_s, b_s):
        t, h = pl.program_id(0), pl.program_id(1)

        @pl.when(t == 0)
        def _():
            for i in range(3):
                xbuf[3 * h + i, pl.ds(0, 8), :] = jnp.zeros((8, HEAD_DIM), F32)
            state[h] = jnp.zeros((HEAD_DIM, HEAD_DIM), F32)

        conv = []
        for i, (ref, cw_ref) in enumerate(((q_ref, cwq_ref), (k_ref, cwk_ref), (v_ref, cwv_ref))):
            xbuf[3 * h + i, pl.ds(8, tb), :] = ref[...]
            conv.append(_conv_taps(xbuf, 3 * h + i, cw_ref[...], tb))
            xbuf[3 * h + i, pl.ds(0, 8), :] = xbuf[3 * h + i, pl.ds(tb, 8), :]
        q, k, v, g, beta = _dn_prep(conv[0], conv[1], conv[2], ba_ref[...], al_ref[...], dt_ref[...], h)
        q_s[...] = q
        k_s[...] = k
        v_s[...] = v
        g_s[...] = g
        b_s[...] = beta

        def chunk(c, st):
            rows = pl.ds(pl.multiple_of(c * CHUNK, CHUNK), CHUNK)
            st_ref[c] = st
            o, st = _dn_chunk(q_s[rows, :], k_s[rows, :], v_s[rows, :], g_s[rows, :], b_s[rows, :], st)
            o_ref[rows, :] = o
            return st

        state[h] = lax.fori_loop(0, nchunk, chunk, state[h])
        y_ref[...] = _gated_norm(o_ref[...], z_ref[...], nw_ref[...]).astype(BF16)

    def cw_spec(off):
        return pl.BlockSpec((4, HEAD_DIM), lambda t, h: (0, off + h))

    row128 = pl.BlockSpec((1, HEAD_DIM), lambda t, h: (0, 0))
    return pl.pallas_call(
        body, name=name, grid=(nt, HEADS),
        in_specs=[col(COL_Q), col(COL_K), col(COL_V), col(COL_DN_Z),
                  pl.BlockSpec((tb, HEAD_DIM), lambda t, h: (t, COL_BA)),
                  cw_spec(0), cw_spec(HEADS), cw_spec(2 * HEADS), row128, row128, row128],
        out_specs=[col(0), col(0), pl.BlockSpec((None, nchunk, HEAD_DIM, HEAD_DIM), lambda t, h: (h, t, 0, 0))],
        out_shape=[jax.ShapeDtypeStruct((s, D_MODEL), BF16), jax.ShapeDtypeStruct((s, D_MODEL), F32),
                   jax.ShapeDtypeStruct((HEADS, s // CHUNK, HEAD_DIM, HEAD_DIM), F32)],
        scratch_shapes=[pltpu.VMEM((3 * HEADS, tb + 8, HEAD_DIM), F32), pltpu.VMEM((HEADS, HEAD_DIM, HEAD_DIM), F32),
                        pltpu.VMEM((tb, HEAD_DIM), F32), pltpu.VMEM((tb, HEAD_DIM), F32),
                        pltpu.VMEM((tb, HEAD_DIM), F32), pltpu.VMEM((tb, 1), F32), pltpu.VMEM((tb, 1), F32)],
        compiler_params=_params(dimension_semantics=("arbitrary", "arbitrary")),
    )(proj, proj, proj, proj, proj, conv_w, conv_w, conv_w, a_log_row, dt_row, nw)


def _dn_bwd(proj, o, states, dy, conv_w, a_log_row, dt_row, nw, name):
    s = proj.shape[0]
    tb = min(TIME_BLOCK, s)
    nt = s // tb
    nchunk = tb // CHUNK
    col = _head_specs(tb, lambda t: nt - 1 - t)

    def body(q_ref, qh_ref, k_ref, kh_ref, v_ref, vh_ref, z_ref, ba_ref, o_ref, st_ref, dy_ref,
             cwq_ref, cwk_ref, cwv_ref, al_ref, dt_ref, nw_ref,
             dq_ref, dk_ref, dv_ref, dz_ref, dba_ref, dcw_ref, dal_ref, ddt_ref, dnw_ref,
             xbuf, dbuf, dstate, q_s, k_s, v_s, g_s, b_s, dq_s, dk_s, dv_s, dg_s, db_s, do_s):
        t, h = pl.program_id(0), pl.program_id(1)
        first_block = t == nt - 1

        @pl.when(t == 0)
        def _():
            for i in range(3):
                dbuf[3 * h + i, pl.ds(tb, 8), :] = jnp.zeros((8, HEAD_DIM), F32)
                dcw_ref[3 * h + i] = jnp.zeros((4, HEAD_DIM), F32)
            dstate[h] = jnp.zeros((HEAD_DIM, HEAD_DIM), F32)

        @pl.when((t == 0) & (h == 0))
        def _():
            for ref in (dal_ref, ddt_ref, dnw_ref):
                ref[...] = jnp.zeros_like(ref)

        keep = jnp.where(first_block, 0.0, 1.0)
        cws = (cwq_ref[...], cwk_ref[...], cwv_ref[...])
        conv = []
        for i, (ref, halo) in enumerate(((q_ref, qh_ref), (k_ref, kh_ref), (v_ref, vh_ref))):
            xbuf[i, pl.ds(0, 8), :] = halo[...] * keep
            xbuf[i, pl.ds(8, tb), :] = ref[...]
            conv.append(_conv_taps(xbuf, i, cws[i], tb))
        (q, k, v, g, beta), prep_vjp = jax.vjp(
            lambda qc, kc, vc, ba, al, dt: _dn_prep(qc, kc, vc, ba, al, dt, h),
            conv[0], conv[1], conv[2], ba_ref[...], al_ref[...], dt_ref[...])
        q_s[...] = q
        k_s[...] = k
        v_s[...] = v
        g_s[...] = g
        b_s[...] = beta
        _, norm_vjp = jax.vjp(_gated_norm, o_ref[...], z_ref[...], nw_ref[...])
        do, dz, dnw = norm_vjp(dy_ref[...])
        dz_ref[...] = dz.astype(dz_ref.dtype)
        dnw_ref[...] += dnw
        do_s[...] = do

        def chunk(i, dst):
            c = nchunk - 1 - i
            rows = pl.ds(pl.multiple_of(c * CHUNK, CHUNK), CHUNK)
            _, vjp = jax.vjp(_dn_chunk, q_s[rows, :], k_s[rows, :], v_s[rows, :], g_s[rows, :], b_s[rows, :], st_ref[c])
            dq, dk, dv, dg, db, dst = vjp((do_s[rows, :], dst))
            dq_s[rows, :] = dq
            dk_s[rows, :] = dk
            dv_s[rows, :] = dv
            dg_s[rows, :] = dg
            db_s[rows, :] = db
            return dst

        dstate[h] = lax.fori_loop(0, nchunk, chunk, dstate[h])
        dqc, dkc, dvc, dba, dal, ddt = prep_vjp((dq_s[...], dk_s[...], dv_s[...], dg_s[...], db_s[...]))
        for i, (dxc, out) in enumerate(((dqc, dq_ref), (dkc, dk_ref), (dvc, dv_ref))):
            dx, dcw = _conv_backward(dbuf, 3 * h + i, xbuf, i, cws[i], dxc, tb)
            out[...] = dx.astype(out.dtype)
            dcw_ref[3 * h + i] += dcw
        dal_ref[...] += dal
        ddt_ref[...] += ddt

        @pl.when(h == 0)
        def _():
            dba_ref[...] = dba.astype(dba_ref.dtype)

        @pl.when(h > 0)
        def _():
            dba_ref[...] += dba.astype(dba_ref.dtype)

    def cw_spec(off):
        return pl.BlockSpec((4, HEAD_DIM), lambda t, h: (0, off + h))

    def whole(shape):
        return pl.BlockSpec(shape, lambda t, h: (0,) * len(shape))

    row128 = whole((1, HEAD_DIM))
    blk = (tb, HEAD_DIM)
    act = jax.ShapeDtypeStruct((s, D_MODEL), BF16)
    row_out = jax.ShapeDtypeStruct((1, HEAD_DIM), F32)
    return pl.pallas_call(
        body, name=name, grid=(nt, HEADS),
        in_specs=[col(COL_Q), _halo_spec(tb, nt, COL_Q), col(COL_K), _halo_spec(tb, nt, COL_K),
                  col(COL_V), _halo_spec(tb, nt, COL_V), col(COL_DN_Z),
                  pl.BlockSpec(blk, lambda t, h: (nt - 1 - t, COL_BA)), col(0),
                  pl.BlockSpec((None, nchunk, HEAD_DIM, HEAD_DIM), lambda t, h: (h, nt - 1 - t, 0, 0)), col(HEADS),
                  cw_spec(0), cw_spec(HEADS), cw_spec(2 * HEADS), row128, row128, row128],
        out_specs=[col(0), col(0), col(0), col(0), pl.BlockSpec(blk, lambda t, h: (nt - 1 - t, 0)),
                   whole((3 * HEADS, 4, HEAD_DIM)), row128, row128, row128],
        out_shape=[act, act, act, act, jax.ShapeDtypeStruct((s, HEAD_DIM), F32),
                   jax.ShapeDtypeStruct((3 * HEADS, 4, HEAD_DIM), F32), row_out, row_out, row_out],
        scratch_shapes=[pltpu.VMEM((3, tb + 8, HEAD_DIM), F32), pltpu.VMEM((3 * HEADS, tb + 8, HEAD_DIM), F32),
                        pltpu.VMEM((HEADS, HEAD_DIM, HEAD_DIM), F32)]
        + [pltpu.VMEM(blk, F32)] * 3 + [pltpu.VMEM((tb, 1), F32)] * 2
        + [pltpu.VMEM(blk, F32)] * 3 + [pltpu.VMEM((tb, 1), F32)] * 2 + [pltpu.VMEM(blk, F32)],
        compiler_params=_params(dimension_semantics=("arbitrary", "arbitrary")),
    )(proj, proj, proj, proj, proj, proj, proj, proj, o, states, dy, conv_w, conv_w, conv_w, a_log_row, dt_row, nw)


def _mesh_position():
    x, y, c = lax.axis_index("x"), lax.axis_index("y"), lax.axis_index("c")
    return x, y, c, 4 * x + 2 * y + c


def _peer(k, x, y, c):
    px = 1 - x if k & 4 else x
    py = 1 - y if k & 2 else y
    pc = 1 - c if k & 1 else c
    return (px, py, pc), 4 * px + 2 * py + pc


def _direct_exchange(arrays, scatter, name):
    n = len(arrays)
    out_shapes = [jax.ShapeDtypeStruct(a.shape if scatter else (N_DEV,) + a.shape, a.dtype) for a in arrays]

    def body(*refs):
        ins, outs = refs[:n], refs[n:2 * n]
        send_sems, recv_sems, local_sems = refs[2 * n:]
        x, y, c, me = _mesh_position()
        local, sends, recvs = [], [], []
        for i in range(n):
            cp = pltpu.make_async_copy(ins[i].at[me] if scatter else ins[i], outs[i].at[me], local_sems.at[i])
            cp.start()
            local.append(cp)
            for k in range(1, N_DEV):
                peer, peer_id = _peer(k, x, y, c)
                sem = i * (N_DEV - 1) + k - 1
                cp = pltpu.make_async_remote_copy(
                    src_ref=ins[i].at[peer_id] if scatter else ins[i], dst_ref=outs[i].at[me],
                    send_sem=send_sems.at[sem], recv_sem=recv_sems.at[sem],
                    device_id=peer, device_id_type=pl.DeviceIdType.MESH)
                cp.start()
                sends.append(cp)
                recvs.append(pltpu.make_async_remote_copy(
                    src_ref=ins[i].at[peer_id] if scatter else ins[i], dst_ref=outs[i].at[peer_id],
                    send_sem=send_sems.at[sem], recv_sem=recv_sems.at[sem],
                    device_id=peer, device_id_type=pl.DeviceIdType.MESH))
        for cp in recvs:
            cp.wait_recv()
        for cp in sends:
            cp.wait_send()
        for cp in local:
            cp.wait()

    hbm = pl.BlockSpec(memory_space=pl.ANY)
    return pl.pallas_call(
        body, name=name, in_specs=[hbm] * n, out_specs=[hbm] * n, out_shape=out_shapes,
        scratch_shapes=[pltpu.SemaphoreType.DMA((n * (N_DEV - 1),)), pltpu.SemaphoreType.DMA((n * (N_DEV - 1),)),
                        pltpu.SemaphoreType.DMA((n,))],
    )(*arrays)


def _adamw(parts, w, m, v, name, rows_per_step):
    rows, cols = w.shape
    tr = min(rows_per_step, rows)
    assert rows % tr == 0, (name, rows, tr)
    c1 = 1.0 / (1.0 - ADAM_B1 ** ADAM_STEP)
    c2 = 1.0 / (1.0 - ADAM_B2 ** ADAM_STEP)

    def body(p_ref, w_ref, m_ref, v_ref, g_ref, d_ref, nm_ref, nv_ref):
        g = p_ref[0]
        for d in range(1, N_DEV):
            g = g + p_ref[d]
        nm = ADAM_B1 * m_ref[...] + (1.0 - ADAM_B1) * g
        nv = ADAM_B2 * v_ref[...] + (1.0 - ADAM_B2) * (g * g)
        g_ref[...] = g
        nm_ref[...] = nm
        nv_ref[...] = nv
        d_ref[...] = -ADAM_LR * ((nm * c1) / (jnp.sqrt(nv * c2) + ADAM_EPS) + ADAM_WD * w_ref[...])

    blk = pl.BlockSpec((tr, cols), lambda i: (i, 0))
    shape = jax.ShapeDtypeStruct((rows, cols), F32)
    return pl.pallas_call(
        body, name=name, grid=(rows // tr,),
        in_specs=[pl.BlockSpec((N_DEV, tr, cols), lambda i: (0, i, 0)), blk, blk, blk],
        out_specs=[blk] * 4, out_shape=[shape] * 4, compiler_params=_params(),
    )(parts, w, m, v)


_REPLICATED = ("norm_w", "lru_conv_b", "lru_wa", "lru_ba", "lru_wx", "lru_bx", "lru_lambda", "lru_norm_w",
               "dn_A_log", "dn_dt_bias", "dn_norm_w", "final_norm_w")
_SHARDED = ("w_in", "lru_conv_w", "dn_conv_w", "w_out")
_WEIGHTS = ("norm_w", "w_in", "lru_conv_w", "lru_conv_b", "lru_wa", "lru_ba", "lru_wx", "lru_bx", "lru_lambda",
            "lru_norm_w", "dn_conv_w", "dn_A_log", "dn_dt_bias", "dn_norm_w", "w_out", "final_norm_w")


def _pack_rows(tree):
    rows = []
    for name in _REPLICATED:
        a = tree[name]
        if a.shape[-1] == HEADS:
            a = jnp.pad(a, ((0, 0), (0, HEAD_DIM - HEADS)))
        rows.append(a.reshape(-1, HEAD_DIM))
    packed = jnp.concatenate(rows, axis=0)
    return jnp.pad(packed, ((0, (-packed.shape[0]) % PACK_ROWS), (0, 0)))


def _unpack_rows(packed, like):
    out, at = {}, 0
    for name in _REPLICATED:
        shape = like[name].shape
        if shape[-1] == HEADS:
            n = shape[0]
            out[name] = packed[at:at + n, :HEADS]
        else:
            n = like[name].size // HEAD_DIM
            out[name] = packed[at:at + n].reshape(shape)
        at += n
    return out


def _heads_to_channels(a):
    return jnp.transpose(a, (1, 0, 2)).reshape(a.shape[1], HEADS * HEAD_DIM)


def kernel(x, norm_w, w_in, lru_conv_w, lru_conv_b, lru_wa, lru_ba, lru_wx, lru_bx, lru_lambda, lru_norm_w, dn_conv_w, dn_A_log, dn_dt_bias, dn_norm_w, w_out, final_norm_w, loss_target, m_norm_w, m_w_in, m_lru_conv_w, m_lru_conv_b, m_lru_wa, m_lru_ba, m_lru_wx, m_lru_bx, m_lru_lambda, m_lru_norm_w, m_dn_conv_w, m_dn_A_log, m_dn_dt_bias, m_dn_norm_w, m_w_out, m_final_norm_w, v_norm_w, v_w_in, v_lru_conv_w, v_lru_conv_b, v_lru_wa, v_lru_ba, v_lru_wx, v_lru_bx, v_lru_lambda, v_lru_norm_w, v_dn_conv_w, v_dn_A_log, v_dn_dt_bias, v_dn_norm_w, v_w_out, v_final_norm_w):
    weights = dict(norm_w=norm_w, w_in=w_in, lru_conv_w=lru_conv_w, lru_conv_b=lru_conv_b, lru_wa=lru_wa,
                   lru_ba=lru_ba, lru_wx=lru_wx, lru_bx=lru_bx, lru_lambda=lru_lambda, lru_norm_w=lru_norm_w,
                   dn_conv_w=dn_conv_w, dn_A_log=dn_A_log, dn_dt_bias=dn_dt_bias, dn_norm_w=dn_norm_w,
                   w_out=w_out, final_norm_w=final_norm_w)
    mom_m = dict(norm_w=m_norm_w, w_in=m_w_in, lru_conv_w=m_lru_conv_w, lru_conv_b=m_lru_conv_b, lru_wa=m_lru_wa,
                 lru_ba=m_lru_ba, lru_wx=m_lru_wx, lru_bx=m_lru_bx, lru_lambda=m_lru_lambda,
                 lru_norm_w=m_lru_norm_w, dn_conv_w=m_dn_conv_w, dn_A_log=m_dn_A_log, dn_dt_bias=m_dn_dt_bias,
                 dn_norm_w=m_dn_norm_w, w_out=m_w_out, final_norm_w=m_final_norm_w)
    mom_v = dict(norm_w=v_norm_w, w_in=v_w_in, lru_conv_w=v_lru_conv_w, lru_conv_b=v_lru_conv_b, lru_wa=v_lru_wa,
                 lru_ba=v_lru_ba, lru_wx=v_lru_wx, lru_bx=v_lru_bx, lru_lambda=v_lru_lambda,
                 lru_norm_w=v_lru_norm_w, dn_conv_w=v_dn_conv_w, dn_A_log=v_dn_A_log, dn_dt_bias=v_dn_dt_bias,
                 dn_norm_w=v_dn_norm_w, w_out=v_w_out, final_norm_w=v_final_norm_w)
    depth = norm_w.shape[0]
    xs = x[0]
    s = xs.shape[0]
    tm = min(512, s)

    g_win, g_lcw, g_dcw, g_wout = _direct_exchange([w_in, lru_conv_w, dn_conv_w, w_out], False, "gather_weights")
    win = jnp.transpose(g_win, (1, 2, 0, 3)).reshape(depth, D_MODEL, D_IN)
    win = jnp.pad(win, ((0, 0), (0, 0), (0, D_IN_PAD - D_IN))).astype(BF16)
    wout = jnp.transpose(g_wout, (1, 0, 2, 3)).reshape(depth, 2 * D_MODEL, D_MODEL).astype(BF16)
    lcw = jnp.transpose(g_lcw, (1, 2, 0, 3)).reshape(depth, 4, D_MODEL)
    dcw = jnp.transpose(g_dcw, (1, 2, 0, 3)).reshape(depth, 4, 3 * D_MODEL)

    def row(a):
        return a.reshape(1, -1)

    def pad_row(a):
        return jnp.pad(a, (0, HEAD_DIM - a.shape[0])).reshape(1, HEAD_DIM)

    saved = []
    cur = xs
    for l in range(depth):
        hn = _rmsnorm_fwd(cur, row(norm_w[l]), f"norm_fwd_{l}")
        proj = _matmul(hn, win[l], "nn", tm, 896, D_MODEL, f"in_proj_{l}")
        y_lru, hs = _lru_fwd(proj, lcw[l], row(lru_conv_b[l]), lru_wa[l], row(lru_ba[l]), lru_wx[l], row(lru_bx[l]),
                             row(lru_lambda[l]), row(lru_norm_w[l]), f"lru_fwd_{l}")
        y_dn, o_dn, states = _dn_fwd(proj, dcw[l], pad_row(dn_A_log[l]), pad_row(dn_dt_bias[l]), row(dn_norm_w[l]),
                                     f"dn_fwd_{l}")
        ycat = jnp.concatenate([y_lru, y_dn], axis=1)
        nxt = _matmul(ycat, wout[l], "nn", tm, D_MODEL, 2 * D_MODEL, f"out_proj_{l}", add=cur)
        saved.append((cur, hn, proj, hs, o_dn, states, ycat))
        cur = nxt
    loss_part, dx, d_final = _final_loss(cur, row(final_norm_w), loss_target[0], "final_loss")

    grads = {k: [None] * depth for k in _WEIGHTS if k != "final_norm_w"}
    for l in reversed(range(depth)):
        x_in, hn, proj, hs, o_dn, states, ycat = saved[l]
        dy = _matmul(dx, wout[l], "nt", tm, D_MODEL, D_MODEL, f"out_proj_dy_{l}")
        grads["w_out"][l] = _matmul(ycat, dx, "tn", D_MODEL, D_MODEL, tm, f"out_proj_dw_{l}")
        (dlx, dlz, g_lcw, g_lcb, g_wa, g_ba, g_wx, g_bx, g_lam, g_lnw) = _lru_bwd(
            proj, hs, dy, lcw[l], row(lru_conv_b[l]), lru_wa[l], row(lru_ba[l]), lru_wx[l], row(lru_bx[l]),
            row(lru_lambda[l]), row(lru_norm_w[l]), f"lru_bwd_{l}")
        (dq, dk, dv, ddz, dba, g_dcw3, g_al, g_dt, g_dnw) = _dn_bwd(
            proj, o_dn, states, dy, dcw[l], pad_row(dn_A_log[l]), pad_row(dn_dt_bias[l]), row(dn_norm_w[l]),
            f"dn_bwd_{l}")
        dproj = jnp.concatenate([dlx, dlz, dq, dk, dv, ddz, dba.astype(BF16)], axis=1)
        dh = _matmul(dproj, win[l], "nt", tm, D_MODEL, 896, f"in_proj_dh_{l}")
        grads["w_in"][l] = _matmul(hn, dproj, "tn", D_MODEL, 896, tm, f"in_proj_dw_{l}")[:, :D_IN]
        dx, g_nw = _rmsnorm_bwd(x_in, row(norm_w[l]), dh, dx, f"norm_bwd_{l}")
        grads["norm_w"][l] = g_nw.reshape(D_MODEL)
        grads["lru_conv_w"][l] = _heads_to_channels(g_lcw)
        grads["lru_conv_b"][l] = g_lcb.reshape(D_MODEL)
        grads["lru_wa"][l] = g_wa
        grads["lru_ba"][l] = g_ba.reshape(D_MODEL)
        grads["lru_wx"][l] = g_wx
        grads["lru_bx"][l] = g_bx.reshape(D_MODEL)
        grads["lru_lambda"][l] = g_lam.reshape(D_MODEL)
        grads["lru_norm_w"][l] = g_lnw.reshape(D_MODEL)
        g_dcw3 = g_dcw3.reshape(HEADS, 3, 4, HEAD_DIM)
        grads["dn_conv_w"][l] = jnp.concatenate([_heads_to_channels(g_dcw3[:, i]) for i in range(3)], axis=1)
        grads["dn_A_log"][l] = g_al[0, :HEADS]
        grads["dn_dt_bias"][l] = g_dt[0, :HEADS]
        grads["dn_norm_w"][l] = g_dnw.reshape(HEAD_DIM)
    part = {k: jnp.stack(v) for k, v in grads.items()}
    part["final_norm_w"] = d_final.reshape(D_MODEL)

    def to_slots(a):
        dd, r, cc = a.shape
        return jnp.transpose(a.reshape(dd, r, N_DEV, cc // N_DEV), (2, 0, 1, 3))

    slots = [to_slots(part["w_in"]), to_slots(part["lru_conv_w"]), to_slots(part["dn_conv_w"]),
             jnp.transpose(part["w_out"].reshape(depth, N_DEV, 2 * D_MODEL // N_DEV, D_MODEL), (1, 0, 2, 3))]
    rep_parts = _direct_exchange([_pack_rows(part)], False, "gather_small_grads")[0]
    r_win, r_lcw, r_dcw, r_wout = _direct_exchange(slots, True, "exchange_grads")

    new = {}
    for name, parts, rows_per_step in (("w_in", r_win, 256), ("lru_conv_w", r_lcw, 8), ("dn_conv_w", r_dcw, 8),
                                       ("w_out", r_wout, 256)):
        w = weights[name]
        flat = (-1, w.shape[-1])
        outs = _adamw(parts.reshape((N_DEV,) + (w.size // w.shape[-1], w.shape[-1])), w.reshape(flat),
                      mom_m[name].reshape(flat), mom_v[name].reshape(flat), f"adamw_{name}", rows_per_step)
        new[name] = [a.reshape(w.shape) for a in outs]
    packed = _adamw(rep_parts, _pack_rows(weights), _pack_rows(mom_m), _pack_rows(mom_v), "adamw_small", PACK_ROWS)
    unpacked = [_unpack_rows(a, weights) for a in packed]
    for name in _REPLICATED:
        new[name] = [u[name] for u in unpacked]

    loss = lax.psum(loss_part[0, 0], ("x", "y", "c"))
    out = [loss, dx.reshape(x.shape)]
    for i in range(4):
        out += [new[name][i] for name in _WEIGHTS]
    return tuple(out)
```

```python
import functools

import jax
import jax.numpy as jnp
from jax import lax
from jax.experimental import pallas as pl
from jax.experimental.pallas import tpu as pltpu

F32 = jnp.float32
BF16 = jnp.bfloat16

N_DEV = 8
D_MODEL = 1024
HEADS = 8
HEAD_DIM = 128
CHUNK = 64
D_IN = 6160
D_IN_PAD = 6272
COL_LRU_X, COL_LRU_Z, COL_Q, COL_K, COL_V, COL_DN_Z, COL_BA = 0, 8, 16, 24, 32, 40, 48
LRU_C = 8.0
EPS = 1e-6
ADAM_LR, ADAM_B1, ADAM_B2, ADAM_EPS, ADAM_WD, ADAM_STEP = 0.001, 0.9, 0.999, 1e-08, 0.01, 10
TIME_BLOCK = 512
PACK_ROWS = 512
VMEM_LIMIT = 56 * 1024 * 1024

NN = (((1,), (0,)), ((), ()))
NT = (((1,), (1,)), ((), ()))
TN = (((0,), (0,)), ((), ()))


B_NN = (((2,), (1,)), ((0,), (0,)))
B_NT = (((2,), (2,)), ((0,), (0,)))
B_TN = (((1,), (1,)), ((0,), (0,)))


def _split_bf16(x):
    hi = x.astype(BF16)
    return hi, (x - hi.astype(F32)).astype(BF16)


def _dot(a, b, dims, prec):
    if prec == "bf16":
        return lax.dot_general(a.astype(BF16), b.astype(BF16), dims, preferred_element_type=F32)
    a1, a2 = _split_bf16(a)
    b1, b2 = _split_bf16(b)
    dg = functools.partial(lax.dot_general, dimension_numbers=dims, preferred_element_type=F32)
    return dg(a1, b1) + (dg(a1, b2) + dg(a2, b1))


def _make_mm(prec, nn_dims, nt_dims, tn_dims):
    @jax.custom_vjp
    def nn(a, b):
        return _dot(a, b, nn_dims, prec)

    @jax.custom_vjp
    def nt(a, b):
        return _dot(a, b, nt_dims, prec)

    @jax.custom_vjp
    def tn(a, b):
        return _dot(a, b, tn_dims, prec)

    nn.defvjp(lambda a, b: (_dot(a, b, nn_dims, prec), (a, b)),
              lambda r, g: (_dot(g, r[1], nt_dims, prec), _dot(r[0], g, tn_dims, prec)))
    nt.defvjp(lambda a, b: (_dot(a, b, nt_dims, prec), (a, b)),
              lambda r, g: (_dot(g, r[1], nn_dims, prec), _dot(g, r[0], tn_dims, prec)))
    tn.defvjp(lambda a, b: (_dot(a, b, tn_dims, prec), (a, b)),
              lambda r, g: (_dot(r[1], g, nt_dims, prec), _dot(r[0], g, nn_dims, prec)))
    return nn, nt, tn


_NN_B, _NT_B, _TN_B = _make_mm("bf16", NN, NT, TN)
_BNN, _BNT, _BTN = _make_mm("bf16", B_NN, B_NT, B_TN)


@jax.custom_vjp
def _unit_lower_inverse(a):
    n = a.shape[-1]
    eye = (lax.broadcasted_iota(jnp.int32, a.shape, 1) == lax.broadcasted_iota(jnp.int32, a.shape, 2)).astype(F32)
    inv = eye - a
    pw = _dot(a, a, B_NN, "bf16x3")
    steps = n.bit_length() - 2
    for j in range(steps):
        inv = inv + _dot(inv, pw, B_NN, "bf16x3")
        if j + 1 < steps:
            pw = _dot(pw, pw, B_NN, "bf16x3")
    return inv


def _uli_fwd(a):
    inv = _unit_lower_inverse(a)
    return inv, inv


def _uli_bwd(inv, g):
    return (-_dot(_dot(inv, g, B_TN, "bf16"), inv, B_NT, "bf16"),)


_unit_lower_inverse.defvjp(_uli_fwd, _uli_bwd)


def _lower_ones(batch, n):
    shape = (batch, n, n)
    return (lax.broadcasted_iota(jnp.int32, shape, 1) >= lax.broadcasted_iota(jnp.int32, shape, 2)).astype(BF16)


@jax.custom_vjp
def _chunk_cumsum(g):
    tri = _lower_ones(g.shape[0], g.shape[1])
    g1, g2 = _split_bf16(g)
    g3 = (g - g1.astype(F32) - g2.astype(F32)).astype(BF16)
    dg = functools.partial(lax.dot_general, dimension_numbers=B_NN, preferred_element_type=F32)
    return dg(tri, g1) + (dg(tri, g2) + dg(tri, g3))


def _chunk_cumsum_bwd(_, ct):
    tri = _lower_ones(ct.shape[0], ct.shape[1])
    c1, c2 = _split_bf16(ct)
    dg = functools.partial(lax.dot_general, dimension_numbers=B_TN, preferred_element_type=F32)
    return (dg(tri, c1) + dg(tri, c2),)


_chunk_cumsum.defvjp(lambda g: (_chunk_cumsum(g), None), _chunk_cumsum_bwd)


def _expm1(x):
    small = x * (1.0 + x * (0.5 + x * (1.0 / 6 + x * (1.0 / 24 + x * (1.0 / 120 + x * (1.0 / 720))))))
    return jnp.where(jnp.abs(x) < 0.2, small, jnp.exp(x) - 1.0)


def _sigmoid(x):
    return 1.0 / (1.0 + jnp.exp(-x))


def _silu(x):
    return x * _sigmoid(x)


def _softplus(x):
    return jnp.maximum(x, 0.0) + jnp.log(1.0 + jnp.exp(-jnp.abs(x)))


def _rmsnorm(x, w):
    return x * lax.rsqrt(jnp.mean(x * x, axis=-1, keepdims=True) + EPS) * w


def _gated_norm(o, z, w):
    return o * lax.rsqrt(jnp.mean(o * o, axis=-1, keepdims=True) + EPS) * w * _silu(z)


def _lru_gates(xc, wa, ba, wx, bx, lam):
    r = _sigmoid(_NN_B(xc, wa) + ba)
    i = _sigmoid(_NN_B(xc, wx) + bx)
    log_a = -LRU_C * r * _softplus(-lam)
    a = jnp.exp(log_a)
    mult = jnp.sqrt(-_expm1(2.0 * log_a))
    return a, mult * (i * xc)


def _lane_pick(row, lane_index):
    lane = lax.broadcasted_iota(jnp.int32, row.shape, 1)
    return jnp.sum(jnp.where(lane == lane_index, row, 0.0), axis=-1, keepdims=True)


def _dn_prep(qc, kc, vc, ba, a_log_row, dt_row, head):
    q = _silu(qc)
    k = _silu(kc)
    v = _silu(vc)
    q = q * lax.rsqrt(jnp.sum(q * q, axis=-1, keepdims=True) + EPS) * (HEAD_DIM ** -0.5)
    k = k * lax.rsqrt(jnp.sum(k * k, axis=-1, keepdims=True) + EPS)
    beta = _sigmoid(_lane_pick(ba, head))
    g = -jnp.exp(_lane_pick(a_log_row, head)) * _softplus(_lane_pick(ba, HEADS + head) + _lane_pick(dt_row, head))
    return q, k, v, g, beta


def _dn_chunks_head(q, k, v, gcol, bcol):
    n, c, d = q.shape
    row = lax.broadcasted_iota(jnp.int32, (n, c, c), 1)
    col = lax.broadcasted_iota(jnp.int32, (n, c, c), 2)
    g_wide = jnp.broadcast_to(gcol, (n, c, d))
    b_wide = jnp.broadcast_to(bcol, (n, c, d))
    gc = _chunk_cumsum(g_wide)
    gc_rows = gc[:, :, :c]
    decay = jnp.exp(jnp.where(row >= col, gc_rows - jnp.swapaxes(gc_rows, 1, 2), -1e30))
    kb = k * b_wide
    eg = jnp.exp(gc)
    a = jnp.where(row > col, _BNT(kb, k) * decay, 0.0)
    tinv = _unit_lower_inverse(a)
    u = _BNN(tinv, v * b_wide)
    w = _BNN(tinv, kb * eg)
    attn = _BNT(q, k) * decay
    g_last = jnp.sum(g_wide, axis=1, keepdims=True)
    return u, w, attn, q * eg, k * jnp.exp(g_last - gc), jnp.exp(g_last)


def _dn_chunks(q, k, v, gcol, bcol, states):
    u, w, attn, qe, kdec, eglast = _dn_chunks_head(q, k, v, gcol, bcol)
    v_new = u - _BNN(w, states)
    o = _BNN(qe, states) + _BNN(attn, v_new)
    return (o, states * eglast + _BTN(kdec, v_new)), (w, attn, qe, kdec, eglast)


def _conv_taps(buf, head, cw, rows):
    acc = cw[0:1, :] * buf[head, pl.ds(5, rows), :]
    for j in range(1, 4):
        acc = acc + cw[j:j + 1, :] * buf[head, pl.ds(5 + j, rows), :]
    return acc


def _conv_backward(dbuf, dhead, xbuf, xhead, cw, dxc, rows):
    dbuf[dhead, pl.ds(0, rows), :] = dxc
    dx = cw[0:1, :] * dbuf[dhead, pl.ds(3, rows), :]
    for j in range(1, 4):
        dx = dx + cw[j:j + 1, :] * dbuf[dhead, pl.ds(3 - j, rows), :]
    dcw = jnp.concatenate(
        [jnp.sum(dxc * xbuf[xhead, pl.ds(5 + j, rows), :], axis=0, keepdims=True) for j in range(4)], axis=0)
    dbuf[dhead, pl.ds(rows, 8), :] = dbuf[dhead, pl.ds(0, 8), :]
    return dx, dcw


def _params(**kw):
    return pltpu.CompilerParams(vmem_limit_bytes=VMEM_LIMIT, **kw)


def _matmul(a, b, form, tm, tn, tk, name, add=None, out_dtype=F32):
    if form == "nn":
        (m, kdim), (_, n) = a.shape, b.shape
        a_spec = pl.BlockSpec((tm, tk), lambda j, i, k: (i, k))
        b_spec = pl.BlockSpec((tk, tn), lambda j, i, k: (k, j))
        dims = NN
    elif form == "nt":
        (m, kdim), (n, _) = a.shape, b.shape
        a_spec = pl.BlockSpec((tm, tk), lambda j, i, k: (i, k))
        b_spec = pl.BlockSpec((tn, tk), lambda j, i, k: (j, k))
        dims = NT
    else:
        (kdim, m), (_, n) = a.shape, b.shape
        a_spec = pl.BlockSpec((tk, tm), lambda j, i, k: (k, i))
        b_spec = pl.BlockSpec((tk, tn), lambda j, i, k: (k, j))
        dims = TN
    assert m % tm == 0 and n % tn == 0 and kdim % tk == 0, (name, m, n, kdim, tm, tn, tk)
    ksteps = kdim // tk
    o_spec = pl.BlockSpec((tm, tn), lambda j, i, k: (i, j))
    has_add = add is not None

    def body(*refs):
        if has_add:
            a_ref, b_ref, c_ref, o_ref, acc = refs
        else:
            a_ref, b_ref, o_ref, acc = refs
        k = pl.program_id(2)

        @pl.when(k == 0)
        def _():
            acc[...] = c_ref[...] if has_add else jnp.zeros_like(acc)

        acc[...] += lax.dot_general(a_ref[...].astype(BF16), b_ref[...].astype(BF16), dims,
                                    preferred_element_type=F32)

        @pl.when(k == ksteps - 1)
        def _():
            o_ref[...] = acc[...].astype(o_ref.dtype)

    in_specs = [a_spec, b_spec] + ([o_spec] if has_add else [])
    args = (a, b) + ((add,) if has_add else ())
    return pl.pallas_call(
        body, name=name, grid=(n // tn, m // tm, ksteps), in_specs=in_specs, out_specs=o_spec,
        out_shape=jax.ShapeDtypeStruct((m, n), out_dtype), scratch_shapes=[pltpu.VMEM((tm, tn), F32)],
        compiler_params=_params(dimension_semantics=("parallel", "parallel", "arbitrary")),
    )(*args)


def _rmsnorm_fwd(x, w_row, name):
    s = x.shape[0]
    tb = min(TIME_BLOCK, s)

    def body(x_ref, w_ref, o_ref):
        o_ref[...] = _rmsnorm(x_ref[...], w_ref[...]).astype(BF16)

    return pl.pallas_call(
        body, name=name, grid=(s // tb,),
        in_specs=[pl.BlockSpec((tb, D_MODEL), lambda i: (i, 0)), pl.BlockSpec((1, D_MODEL), lambda i: (0, 0))],
        out_specs=pl.BlockSpec((tb, D_MODEL), lambda i: (i, 0)),
        out_shape=jax.ShapeDtypeStruct((s, D_MODEL), BF16), compiler_params=_params(),
    )(x, w_row)


def _rmsnorm_bwd(x, w_row, dh, dres, name):
    s = x.shape[0]
    tb = min(TIME_BLOCK, s)

    def body(x_ref, w_ref, dh_ref, dres_ref, dx_ref, dw_ref):
        _, vjp = jax.vjp(_rmsnorm, x_ref[...], w_ref[...])
        dx, dw = vjp(dh_ref[...])
        dx_ref[...] = dres_ref[...] + dx

        @pl.when(pl.program_id(0) == 0)
        def _():
            dw_ref[...] = jnp.zeros_like(dw_ref)

        dw_ref[...] += dw

    row = pl.BlockSpec((tb, D_MODEL), lambda i: (i, 0))
    vec = pl.BlockSpec((1, D_MODEL), lambda i: (0, 0))
    return pl.pallas_call(
        body, name=name, grid=(s // tb,), in_specs=[row, vec, row, row], out_specs=[row, vec],
        out_shape=[jax.ShapeDtypeStruct((s, D_MODEL), F32), jax.ShapeDtypeStruct((1, D_MODEL), F32)],
        compiler_params=_params(),
    )(x, w_row, dh, dres)


def _final_loss(x, w_row, target, name):
    s = x.shape[0]
    tb = min(TIME_BLOCK, s)

    def loss_fn(xv, wv, tv):
        err = _rmsnorm(xv, wv) - tv
        return 0.5 * jnp.sum(jnp.sum(err * err, axis=-1, keepdims=True), axis=0, keepdims=True) * (1.0 / D_MODEL)

    def body(x_ref, w_ref, t_ref, loss_ref, dx_ref, dw_ref):
        tv = t_ref[...]
        loss, vjp = jax.vjp(lambda xv, wv: loss_fn(xv, wv, tv), x_ref[...], w_ref[...])
        dx, dw = vjp(jnp.ones((1, 1), F32))
        dx_ref[...] = dx

        @pl.when(pl.program_id(0) == 0)
        def _():
            dw_ref[...] = jnp.zeros_like(dw_ref)
            loss_ref[...] = jnp.zeros_like(loss_ref)

        dw_ref[...] += dw
        loss_ref[...] += jnp.broadcast_to(loss, loss_ref.shape)

    row = pl.BlockSpec((tb, D_MODEL), lambda i: (i, 0))
    vec = pl.BlockSpec((1, D_MODEL), lambda i: (0, 0))
    return pl.pallas_call(
        body, name=name, grid=(s // tb,), in_specs=[row, vec, row],
        out_specs=[pl.BlockSpec((1, HEAD_DIM), lambda i: (0, 0)), row, vec],
        out_shape=[jax.ShapeDtypeStruct((1, HEAD_DIM), F32), jax.ShapeDtypeStruct((s, D_MODEL), F32),
                   jax.ShapeDtypeStruct((1, D_MODEL), F32)],
        compiler_params=_params(),
    )(x, w_row, target)


def _head_specs(tb, time_of):
    def col(off):
        return pl.BlockSpec((tb, HEAD_DIM), lambda t, h: (time_of(t), off + h))
    return col


def _vec_spec():
    return pl.BlockSpec((1, HEAD_DIM), lambda t, h: (0, h))


def _lru_fwd(proj, conv_w, conv_b, wa, ba, wx, bx, lam, nw, name):
    s = proj.shape[0]
    tb = min(TIME_BLOCK, s)
    nt = s // tb
    col = _head_specs(tb, lambda t: t)

    def body(x_ref, z_ref, cw_ref, cb_ref, wa_ref, ba_ref, wx_ref, bx_ref, lam_ref, nw_ref,
             y_ref, hs_ref, xbuf, hcar, a_s, b_s):
        t, h = pl.program_id(0), pl.program_id(1)

        @pl.when(t == 0)
        def _():
            xbuf[h, pl.ds(0, 8), :] = jnp.zeros((8, HEAD_DIM), F32)
            hcar[h] = jnp.zeros((8, HEAD_DIM), F32)

        xbuf[h, pl.ds(8, tb), :] = x_ref[...]
        xc = _conv_taps(xbuf, h, cw_ref[...], tb) + cb_ref[...]
        a, b = _lru_gates(xc, wa_ref[...], ba_ref[...], wx_ref[...], bx_ref[...], lam_ref[...])
        a_s[...] = a
        b_s[...] = b

        def step(r, hprev):
            hnew = a_s[pl.ds(r, 1), :] * hprev + b_s[pl.ds(r, 1), :]
            hs_ref[pl.ds(r, 1), :] = hnew
            return hnew

        hlast = lax.fori_loop(0, tb, step, hcar[h, pl.ds(0, 1), :], unroll=8)
        hcar[h, pl.ds(0, 1), :] = hlast
        xbuf[h, pl.ds(0, 8), :] = xbuf[h, pl.ds(tb, 8), :]
        y_ref[...] = _gated_norm(hs_ref[...], z_ref[...], nw_ref[...]).astype(BF16)

    vec = _vec_spec()
    return pl.pallas_call(
        body, name=name, grid=(nt, HEADS),
        in_specs=[col(COL_LRU_X), col(COL_LRU_Z), pl.BlockSpec((4, HEAD_DIM), lambda t, h: (0, h)), vec,
                  pl.BlockSpec((None, HEAD_DIM, HEAD_DIM), lambda t, h: (h, 0, 0)), vec,
                  pl.BlockSpec((None, HEAD_DIM, HEAD_DIM), lambda t, h: (h, 0, 0)), vec, vec, vec],
        out_specs=[col(0), col(0)],
        out_shape=[jax.ShapeDtypeStruct((s, D_MODEL), BF16), jax.ShapeDtypeStruct((s, D_MODEL), F32)],
        scratch_shapes=[pltpu.VMEM((HEADS, tb + 8, HEAD_DIM), F32), pltpu.VMEM((HEADS, 8, HEAD_DIM), F32),
                        pltpu.VMEM((tb, HEAD_DIM), F32), pltpu.VMEM((tb, HEAD_DIM), F32)],
        compiler_params=_params(dimension_semantics=("arbitrary", "arbitrary")),
    )(proj, proj, conv_w, conv_b, wa, ba, wx, bx, lam, nw)


def _halo_spec(tb, nt, off):
    per = tb // 8
    return pl.BlockSpec((8, HEAD_DIM), lambda t, h: (jnp.maximum((nt - 1 - t) * per - 1, 0), off + h))


def _lru_bwd(proj, hs, dy, conv_w, conv_b, wa, ba, wx, bx, lam, nw, name):
    s = proj.shape[0]
    tb = min(TIME_BLOCK, s)
    nt = s // tb
    col = _head_specs(tb, lambda t: nt - 1 - t)

    def body(x_ref, xh_ref, z_ref, hs_ref, hh_ref, dy_ref, cw_ref, cb_ref, wa_ref, ba_ref, wx_ref, bx_ref,
             lam_ref, nw_ref, dx_ref, dz_ref, dcw_ref, dcb_ref, dwa_ref, dba_ref, dwx_ref, dbx_ref, dlam_ref,
             dnw_ref, xbuf, hbuf, dbuf, gcar, a_s, g_s):
        t, h = pl.program_id(0), pl.program_id(1)
        first_block = t == nt - 1

        @pl.when(t == 0)
        def _():
            dbuf[h, pl.ds(tb, 8), :] = jnp.zeros((8, HEAD_DIM), F32)
            gcar[h] = jnp.zeros((8, HEAD_DIM), F32)
            dcw_ref[h] = jnp.zeros((4, HEAD_DIM), F32)
            dwa_ref[h] = jnp.zeros((HEAD_DIM, HEAD_DIM), F32)
            dwx_ref[h] = jnp.zeros((HEAD_DIM, HEAD_DIM), F32)
            for ref in (dcb_ref, dba_ref, dbx_ref, dlam_ref, dnw_ref):
                ref[h] = jnp.zeros((1, HEAD_DIM), F32)

        keep = jnp.where(first_block, 0.0, 1.0)
        xbuf[0, pl.ds(0, 8), :] = xh_ref[...] * keep
        xbuf[0, pl.ds(8, tb), :] = x_ref[...]
        hbuf[pl.ds(0, 8), :] = hh_ref[...] * keep
        hbuf[pl.ds(8, tb), :] = hs_ref[...]
        cw = cw_ref[...]
        xc = _conv_taps(xbuf, 0, cw, tb) + cb_ref[...]
        (a, _), gates_vjp = jax.vjp(_lru_gates, xc, wa_ref[...], ba_ref[...], wx_ref[...], bx_ref[...], lam_ref[...])
        a_s[...] = a
        _, norm_vjp = jax.vjp(_gated_norm, hs_ref[...], z_ref[...], nw_ref[...])
        dh, dz, dnw = norm_vjp(dy_ref[...])
        dz_ref[...] = dz.astype(dz_ref.dtype)
        g_s[...] = dh

        def step(i, car):
            r = tb - 1 - i
            g = g_s[pl.ds(r, 1), :] + car
            g_s[pl.ds(r, 1), :] = g
            return a_s[pl.ds(r, 1), :] * g

        gcar[h, pl.ds(0, 1), :] = lax.fori_loop(0, tb, step, gcar[h, pl.ds(0, 1), :], unroll=8)
        g = g_s[...]
        dxc, dwa, dba, dwx, dbx, dlam = gates_vjp((g * hbuf[pl.ds(7, tb), :], g))
        dx, dcw = _conv_backward(dbuf, h, xbuf, 0, cw, dxc, tb)
        dx_ref[...] = dx.astype(dx_ref.dtype)
        dcw_ref[h] += dcw
        dcb_ref[h] += jnp.sum(dxc, axis=0, keepdims=True)
        dwa_ref[h] += dwa
        dwx_ref[h] += dwx
        dba_ref[h] += dba
        dbx_ref[h] += dbx
        dlam_ref[h] += dlam
        dnw_ref[h] += dnw

    vec = _vec_spec()
    mat = pl.BlockSpec((None, HEAD_DIM, HEAD_DIM), lambda t, h: (h, 0, 0))

    def whole(shape):
        return pl.BlockSpec(shape, lambda t, h: (0,) * len(shape))

    head_vec = jax.ShapeDtypeStruct((HEADS, 1, HEAD_DIM), F32)
    head_mat = jax.ShapeDtypeStruct((HEADS, HEAD_DIM, HEAD_DIM), F32)
    return pl.pallas_call(
        body, name=name, grid=(nt, HEADS),
        in_specs=[col(COL_LRU_X), _halo_spec(tb, nt, COL_LRU_X), col(COL_LRU_Z), col(0), _halo_spec(tb, nt, 0), col(0),
                  pl.BlockSpec((4, HEAD_DIM), lambda t, h: (0, h)), vec, mat, vec, mat, vec, vec, vec],
        out_specs=[col(0), col(0), whole((HEADS, 4, HEAD_DIM)), whole((HEADS, 1, HEAD_DIM)),
                   whole((HEADS, HEAD_DIM, HEAD_DIM)), whole((HEADS, 1, HEAD_DIM)),
                   whole((HEADS, HEAD_DIM, HEAD_DIM)), whole((HEADS, 1, HEAD_DIM)), whole((HEADS, 1, HEAD_DIM)),
                   whole((HEADS, 1, HEAD_DIM))],
        out_shape=[jax.ShapeDtypeStruct((s, D_MODEL), BF16), jax.ShapeDtypeStruct((s, D_MODEL), BF16),
                   jax.ShapeDtypeStruct((HEADS, 4, HEAD_DIM), F32), head_vec, head_mat, head_vec, head_mat, head_vec,
                   head_vec, head_vec],
        scratch_shapes=[pltpu.VMEM((1, tb + 8, HEAD_DIM), F32), pltpu.VMEM((tb + 8, HEAD_DIM), F32),
                        pltpu.VMEM((HEADS, tb + 8, HEAD_DIM), F32), pltpu.VMEM((HEADS, 8, HEAD_DIM), F32),
                        pltpu.VMEM((tb, HEAD_DIM), F32), pltpu.VMEM((tb, HEAD_DIM), F32)],
        compiler_params=_params(dimension_semantics=("arbitrary", "arbitrary")),
    )(proj, proj, proj, hs, hs, dy, conv_w, conv_b, wa, ba, wx, bx, lam, nw)


def _dn_fwd(proj, conv_w, a_log_row, dt_row, nw, name):
    s = proj.shape[0]
    tb = min(TIME_BLOCK, s)
    nt = s // tb
    nchunk = tb // CHUNK
    col = _head_specs(tb, lambda t: t)

    def body(q_ref, k_ref, v_ref, z_ref, ba_ref, cwq_ref, cwk_ref, cwv_ref, al_ref, dt_ref, nw_ref,
             y_ref, o_ref, st_ref, xbuf, state):
        t, h = pl.program_id(0), pl.program_id(1)

        @pl.when(t == 0)
        def _():
            for i in range(3):
                xbuf[3 * h + i, pl.ds(0, 8), :] = jnp.zeros((8, HEAD_DIM), F32)
            state[h] = jnp.zeros((HEAD_DIM, HEAD_DIM), F32)

        conv = []
        for i, (ref, cw_ref) in enumerate(((q_ref, cwq_ref), (k_ref, cwk_ref), (v_ref, cwv_ref))):
            xbuf[3 * h + i, pl.ds(8, tb), :] = ref[...]
            conv.append(_conv_taps(xbuf, 3 * h + i, cw_ref[...], tb))
            xbuf[3 * h + i, pl.ds(0, 8), :] = xbuf[3 * h + i, pl.ds(tb, 8), :]
        q, k, v, g, beta = _dn_prep(conv[0], conv[1], conv[2], ba_ref[...], al_ref[...], dt_ref[...], h)
        def chunks(a):
            return a.reshape(nchunk, CHUNK, a.shape[-1])

        u, w, attn, qe, kdec, eglast = _dn_chunks_head(chunks(q), chunks(k), chunks(v), chunks(g), chunks(beta))
        st = state[h]
        for c in range(nchunk):
            st_ref[c] = st
            v_new = u[c] - _NN_B(w[c], st)
            o_ref[pl.ds(c * CHUNK, CHUNK), :] = _NN_B(qe[c], st) + _NN_B(attn[c], v_new)
            st = st * eglast[c] + _TN_B(kdec[c], v_new)
        state[h] = st
        y_ref[...] = _gated_norm(o_ref[...], z_ref[...], nw_ref[...]).astype(BF16)

    def cw_spec(off):
        return pl.BlockSpec((4, HEAD_DIM), lambda t, h: (0, off + h))

    row128 = pl.BlockSpec((1, HEAD_DIM), lambda t, h: (0, 0))
    return pl.pallas_call(
        body, name=name, grid=(nt, HEADS),
        in_specs=[col(COL_Q), col(COL_K), col(COL_V), col(COL_DN_Z),
                  pl.BlockSpec((tb, HEAD_DIM), lambda t, h: (t, COL_BA)),
                  cw_spec(0), cw_spec(HEADS), cw_spec(2 * HEADS), row128, row128, row128],
        out_specs=[col(0), col(0), pl.BlockSpec((None, nchunk, HEAD_DIM, HEAD_DIM), lambda t, h: (h, t, 0, 0))],
        out_shape=[jax.ShapeDtypeStruct((s, D_MODEL), BF16), jax.ShapeDtypeStruct((s, D_MODEL), F32),
                   jax.ShapeDtypeStruct((HEADS, s // CHUNK, HEAD_DIM, HEAD_DIM), F32)],
        scratch_shapes=[pltpu.VMEM((3 * HEADS, tb + 8, HEAD_DIM), F32), pltpu.VMEM((HEADS, HEAD_DIM, HEAD_DIM), F32)],
        compiler_params=_params(dimension_semantics=("arbitrary", "arbitrary")),
    )(proj, proj, proj, proj, proj, conv_w, conv_w, conv_w, a_log_row, dt_row, nw)


def _dn_bwd(proj, o, states, dy, conv_w, a_log_row, dt_row, nw, name):
    s = proj.shape[0]
    tb = min(TIME_BLOCK, s)
    nt = s // tb
    nchunk = tb // CHUNK
    col = _head_specs(tb, lambda t: nt - 1 - t)

    def body(q_ref, qh_ref, k_ref, kh_ref, v_ref, vh_ref, z_ref, ba_ref, o_ref, st_ref, dy_ref,
             cwq_ref, cwk_ref, cwv_ref, al_ref, dt_ref, nw_ref,
             dq_ref, dk_ref, dv_ref, dz_ref, dba_ref, dcw_ref, dal_ref, ddt_ref, dnw_ref,
             xbuf, dbuf, dstate, dst_s):
        t, h = pl.program_id(0), pl.program_id(1)
        first_block = t == nt - 1

        @pl.when(t == 0)
        def _():
            for i in range(3):
                dbuf[3 * h + i, pl.ds(tb, 8), :] = jnp.zeros((8, HEAD_DIM), F32)
                dcw_ref[3 * h + i] = jnp.zeros((4, HEAD_DIM), F32)
            dstate[h] = jnp.zeros((HEAD_DIM, HEAD_DIM), F32)

        @pl.when((t == 0) & (h == 0))
        def _():
            for ref in (dal_ref, ddt_ref, dnw_ref):
                ref[...] = jnp.zeros_like(ref)

        keep = jnp.where(first_block, 0.0, 1.0)
        cws = (cwq_ref[...], cwk_ref[...], cwv_ref[...])
        conv = []
        for i, (ref, halo) in enumerate(((q_ref, qh_ref), (k_ref, kh_ref), (v_ref, vh_ref))):
            xbuf[i, pl.ds(0, 8), :] = halo[...] * keep
            xbuf[i, pl.ds(8, tb), :] = ref[...]
            conv.append(_conv_taps(xbuf, i, cws[i], tb))
        (q, k, v, g, beta), prep_vjp = jax.vjp(
            lambda qc, kc, vc, ba, al, dt: _dn_prep(qc, kc, vc, ba, al, dt, h),
            conv[0], conv[1], conv[2], ba_ref[...], al_ref[...], dt_ref[...])
        _, norm_vjp = jax.vjp(_gated_norm, o_ref[...], z_ref[...], nw_ref[...])
        do, dz, dnw = norm_vjp(dy_ref[...])
        dz_ref[...] = dz.astype(dz_ref.dtype)
        dnw_ref[...] += dnw

        def chunks(a):
            return a.reshape(nchunk, CHUNK, a.shape[-1])

        do = chunks(do)
        _, chunks_vjp, (w, attn, qe, kdec, eglast) = jax.vjp(
            _dn_chunks, chunks(q), chunks(k), chunks(v), chunks(g), chunks(beta), st_ref[...], has_aux=True)
        from_o = _dot(attn, do, B_TN, "bf16")
        from_qe = _dot(qe, do, B_TN, "bf16")
        dst = dstate[h]
        for c in reversed(range(nchunk)):
            dst_s[c] = dst
            dv_new = from_o[c] + _dot(kdec[c], dst, NN, "bf16")
            dst = dst * eglast[c] + from_qe[c] - _dot(w[c], dv_new, TN, "bf16")
        dstate[h] = dst
        dq, dk, dv, dg, db, _ = chunks_vjp((do, dst_s[...]))

        def rows(a):
            return a.reshape(tb, a.shape[-1])

        dqc, dkc, dvc, dba, dal, ddt = prep_vjp((rows(dq), rows(dk), rows(dv), rows(dg), rows(db)))
        for i, (dxc, out) in enumerate(((dqc, dq_ref), (dkc, dk_ref), (dvc, dv_ref))):
            dx, dcw = _conv_backward(dbuf, 3 * h + i, xbuf, i, cws[i], dxc, tb)
            out[...] = dx.astype(out.dtype)
            dcw_ref[3 * h + i] += dcw
        dal_ref[...] += dal
        ddt_ref[...] += ddt

        @pl.when(h == 0)
        def _():
            dba_ref[...] = dba.astype(dba_ref.dtype)

        @pl.when(h > 0)
        def _():
            dba_ref[...] += dba.astype(dba_ref.dtype)

    def cw_spec(off):
        return pl.BlockSpec((4, HEAD_DIM), lambda t, h: (0, off + h))

    def whole(shape):
        return pl.BlockSpec(shape, lambda t, h: (0,) * len(shape))

    row128 = whole((1, HEAD_DIM))
    blk = (tb, HEAD_DIM)
    act = jax.ShapeDtypeStruct((s, D_MODEL), BF16)
    row_out = jax.ShapeDtypeStruct((1, HEAD_DIM), F32)
    return pl.pallas_call(
        body, name=name, grid=(nt, HEADS),
        in_specs=[col(COL_Q), _halo_spec(tb, nt, COL_Q), col(COL_K), _halo_spec(tb, nt, COL_K),
                  col(COL_V), _halo_spec(tb, nt, COL_V), col(COL_DN_Z),
                  pl.BlockSpec(blk, lambda t, h: (nt - 1 - t, COL_BA)), col(0),
                  pl.BlockSpec((None, nchunk, HEAD_DIM, HEAD_DIM), lambda t, h: (h, nt - 1 - t, 0, 0)), col(HEADS),
                  cw_spec(0), cw_spec(HEADS), cw_spec(2 * HEADS), row128, row128, row128],
        out_specs=[col(0), col(0), col(0), col(0), pl.BlockSpec(blk, lambda t, h: (nt - 1 - t, 0)),
                   whole((3 * HEADS, 4, HEAD_DIM)), row128, row128, row128],
        out_shape=[act, act, act, act, jax.ShapeDtypeStruct((s, HEAD_DIM), F32),
                   jax.ShapeDtypeStruct((3 * HEADS, 4, HEAD_DIM), F32), row_out, row_out, row_out],
        scratch_shapes=[pltpu.VMEM((3, tb + 8, HEAD_DIM), F32), pltpu.VMEM((3 * HEADS, tb + 8, HEAD_DIM), F32),
                        pltpu.VMEM((HEADS, HEAD_DIM, HEAD_DIM), F32), pltpu.VMEM((nchunk, HEAD_DIM, HEAD_DIM), F32)],
        compiler_params=_params(dimension_semantics=("arbitrary", "arbitrary")),
    )(proj, proj, proj, proj, proj, proj, proj, proj, o, states, dy, conv_w, conv_w, conv_w, a_log_row, dt_row, nw)


def _mesh_position():
    x, y, c = lax.axis_index("x"), lax.axis_index("y"), lax.axis_index("c")
    return x, y, c, 4 * x + 2 * y + c


def _peer(k, x, y, c):
    px = 1 - x if k & 4 else x
    py = 1 - y if k & 2 else y
    pc = 1 - c if k & 1 else c
    return (px, py, pc), 4 * px + 2 * py + pc


def _direct_exchange(arrays, scatter, name):
    n = len(arrays)
    out_shapes = [jax.ShapeDtypeStruct(a.shape if scatter else (N_DEV,) + a.shape, a.dtype) for a in arrays]

    def body(*refs):
        ins, outs = refs[:n], refs[n:2 * n]
        send_sems, recv_sems, local_sems = refs[2 * n:]
        x, y, c, me = _mesh_position()
        local, sends, recvs = [], [], []
        for i in range(n):
            cp = pltpu.make_async_copy(ins[i].at[me] if scatter else ins[i], outs[i].at[me], local_sems.at[i])
            cp.start()
            local.append(cp)
            for k in range(1, N_DEV):
                peer, peer_id = _peer(k, x, y, c)
                sem = i * (N_DEV - 1) + k - 1
                cp = pltpu.make_async_remote_copy(
                    src_ref=ins[i].at[peer_id] if scatter else ins[i], dst_ref=outs[i].at[me],
                    send_sem=send_sems.at[sem], recv_sem=recv_sems.at[sem],
                    device_id=peer, device_id_type=pl.DeviceIdType.MESH)
                cp.start()
                sends.append(cp)
                recvs.append(pltpu.make_async_remote_copy(
                    src_ref=ins[i].at[peer_id] if scatter else ins[i], dst_ref=outs[i].at[peer_id],
                    send_sem=send_sems.at[sem], recv_sem=recv_sems.at[sem],
                    device_id=peer, device_id_type=pl.DeviceIdType.MESH))
        for cp in recvs:
            cp.wait_recv()
        for cp in sends:
            cp.wait_send()
        for cp in local:
            cp.wait()

    hbm = pl.BlockSpec(memory_space=pl.ANY)
    return pl.pallas_call(
        body, name=name, in_specs=[hbm] * n, out_specs=[hbm] * n, out_shape=out_shapes,
        scratch_shapes=[pltpu.SemaphoreType.DMA((n * (N_DEV - 1),)), pltpu.SemaphoreType.DMA((n * (N_DEV - 1),)),
                        pltpu.SemaphoreType.DMA((n,))],
    )(*arrays)


def _adamw(parts, w, m, v, name, rows_per_step):
    rows, cols = w.shape
    tr = min(rows_per_step, rows)
    assert rows % tr == 0, (name, rows, tr)
    c1 = 1.0 / (1.0 - ADAM_B1 ** ADAM_STEP)
    c2 = 1.0 / (1.0 - ADAM_B2 ** ADAM_STEP)

    def body(p_ref, w_ref, m_ref, v_ref, g_ref, d_ref, nm_ref, nv_ref):
        g = p_ref[0]
        for d in range(1, N_DEV):
            g = g + p_ref[d]
        nm = ADAM_B1 * m_ref[...] + (1.0 - ADAM_B1) * g
        nv = ADAM_B2 * v_ref[...] + (1.0 - ADAM_B2) * (g * g)
        g_ref[...] = g
        nm_ref[...] = nm
        nv_ref[...] = nv
        d_ref[...] = -ADAM_LR * ((nm * c1) / (jnp.sqrt(nv * c2) + ADAM_EPS) + ADAM_WD * w_ref[...])

    blk = pl.BlockSpec((tr, cols), lambda i: (i, 0))
    shape = jax.ShapeDtypeStruct((rows, cols), F32)
    return pl.pallas_call(
        body, name=name, grid=(rows // tr,),
        in_specs=[pl.BlockSpec((N_DEV, tr, cols), lambda i: (0, i, 0)), blk, blk, blk],
        out_specs=[blk] * 4, out_shape=[shape] * 4, compiler_params=_params(),
    )(parts, w, m, v)


_REPLICATED = ("norm_w", "lru_conv_b", "lru_wa", "lru_ba", "lru_wx", "lru_bx", "lru_lambda", "lru_norm_w",
               "dn_A_log", "dn_dt_bias", "dn_norm_w", "final_norm_w")
_SHARDED = ("w_in", "lru_conv_w", "dn_conv_w", "w_out")
_WEIGHTS = ("norm_w", "w_in", "lru_conv_w", "lru_conv_b", "lru_wa", "lru_ba", "lru_wx", "lru_bx", "lru_lambda",
            "lru_norm_w", "dn_conv_w", "dn_A_log", "dn_dt_bias", "dn_norm_w", "w_out", "final_norm_w")


def _pack_rows(tree):
    rows = []
    for name in _REPLICATED:
        a = tree[name]
        if a.shape[-1] == HEADS:
            a = jnp.pad(a, ((0, 0), (0, HEAD_DIM - HEADS)))
        rows.append(a.reshape(-1, HEAD_DIM))
    packed = jnp.concatenate(rows, axis=0)
    return jnp.pad(packed, ((0, (-packed.shape[0]) % PACK_ROWS), (0, 0)))


def _unpack_rows(packed, like):
    out, at = {}, 0
    for name in _REPLICATED:
        shape = like[name].shape
        if shape[-1] == HEADS:
            n = shape[0]
            out[name] = packed[at:at + n, :HEADS]
        else:
            n = like[name].size // HEAD_DIM
            out[name] = packed[at:at + n].reshape(shape)
        at += n
    return out


def _heads_to_channels(a):
    return jnp.transpose(a, (1, 0, 2)).reshape(a.shape[1], HEADS * HEAD_DIM)


def kernel(x, norm_w, w_in, lru_conv_w, lru_conv_b, lru_wa, lru_ba, lru_wx, lru_bx, lru_lambda, lru_norm_w, dn_conv_w, dn_A_log, dn_dt_bias, dn_norm_w, w_out, final_norm_w, loss_target, m_norm_w, m_w_in, m_lru_conv_w, m_lru_conv_b, m_lru_wa, m_lru_ba, m_lru_wx, m_lru_bx, m_lru_lambda, m_lru_norm_w, m_dn_conv_w, m_dn_A_log, m_dn_dt_bias, m_dn_norm_w, m_w_out, m_final_norm_w, v_norm_w, v_w_in, v_lru_conv_w, v_lru_conv_b, v_lru_wa, v_lru_ba, v_lru_wx, v_lru_bx, v_lru_lambda, v_lru_norm_w, v_dn_conv_w, v_dn_A_log, v_dn_dt_bias, v_dn_norm_w, v_w_out, v_final_norm_w):
    weights = dict(norm_w=norm_w, w_in=w_in, lru_conv_w=lru_conv_w, lru_conv_b=lru_conv_b, lru_wa=lru_wa,
                   lru_ba=lru_ba, lru_wx=lru_wx, lru_bx=lru_bx, lru_lambda=lru_lambda, lru_norm_w=lru_norm_w,
                   dn_conv_w=dn_conv_w, dn_A_log=dn_A_log, dn_dt_bias=dn_dt_bias, dn_norm_w=dn_norm_w,
                   w_out=w_out, final_norm_w=final_norm_w)
    mom_m = dict(norm_w=m_norm_w, w_in=m_w_in, lru_conv_w=m_lru_conv_w, lru_conv_b=m_lru_conv_b, lru_wa=m_lru_wa,
                 lru_ba=m_lru_ba, lru_wx=m_lru_wx, lru_bx=m_lru_bx, lru_lambda=m_lru_lambda,
                 lru_norm_w=m_lru_norm_w, dn_conv_w=m_dn_conv_w, dn_A_log=m_dn_A_log, dn_dt_bias=m_dn_dt_bias,
                 dn_norm_w=m_dn_norm_w, w_out=m_w_out, final_norm_w=m_final_norm_w)
    mom_v = dict(norm_w=v_norm_w, w_in=v_w_in, lru_conv_w=v_lru_conv_w, lru_conv_b=v_lru_conv_b, lru_wa=v_lru_wa,
                 lru_ba=v_lru_ba, lru_wx=v_lru_wx, lru_bx=v_lru_bx, lru_lambda=v_lru_lambda,
                 lru_norm_w=v_lru_norm_w, dn_conv_w=v_dn_conv_w, dn_A_log=v_dn_A_log, dn_dt_bias=v_dn_dt_bias,
                 dn_norm_w=v_dn_norm_w, w_out=v_w_out, final_norm_w=v_final_norm_w)
    depth = norm_w.shape[0]
    xs = x[0]
    s = xs.shape[0]
    tm = min(512, s)

    g_win, g_lcw, g_dcw, g_wout = _direct_exchange([w_in, lru_conv_w, dn_conv_w, w_out], False, "gather_weights")
    win = jnp.transpose(g_win, (1, 2, 0, 3)).reshape(depth, D_MODEL, D_IN)
    win = jnp.pad(win, ((0, 0), (0, 0), (0, D_IN_PAD - D_IN))).astype(BF16)
    wout = jnp.transpose(g_wout, (1, 0, 2, 3)).reshape(depth, 2 * D_MODEL, D_MODEL).astype(BF16)
    lcw = jnp.transpose(g_lcw, (1, 2, 0, 3)).reshape(depth, 4, D_MODEL)
    dcw = jnp.transpose(g_dcw, (1, 2, 0, 3)).reshape(depth, 4, 3 * D_MODEL)

    def row(a):
        return a.reshape(1, -1)

    def pad_row(a):
        return jnp.pad(a, (0, HEAD_DIM - a.shape[0])).reshape(1, HEAD_DIM)

    saved = []
    cur = xs
    for l in range(depth):
        hn = _rmsnorm_fwd(cur, row(norm_w[l]), f"norm_fwd_{l}")
        proj = _matmul(hn, win[l], "nn", tm, 896, D_MODEL, f"in_proj_{l}")
        y_lru, hs = _lru_fwd(proj, lcw[l], row(lru_conv_b[l]), lru_wa[l], row(lru_ba[l]), lru_wx[l], row(lru_bx[l]),
                             row(lru_lambda[l]), row(lru_norm_w[l]), f"lru_fwd_{l}")
        y_dn, o_dn, states = _dn_fwd(proj, dcw[l], pad_row(dn_A_log[l]), pad_row(dn_dt_bias[l]), row(dn_norm_w[l]),
                                     f"dn_fwd_{l}")
        ycat = jnp.concatenate([y_lru, y_dn], axis=1)
        nxt = _matmul(ycat, wout[l], "nn", tm, D_MODEL, 2 * D_MODEL, f"out_proj_{l}", add=cur)
        saved.append((cur, hn, proj, hs, o_dn, states, ycat))
        cur = nxt
    loss_part, dx, d_final = _final_loss(cur, row(final_norm_w), loss_target[0], "final_loss")

    grads = {k: [None] * depth for k in _WEIGHTS if k != "final_norm_w"}
    for l in reversed(range(depth)):
        x_in, hn, proj, hs, o_dn, states, ycat = saved[l]
        dy = _matmul(dx, wout[l], "nt", tm, D_MODEL, D_MODEL, f"out_proj_dy_{l}")
        grads["w_out"][l] = _matmul(ycat, dx, "tn", D_MODEL, D_MODEL, tm, f"out_proj_dw_{l}")
        (dlx, dlz, g_lcw, g_lcb, g_wa, g_ba, g_wx, g_bx, g_lam, g_lnw) = _lru_bwd(
            proj, hs, dy, lcw[l], row(lru_conv_b[l]), lru_wa[l], row(lru_ba[l]), lru_wx[l], row(lru_bx[l]),
            row(lru_lambda[l]), row(lru_norm_w[l]), f"lru_bwd_{l}")
        (dq, dk, dv, ddz, dba, g_dcw3, g_al, g_dt, g_dnw) = _dn_bwd(
            proj, o_dn, states, dy, dcw[l], pad_row(dn_A_log[l]), pad_row(dn_dt_bias[l]), row(dn_norm_w[l]),
            f"dn_bwd_{l}")
        dproj = jnp.concatenate([dlx, dlz, dq, dk, dv, ddz, dba.astype(BF16)], axis=1)
        dh = _matmul(dproj, win[l], "nt", tm, D_MODEL, 896, f"in_proj_dh_{l}")
        grads["w_in"][l] = _matmul(hn, dproj, "tn", D_MODEL, 896, tm, f"in_proj_dw_{l}")[:, :D_IN]
        dx, g_nw = _rmsnorm_bwd(x_in, row(norm_w[l]), dh, dx, f"norm_bwd_{l}")
        grads["norm_w"][l] = g_nw.reshape(D_MODEL)
        grads["lru_conv_w"][l] = _heads_to_channels(g_lcw)
        grads["lru_conv_b"][l] = g_lcb.reshape(D_MODEL)
        grads["lru_wa"][l] = g_wa
        grads["lru_ba"][l] = g_ba.reshape(D_MODEL)
        grads["lru_wx"][l] = g_wx
        grads["lru_bx"][l] = g_bx.reshape(D_MODEL)
        grads["lru_lambda"][l] = g_lam.reshape(D_MODEL)
        grads["lru_norm_w"][l] = g_lnw.reshape(D_MODEL)
        g_dcw3 = g_dcw3.reshape(HEADS, 3, 4, HEAD_DIM)
        grads["dn_conv_w"][l] = jnp.concatenate([_heads_to_channels(g_dcw3[:, i]) for i in range(3)], axis=1)
        grads["dn_A_log"][l] = g_al[0, :HEADS]
        grads["dn_dt_bias"][l] = g_dt[0, :HEADS]
        grads["dn_norm_w"][l] = g_dnw.reshape(HEAD_DIM)
    part = {k: jnp.stack(v) for k, v in grads.items()}
    part["final_norm_w"] = d_final.reshape(D_MODEL)

    def to_slots(a):
        dd, r, cc = a.shape
        return jnp.transpose(a.reshape(dd, r, N_DEV, cc // N_DEV), (2, 0, 1, 3))

    slots = [to_slots(part["w_in"]), to_slots(part["lru_conv_w"]), to_slots(part["dn_conv_w"]),
             jnp.transpose(part["w_out"].reshape(depth, N_DEV, 2 * D_MODEL // N_DEV, D_MODEL), (1, 0, 2, 3))]
    rep_parts = _direct_exchange([_pack_rows(part)], False, "gather_small_grads")[0]
    r_win, r_lcw, r_dcw, r_wout = _direct_exchange(slots, True, "exchange_grads")

    new = {}
    for name, parts, rows_per_step in (("w_in", r_win, 256), ("lru_conv_w", r_lcw, 8), ("dn_conv_w", r_dcw, 8),
                                       ("w_out", r_wout, 256)):
        w = weights[name]
        flat = (-1, w.shape[-1])
        outs = _adamw(parts.reshape((N_DEV,) + (w.size // w.shape[-1], w.shape[-1])), w.reshape(flat),
                      mom_m[name].reshape(flat), mom_v[name].reshape(flat), f"adamw_{name}", rows_per_step)
        new[name] = [a.reshape(w.shape) for a in outs]
    packed = _adamw(rep_parts, _pack_rows(weights), _pack_rows(mom_m), _pack_rows(mom_v), "adamw_small", PACK_ROWS)
    unpacked = [_unpack_rows(a, weights) for a in packed]
    for name in _REPLICATED:
        new[name] = [u[name] for u in unpacked]

    loss = lax.psum(loss_part[0, 0], ("x", "y", "c"))
    out = [loss, dx.reshape(x.shape)]
    for i in range(4):
        out += [new[name][i] for name in _WEIGHTS]
    return tuple(out)
```

```python
import functools

import jax
import jax.numpy as jnp
from jax import lax
from jax.experimental import pallas as pl
from jax.experimental.pallas import tpu as pltpu

F32 = jnp.float32
BF16 = jnp.bfloat16

N_DEV = 8
D_MODEL = 1024
HEADS = 8
HEAD_DIM = 128
CHUNK = 64
D_IN = 6160
D_IN_PAD = 6272
COL_LRU_X, COL_LRU_Z, COL_Q, COL_K, COL_V, COL_DN_Z, COL_BA = 0, 8, 16, 24, 32, 40, 48
LRU_C = 8.0
EPS = 1e-6
ADAM_LR, ADAM_B1, ADAM_B2, ADAM_EPS, ADAM_WD, ADAM_STEP = 0.001, 0.9, 0.999, 1e-08, 0.01, 10
TIME_BLOCK = 512
PACK_ROWS = 512
VMEM_LIMIT = 56 * 1024 * 1024

NN = (((1,), (0,)), ((), ()))
NT = (((1,), (1,)), ((), ()))
TN = (((0,), (0,)), ((), ()))


B_NN = (((2,), (1,)), ((0,), (0,)))
B_NT = (((2,), (2,)), ((0,), (0,)))
B_TN = (((1,), (1,)), ((0,), (0,)))


def _split_bf16(x):
    hi = x.astype(BF16)
    return hi, (x - hi.astype(F32)).astype(BF16)


def _dot(a, b, dims, prec):
    if prec == "bf16":
        return lax.dot_general(a.astype(BF16), b.astype(BF16), dims, preferred_element_type=F32)
    a1, a2 = _split_bf16(a)
    b1, b2 = _split_bf16(b)
    dg = functools.partial(lax.dot_general, dimension_numbers=dims, preferred_element_type=F32)
    return dg(a1, b1) + (dg(a1, b2) + dg(a2, b1))


def _make_mm(prec, nn_dims, nt_dims, tn_dims):
    @jax.custom_vjp
    def nn(a, b):
        return _dot(a, b, nn_dims, prec)

    @jax.custom_vjp
    def nt(a, b):
        return _dot(a, b, nt_dims, prec)

    @jax.custom_vjp
    def tn(a, b):
        return _dot(a, b, tn_dims, prec)

    nn.defvjp(lambda a, b: (_dot(a, b, nn_dims, prec), (a, b)),
              lambda r, g: (_dot(g, r[1], nt_dims, prec), _dot(r[0], g, tn_dims, prec)))
    nt.defvjp(lambda a, b: (_dot(a, b, nt_dims, prec), (a, b)),
              lambda r, g: (_dot(g, r[1], nn_dims, prec), _dot(g, r[0], tn_dims, prec)))
    tn.defvjp(lambda a, b: (_dot(a, b, tn_dims, prec), (a, b)),
              lambda r, g: (_dot(r[1], g, nt_dims, prec), _dot(r[0], g, nn_dims, prec)))
    return nn, nt, tn


_NN_B, _NT_B, _TN_B = _make_mm("bf16", NN, NT, TN)
_BNN, _BNT, _BTN = _make_mm("bf16", B_NN, B_NT, B_TN)


@jax.custom_vjp
def _unit_lower_inverse(a):
    n = a.shape[-1]
    eye = (lax.broadcasted_iota(jnp.int32, a.shape, 1) == lax.broadcasted_iota(jnp.int32, a.shape, 2)).astype(F32)
    inv = eye - a
    pw = _dot(a, a, B_NN, "bf16x3")
    steps = n.bit_length() - 2
    for j in range(steps):
        inv = inv + _dot(inv, pw, B_NN, "bf16x3")
        if j + 1 < steps:
            pw = _dot(pw, pw, B_NN, "bf16x3")
    return inv


def _uli_fwd(a):
    inv = _unit_lower_inverse(a)
    return inv, inv


def _uli_bwd(inv, g):
    return (-_dot(_dot(inv, g, B_TN, "bf16"), inv, B_NT, "bf16"),)


_unit_lower_inverse.defvjp(_uli_fwd, _uli_bwd)


def _lower_ones(batch, n):
    shape = (batch, n, n)
    return (lax.broadcasted_iota(jnp.int32, shape, 1) >= lax.broadcasted_iota(jnp.int32, shape, 2)).astype(BF16)


@jax.custom_vjp
def _chunk_cumsum(g):
    tri = _lower_ones(g.shape[0], g.shape[1])
    g1, g2 = _split_bf16(g)
    g3 = (g - g1.astype(F32) - g2.astype(F32)).astype(BF16)
    dg = functools.partial(lax.dot_general, dimension_numbers=B_NN, preferred_element_type=F32)
    return dg(tri, g1) + (dg(tri, g2) + dg(tri, g3))


def _chunk_cumsum_bwd(_, ct):
    tri = _lower_ones(ct.shape[0], ct.shape[1])
    c1, c2 = _split_bf16(ct)
    dg = functools.partial(lax.dot_general, dimension_numbers=B_TN, preferred_element_type=F32)
    return (dg(tri, c1) + dg(tri, c2),)


_chunk_cumsum.defvjp(lambda g: (_chunk_cumsum(g), None), _chunk_cumsum_bwd)


def _expm1(x):
    small = x * (1.0 + x * (0.5 + x * (1.0 / 6 + x * (1.0 / 24 + x * (1.0 / 120 + x * (1.0 / 720))))))
    return jnp.where(jnp.abs(x) < 0.2, small, jnp.exp(x) - 1.0)


def _sigmoid(x):
    return 1.0 / (1.0 + jnp.exp(-x))


def _silu(x):
    return x * _sigmoid(x)


def _softplus(x):
    return jnp.maximum(x, 0.0) + jnp.log(1.0 + jnp.exp(-jnp.abs(x)))


def _rmsnorm(x, w):
    return x * lax.rsqrt(jnp.mean(x * x, axis=-1, keepdims=True) + EPS) * w


def _gated_norm(o, z, w):
    return o * lax.rsqrt(jnp.mean(o * o, axis=-1, keepdims=True) + EPS) * w * _silu(z)


def _lru_gates(xc, wa, ba, wx, bx, lam):
    r = _sigmoid(_NN_B(xc, wa) + ba)
    i = _sigmoid(_NN_B(xc, wx) + bx)
    log_a = -LRU_C * r * _softplus(-lam)
    a = jnp.exp(log_a)
    mult = jnp.sqrt(-_expm1(2.0 * log_a))
    return a, mult * (i * xc)


def _scan_forward(a, b, h0):
    rows = a.shape[0]
    row = lax.broadcasted_iota(jnp.int32, a.shape, 0)
    k = 1
    while k < rows:
        seen = row >= k
        b = jnp.where(seen, a * pltpu.roll(b, k, 0) + b, b)
        a = jnp.where(seen, a * pltpu.roll(a, k, 0), a)
        k *= 2
    return b + a * h0


def _scan_reverse(a, d, carry):
    rows = a.shape[0]
    row = lax.broadcasted_iota(jnp.int32, a.shape, 0)
    last = row == rows - 1
    c = jnp.where(last, 0.0, pltpu.roll(a, rows - 1, 0))
    d = d + jnp.where(last, carry, 0.0)
    k = 1
    while k < rows:
        seen = row < rows - k
        d = jnp.where(seen, d + c * pltpu.roll(d, rows - k, 0), d)
        c = jnp.where(seen, c * pltpu.roll(c, rows - k, 0), c)
        k *= 2
    return d


def _lane_pick(row, lane_index):
    lane = lax.broadcasted_iota(jnp.int32, row.shape, 1)
    return jnp.sum(jnp.where(lane == lane_index, row, 0.0), axis=-1, keepdims=True)


def _dn_prep(qc, kc, vc, ba, a_log_row, dt_row, head):
    q = _silu(qc)
    k = _silu(kc)
    v = _silu(vc)
    q = q * lax.rsqrt(jnp.sum(q * q, axis=-1, keepdims=True) + EPS) * (HEAD_DIM ** -0.5)
    k = k * lax.rsqrt(jnp.sum(k * k, axis=-1, keepdims=True) + EPS)
    beta = _sigmoid(_lane_pick(ba, head))
    g = -jnp.exp(_lane_pick(a_log_row, head)) * _softplus(_lane_pick(ba, HEADS + head) + _lane_pick(dt_row, head))
    return q, k, v, g, beta


def _dn_chunks_head(q, k, v, gcol, bcol):
    n, c, d = q.shape
    row = lax.broadcasted_iota(jnp.int32, (n, c, c), 1)
    col = lax.broadcasted_iota(jnp.int32, (n, c, c), 2)
    g_wide = jnp.broadcast_to(gcol, (n, c, d))
    b_wide = jnp.broadcast_to(bcol, (n, c, d))
    gc = _chunk_cumsum(g_wide)
    gc_rows = gc[:, :, :c]
    decay = jnp.exp(jnp.where(row >= col, gc_rows - jnp.swapaxes(gc_rows, 1, 2), -1e30))
    kb = k * b_wide
    eg = jnp.exp(gc)
    a = jnp.where(row > col, _BNT(kb, k) * decay, 0.0)
    tinv = _unit_lower_inverse(a)
    u = _BNN(tinv, v * b_wide)
    w = _BNN(tinv, kb * eg)
    attn = _BNT(q, k) * decay
    g_last = jnp.sum(g_wide, axis=1, keepdims=True)
    return u, w, attn, q * eg, k * jnp.exp(g_last - gc), jnp.exp(g_last)


def _dn_chunks(q, k, v, gcol, bcol, states):
    u, w, attn, qe, kdec, eglast = _dn_chunks_head(q, k, v, gcol, bcol)
    v_new = u - _BNN(w, states)
    o = _BNN(qe, states) + _BNN(attn, v_new)
    return (o, states * eglast + _BTN(kdec, v_new)), (w, attn, qe, kdec, eglast)


def _conv_taps(buf, head, cw, rows):
    acc = cw[0:1, :] * buf[head, pl.ds(5, rows), :]
    for j in range(1, 4):
        acc = acc + cw[j:j + 1, :] * buf[head, pl.ds(5 + j, rows), :]
    return acc


def _conv_backward(dbuf, dhead, xbuf, xhead, cw, dxc, rows):
    dbuf[dhead, pl.ds(0, rows), :] = dxc
    dx = cw[0:1, :] * dbuf[dhead, pl.ds(3, rows), :]
    for j in range(1, 4):
        dx = dx + cw[j:j + 1, :] * dbuf[dhead, pl.ds(3 - j, rows), :]
    dcw = jnp.concatenate(
        [jnp.sum(dxc * xbuf[xhead, pl.ds(5 + j, rows), :], axis=0, keepdims=True) for j in range(4)], axis=0)
    dbuf[dhead, pl.ds(rows, 8), :] = dbuf[dhead, pl.ds(0, 8), :]
    return dx, dcw


def _params(**kw):
    return pltpu.CompilerParams(vmem_limit_bytes=VMEM_LIMIT, **kw)


def _matmul(a, b, form, tm, tn, tk, name, add=None, out_dtype=F32):
    if form == "nn":
        (m, kdim), (_, n) = a.shape, b.shape
        a_spec = pl.BlockSpec((tm, tk), lambda j, i, k: (i, k))
        b_spec = pl.BlockSpec((tk, tn), lambda j, i, k: (k, j))
        dims = NN
    elif form == "nt":
        (m, kdim), (n, _) = a.shape, b.shape
        a_spec = pl.BlockSpec((tm, tk), lambda j, i, k: (i, k))
        b_spec = pl.BlockSpec((tn, tk), lambda j, i, k: (j, k))
        dims = NT
    else:
        (kdim, m), (_, n) = a.shape, b.shape
        a_spec = pl.BlockSpec((tk, tm), lambda j, i, k: (k, i))
        b_spec = pl.BlockSpec((tk, tn), lambda j, i, k: (k, j))
        dims = TN
    assert m % tm == 0 and n % tn == 0 and kdim % tk == 0, (name, m, n, kdim, tm, tn, tk)
    ksteps = kdim // tk
    o_spec = pl.BlockSpec((tm, tn), lambda j, i, k: (i, j))
    has_add = add is not None

    def body(*refs):
        if has_add:
            a_ref, b_ref, c_ref, o_ref, acc = refs
        else:
            a_ref, b_ref, o_ref, acc = refs
        k = pl.program_id(2)

        @pl.when(k == 0)
        def _():
            acc[...] = c_ref[...] if has_add else jnp.zeros_like(acc)

        acc[...] += lax.dot_general(a_ref[...].astype(BF16), b_ref[...].astype(BF16), dims,
                                    preferred_element_type=F32)

        @pl.when(k == ksteps - 1)
        def _():
            o_ref[...] = acc[...].astype(o_ref.dtype)

    in_specs = [a_spec, b_spec] + ([o_spec] if has_add else [])
    args = (a, b) + ((add,) if has_add else ())
    return pl.pallas_call(
        body, name=name, grid=(n // tn, m // tm, ksteps), in_specs=in_specs, out_specs=o_spec,
        out_shape=jax.ShapeDtypeStruct((m, n), out_dtype), scratch_shapes=[pltpu.VMEM((tm, tn), F32)],
        compiler_params=_params(dimension_semantics=("parallel", "parallel", "arbitrary")),
    )(*args)


def _rmsnorm_fwd(x, w_row, name):
    s = x.shape[0]
    tb = min(TIME_BLOCK, s)

    def body(x_ref, w_ref, o_ref):
        o_ref[...] = _rmsnorm(x_ref[...], w_ref[...]).astype(BF16)

    return pl.pallas_call(
        body, name=name, grid=(s // tb,),
        in_specs=[pl.BlockSpec((tb, D_MODEL), lambda i: (i, 0)), pl.BlockSpec((1, D_MODEL), lambda i: (0, 0))],
        out_specs=pl.BlockSpec((tb, D_MODEL), lambda i: (i, 0)),
        out_shape=jax.ShapeDtypeStruct((s, D_MODEL), BF16), compiler_params=_params(),
    )(x, w_row)


def _rmsnorm_bwd(x, w_row, dh, dres, name):
    s = x.shape[0]
    tb = min(TIME_BLOCK, s)

    def body(x_ref, w_ref, dh_ref, dres_ref, dx_ref, dw_ref):
        _, vjp = jax.vjp(_rmsnorm, x_ref[...], w_ref[...])
        dx, dw = vjp(dh_ref[...])
        dx_ref[...] = dres_ref[...] + dx

        @pl.when(pl.program_id(0) == 0)
        def _():
            dw_ref[...] = jnp.zeros_like(dw_ref)

        dw_ref[...] += dw

    row = pl.BlockSpec((tb, D_MODEL), lambda i: (i, 0))
    vec = pl.BlockSpec((1, D_MODEL), lambda i: (0, 0))
    return pl.pallas_call(
        body, name=name, grid=(s // tb,), in_specs=[row, vec, row, row], out_specs=[row, vec],
        out_shape=[jax.ShapeDtypeStruct((s, D_MODEL), F32), jax.ShapeDtypeStruct((1, D_MODEL), F32)],
        compiler_params=_params(),
    )(x, w_row, dh, dres)


def _final_loss(x, w_row, target, name):
    s = x.shape[0]
    tb = min(TIME_BLOCK, s)

    def loss_fn(xv, wv, tv):
        err = _rmsnorm(xv, wv) - tv
        return 0.5 * jnp.sum(jnp.sum(err * err, axis=-1, keepdims=True), axis=0, keepdims=True) * (1.0 / D_MODEL)

    def body(x_ref, w_ref, t_ref, loss_ref, dx_ref, dw_ref):
        tv = t_ref[...]
        loss, vjp = jax.vjp(lambda xv, wv: loss_fn(xv, wv, tv), x_ref[...], w_ref[...])
        dx, dw = vjp(jnp.ones((1, 1), F32))
        dx_ref[...] = dx

        @pl.when(pl.program_id(0) == 0)
        def _():
            dw_ref[...] = jnp.zeros_like(dw_ref)
            loss_ref[...] = jnp.zeros_like(loss_ref)

        dw_ref[...] += dw
        loss_ref[...] += jnp.broadcast_to(loss, loss_ref.shape)

    row = pl.BlockSpec((tb, D_MODEL), lambda i: (i, 0))
    vec = pl.BlockSpec((1, D_MODEL), lambda i: (0, 0))
    return pl.pallas_call(
        body, name=name, grid=(s // tb,), in_specs=[row, vec, row],
        out_specs=[pl.BlockSpec((1, HEAD_DIM), lambda i: (0, 0)), row, vec],
        out_shape=[jax.ShapeDtypeStruct((1, HEAD_DIM), F32), jax.ShapeDtypeStruct((s, D_MODEL), F32),
                   jax.ShapeDtypeStruct((1, D_MODEL), F32)],
        compiler_params=_params(),
    )(x, w_row, target)


def _head_specs(tb, time_of):
    def col(off):
        return pl.BlockSpec((tb, HEAD_DIM), lambda t, h: (time_of(t), off + h))
    return col


def _vec_spec():
    return pl.BlockSpec((1, HEAD_DIM), lambda t, h: (0, h))


def _lru_fwd(proj, conv_w, conv_b, wa, ba, wx, bx, lam, nw, name):
    s = proj.shape[0]
    tb = min(TIME_BLOCK, s)
    nt = s // tb
    col = _head_specs(tb, lambda t: t)

    def body(x_ref, z_ref, cw_ref, cb_ref, wa_ref, ba_ref, wx_ref, bx_ref, lam_ref, nw_ref,
             y_ref, hs_ref, xbuf, hcar):
        t, h = pl.program_id(0), pl.program_id(1)

        @pl.when(t == 0)
        def _():
            xbuf[h, pl.ds(0, 8), :] = jnp.zeros((8, HEAD_DIM), F32)
            hcar[h] = jnp.zeros((8, HEAD_DIM), F32)

        xbuf[h, pl.ds(8, tb), :] = x_ref[...]
        xc = _conv_taps(xbuf, h, cw_ref[...], tb) + cb_ref[...]
        a, b = _lru_gates(xc, wa_ref[...], ba_ref[...], wx_ref[...], bx_ref[...], lam_ref[...])
        hs_ref[...] = _scan_forward(a, b, hcar[h, pl.ds(0, 1), :])
        hcar[h, pl.ds(0, 1), :] = hs_ref[pl.ds(tb - 1, 1), :]
        xbuf[h, pl.ds(0, 8), :] = xbuf[h, pl.ds(tb, 8), :]
        y_ref[...] = _gated_norm(hs_ref[...], z_ref[...], nw_ref[...]).astype(BF16)

    vec = _vec_spec()
    return pl.pallas_call(
        body, name=name, grid=(nt, HEADS),
        in_specs=[col(COL_LRU_X), col(COL_LRU_Z), pl.BlockSpec((4, HEAD_DIM), lambda t, h: (0, h)), vec,
                  pl.BlockSpec((None, HEAD_DIM, HEAD_DIM), lambda t, h: (h, 0, 0)), vec,
                  pl.BlockSpec((None, HEAD_DIM, HEAD_DIM), lambda t, h: (h, 0, 0)), vec, vec, vec],
        out_specs=[col(0), col(0)],
        out_shape=[jax.ShapeDtypeStruct((s, D_MODEL), BF16), jax.ShapeDtypeStruct((s, D_MODEL), F32)],
        scratch_shapes=[pltpu.VMEM((HEADS, tb + 8, HEAD_DIM), F32), pltpu.VMEM((HEADS, 8, HEAD_DIM), F32)],
        compiler_params=_params(dimension_semantics=("arbitrary", "arbitrary")),
    )(proj, proj, conv_w, conv_b, wa, ba, wx, bx, lam, nw)


def _halo_spec(tb, nt, off):
    per = tb // 8
    return pl.BlockSpec((8, HEAD_DIM), lambda t, h: (jnp.maximum((nt - 1 - t) * per - 1, 0), off + h))


def _lru_bwd(proj, hs, dy, conv_w, conv_b, wa, ba, wx, bx, lam, nw, name):
    s = proj.shape[0]
    tb = min(TIME_BLOCK, s)
    nt = s // tb
    col = _head_specs(tb, lambda t: nt - 1 - t)

    def body(x_ref, xh_ref, z_ref, hs_ref, hh_ref, dy_ref, cw_ref, cb_ref, wa_ref, ba_ref, wx_ref, bx_ref,
             lam_ref, nw_ref, dx_ref, dz_ref, dcw_ref, dcb_ref, dwa_ref, dba_ref, dwx_ref, dbx_ref, dlam_ref,
             dnw_ref, xbuf, hbuf, dbuf, gcar):
        t, h = pl.program_id(0), pl.program_id(1)
        first_block = t == nt - 1

        @pl.when(t == 0)
        def _():
            dbuf[h, pl.ds(tb, 8), :] = jnp.zeros((8, HEAD_DIM), F32)
            gcar[h] = jnp.zeros((8, HEAD_DIM), F32)
            dcw_ref[h] = jnp.zeros((4, HEAD_DIM), F32)
            dwa_ref[h] = jnp.zeros((HEAD_DIM, HEAD_DIM), F32)
            dwx_ref[h] = jnp.zeros((HEAD_DIM, HEAD_DIM), F32)
            for ref in (dcb_ref, dba_ref, dbx_ref, dlam_ref, dnw_ref):
                ref[h] = jnp.zeros((1, HEAD_DIM), F32)

        keep = jnp.where(first_block, 0.0, 1.0)
        xbuf[0, pl.ds(0, 8), :] = xh_ref[...] * keep
        xbuf[0, pl.ds(8, tb), :] = x_ref[...]
        hbuf[pl.ds(0, 8), :] = hh_ref[...] * keep
        hbuf[pl.ds(8, tb), :] = hs_ref[...]
        cw = cw_ref[...]
        xc = _conv_taps(xbuf, 0, cw, tb) + cb_ref[...]
        (a, _), gates_vjp = jax.vjp(_lru_gates, xc, wa_ref[...], ba_ref[...], wx_ref[...], bx_ref[...], lam_ref[...])
        _, norm_vjp = jax.vjp(_gated_norm, hs_ref[...], z_ref[...], nw_ref[...])
        dh, dz, dnw = norm_vjp(dy_ref[...])
        dz_ref[...] = dz.astype(dz_ref.dtype)
        g = _scan_reverse(a, dh, gcar[h, pl.ds(0, 1), :])
        gcar[h, pl.ds(0, 1), :] = a[0:1, :] * g[0:1, :]
        dxc, dwa, dba, dwx, dbx, dlam = gates_vjp((g * hbuf[pl.ds(7, tb), :], g))
        dx, dcw = _conv_backward(dbuf, h, xbuf, 0, cw, dxc, tb)
        dx_ref[...] = dx.astype(dx_ref.dtype)
        dcw_ref[h] += dcw
        dcb_ref[h] += jnp.sum(dxc, axis=0, keepdims=True)
        dwa_ref[h] += dwa
        dwx_ref[h] += dwx
        dba_ref[h] += dba
        dbx_ref[h] += dbx
        dlam_ref[h] += dlam
        dnw_ref[h] += dnw

    vec = _vec_spec()
    mat = pl.BlockSpec((None, HEAD_DIM, HEAD_DIM), lambda t, h: (h, 0, 0))

    def whole(shape):
        return pl.BlockSpec(shape, lambda t, h: (0,) * len(shape))

    head_vec = jax.ShapeDtypeStruct((HEADS, 1, HEAD_DIM), F32)
    head_mat = jax.ShapeDtypeStruct((HEADS, HEAD_DIM, HEAD_DIM), F32)
    return pl.pallas_call(
        body, name=name, grid=(nt, HEADS),
        in_specs=[col(COL_LRU_X), _halo_spec(tb, nt, COL_LRU_X), col(COL_LRU_Z), col(0), _halo_spec(tb, nt, 0), col(0),
                  pl.BlockSpec((4, HEAD_DIM), lambda t, h: (0, h)), vec, mat, vec, mat, vec, vec, vec],
        out_specs=[col(0), col(0), whole((HEADS, 4, HEAD_DIM)), whole((HEADS, 1, HEAD_DIM)),
                   whole((HEADS, HEAD_DIM, HEAD_DIM)), whole((HEADS, 1, HEAD_DIM)),
                   whole((HEADS, HEAD_DIM, HEAD_DIM)), whole((HEADS, 1, HEAD_DIM)), whole((HEADS, 1, HEAD_DIM)),
                   whole((HEADS, 1, HEAD_DIM))],
        out_shape=[jax.ShapeDtypeStruct((s, D_MODEL), BF16), jax.ShapeDtypeStruct((s, D_MODEL), BF16),
                   jax.ShapeDtypeStruct((HEADS, 4, HEAD_DIM), F32), head_vec, head_mat, head_vec, head_mat, head_vec,
                   head_vec, head_vec],
        scratch_shapes=[pltpu.VMEM((1, tb + 8, HEAD_DIM), F32), pltpu.VMEM((tb + 8, HEAD_DIM), F32),
                        pltpu.VMEM((HEADS, tb + 8, HEAD_DIM), F32), pltpu.VMEM((HEADS, 8, HEAD_DIM), F32)],
        compiler_params=_params(dimension_semantics=("arbitrary", "arbitrary")),
    )(proj, proj, proj, hs, hs, dy, conv_w, conv_b, wa, ba, wx, bx, lam, nw)


def _dn_fwd(proj, conv_w, a_log_row, dt_row, nw, name):
    s = proj.shape[0]
    tb = min(TIME_BLOCK, s)
    nt = s // tb
    nchunk = tb // CHUNK
    col = _head_specs(tb, lambda t: t)

    def body(q_ref, k_ref, v_ref, z_ref, ba_ref, cwq_ref, cwk_ref, cwv_ref, al_ref, dt_ref, nw_ref,
             y_ref, o_ref, st_ref, xbuf, state):
        t, h = pl.program_id(0), pl.program_id(1)

        @pl.when(t == 0)
        def _():
            for i in range(3):
                xbuf[3 * h + i, pl.ds(0, 8), :] = jnp.zeros((8, HEAD_DIM), F32)
            state[h] = jnp.zeros((HEAD_DIM, HEAD_DIM), F32)

        conv = []
        for i, (ref, cw_ref) in enumerate(((q_ref, cwq_ref), (k_ref, cwk_ref), (v_ref, cwv_ref))):
            xbuf[3 * h + i, pl.ds(8, tb), :] = ref[...]
            conv.append(_conv_taps(xbuf, 3 * h + i, cw_ref[...], tb))
            xbuf[3 * h + i, pl.ds(0, 8), :] = xbuf[3 * h + i, pl.ds(tb, 8), :]
        q, k, v, g, beta = _dn_prep(conv[0], conv[1], conv[2], ba_ref[...], al_ref[...], dt_ref[...], h)
        def chunks(a):
            return a.reshape(nchunk, CHUNK, a.shape[-1])

        u, w, attn, qe, kdec, eglast = _dn_chunks_head(chunks(q), chunks(k), chunks(v), chunks(g), chunks(beta))
        st = state[h]
        for c in range(nchunk):
            st_ref[c] = st
            v_new = u[c] - _NN_B(w[c], st)
            o_ref[pl.ds(c * CHUNK, CHUNK), :] = _NN_B(qe[c], st) + _NN_B(attn[c], v_new)
            st = st * eglast[c] + _TN_B(kdec[c], v_new)
        state[h] = st
        y_ref[...] = _gated_norm(o_ref[...], z_ref[...], nw_ref[...]).astype(BF16)

    def cw_spec(off):
        return pl.BlockSpec((4, HEAD_DIM), lambda t, h: (0, off + h))

    row128 = pl.BlockSpec((1, HEAD_DIM), lambda t, h: (0, 0))
    return pl.pallas_call(
        body, name=name, grid=(nt, HEADS),
        in_specs=[col(COL_Q), col(COL_K), col(COL_V), col(COL_DN_Z),
                  pl.BlockSpec((tb, HEAD_DIM), lambda t, h: (t, COL_BA)),
                  cw_spec(0), cw_spec(HEADS), cw_spec(2 * HEADS), row128, row128, row128],
        out_specs=[col(0), col(0), pl.BlockSpec((None, nchunk, HEAD_DIM, HEAD_DIM), lambda t, h: (h, t, 0, 0))],
        out_shape=[jax.ShapeDtypeStruct((s, D_MODEL), BF16), jax.ShapeDtypeStruct((s, D_MODEL), F32),
                   jax.ShapeDtypeStruct((HEADS, s // CHUNK, HEAD_DIM, HEAD_DIM), F32)],
        scratch_shapes=[pltpu.VMEM((3 * HEADS, tb + 8, HEAD_DIM), F32), pltpu.VMEM((HEADS, HEAD_DIM, HEAD_DIM), F32)],
        compiler_params=_params(dimension_semantics=("arbitrary", "arbitrary")),
    )(proj, proj, proj, proj, proj, conv_w, conv_w, conv_w, a_log_row, dt_row, nw)


def _dn_bwd(proj, o, states, dy, conv_w, a_log_row, dt_row, nw, name):
    s = proj.shape[0]
    tb = min(TIME_BLOCK, s)
    nt = s // tb
    nchunk = tb // CHUNK
    col = _head_specs(tb, lambda t: nt - 1 - t)

    def body(q_ref, qh_ref, k_ref, kh_ref, v_ref, vh_ref, z_ref, ba_ref, o_ref, st_ref, dy_ref,
             cwq_ref, cwk_ref, cwv_ref, al_ref, dt_ref, nw_ref,
             dq_ref, dk_ref, dv_ref, dz_ref, dba_ref, dcw_ref, dal_ref, ddt_ref, dnw_ref,
             xbuf, dbuf, dstate, dst_s):
        t, h = pl.program_id(0), pl.program_id(1)
        first_block = t == nt - 1

        @pl.when(t == 0)
        def _():
            for i in range(3):
                dbuf[3 * h + i, pl.ds(tb, 8), :] = jnp.zeros((8, HEAD_DIM), F32)
                dcw_ref[3 * h + i] = jnp.zeros((4, HEAD_DIM), F32)
            dstate[h] = jnp.zeros((HEAD_DIM, HEAD_DIM), F32)

        @pl.when((t == 0) & (h == 0))
        def _():
            for ref in (dal_ref, ddt_ref, dnw_ref):
                ref[...] = jnp.zeros_like(ref)

        keep = jnp.where(first_block, 0.0, 1.0)
        cws = (cwq_ref[...], cwk_ref[...], cwv_ref[...])
        conv = []
        for i, (ref, halo) in enumerate(((q_ref, qh_ref), (k_ref, kh_ref), (v_ref, vh_ref))):
            xbuf[i, pl.ds(0, 8), :] = halo[...] * keep
            xbuf[i, pl.ds(8, tb), :] = ref[...]
            conv.append(_conv_taps(xbuf, i, cws[i], tb))
        (q, k, v, g, beta), prep_vjp = jax.vjp(
            lambda qc, kc, vc, ba, al, dt: _dn_prep(qc, kc, vc, ba, al, dt, h),
            conv[0], conv[1], conv[2], ba_ref[...], al_ref[...], dt_ref[...])
        _, norm_vjp = jax.vjp(_gated_norm, o_ref[...], z_ref[...], nw_ref[...])
        do, dz, dnw = norm_vjp(dy_ref[...])
        dz_ref[...] = dz.astype(dz_ref.dtype)
        dnw_ref[...] += dnw

        def chunks(a):
            return a.reshape(nchunk, CHUNK, a.shape[-1])

        do = chunks(do)
        _, chunks_vjp, (w, attn, qe, kdec, eglast) = jax.vjp(
            _dn_chunks, chunks(q), chunks(k), chunks(v), chunks(g), chunks(beta), st_ref[...], has_aux=True)
        from_o = _dot(attn, do, B_TN, "bf16")
        from_qe = _dot(qe, do, B_TN, "bf16")
        dst = dstate[h]
        for c in reversed(range(nchunk)):
            dst_s[c] = dst
            dv_new = from_o[c] + _dot(kdec[c], dst, NN, "bf16")
            dst = dst * eglast[c] + from_qe[c] - _dot(w[c], dv_new, TN, "bf16")
        dstate[h] = dst
        dq, dk, dv, dg, db, _ = chunks_vjp((do, dst_s[...]))

        def rows(a):
            return a.reshape(tb, a.shape[-1])

        dqc, dkc, dvc, dba, dal, ddt = prep_vjp((rows(dq), rows(dk), rows(dv), rows(dg), rows(db)))
        for i, (dxc, out) in enumerate(((dqc, dq_ref), (dkc, dk_ref), (dvc, dv_ref))):
            dx, dcw = _conv_backward(dbuf, 3 * h + i, xbuf, i, cws[i], dxc, tb)
            out[...] = dx.astype(out.dtype)
            dcw_ref[3 * h + i] += dcw
        dal_ref[...] += dal
        ddt_ref[...] += ddt

        @pl.when(h == 0)
        def _():
            dba_ref[...] = dba.astype(dba_ref.dtype)

        @pl.when(h > 0)
        def _():
            dba_ref[...] += dba.astype(dba_ref.dtype)

    def cw_spec(off):
        return pl.BlockSpec((4, HEAD_DIM), lambda t, h: (0, off + h))

    def whole(shape):
        return pl.BlockSpec(shape, lambda t, h: (0,) * len(shape))

    row128 = whole((1, HEAD_DIM))
    blk = (tb, HEAD_DIM)
    act = jax.ShapeDtypeStruct((s, D_MODEL), BF16)
    row_out = jax.ShapeDtypeStruct((1, HEAD_DIM), F32)
    return pl.pallas_call(
        body, name=name, grid=(nt, HEADS),
        in_specs=[col(COL_Q), _halo_spec(tb, nt, COL_Q), col(COL_K), _halo_spec(tb, nt, COL_K),
                  col(COL_V), _halo_spec(tb, nt, COL_V), col(COL_DN_Z),
                  pl.BlockSpec(blk, lambda t, h: (nt - 1 - t, COL_BA)), col(0),
                  pl.BlockSpec((None, nchunk, HEAD_DIM, HEAD_DIM), lambda t, h: (h, nt - 1 - t, 0, 0)), col(HEADS),
                  cw_spec(0), cw_spec(HEADS), cw_spec(2 * HEADS), row128, row128, row128],
        out_specs=[col(0), col(0), col(0), col(0), pl.BlockSpec(blk, lambda t, h: (nt - 1 - t, 0)),
                   whole((3 * HEADS, 4, HEAD_DIM)), row128, row128, row128],
        out_shape=[act, act, act, act, jax.ShapeDtypeStruct((s, HEAD_DIM), F32),
                   jax.ShapeDtypeStruct((3 * HEADS, 4, HEAD_DIM), F32), row_out, row_out, row_out],
        scratch_shapes=[pltpu.VMEM((3, tb + 8, HEAD_DIM), F32), pltpu.VMEM((3 * HEADS, tb + 8, HEAD_DIM), F32),
                        pltpu.VMEM((HEADS, HEAD_DIM, HEAD_DIM), F32), pltpu.VMEM((nchunk, HEAD_DIM, HEAD_DIM), F32)],
        compiler_params=_params(dimension_semantics=("arbitrary", "arbitrary")),
    )(proj, proj, proj, proj, proj, proj, proj, proj, o, states, dy, conv_w, conv_w, conv_w, a_log_row, dt_row, nw)


def _mesh_position():
    x, y, c = lax.axis_index("x"), lax.axis_index("y"), lax.axis_index("c")
    return x, y, c, 4 * x + 2 * y + c


def _peer(k, x, y, c):
    px = 1 - x if k & 4 else x
    py = 1 - y if k & 2 else y
    pc = 1 - c if k & 1 else c
    return (px, py, pc), 4 * px + 2 * py + pc


def _direct_exchange(arrays, scatter, name):
    n = len(arrays)
    out_shapes = [jax.ShapeDtypeStruct(a.shape if scatter else (N_DEV,) + a.shape, a.dtype) for a in arrays]

    def body(*refs):
        ins, outs = refs[:n], refs[n:2 * n]
        send_sems, recv_sems, local_sems = refs[2 * n:]
        x, y, c, me = _mesh_position()
        local, sends, recvs = [], [], []
        for i in range(n):
            cp = pltpu.make_async_copy(ins[i].at[me] if scatter else ins[i], outs[i].at[me], local_sems.at[i])
            cp.start()
            local.append(cp)
            for k in range(1, N_DEV):
                peer, peer_id = _peer(k, x, y, c)
                sem = i * (N_DEV - 1) + k - 1
                cp = pltpu.make_async_remote_copy(
                    src_ref=ins[i].at[peer_id] if scatter else ins[i], dst_ref=outs[i].at[me],
                    send_sem=send_sems.at[sem], recv_sem=recv_sems.at[sem],
                    device_id=peer, device_id_type=pl.DeviceIdType.MESH)
                cp.start()
                sends.append(cp)
                recvs.append(pltpu.make_async_remote_copy(
                    src_ref=ins[i].at[peer_id] if scatter else ins[i], dst_ref=outs[i].at[peer_id],
                    send_sem=send_sems.at[sem], recv_sem=recv_sems.at[sem],
                    device_id=peer, device_id_type=pl.DeviceIdType.MESH))
        for cp in recvs:
            cp.wait_recv()
        for cp in sends:
            cp.wait_send()
        for cp in local:
            cp.wait()

    hbm = pl.BlockSpec(memory_space=pl.ANY)
    return pl.pallas_call(
        body, name=name, in_specs=[hbm] * n, out_specs=[hbm] * n, out_shape=out_shapes,
        scratch_shapes=[pltpu.SemaphoreType.DMA((n * (N_DEV - 1),)), pltpu.SemaphoreType.DMA((n * (N_DEV - 1),)),
                        pltpu.SemaphoreType.DMA((n,))],
    )(*arrays)


def _adamw(parts, w, m, v, name, rows_per_step):
    rows, cols = w.shape
    tr = min(rows_per_step, rows)
    assert rows % tr == 0, (name, rows, tr)
    c1 = 1.0 / (1.0 - ADAM_B1 ** ADAM_STEP)
    c2 = 1.0 / (1.0 - ADAM_B2 ** ADAM_STEP)

    def body(p_ref, w_ref, m_ref, v_ref, g_ref, d_ref, nm_ref, nv_ref):
        g = p_ref[0]
        for d in range(1, N_DEV):
            g = g + p_ref[d]
        nm = ADAM_B1 * m_ref[...] + (1.0 - ADAM_B1) * g
        nv = ADAM_B2 * v_ref[...] + (1.0 - ADAM_B2) * (g * g)
        g_ref[...] = g
        nm_ref[...] = nm
        nv_ref[...] = nv
        d_ref[...] = -ADAM_LR * ((nm * c1) / (jnp.sqrt(nv * c2) + ADAM_EPS) + ADAM_WD * w_ref[...])

    blk = pl.BlockSpec((tr, cols), lambda i: (i, 0))
    shape = jax.ShapeDtypeStruct((rows, cols), F32)
    return pl.pallas_call(
        body, name=name, grid=(rows // tr,),
        in_specs=[pl.BlockSpec((N_DEV, tr, cols), lambda i: (0, i, 0)), blk, blk, blk],
        out_specs=[blk] * 4, out_shape=[shape] * 4, compiler_params=_params(),
    )(parts, w, m, v)


_REPLICATED = ("norm_w", "lru_conv_b", "lru_wa", "lru_ba", "lru_wx", "lru_bx", "lru_lambda", "lru_norm_w",
               "dn_A_log", "dn_dt_bias", "dn_norm_w", "final_norm_w")
_SHARDED = ("w_in", "lru_conv_w", "dn_conv_w", "w_out")
_WEIGHTS = ("norm_w", "w_in", "lru_conv_w", "lru_conv_b", "lru_wa", "lru_ba", "lru_wx", "lru_bx", "lru_lambda",
            "lru_norm_w", "dn_conv_w", "dn_A_log", "dn_dt_bias", "dn_norm_w", "w_out", "final_norm_w")


def _pack_rows(tree):
    rows = []
    for name in _REPLICATED:
        a = tree[name]
        if a.shape[-1] == HEADS:
            a = jnp.pad(a, ((0, 0), (0, HEAD_DIM - HEADS)))
        rows.append(a.reshape(-1, HEAD_DIM))
    packed = jnp.concatenate(rows, axis=0)
    return jnp.pad(packed, ((0, (-packed.shape[0]) % PACK_ROWS), (0, 0)))


def _unpack_rows(packed, like):
    out, at = {}, 0
    for name in _REPLICATED:
        shape = like[name].shape
        if shape[-1] == HEADS:
            n = shape[0]
            out[name] = packed[at:at + n, :HEADS]
        else:
            n = like[name].size // HEAD_DIM
            out[name] = packed[at:at + n].reshape(shape)
        at += n
    return out


def _heads_to_channels(a):
    return jnp.transpose(a, (1, 0, 2)).reshape(a.shape[1], HEADS * HEAD_DIM)


def kernel(x, norm_w, w_in, lru_conv_w, lru_conv_b, lru_wa, lru_ba, lru_wx, lru_bx, lru_lambda, lru_norm_w, dn_conv_w, dn_A_log, dn_dt_bias, dn_norm_w, w_out, final_norm_w, loss_target, m_norm_w, m_w_in, m_lru_conv_w, m_lru_conv_b, m_lru_wa, m_lru_ba, m_lru_wx, m_lru_bx, m_lru_lambda, m_lru_norm_w, m_dn_conv_w, m_dn_A_log, m_dn_dt_bias, m_dn_norm_w, m_w_out, m_final_norm_w, v_norm_w, v_w_in, v_lru_conv_w, v_lru_conv_b, v_lru_wa, v_lru_ba, v_lru_wx, v_lru_bx, v_lru_lambda, v_lru_norm_w, v_dn_conv_w, v_dn_A_log, v_dn_dt_bias, v_dn_norm_w, v_w_out, v_final_norm_w):
    weights = dict(norm_w=norm_w, w_in=w_in, lru_conv_w=lru_conv_w, lru_conv_b=lru_conv_b, lru_wa=lru_wa,
                   lru_ba=lru_ba, lru_wx=lru_wx, lru_bx=lru_bx, lru_lambda=lru_lambda, lru_norm_w=lru_norm_w,
                   dn_conv_w=dn_conv_w, dn_A_log=dn_A_log, dn_dt_bias=dn_dt_bias, dn_norm_w=dn_norm_w,
                   w_out=w_out, final_norm_w=final_norm_w)
    mom_m = dict(norm_w=m_norm_w, w_in=m_w_in, lru_conv_w=m_lru_conv_w, lru_conv_b=m_lru_conv_b, lru_wa=m_lru_wa,
                 lru_ba=m_lru_ba, lru_wx=m_lru_wx, lru_bx=m_lru_bx, lru_lambda=m_lru_lambda,
                 lru_norm_w=m_lru_norm_w, dn_conv_w=m_dn_conv_w, dn_A_log=m_dn_A_log, dn_dt_bias=m_dn_dt_bias,
                 dn_norm_w=m_dn_norm_w, w_out=m_w_out, final_norm_w=m_final_norm_w)
    mom_v = dict(norm_w=v_norm_w, w_in=v_w_in, lru_conv_w=v_lru_conv_w, lru_conv_b=v_lru_conv_b, lru_wa=v_lru_wa,
                 lru_ba=v_lru_ba, lru_wx=v_lru_wx, lru_bx=v_lru_bx, lru_lambda=v_lru_lambda,
                 lru_norm_w=v_lru_norm_w, dn_conv_w=v_dn_conv_w, dn_A_log=v_dn_A_log, dn_dt_bias=v_dn_dt_bias,
                 dn_norm_w=v_dn_norm_w, w_out=v_w_out, final_norm_w=v_final_norm_w)
    depth = norm_w.shape[0]
    xs = x[0]
    s = xs.shape[0]
    tm = min(512, s)

    g_win, g_lcw, g_dcw, g_wout = _direct_exchange(
        [w_in.astype(BF16), lru_conv_w, dn_conv_w, w_out.astype(BF16)], False, "gather_weights")
    win = jnp.transpose(g_win, (1, 2, 0, 3)).reshape(depth, D_MODEL, D_IN)
    win = jnp.pad(win, ((0, 0), (0, 0), (0, D_IN_PAD - D_IN)))
    wout = jnp.transpose(g_wout, (1, 0, 2, 3)).reshape(depth, 2 * D_MODEL, D_MODEL)
    lcw = jnp.transpose(g_lcw, (1, 2, 0, 3)).reshape(depth, 4, D_MODEL)
    dcw = jnp.transpose(g_dcw, (1, 2, 0, 3)).reshape(depth, 4, 3 * D_MODEL)

    def row(a):
        return a.reshape(1, -1)

    def pad_row(a):
        return jnp.pad(a, (0, HEAD_DIM - a.shape[0])).reshape(1, HEAD_DIM)

    saved = []
    cur = xs
    for l in range(depth):
        hn = _rmsnorm_fwd(cur, row(norm_w[l]), f"norm_fwd_{l}")
        proj = _matmul(hn, win[l], "nn", tm, 896, D_MODEL, f"in_proj_{l}")
        y_lru, hs = _lru_fwd(proj, lcw[l], row(lru_conv_b[l]), lru_wa[l], row(lru_ba[l]), lru_wx[l], row(lru_bx[l]),
                             row(lru_lambda[l]), row(lru_norm_w[l]), f"lru_fwd_{l}")
        y_dn, o_dn, states = _dn_fwd(proj, dcw[l], pad_row(dn_A_log[l]), pad_row(dn_dt_bias[l]), row(dn_norm_w[l]),
                                     f"dn_fwd_{l}")
        ycat = jnp.concatenate([y_lru, y_dn], axis=1)
        nxt = _matmul(ycat, wout[l], "nn", tm, D_MODEL, 2 * D_MODEL, f"out_proj_{l}", add=cur)
        saved.append((cur, hn, proj, hs, o_dn, states, ycat))
        cur = nxt
    loss_part, dx, d_final = _final_loss(cur, row(final_norm_w), loss_target[0], "final_loss")

    grads = {k: [None] * depth for k in _WEIGHTS if k != "final_norm_w"}
    for l in reversed(range(depth)):
        x_in, hn, proj, hs, o_dn, states, ycat = saved[l]
        dy = _matmul(dx, wout[l], "nt", tm, D_MODEL, D_MODEL, f"out_proj_dy_{l}")
        grads["w_out"][l] = _matmul(ycat, dx, "tn", D_MODEL, D_MODEL, tm, f"out_proj_dw_{l}")
        (dlx, dlz, g_lcw, g_lcb, g_wa, g_ba, g_wx, g_bx, g_lam, g_lnw) = _lru_bwd(
            proj, hs, dy, lcw[l], row(lru_conv_b[l]), lru_wa[l], row(lru_ba[l]), lru_wx[l], row(lru_bx[l]),
            row(lru_lambda[l]), row(lru_norm_w[l]), f"lru_bwd_{l}")
        (dq, dk, dv, ddz, dba, g_dcw3, g_al, g_dt, g_dnw) = _dn_bwd(
            proj, o_dn, states, dy, dcw[l], pad_row(dn_A_log[l]), pad_row(dn_dt_bias[l]), row(dn_norm_w[l]),
            f"dn_bwd_{l}")
        dproj = jnp.concatenate([dlx, dlz, dq, dk, dv, ddz, dba.astype(BF16)], axis=1)
        dh = _matmul(dproj, win[l], "nt", tm, D_MODEL, 896, f"in_proj_dh_{l}")
        grads["w_in"][l] = _matmul(hn, dproj, "tn", D_MODEL, 896, tm, f"in_proj_dw_{l}")[:, :D_IN]
        dx, g_nw = _rmsnorm_bwd(x_in, row(norm_w[l]), dh, dx, f"norm_bwd_{l}")
        grads["norm_w"][l] = g_nw.reshape(D_MODEL)
        grads["lru_conv_w"][l] = _heads_to_channels(g_lcw)
        grads["lru_conv_b"][l] = g_lcb.reshape(D_MODEL)
        grads["lru_wa"][l] = g_wa
        grads["lru_ba"][l] = g_ba.reshape(D_MODEL)
        grads["lru_wx"][l] = g_wx
        grads["lru_bx"][l] = g_bx.reshape(D_MODEL)
        grads["lru_lambda"][l] = g_lam.reshape(D_MODEL)
        grads["lru_norm_w"][l] = g_lnw.reshape(D_MODEL)
        g_dcw3 = g_dcw3.reshape(HEADS, 3, 4, HEAD_DIM)
        grads["dn_conv_w"][l] = jnp.concatenate([_heads_to_channels(g_dcw3[:, i]) for i in range(3)], axis=1)
        grads["dn_A_log"][l] = g_al[0, :HEADS]
        grads["dn_dt_bias"][l] = g_dt[0, :HEADS]
        grads["dn_norm_w"][l] = g_dnw.reshape(HEAD_DIM)
    part = {k: jnp.stack(v) for k, v in grads.items()}
    part["final_norm_w"] = d_final.reshape(D_MODEL)

    def to_slots(a):
        dd, r, cc = a.shape
        return jnp.transpose(a.reshape(dd, r, N_DEV, cc // N_DEV), (2, 0, 1, 3))

    slots = [to_slots(part["w_in"]), to_slots(part["lru_conv_w"]), to_slots(part["dn_conv_w"]),
             jnp.transpose(part["w_out"].reshape(depth, N_DEV, 2 * D_MODEL // N_DEV, D_MODEL), (1, 0, 2, 3))]
    rep_parts = _direct_exchange([_pack_rows(part)], False, "gather_small_grads")[0]
    r_win, r_lcw, r_dcw, r_wout = _direct_exchange(slots, True, "exchange_grads")

    new = {}
    for name, parts, rows_per_step in (("w_in", r_win, 256), ("lru_conv_w", r_lcw, 8), ("dn_conv_w", r_dcw, 8),
                                       ("w_out", r_wout, 256)):
        w = weights[name]
        flat = (-1, w.shape[-1])
        outs = _adamw(parts.reshape((N_DEV,) + (w.size // w.shape[-1], w.shape[-1])), w.reshape(flat),
                      mom_m[name].reshape(flat), mom_v[name].reshape(flat), f"adamw_{name}", rows_per_step)
        new[name] = [a.reshape(w.shape) for a in outs]
    packed = _adamw(rep_parts, _pack_rows(weights), _pack_rows(mom_m), _pack_rows(mom_v), "adamw_small", PACK_ROWS)
    unpacked = [_unpack_rows(a, weights) for a in packed]
    for name in _REPLICATED:
        new[name] = [u[name] for u in unpacked]

    loss = lax.psum(loss_part[0, 0], ("x", "y", "c"))
    out = [loss, dx.reshape(x.shape)]
    for i in range(4):
        out += [new[name][i] for name in _WEIGHTS]
    return tuple(out)
```

```python
import functools

import jax
import jax.numpy as jnp
from jax import lax
from jax.experimental import pallas as pl
from jax.experimental.pallas import tpu as pltpu

F32 = jnp.float32
BF16 = jnp.bfloat16

N_DEV = 8
D_MODEL = 1024
HEADS = 8
HEAD_DIM = 128
CHUNK = 64
D_IN = 6160
D_IN_PAD = 6272
COL_LRU_X, COL_LRU_Z, COL_Q, COL_K, COL_V, COL_DN_Z, COL_BA = 0, 8, 16, 24, 32, 40, 48
LRU_C = 8.0
EPS = 1e-6
ADAM_LR, ADAM_B1, ADAM_B2, ADAM_EPS, ADAM_WD, ADAM_STEP = 0.001, 0.9, 0.999, 1e-08, 0.01, 10
TIME_BLOCK = 512
PACK_ROWS = 512
VMEM_LIMIT = 56 * 1024 * 1024

NN = (((1,), (0,)), ((), ()))
NT = (((1,), (1,)), ((), ()))
TN = (((0,), (0,)), ((), ()))


B_NN = (((2,), (1,)), ((0,), (0,)))
B_NT = (((2,), (2,)), ((0,), (0,)))
B_TN = (((1,), (1,)), ((0,), (0,)))


def _split_bf16(x):
    hi = x.astype(BF16)
    return hi, (x - hi.astype(F32)).astype(BF16)


def _dot(a, b, dims, prec):
    if prec == "bf16":
        return lax.dot_general(a.astype(BF16), b.astype(BF16), dims, preferred_element_type=F32)
    a1, a2 = _split_bf16(a)
    b1, b2 = _split_bf16(b)
    dg = functools.partial(lax.dot_general, dimension_numbers=dims, preferred_element_type=F32)
    return dg(a1, b1) + (dg(a1, b2) + dg(a2, b1))


def _make_mm(prec, nn_dims, nt_dims, tn_dims):
    @jax.custom_vjp
    def nn(a, b):
        return _dot(a, b, nn_dims, prec)

    @jax.custom_vjp
    def nt(a, b):
        return _dot(a, b, nt_dims, prec)

    @jax.custom_vjp
    def tn(a, b):
        return _dot(a, b, tn_dims, prec)

    nn.defvjp(lambda a, b: (_dot(a, b, nn_dims, prec), (a, b)),
              lambda r, g: (_dot(g, r[1], nt_dims, prec), _dot(r[0], g, tn_dims, prec)))
    nt.defvjp(lambda a, b: (_dot(a, b, nt_dims, prec), (a, b)),
              lambda r, g: (_dot(g, r[1], nn_dims, prec), _dot(g, r[0], tn_dims, prec)))
    tn.defvjp(lambda a, b: (_dot(a, b, tn_dims, prec), (a, b)),
              lambda r, g: (_dot(r[1], g, nt_dims, prec), _dot(r[0], g, nn_dims, prec)))
    return nn, nt, tn


_NN_B, _NT_B, _TN_B = _make_mm("bf16", NN, NT, TN)
_BNN, _BNT, _BTN = _make_mm("bf16", B_NN, B_NT, B_TN)


@jax.custom_vjp
def _unit_lower_inverse(a):
    n = a.shape[-1]
    eye = (lax.broadcasted_iota(jnp.int32, a.shape, 1) == lax.broadcasted_iota(jnp.int32, a.shape, 2)).astype(F32)
    inv = eye - a
    pw = _dot(a, a, B_NN, "bf16x3")
    steps = n.bit_length() - 2
    for j in range(steps):
        inv = inv + _dot(inv, pw, B_NN, "bf16x3")
        if j + 1 < steps:
            pw = _dot(pw, pw, B_NN, "bf16x3")
    return inv


def _uli_fwd(a):
    inv = _unit_lower_inverse(a)
    return inv, inv


def _uli_bwd(inv, g):
    return (-_dot(_dot(inv, g, B_TN, "bf16"), inv, B_NT, "bf16"),)


_unit_lower_inverse.defvjp(_uli_fwd, _uli_bwd)


def _lower_ones(batch, n):
    shape = (batch, n, n)
    return (lax.broadcasted_iota(jnp.int32, shape, 1) >= lax.broadcasted_iota(jnp.int32, shape, 2)).astype(BF16)


@jax.custom_vjp
def _chunk_cumsum(g):
    tri = _lower_ones(g.shape[0], g.shape[1])
    g1, g2 = _split_bf16(g)
    g3 = (g - g1.astype(F32) - g2.astype(F32)).astype(BF16)
    dg = functools.partial(lax.dot_general, dimension_numbers=B_NN, preferred_element_type=F32)
    return dg(tri, g1) + (dg(tri, g2) + dg(tri, g3))


def _chunk_cumsum_bwd(_, ct):
    tri = _lower_ones(ct.shape[0], ct.shape[1])
    c1, c2 = _split_bf16(ct)
    dg = functools.partial(lax.dot_general, dimension_numbers=B_TN, preferred_element_type=F32)
    return (dg(tri, c1) + dg(tri, c2),)


_chunk_cumsum.defvjp(lambda g: (_chunk_cumsum(g), None), _chunk_cumsum_bwd)


def _expm1(x):
    small = x * (1.0 + x * (0.5 + x * (1.0 / 6 + x * (1.0 / 24 + x * (1.0 / 120 + x * (1.0 / 720))))))
    return jnp.where(jnp.abs(x) < 0.2, small, jnp.exp(x) - 1.0)


def _sigmoid(x):
    return 1.0 / (1.0 + jnp.exp(-x))


def _silu(x):
    return x * _sigmoid(x)


def _softplus(x):
    return jnp.maximum(x, 0.0) + jnp.log(1.0 + jnp.exp(-jnp.abs(x)))


def _rmsnorm(x, w):
    return x * lax.rsqrt(jnp.mean(x * x, axis=-1, keepdims=True) + EPS) * w


def _gated_norm(o, z, w):
    return o * lax.rsqrt(jnp.mean(o * o, axis=-1, keepdims=True) + EPS) * w * _silu(z)


def _lru_gates(xc, wa, ba, wx, bx, lam):
    r = _sigmoid(_NN_B(xc, wa) + ba)
    i = _sigmoid(_NN_B(xc, wx) + bx)
    log_a = -LRU_C * r * _softplus(-lam)
    a = jnp.exp(log_a)
    mult = jnp.sqrt(-_expm1(2.0 * log_a))
    return a, mult * (i * xc)


def _scan_forward(a, b, h0):
    rows = a.shape[0]
    row = lax.broadcasted_iota(jnp.int32, a.shape, 0)
    k = 1
    while k < rows:
        seen = row >= k
        b = jnp.where(seen, a * pltpu.roll(b, k, 0) + b, b)
        a = jnp.where(seen, a * pltpu.roll(a, k, 0), a)
        k *= 2
    return b + a * h0


def _scan_reverse(a, d, carry):
    rows = a.shape[0]
    row = lax.broadcasted_iota(jnp.int32, a.shape, 0)
    last = row == rows - 1
    c = jnp.where(last, 0.0, pltpu.roll(a, rows - 1, 0))
    d = d + jnp.where(last, carry, 0.0)
    k = 1
    while k < rows:
        seen = row < rows - k
        d = jnp.where(seen, d + c * pltpu.roll(d, rows - k, 0), d)
        c = jnp.where(seen, c * pltpu.roll(c, rows - k, 0), c)
        k *= 2
    return d


def _lane_pick(row, lane_index):
    lane = lax.broadcasted_iota(jnp.int32, row.shape, 1)
    return jnp.sum(jnp.where(lane == lane_index, row, 0.0), axis=-1, keepdims=True)


def _dn_prep(qc, kc, vc, ba, a_log_row, dt_row, head):
    q = _silu(qc)
    k = _silu(kc)
    v = _silu(vc)
    q = q * lax.rsqrt(jnp.sum(q * q, axis=-1, keepdims=True) + EPS) * (HEAD_DIM ** -0.5)
    k = k * lax.rsqrt(jnp.sum(k * k, axis=-1, keepdims=True) + EPS)
    beta = _sigmoid(_lane_pick(ba, head))
    g = -jnp.exp(_lane_pick(a_log_row, head)) * _softplus(_lane_pick(ba, HEADS + head) + _lane_pick(dt_row, head))
    return q, k, v, g, beta


def _dn_chunks_head(q, k, v, gcol, bcol):
    n, c, d = q.shape
    row = lax.broadcasted_iota(jnp.int32, (n, c, c), 1)
    col = lax.broadcasted_iota(jnp.int32, (n, c, c), 2)
    g_wide = jnp.broadcast_to(gcol, (n, c, d))
    b_wide = jnp.broadcast_to(bcol, (n, c, d))
    gc = _chunk_cumsum(g_wide)
    gc_rows = gc[:, :, :c]
    decay = jnp.exp(jnp.where(row >= col, gc_rows - jnp.swapaxes(gc_rows, 1, 2), -1e30))
    kb = k * b_wide
    eg = jnp.exp(gc)
    a = jnp.where(row > col, _BNT(kb, k) * decay, 0.0)
    tinv = _unit_lower_inverse(a)
    u = _BNN(tinv, v * b_wide)
    w = _BNN(tinv, kb * eg)
    attn = _BNT(q, k) * decay
    g_last = jnp.sum(g_wide, axis=1, keepdims=True)
    return u, w, attn, q * eg, k * jnp.exp(g_last - gc), jnp.exp(g_last)


def _dn_chunks(q, k, v, gcol, bcol, states):
    u, w, attn, qe, kdec, eglast = _dn_chunks_head(q, k, v, gcol, bcol)
    v_new = u - _BNN(w, states)
    o = _BNN(qe, states) + _BNN(attn, v_new)
    return (o, states * eglast + _BTN(kdec, v_new)), (w, attn, qe, kdec, eglast)


def _conv_taps(buf, head, cw, rows):
    acc = cw[0:1, :] * buf[head, pl.ds(5, rows), :]
    for j in range(1, 4):
        acc = acc + cw[j:j + 1, :] * buf[head, pl.ds(5 + j, rows), :]
    return acc


def _conv_backward(dbuf, dhead, xbuf, xhead, cw, dxc, rows):
    dbuf[dhead, pl.ds(0, rows), :] = dxc
    dx = cw[0:1, :] * dbuf[dhead, pl.ds(3, rows), :]
    for j in range(1, 4):
        dx = dx + cw[j:j + 1, :] * dbuf[dhead, pl.ds(3 - j, rows), :]
    dcw = jnp.concatenate(
        [jnp.sum(dxc * xbuf[xhead, pl.ds(5 + j, rows), :], axis=0, keepdims=True) for j in range(4)], axis=0)
    dbuf[dhead, pl.ds(rows, 8), :] = dbuf[dhead, pl.ds(0, 8), :]
    return dx, dcw


def _params(**kw):
    return pltpu.CompilerParams(vmem_limit_bytes=VMEM_LIMIT, **kw)


def _matmul(a, b, form, tm, tn, tk, name, add=None, out_dtype=F32, dep=None):
    if form == "nn":
        (m, kdim), (_, n) = a.shape, b.shape
        a_spec = pl.BlockSpec((tm, tk), lambda j, i, k: (i, k))
        b_spec = pl.BlockSpec((tk, tn), lambda j, i, k: (k, j))
        dims = NN
    elif form == "nt":
        (m, kdim), (n, _) = a.shape, b.shape
        a_spec = pl.BlockSpec((tm, tk), lambda j, i, k: (i, k))
        b_spec = pl.BlockSpec((tn, tk), lambda j, i, k: (j, k))
        dims = NT
    else:
        (kdim, m), (_, n) = a.shape, b.shape
        a_spec = pl.BlockSpec((tk, tm), lambda j, i, k: (k, i))
        b_spec = pl.BlockSpec((tk, tn), lambda j, i, k: (k, j))
        dims = TN
    assert m % tm == 0 and n % tn == 0 and kdim % tk == 0, (name, m, n, kdim, tm, tn, tk)
    ksteps = kdim // tk
    o_spec = pl.BlockSpec((tm, tn), lambda j, i, k: (i, j))
    has_add = add is not None
    extra = [] if dep is None else [dep]

    def body(*refs):
        a_ref, b_ref = refs[:2]
        c_ref = refs[2] if has_add else None
        o_ref, acc = refs[-2:]
        k = pl.program_id(2)

        @pl.when(k == 0)
        def _():
            acc[...] = c_ref[...] if has_add else jnp.zeros_like(acc)

        acc[...] += lax.dot_general(a_ref[...].astype(BF16), b_ref[...].astype(BF16), dims,
                                    preferred_element_type=F32)

        @pl.when(k == ksteps - 1)
        def _():
            o_ref[...] = acc[...].astype(o_ref.dtype)

    in_specs = [a_spec, b_spec] + ([o_spec] if has_add else []) + [pl.BlockSpec((8, HEAD_DIM), lambda j, i, k: (0, 0))
                                                                   for _ in extra]
    args = (a, b) + ((add,) if has_add else ()) + tuple(extra)
    return pl.pallas_call(
        body, name=name, grid=(n // tn, m // tm, ksteps), in_specs=in_specs, out_specs=o_spec,
        out_shape=jax.ShapeDtypeStruct((m, n), out_dtype), scratch_shapes=[pltpu.VMEM((tm, tn), F32)],
        compiler_params=_params(dimension_semantics=("parallel", "parallel", "arbitrary")),
    )(*args)


def _rmsnorm_fwd(x, w_row, name):
    s = x.shape[0]
    tb = min(TIME_BLOCK, s)

    def body(x_ref, w_ref, o_ref):
        o_ref[...] = _rmsnorm(x_ref[...], w_ref[...]).astype(BF16)

    return pl.pallas_call(
        body, name=name, grid=(s // tb,),
        in_specs=[pl.BlockSpec((tb, D_MODEL), lambda i: (i, 0)), pl.BlockSpec((1, D_MODEL), lambda i: (0, 0))],
        out_specs=pl.BlockSpec((tb, D_MODEL), lambda i: (i, 0)),
        out_shape=jax.ShapeDtypeStruct((s, D_MODEL), BF16), compiler_params=_params(),
    )(x, w_row)


def _rmsnorm_bwd(x, w_row, dh, dres, name):
    s = x.shape[0]
    tb = min(TIME_BLOCK, s)

    def body(x_ref, w_ref, dh_ref, dres_ref, dx_ref, dw_ref):
        _, vjp = jax.vjp(_rmsnorm, x_ref[...], w_ref[...])
        dx, dw = vjp(dh_ref[...])
        dx_ref[...] = dres_ref[...] + dx

        @pl.when(pl.program_id(0) == 0)
        def _():
            dw_ref[...] = jnp.zeros_like(dw_ref)

        dw_ref[...] += dw

    row = pl.BlockSpec((tb, D_MODEL), lambda i: (i, 0))
    vec = pl.BlockSpec((1, D_MODEL), lambda i: (0, 0))
    return pl.pallas_call(
        body, name=name, grid=(s // tb,), in_specs=[row, vec, row, row], out_specs=[row, vec],
        out_shape=[jax.ShapeDtypeStruct((s, D_MODEL), F32), jax.ShapeDtypeStruct((1, D_MODEL), F32)],
        compiler_params=_params(),
    )(x, w_row, dh, dres)


def _final_loss(x, w_row, target, name):
    s = x.shape[0]
    tb = min(TIME_BLOCK, s)

    def loss_fn(xv, wv, tv):
        err = _rmsnorm(xv, wv) - tv
        return 0.5 * jnp.sum(jnp.sum(err * err, axis=-1, keepdims=True), axis=0, keepdims=True) * (1.0 / D_MODEL)

    def body(x_ref, w_ref, t_ref, loss_ref, dx_ref, dw_ref):
        tv = t_ref[...]
        loss, vjp = jax.vjp(lambda xv, wv: loss_fn(xv, wv, tv), x_ref[...], w_ref[...])
        dx, dw = vjp(jnp.ones((1, 1), F32))
        dx_ref[...] = dx

        @pl.when(pl.program_id(0) == 0)
        def _():
            dw_ref[...] = jnp.zeros_like(dw_ref)
            loss_ref[...] = jnp.zeros_like(loss_ref)

        dw_ref[...] += dw
        loss_ref[...] += jnp.broadcast_to(loss, loss_ref.shape)

    row = pl.BlockSpec((tb, D_MODEL), lambda i: (i, 0))
    vec = pl.BlockSpec((1, D_MODEL), lambda i: (0, 0))
    return pl.pallas_call(
        body, name=name, grid=(s // tb,), in_specs=[row, vec, row],
        out_specs=[pl.BlockSpec((1, HEAD_DIM), lambda i: (0, 0)), row, vec],
        out_shape=[jax.ShapeDtypeStruct((1, HEAD_DIM), F32), jax.ShapeDtypeStruct((s, D_MODEL), F32),
                   jax.ShapeDtypeStruct((1, D_MODEL), F32)],
        compiler_params=_params(),
    )(x, w_row, target)


def _head_specs(tb, time_of):
    def col(off):
        return pl.BlockSpec((tb, HEAD_DIM), lambda t, h: (time_of(t), off + h))
    return col


def _vec_spec():
    return pl.BlockSpec((1, HEAD_DIM), lambda t, h: (0, h))


def _lru_fwd(proj, conv_w, conv_b, wa, ba, wx, bx, lam, nw, name):
    s = proj.shape[0]
    tb = min(TIME_BLOCK, s)
    nt = s // tb
    col = _head_specs(tb, lambda t: t)

    def body(x_ref, z_ref, cw_ref, cb_ref, wa_ref, ba_ref, wx_ref, bx_ref, lam_ref, nw_ref,
             y_ref, hs_ref, xbuf, hcar):
        t, h = pl.program_id(0), pl.program_id(1)

        @pl.when(t == 0)
        def _():
            xbuf[h, pl.ds(0, 8), :] = jnp.zeros((8, HEAD_DIM), F32)
            hcar[h] = jnp.zeros((8, HEAD_DIM), F32)

        xbuf[h, pl.ds(8, tb), :] = x_ref[...]
        xc = _conv_taps(xbuf, h, cw_ref[...], tb) + cb_ref[...]
        a, b = _lru_gates(xc, wa_ref[...], ba_ref[...], wx_ref[...], bx_ref[...], lam_ref[...])
        hs_ref[...] = _scan_forward(a, b, hcar[h, pl.ds(0, 1), :])
        hcar[h, pl.ds(0, 1), :] = hs_ref[pl.ds(tb - 1, 1), :]
        xbuf[h, pl.ds(0, 8), :] = xbuf[h, pl.ds(tb, 8), :]
        y_ref[...] = _gated_norm(hs_ref[...], z_ref[...], nw_ref[...]).astype(BF16)

    vec = _vec_spec()
    return pl.pallas_call(
        body, name=name, grid=(nt, HEADS),
        in_specs=[col(COL_LRU_X), col(COL_LRU_Z), pl.BlockSpec((4, HEAD_DIM), lambda t, h: (0, h)), vec,
                  pl.BlockSpec((None, HEAD_DIM, HEAD_DIM), lambda t, h: (h, 0, 0)), vec,
                  pl.BlockSpec((None, HEAD_DIM, HEAD_DIM), lambda t, h: (h, 0, 0)), vec, vec, vec],
        out_specs=[col(0), col(0)],
        out_shape=[jax.ShapeDtypeStruct((s, D_MODEL), BF16), jax.ShapeDtypeStruct((s, D_MODEL), F32)],
        scratch_shapes=[pltpu.VMEM((HEADS, tb + 8, HEAD_DIM), F32), pltpu.VMEM((HEADS, 8, HEAD_DIM), F32)],
        compiler_params=_params(dimension_semantics=("arbitrary", "arbitrary")),
    )(proj, proj, conv_w, conv_b, wa, ba, wx, bx, lam, nw)


def _halo_spec(tb, nt, off):
    per = tb // 8
    return pl.BlockSpec((8, HEAD_DIM), lambda t, h: (jnp.maximum((nt - 1 - t) * per - 1, 0), off + h))


def _lru_bwd(proj, hs, dy, conv_w, conv_b, wa, ba, wx, bx, lam, nw, name):
    s = proj.shape[0]
    tb = min(TIME_BLOCK, s)
    nt = s // tb
    col = _head_specs(tb, lambda t: nt - 1 - t)

    def body(x_ref, xh_ref, z_ref, hs_ref, hh_ref, dy_ref, cw_ref, cb_ref, wa_ref, ba_ref, wx_ref, bx_ref,
             lam_ref, nw_ref, dx_ref, dz_ref, dcw_ref, dcb_ref, dwa_ref, dba_ref, dwx_ref, dbx_ref, dlam_ref,
             dnw_ref, xbuf, hbuf, dbuf, gcar):
        t, h = pl.program_id(0), pl.program_id(1)
        first_block = t == nt - 1

        @pl.when(t == 0)
        def _():
            dbuf[h, pl.ds(tb, 8), :] = jnp.zeros((8, HEAD_DIM), F32)
            gcar[h] = jnp.zeros((8, HEAD_DIM), F32)
            dcw_ref[h] = jnp.zeros((4, HEAD_DIM), F32)
            dwa_ref[h] = jnp.zeros((HEAD_DIM, HEAD_DIM), F32)
            dwx_ref[h] = jnp.zeros((HEAD_DIM, HEAD_DIM), F32)
            for ref in (dcb_ref, dba_ref, dbx_ref, dlam_ref, dnw_ref):
                ref[h] = jnp.zeros((1, HEAD_DIM), F32)

        keep = jnp.where(first_block, 0.0, 1.0)
        xbuf[0, pl.ds(0, 8), :] = xh_ref[...] * keep
        xbuf[0, pl.ds(8, tb), :] = x_ref[...]
        hbuf[pl.ds(0, 8), :] = hh_ref[...] * keep
        hbuf[pl.ds(8, tb), :] = hs_ref[...]
        cw = cw_ref[...]
        xc = _conv_taps(xbuf, 0, cw, tb) + cb_ref[...]
        (a, _), gates_vjp = jax.vjp(_lru_gates, xc, wa_ref[...], ba_ref[...], wx_ref[...], bx_ref[...], lam_ref[...])
        _, norm_vjp = jax.vjp(_gated_norm, hs_ref[...], z_ref[...], nw_ref[...])
        dh, dz, dnw = norm_vjp(dy_ref[...])
        dz_ref[...] = dz.astype(dz_ref.dtype)
        g = _scan_reverse(a, dh, gcar[h, pl.ds(0, 1), :])
        gcar[h, pl.ds(0, 1), :] = a[0:1, :] * g[0:1, :]
        dxc, dwa, dba, dwx, dbx, dlam = gates_vjp((g * hbuf[pl.ds(7, tb), :], g))
        dx, dcw = _conv_backward(dbuf, h, xbuf, 0, cw, dxc, tb)
        dx_ref[...] = dx.astype(dx_ref.dtype)
        dcw_ref[h] += dcw
        dcb_ref[h] += jnp.sum(dxc, axis=0, keepdims=True)
        dwa_ref[h] += dwa
        dwx_ref[h] += dwx
        dba_ref[h] += dba
        dbx_ref[h] += dbx
        dlam_ref[h] += dlam
        dnw_ref[h] += dnw

    vec = _vec_spec()
    mat = pl.BlockSpec((None, HEAD_DIM, HEAD_DIM), lambda t, h: (h, 0, 0))

    def whole(shape):
        return pl.BlockSpec(shape, lambda t, h: (0,) * len(shape))

    head_vec = jax.ShapeDtypeStruct((HEADS, 1, HEAD_DIM), F32)
    head_mat = jax.ShapeDtypeStruct((HEADS, HEAD_DIM, HEAD_DIM), F32)
    return pl.pallas_call(
        body, name=name, grid=(nt, HEADS),
        in_specs=[col(COL_LRU_X), _halo_spec(tb, nt, COL_LRU_X), col(COL_LRU_Z), col(0), _halo_spec(tb, nt, 0), col(0),
                  pl.BlockSpec((4, HEAD_DIM), lambda t, h: (0, h)), vec, mat, vec, mat, vec, vec, vec],
        out_specs=[col(0), col(0), whole((HEADS, 4, HEAD_DIM)), whole((HEADS, 1, HEAD_DIM)),
                   whole((HEADS, HEAD_DIM, HEAD_DIM)), whole((HEADS, 1, HEAD_DIM)),
                   whole((HEADS, HEAD_DIM, HEAD_DIM)), whole((HEADS, 1, HEAD_DIM)), whole((HEADS, 1, HEAD_DIM)),
                   whole((HEADS, 1, HEAD_DIM))],
        out_shape=[jax.ShapeDtypeStruct((s, D_MODEL), BF16), jax.ShapeDtypeStruct((s, D_MODEL), BF16),
                   jax.ShapeDtypeStruct((HEADS, 4, HEAD_DIM), F32), head_vec, head_mat, head_vec, head_mat, head_vec,
                   head_vec, head_vec],
        scratch_shapes=[pltpu.VMEM((1, tb + 8, HEAD_DIM), F32), pltpu.VMEM((tb + 8, HEAD_DIM), F32),
                        pltpu.VMEM((HEADS, tb + 8, HEAD_DIM), F32), pltpu.VMEM((HEADS, 8, HEAD_DIM), F32)],
        compiler_params=_params(dimension_semantics=("arbitrary", "arbitrary")),
    )(proj, proj, proj, hs, hs, dy, conv_w, conv_b, wa, ba, wx, bx, lam, nw)


def _dn_fwd(proj, conv_w, a_log_row, dt_row, nw, name):
    s = proj.shape[0]
    tb = min(TIME_BLOCK, s)
    nt = s // tb
    nchunk = tb // CHUNK
    col = _head_specs(tb, lambda t: t)

    def body(q_ref, k_ref, v_ref, z_ref, ba_ref, cwq_ref, cwk_ref, cwv_ref, al_ref, dt_ref, nw_ref,
             y_ref, o_ref, st_ref, xbuf, state):
        t, h = pl.program_id(0), pl.program_id(1)

        @pl.when(t == 0)
        def _():
            for i in range(3):
                xbuf[3 * h + i, pl.ds(0, 8), :] = jnp.zeros((8, HEAD_DIM), F32)
            state[h] = jnp.zeros((HEAD_DIM, HEAD_DIM), F32)

        conv = []
        for i, (ref, cw_ref) in enumerate(((q_ref, cwq_ref), (k_ref, cwk_ref), (v_ref, cwv_ref))):
            xbuf[3 * h + i, pl.ds(8, tb), :] = ref[...]
            conv.append(_conv_taps(xbuf, 3 * h + i, cw_ref[...], tb))
            xbuf[3 * h + i, pl.ds(0, 8), :] = xbuf[3 * h + i, pl.ds(tb, 8), :]
        q, k, v, g, beta = _dn_prep(conv[0], conv[1], conv[2], ba_ref[...], al_ref[...], dt_ref[...], h)
        def chunks(a):
            return a.reshape(nchunk, CHUNK, a.shape[-1])

        u, w, attn, qe, kdec, eglast = _dn_chunks_head(chunks(q), chunks(k), chunks(v), chunks(g), chunks(beta))
        st = state[h]
        for c in range(nchunk):
            st_ref[c] = st
            v_new = u[c] - _NN_B(w[c], st)
            o_ref[pl.ds(c * CHUNK, CHUNK), :] = _NN_B(qe[c], st) + _NN_B(attn[c], v_new)
            st = st * eglast[c] + _TN_B(kdec[c], v_new)
        state[h] = st
        y_ref[...] = _gated_norm(o_ref[...], z_ref[...], nw_ref[...]).astype(BF16)

    def cw_spec(off):
        return pl.BlockSpec((4, HEAD_DIM), lambda t, h: (0, off + h))

    row128 = pl.BlockSpec((1, HEAD_DIM), lambda t, h: (0, 0))
    return pl.pallas_call(
        body, name=name, grid=(nt, HEADS),
        in_specs=[col(COL_Q), col(COL_K), col(COL_V), col(COL_DN_Z),
                  pl.BlockSpec((tb, HEAD_DIM), lambda t, h: (t, COL_BA)),
                  cw_spec(0), cw_spec(HEADS), cw_spec(2 * HEADS), row128, row128, row128],
        out_specs=[col(0), col(0), pl.BlockSpec((None, nchunk, HEAD_DIM, HEAD_DIM), lambda t, h: (h, t, 0, 0))],
        out_shape=[jax.ShapeDtypeStruct((s, D_MODEL), BF16), jax.ShapeDtypeStruct((s, D_MODEL), F32),
                   jax.ShapeDtypeStruct((HEADS, s // CHUNK, HEAD_DIM, HEAD_DIM), F32)],
        scratch_shapes=[pltpu.VMEM((3 * HEADS, tb + 8, HEAD_DIM), F32), pltpu.VMEM((HEADS, HEAD_DIM, HEAD_DIM), F32)],
        compiler_params=_params(dimension_semantics=("arbitrary", "arbitrary")),
    )(proj, proj, proj, proj, proj, conv_w, conv_w, conv_w, a_log_row, dt_row, nw)


def _dn_bwd(proj, o, states, dy, conv_w, a_log_row, dt_row, nw, name):
    s = proj.shape[0]
    tb = min(TIME_BLOCK, s)
    nt = s // tb
    nchunk = tb // CHUNK
    col = _head_specs(tb, lambda t: nt - 1 - t)

    def body(q_ref, qh_ref, k_ref, kh_ref, v_ref, vh_ref, z_ref, ba_ref, o_ref, st_ref, dy_ref,
             cwq_ref, cwk_ref, cwv_ref, al_ref, dt_ref, nw_ref,
             dq_ref, dk_ref, dv_ref, dz_ref, dba_ref, dcw_ref, dal_ref, ddt_ref, dnw_ref,
             xbuf, dbuf, dstate, dst_s):
        t, h = pl.program_id(0), pl.program_id(1)
        first_block = t == nt - 1

        @pl.when(t == 0)
        def _():
            for i in range(3):
                dbuf[3 * h + i, pl.ds(tb, 8), :] = jnp.zeros((8, HEAD_DIM), F32)
                dcw_ref[3 * h + i] = jnp.zeros((4, HEAD_DIM), F32)
            dstate[h] = jnp.zeros((HEAD_DIM, HEAD_DIM), F32)

        @pl.when((t == 0) & (h == 0))
        def _():
            for ref in (dal_ref, ddt_ref, dnw_ref):
                ref[...] = jnp.zeros_like(ref)

        keep = jnp.where(first_block, 0.0, 1.0)
        cws = (cwq_ref[...], cwk_ref[...], cwv_ref[...])
        conv = []
        for i, (ref, halo) in enumerate(((q_ref, qh_ref), (k_ref, kh_ref), (v_ref, vh_ref))):
            xbuf[i, pl.ds(0, 8), :] = halo[...] * keep
            xbuf[i, pl.ds(8, tb), :] = ref[...]
            conv.append(_conv_taps(xbuf, i, cws[i], tb))
        (q, k, v, g, beta), prep_vjp = jax.vjp(
            lambda qc, kc, vc, ba, al, dt: _dn_prep(qc, kc, vc, ba, al, dt, h),
            conv[0], conv[1], conv[2], ba_ref[...], al_ref[...], dt_ref[...])
        _, norm_vjp = jax.vjp(_gated_norm, o_ref[...], z_ref[...], nw_ref[...])
        do, dz, dnw = norm_vjp(dy_ref[...])
        dz_ref[...] = dz.astype(dz_ref.dtype)
        dnw_ref[...] += dnw

        def chunks(a):
            return a.reshape(nchunk, CHUNK, a.shape[-1])

        do = chunks(do)
        _, chunks_vjp, (w, attn, qe, kdec, eglast) = jax.vjp(
            _dn_chunks, chunks(q), chunks(k), chunks(v), chunks(g), chunks(beta), st_ref[...], has_aux=True)
        from_o = _dot(attn, do, B_TN, "bf16")
        from_qe = _dot(qe, do, B_TN, "bf16")
        dst = dstate[h]
        for c in reversed(range(nchunk)):
            dst_s[c] = dst
            dv_new = from_o[c] + _dot(kdec[c], dst, NN, "bf16")
            dst = dst * eglast[c] + from_qe[c] - _dot(w[c], dv_new, TN, "bf16")
        dstate[h] = dst
        dq, dk, dv, dg, db, _ = chunks_vjp((do, dst_s[...]))

        def rows(a):
            return a.reshape(tb, a.shape[-1])

        dqc, dkc, dvc, dba, dal, ddt = prep_vjp((rows(dq), rows(dk), rows(dv), rows(dg), rows(db)))
        for i, (dxc, out) in enumerate(((dqc, dq_ref), (dkc, dk_ref), (dvc, dv_ref))):
            dx, dcw = _conv_backward(dbuf, 3 * h + i, xbuf, i, cws[i], dxc, tb)
            out[...] = dx.astype(out.dtype)
            dcw_ref[3 * h + i] += dcw
        dal_ref[...] += dal
        ddt_ref[...] += ddt

        @pl.when(h == 0)
        def _():
            dba_ref[...] = dba.astype(dba_ref.dtype)

        @pl.when(h > 0)
        def _():
            dba_ref[...] += dba.astype(dba_ref.dtype)

    def cw_spec(off):
        return pl.BlockSpec((4, HEAD_DIM), lambda t, h: (0, off + h))

    def whole(shape):
        return pl.BlockSpec(shape, lambda t, h: (0,) * len(shape))

    row128 = whole((1, HEAD_DIM))
    blk = (tb, HEAD_DIM)
    act = jax.ShapeDtypeStruct((s, D_MODEL), BF16)
    row_out = jax.ShapeDtypeStruct((1, HEAD_DIM), F32)
    return pl.pallas_call(
        body, name=name, grid=(nt, HEADS),
        in_specs=[col(COL_Q), _halo_spec(tb, nt, COL_Q), col(COL_K), _halo_spec(tb, nt, COL_K),
                  col(COL_V), _halo_spec(tb, nt, COL_V), col(COL_DN_Z),
                  pl.BlockSpec(blk, lambda t, h: (nt - 1 - t, COL_BA)), col(0),
                  pl.BlockSpec((None, nchunk, HEAD_DIM, HEAD_DIM), lambda t, h: (h, nt - 1 - t, 0, 0)), col(HEADS),
                  cw_spec(0), cw_spec(HEADS), cw_spec(2 * HEADS), row128, row128, row128],
        out_specs=[col(0), col(0), col(0), col(0), pl.BlockSpec(blk, lambda t, h: (nt - 1 - t, 0)),
                   whole((3 * HEADS, 4, HEAD_DIM)), row128, row128, row128],
        out_shape=[act, act, act, act, jax.ShapeDtypeStruct((s, HEAD_DIM), F32),
                   jax.ShapeDtypeStruct((3 * HEADS, 4, HEAD_DIM), F32), row_out, row_out, row_out],
        scratch_shapes=[pltpu.VMEM((3, tb + 8, HEAD_DIM), F32), pltpu.VMEM((3 * HEADS, tb + 8, HEAD_DIM), F32),
                        pltpu.VMEM((HEADS, HEAD_DIM, HEAD_DIM), F32), pltpu.VMEM((nchunk, HEAD_DIM, HEAD_DIM), F32)],
        compiler_params=_params(dimension_semantics=("arbitrary", "arbitrary")),
    )(proj, proj, proj, proj, proj, proj, proj, proj, o, states, dy, conv_w, conv_w, conv_w, a_log_row, dt_row, nw)


def _mesh_position():
    x, y, c = lax.axis_index("x"), lax.axis_index("y"), lax.axis_index("c")
    return x, y, c, 4 * x + 2 * y + c


def _peer(k, x, y, c):
    px = 1 - x if k & 4 else x
    py = 1 - y if k & 2 else y
    pc = 1 - c if k & 1 else c
    return (px, py, pc), 4 * px + 2 * py + pc


def _exchange_copies(ins, lands, scatter, send_sems, recv_sems):
    x, y, c, me = _mesh_position()
    sends, recvs = [], []
    for i, (src, land) in enumerate(zip(ins, lands)):
        for k in range(1, N_DEV):
            peer, peer_id = _peer(k, x, y, c)
            sem = i * (N_DEV - 1) + k - 1
            for dst, out in ((me, sends), (peer_id, recvs)):
                out.append(pltpu.make_async_remote_copy(
                    src_ref=src.at[peer_id] if scatter[i] else src, dst_ref=land.at[dst],
                    send_sem=send_sems.at[sem], recv_sem=recv_sems.at[sem],
                    device_id=peer, device_id_type=pl.DeviceIdType.MESH))
    return sends, recvs


def _landing_shape(a, scatter):
    return a.shape if scatter else (N_DEV,) + a.shape


def _direct_exchange(arrays, scatter, name):
    n = len(arrays)
    out_shapes = [jax.ShapeDtypeStruct(_landing_shape(a, sc), a.dtype) for a, sc in zip(arrays, scatter)]

    def body(*refs):
        ins, outs = refs[:n], refs[n:2 * n]
        send_sems, recv_sems, local_sems = refs[2 * n:]
        me = _mesh_position()[3]
        local = [pltpu.make_async_copy(ins[i].at[me] if scatter[i] else ins[i], outs[i].at[me], local_sems.at[i])
                 for i in range(n)]
        sends, recvs = _exchange_copies(ins, outs, scatter, send_sems, recv_sems)
        for cp in local + sends:
            cp.start()
        for cp in recvs:
            cp.wait_recv()
        for cp in sends:
            cp.wait_send()
        for cp in local:
            cp.wait()

    hbm = pl.BlockSpec(memory_space=pl.ANY)
    return pl.pallas_call(
        body, name=name, in_specs=[hbm] * n, out_specs=[hbm] * n, out_shape=out_shapes,
        scratch_shapes=[pltpu.SemaphoreType.DMA((n * (N_DEV - 1),)), pltpu.SemaphoreType.DMA((n * (N_DEV - 1),)),
                        pltpu.SemaphoreType.DMA((n,))],
    )(*arrays)


_HBM = pl.BlockSpec(memory_space=pltpu.HBM)
_SEM = pl.BlockSpec(memory_space=pltpu.SEMAPHORE)
_DATAFLOW = pltpu.SideEffectType.DATAFLOW_SIDE_EFFECTING


def _exchange_start(arrays, scatter, name):
    n = len(arrays)
    srcs = [pltpu.with_memory_space_constraint(a, pltpu.HBM) for a in arrays]
    lands = [pltpu.with_memory_space_constraint(lax.empty(_landing_shape(a, sc), a.dtype), pltpu.HBM)
             for a, sc in zip(arrays, scatter)]
    nsem = n * (N_DEV - 1)

    def body(*refs):
        ins, zones = refs[:n], refs[n:2 * n]
        send_sems, recv_sems = refs[2 * n], refs[2 * n + 1]
        token = refs[-1]
        sends, _ = _exchange_copies(ins, zones, scatter, send_sems, recv_sems)
        for cp in sends:
            cp.start()
        token[...] = jnp.zeros_like(token)

    res = pl.pallas_call(
        body, name=name,
        out_shape=(pltpu.SemaphoreType.DMA((nsem,)), pltpu.SemaphoreType.DMA((nsem,)),
                   *[pltpu.HBM(a.shape, a.dtype) for a in srcs + lands], jax.ShapeDtypeStruct((8, HEAD_DIM), F32)),
        in_specs=[_HBM] * (2 * n),
        out_specs=(_SEM, _SEM, *[_HBM] * (2 * n), pl.BlockSpec(memory_space=pltpu.VMEM)),
        input_output_aliases={i: 2 + i for i in range(2 * n)},
        compiler_params=pltpu.CompilerParams(has_side_effects=_DATAFLOW),
    )(*srcs, *lands)
    return dict(sems=res[:2], srcs=res[2:2 + n], lands=res[2 + n:2 + 2 * n], token_block=res[-1],
                token=res[-1][0, 0], scatter=scatter)


def _exchange_wait(started, after, name):
    scatter = started["scatter"]
    n = len(scatter)

    def body(*refs):
        ins, zones = refs[:n], refs[n:2 * n]
        send_sems, recv_sems = refs[2 * n], refs[2 * n + 1]
        sends, recvs = _exchange_copies(ins, zones, scatter, send_sems, recv_sems)
        for cp in sends:
            cp.wait_send()
        for cp in recvs:
            cp.wait_recv()

    thru = list(started["srcs"]) + list(started["lands"])
    res = pl.pallas_call(
        body, name=name, out_shape=[pltpu.HBM(a.shape, a.dtype) for a in thru],
        in_specs=[_HBM] * (2 * n) + [_SEM, _SEM, pl.BlockSpec(memory_space=pl.ANY)], out_specs=[_HBM] * (2 * n),
        input_output_aliases={i: i for i in range(2 * n)},
        compiler_params=pltpu.CompilerParams(has_side_effects=_DATAFLOW),
    )(*thru, *started["sems"], after)
    me = 4 * lax.axis_index("x") + 2 * lax.axis_index("y") + lax.axis_index("c")
    out = []
    for src, got, sc in zip(res[:n], res[n:], scatter):
        own = lax.dynamic_index_in_dim(src, me, 0, keepdims=False) if sc else src
        out.append(lax.dynamic_update_index_in_dim(got, own, me, 0))
    return out


def _adamw(parts, w, m, v, name, rows_per_step, row_offset=0, into=None):
    rows, cols = parts.shape[1:]
    tr = min(rows_per_step, rows)
    assert rows % tr == 0 and row_offset % tr == 0, (name, rows, tr, row_offset)
    first = row_offset // tr
    c1 = 1.0 / (1.0 - ADAM_B1 ** ADAM_STEP)
    c2 = 1.0 / (1.0 - ADAM_B2 ** ADAM_STEP)

    def body(p_ref, w_ref, m_ref, v_ref, *rest):
        g_ref, d_ref, nm_ref, nv_ref = rest[-4:]
        g = p_ref[0]
        for d in range(1, N_DEV):
            g = g + p_ref[d]
        nm = ADAM_B1 * m_ref[...] + (1.0 - ADAM_B1) * g
        nv = ADAM_B2 * v_ref[...] + (1.0 - ADAM_B2) * (g * g)
        g_ref[...] = g
        nm_ref[...] = nm
        nv_ref[...] = nv
        d_ref[...] = -ADAM_LR * ((nm * c1) / (jnp.sqrt(nv * c2) + ADAM_EPS) + ADAM_WD * w_ref[...])

    blk = pl.BlockSpec((tr, cols), lambda i: (i + first, 0))
    shape = jax.ShapeDtypeStruct(w.shape, F32)
    prior = [] if into is None else list(into)
    return pl.pallas_call(
        body, name=name, grid=(rows // tr,),
        in_specs=[pl.BlockSpec((N_DEV, tr, cols), lambda i: (0, i, 0)), blk, blk, blk]
        + [pl.BlockSpec(memory_space=pl.ANY)] * len(prior),
        out_specs=[blk] * 4, out_shape=[shape] * 4,
        input_output_aliases={4 + j: j for j in range(len(prior))}, compiler_params=_params(),
    )(parts, w, m, v, *prior)


_REPLICATED = ("norm_w", "lru_conv_b", "lru_wa", "lru_ba", "lru_wx", "lru_bx", "lru_lambda", "lru_norm_w",
               "dn_A_log", "dn_dt_bias", "dn_norm_w", "final_norm_w")
_SHARDED = ("w_in", "lru_conv_w", "dn_conv_w", "w_out")
_WEIGHTS = ("norm_w", "w_in", "lru_conv_w", "lru_conv_b", "lru_wa", "lru_ba", "lru_wx", "lru_bx", "lru_lambda",
            "lru_norm_w", "dn_conv_w", "dn_A_log", "dn_dt_bias", "dn_norm_w", "w_out", "final_norm_w")


def _pack_rows(tree):
    rows = []
    for name in _REPLICATED:
        a = tree[name]
        if a.shape[-1] == HEADS:
            a = jnp.pad(a, ((0, 0), (0, HEAD_DIM - HEADS)))
        rows.append(a.reshape(-1, HEAD_DIM))
    packed = jnp.concatenate(rows, axis=0)
    return jnp.pad(packed, ((0, (-packed.shape[0]) % PACK_ROWS), (0, 0)))


def _unpack_rows(packed, like):
    out, at = {}, 0
    for name in _REPLICATED:
        shape = like[name].shape
        if shape[-1] == HEADS:
            n = shape[0]
            out[name] = packed[at:at + n, :HEADS]
        else:
            n = like[name].size // HEAD_DIM
            out[name] = packed[at:at + n].reshape(shape)
        at += n
    return out


def _heads_to_channels(a):
    return jnp.transpose(a, (1, 0, 2)).reshape(a.shape[1], HEADS * HEAD_DIM)


def kernel(x, norm_w, w_in, lru_conv_w, lru_conv_b, lru_wa, lru_ba, lru_wx, lru_bx, lru_lambda, lru_norm_w, dn_conv_w, dn_A_log, dn_dt_bias, dn_norm_w, w_out, final_norm_w, loss_target, m_norm_w, m_w_in, m_lru_conv_w, m_lru_conv_b, m_lru_wa, m_lru_ba, m_lru_wx, m_lru_bx, m_lru_lambda, m_lru_norm_w, m_dn_conv_w, m_dn_A_log, m_dn_dt_bias, m_dn_norm_w, m_w_out, m_final_norm_w, v_norm_w, v_w_in, v_lru_conv_w, v_lru_conv_b, v_lru_wa, v_lru_ba, v_lru_wx, v_lru_bx, v_lru_lambda, v_lru_norm_w, v_dn_conv_w, v_dn_A_log, v_dn_dt_bias, v_dn_norm_w, v_w_out, v_final_norm_w):
    weights = dict(norm_w=norm_w, w_in=w_in, lru_conv_w=lru_conv_w, lru_conv_b=lru_conv_b, lru_wa=lru_wa,
                   lru_ba=lru_ba, lru_wx=lru_wx, lru_bx=lru_bx, lru_lambda=lru_lambda, lru_norm_w=lru_norm_w,
                   dn_conv_w=dn_conv_w, dn_A_log=dn_A_log, dn_dt_bias=dn_dt_bias, dn_norm_w=dn_norm_w,
                   w_out=w_out, final_norm_w=final_norm_w)
    mom_m = dict(norm_w=m_norm_w, w_in=m_w_in, lru_conv_w=m_lru_conv_w, lru_conv_b=m_lru_conv_b, lru_wa=m_lru_wa,
                 lru_ba=m_lru_ba, lru_wx=m_lru_wx, lru_bx=m_lru_bx, lru_lambda=m_lru_lambda,
                 lru_norm_w=m_lru_norm_w, dn_conv_w=m_dn_conv_w, dn_A_log=m_dn_A_log, dn_dt_bias=m_dn_dt_bias,
                 dn_norm_w=m_dn_norm_w, w_out=m_w_out, final_norm_w=m_final_norm_w)
    mom_v = dict(norm_w=v_norm_w, w_in=v_w_in, lru_conv_w=v_lru_conv_w, lru_conv_b=v_lru_conv_b, lru_wa=v_lru_wa,
                 lru_ba=v_lru_ba, lru_wx=v_lru_wx, lru_bx=v_lru_bx, lru_lambda=v_lru_lambda,
                 lru_norm_w=v_lru_norm_w, dn_conv_w=v_dn_conv_w, dn_A_log=v_dn_A_log, dn_dt_bias=v_dn_dt_bias,
                 dn_norm_w=v_dn_norm_w, w_out=v_w_out, final_norm_w=v_final_norm_w)
    depth = norm_w.shape[0]
    xs = x[0]
    s = xs.shape[0]
    tm = min(512, s)

    assert depth >= 2, depth

    def row(a):
        return a.reshape(1, -1)

    def pad_row(a):
        return jnp.pad(a, (0, HEAD_DIM - a.shape[0])).reshape(1, HEAD_DIM)

    def full_w_in(g):
        w = jnp.transpose(g, (1, 2, 0, 3)).reshape(g.shape[1], D_MODEL, D_IN)
        return jnp.pad(w, ((0, 0), (0, 0), (0, D_IN_PAD - D_IN)))

    g_win0, g_lcw, g_dcw = _direct_exchange([w_in[:1].astype(BF16), lru_conv_w, dn_conv_w], [False] * 3,
                                            "gather_first")
    rest = _exchange_start([w_in[1:].astype(BF16), w_out.astype(BF16)], [False] * 2, "gather_rest_start")
    win = [full_w_in(g_win0)[0]]
    wout = None
    lcw = jnp.transpose(g_lcw, (1, 2, 0, 3)).reshape(depth, 4, D_MODEL)
    dcw = jnp.transpose(g_dcw, (1, 2, 0, 3)).reshape(depth, 4, 3 * D_MODEL)

    saved = []
    cur = xs
    for l in range(depth):
        nw_row = row(norm_w[l]) + rest["token"] if l == 0 else row(norm_w[l])
        hn = _rmsnorm_fwd(cur, nw_row, f"norm_fwd_{l}")
        proj = _matmul(hn, win[l], "nn", tm, 896, D_MODEL, f"in_proj_{l}")
        y_lru, hs = _lru_fwd(proj, lcw[l], row(lru_conv_b[l]), lru_wa[l], row(lru_ba[l]), lru_wx[l], row(lru_bx[l]),
                             row(lru_lambda[l]), row(lru_norm_w[l]), f"lru_fwd_{l}")
        y_dn, o_dn, states = _dn_fwd(proj, dcw[l], pad_row(dn_A_log[l]), pad_row(dn_dt_bias[l]), row(dn_norm_w[l]),
                                     f"dn_fwd_{l}")
        ycat = jnp.concatenate([y_lru, y_dn], axis=1)
        if l == 0:
            g_win_rest, g_wout = _exchange_wait(rest, ycat, "gather_rest_wait")
            win += list(full_w_in(g_win_rest))
            wout = jnp.transpose(g_wout, (1, 0, 2, 3)).reshape(depth, 2 * D_MODEL, D_MODEL)
        nxt = _matmul(ycat, wout[l], "nn", tm, D_MODEL, 2 * D_MODEL, f"out_proj_{l}", add=cur)
        saved.append((cur, hn, proj, hs, o_dn, states, ycat))
        cur = nxt
    loss_part, dx, d_final = _final_loss(cur, row(final_norm_w), loss_target[0], "final_loss")

    def win_slots(g):
        return jnp.transpose(g.reshape(D_MODEL, N_DEV, D_IN // N_DEV), (1, 0, 2))

    def wout_slots(g):
        return g.reshape(N_DEV, 2 * D_MODEL // N_DEV, D_MODEL)

    grads = {k: [None] * depth for k in _WEIGHTS if k not in ("final_norm_w", "w_in", "w_out")}
    big = {}
    token = None
    for l in reversed(range(depth)):
        x_in, hn, proj, hs, o_dn, states, ycat = saved[l]
        dy = _matmul(dx, wout[l], "nt", tm, D_MODEL, D_MODEL, f"out_proj_dy_{l}")
        g_wout_l = _matmul(ycat, dx, "tn", D_MODEL, D_MODEL, tm, f"out_proj_dw_{l}")
        cb_row = row(lru_conv_b[l]) if token is None else row(lru_conv_b[l]) + token
        (dlx, dlz, g_lcw, g_lcb, g_wa, g_ba, g_wx, g_bx, g_lam, g_lnw) = _lru_bwd(
            proj, hs, dy, lcw[l], cb_row, lru_wa[l], row(lru_ba[l]), lru_wx[l], row(lru_bx[l]),
            row(lru_lambda[l]), row(lru_norm_w[l]), f"lru_bwd_{l}")
        (dq, dk, dv, ddz, dba, g_dcw3, g_al, g_dt, g_dnw) = _dn_bwd(
            proj, o_dn, states, dy, dcw[l], pad_row(dn_A_log[l]), pad_row(dn_dt_bias[l]), row(dn_norm_w[l]),
            f"dn_bwd_{l}")
        dproj = jnp.concatenate([dlx, dlz, dq, dk, dv, ddz, dba.astype(BF16)], axis=1)
        dh = _matmul(dproj, win[l], "nt", tm, D_MODEL, 896, f"in_proj_dh_{l}")
        dx, g_nw = _rmsnorm_bwd(x_in, row(norm_w[l]), dh, dx, f"norm_bwd_{l}")
        grads["norm_w"][l] = g_nw.reshape(D_MODEL)
        grads["lru_conv_w"][l] = _heads_to_channels(g_lcw)
        grads["lru_conv_b"][l] = g_lcb.reshape(D_MODEL)
        grads["lru_wa"][l] = g_wa
        grads["lru_ba"][l] = g_ba.reshape(D_MODEL)
        grads["lru_wx"][l] = g_wx
        grads["lru_bx"][l] = g_bx.reshape(D_MODEL)
        grads["lru_lambda"][l] = g_lam.reshape(D_MODEL)
        grads["lru_norm_w"][l] = g_lnw.reshape(D_MODEL)
        g_dcw3 = g_dcw3.reshape(HEADS, 3, 4, HEAD_DIM)
        grads["dn_conv_w"][l] = jnp.concatenate([_heads_to_channels(g_dcw3[:, i]) for i in range(3)], axis=1)
        grads["dn_A_log"][l] = g_al[0, :HEADS]
        grads["dn_dt_bias"][l] = g_dt[0, :HEADS]
        grads["dn_norm_w"][l] = g_dnw.reshape(HEAD_DIM)
        if l > 0:
            g_win_l = _matmul(hn, dproj, "tn", D_MODEL, 896, tm, f"in_proj_dw_{l}")[:, :D_IN]
            big[l] = _exchange_start([win_slots(g_win_l), wout_slots(g_wout_l)], [True, True], f"exchange_{l}_start")
            token = big[l]["token"]
    part = {k: jnp.stack(v) for k, v in grads.items()}
    part["final_norm_w"] = d_final.reshape(D_MODEL)

    def conv_slots(a):
        dd, r, cc = a.shape
        return jnp.transpose(a.reshape(dd, r, N_DEV, cc // N_DEV), (2, 0, 1, 3))

    small = _exchange_start(
        [wout_slots(g_wout_l), conv_slots(part["lru_conv_w"]), conv_slots(part["dn_conv_w"]), _pack_rows(part)],
        [True, True, True, False], "exchange_small_start")
    g_win_0 = _matmul(hn, dproj, "tn", D_MODEL, 896, tm, "in_proj_dw_0", dep=small["token_block"])[:, :D_IN]
    last = _exchange_start([win_slots(g_win_0)], [True], "exchange_0_start")

    new = {}
    flat_in = (depth * D_MODEL, D_IN // N_DEV)
    flat_out = (depth * 2 * D_MODEL // N_DEV, D_MODEL)
    acc_in = acc_out = None
    after = last["token_block"]
    for l in reversed(range(1, depth)):
        r_win, r_wout = _exchange_wait(big[l], after, f"exchange_{l}_wait")
        acc_in = _adamw(r_win, w_in.reshape(flat_in), m_w_in.reshape(flat_in), v_w_in.reshape(flat_in),
                        f"adamw_w_in_{l}", 256, l * D_MODEL, acc_in)
        acc_out = _adamw(r_wout, w_out.reshape(flat_out), m_w_out.reshape(flat_out), v_w_out.reshape(flat_out),
                         f"adamw_w_out_{l}", 256, l * flat_out[0] // depth, acc_out)
        after = acc_out[0]
    r_wout, r_lcw, r_dcw, rep_parts = _exchange_wait(small, after, "exchange_small_wait")
    acc_out = _adamw(r_wout, w_out.reshape(flat_out), m_w_out.reshape(flat_out), v_w_out.reshape(flat_out),
                     "adamw_w_out_0", 256, 0, acc_out)
    new["w_out"] = [a.reshape(w_out.shape) for a in acc_out]
    for name, parts in (("lru_conv_w", r_lcw), ("dn_conv_w", r_dcw)):
        w = weights[name]
        flat = (-1, w.shape[-1])
        outs = _adamw(parts.reshape((N_DEV,) + (w.size // w.shape[-1], w.shape[-1])), w.reshape(flat),
                      mom_m[name].reshape(flat), mom_v[name].reshape(flat), f"adamw_{name}", 8)
        new[name] = [a.reshape(w.shape) for a in outs]
    packed = _adamw(rep_parts, _pack_rows(weights), _pack_rows(mom_m), _pack_rows(mom_v), "adamw_small", PACK_ROWS)
    unpacked = [_unpack_rows(a, weights) for a in packed]
    for name in _REPLICATED:
        new[name] = [u[name] for u in unpacked]
    (r_win,) = _exchange_wait(last, packed[0], "exchange_0_wait")
    acc_in = _adamw(r_win, w_in.reshape(flat_in), m_w_in.reshape(flat_in), v_w_in.reshape(flat_in),
                    "adamw_w_in_0", 256, 0, acc_in)
    new["w_in"] = [a.reshape(w_in.shape) for a in acc_in]

    loss = lax.psum(loss_part[0, 0], ("x", "y", "c"))
    out = [loss, dx.reshape(x.shape)]
    for i in range(4):
        out += [new[name][i] for name in _WEIGHTS]
    return tuple(out)
```

```python
import functools

import jax
import jax.numpy as jnp
from jax import lax
from jax.experimental import pallas as pl
from jax.experimental.pallas import tpu as pltpu

F32 = jnp.float32
BF16 = jnp.bfloat16

N_DEV = 8
D_MODEL = 1024
HEADS = 8
HEAD_DIM = 128
CHUNK = 64
D_IN = 6160
D_IN_PAD = 6272
COL_LRU_X, COL_LRU_Z, COL_Q, COL_K, COL_V, COL_DN_Z, COL_BA = 0, 8, 16, 24, 32, 40, 48
LRU_C = 8.0
EPS = 1e-6
ADAM_LR, ADAM_B1, ADAM_B2, ADAM_EPS, ADAM_WD, ADAM_STEP = 0.001, 0.9, 0.999, 1e-08, 0.01, 10
TIME_BLOCK = 512
VMEM_LIMIT = 56 * 1024 * 1024

NN = (((1,), (0,)), ((), ()))
NT = (((1,), (1,)), ((), ()))
TN = (((0,), (0,)), ((), ()))


B_NN = (((2,), (1,)), ((0,), (0,)))
B_NT = (((2,), (2,)), ((0,), (0,)))
B_TN = (((1,), (1,)), ((0,), (0,)))


def _split_bf16(x):
    hi = x.astype(BF16)
    return hi, (x - hi.astype(F32)).astype(BF16)


def _dot(a, b, dims, prec):
    if prec == "bf16":
        return lax.dot_general(a.astype(BF16), b.astype(BF16), dims, preferred_element_type=F32)
    a1, a2 = _split_bf16(a)
    b1, b2 = _split_bf16(b)
    dg = functools.partial(lax.dot_general, dimension_numbers=dims, preferred_element_type=F32)
    return dg(a1, b1) + (dg(a1, b2) + dg(a2, b1))


def _make_mm(prec, nn_dims, nt_dims, tn_dims):
    @jax.custom_vjp
    def nn(a, b):
        return _dot(a, b, nn_dims, prec)

    @jax.custom_vjp
    def nt(a, b):
        return _dot(a, b, nt_dims, prec)

    @jax.custom_vjp
    def tn(a, b):
        return _dot(a, b, tn_dims, prec)

    nn.defvjp(lambda a, b: (_dot(a, b, nn_dims, prec), (a, b)),
              lambda r, g: (_dot(g, r[1], nt_dims, prec), _dot(r[0], g, tn_dims, prec)))
    nt.defvjp(lambda a, b: (_dot(a, b, nt_dims, prec), (a, b)),
              lambda r, g: (_dot(g, r[1], nn_dims, prec), _dot(g, r[0], tn_dims, prec)))
    tn.defvjp(lambda a, b: (_dot(a, b, tn_dims, prec), (a, b)),
              lambda r, g: (_dot(r[1], g, nt_dims, prec), _dot(r[0], g, nn_dims, prec)))
    return nn, nt, tn


_NN_B, _NT_B, _TN_B = _make_mm("bf16", NN, NT, TN)
_BNN, _BNT, _BTN = _make_mm("bf16", B_NN, B_NT, B_TN)


@jax.custom_vjp
def _unit_lower_inverse(a):
    n = a.shape[-1]
    eye = (lax.broadcasted_iota(jnp.int32, a.shape, 1) == lax.broadcasted_iota(jnp.int32, a.shape, 2)).astype(F32)
    inv = eye - a
    pw = _dot(a, a, B_NN, "bf16x3")
    steps = n.bit_length() - 2
    for j in range(steps):
        inv = inv + _dot(inv, pw, B_NN, "bf16x3")
        if j + 1 < steps:
            pw = _dot(pw, pw, B_NN, "bf16x3")
    return inv


def _uli_fwd(a):
    inv = _unit_lower_inverse(a)
    return inv, inv


def _uli_bwd(inv, g):
    return (-_dot(_dot(inv, g, B_TN, "bf16"), inv, B_NT, "bf16"),)


_unit_lower_inverse.defvjp(_uli_fwd, _uli_bwd)


def _lower_ones(batch, n):
    shape = (batch, n, n)
    return (lax.broadcasted_iota(jnp.int32, shape, 1) >= lax.broadcasted_iota(jnp.int32, shape, 2)).astype(BF16)


@jax.custom_vjp
def _chunk_cumsum(g):
    tri = _lower_ones(g.shape[0], g.shape[1])
    g1, g2 = _split_bf16(g)
    g3 = (g - g1.astype(F32) - g2.astype(F32)).astype(BF16)
    dg = functools.partial(lax.dot_general, dimension_numbers=B_NN, preferred_element_type=F32)
    return dg(tri, g1) + (dg(tri, g2) + dg(tri, g3))


def _chunk_cumsum_bwd(_, ct):
    tri = _lower_ones(ct.shape[0], ct.shape[1])
    c1, c2 = _split_bf16(ct)
    dg = functools.partial(lax.dot_general, dimension_numbers=B_TN, preferred_element_type=F32)
    return (dg(tri, c1) + dg(tri, c2),)


_chunk_cumsum.defvjp(lambda g: (_chunk_cumsum(g), None), _chunk_cumsum_bwd)


def _expm1(x):
    small = x * (1.0 + x * (0.5 + x * (1.0 / 6 + x * (1.0 / 24 + x * (1.0 / 120 + x * (1.0 / 720))))))
    return jnp.where(jnp.abs(x) < 0.2, small, jnp.exp(x) - 1.0)


def _sigmoid(x):
    return 1.0 / (1.0 + jnp.exp(-x))


def _silu(x):
    return x * _sigmoid(x)


def _softplus(x):
    return jnp.maximum(x, 0.0) + jnp.log(1.0 + jnp.exp(-jnp.abs(x)))


def _rmsnorm(x, w):
    return x * lax.rsqrt(jnp.mean(x * x, axis=-1, keepdims=True) + EPS) * w


def _gated_norm(o, z, w):
    return o * lax.rsqrt(jnp.mean(o * o, axis=-1, keepdims=True) + EPS) * w * _silu(z)


def _lru_gates(xc, wa, ba, wx, bx, lam):
    r = _sigmoid(_NN_B(xc, wa) + ba)
    i = _sigmoid(_NN_B(xc, wx) + bx)
    log_a = -LRU_C * r * _softplus(-lam)
    a = jnp.exp(log_a)
    mult = jnp.sqrt(-_expm1(2.0 * log_a))
    return a, mult * (i * xc)


def _scan_forward(a, b, h0):
    rows = a.shape[0]
    row = lax.broadcasted_iota(jnp.int32, a.shape, 0)
    k = 1
    while k < rows:
        seen = row >= k
        b = jnp.where(seen, a * pltpu.roll(b, k, 0) + b, b)
        a = jnp.where(seen, a * pltpu.roll(a, k, 0), a)
        k *= 2
    return b + a * h0


def _scan_reverse(a, d, carry):
    rows = a.shape[0]
    row = lax.broadcasted_iota(jnp.int32, a.shape, 0)
    last = row == rows - 1
    c = jnp.where(last, 0.0, pltpu.roll(a, rows - 1, 0))
    d = d + jnp.where(last, carry, 0.0)
    k = 1
    while k < rows:
        seen = row < rows - k
        d = jnp.where(seen, d + c * pltpu.roll(d, rows - k, 0), d)
        c = jnp.where(seen, c * pltpu.roll(c, rows - k, 0), c)
        k *= 2
    return d


def _lane_pick(row, lane_index):
    lane = lax.broadcasted_iota(jnp.int32, row.shape, 1)
    return jnp.sum(jnp.where(lane == lane_index, row, 0.0), axis=-1, keepdims=True)


def _dn_prep(qc, kc, vc, ba, a_log_row, dt_row, head):
    q = _silu(qc)
    k = _silu(kc)
    v = _silu(vc)
    q = q * lax.rsqrt(jnp.sum(q * q, axis=-1, keepdims=True) + EPS) * (HEAD_DIM ** -0.5)
    k = k * lax.rsqrt(jnp.sum(k * k, axis=-1, keepdims=True) + EPS)
    beta = _sigmoid(_lane_pick(ba, head))
    g = -jnp.exp(_lane_pick(a_log_row, head)) * _softplus(_lane_pick(ba, HEADS + head) + _lane_pick(dt_row, head))
    return q, k, v, g, beta


def _dn_chunks_head(q, k, v, gcol, bcol):
    n, c, d = q.shape
    row = lax.broadcasted_iota(jnp.int32, (n, c, c), 1)
    col = lax.broadcasted_iota(jnp.int32, (n, c, c), 2)
    g_wide = jnp.broadcast_to(gcol, (n, c, d))
    b_wide = jnp.broadcast_to(bcol, (n, c, d))
    gc = _chunk_cumsum(g_wide)
    gc_rows = gc[:, :, :c]
    decay = jnp.exp(jnp.where(row >= col, gc_rows - jnp.swapaxes(gc_rows, 1, 2), -1e30))
    kb = k * b_wide
    eg = jnp.exp(gc)
    a = jnp.where(row > col, _BNT(kb, k) * decay, 0.0)
    tinv = _unit_lower_inverse(a)
    u = _BNN(tinv, v * b_wide)
    w = _BNN(tinv, kb * eg)
    attn = _BNT(q, k) * decay
    g_last = jnp.sum(g_wide, axis=1, keepdims=True)
    return u, w, attn, q * eg, k * jnp.exp(g_last - gc), jnp.exp(g_last)


def _dn_chunks(q, k, v, gcol, bcol, states):
    u, w, attn, qe, kdec, eglast = _dn_chunks_head(q, k, v, gcol, bcol)
    v_new = u - _BNN(w, states)
    o = _BNN(qe, states) + _BNN(attn, v_new)
    return (o, states * eglast + _BTN(kdec, v_new)), (w, attn, qe, kdec, eglast)


def _conv_taps(buf, head, cw, rows):
    acc = cw[0:1, :] * buf[head, pl.ds(5, rows), :]
    for j in range(1, 4):
        acc = acc + cw[j:j + 1, :] * buf[head, pl.ds(5 + j, rows), :]
    return acc


def _conv_backward(dbuf, dhead, xbuf, xhead, cw, dxc, rows):
    dbuf[dhead, pl.ds(0, rows), :] = dxc
    dx = cw[0:1, :] * dbuf[dhead, pl.ds(3, rows), :]
    for j in range(1, 4):
        dx = dx + cw[j:j + 1, :] * dbuf[dhead, pl.ds(3 - j, rows), :]
    dcw = jnp.concatenate(
        [jnp.sum(dxc * xbuf[xhead, pl.ds(5 + j, rows), :], axis=0, keepdims=True) for j in range(4)], axis=0)
    dbuf[dhead, pl.ds(rows, 8), :] = dbuf[dhead, pl.ds(0, 8), :]
    return dx, dcw


def _params(**kw):
    return pltpu.CompilerParams(vmem_limit_bytes=VMEM_LIMIT, **kw)


def _matmul(a, b, form, tm, tn, tk, name, add=None, out_dtype=F32, dep=None):
    if form == "nn":
        (m, kdim), (_, n) = a.shape, b.shape
        a_spec = pl.BlockSpec((tm, tk), lambda j, i, k: (i, k))
        b_spec = pl.BlockSpec((tk, tn), lambda j, i, k: (k, j))
        dims = NN
    elif form == "nt":
        (m, kdim), (n, _) = a.shape, b.shape
        a_spec = pl.BlockSpec((tm, tk), lambda j, i, k: (i, k))
        b_spec = pl.BlockSpec((tn, tk), lambda j, i, k: (j, k))
        dims = NT
    else:
        (kdim, m), (_, n) = a.shape, b.shape
        a_spec = pl.BlockSpec((tk, tm), lambda j, i, k: (k, i))
        b_spec = pl.BlockSpec((tk, tn), lambda j, i, k: (k, j))
        dims = TN
    assert m % tm == 0 and n % tn == 0 and kdim % tk == 0, (name, m, n, kdim, tm, tn, tk)
    ksteps = kdim // tk
    o_spec = pl.BlockSpec((tm, tn), lambda j, i, k: (i, j))
    has_add = add is not None
    extra = [] if dep is None else [dep]

    def body(*refs):
        a_ref, b_ref = refs[:2]
        c_ref = refs[2] if has_add else None
        o_ref, acc = refs[-2:]
        k = pl.program_id(2)

        @pl.when(k == 0)
        def _():
            acc[...] = c_ref[...] if has_add else jnp.zeros_like(acc)

        acc[...] += lax.dot_general(a_ref[...].astype(BF16), b_ref[...].astype(BF16), dims,
                                    preferred_element_type=F32)

        @pl.when(k == ksteps - 1)
        def _():
            o_ref[...] = acc[...].astype(o_ref.dtype)

    in_specs = [a_spec, b_spec] + ([o_spec] if has_add else []) + [pl.BlockSpec((8, HEAD_DIM), lambda j, i, k: (0, 0))
                                                                   for _ in extra]
    args = (a, b) + ((add,) if has_add else ()) + tuple(extra)
    return pl.pallas_call(
        body, name=name, grid=(n // tn, m // tm, ksteps), in_specs=in_specs, out_specs=o_spec,
        out_shape=jax.ShapeDtypeStruct((m, n), out_dtype), scratch_shapes=[pltpu.VMEM((tm, tn), F32)],
        compiler_params=_params(dimension_semantics=("parallel", "parallel", "arbitrary")),
    )(*args)


def _rmsnorm_fwd(x, w_row, name):
    s = x.shape[0]
    tb = min(TIME_BLOCK, s)

    def body(x_ref, w_ref, o_ref):
        o_ref[...] = _rmsnorm(x_ref[...], w_ref[...]).astype(BF16)

    return pl.pallas_call(
        body, name=name, grid=(s // tb,),
        in_specs=[pl.BlockSpec((tb, D_MODEL), lambda i: (i, 0)), pl.BlockSpec((1, D_MODEL), lambda i: (0, 0))],
        out_specs=pl.BlockSpec((tb, D_MODEL), lambda i: (i, 0)),
        out_shape=jax.ShapeDtypeStruct((s, D_MODEL), BF16), compiler_params=_params(),
    )(x, w_row)


def _rmsnorm_bwd(x, w_row, dh, dres, name):
    s = x.shape[0]
    tb = min(TIME_BLOCK, s)

    def body(x_ref, w_ref, dh_ref, dres_ref, dx_ref, dw_ref):
        _, vjp = jax.vjp(_rmsnorm, x_ref[...], w_ref[...])
        dx, dw = vjp(dh_ref[...])
        dx_ref[...] = dres_ref[...] + dx

        @pl.when(pl.program_id(0) == 0)
        def _():
            dw_ref[...] = jnp.zeros_like(dw_ref)

        dw_ref[...] += dw

    row = pl.BlockSpec((tb, D_MODEL), lambda i: (i, 0))
    vec = pl.BlockSpec((1, D_MODEL), lambda i: (0, 0))
    return pl.pallas_call(
        body, name=name, grid=(s // tb,), in_specs=[row, vec, row, row], out_specs=[row, vec],
        out_shape=[jax.ShapeDtypeStruct((s, D_MODEL), F32), jax.ShapeDtypeStruct((1, D_MODEL), F32)],
        compiler_params=_params(),
    )(x, w_row, dh, dres)


def _final_loss(x, w_row, target, name):
    s = x.shape[0]
    tb = min(TIME_BLOCK, s)

    def loss_fn(xv, wv, tv):
        err = _rmsnorm(xv, wv) - tv
        return 0.5 * jnp.sum(jnp.sum(err * err, axis=-1, keepdims=True), axis=0, keepdims=True) * (1.0 / D_MODEL)

    def body(x_ref, w_ref, t_ref, loss_ref, dx_ref, dw_ref):
        tv = t_ref[...]
        loss, vjp = jax.vjp(lambda xv, wv: loss_fn(xv, wv, tv), x_ref[...], w_ref[...])
        dx, dw = vjp(jnp.ones((1, 1), F32))
        dx_ref[...] = dx

        @pl.when(pl.program_id(0) == 0)
        def _():
            dw_ref[...] = jnp.zeros_like(dw_ref)
            loss_ref[...] = jnp.zeros_like(loss_ref)

        dw_ref[...] += dw
        loss_ref[...] += jnp.broadcast_to(loss, loss_ref.shape)

    row = pl.BlockSpec((tb, D_MODEL), lambda i: (i, 0))
    vec = pl.BlockSpec((1, D_MODEL), lambda i: (0, 0))
    return pl.pallas_call(
        body, name=name, grid=(s // tb,), in_specs=[row, vec, row],
        out_specs=[pl.BlockSpec((1, HEAD_DIM), lambda i: (0, 0)), row, vec],
        out_shape=[jax.ShapeDtypeStruct((1, HEAD_DIM), F32), jax.ShapeDtypeStruct((s, D_MODEL), F32),
                   jax.ShapeDtypeStruct((1, D_MODEL), F32)],
        compiler_params=_params(),
    )(x, w_row, target)


def _head_specs(tb, time_of):
    def col(off):
        return pl.BlockSpec((tb, HEAD_DIM), lambda t, h: (time_of(t), off + h))
    return col


def _vec_spec():
    return pl.BlockSpec((1, HEAD_DIM), lambda t, h: (0, h))


def _lru_fwd(proj, conv_w, conv_b, wa, ba, wx, bx, lam, nw, name):
    s = proj.shape[0]
    tb = min(TIME_BLOCK, s)
    nt = s // tb
    col = _head_specs(tb, lambda t: t)

    def body(x_ref, z_ref, cw_ref, cb_ref, wa_ref, ba_ref, wx_ref, bx_ref, lam_ref, nw_ref,
             y_ref, hs_ref, xbuf, hcar):
        t, h = pl.program_id(0), pl.program_id(1)

        @pl.when(t == 0)
        def _():
            xbuf[h, pl.ds(0, 8), :] = jnp.zeros((8, HEAD_DIM), F32)
            hcar[h] = jnp.zeros((8, HEAD_DIM), F32)

        xbuf[h, pl.ds(8, tb), :] = x_ref[...]
        xc = _conv_taps(xbuf, h, cw_ref[...], tb) + cb_ref[...]
        a, b = _lru_gates(xc, wa_ref[...], ba_ref[...], wx_ref[...], bx_ref[...], lam_ref[...])
        hs_ref[...] = _scan_forward(a, b, hcar[h, pl.ds(0, 1), :])
        hcar[h, pl.ds(0, 1), :] = hs_ref[pl.ds(tb - 1, 1), :]
        xbuf[h, pl.ds(0, 8), :] = xbuf[h, pl.ds(tb, 8), :]
        y_ref[...] = _gated_norm(hs_ref[...], z_ref[...], nw_ref[...]).astype(BF16)

    vec = _vec_spec()
    return pl.pallas_call(
        body, name=name, grid=(nt, HEADS),
        in_specs=[col(COL_LRU_X), col(COL_LRU_Z), pl.BlockSpec((4, HEAD_DIM), lambda t, h: (0, h)), vec,
                  pl.BlockSpec((None, HEAD_DIM, HEAD_DIM), lambda t, h: (h, 0, 0)), vec,
                  pl.BlockSpec((None, HEAD_DIM, HEAD_DIM), lambda t, h: (h, 0, 0)), vec, vec, vec],
        out_specs=[col(0), col(0)],
        out_shape=[jax.ShapeDtypeStruct((s, 2 * D_MODEL), BF16), jax.ShapeDtypeStruct((s, D_MODEL), F32)],
        scratch_shapes=[pltpu.VMEM((HEADS, tb + 8, HEAD_DIM), F32), pltpu.VMEM((HEADS, 8, HEAD_DIM), F32)],
        compiler_params=_params(dimension_semantics=("arbitrary", "arbitrary")),
    )(proj, proj, conv_w, conv_b, wa, ba, wx, bx, lam, nw)


def _halo_spec(tb, nt, off):
    per = tb // 8
    return pl.BlockSpec((8, HEAD_DIM), lambda t, h: (jnp.maximum((nt - 1 - t) * per - 1, 0), off + h))


def _lru_bwd(proj, hs, dy, conv_w, conv_b, wa, ba, wx, bx, lam, nw, name):
    s = proj.shape[0]
    tb = min(TIME_BLOCK, s)
    nt = s // tb
    col = _head_specs(tb, lambda t: nt - 1 - t)

    def body(x_ref, xh_ref, z_ref, hs_ref, hh_ref, dy_ref, cw_ref, cb_ref, wa_ref, ba_ref, wx_ref, bx_ref,
             lam_ref, nw_ref, dx_ref, dz_ref, dcw_ref, dcb_ref, dwa_ref, dba_ref, dwx_ref, dbx_ref, dlam_ref,
             dnw_ref, xbuf, hbuf, dbuf, gcar):
        t, h = pl.program_id(0), pl.program_id(1)
        first_block = t == nt - 1

        @pl.when(t == 0)
        def _():
            dbuf[h, pl.ds(tb, 8), :] = jnp.zeros((8, HEAD_DIM), F32)
            gcar[h] = jnp.zeros((8, HEAD_DIM), F32)
            dcw_ref[h] = jnp.zeros((4, HEAD_DIM), F32)
            dwa_ref[h] = jnp.zeros((HEAD_DIM, HEAD_DIM), F32)
            dwx_ref[h] = jnp.zeros((HEAD_DIM, HEAD_DIM), F32)
            for ref in (dcb_ref, dba_ref, dbx_ref, dlam_ref, dnw_ref):
                ref[h] = jnp.zeros((1, HEAD_DIM), F32)

        keep = jnp.where(first_block, 0.0, 1.0)
        xbuf[0, pl.ds(0, 8), :] = xh_ref[...] * keep
        xbuf[0, pl.ds(8, tb), :] = x_ref[...]
        hbuf[pl.ds(0, 8), :] = hh_ref[...] * keep
        hbuf[pl.ds(8, tb), :] = hs_ref[...]
        cw = cw_ref[...]
        xc = _conv_taps(xbuf, 0, cw, tb) + cb_ref[...]
        (a, _), gates_vjp = jax.vjp(_lru_gates, xc, wa_ref[...], ba_ref[...], wx_ref[...], bx_ref[...], lam_ref[...])
        _, norm_vjp = jax.vjp(_gated_norm, hs_ref[...], z_ref[...], nw_ref[...])
        dh, dz, dnw = norm_vjp(dy_ref[...])
        dz_ref[...] = dz.astype(dz_ref.dtype)
        g = _scan_reverse(a, dh, gcar[h, pl.ds(0, 1), :])
        gcar[h, pl.ds(0, 1), :] = a[0:1, :] * g[0:1, :]
        dxc, dwa, dba, dwx, dbx, dlam = gates_vjp((g * hbuf[pl.ds(7, tb), :], g))
        dx, dcw = _conv_backward(dbuf, h, xbuf, 0, cw, dxc, tb)
        dx_ref[...] = dx.astype(dx_ref.dtype)
        dcw_ref[h] += dcw
        dcb_ref[h] += jnp.sum(dxc, axis=0, keepdims=True)
        dwa_ref[h] += dwa
        dwx_ref[h] += dwx
        dba_ref[h] += dba
        dbx_ref[h] += dbx
        dlam_ref[h] += dlam
        dnw_ref[h] += dnw

    vec = _vec_spec()
    mat = pl.BlockSpec((None, HEAD_DIM, HEAD_DIM), lambda t, h: (h, 0, 0))

    def whole(shape):
        return pl.BlockSpec(shape, lambda t, h: (0,) * len(shape))

    head_vec = jax.ShapeDtypeStruct((HEADS, 1, HEAD_DIM), F32)
    head_mat = jax.ShapeDtypeStruct((HEADS, HEAD_DIM, HEAD_DIM), F32)
    return pl.pallas_call(
        body, name=name, grid=(nt, HEADS),
        in_specs=[col(COL_LRU_X), _halo_spec(tb, nt, COL_LRU_X), col(COL_LRU_Z), col(0), _halo_spec(tb, nt, 0), col(0),
                  pl.BlockSpec((4, HEAD_DIM), lambda t, h: (0, h)), vec, mat, vec, mat, vec, vec, vec],
        out_specs=[col(0), col(0), whole((HEADS, 4, HEAD_DIM)), whole((HEADS, 1, HEAD_DIM)),
                   whole((HEADS, HEAD_DIM, HEAD_DIM)), whole((HEADS, 1, HEAD_DIM)),
                   whole((HEADS, HEAD_DIM, HEAD_DIM)), whole((HEADS, 1, HEAD_DIM)), whole((HEADS, 1, HEAD_DIM)),
                   whole((HEADS, 1, HEAD_DIM))],
        out_shape=[jax.ShapeDtypeStruct((s, D_MODEL), BF16), jax.ShapeDtypeStruct((s, D_MODEL), BF16),
                   jax.ShapeDtypeStruct((HEADS, 4, HEAD_DIM), F32), head_vec, head_mat, head_vec, head_mat, head_vec,
                   head_vec, head_vec],
        scratch_shapes=[pltpu.VMEM((1, tb + 8, HEAD_DIM), F32), pltpu.VMEM((tb + 8, HEAD_DIM), F32),
                        pltpu.VMEM((HEADS, tb + 8, HEAD_DIM), F32), pltpu.VMEM((HEADS, 8, HEAD_DIM), F32)],
        compiler_params=_params(dimension_semantics=("arbitrary", "arbitrary")),
    )(proj, proj, proj, hs, hs, dy, conv_w, conv_b, wa, ba, wx, bx, lam, nw)


def _dn_fwd(proj, y, conv_w, a_log_row, dt_row, nw, name):
    s = proj.shape[0]
    tb = min(TIME_BLOCK, s)
    nt = s // tb
    nchunk = tb // CHUNK
    col = _head_specs(tb, lambda t: t)

    def body(q_ref, k_ref, v_ref, z_ref, ba_ref, cwq_ref, cwk_ref, cwv_ref, al_ref, dt_ref, nw_ref, y_in_ref,
             y_ref, o_ref, st_ref, xbuf, state):
        t, h = pl.program_id(0), pl.program_id(1)

        @pl.when(t == 0)
        def _():
            for i in range(3):
                xbuf[3 * h + i, pl.ds(0, 8), :] = jnp.zeros((8, HEAD_DIM), F32)
            state[h] = jnp.zeros((HEAD_DIM, HEAD_DIM), F32)

        conv = []
        for i, (ref, cw_ref) in enumerate(((q_ref, cwq_ref), (k_ref, cwk_ref), (v_ref, cwv_ref))):
            xbuf[3 * h + i, pl.ds(8, tb), :] = ref[...]
            conv.append(_conv_taps(xbuf, 3 * h + i, cw_ref[...], tb))
            xbuf[3 * h + i, pl.ds(0, 8), :] = xbuf[3 * h + i, pl.ds(tb, 8), :]
        q, k, v, g, beta = _dn_prep(conv[0], conv[1], conv[2], ba_ref[...], al_ref[...], dt_ref[...], h)
        def chunks(a):
            return a.reshape(nchunk, CHUNK, a.shape[-1])

        u, w, attn, qe, kdec, eglast = _dn_chunks_head(chunks(q), chunks(k), chunks(v), chunks(g), chunks(beta))
        st = state[h]
        for c in range(nchunk):
            st_ref[c] = st
            v_new = u[c] - _NN_B(w[c], st)
            o_ref[pl.ds(c * CHUNK, CHUNK), :] = _NN_B(qe[c], st) + _NN_B(attn[c], v_new)
            st = st * eglast[c] + _TN_B(kdec[c], v_new)
        state[h] = st
        y_ref[...] = _gated_norm(o_ref[...], z_ref[...], nw_ref[...]).astype(BF16)

    def cw_spec(off):
        return pl.BlockSpec((4, HEAD_DIM), lambda t, h: (0, off + h))

    row128 = pl.BlockSpec((1, HEAD_DIM), lambda t, h: (0, 0))
    return pl.pallas_call(
        body, name=name, grid=(nt, HEADS),
        in_specs=[col(COL_Q), col(COL_K), col(COL_V), col(COL_DN_Z),
                  pl.BlockSpec((tb, HEAD_DIM), lambda t, h: (t, COL_BA)),
                  cw_spec(0), cw_spec(HEADS), cw_spec(2 * HEADS), row128, row128, row128,
                  pl.BlockSpec(memory_space=pl.ANY)],
        out_specs=[col(HEADS), col(0), pl.BlockSpec((None, nchunk, HEAD_DIM, HEAD_DIM), lambda t, h: (h, t, 0, 0))],
        out_shape=[jax.ShapeDtypeStruct((s, 2 * D_MODEL), BF16), jax.ShapeDtypeStruct((s, D_MODEL), F32),
                   jax.ShapeDtypeStruct((HEADS, s // CHUNK, HEAD_DIM, HEAD_DIM), F32)],
        input_output_aliases={11: 0},
        scratch_shapes=[pltpu.VMEM((3 * HEADS, tb + 8, HEAD_DIM), F32), pltpu.VMEM((HEADS, HEAD_DIM, HEAD_DIM), F32)],
        compiler_params=_params(dimension_semantics=("arbitrary", "arbitrary")),
    )(proj, proj, proj, proj, proj, conv_w, conv_w, conv_w, a_log_row, dt_row, nw, y)


def _dn_bwd(proj, o, states, dy, conv_w, a_log_row, dt_row, nw, name):
    s = proj.shape[0]
    tb = min(TIME_BLOCK, s)
    nt = s // tb
    nchunk = tb // CHUNK
    col = _head_specs(tb, lambda t: nt - 1 - t)

    def body(q_ref, qh_ref, k_ref, kh_ref, v_ref, vh_ref, z_ref, ba_ref, o_ref, st_ref, dy_ref,
             cwq_ref, cwk_ref, cwv_ref, al_ref, dt_ref, nw_ref,
             dq_ref, dk_ref, dv_ref, dz_ref, dba_ref, dcw_ref, dal_ref, ddt_ref, dnw_ref,
             xbuf, dbuf, dstate, dst_s):
        t, h = pl.program_id(0), pl.program_id(1)
        first_block = t == nt - 1

        @pl.when(t == 0)
        def _():
            for i in range(3):
                dbuf[3 * h + i, pl.ds(tb, 8), :] = jnp.zeros((8, HEAD_DIM), F32)
                dcw_ref[3 * h + i] = jnp.zeros((4, HEAD_DIM), F32)
            dstate[h] = jnp.zeros((HEAD_DIM, HEAD_DIM), F32)

        @pl.when((t == 0) & (h == 0))
        def _():
            for ref in (dal_ref, ddt_ref, dnw_ref):
                ref[...] = jnp.zeros_like(ref)

        keep = jnp.where(first_block, 0.0, 1.0)
        cws = (cwq_ref[...], cwk_ref[...], cwv_ref[...])
        conv = []
        for i, (ref, halo) in enumerate(((q_ref, qh_ref), (k_ref, kh_ref), (v_ref, vh_ref))):
            xbuf[i, pl.ds(0, 8), :] = halo[...] * keep
            xbuf[i, pl.ds(8, tb), :] = ref[...]
            conv.append(_conv_taps(xbuf, i, cws[i], tb))
        (q, k, v, g, beta), prep_vjp = jax.vjp(
            lambda qc, kc, vc, ba, al, dt: _dn_prep(qc, kc, vc, ba, al, dt, h),
            conv[0], conv[1], conv[2], ba_ref[...], al_ref[...], dt_ref[...])
        _, norm_vjp = jax.vjp(_gated_norm, o_ref[...], z_ref[...], nw_ref[...])
        do, dz, dnw = norm_vjp(dy_ref[...])
        dz_ref[...] = dz.astype(dz_ref.dtype)
        dnw_ref[...] += dnw

        def chunks(a):
            return a.reshape(nchunk, CHUNK, a.shape[-1])

        do = chunks(do)
        _, chunks_vjp, (w, attn, qe, kdec, eglast) = jax.vjp(
            _dn_chunks, chunks(q), chunks(k), chunks(v), chunks(g), chunks(beta), st_ref[...], has_aux=True)
        from_o = _dot(attn, do, B_TN, "bf16")
        from_qe = _dot(qe, do, B_TN, "bf16")
        dst = dstate[h]
        for c in reversed(range(nchunk)):
            dst_s[c] = dst
            dv_new = from_o[c] + _dot(kdec[c], dst, NN, "bf16")
            dst = dst * eglast[c] + from_qe[c] - _dot(w[c], dv_new, TN, "bf16")
        dstate[h] = dst
        dq, dk, dv, dg, db, _ = chunks_vjp((do, dst_s[...]))

        def rows(a):
            return a.reshape(tb, a.shape[-1])

        dqc, dkc, dvc, dba, dal, ddt = prep_vjp((rows(dq), rows(dk), rows(dv), rows(dg), rows(db)))
        for i, (dxc, out) in enumerate(((dqc, dq_ref), (dkc, dk_ref), (dvc, dv_ref))):
            dx, dcw = _conv_backward(dbuf, 3 * h + i, xbuf, i, cws[i], dxc, tb)
            out[...] = dx.astype(out.dtype)
            dcw_ref[3 * h + i] += dcw
        dal_ref[...] += dal
        ddt_ref[...] += ddt

        @pl.when(h == 0)
        def _():
            dba_ref[...] = dba.astype(dba_ref.dtype)

        @pl.when(h > 0)
        def _():
            dba_ref[...] += dba.astype(dba_ref.dtype)

    def cw_spec(off):
        return pl.BlockSpec((4, HEAD_DIM), lambda t, h: (0, off + h))

    def whole(shape):
        return pl.BlockSpec(shape, lambda t, h: (0,) * len(shape))

    row128 = whole((1, HEAD_DIM))
    blk = (tb, HEAD_DIM)
    act = jax.ShapeDtypeStruct((s, D_MODEL), BF16)
    row_out = jax.ShapeDtypeStruct((1, HEAD_DIM), F32)
    return pl.pallas_call(
        body, name=name, grid=(nt, HEADS),
        in_specs=[col(COL_Q), _halo_spec(tb, nt, COL_Q), col(COL_K), _halo_spec(tb, nt, COL_K),
                  col(COL_V), _halo_spec(tb, nt, COL_V), col(COL_DN_Z),
                  pl.BlockSpec(blk, lambda t, h: (nt - 1 - t, COL_BA)), col(0),
                  pl.BlockSpec((None, nchunk, HEAD_DIM, HEAD_DIM), lambda t, h: (h, nt - 1 - t, 0, 0)), col(HEADS),
                  cw_spec(0), cw_spec(HEADS), cw_spec(2 * HEADS), row128, row128, row128],
        out_specs=[col(0), col(0), col(0), col(0), pl.BlockSpec(blk, lambda t, h: (nt - 1 - t, 0)),
                   whole((3 * HEADS, 4, HEAD_DIM)), row128, row128, row128],
        out_shape=[act, act, act, act, jax.ShapeDtypeStruct((s, HEAD_DIM), F32),
                   jax.ShapeDtypeStruct((3 * HEADS, 4, HEAD_DIM), F32), row_out, row_out, row_out],
        scratch_shapes=[pltpu.VMEM((3, tb + 8, HEAD_DIM), F32), pltpu.VMEM((3 * HEADS, tb + 8, HEAD_DIM), F32),
                        pltpu.VMEM((HEADS, HEAD_DIM, HEAD_DIM), F32), pltpu.VMEM((nchunk, HEAD_DIM, HEAD_DIM), F32)],
        compiler_params=_params(dimension_semantics=("arbitrary", "arbitrary")),
    )(proj, proj, proj, proj, proj, proj, proj, proj, o, states, dy, conv_w, conv_w, conv_w, a_log_row, dt_row, nw)


def _mesh_position():
    x, y, c = lax.axis_index("x"), lax.axis_index("y"), lax.axis_index("c")
    return x, y, c, 4 * x + 2 * y + c


def _peer(k, x, y, c):
    px = 1 - x if k & 4 else x
    py = 1 - y if k & 2 else y
    pc = 1 - c if k & 1 else c
    return (px, py, pc), 4 * px + 2 * py + pc


def _exchange_copies(ins, lands, scatter, send_sems, recv_sems):
    x, y, c, me = _mesh_position()
    sends, recvs = [], []
    for i, (src, land) in enumerate(zip(ins, lands)):
        for k in range(1, N_DEV):
            peer, peer_id = _peer(k, x, y, c)
            sem = i * (N_DEV - 1) + k - 1
            for dst, out in ((me, sends), (peer_id, recvs)):
                out.append(pltpu.make_async_remote_copy(
                    src_ref=src.at[peer_id] if scatter[i] else src, dst_ref=land.at[dst],
                    send_sem=send_sems.at[sem], recv_sem=recv_sems.at[sem],
                    device_id=peer, device_id_type=pl.DeviceIdType.MESH))
    return sends, recvs


def _landing_shape(a, scatter):
    return a.shape if scatter else (N_DEV,) + a.shape


def _direct_exchange(arrays, scatter, name):
    n = len(arrays)
    out_shapes = [jax.ShapeDtypeStruct(_landing_shape(a, sc), a.dtype) for a, sc in zip(arrays, scatter)]

    def body(*refs):
        ins, outs = refs[:n], refs[n:2 * n]
        send_sems, recv_sems, local_sems = refs[2 * n:]
        me = _mesh_position()[3]
        local = [pltpu.make_async_copy(ins[i].at[me] if scatter[i] else ins[i], outs[i].at[me], local_sems.at[i])
                 for i in range(n)]
        sends, recvs = _exchange_copies(ins, outs, scatter, send_sems, recv_sems)
        for cp in local + sends:
            cp.start()
        for cp in recvs:
            cp.wait_recv()
        for cp in sends:
            cp.wait_send()
        for cp in local:
            cp.wait()

    hbm = pl.BlockSpec(memory_space=pl.ANY)
    return pl.pallas_call(
        body, name=name, in_specs=[hbm] * n, out_specs=[hbm] * n, out_shape=out_shapes,
        scratch_shapes=[pltpu.SemaphoreType.DMA((n * (N_DEV - 1),)), pltpu.SemaphoreType.DMA((n * (N_DEV - 1),)),
                        pltpu.SemaphoreType.DMA((n,))],
    )(*arrays)


_HBM = pl.BlockSpec(memory_space=pltpu.HBM)
_SEM = pl.BlockSpec(memory_space=pltpu.SEMAPHORE)
_DATAFLOW = pltpu.SideEffectType.DATAFLOW_SIDE_EFFECTING


def _exchange_start(arrays, scatter, name):
    n = len(arrays)
    srcs = [pltpu.with_memory_space_constraint(a, pltpu.HBM) for a in arrays]
    lands = [pltpu.with_memory_space_constraint(lax.empty(_landing_shape(a, sc), a.dtype), pltpu.HBM)
             for a, sc in zip(arrays, scatter)]
    nsem = n * (N_DEV - 1)

    def body(*refs):
        ins, zones = refs[:n], refs[n:2 * n]
        send_sems, recv_sems = refs[2 * n], refs[2 * n + 1]
        token = refs[-1]
        sends, _ = _exchange_copies(ins, zones, scatter, send_sems, recv_sems)
        for cp in sends:
            cp.start()
        token[...] = jnp.zeros_like(token)

    res = pl.pallas_call(
        body, name=name,
        out_shape=(pltpu.SemaphoreType.DMA((nsem,)), pltpu.SemaphoreType.DMA((nsem,)),
                   *[pltpu.HBM(a.shape, a.dtype) for a in srcs + lands], jax.ShapeDtypeStruct((8, HEAD_DIM), F32)),
        in_specs=[_HBM] * (2 * n),
        out_specs=(_SEM, _SEM, *[_HBM] * (2 * n), pl.BlockSpec(memory_space=pltpu.VMEM)),
        input_output_aliases={i: 2 + i for i in range(2 * n)},
        compiler_params=pltpu.CompilerParams(has_side_effects=_DATAFLOW),
    )(*srcs, *lands)
    return dict(sems=res[:2], srcs=res[2:2 + n], lands=res[2 + n:2 + 2 * n], token_block=res[-1],
                token=res[-1][0, 0], scatter=scatter)


def _exchange_wait(started, after, name):
    scatter = started["scatter"]
    n = len(scatter)

    def body(*refs):
        ins, zones = refs[:n], refs[n:2 * n]
        send_sems, recv_sems = refs[2 * n], refs[2 * n + 1]
        sends, recvs = _exchange_copies(ins, zones, scatter, send_sems, recv_sems)
        for cp in sends:
            cp.wait_send()
        for cp in recvs:
            cp.wait_recv()

    thru = list(started["srcs"]) + list(started["lands"])
    res = pl.pallas_call(
        body, name=name, out_shape=[pltpu.HBM(a.shape, a.dtype) for a in thru],
        in_specs=[_HBM] * (2 * n) + [_SEM, _SEM, pl.BlockSpec(memory_space=pl.ANY)], out_specs=[_HBM] * (2 * n),
        input_output_aliases={i: i for i in range(2 * n)},
        compiler_params=pltpu.CompilerParams(has_side_effects=_DATAFLOW),
    )(*thru, *started["sems"], after)
    me = 4 * lax.axis_index("x") + 2 * lax.axis_index("y") + lax.axis_index("c")
    out = []
    for src, got, sc in zip(res[:n], res[n:], scatter):
        own = lax.dynamic_index_in_dim(src, me, 0, keepdims=False) if sc else src
        out.append(lax.dynamic_update_index_in_dim(got, own, me, 0))
    return out


def _adamw(parts, w, m, v, name, rows_per_step, row_offset=0, into=None):
    rows, cols = parts.shape[1:]
    tr = min(rows_per_step, rows)
    assert rows % tr == 0 and row_offset % tr == 0, (name, rows, tr, row_offset)
    first = row_offset // tr
    c1 = 1.0 / (1.0 - ADAM_B1 ** ADAM_STEP)
    c2 = 1.0 / (1.0 - ADAM_B2 ** ADAM_STEP)

    def body(p_ref, w_ref, m_ref, v_ref, *rest):
        g_ref, d_ref, nm_ref, nv_ref = rest[-4:]
        g = p_ref[0].astype(F32)
        for d in range(1, N_DEV):
            g = g + p_ref[d].astype(F32)
        nm = ADAM_B1 * m_ref[...] + (1.0 - ADAM_B1) * g
        nv = ADAM_B2 * v_ref[...] + (1.0 - ADAM_B2) * (g * g)
        g_ref[...] = g
        nm_ref[...] = nm
        nv_ref[...] = nv
        d_ref[...] = -ADAM_LR * ((nm * c1) / (jnp.sqrt(nv * c2) + ADAM_EPS) + ADAM_WD * w_ref[...])

    blk = pl.BlockSpec((tr, cols), lambda i: (i + first, 0))
    shape = jax.ShapeDtypeStruct(w.shape, F32)
    prior = [] if into is None else list(into)
    return pl.pallas_call(
        body, name=name, grid=(rows // tr,),
        in_specs=[pl.BlockSpec((N_DEV, tr, cols), lambda i: (0, i, 0)), blk, blk, blk]
        + [pl.BlockSpec(memory_space=pl.ANY)] * len(prior),
        out_specs=[blk] * 4, out_shape=[shape] * 4,
        input_output_aliases={4 + j: j for j in range(len(prior))}, compiler_params=_params(),
    )(parts, w, m, v, *prior)


_LAYERED = ("norm_w", "lru_conv_b", "lru_wa", "lru_ba", "lru_wx", "lru_bx", "lru_lambda", "lru_norm_w",
            "dn_A_log", "dn_dt_bias", "dn_norm_w")
_WEIGHTS = ("norm_w", "w_in", "lru_conv_w", "lru_conv_b", "lru_wa", "lru_ba", "lru_wx", "lru_bx", "lru_lambda",
            "lru_norm_w", "dn_conv_w", "dn_A_log", "dn_dt_bias", "dn_norm_w", "w_out", "final_norm_w")


def _pack_layer(tree, layer, tail=()):
    rows = []
    for name in _LAYERED:
        a = tree[name][layer]
        if a.shape[-1] == HEADS:
            a = jnp.pad(a, (0, HEAD_DIM - HEADS))
        rows.append(a.reshape(-1, HEAD_DIM))
    rows += [t.reshape(-1, HEAD_DIM) for t in tail]
    packed = jnp.concatenate(rows, axis=0)
    return jnp.pad(packed, ((0, (-packed.shape[0]) % 8), (0, 0)))


def _unpack_layer(packed, like):
    out, at = {}, 0
    for name in _LAYERED:
        shape = like[name].shape[1:]
        if shape[-1] == HEADS:
            n = 1
            out[name] = packed[at, :HEADS]
        else:
            n = like[name][0].size // HEAD_DIM
            out[name] = packed[at:at + n].reshape(shape)
        at += n
    return out, at


def _heads_to_channels(a):
    return jnp.transpose(a, (1, 0, 2)).reshape(a.shape[1], HEADS * HEAD_DIM)


def kernel(x, norm_w, w_in, lru_conv_w, lru_conv_b, lru_wa, lru_ba, lru_wx, lru_bx, lru_lambda, lru_norm_w, dn_conv_w, dn_A_log, dn_dt_bias, dn_norm_w, w_out, final_norm_w, loss_target, m_norm_w, m_w_in, m_lru_conv_w, m_lru_conv_b, m_lru_wa, m_lru_ba, m_lru_wx, m_lru_bx, m_lru_lambda, m_lru_norm_w, m_dn_conv_w, m_dn_A_log, m_dn_dt_bias, m_dn_norm_w, m_w_out, m_final_norm_w, v_norm_w, v_w_in, v_lru_conv_w, v_lru_conv_b, v_lru_wa, v_lru_ba, v_lru_wx, v_lru_bx, v_lru_lambda, v_lru_norm_w, v_dn_conv_w, v_dn_A_log, v_dn_dt_bias, v_dn_norm_w, v_w_out, v_final_norm_w):
    weights = dict(norm_w=norm_w, w_in=w_in, lru_conv_w=lru_conv_w, lru_conv_b=lru_conv_b, lru_wa=lru_wa,
                   lru_ba=lru_ba, lru_wx=lru_wx, lru_bx=lru_bx, lru_lambda=lru_lambda, lru_norm_w=lru_norm_w,
                   dn_conv_w=dn_conv_w, dn_A_log=dn_A_log, dn_dt_bias=dn_dt_bias, dn_norm_w=dn_norm_w,
                   w_out=w_out, final_norm_w=final_norm_w)
    mom_m = dict(norm_w=m_norm_w, w_in=m_w_in, lru_conv_w=m_lru_conv_w, lru_conv_b=m_lru_conv_b, lru_wa=m_lru_wa,
                 lru_ba=m_lru_ba, lru_wx=m_lru_wx, lru_bx=m_lru_bx, lru_lambda=m_lru_lambda,
                 lru_norm_w=m_lru_norm_w, dn_conv_w=m_dn_conv_w, dn_A_log=m_dn_A_log, dn_dt_bias=m_dn_dt_bias,
                 dn_norm_w=m_dn_norm_w, w_out=m_w_out, final_norm_w=m_final_norm_w)
    mom_v = dict(norm_w=v_norm_w, w_in=v_w_in, lru_conv_w=v_lru_conv_w, lru_conv_b=v_lru_conv_b, lru_wa=v_lru_wa,
                 lru_ba=v_lru_ba, lru_wx=v_lru_wx, lru_bx=v_lru_bx, lru_lambda=v_lru_lambda,
                 lru_norm_w=v_lru_norm_w, dn_conv_w=v_dn_conv_w, dn_A_log=v_dn_A_log, dn_dt_bias=v_dn_dt_bias,
                 dn_norm_w=v_dn_norm_w, w_out=v_w_out, final_norm_w=v_final_norm_w)
    depth = norm_w.shape[0]
    xs = x[0]
    s = xs.shape[0]
    tm = min(512, s)

    assert depth >= 2, depth

    def row(a):
        return a.reshape(1, -1)

    def pad_row(a):
        return jnp.pad(a, (0, HEAD_DIM - a.shape[0])).reshape(1, HEAD_DIM)

    def full_w_in(g):
        w = jnp.transpose(g, (1, 2, 0, 3)).reshape(g.shape[1], D_MODEL, D_IN)
        return jnp.pad(w, ((0, 0), (0, 0), (0, D_IN_PAD - D_IN)))

    g_win0, g_lcw, g_dcw = _direct_exchange([w_in[:1].astype(BF16), lru_conv_w, dn_conv_w], [False] * 3,
                                            "gather_first")
    rest = _exchange_start([w_in[1:].astype(BF16), w_out.astype(BF16)], [False] * 2, "gather_rest_start")
    win = [full_w_in(g_win0)[0]]
    wout = None
    lcw = jnp.transpose(g_lcw, (1, 2, 0, 3)).reshape(depth, 4, D_MODEL)
    dcw = jnp.transpose(g_dcw, (1, 2, 0, 3)).reshape(depth, 4, 3 * D_MODEL)

    saved = []
    cur = xs
    for l in range(depth):
        nw_row = row(norm_w[l]) + rest["token"] if l == 0 else row(norm_w[l])
        hn = _rmsnorm_fwd(cur, nw_row, f"norm_fwd_{l}")
        proj = _matmul(hn, win[l], "nn", tm, 896, D_MODEL, f"in_proj_{l}")
        y_lru, hs = _lru_fwd(proj, lcw[l], row(lru_conv_b[l]), lru_wa[l], row(lru_ba[l]), lru_wx[l], row(lru_bx[l]),
                             row(lru_lambda[l]), row(lru_norm_w[l]), f"lru_fwd_{l}")
        ycat, o_dn, states = _dn_fwd(proj, y_lru, dcw[l], pad_row(dn_A_log[l]), pad_row(dn_dt_bias[l]),
                                     row(dn_norm_w[l]), f"dn_fwd_{l}")
        if l == 0:
            g_win_rest, g_wout = _exchange_wait(rest, ycat, "gather_rest_wait")
            win += list(full_w_in(g_win_rest))
            wout = jnp.transpose(g_wout, (1, 0, 2, 3)).reshape(depth, 2 * D_MODEL, D_MODEL)
        nxt = _matmul(ycat, wout[l], "nn", tm, D_MODEL, 2 * D_MODEL, f"out_proj_{l}", add=cur)
        saved.append((cur, hn, proj, hs, o_dn, states, ycat))
        cur = nxt
    loss_part, dx, d_final = _final_loss(cur, row(final_norm_w), loss_target[0], "final_loss")

    def win_slots(g):
        return jnp.transpose(g.reshape(D_MODEL, N_DEV, D_IN // N_DEV), (1, 0, 2))

    def wout_slots(g):
        return g.reshape(N_DEV, 2 * D_MODEL // N_DEV, D_MODEL)

    grads = {k: [None] * depth for k in _WEIGHTS if k not in ("final_norm_w", "w_in", "w_out")}
    started = {}
    token = None
    for l in reversed(range(depth)):
        x_in, hn, proj, hs, o_dn, states, ycat = saved[l]
        dy = _matmul(dx, wout[l], "nt", tm, D_MODEL, D_MODEL, f"out_proj_dy_{l}")
        g_wout_l = _matmul(ycat, dx, "tn", D_MODEL, D_MODEL, tm, f"out_proj_dw_{l}", out_dtype=BF16)
        if l == 0:
            started["w_out_0"] = _exchange_start([wout_slots(g_wout_l)], [True], "exchange_w_out_0_start")
            token = token + started["w_out_0"]["token"]
        cb_row = row(lru_conv_b[l]) if token is None else row(lru_conv_b[l]) + token
        (dlx, dlz, g_lcw, g_lcb, g_wa, g_ba, g_wx, g_bx, g_lam, g_lnw) = _lru_bwd(
            proj, hs, dy, lcw[l], cb_row, lru_wa[l], row(lru_ba[l]), lru_wx[l], row(lru_bx[l]),
            row(lru_lambda[l]), row(lru_norm_w[l]), f"lru_bwd_{l}")
        (dq, dk, dv, ddz, dba, g_dcw3, g_al, g_dt, g_dnw) = _dn_bwd(
            proj, o_dn, states, dy, dcw[l], pad_row(dn_A_log[l]), pad_row(dn_dt_bias[l]), row(dn_norm_w[l]),
            f"dn_bwd_{l}")
        dproj = jnp.concatenate([dlx, dlz, dq, dk, dv, ddz, dba.astype(BF16)], axis=1)
        dh = _matmul(dproj, win[l], "nt", tm, D_MODEL, 896, f"in_proj_dh_{l}")
        dx, g_nw = _rmsnorm_bwd(x_in, row(norm_w[l]), dh, dx, f"norm_bwd_{l}")
        grads["norm_w"][l] = g_nw.reshape(D_MODEL)
        grads["lru_conv_w"][l] = _heads_to_channels(g_lcw)
        grads["lru_conv_b"][l] = g_lcb.reshape(D_MODEL)
        grads["lru_wa"][l] = g_wa
        grads["lru_ba"][l] = g_ba.reshape(D_MODEL)
        grads["lru_wx"][l] = g_wx
        grads["lru_bx"][l] = g_bx.reshape(D_MODEL)
        grads["lru_lambda"][l] = g_lam.reshape(D_MODEL)
        grads["lru_norm_w"][l] = g_lnw.reshape(D_MODEL)
        g_dcw3 = g_dcw3.reshape(HEADS, 3, 4, HEAD_DIM)
        grads["dn_conv_w"][l] = jnp.concatenate([_heads_to_channels(g_dcw3[:, i]) for i in range(3)], axis=1)
        grads["dn_A_log"][l] = g_al[0, :HEADS]
        grads["dn_dt_bias"][l] = g_dt[0, :HEADS]
        grads["dn_norm_w"][l] = g_dnw.reshape(HEAD_DIM)
        if l > 0:
            tail = (d_final, loss_part) if l == depth - 1 else ()
            g_win_l = _matmul(hn, dproj, "tn", D_MODEL, 896, tm, f"in_proj_dw_{l}", out_dtype=BF16)[:, :D_IN]
            started[l] = _exchange_start([win_slots(g_win_l), wout_slots(g_wout_l), _pack_layer(grads, l, tail)],
                                         [True, True, False], f"exchange_{l}_start")
            token = started[l]["token"]

    def conv_slots(a):
        dd, r, cc = a.shape
        return jnp.transpose(a.reshape(dd, r, N_DEV, cc // N_DEV), (2, 0, 1, 3))

    small = _exchange_start(
        [conv_slots(jnp.stack(grads["lru_conv_w"])), conv_slots(jnp.stack(grads["dn_conv_w"])), _pack_layer(grads, 0)],
        [True, True, False], "exchange_small_start")
    g_win_0 = _matmul(hn, dproj, "tn", D_MODEL, 896, tm, "in_proj_dw_0", out_dtype=BF16,
                      dep=small["token_block"])[:, :D_IN]
    last = _exchange_start([win_slots(g_win_0)], [True], "exchange_0_start")

    new = {}
    flat_in = (depth * D_MODEL, D_IN // N_DEV)
    flat_out = (depth * 2 * D_MODEL // N_DEV, D_MODEL)
    zero_row = jnp.zeros((1, HEAD_DIM), F32)

    def adamw_pack(parts, layer):
        tails = [(t, zero_row) if layer == depth - 1 else () for t in (final_norm_w, m_final_norm_w, v_final_norm_w)]
        return _adamw(parts, _pack_layer(weights, layer, tails[0]), _pack_layer(mom_m, layer, tails[1]),
                      _pack_layer(mom_v, layer, tails[2]), f"adamw_small_{layer}", parts.shape[1])

    def adamw_w_in(parts, layer, into):
        return _adamw(parts, w_in.reshape(flat_in), m_w_in.reshape(flat_in), v_w_in.reshape(flat_in),
                      f"adamw_w_in_{layer}", 256, layer * D_MODEL, into)

    def adamw_w_out(parts, layer, into):
        return _adamw(parts, w_out.reshape(flat_out), m_w_out.reshape(flat_out), v_w_out.reshape(flat_out),
                      f"adamw_w_out_{layer}", 256, layer * flat_out[0] // depth, into)

    acc_in = acc_out = None
    packs = [None] * depth
    after = last["token_block"]
    for l in reversed(range(1, depth)):
        r_win, r_wout, r_pack = _exchange_wait(started[l], after, f"exchange_{l}_wait")
        acc_in = adamw_w_in(r_win, l, acc_in)
        acc_out = adamw_w_out(r_wout, l, acc_out)
        packs[l] = adamw_pack(r_pack, l)
        after = packs[l][0]
    (r_wout,) = _exchange_wait(started["w_out_0"], after, "exchange_w_out_0_wait")
    acc_out = adamw_w_out(r_wout, 0, acc_out)
    r_lcw, r_dcw, r_pack = _exchange_wait(small, acc_out[0], "exchange_small_wait")
    for name, parts in (("lru_conv_w", r_lcw), ("dn_conv_w", r_dcw)):
        w = weights[name]
        flat = (-1, w.shape[-1])
        outs = _adamw(parts.reshape((N_DEV,) + (w.size // w.shape[-1], w.shape[-1])), w.reshape(flat),
                      mom_m[name].reshape(flat), mom_v[name].reshape(flat), f"adamw_{name}", 8)
        new[name] = [a.reshape(w.shape) for a in outs]
    packs[0] = adamw_pack(r_pack, 0)
    (r_win,) = _exchange_wait(last, packs[0][0], "exchange_0_wait")
    acc_in = adamw_w_in(r_win, 0, acc_in)
    new["w_in"] = [a.reshape(w_in.shape) for a in acc_in]
    new["w_out"] = [a.reshape(w_out.shape) for a in acc_out]
    for name in _LAYERED:
        new[name] = [jnp.stack([_unpack_layer(packs[l][i], weights)[0][name] for l in range(depth)]) for i in range(4)]
    tail_at = _unpack_layer(packs[depth - 1][0], weights)[1]
    rows_final = D_MODEL // HEAD_DIM
    new["final_norm_w"] = [packs[depth - 1][i][tail_at:tail_at + rows_final].reshape(D_MODEL) for i in range(4)]
    loss = packs[depth - 1][0][tail_at + rows_final, 0]
    out = [loss, dx.reshape(x.shape)]
    for i in range(4):
        out += [new[name][i] for name in _WEIGHTS]
    return tuple(out)
```

```python
import functools

import jax
import jax.numpy as jnp
from jax import lax
from jax.experimental import pallas as pl
from jax.experimental.pallas import tpu as pltpu

F32 = jnp.float32
BF16 = jnp.bfloat16

N_DEV = 8
D_MODEL = 1024
HEADS = 8
HEAD_DIM = 128
CHUNK = 64
D_IN = 6160
D_IN_PAD = 6272
COL_LRU_X, COL_LRU_Z, COL_Q, COL_K, COL_V, COL_DN_Z, COL_BA = 0, 8, 16, 24, 32, 40, 48
LRU_C = 8.0
EPS = 1e-6
ADAM_LR, ADAM_B1, ADAM_B2, ADAM_EPS, ADAM_WD, ADAM_STEP = 0.001, 0.9, 0.999, 1e-08, 0.01, 10
TIME_BLOCK = 1024
VMEM_LIMIT = 56 * 1024 * 1024

NN = (((1,), (0,)), ((), ()))
NT = (((1,), (1,)), ((), ()))
TN = (((0,), (0,)), ((), ()))


B_NN = (((2,), (1,)), ((0,), (0,)))
B_NT = (((2,), (2,)), ((0,), (0,)))
B_TN = (((1,), (1,)), ((0,), (0,)))


def _split_bf16(x):
    hi = x.astype(BF16)
    return hi, (x - hi.astype(F32)).astype(BF16)


def _dot(a, b, dims, prec):
    if prec == "bf16":
        return lax.dot_general(a.astype(BF16), b.astype(BF16), dims, preferred_element_type=F32)
    a1, a2 = _split_bf16(a)
    b1, b2 = _split_bf16(b)
    dg = functools.partial(lax.dot_general, dimension_numbers=dims, preferred_element_type=F32)
    return dg(a1, b1) + (dg(a1, b2) + dg(a2, b1))


def _make_mm(prec, nn_dims, nt_dims, tn_dims):
    @jax.custom_vjp
    def nn(a, b):
        return _dot(a, b, nn_dims, prec)

    @jax.custom_vjp
    def nt(a, b):
        return _dot(a, b, nt_dims, prec)

    @jax.custom_vjp
    def tn(a, b):
        return _dot(a, b, tn_dims, prec)

    nn.defvjp(lambda a, b: (_dot(a, b, nn_dims, prec), (a, b)),
              lambda r, g: (_dot(g, r[1], nt_dims, prec), _dot(r[0], g, tn_dims, prec)))
    nt.defvjp(lambda a, b: (_dot(a, b, nt_dims, prec), (a, b)),
              lambda r, g: (_dot(g, r[1], nn_dims, prec), _dot(g, r[0], tn_dims, prec)))
    tn.defvjp(lambda a, b: (_dot(a, b, tn_dims, prec), (a, b)),
              lambda r, g: (_dot(r[1], g, nt_dims, prec), _dot(r[0], g, nn_dims, prec)))
    return nn, nt, tn


_NN_B, _NT_B, _TN_B = _make_mm("bf16", NN, NT, TN)
_BNN, _BNT, _BTN = _make_mm("bf16", B_NN, B_NT, B_TN)


@jax.custom_vjp
def _unit_lower_inverse(a):
    n = a.shape[-1]
    eye = (lax.broadcasted_iota(jnp.int32, a.shape, 1) == lax.broadcasted_iota(jnp.int32, a.shape, 2)).astype(F32)
    inv = eye - a
    pw = _dot(a, a, B_NN, "bf16x3")
    steps = n.bit_length() - 2
    for j in range(steps):
        inv = inv + _dot(inv, pw, B_NN, "bf16x3")
        if j + 1 < steps:
            pw = _dot(pw, pw, B_NN, "bf16x3")
    return inv


def _uli_fwd(a):
    inv = _unit_lower_inverse(a)
    return inv, inv


def _uli_bwd(inv, g):
    return (-_dot(_dot(inv, g, B_TN, "bf16"), inv, B_NT, "bf16"),)


_unit_lower_inverse.defvjp(_uli_fwd, _uli_bwd)


def _lower_ones(batch, n):
    shape = (batch, n, n)
    return (lax.broadcasted_iota(jnp.int32, shape, 1) >= lax.broadcasted_iota(jnp.int32, shape, 2)).astype(BF16)


@jax.custom_vjp
def _chunk_cumsum(g):
    tri = _lower_ones(g.shape[0], g.shape[1])
    g1, g2 = _split_bf16(g)
    g3 = (g - g1.astype(F32) - g2.astype(F32)).astype(BF16)
    dg = functools.partial(lax.dot_general, dimension_numbers=B_NN, preferred_element_type=F32)
    return dg(tri, g1) + (dg(tri, g2) + dg(tri, g3))


def _chunk_cumsum_bwd(_, ct):
    tri = _lower_ones(ct.shape[0], ct.shape[1])
    c1, c2 = _split_bf16(ct)
    dg = functools.partial(lax.dot_general, dimension_numbers=B_TN, preferred_element_type=F32)
    return (dg(tri, c1) + dg(tri, c2),)


_chunk_cumsum.defvjp(lambda g: (_chunk_cumsum(g), None), _chunk_cumsum_bwd)


def _expm1(x):
    small = x * (1.0 + x * (0.5 + x * (1.0 / 6 + x * (1.0 / 24 + x * (1.0 / 120 + x * (1.0 / 720))))))
    return jnp.where(jnp.abs(x) < 0.2, small, jnp.exp(x) - 1.0)


def _sigmoid(x):
    return 1.0 / (1.0 + jnp.exp(-x))


def _silu(x):
    return x * _sigmoid(x)


def _softplus(x):
    return jnp.maximum(x, 0.0) + jnp.log(1.0 + jnp.exp(-jnp.abs(x)))


def _rmsnorm(x, w):
    return x * lax.rsqrt(jnp.mean(x * x, axis=-1, keepdims=True) + EPS) * w


def _gated_norm(o, z, w):
    return o * lax.rsqrt(jnp.mean(o * o, axis=-1, keepdims=True) + EPS) * w * _silu(z)


def _lru_gates(xc, wa, ba, wx, bx, lam):
    r = _sigmoid(_NN_B(xc, wa) + ba)
    i = _sigmoid(_NN_B(xc, wx) + bx)
    log_a = -LRU_C * r * _softplus(-lam)
    a = jnp.exp(log_a)
    mult = jnp.sqrt(-_expm1(2.0 * log_a))
    return a, mult * (i * xc)


def _scan_forward(a, b, h0):
    rows = a.shape[0]
    row = lax.broadcasted_iota(jnp.int32, a.shape, 0)
    k = 1
    while k < rows:
        seen = row >= k
        b = jnp.where(seen, a * pltpu.roll(b, k, 0) + b, b)
        a = jnp.where(seen, a * pltpu.roll(a, k, 0), a)
        k *= 2
    return b + a * h0


def _scan_reverse(a, d, carry):
    rows = a.shape[0]
    row = lax.broadcasted_iota(jnp.int32, a.shape, 0)
    last = row == rows - 1
    c = jnp.where(last, 0.0, pltpu.roll(a, rows - 1, 0))
    d = d + jnp.where(last, carry, 0.0)
    k = 1
    while k < rows:
        seen = row < rows - k
        d = jnp.where(seen, d + c * pltpu.roll(d, rows - k, 0), d)
        c = jnp.where(seen, c * pltpu.roll(c, rows - k, 0), c)
        k *= 2
    return d


def _lane_pick(row, lane_index):
    lane = lax.broadcasted_iota(jnp.int32, row.shape, 1)
    return jnp.sum(jnp.where(lane == lane_index, row, 0.0), axis=-1, keepdims=True)


def _dn_prep(qc, kc, vc, ba, a_log_row, dt_row, head):
    q = _silu(qc)
    k = _silu(kc)
    v = _silu(vc)
    q = q * lax.rsqrt(jnp.sum(q * q, axis=-1, keepdims=True) + EPS) * (HEAD_DIM ** -0.5)
    k = k * lax.rsqrt(jnp.sum(k * k, axis=-1, keepdims=True) + EPS)
    beta = _sigmoid(_lane_pick(ba, head))
    g = -jnp.exp(_lane_pick(a_log_row, head)) * _softplus(_lane_pick(ba, HEADS + head) + _lane_pick(dt_row, head))
    return q, k, v, g, beta


def _dn_chunks_head(q, k, v, gcol, bcol):
    n, c, d = q.shape
    row = lax.broadcasted_iota(jnp.int32, (n, c, c), 1)
    col = lax.broadcasted_iota(jnp.int32, (n, c, c), 2)
    g_wide = jnp.broadcast_to(gcol, (n, c, d))
    b_wide = jnp.broadcast_to(bcol, (n, c, d))
    gc = _chunk_cumsum(g_wide)
    gc_rows = gc[:, :, :c]
    decay = jnp.exp(jnp.where(row >= col, gc_rows - jnp.swapaxes(gc_rows, 1, 2), -1e30))
    kb = k * b_wide
    eg = jnp.exp(gc)
    a = jnp.where(row > col, _BNT(kb, k) * decay, 0.0)
    tinv = _unit_lower_inverse(a)
    u = _BNN(tinv, v * b_wide)
    w = _BNN(tinv, kb * eg)
    attn = _BNT(q, k) * decay
    g_last = jnp.sum(g_wide, axis=1, keepdims=True)
    return u, w, attn, q * eg, k * jnp.exp(g_last - gc), jnp.exp(g_last)


def _dn_chunks(q, k, v, gcol, bcol, states):
    u, w, attn, qe, kdec, eglast = _dn_chunks_head(q, k, v, gcol, bcol)
    v_new = u - _BNN(w, states)
    o = _BNN(qe, states) + _BNN(attn, v_new)
    return (o, states * eglast + _BTN(kdec, v_new)), (w, attn, qe, kdec, eglast)


def _conv_taps(buf, head, cw, rows):
    acc = cw[0:1, :] * buf[head, pl.ds(5, rows), :]
    for j in range(1, 4):
        acc = acc + cw[j:j + 1, :] * buf[head, pl.ds(5 + j, rows), :]
    return acc


def _conv_backward(dbuf, dhead, xbuf, xhead, cw, dxc, rows):
    dbuf[dhead, pl.ds(0, rows), :] = dxc
    dx = cw[0:1, :] * dbuf[dhead, pl.ds(3, rows), :]
    for j in range(1, 4):
        dx = dx + cw[j:j + 1, :] * dbuf[dhead, pl.ds(3 - j, rows), :]
    dcw = jnp.concatenate(
        [jnp.sum(dxc * xbuf[xhead, pl.ds(5 + j, rows), :], axis=0, keepdims=True) for j in range(4)], axis=0)
    dbuf[dhead, pl.ds(rows, 8), :] = dbuf[dhead, pl.ds(0, 8), :]
    return dx, dcw


def _params(**kw):
    return pltpu.CompilerParams(vmem_limit_bytes=VMEM_LIMIT, **kw)


def _matmul(a, b, form, tm, tn, tk, name, add=None, out_dtype=F32, dep=None):
    if form == "nn":
        (m, kdim), (_, n) = a.shape, b.shape
        a_spec = pl.BlockSpec((tm, tk), lambda j, i, k: (i, k))
        b_spec = pl.BlockSpec((tk, tn), lambda j, i, k: (k, j))
        dims = NN
    elif form == "nt":
        (m, kdim), (n, _) = a.shape, b.shape
        a_spec = pl.BlockSpec((tm, tk), lambda j, i, k: (i, k))
        b_spec = pl.BlockSpec((tn, tk), lambda j, i, k: (j, k))
        dims = NT
    else:
        (kdim, m), (_, n) = a.shape, b.shape
        a_spec = pl.BlockSpec((tk, tm), lambda j, i, k: (k, i))
        b_spec = pl.BlockSpec((tk, tn), lambda j, i, k: (k, j))
        dims = TN
    assert m % tm == 0 and n % tn == 0 and kdim % tk == 0, (name, m, n, kdim, tm, tn, tk)
    ksteps = kdim // tk
    o_spec = pl.BlockSpec((tm, tn), lambda j, i, k: (i, j))
    has_add = add is not None
    extra = [] if dep is None else [dep]

    def body(*refs):
        a_ref, b_ref = refs[:2]
        c_ref = refs[2] if has_add else None
        o_ref, acc = refs[-2:]
        k = pl.program_id(2)

        @pl.when(k == 0)
        def _():
            acc[...] = c_ref[...] if has_add else jnp.zeros_like(acc)

        acc[...] += lax.dot_general(a_ref[...].astype(BF16), b_ref[...].astype(BF16), dims,
                                    preferred_element_type=F32)

        @pl.when(k == ksteps - 1)
        def _():
            o_ref[...] = acc[...].astype(o_ref.dtype)

    in_specs = [a_spec, b_spec] + ([o_spec] if has_add else []) + [pl.BlockSpec((8, HEAD_DIM), lambda j, i, k: (0, 0))
                                                                   for _ in extra]
    args = (a, b) + ((add,) if has_add else ()) + tuple(extra)
    return pl.pallas_call(
        body, name=name, grid=(n // tn, m // tm, ksteps), in_specs=in_specs, out_specs=o_spec,
        out_shape=jax.ShapeDtypeStruct((m, n), out_dtype), scratch_shapes=[pltpu.VMEM((tm, tn), F32)],
        compiler_params=_params(dimension_semantics=("parallel", "parallel", "arbitrary")),
    )(*args)


def _rmsnorm_fwd(x, w_row, name):
    s = x.shape[0]
    tb = min(TIME_BLOCK, s)

    def body(x_ref, w_ref, o_ref):
        o_ref[...] = _rmsnorm(x_ref[...], w_ref[...]).astype(BF16)

    return pl.pallas_call(
        body, name=name, grid=(s // tb,),
        in_specs=[pl.BlockSpec((tb, D_MODEL), lambda i: (i, 0)), pl.BlockSpec((1, D_MODEL), lambda i: (0, 0))],
        out_specs=pl.BlockSpec((tb, D_MODEL), lambda i: (i, 0)),
        out_shape=jax.ShapeDtypeStruct((s, D_MODEL), BF16), compiler_params=_params(),
    )(x, w_row)


def _rmsnorm_bwd(x, w_row, dh, dres, name):
    s = x.shape[0]
    tb = min(TIME_BLOCK, s)

    def body(x_ref, w_ref, dh_ref, dres_ref, dx_ref, dw_ref):
        _, vjp = jax.vjp(_rmsnorm, x_ref[...], w_ref[...])
        dx, dw = vjp(dh_ref[...])
        dx_ref[...] = dres_ref[...] + dx

        @pl.when(pl.program_id(0) == 0)
        def _():
            dw_ref[...] = jnp.zeros_like(dw_ref)

        dw_ref[...] += dw

    row = pl.BlockSpec((tb, D_MODEL), lambda i: (i, 0))
    vec = pl.BlockSpec((1, D_MODEL), lambda i: (0, 0))
    return pl.pallas_call(
        body, name=name, grid=(s // tb,), in_specs=[row, vec, row, row], out_specs=[row, vec],
        out_shape=[jax.ShapeDtypeStruct((s, D_MODEL), F32), jax.ShapeDtypeStruct((1, D_MODEL), F32)],
        compiler_params=_params(),
    )(x, w_row, dh, dres)


def _final_loss(x, w_row, target, name):
    s = x.shape[0]
    tb = min(TIME_BLOCK, s)

    def loss_fn(xv, wv, tv):
        err = _rmsnorm(xv, wv) - tv
        return 0.5 * jnp.sum(jnp.sum(err * err, axis=-1, keepdims=True), axis=0, keepdims=True) * (1.0 / D_MODEL)

    def body(x_ref, w_ref, t_ref, loss_ref, dx_ref, dw_ref):
        tv = t_ref[...]
        loss, vjp = jax.vjp(lambda xv, wv: loss_fn(xv, wv, tv), x_ref[...], w_ref[...])
        dx, dw = vjp(jnp.ones((1, 1), F32))
        dx_ref[...] = dx

        @pl.when(pl.program_id(0) == 0)
        def _():
            dw_ref[...] = jnp.zeros_like(dw_ref)
            loss_ref[...] = jnp.zeros_like(loss_ref)

        dw_ref[...] += dw
        loss_ref[...] += jnp.broadcast_to(loss, loss_ref.shape)

    row = pl.BlockSpec((tb, D_MODEL), lambda i: (i, 0))
    vec = pl.BlockSpec((1, D_MODEL), lambda i: (0, 0))
    return pl.pallas_call(
        body, name=name, grid=(s // tb,), in_specs=[row, vec, row],
        out_specs=[pl.BlockSpec((1, HEAD_DIM), lambda i: (0, 0)), row, vec],
        out_shape=[jax.ShapeDtypeStruct((1, HEAD_DIM), F32), jax.ShapeDtypeStruct((s, D_MODEL), F32),
                   jax.ShapeDtypeStruct((1, D_MODEL), F32)],
        compiler_params=_params(),
    )(x, w_row, target)


def _head_specs(tb, time_of):
    def col(off):
        return pl.BlockSpec((tb, HEAD_DIM), lambda t, h: (time_of(t), off + h))
    return col


def _vec_spec():
    return pl.BlockSpec((1, HEAD_DIM), lambda t, h: (0, h))


def _lru_fwd(proj, conv_w, conv_b, wa, ba, wx, bx, lam, nw, name):
    s = proj.shape[0]
    tb = min(TIME_BLOCK, s)
    nt = s // tb
    col = _head_specs(tb, lambda t: t)

    def body(x_ref, z_ref, cw_ref, cb_ref, wa_ref, ba_ref, wx_ref, bx_ref, lam_ref, nw_ref,
             y_ref, hs_ref, xbuf, hcar):
        t, h = pl.program_id(0), pl.program_id(1)

        @pl.when(t == 0)
        def _():
            xbuf[h, pl.ds(0, 8), :] = jnp.zeros((8, HEAD_DIM), F32)
            hcar[h] = jnp.zeros((8, HEAD_DIM), F32)

        xbuf[h, pl.ds(8, tb), :] = x_ref[...]
        xc = _conv_taps(xbuf, h, cw_ref[...], tb) + cb_ref[...]
        a, b = _lru_gates(xc, wa_ref[...], ba_ref[...], wx_ref[...], bx_ref[...], lam_ref[...])
        hs_ref[...] = _scan_forward(a, b, hcar[h, pl.ds(0, 1), :])
        hcar[h, pl.ds(0, 1), :] = hs_ref[pl.ds(tb - 1, 1), :]
        xbuf[h, pl.ds(0, 8), :] = xbuf[h, pl.ds(tb, 8), :]
        y_ref[...] = _gated_norm(hs_ref[...], z_ref[...], nw_ref[...]).astype(BF16)

    vec = _vec_spec()
    return pl.pallas_call(
        body, name=name, grid=(nt, HEADS),
        in_specs=[col(COL_LRU_X), col(COL_LRU_Z), pl.BlockSpec((4, HEAD_DIM), lambda t, h: (0, h)), vec,
                  pl.BlockSpec((None, HEAD_DIM, HEAD_DIM), lambda t, h: (h, 0, 0)), vec,
                  pl.BlockSpec((None, HEAD_DIM, HEAD_DIM), lambda t, h: (h, 0, 0)), vec, vec, vec],
        out_specs=[col(0), col(0)],
        out_shape=[jax.ShapeDtypeStruct((s, 2 * D_MODEL), BF16), jax.ShapeDtypeStruct((s, D_MODEL), F32)],
        scratch_shapes=[pltpu.VMEM((HEADS, tb + 8, HEAD_DIM), F32), pltpu.VMEM((HEADS, 8, HEAD_DIM), F32)],
        compiler_params=_params(dimension_semantics=("arbitrary", "arbitrary")),
    )(proj, proj, conv_w, conv_b, wa, ba, wx, bx, lam, nw)


def _halo_spec(tb, nt, off):
    per = tb // 8
    return pl.BlockSpec((8, HEAD_DIM), lambda t, h: (jnp.maximum((nt - 1 - t) * per - 1, 0), off + h))


def _lru_bwd(proj, hs, dy, conv_w, conv_b, wa, ba, wx, bx, lam, nw, name):
    s = proj.shape[0]
    tb = min(TIME_BLOCK, s)
    nt = s // tb
    col = _head_specs(tb, lambda t: nt - 1 - t)

    def body(x_ref, xh_ref, z_ref, hs_ref, hh_ref, dy_ref, cw_ref, cb_ref, wa_ref, ba_ref, wx_ref, bx_ref,
             lam_ref, nw_ref, dx_ref, dz_ref, dcw_ref, dcb_ref, dwa_ref, dba_ref, dwx_ref, dbx_ref, dlam_ref,
             dnw_ref, xbuf, hbuf, dbuf, gcar):
        t, h = pl.program_id(0), pl.program_id(1)
        first_block = t == nt - 1

        @pl.when(t == 0)
        def _():
            dbuf[h, pl.ds(tb, 8), :] = jnp.zeros((8, HEAD_DIM), F32)
            gcar[h] = jnp.zeros((8, HEAD_DIM), F32)
            dcw_ref[h] = jnp.zeros((4, HEAD_DIM), F32)
            dwa_ref[h] = jnp.zeros((HEAD_DIM, HEAD_DIM), F32)
            dwx_ref[h] = jnp.zeros((HEAD_DIM, HEAD_DIM), F32)
            for ref in (dcb_ref, dba_ref, dbx_ref, dlam_ref, dnw_ref):
                ref[h] = jnp.zeros((1, HEAD_DIM), F32)

        keep = jnp.where(first_block, 0.0, 1.0)
        xbuf[0, pl.ds(0, 8), :] = xh_ref[...] * keep
        xbuf[0, pl.ds(8, tb), :] = x_ref[...]
        hbuf[pl.ds(0, 8), :] = hh_ref[...] * keep
        hbuf[pl.ds(8, tb), :] = hs_ref[...]
        cw = cw_ref[...]
        xc = _conv_taps(xbuf, 0, cw, tb) + cb_ref[...]
        (a, _), gates_vjp = jax.vjp(_lru_gates, xc, wa_ref[...], ba_ref[...], wx_ref[...], bx_ref[...], lam_ref[...])
        _, norm_vjp = jax.vjp(_gated_norm, hs_ref[...], z_ref[...], nw_ref[...])
        dh, dz, dnw = norm_vjp(dy_ref[...])
        dz_ref[...] = dz.astype(dz_ref.dtype)
        g = _scan_reverse(a, dh, gcar[h, pl.ds(0, 1), :])
        gcar[h, pl.ds(0, 1), :] = a[0:1, :] * g[0:1, :]
        dxc, dwa, dba, dwx, dbx, dlam = gates_vjp((g * hbuf[pl.ds(7, tb), :], g))
        dx, dcw = _conv_backward(dbuf, h, xbuf, 0, cw, dxc, tb)
        dx_ref[...] = dx.astype(dx_ref.dtype)
        dcw_ref[h] += dcw
        dcb_ref[h] += jnp.sum(dxc, axis=0, keepdims=True)
        dwa_ref[h] += dwa
        dwx_ref[h] += dwx
        dba_ref[h] += dba
        dbx_ref[h] += dbx
        dlam_ref[h] += dlam
        dnw_ref[h] += dnw

    vec = _vec_spec()
    mat = pl.BlockSpec((None, HEAD_DIM, HEAD_DIM), lambda t, h: (h, 0, 0))

    def whole(shape):
        return pl.BlockSpec(shape, lambda t, h: (0,) * len(shape))

    head_vec = jax.ShapeDtypeStruct((HEADS, 1, HEAD_DIM), F32)
    head_mat = jax.ShapeDtypeStruct((HEADS, HEAD_DIM, HEAD_DIM), F32)
    return pl.pallas_call(
        body, name=name, grid=(nt, HEADS),
        in_specs=[col(COL_LRU_X), _halo_spec(tb, nt, COL_LRU_X), col(COL_LRU_Z), col(0), _halo_spec(tb, nt, 0), col(0),
                  pl.BlockSpec((4, HEAD_DIM), lambda t, h: (0, h)), vec, mat, vec, mat, vec, vec, vec],
        out_specs=[col(0), col(0), whole((HEADS, 4, HEAD_DIM)), whole((HEADS, 1, HEAD_DIM)),
                   whole((HEADS, HEAD_DIM, HEAD_DIM)), whole((HEADS, 1, HEAD_DIM)),
                   whole((HEADS, HEAD_DIM, HEAD_DIM)), whole((HEADS, 1, HEAD_DIM)), whole((HEADS, 1, HEAD_DIM)),
                   whole((HEADS, 1, HEAD_DIM))],
        out_shape=[jax.ShapeDtypeStruct((s, D_MODEL), BF16), jax.ShapeDtypeStruct((s, D_MODEL), BF16),
                   jax.ShapeDtypeStruct((HEADS, 4, HEAD_DIM), F32), head_vec, head_mat, head_vec, head_mat, head_vec,
                   head_vec, head_vec],
        scratch_shapes=[pltpu.VMEM((1, tb + 8, HEAD_DIM), F32), pltpu.VMEM((tb + 8, HEAD_DIM), F32),
                        pltpu.VMEM((HEADS, tb + 8, HEAD_DIM), F32), pltpu.VMEM((HEADS, 8, HEAD_DIM), F32)],
        compiler_params=_params(dimension_semantics=("arbitrary", "arbitrary")),
    )(proj, proj, proj, hs, hs, dy, conv_w, conv_b, wa, ba, wx, bx, lam, nw)


def _dn_fwd(proj, y, conv_w, a_log_row, dt_row, nw, name):
    s = proj.shape[0]
    tb = min(TIME_BLOCK, s)
    nt = s // tb
    nchunk = tb // CHUNK
    col = _head_specs(tb, lambda t: t)

    def body(q_ref, k_ref, v_ref, z_ref, ba_ref, cwq_ref, cwk_ref, cwv_ref, al_ref, dt_ref, nw_ref, y_in_ref,
             y_ref, o_ref, st_ref, xbuf, state):
        t, h = pl.program_id(0), pl.program_id(1)

        @pl.when(t == 0)
        def _():
            for i in range(3):
                xbuf[3 * h + i, pl.ds(0, 8), :] = jnp.zeros((8, HEAD_DIM), F32)
            state[h] = jnp.zeros((HEAD_DIM, HEAD_DIM), F32)

        conv = []
        for i, (ref, cw_ref) in enumerate(((q_ref, cwq_ref), (k_ref, cwk_ref), (v_ref, cwv_ref))):
            xbuf[3 * h + i, pl.ds(8, tb), :] = ref[...]
            conv.append(_conv_taps(xbuf, 3 * h + i, cw_ref[...], tb))
            xbuf[3 * h + i, pl.ds(0, 8), :] = xbuf[3 * h + i, pl.ds(tb, 8), :]
        q, k, v, g, beta = _dn_prep(conv[0], conv[1], conv[2], ba_ref[...], al_ref[...], dt_ref[...], h)
        def chunks(a):
            return a.reshape(nchunk, CHUNK, a.shape[-1])

        u, w, attn, qe, kdec, eglast = _dn_chunks_head(chunks(q), chunks(k), chunks(v), chunks(g), chunks(beta))
        st = state[h]
        for c in range(nchunk):
            st_ref[c] = st
            v_new = u[c] - _NN_B(w[c], st)
            o_ref[pl.ds(c * CHUNK, CHUNK), :] = _NN_B(qe[c], st) + _NN_B(attn[c], v_new)
            st = st * eglast[c] + _TN_B(kdec[c], v_new)
        state[h] = st
        y_ref[...] = _gated_norm(o_ref[...], z_ref[...], nw_ref[...]).astype(BF16)

    def cw_spec(off):
        return pl.BlockSpec((4, HEAD_DIM), lambda t, h: (0, off + h))

    row128 = pl.BlockSpec((1, HEAD_DIM), lambda t, h: (0, 0))
    return pl.pallas_call(
        body, name=name, grid=(nt, HEADS),
        in_specs=[col(COL_Q), col(COL_K), col(COL_V), col(COL_DN_Z),
                  pl.BlockSpec((tb, HEAD_DIM), lambda t, h: (t, COL_BA)),
                  cw_spec(0), cw_spec(HEADS), cw_spec(2 * HEADS), row128, row128, row128,
                  pl.BlockSpec(memory_space=pl.ANY)],
        out_specs=[col(HEADS), col(0), pl.BlockSpec((None, nchunk, HEAD_DIM, HEAD_DIM), lambda t, h: (h, t, 0, 0))],
        out_shape=[jax.ShapeDtypeStruct((s, 2 * D_MODEL), BF16), jax.ShapeDtypeStruct((s, D_MODEL), F32),
                   jax.ShapeDtypeStruct((HEADS, s // CHUNK, HEAD_DIM, HEAD_DIM), F32)],
        input_output_aliases={11: 0},
        scratch_shapes=[pltpu.VMEM((3 * HEADS, tb + 8, HEAD_DIM), F32), pltpu.VMEM((HEADS, HEAD_DIM, HEAD_DIM), F32)],
        compiler_params=_params(dimension_semantics=("arbitrary", "arbitrary")),
    )(proj, proj, proj, proj, proj, conv_w, conv_w, conv_w, a_log_row, dt_row, nw, y)


def _dn_bwd(proj, o, states, dy, conv_w, a_log_row, dt_row, nw, name):
    s = proj.shape[0]
    tb = min(TIME_BLOCK, s)
    nt = s // tb
    nchunk = tb // CHUNK
    col = _head_specs(tb, lambda t: nt - 1 - t)

    def body(q_ref, qh_ref, k_ref, kh_ref, v_ref, vh_ref, z_ref, ba_ref, o_ref, st_ref, dy_ref,
             cwq_ref, cwk_ref, cwv_ref, al_ref, dt_ref, nw_ref,
             dq_ref, dk_ref, dv_ref, dz_ref, dba_ref, dcw_ref, dal_ref, ddt_ref, dnw_ref,
             xbuf, dbuf, dstate, dst_s):
        t, h = pl.program_id(0), pl.program_id(1)
        first_block = t == nt - 1

        @pl.when(t == 0)
        def _():
            for i in range(3):
                dbuf[3 * h + i, pl.ds(tb, 8), :] = jnp.zeros((8, HEAD_DIM), F32)
                dcw_ref[3 * h + i] = jnp.zeros((4, HEAD_DIM), F32)
            dstate[h] = jnp.zeros((HEAD_DIM, HEAD_DIM), F32)

        @pl.when((t == 0) & (h == 0))
        def _():
            for ref in (dal_ref, ddt_ref, dnw_ref):
                ref[...] = jnp.zeros_like(ref)

        keep = jnp.where(first_block, 0.0, 1.0)
        cws = (cwq_ref[...], cwk_ref[...], cwv_ref[...])
        conv = []
        for i, (ref, halo) in enumerate(((q_ref, qh_ref), (k_ref, kh_ref), (v_ref, vh_ref))):
            xbuf[i, pl.ds(0, 8), :] = halo[...] * keep
            xbuf[i, pl.ds(8, tb), :] = ref[...]
            conv.append(_conv_taps(xbuf, i, cws[i], tb))
        (q, k, v, g, beta), prep_vjp = jax.vjp(
            lambda qc, kc, vc, ba, al, dt: _dn_prep(qc, kc, vc, ba, al, dt, h),
            conv[0], conv[1], conv[2], ba_ref[...], al_ref[...], dt_ref[...])
        _, norm_vjp = jax.vjp(_gated_norm, o_ref[...], z_ref[...], nw_ref[...])
        do, dz, dnw = norm_vjp(dy_ref[...])
        dz_ref[...] = dz.astype(dz_ref.dtype)
        dnw_ref[...] += dnw

        def chunks(a):
            return a.reshape(nchunk, CHUNK, a.shape[-1])

        do = chunks(do)
        _, chunks_vjp, (w, attn, qe, kdec, eglast) = jax.vjp(
            _dn_chunks, chunks(q), chunks(k), chunks(v), chunks(g), chunks(beta), st_ref[...], has_aux=True)
        from_o = _dot(attn, do, B_TN, "bf16")
        from_qe = _dot(qe, do, B_TN, "bf16")
        dst = dstate[h]
        for c in reversed(range(nchunk)):
            dst_s[c] = dst
            dv_new = from_o[c] + _dot(kdec[c], dst, NN, "bf16")
            dst = dst * eglast[c] + from_qe[c] - _dot(w[c], dv_new, TN, "bf16")
        dstate[h] = dst
        dq, dk, dv, dg, db, _ = chunks_vjp((do, dst_s[...]))

        def rows(a):
            return a.reshape(tb, a.shape[-1])

        dqc, dkc, dvc, dba, dal, ddt = prep_vjp((rows(dq), rows(dk), rows(dv), rows(dg), rows(db)))
        for i, (dxc, out) in enumerate(((dqc, dq_ref), (dkc, dk_ref), (dvc, dv_ref))):
            dx, dcw = _conv_backward(dbuf, 3 * h + i, xbuf, i, cws[i], dxc, tb)
            out[...] = dx.astype(out.dtype)
            dcw_ref[3 * h + i] += dcw
        dal_ref[...] += dal
        ddt_ref[...] += ddt

        @pl.when(h == 0)
        def _():
            dba_ref[...] = dba.astype(dba_ref.dtype)

        @pl.when(h > 0)
        def _():
            dba_ref[...] += dba.astype(dba_ref.dtype)

    def cw_spec(off):
        return pl.BlockSpec((4, HEAD_DIM), lambda t, h: (0, off + h))

    def whole(shape):
        return pl.BlockSpec(shape, lambda t, h: (0,) * len(shape))

    row128 = whole((1, HEAD_DIM))
    blk = (tb, HEAD_DIM)
    act = jax.ShapeDtypeStruct((s, D_MODEL), BF16)
    row_out = jax.ShapeDtypeStruct((1, HEAD_DIM), F32)
    return pl.pallas_call(
        body, name=name, grid=(nt, HEADS),
        in_specs=[col(COL_Q), _halo_spec(tb, nt, COL_Q), col(COL_K), _halo_spec(tb, nt, COL_K),
                  col(COL_V), _halo_spec(tb, nt, COL_V), col(COL_DN_Z),
                  pl.BlockSpec(blk, lambda t, h: (nt - 1 - t, COL_BA)), col(0),
                  pl.BlockSpec((None, nchunk, HEAD_DIM, HEAD_DIM), lambda t, h: (h, nt - 1 - t, 0, 0)), col(HEADS),
                  cw_spec(0), cw_spec(HEADS), cw_spec(2 * HEADS), row128, row128, row128],
        out_specs=[col(0), col(0), col(0), col(0), pl.BlockSpec(blk, lambda t, h: (nt - 1 - t, 0)),
                   whole((3 * HEADS, 4, HEAD_DIM)), row128, row128, row128],
        out_shape=[act, act, act, act, jax.ShapeDtypeStruct((s, HEAD_DIM), F32),
                   jax.ShapeDtypeStruct((3 * HEADS, 4, HEAD_DIM), F32), row_out, row_out, row_out],
        scratch_shapes=[pltpu.VMEM((3, tb + 8, HEAD_DIM), F32), pltpu.VMEM((3 * HEADS, tb + 8, HEAD_DIM), F32),
                        pltpu.VMEM((HEADS, HEAD_DIM, HEAD_DIM), F32), pltpu.VMEM((nchunk, HEAD_DIM, HEAD_DIM), F32)],
        compiler_params=_params(dimension_semantics=("arbitrary", "arbitrary")),
    )(proj, proj, proj, proj, proj, proj, proj, proj, o, states, dy, conv_w, conv_w, conv_w, a_log_row, dt_row, nw)


def _mesh_position():
    x, y, c = lax.axis_index("x"), lax.axis_index("y"), lax.axis_index("c")
    return x, y, c, 4 * x + 2 * y + c


def _peer(k, x, y, c):
    px = 1 - x if k & 4 else x
    py = 1 - y if k & 2 else y
    pc = 1 - c if k & 1 else c
    return (px, py, pc), 4 * px + 2 * py + pc


def _exchange_copies(ins, lands, scatter, send_sems, recv_sems, receives=True):
    x, y, c, me = _mesh_position()
    sends, recvs = [], []
    for i, (src, land) in enumerate(zip(ins, lands)):
        for k in range(1, N_DEV):
            peer, peer_id = _peer(k, x, y, c)
            sem = i * (N_DEV - 1) + k - 1
            for dst, out in ((me, sends), (peer_id, recvs)) if receives else ((me, sends),):
                out.append(pltpu.make_async_remote_copy(
                    src_ref=src.at[peer_id] if scatter[i] else src, dst_ref=land.at[dst],
                    send_sem=send_sems.at[sem], recv_sem=recv_sems.at[sem],
                    device_id=peer, device_id_type=pl.DeviceIdType.MESH))
    return sends, recvs


def _landing_shape(a, scatter):
    return a.shape if scatter else (N_DEV,) + a.shape


def _direct_exchange(arrays, scatter, name):
    n = len(arrays)
    out_shapes = [jax.ShapeDtypeStruct(_landing_shape(a, sc), a.dtype) for a, sc in zip(arrays, scatter)]

    def body(*refs):
        ins, outs = refs[:n], refs[n:2 * n]
        send_sems, recv_sems, local_sems = refs[2 * n:]
        me = _mesh_position()[3]
        local = [pltpu.make_async_copy(ins[i].at[me] if scatter[i] else ins[i], outs[i].at[me], local_sems.at[i])
                 for i in range(n)]
        sends, recvs = _exchange_copies(ins, outs, scatter, send_sems, recv_sems)
        for cp in local + sends:
            cp.start()
        for cp in recvs:
            cp.wait_recv()
        for cp in sends:
            cp.wait_send()
        for cp in local:
            cp.wait()

    hbm = pl.BlockSpec(memory_space=pl.ANY)
    return pl.pallas_call(
        body, name=name, in_specs=[hbm] * n, out_specs=[hbm] * n, out_shape=out_shapes,
        scratch_shapes=[pltpu.SemaphoreType.DMA((n * (N_DEV - 1),)), pltpu.SemaphoreType.DMA((n * (N_DEV - 1),)),
                        pltpu.SemaphoreType.DMA((n,))],
    )(*arrays)


_HBM = pl.BlockSpec(memory_space=pltpu.HBM)
_SEM = pl.BlockSpec(memory_space=pltpu.SEMAPHORE)
_DATAFLOW = pltpu.SideEffectType.DATAFLOW_SIDE_EFFECTING


def _exchange_start(arrays, scatter, name):
    n = len(arrays)
    srcs = [pltpu.with_memory_space_constraint(a, pltpu.HBM) for a in arrays]
    lands = [pltpu.with_memory_space_constraint(lax.empty(_landing_shape(a, sc), a.dtype), pltpu.HBM)
             for a, sc in zip(arrays, scatter)]
    nsem = n * (N_DEV - 1)

    def body(*refs):
        ins, zones = refs[:n], refs[n:2 * n]
        send_sems, recv_sems = refs[2 * n], refs[2 * n + 1]
        token = refs[-1]
        sends, _ = _exchange_copies(ins, zones, scatter, send_sems, recv_sems, receives=False)
        for cp in sends:
            cp.start()
        token[...] = jnp.zeros_like(token)

    res = pl.pallas_call(
        body, name=name,
        out_shape=(pltpu.SemaphoreType.DMA((nsem,)), pltpu.SemaphoreType.DMA((nsem,)),
                   *[pltpu.HBM(a.shape, a.dtype) for a in srcs + lands], jax.ShapeDtypeStruct((8, HEAD_DIM), F32)),
        in_specs=[_HBM] * (2 * n),
        out_specs=(_SEM, _SEM, *[_HBM] * (2 * n), pl.BlockSpec(memory_space=pltpu.VMEM)),
        input_output_aliases={i: 2 + i for i in range(2 * n)},
        compiler_params=pltpu.CompilerParams(has_side_effects=_DATAFLOW),
    )(*srcs, *lands)
    return dict(sems=res[:2], srcs=res[2:2 + n], lands=res[2 + n:2 + 2 * n], token_block=res[-1],
                token=res[-1][0, 0], scatter=scatter)


def _exchange_wait(started, after, name):
    scatter = started["scatter"]
    n = len(scatter)

    def body(*refs):
        ins, zones = refs[:n], refs[n:2 * n]
        send_sems, recv_sems = refs[2 * n], refs[2 * n + 1]
        sends, recvs = _exchange_copies(ins, zones, scatter, send_sems, recv_sems)
        for cp in sends:
            cp.wait_send()
        for cp in recvs:
            cp.wait_recv()

    thru = list(started["srcs"]) + list(started["lands"])
    res = pl.pallas_call(
        body, name=name, out_shape=[pltpu.HBM(a.shape, a.dtype) for a in thru],
        in_specs=[_HBM] * (2 * n) + [_SEM, _SEM, pl.BlockSpec(memory_space=pl.ANY)], out_specs=[_HBM] * (2 * n),
        input_output_aliases={i: i for i in range(2 * n)},
        compiler_params=pltpu.CompilerParams(has_side_effects=_DATAFLOW),
    )(*thru, *started["sems"], after)
    me = 4 * lax.axis_index("x") + 2 * lax.axis_index("y") + lax.axis_index("c")
    out = []
    for src, got, sc in zip(res[:n], res[n:], scatter):
        own = lax.dynamic_index_in_dim(src, me, 0, keepdims=False) if sc else src
        out.append(lax.dynamic_update_index_in_dim(got, own, me, 0))
    return out


def _adamw(parts, w, m, v, name, rows_per_step, row_offset=0, into=None):
    rows, cols = parts.shape[1:]
    tr = min(rows_per_step, rows)
    assert rows % tr == 0 and row_offset % tr == 0, (name, rows, tr, row_offset)
    first = row_offset // tr
    c1 = 1.0 / (1.0 - ADAM_B1 ** ADAM_STEP)
    c2 = 1.0 / (1.0 - ADAM_B2 ** ADAM_STEP)

    def body(p_ref, w_ref, m_ref, v_ref, *rest):
        g_ref, d_ref, nm_ref, nv_ref = rest[-4:]
        g = p_ref[0].astype(F32)
        for d in range(1, N_DEV):
            g = g + p_ref[d].astype(F32)
        nm = ADAM_B1 * m_ref[...] + (1.0 - ADAM_B1) * g
        nv = ADAM_B2 * v_ref[...] + (1.0 - ADAM_B2) * (g * g)
        g_ref[...] = g
        nm_ref[...] = nm
        nv_ref[...] = nv
        d_ref[...] = -ADAM_LR * ((nm * c1) / (jnp.sqrt(nv * c2) + ADAM_EPS) + ADAM_WD * w_ref[...])

    blk = pl.BlockSpec((tr, cols), lambda i: (i + first, 0))
    shape = jax.ShapeDtypeStruct(w.shape, F32)
    prior = [] if into is None else list(into)
    return pl.pallas_call(
        body, name=name, grid=(rows // tr,),
        in_specs=[pl.BlockSpec((N_DEV, tr, cols), lambda i: (0, i, 0)), blk, blk, blk]
        + [pl.BlockSpec(memory_space=pl.ANY)] * len(prior),
        out_specs=[blk] * 4, out_shape=[shape] * 4,
        input_output_aliases={4 + j: j for j in range(len(prior))}, compiler_params=_params(),
    )(parts, w, m, v, *prior)


_LAYERED = ("norm_w", "lru_conv_b", "lru_wa", "lru_ba", "lru_wx", "lru_bx", "lru_lambda", "lru_norm_w",
            "dn_A_log", "dn_dt_bias", "dn_norm_w")
_WEIGHTS = ("norm_w", "w_in", "lru_conv_w", "lru_conv_b", "lru_wa", "lru_ba", "lru_wx", "lru_bx", "lru_lambda",
            "lru_norm_w", "dn_conv_w", "dn_A_log", "dn_dt_bias", "dn_norm_w", "w_out", "final_norm_w")


def _pack_layer(tree, layer, tail=()):
    rows = []
    for name in _LAYERED:
        a = tree[name][layer]
        if a.shape[-1] == HEADS:
            a = jnp.pad(a, (0, HEAD_DIM - HEADS))
        rows.append(a.reshape(-1, HEAD_DIM))
    rows += [t.reshape(-1, HEAD_DIM) for t in tail]
    packed = jnp.concatenate(rows, axis=0)
    return jnp.pad(packed, ((0, (-packed.shape[0]) % 8), (0, 0)))


def _unpack_layer(packed, like):
    out, at = {}, 0
    for name in _LAYERED:
        shape = like[name].shape[1:]
        if shape[-1] == HEADS:
            n = 1
            out[name] = packed[at, :HEADS]
        else:
            n = like[name][0].size // HEAD_DIM
            out[name] = packed[at:at + n].reshape(shape)
        at += n
    return out, at


def _heads_to_channels(a):
    return jnp.transpose(a, (1, 0, 2)).reshape(a.shape[1], HEADS * HEAD_DIM)


def kernel(x, norm_w, w_in, lru_conv_w, lru_conv_b, lru_wa, lru_ba, lru_wx, lru_bx, lru_lambda, lru_norm_w, dn_conv_w, dn_A_log, dn_dt_bias, dn_norm_w, w_out, final_norm_w, loss_target, m_norm_w, m_w_in, m_lru_conv_w, m_lru_conv_b, m_lru_wa, m_lru_ba, m_lru_wx, m_lru_bx, m_lru_lambda, m_lru_norm_w, m_dn_conv_w, m_dn_A_log, m_dn_dt_bias, m_dn_norm_w, m_w_out, m_final_norm_w, v_norm_w, v_w_in, v_lru_conv_w, v_lru_conv_b, v_lru_wa, v_lru_ba, v_lru_wx, v_lru_bx, v_lru_lambda, v_lru_norm_w, v_dn_conv_w, v_dn_A_log, v_dn_dt_bias, v_dn_norm_w, v_w_out, v_final_norm_w):
    weights = dict(norm_w=norm_w, w_in=w_in, lru_conv_w=lru_conv_w, lru_conv_b=lru_conv_b, lru_wa=lru_wa,
                   lru_ba=lru_ba, lru_wx=lru_wx, lru_bx=lru_bx, lru_lambda=lru_lambda, lru_norm_w=lru_norm_w,
                   dn_conv_w=dn_conv_w, dn_A_log=dn_A_log, dn_dt_bias=dn_dt_bias, dn_norm_w=dn_norm_w,
                   w_out=w_out, final_norm_w=final_norm_w)
    mom_m = dict(norm_w=m_norm_w, w_in=m_w_in, lru_conv_w=m_lru_conv_w, lru_conv_b=m_lru_conv_b, lru_wa=m_lru_wa,
                 lru_ba=m_lru_ba, lru_wx=m_lru_wx, lru_bx=m_lru_bx, lru_lambda=m_lru_lambda,
                 lru_norm_w=m_lru_norm_w, dn_conv_w=m_dn_conv_w, dn_A_log=m_dn_A_log, dn_dt_bias=m_dn_dt_bias,
                 dn_norm_w=m_dn_norm_w, w_out=m_w_out, final_norm_w=m_final_norm_w)
    mom_v = dict(norm_w=v_norm_w, w_in=v_w_in, lru_conv_w=v_lru_conv_w, lru_conv_b=v_lru_conv_b, lru_wa=v_lru_wa,
                 lru_ba=v_lru_ba, lru_wx=v_lru_wx, lru_bx=v_lru_bx, lru_lambda=v_lru_lambda,
                 lru_norm_w=v_lru_norm_w, dn_conv_w=v_dn_conv_w, dn_A_log=v_dn_A_log, dn_dt_bias=v_dn_dt_bias,
                 dn_norm_w=v_dn_norm_w, w_out=v_w_out, final_norm_w=v_final_norm_w)
    depth = norm_w.shape[0]
    xs = x[0]
    s = xs.shape[0]
    tm = min(512, s)

    assert depth >= 2, depth

    def row(a):
        return a.reshape(1, -1)

    def pad_row(a):
        return jnp.pad(a, (0, HEAD_DIM - a.shape[0])).reshape(1, HEAD_DIM)

    def full_w_in(g):
        w = jnp.transpose(g, (1, 2, 0, 3)).reshape(g.shape[1], D_MODEL, D_IN)
        return jnp.pad(w, ((0, 0), (0, 0), (0, D_IN_PAD - D_IN)))

    g_win0, g_lcw, g_dcw = _direct_exchange([w_in[:1].astype(BF16), lru_conv_w, dn_conv_w], [False] * 3,
                                            "gather_first")
    rest = _exchange_start([w_in[1:].astype(BF16), w_out.astype(BF16)], [False] * 2, "gather_rest_start")
    win = [full_w_in(g_win0)[0]]
    wout = None
    lcw = jnp.transpose(g_lcw, (1, 2, 0, 3)).reshape(depth, 4, D_MODEL)
    dcw = jnp.transpose(g_dcw, (1, 2, 0, 3)).reshape(depth, 4, 3 * D_MODEL)

    saved = []
    cur = xs
    for l in range(depth):
        nw_row = row(norm_w[l]) + rest["token"] if l == 0 else row(norm_w[l])
        hn = _rmsnorm_fwd(cur, nw_row, f"norm_fwd_{l}")
        proj = _matmul(hn, win[l], "nn", tm, 896, D_MODEL, f"in_proj_{l}")
        y_lru, hs = _lru_fwd(proj, lcw[l], row(lru_conv_b[l]), lru_wa[l], row(lru_ba[l]), lru_wx[l], row(lru_bx[l]),
                             row(lru_lambda[l]), row(lru_norm_w[l]), f"lru_fwd_{l}")
        ycat, o_dn, states = _dn_fwd(proj, y_lru, dcw[l], pad_row(dn_A_log[l]), pad_row(dn_dt_bias[l]),
                                     row(dn_norm_w[l]), f"dn_fwd_{l}")
        if l == 0:
            g_win_rest, g_wout = _exchange_wait(rest, ycat, "gather_rest_wait")
            win += list(full_w_in(g_win_rest))
            wout = jnp.transpose(g_wout, (1, 0, 2, 3)).reshape(depth, 2 * D_MODEL, D_MODEL)
        nxt = _matmul(ycat, wout[l], "nn", tm, D_MODEL, 2 * D_MODEL, f"out_proj_{l}", add=cur)
        saved.append((cur, hn, proj, hs, o_dn, states, ycat))
        cur = nxt
    loss_part, dx, d_final = _final_loss(cur, row(final_norm_w), loss_target[0], "final_loss")

    def win_slots(g):
        return jnp.transpose(g.reshape(D_MODEL, N_DEV, D_IN // N_DEV), (1, 0, 2))

    def wout_slots(g):
        return g.reshape(N_DEV, 2 * D_MODEL // N_DEV, D_MODEL)

    grads = {k: [None] * depth for k in _WEIGHTS if k not in ("final_norm_w", "w_in", "w_out")}
    started = {}
    token = None
    for l in reversed(range(depth)):
        x_in, hn, proj, hs, o_dn, states, ycat = saved[l]
        dy = _matmul(dx, wout[l], "nt", tm, D_MODEL, D_MODEL, f"out_proj_dy_{l}")
        g_wout_l = _matmul(ycat, dx, "tn", D_MODEL, D_MODEL, tm, f"out_proj_dw_{l}", out_dtype=BF16)
        if l == 0:
            started["w_out_0"] = _exchange_start([wout_slots(g_wout_l)], [True], "exchange_w_out_0_start")
            token = token + started["w_out_0"]["token"]
        cb_row = row(lru_conv_b[l]) if token is None else row(lru_conv_b[l]) + token
        (dlx, dlz, g_lcw, g_lcb, g_wa, g_ba, g_wx, g_bx, g_lam, g_lnw) = _lru_bwd(
            proj, hs, dy, lcw[l], cb_row, lru_wa[l], row(lru_ba[l]), lru_wx[l], row(lru_bx[l]),
            row(lru_lambda[l]), row(lru_norm_w[l]), f"lru_bwd_{l}")
        (dq, dk, dv, ddz, dba, g_dcw3, g_al, g_dt, g_dnw) = _dn_bwd(
            proj, o_dn, states, dy, dcw[l], pad_row(dn_A_log[l]), pad_row(dn_dt_bias[l]), row(dn_norm_w[l]),
            f"dn_bwd_{l}")
        dproj = jnp.concatenate([dlx, dlz, dq, dk, dv, ddz, dba.astype(BF16)], axis=1)
        g_win_l = _matmul(hn, dproj, "tn", D_MODEL, 896, tm, f"in_proj_dw_{l}", out_dtype=BF16)[:, :D_IN]
        dep = None
        if l == 0:
            started[0] = _exchange_start([win_slots(g_win_l)], [True], "exchange_0_start")
            dep = started[0]["token_block"]
        dh = _matmul(dproj, win[l], "nt", tm, D_MODEL, 896, f"in_proj_dh_{l}", dep=dep)
        dx, g_nw = _rmsnorm_bwd(x_in, row(norm_w[l]), dh, dx, f"norm_bwd_{l}")
        grads["norm_w"][l] = g_nw.reshape(D_MODEL)
        grads["lru_conv_w"][l] = _heads_to_channels(g_lcw)
        grads["lru_conv_b"][l] = g_lcb.reshape(D_MODEL)
        grads["lru_wa"][l] = g_wa
        grads["lru_ba"][l] = g_ba.reshape(D_MODEL)
        grads["lru_wx"][l] = g_wx
        grads["lru_bx"][l] = g_bx.reshape(D_MODEL)
        grads["lru_lambda"][l] = g_lam.reshape(D_MODEL)
        grads["lru_norm_w"][l] = g_lnw.reshape(D_MODEL)
        g_dcw3 = g_dcw3.reshape(HEADS, 3, 4, HEAD_DIM)
        grads["dn_conv_w"][l] = jnp.concatenate([_heads_to_channels(g_dcw3[:, i]) for i in range(3)], axis=1)
        grads["dn_A_log"][l] = g_al[0, :HEADS]
        grads["dn_dt_bias"][l] = g_dt[0, :HEADS]
        grads["dn_norm_w"][l] = g_dnw.reshape(HEAD_DIM)
        if l > 0:
            tail = (d_final, loss_part) if l == depth - 1 else ()
            started[l] = _exchange_start([win_slots(g_win_l), wout_slots(g_wout_l), _pack_layer(grads, l, tail)],
                                         [True, True, False], f"exchange_{l}_start")
            token = started[l]["token"]

    def conv_slots(a):
        dd, r, cc = a.shape
        return jnp.transpose(a.reshape(dd, r, N_DEV, cc // N_DEV), (2, 0, 1, 3))

    small = _exchange_start(
        [conv_slots(jnp.stack(grads["lru_conv_w"])), conv_slots(jnp.stack(grads["dn_conv_w"])), _pack_layer(grads, 0)],
        [True, True, False], "exchange_small_start")

    new = {}
    flat_in = (depth * D_MODEL, D_IN // N_DEV)
    flat_out = (depth * 2 * D_MODEL // N_DEV, D_MODEL)
    zero_row = jnp.zeros((1, HEAD_DIM), F32)

    def adamw_pack(parts, layer):
        tails = [(t, zero_row) if layer == depth - 1 else () for t in (final_norm_w, m_final_norm_w, v_final_norm_w)]
        return _adamw(parts, _pack_layer(weights, layer, tails[0]), _pack_layer(mom_m, layer, tails[1]),
                      _pack_layer(mom_v, layer, tails[2]), f"adamw_small_{layer}", parts.shape[1])

    def adamw_w_in(parts, layer, into):
        return _adamw(parts, w_in.reshape(flat_in), m_w_in.reshape(flat_in), v_w_in.reshape(flat_in),
                      f"adamw_w_in_{layer}", 256, layer * D_MODEL, into)

    def adamw_w_out(parts, layer, into):
        return _adamw(parts, w_out.reshape(flat_out), m_w_out.reshape(flat_out), v_w_out.reshape(flat_out),
                      f"adamw_w_out_{layer}", 256, layer * flat_out[0] // depth, into)

    acc_in = acc_out = None
    packs = [None] * depth
    after = small["token_block"]
    for l in reversed(range(1, depth)):
        r_win, r_wout, r_pack = _exchange_wait(started[l], after, f"exchange_{l}_wait")
        acc_in = adamw_w_in(r_win, l, acc_in)
        acc_out = adamw_w_out(r_wout, l, acc_out)
        packs[l] = adamw_pack(r_pack, l)
        after = packs[l][0]
    (r_wout,) = _exchange_wait(started["w_out_0"], after, "exchange_w_out_0_wait")
    acc_out = adamw_w_out(r_wout, 0, acc_out)
    (r_win,) = _exchange_wait(started[0], acc_out[0], "exchange_0_wait")
    acc_in = adamw_w_in(r_win, 0, acc_in)
    r_lcw, r_dcw, r_pack = _exchange_wait(small, acc_in[0], "exchange_small_wait")
    for name, parts in (("lru_conv_w", r_lcw), ("dn_conv_w", r_dcw)):
        w = weights[name]
        flat = (-1, w.shape[-1])
        outs = _adamw(parts.reshape((N_DEV,) + (w.size // w.shape[-1], w.shape[-1])), w.reshape(flat),
                      mom_m[name].reshape(flat), mom_v[name].reshape(flat), f"adamw_{name}", 8)
        new[name] = [a.reshape(w.shape) for a in outs]
    packs[0] = adamw_pack(r_pack, 0)
    new["w_in"] = [a.reshape(w_in.shape) for a in acc_in]
    new["w_out"] = [a.reshape(w_out.shape) for a in acc_out]
    for name in _LAYERED:
        new[name] = [jnp.stack([_unpack_layer(packs[l][i], weights)[0][name] for l in range(depth)]) for i in range(4)]
    tail_at = _unpack_layer(packs[depth - 1][0], weights)[1]
    rows_final = D_MODEL // HEAD_DIM
    new["final_norm_w"] = [packs[depth - 1][i][tail_at:tail_at + rows_final].reshape(D_MODEL) for i in range(4)]
    loss = packs[depth - 1][0][tail_at + rows_final, 0]
    out = [loss, dx.reshape(x.shape)]
    for i in range(4):
        out += [new[name][i] for name in _WEIGHTS]
    return tuple(out)
```

```python
import functools

import jax
import jax.numpy as jnp
from jax import lax
from jax.experimental import pallas as pl
from jax.experimental.pallas import tpu as pltpu

F32 = jnp.float32
BF16 = jnp.bfloat16

N_DEV = 8
D_MODEL = 1024
HEADS = 8
HEAD_DIM = 128
CHUNK = 64
D_IN = 6160
D_IN_PAD = 6272
COL_LRU_X, COL_LRU_Z, COL_Q, COL_K, COL_V, COL_DN_Z, COL_BA = 0, 8, 16, 24, 32, 40, 48
LRU_C = 8.0
EPS = 1e-6
ADAM_LR, ADAM_B1, ADAM_B2, ADAM_EPS, ADAM_WD, ADAM_STEP = 0.001, 0.9, 0.999, 1e-08, 0.01, 10
TIME_BLOCK = 1024
VMEM_LIMIT = 56 * 1024 * 1024

NN = (((1,), (0,)), ((), ()))
NT = (((1,), (1,)), ((), ()))
TN = (((0,), (0,)), ((), ()))


B_NN = (((2,), (1,)), ((0,), (0,)))
B_NT = (((2,), (2,)), ((0,), (0,)))
B_TN = (((1,), (1,)), ((0,), (0,)))


def _split_bf16(x):
    hi = x.astype(BF16)
    return hi, (x - hi.astype(F32)).astype(BF16)


def _dot(a, b, dims, prec):
    if prec == "bf16":
        return lax.dot_general(a.astype(BF16), b.astype(BF16), dims, preferred_element_type=F32)
    a1, a2 = _split_bf16(a)
    b1, b2 = _split_bf16(b)
    dg = functools.partial(lax.dot_general, dimension_numbers=dims, preferred_element_type=F32)
    return dg(a1, b1) + (dg(a1, b2) + dg(a2, b1))


def _make_mm(prec, nn_dims, nt_dims, tn_dims):
    @jax.custom_vjp
    def nn(a, b):
        return _dot(a, b, nn_dims, prec)

    @jax.custom_vjp
    def nt(a, b):
        return _dot(a, b, nt_dims, prec)

    @jax.custom_vjp
    def tn(a, b):
        return _dot(a, b, tn_dims, prec)

    nn.defvjp(lambda a, b: (_dot(a, b, nn_dims, prec), (a, b)),
              lambda r, g: (_dot(g, r[1], nt_dims, prec), _dot(r[0], g, tn_dims, prec)))
    nt.defvjp(lambda a, b: (_dot(a, b, nt_dims, prec), (a, b)),
              lambda r, g: (_dot(g, r[1], nn_dims, prec), _dot(g, r[0], tn_dims, prec)))
    tn.defvjp(lambda a, b: (_dot(a, b, tn_dims, prec), (a, b)),
              lambda r, g: (_dot(r[1], g, nt_dims, prec), _dot(r[0], g, nn_dims, prec)))
    return nn, nt, tn


_NN_B, _NT_B, _TN_B = _make_mm("bf16", NN, NT, TN)
_BNN, _BNT, _BTN = _make_mm("bf16", B_NN, B_NT, B_TN)


@jax.custom_vjp
def _unit_lower_inverse(a):
    n = a.shape[-1]
    eye = (lax.broadcasted_iota(jnp.int32, a.shape, 1) == lax.broadcasted_iota(jnp.int32, a.shape, 2)).astype(F32)
    dg = functools.partial(lax.dot_general, dimension_numbers=B_NN, preferred_element_type=F32)
    inv = eye - a
    pw = _dot(a, a, B_NN, "bf16x3")
    steps = n.bit_length() - 2
    for j in range(steps):
        i1, i2 = _split_bf16(inv)
        p1, p2 = _split_bf16(pw)
        square = j + 1 < steps
        by_hi = dg(jnp.concatenate([i1, i2, p1, p2] if square else [i1, i2], axis=1), p1)
        by_lo = dg(jnp.concatenate([i1, p1], axis=1) if square else i1, p2)
        inv = inv + (by_hi[:, :n] + (by_lo[:, :n] + by_hi[:, n:2 * n]))
        if square:
            pw = by_hi[:, 2 * n:3 * n] + (by_lo[:, n:] + by_hi[:, 3 * n:])
    return inv


def _uli_fwd(a):
    inv = _unit_lower_inverse(a)
    return inv, inv


def _uli_bwd(inv, g):
    return (-_dot(_dot(inv, g, B_TN, "bf16"), inv, B_NT, "bf16"),)


_unit_lower_inverse.defvjp(_uli_fwd, _uli_bwd)


def _rows2(y, m):
    return y[:, :m], y[:, m:]


@jax.custom_vjp
def _pair_nn(x1, x2, r):
    return _rows2(_dot(jnp.concatenate([x1, x2], axis=1), r, B_NN, "bf16"), x1.shape[1])


def _pair_nn_bwd(res, g):
    x1, x2, r = res
    g = jnp.concatenate(g, axis=1)
    dx1, dx2 = _rows2(_dot(g, r, B_NT, "bf16"), x1.shape[1])
    return dx1, dx2, _dot(jnp.concatenate([x1, x2], axis=1), g, B_TN, "bf16")


_pair_nn.defvjp(lambda x1, x2, r: (_pair_nn(x1, x2, r), (x1, x2, r)), _pair_nn_bwd)


@jax.custom_vjp
def _pair_nt(x1, x2, r):
    return _rows2(_dot(jnp.concatenate([x1, x2], axis=1), r, B_NT, "bf16"), x1.shape[1])


def _pair_nt_bwd(res, g):
    x1, x2, r = res
    g = jnp.concatenate(g, axis=1)
    dx1, dx2 = _rows2(_dot(g, r, B_NN, "bf16"), x1.shape[1])
    return dx1, dx2, _dot(g, jnp.concatenate([x1, x2], axis=1), B_TN, "bf16")


_pair_nt.defvjp(lambda x1, x2, r: (_pair_nt(x1, x2, r), (x1, x2, r)), _pair_nt_bwd)


@jax.custom_vjp
def _wide_nn(l, r1, r2):
    y = _dot(l, jnp.concatenate([r1, r2], axis=2), B_NN, "bf16")
    return y[:, :, :r1.shape[2]], y[:, :, r1.shape[2]:]


def _wide_nn_bwd(res, g):
    l, r1, r2 = res
    g = jnp.concatenate(g, axis=2)
    dr = _dot(l, g, B_TN, "bf16")
    return (_dot(g, jnp.concatenate([r1, r2], axis=2), B_NT, "bf16"), dr[:, :, :r1.shape[2]], dr[:, :, r1.shape[2]:])


_wide_nn.defvjp(lambda l, r1, r2: (_wide_nn(l, r1, r2), (l, r1, r2)), _wide_nn_bwd)


def _lower_ones(batch, n):
    shape = (batch, n, n)
    return (lax.broadcasted_iota(jnp.int32, shape, 1) >= lax.broadcasted_iota(jnp.int32, shape, 2)).astype(BF16)


@jax.custom_vjp
def _chunk_cumsum(g):
    tri = _lower_ones(g.shape[0], g.shape[1])
    g1, g2 = _split_bf16(g)
    g3 = (g - g1.astype(F32) - g2.astype(F32)).astype(BF16)
    dg = functools.partial(lax.dot_general, dimension_numbers=B_NN, preferred_element_type=F32)
    return dg(tri, g1) + (dg(tri, g2) + dg(tri, g3))


def _chunk_cumsum_bwd(_, ct):
    tri = _lower_ones(ct.shape[0], ct.shape[1])
    c1, c2 = _split_bf16(ct)
    dg = functools.partial(lax.dot_general, dimension_numbers=B_TN, preferred_element_type=F32)
    return (dg(tri, c1) + dg(tri, c2),)


_chunk_cumsum.defvjp(lambda g: (_chunk_cumsum(g), None), _chunk_cumsum_bwd)


def _expm1(x):
    small = x * (1.0 + x * (0.5 + x * (1.0 / 6 + x * (1.0 / 24 + x * (1.0 / 120 + x * (1.0 / 720))))))
    return jnp.where(jnp.abs(x) < 0.2, small, jnp.exp(x) - 1.0)


def _sigmoid(x):
    return 1.0 / (1.0 + jnp.exp(-x))


def _silu(x):
    return x * _sigmoid(x)


def _softplus(x):
    return jnp.maximum(x, 0.0) + jnp.log(1.0 + jnp.exp(-jnp.abs(x)))


def _rmsnorm(x, w):
    return x * lax.rsqrt(jnp.mean(x * x, axis=-1, keepdims=True) + EPS) * w


def _gated_norm(o, z, w):
    return o * lax.rsqrt(jnp.mean(o * o, axis=-1, keepdims=True) + EPS) * w * _silu(z)


def _lru_gates(xc, wa, ba, wx, bx, lam):
    r = _sigmoid(_NN_B(xc, wa) + ba)
    i = _sigmoid(_NN_B(xc, wx) + bx)
    log_a = -LRU_C * r * _softplus(-lam)
    a = jnp.exp(log_a)
    mult = jnp.sqrt(-_expm1(2.0 * log_a))
    return a, mult * (i * xc)


def _scan_forward(a, b, h0):
    rows = a.shape[0]
    row = lax.broadcasted_iota(jnp.int32, a.shape, 0)
    k = 1
    while k < rows:
        seen = row >= k
        b = jnp.where(seen, a * pltpu.roll(b, k, 0) + b, b)
        a = jnp.where(seen, a * pltpu.roll(a, k, 0), a)
        k *= 2
    return b + a * h0


def _scan_reverse(a, d, carry):
    rows = a.shape[0]
    row = lax.broadcasted_iota(jnp.int32, a.shape, 0)
    last = row == rows - 1
    c = jnp.where(last, 0.0, pltpu.roll(a, rows - 1, 0))
    d = d + jnp.where(last, carry, 0.0)
    k = 1
    while k < rows:
        seen = row < rows - k
        d = jnp.where(seen, d + c * pltpu.roll(d, rows - k, 0), d)
        c = jnp.where(seen, c * pltpu.roll(c, rows - k, 0), c)
        k *= 2
    return d


def _lane_pick(row, lane_index):
    lane = lax.broadcasted_iota(jnp.int32, row.shape, 1)
    return jnp.sum(jnp.where(lane == lane_index, row, 0.0), axis=-1, keepdims=True)


def _dn_prep(qc, kc, vc, ba, a_log_row, dt_row, head):
    q = _silu(qc)
    k = _silu(kc)
    v = _silu(vc)
    q = q * lax.rsqrt(jnp.sum(q * q, axis=-1, keepdims=True) + EPS) * (HEAD_DIM ** -0.5)
    k = k * lax.rsqrt(jnp.sum(k * k, axis=-1, keepdims=True) + EPS)
    beta = _sigmoid(_lane_pick(ba, head))
    g = -jnp.exp(_lane_pick(a_log_row, head)) * _softplus(_lane_pick(ba, HEADS + head) + _lane_pick(dt_row, head))
    return q, k, v, g, beta


def _dn_chunks_head(q, k, v, gcol, bcol):
    n, c, d = q.shape
    row = lax.broadcasted_iota(jnp.int32, (n, c, c), 1)
    col = lax.broadcasted_iota(jnp.int32, (n, c, c), 2)
    g_wide = jnp.broadcast_to(gcol, (n, c, d))
    b_wide = jnp.broadcast_to(bcol, (n, c, d))
    gc = _chunk_cumsum(g_wide)
    gc_rows = gc[:, :, :c]
    decay = jnp.exp(jnp.where(row >= col, gc_rows - jnp.swapaxes(gc_rows, 1, 2), -1e30))
    kb = k * b_wide
    eg = jnp.exp(gc)
    kbk, qk = _pair_nt(kb, q, k)
    tinv = _unit_lower_inverse(jnp.where(row > col, kbk * decay, 0.0))
    u, w = _wide_nn(tinv, v * b_wide, kb * eg)
    g_last = jnp.sum(g_wide, axis=1, keepdims=True)
    return u, w, qk * decay, q * eg, k * jnp.exp(g_last - gc), jnp.exp(g_last)


def _dn_chunks(q, k, v, gcol, bcol, states):
    u, w, attn, qe, kdec, eglast = _dn_chunks_head(q, k, v, gcol, bcol)
    w_st, qe_st = _pair_nn(w, qe, states)
    v_new = u - w_st
    o = qe_st + _BNN(attn, v_new)
    return (o, states * eglast + _BTN(kdec, v_new)), (w, attn, qe, kdec, eglast)


def _conv_taps(buf, head, cw, rows):
    acc = cw[0:1, :] * buf[head, pl.ds(5, rows), :]
    for j in range(1, 4):
        acc = acc + cw[j:j + 1, :] * buf[head, pl.ds(5 + j, rows), :]
    return acc


def _conv_backward(dbuf, dhead, xbuf, xhead, cw, dxc, rows):
    dbuf[dhead, pl.ds(0, rows), :] = dxc
    dx = cw[0:1, :] * dbuf[dhead, pl.ds(3, rows), :]
    for j in range(1, 4):
        dx = dx + cw[j:j + 1, :] * dbuf[dhead, pl.ds(3 - j, rows), :]
    dcw = jnp.concatenate(
        [jnp.sum(dxc * xbuf[xhead, pl.ds(5 + j, rows), :], axis=0, keepdims=True) for j in range(4)], axis=0)
    dbuf[dhead, pl.ds(rows, 8), :] = dbuf[dhead, pl.ds(0, 8), :]
    return dx, dcw


def _params(**kw):
    return pltpu.CompilerParams(vmem_limit_bytes=VMEM_LIMIT, **kw)


def _matmul(a, b, form, tm, tn, tk, name, add=None, out_dtype=F32, dep=None):
    if form == "nn":
        (m, kdim), (_, n) = a.shape, b.shape
        a_spec = pl.BlockSpec((tm, tk), lambda j, i, k: (i, k))
        b_spec = pl.BlockSpec((tk, tn), lambda j, i, k: (k, j))
        dims = NN
    elif form == "nt":
        (m, kdim), (n, _) = a.shape, b.shape
        a_spec = pl.BlockSpec((tm, tk), lambda j, i, k: (i, k))
        b_spec = pl.BlockSpec((tn, tk), lambda j, i, k: (j, k))
        dims = NT
    else:
        (kdim, m), (_, n) = a.shape, b.shape
        a_spec = pl.BlockSpec((tk, tm), lambda j, i, k: (k, i))
        b_spec = pl.BlockSpec((tk, tn), lambda j, i, k: (k, j))
        dims = TN
    assert m % tm == 0 and n % tn == 0 and kdim % tk == 0, (name, m, n, kdim, tm, tn, tk)
    ksteps = kdim // tk
    o_spec = pl.BlockSpec((tm, tn), lambda j, i, k: (i, j))
    has_add = add is not None
    extra = [] if dep is None else [dep]

    def body(*refs):
        a_ref, b_ref = refs[:2]
        c_ref = refs[2] if has_add else None
        o_ref, acc = refs[-2:]
        k = pl.program_id(2)

        @pl.when(k == 0)
        def _():
            acc[...] = c_ref[...] if has_add else jnp.zeros_like(acc)

        acc[...] += lax.dot_general(a_ref[...].astype(BF16), b_ref[...].astype(BF16), dims,
                                    preferred_element_type=F32)

        @pl.when(k == ksteps - 1)
        def _():
            o_ref[...] = acc[...].astype(o_ref.dtype)

    in_specs = [a_spec, b_spec] + ([o_spec] if has_add else []) + [pl.BlockSpec((8, HEAD_DIM), lambda j, i, k: (0, 0))
                                                                   for _ in extra]
    args = (a, b) + ((add,) if has_add else ()) + tuple(extra)
    return pl.pallas_call(
        body, name=name, grid=(n // tn, m // tm, ksteps), in_specs=in_specs, out_specs=o_spec,
        out_shape=jax.ShapeDtypeStruct((m, n), out_dtype), scratch_shapes=[pltpu.VMEM((tm, tn), F32)],
        compiler_params=_params(dimension_semantics=("parallel", "parallel", "arbitrary")),
    )(*args)


def _rmsnorm_fwd(x, w_row, name):
    s = x.shape[0]
    tb = min(TIME_BLOCK, s)

    def body(x_ref, w_ref, o_ref):
        o_ref[...] = _rmsnorm(x_ref[...], w_ref[...]).astype(BF16)

    return pl.pallas_call(
        body, name=name, grid=(s // tb,),
        in_specs=[pl.BlockSpec((tb, D_MODEL), lambda i: (i, 0)), pl.BlockSpec((1, D_MODEL), lambda i: (0, 0))],
        out_specs=pl.BlockSpec((tb, D_MODEL), lambda i: (i, 0)),
        out_shape=jax.ShapeDtypeStruct((s, D_MODEL), BF16), compiler_params=_params(),
    )(x, w_row)


def _rmsnorm_bwd(x, w_row, dh, dres, name):
    s = x.shape[0]
    tb = min(TIME_BLOCK, s)

    def body(x_ref, w_ref, dh_ref, dres_ref, dx_ref, dw_ref):
        _, vjp = jax.vjp(_rmsnorm, x_ref[...], w_ref[...])
        dx, dw = vjp(dh_ref[...])
        dx_ref[...] = dres_ref[...] + dx

        @pl.when(pl.program_id(0) == 0)
        def _():
            dw_ref[...] = jnp.zeros_like(dw_ref)

        dw_ref[...] += dw

    row = pl.BlockSpec((tb, D_MODEL), lambda i: (i, 0))
    vec = pl.BlockSpec((1, D_MODEL), lambda i: (0, 0))
    return pl.pallas_call(
        body, name=name, grid=(s // tb,), in_specs=[row, vec, row, row], out_specs=[row, vec],
        out_shape=[jax.ShapeDtypeStruct((s, D_MODEL), F32), jax.ShapeDtypeStruct((1, D_MODEL), F32)],
        compiler_params=_params(),
    )(x, w_row, dh, dres)


def _final_loss(x, w_row, target, name):
    s = x.shape[0]
    tb = min(TIME_BLOCK, s)

    def loss_fn(xv, wv, tv):
        err = _rmsnorm(xv, wv) - tv
        return 0.5 * jnp.sum(jnp.sum(err * err, axis=-1, keepdims=True), axis=0, keepdims=True) * (1.0 / D_MODEL)

    def body(x_ref, w_ref, t_ref, loss_ref, dx_ref, dw_ref):
        tv = t_ref[...]
        loss, vjp = jax.vjp(lambda xv, wv: loss_fn(xv, wv, tv), x_ref[...], w_ref[...])
        dx, dw = vjp(jnp.ones((1, 1), F32))
        dx_ref[...] = dx

        @pl.when(pl.program_id(0) == 0)
        def _():
            dw_ref[...] = jnp.zeros_like(dw_ref)
            loss_ref[...] = jnp.zeros_like(loss_ref)

        dw_ref[...] += dw
        loss_ref[...] += jnp.broadcast_to(loss, loss_ref.shape)

    row = pl.BlockSpec((tb, D_MODEL), lambda i: (i, 0))
    vec = pl.BlockSpec((1, D_MODEL), lambda i: (0, 0))
    return pl.pallas_call(
        body, name=name, grid=(s // tb,), in_specs=[row, vec, row],
        out_specs=[pl.BlockSpec((1, HEAD_DIM), lambda i: (0, 0)), row, vec],
        out_shape=[jax.ShapeDtypeStruct((1, HEAD_DIM), F32), jax.ShapeDtypeStruct((s, D_MODEL), F32),
                   jax.ShapeDtypeStruct((1, D_MODEL), F32)],
        compiler_params=_params(),
    )(x, w_row, target)


def _head_specs(tb, time_of):
    def col(off):
        return pl.BlockSpec((tb, HEAD_DIM), lambda t, h: (time_of(t), off + h))
    return col


def _vec_spec():
    return pl.BlockSpec((1, HEAD_DIM), lambda t, h: (0, h))


def _lru_fwd(proj, conv_w, conv_b, wa, ba, wx, bx, lam, nw, name):
    s = proj.shape[0]
    tb = min(TIME_BLOCK, s)
    nt = s // tb
    col = _head_specs(tb, lambda t: t)

    def body(x_ref, z_ref, cw_ref, cb_ref, wa_ref, ba_ref, wx_ref, bx_ref, lam_ref, nw_ref,
             y_ref, hs_ref, xbuf, hcar):
        t, h = pl.program_id(0), pl.program_id(1)

        @pl.when(t == 0)
        def _():
            xbuf[h, pl.ds(0, 8), :] = jnp.zeros((8, HEAD_DIM), F32)
            hcar[h] = jnp.zeros((8, HEAD_DIM), F32)

        xbuf[h, pl.ds(8, tb), :] = x_ref[...]
        xc = _conv_taps(xbuf, h, cw_ref[...], tb) + cb_ref[...]
        a, b = _lru_gates(xc, wa_ref[...], ba_ref[...], wx_ref[...], bx_ref[...], lam_ref[...])
        hs_ref[...] = _scan_forward(a, b, hcar[h, pl.ds(0, 1), :])
        hcar[h, pl.ds(0, 1), :] = hs_ref[pl.ds(tb - 1, 1), :]
        xbuf[h, pl.ds(0, 8), :] = xbuf[h, pl.ds(tb, 8), :]
        y_ref[...] = _gated_norm(hs_ref[...], z_ref[...], nw_ref[...]).astype(BF16)

    vec = _vec_spec()
    return pl.pallas_call(
        body, name=name, grid=(nt, HEADS),
        in_specs=[col(COL_LRU_X), col(COL_LRU_Z), pl.BlockSpec((4, HEAD_DIM), lambda t, h: (0, h)), vec,
                  pl.BlockSpec((None, HEAD_DIM, HEAD_DIM), lambda t, h: (h, 0, 0)), vec,
                  pl.BlockSpec((None, HEAD_DIM, HEAD_DIM), lambda t, h: (h, 0, 0)), vec, vec, vec],
        out_specs=[col(0), col(0)],
        out_shape=[jax.ShapeDtypeStruct((s, 2 * D_MODEL), BF16), jax.ShapeDtypeStruct((s, D_MODEL), F32)],
        scratch_shapes=[pltpu.VMEM((HEADS, tb + 8, HEAD_DIM), F32), pltpu.VMEM((HEADS, 8, HEAD_DIM), F32)],
        compiler_params=_params(dimension_semantics=("arbitrary", "arbitrary")),
    )(proj, proj, conv_w, conv_b, wa, ba, wx, bx, lam, nw)


def _halo_spec(tb, nt, off):
    per = tb // 8
    return pl.BlockSpec((8, HEAD_DIM), lambda t, h: (jnp.maximum((nt - 1 - t) * per - 1, 0), off + h))


def _lru_bwd(proj, hs, dy, conv_w, conv_b, wa, ba, wx, bx, lam, nw, name):
    s = proj.shape[0]
    tb = min(TIME_BLOCK, s)
    nt = s // tb
    col = _head_specs(tb, lambda t: nt - 1 - t)

    def body(x_ref, xh_ref, z_ref, hs_ref, hh_ref, dy_ref, cw_ref, cb_ref, wa_ref, ba_ref, wx_ref, bx_ref,
             lam_ref, nw_ref, dx_ref, dz_ref, dcw_ref, dcb_ref, dwa_ref, dba_ref, dwx_ref, dbx_ref, dlam_ref,
             dnw_ref, xbuf, hbuf, dbuf, gcar):
        t, h = pl.program_id(0), pl.program_id(1)
        first_block = t == nt - 1

        @pl.when(t == 0)
        def _():
            dbuf[h, pl.ds(tb, 8), :] = jnp.zeros((8, HEAD_DIM), F32)
            gcar[h] = jnp.zeros((8, HEAD_DIM), F32)
            dcw_ref[h] = jnp.zeros((4, HEAD_DIM), F32)
            dwa_ref[h] = jnp.zeros((HEAD_DIM, HEAD_DIM), F32)
            dwx_ref[h] = jnp.zeros((HEAD_DIM, HEAD_DIM), F32)
            for ref in (dcb_ref, dba_ref, dbx_ref, dlam_ref, dnw_ref):
                ref[h] = jnp.zeros((1, HEAD_DIM), F32)

        keep = jnp.where(first_block, 0.0, 1.0)
        xbuf[0, pl.ds(0, 8), :] = xh_ref[...] * keep
        xbuf[0, pl.ds(8, tb), :] = x_ref[...]
        hbuf[pl.ds(0, 8), :] = hh_ref[...] * keep
        hbuf[pl.ds(8, tb), :] = hs_ref[...]
        cw = cw_ref[...]
        xc = _conv_taps(xbuf, 0, cw, tb) + cb_ref[...]
        (a, _), gates_vjp = jax.vjp(_lru_gates, xc, wa_ref[...], ba_ref[...], wx_ref[...], bx_ref[...], lam_ref[...])
        _, norm_vjp = jax.vjp(_gated_norm, hs_ref[...], z_ref[...], nw_ref[...])
        dh, dz, dnw = norm_vjp(dy_ref[...])
        dz_ref[...] = dz.astype(dz_ref.dtype)
        g = _scan_reverse(a, dh, gcar[h, pl.ds(0, 1), :])
        gcar[h, pl.ds(0, 1), :] = a[0:1, :] * g[0:1, :]
        dxc, dwa, dba, dwx, dbx, dlam = gates_vjp((g * hbuf[pl.ds(7, tb), :], g))
        dx, dcw = _conv_backward(dbuf, h, xbuf, 0, cw, dxc, tb)
        dx_ref[...] = dx.astype(dx_ref.dtype)
        dcw_ref[h] += dcw
        dcb_ref[h] += jnp.sum(dxc, axis=0, keepdims=True)
        dwa_ref[h] += dwa
        dwx_ref[h] += dwx
        dba_ref[h] += dba
        dbx_ref[h] += dbx
        dlam_ref[h] += dlam
        dnw_ref[h] += dnw

    vec = _vec_spec()
    mat = pl.BlockSpec((None, HEAD_DIM, HEAD_DIM), lambda t, h: (h, 0, 0))

    def whole(shape):
        return pl.BlockSpec(shape, lambda t, h: (0,) * len(shape))

    head_vec = jax.ShapeDtypeStruct((HEADS, 1, HEAD_DIM), F32)
    head_mat = jax.ShapeDtypeStruct((HEADS, HEAD_DIM, HEAD_DIM), F32)
    return pl.pallas_call(
        body, name=name, grid=(nt, HEADS),
        in_specs=[col(COL_LRU_X), _halo_spec(tb, nt, COL_LRU_X), col(COL_LRU_Z), col(0), _halo_spec(tb, nt, 0), col(0),
                  pl.BlockSpec((4, HEAD_DIM), lambda t, h: (0, h)), vec, mat, vec, mat, vec, vec, vec],
        out_specs=[col(0), col(0), whole((HEADS, 4, HEAD_DIM)), whole((HEADS, 1, HEAD_DIM)),
                   whole((HEADS, HEAD_DIM, HEAD_DIM)), whole((HEADS, 1, HEAD_DIM)),
                   whole((HEADS, HEAD_DIM, HEAD_DIM)), whole((HEADS, 1, HEAD_DIM)), whole((HEADS, 1, HEAD_DIM)),
                   whole((HEADS, 1, HEAD_DIM))],
        out_shape=[jax.ShapeDtypeStruct((s, D_MODEL), BF16), jax.ShapeDtypeStruct((s, D_MODEL), BF16),
                   jax.ShapeDtypeStruct((HEADS, 4, HEAD_DIM), F32), head_vec, head_mat, head_vec, head_mat, head_vec,
                   head_vec, head_vec],
        scratch_shapes=[pltpu.VMEM((1, tb + 8, HEAD_DIM), F32), pltpu.VMEM((tb + 8, HEAD_DIM), F32),
                        pltpu.VMEM((HEADS, tb + 8, HEAD_DIM), F32), pltpu.VMEM((HEADS, 8, HEAD_DIM), F32)],
        compiler_params=_params(dimension_semantics=("arbitrary", "arbitrary")),
    )(proj, proj, proj, hs, hs, dy, conv_w, conv_b, wa, ba, wx, bx, lam, nw)


def _dn_fwd(proj, y, conv_w, a_log_row, dt_row, nw, name):
    s = proj.shape[0]
    tb = min(TIME_BLOCK, s)
    nt = s // tb
    nchunk = tb // CHUNK
    col = _head_specs(tb, lambda t: t)

    def body(q_ref, k_ref, v_ref, z_ref, ba_ref, cwq_ref, cwk_ref, cwv_ref, al_ref, dt_ref, nw_ref, y_in_ref,
             y_ref, o_ref, st_ref, xbuf, state):
        t, h = pl.program_id(0), pl.program_id(1)

        @pl.when(t == 0)
        def _():
            for i in range(3):
                xbuf[3 * h + i, pl.ds(0, 8), :] = jnp.zeros((8, HEAD_DIM), F32)
            state[h] = jnp.zeros((HEAD_DIM, HEAD_DIM), F32)

        conv = []
        for i, (ref, cw_ref) in enumerate(((q_ref, cwq_ref), (k_ref, cwk_ref), (v_ref, cwv_ref))):
            xbuf[3 * h + i, pl.ds(8, tb), :] = ref[...]
            conv.append(_conv_taps(xbuf, 3 * h + i, cw_ref[...], tb))
            xbuf[3 * h + i, pl.ds(0, 8), :] = xbuf[3 * h + i, pl.ds(tb, 8), :]
        q, k, v, g, beta = _dn_prep(conv[0], conv[1], conv[2], ba_ref[...], al_ref[...], dt_ref[...], h)
        def chunks(a):
            return a.reshape(nchunk, CHUNK, a.shape[-1])

        u, w, attn, qe, kdec, eglast = _dn_chunks_head(chunks(q), chunks(k), chunks(v), chunks(g), chunks(beta))
        w_u = jnp.concatenate([w, u], axis=2)
        kdec_w_u = _dot(kdec, w_u, B_TN, "bf16")
        attn_w_u = _dot(attn, w_u, B_NN, "bf16")
        st = state[h]
        for c in range(nchunk):
            st_ref[c] = st
            st = st * eglast[c] - _NN_B(kdec_w_u[c, :, :HEAD_DIM], st) + kdec_w_u[c, :, HEAD_DIM:]
        state[h] = st
        o = _dot(qe - attn_w_u[:, :, :HEAD_DIM], st_ref[...], B_NN, "bf16") + attn_w_u[:, :, HEAD_DIM:]
        o_ref[...] = o.reshape(tb, HEAD_DIM)
        y_ref[...] = _gated_norm(o_ref[...], z_ref[...], nw_ref[...]).astype(BF16)

    def cw_spec(off):
        return pl.BlockSpec((4, HEAD_DIM), lambda t, h: (0, off + h))

    row128 = pl.BlockSpec((1, HEAD_DIM), lambda t, h: (0, 0))
    return pl.pallas_call(
        body, name=name, grid=(nt, HEADS),
        in_specs=[col(COL_Q), col(COL_K), col(COL_V), col(COL_DN_Z),
                  pl.BlockSpec((tb, HEAD_DIM), lambda t, h: (t, COL_BA)),
                  cw_spec(0), cw_spec(HEADS), cw_spec(2 * HEADS), row128, row128, row128,
                  pl.BlockSpec(memory_space=pl.ANY)],
        out_specs=[col(HEADS), col(0), pl.BlockSpec((None, nchunk, HEAD_DIM, HEAD_DIM), lambda t, h: (h, t, 0, 0))],
        out_shape=[jax.ShapeDtypeStruct((s, 2 * D_MODEL), BF16), jax.ShapeDtypeStruct((s, D_MODEL), F32),
                   jax.ShapeDtypeStruct((HEADS, s // CHUNK, HEAD_DIM, HEAD_DIM), F32)],
        input_output_aliases={11: 0},
        scratch_shapes=[pltpu.VMEM((3 * HEADS, tb + 8, HEAD_DIM), F32), pltpu.VMEM((HEADS, HEAD_DIM, HEAD_DIM), F32)],
        compiler_params=_params(dimension_semantics=("arbitrary", "arbitrary")),
    )(proj, proj, proj, proj, proj, conv_w, conv_w, conv_w, a_log_row, dt_row, nw, y)


def _dn_bwd(proj, o, states, dy, conv_w, a_log_row, dt_row, nw, name):
    s = proj.shape[0]
    tb = min(TIME_BLOCK, s)
    nt = s // tb
    nchunk = tb // CHUNK
    col = _head_specs(tb, lambda t: nt - 1 - t)

    def body(q_ref, qh_ref, k_ref, kh_ref, v_ref, vh_ref, z_ref, ba_ref, o_ref, st_ref, dy_ref,
             cwq_ref, cwk_ref, cwv_ref, al_ref, dt_ref, nw_ref,
             dq_ref, dk_ref, dv_ref, dz_ref, dba_ref, dcw_ref, dal_ref, ddt_ref, dnw_ref,
             xbuf, dbuf, dstate, dst_s):
        t, h = pl.program_id(0), pl.program_id(1)
        first_block = t == nt - 1

        @pl.when(t == 0)
        def _():
            for i in range(3):
                dbuf[3 * h + i, pl.ds(tb, 8), :] = jnp.zeros((8, HEAD_DIM), F32)
                dcw_ref[3 * h + i] = jnp.zeros((4, HEAD_DIM), F32)
            dstate[h] = jnp.zeros((HEAD_DIM, HEAD_DIM), F32)

        @pl.when((t == 0) & (h == 0))
        def _():
            for ref in (dal_ref, ddt_ref, dnw_ref):
                ref[...] = jnp.zeros_like(ref)

        keep = jnp.where(first_block, 0.0, 1.0)
        cws = (cwq_ref[...], cwk_ref[...], cwv_ref[...])
        conv = []
        for i, (ref, halo) in enumerate(((q_ref, qh_ref), (k_ref, kh_ref), (v_ref, vh_ref))):
            xbuf[i, pl.ds(0, 8), :] = halo[...] * keep
            xbuf[i, pl.ds(8, tb), :] = ref[...]
            conv.append(_conv_taps(xbuf, i, cws[i], tb))
        (q, k, v, g, beta), prep_vjp = jax.vjp(
            lambda qc, kc, vc, ba, al, dt: _dn_prep(qc, kc, vc, ba, al, dt, h),
            conv[0], conv[1], conv[2], ba_ref[...], al_ref[...], dt_ref[...])
        _, norm_vjp = jax.vjp(_gated_norm, o_ref[...], z_ref[...], nw_ref[...])
        do, dz, dnw = norm_vjp(dy_ref[...])
        dz_ref[...] = dz.astype(dz_ref.dtype)
        dnw_ref[...] += dnw

        def chunks(a):
            return a.reshape(nchunk, CHUNK, a.shape[-1])

        do = chunks(do)
        _, chunks_vjp, (w, attn, qe, kdec, eglast) = jax.vjp(
            _dn_chunks, chunks(q), chunks(k), chunks(v), chunks(g), chunks(beta), st_ref[...], has_aux=True)
        kdec_w = _dot(kdec, w, B_TN, "bf16")
        fixed = _dot(qe, do, B_TN, "bf16") - _dot(w, _dot(attn, do, B_TN, "bf16"), B_TN, "bf16")
        dst = dstate[h]
        for c in reversed(range(nchunk)):
            dst_s[c] = dst
            dst = dst * eglast[c] - _dot(kdec_w[c], dst, TN, "bf16") + fixed[c]
        dstate[h] = dst
        dq, dk, dv, dg, db, _ = chunks_vjp((do, dst_s[...]))

        def rows(a):
            return a.reshape(tb, a.shape[-1])

        dqc, dkc, dvc, dba, dal, ddt = prep_vjp((rows(dq), rows(dk), rows(dv), rows(dg), rows(db)))
        for i, (dxc, out) in enumerate(((dqc, dq_ref), (dkc, dk_ref), (dvc, dv_ref))):
            dx, dcw = _conv_backward(dbuf, 3 * h + i, xbuf, i, cws[i], dxc, tb)
            out[...] = dx.astype(out.dtype)
            dcw_ref[3 * h + i] += dcw
        dal_ref[...] += dal
        ddt_ref[...] += ddt

        @pl.when(h == 0)
        def _():
            dba_ref[...] = dba.astype(dba_ref.dtype)

        @pl.when(h > 0)
        def _():
            dba_ref[...] += dba.astype(dba_ref.dtype)

    def cw_spec(off):
        return pl.BlockSpec((4, HEAD_DIM), lambda t, h: (0, off + h))

    def whole(shape):
        return pl.BlockSpec(shape, lambda t, h: (0,) * len(shape))

    row128 = whole((1, HEAD_DIM))
    blk = (tb, HEAD_DIM)
    act = jax.ShapeDtypeStruct((s, D_MODEL), BF16)
    row_out = jax.ShapeDtypeStruct((1, HEAD_DIM), F32)
    return pl.pallas_call(
        body, name=name, grid=(nt, HEADS),
        in_specs=[col(COL_Q), _halo_spec(tb, nt, COL_Q), col(COL_K), _halo_spec(tb, nt, COL_K),
                  col(COL_V), _halo_spec(tb, nt, COL_V), col(COL_DN_Z),
                  pl.BlockSpec(blk, lambda t, h: (nt - 1 - t, COL_BA)), col(0),
                  pl.BlockSpec((None, nchunk, HEAD_DIM, HEAD_DIM), lambda t, h: (h, nt - 1 - t, 0, 0)), col(HEADS),
                  cw_spec(0), cw_spec(HEADS), cw_spec(2 * HEADS), row128, row128, row128],
        out_specs=[col(0), col(0), col(0), col(0), pl.BlockSpec(blk, lambda t, h: (nt - 1 - t, 0)),
                   whole((3 * HEADS, 4, HEAD_DIM)), row128, row128, row128],
        out_shape=[act, act, act, act, jax.ShapeDtypeStruct((s, HEAD_DIM), F32),
                   jax.ShapeDtypeStruct((3 * HEADS, 4, HEAD_DIM), F32), row_out, row_out, row_out],
        scratch_shapes=[pltpu.VMEM((3, tb + 8, HEAD_DIM), F32), pltpu.VMEM((3 * HEADS, tb + 8, HEAD_DIM), F32),
                        pltpu.VMEM((HEADS, HEAD_DIM, HEAD_DIM), F32), pltpu.VMEM((nchunk, HEAD_DIM, HEAD_DIM), F32)],
        compiler_params=_params(dimension_semantics=("arbitrary", "arbitrary")),
    )(proj, proj, proj, proj, proj, proj, proj, proj, o, states, dy, conv_w, conv_w, conv_w, a_log_row, dt_row, nw)


def _mesh_position():
    x, y, c = lax.axis_index("x"), lax.axis_index("y"), lax.axis_index("c")
    return x, y, c, 4 * x + 2 * y + c


def _peer(k, x, y, c):
    px = 1 - x if k & 4 else x
    py = 1 - y if k & 2 else y
    pc = 1 - c if k & 1 else c
    return (px, py, pc), 4 * px + 2 * py + pc


def _exchange_copies(ins, lands, scatter, send_sems, recv_sems, receives=True):
    x, y, c, me = _mesh_position()
    sends, recvs = [], []
    for i, (src, land) in enumerate(zip(ins, lands)):
        for k in range(1, N_DEV):
            peer, peer_id = _peer(k, x, y, c)
            sem = i * (N_DEV - 1) + k - 1
            for dst, out in ((me, sends), (peer_id, recvs)) if receives else ((me, sends),):
                out.append(pltpu.make_async_remote_copy(
                    src_ref=src.at[peer_id] if scatter[i] else src, dst_ref=land.at[dst],
                    send_sem=send_sems.at[sem], recv_sem=recv_sems.at[sem],
                    device_id=peer, device_id_type=pl.DeviceIdType.MESH))
    return sends, recvs


def _landing_shape(a, scatter):
    return a.shape if scatter else (N_DEV,) + a.shape


def _direct_exchange(arrays, scatter, name):
    n = len(arrays)
    out_shapes = [jax.ShapeDtypeStruct(_landing_shape(a, sc), a.dtype) for a, sc in zip(arrays, scatter)]

    def body(*refs):
        ins, outs = refs[:n], refs[n:2 * n]
        send_sems, recv_sems, local_sems = refs[2 * n:]
        me = _mesh_position()[3]
        local = [pltpu.make_async_copy(ins[i].at[me] if scatter[i] else ins[i], outs[i].at[me], local_sems.at[i])
                 for i in range(n)]
        sends, recvs = _exchange_copies(ins, outs, scatter, send_sems, recv_sems)
        for cp in local + sends:
            cp.start()
        for cp in recvs:
            cp.wait_recv()
        for cp in sends:
            cp.wait_send()
        for cp in local:
            cp.wait()

    hbm = pl.BlockSpec(memory_space=pl.ANY)
    return pl.pallas_call(
        body, name=name, in_specs=[hbm] * n, out_specs=[hbm] * n, out_shape=out_shapes,
        scratch_shapes=[pltpu.SemaphoreType.DMA((n * (N_DEV - 1),)), pltpu.SemaphoreType.DMA((n * (N_DEV - 1),)),
                        pltpu.SemaphoreType.DMA((n,))],
    )(*arrays)


_HBM = pl.BlockSpec(memory_space=pltpu.HBM)
_SEM = pl.BlockSpec(memory_space=pltpu.SEMAPHORE)
_DATAFLOW = pltpu.SideEffectType.DATAFLOW_SIDE_EFFECTING


def _exchange_start(arrays, scatter, name):
    n = len(arrays)
    srcs = [pltpu.with_memory_space_constraint(a, pltpu.HBM) for a in arrays]
    lands = [pltpu.with_memory_space_constraint(lax.empty(_landing_shape(a, sc), a.dtype), pltpu.HBM)
             for a, sc in zip(arrays, scatter)]
    nsem = n * (N_DEV - 1)

    def body(*refs):
        ins, zones = refs[:n], refs[n:2 * n]
        send_sems, recv_sems = refs[2 * n], refs[2 * n + 1]
        token = refs[-1]
        sends, _ = _exchange_copies(ins, zones, scatter, send_sems, recv_sems, receives=False)
        for cp in sends:
            cp.start()
        token[...] = jnp.zeros_like(token)

    res = pl.pallas_call(
        body, name=name,
        out_shape=(pltpu.SemaphoreType.DMA((nsem,)), pltpu.SemaphoreType.DMA((nsem,)),
                   *[pltpu.HBM(a.shape, a.dtype) for a in srcs + lands], jax.ShapeDtypeStruct((8, HEAD_DIM), F32)),
        in_specs=[_HBM] * (2 * n),
        out_specs=(_SEM, _SEM, *[_HBM] * (2 * n), pl.BlockSpec(memory_space=pltpu.VMEM)),
        input_output_aliases={i: 2 + i for i in range(2 * n)},
        compiler_params=pltpu.CompilerParams(has_side_effects=_DATAFLOW),
    )(*srcs, *lands)
    return dict(sems=res[:2], srcs=res[2:2 + n], lands=res[2 + n:2 + 2 * n], token_block=res[-1],
                token=res[-1][0, 0], scatter=scatter)


def _exchange_wait(started, after, name):
    scatter = started["scatter"]
    n = len(scatter)

    def body(*refs):
        ins, zones = refs[:n], refs[n:2 * n]
        send_sems, recv_sems = refs[2 * n], refs[2 * n + 1]
        sends, recvs = _exchange_copies(ins, zones, scatter, send_sems, recv_sems)
        for cp in sends:
            cp.wait_send()
        for cp in recvs:
            cp.wait_recv()

    thru = list(started["srcs"]) + list(started["lands"])
    res = pl.pallas_call(
        body, name=name, out_shape=[pltpu.HBM(a.shape, a.dtype) for a in thru],
        in_specs=[_HBM] * (2 * n) + [_SEM, _SEM, pl.BlockSpec(memory_space=pl.ANY)], out_specs=[_HBM] * (2 * n),
        input_output_aliases={i: i for i in range(2 * n)},
        compiler_params=pltpu.CompilerParams(has_side_effects=_DATAFLOW),
    )(*thru, *started["sems"], after)
    me = 4 * lax.axis_index("x") + 2 * lax.axis_index("y") + lax.axis_index("c")
    out = []
    for src, got, sc in zip(res[:n], res[n:], scatter):
        own = lax.dynamic_index_in_dim(src, me, 0, keepdims=False) if sc else src
        out.append(lax.dynamic_update_index_in_dim(got, own, me, 0))
    return out


def _adamw(parts, w, m, v, name, rows_per_step, row_offset=0, into=None):
    rows, cols = parts.shape[1:]
    tr = min(rows_per_step, rows)
    assert rows % tr == 0 and row_offset % tr == 0, (name, rows, tr, row_offset)
    first = row_offset // tr
    c1 = 1.0 / (1.0 - ADAM_B1 ** ADAM_STEP)
    c2 = 1.0 / (1.0 - ADAM_B2 ** ADAM_STEP)

    def body(p_ref, w_ref, m_ref, v_ref, *rest):
        g_ref, d_ref, nm_ref, nv_ref = rest[-4:]
        g = p_ref[0].astype(F32)
        for d in range(1, N_DEV):
            g = g + p_ref[d].astype(F32)
        nm = ADAM_B1 * m_ref[...] + (1.0 - ADAM_B1) * g
        nv = ADAM_B2 * v_ref[...] + (1.0 - ADAM_B2) * (g * g)
        g_ref[...] = g
        nm_ref[...] = nm
        nv_ref[...] = nv
        d_ref[...] = -ADAM_LR * ((nm * c1) / (jnp.sqrt(nv * c2) + ADAM_EPS) + ADAM_WD * w_ref[...])

    blk = pl.BlockSpec((tr, cols), lambda i: (i + first, 0))
    shape = jax.ShapeDtypeStruct(w.shape, F32)
    prior = [] if into is None else list(into)
    return pl.pallas_call(
        body, name=name, grid=(rows // tr,),
        in_specs=[pl.BlockSpec((N_DEV, tr, cols), lambda i: (0, i, 0)), blk, blk, blk]
        + [pl.BlockSpec(memory_space=pl.ANY)] * len(prior),
        out_specs=[blk] * 4, out_shape=[shape] * 4,
        input_output_aliases={4 + j: j for j in range(len(prior))}, compiler_params=_params(),
    )(parts, w, m, v, *prior)


_LAYERED = ("norm_w", "lru_conv_b", "lru_wa", "lru_ba", "lru_wx", "lru_bx", "lru_lambda", "lru_norm_w",
            "dn_A_log", "dn_dt_bias", "dn_norm_w")
_WEIGHTS = ("norm_w", "w_in", "lru_conv_w", "lru_conv_b", "lru_wa", "lru_ba", "lru_wx", "lru_bx", "lru_lambda",
            "lru_norm_w", "dn_conv_w", "dn_A_log", "dn_dt_bias", "dn_norm_w", "w_out", "final_norm_w")


def _pack_layer(tree, layer, tail=()):
    rows = []
    for name in _LAYERED:
        a = tree[name][layer]
        if a.shape[-1] == HEADS:
            a = jnp.pad(a, (0, HEAD_DIM - HEADS))
        rows.append(a.reshape(-1, HEAD_DIM))
    rows += [t.reshape(-1, HEAD_DIM) for t in tail]
    packed = jnp.concatenate(rows, axis=0)
    return jnp.pad(packed, ((0, (-packed.shape[0]) % 8), (0, 0)))


def _unpack_layer(packed, like):
    out, at = {}, 0
    for name in _LAYERED:
        shape = like[name].shape[1:]
        if shape[-1] == HEADS:
            n = 1
            out[name] = packed[at, :HEADS]
        else:
            n = like[name][0].size // HEAD_DIM
            out[name] = packed[at:at + n].reshape(shape)
        at += n
    return out, at


def _heads_to_channels(a):
    return jnp.transpose(a, (1, 0, 2)).reshape(a.shape[1], HEADS * HEAD_DIM)


def kernel(x, norm_w, w_in, lru_conv_w, lru_conv_b, lru_wa, lru_ba, lru_wx, lru_bx, lru_lambda, lru_norm_w, dn_conv_w, dn_A_log, dn_dt_bias, dn_norm_w, w_out, final_norm_w, loss_target, m_norm_w, m_w_in, m_lru_conv_w, m_lru_conv_b, m_lru_wa, m_lru_ba, m_lru_wx, m_lru_bx, m_lru_lambda, m_lru_norm_w, m_dn_conv_w, m_dn_A_log, m_dn_dt_bias, m_dn_norm_w, m_w_out, m_final_norm_w, v_norm_w, v_w_in, v_lru_conv_w, v_lru_conv_b, v_lru_wa, v_lru_ba, v_lru_wx, v_lru_bx, v_lru_lambda, v_lru_norm_w, v_dn_conv_w, v_dn_A_log, v_dn_dt_bias, v_dn_norm_w, v_w_out, v_final_norm_w):
    weights = dict(norm_w=norm_w, w_in=w_in, lru_conv_w=lru_conv_w, lru_conv_b=lru_conv_b, lru_wa=lru_wa,
                   lru_ba=lru_ba, lru_wx=lru_wx, lru_bx=lru_bx, lru_lambda=lru_lambda, lru_norm_w=lru_norm_w,
                   dn_conv_w=dn_conv_w, dn_A_log=dn_A_log, dn_dt_bias=dn_dt_bias, dn_norm_w=dn_norm_w,
                   w_out=w_out, final_norm_w=final_norm_w)
    mom_m = dict(norm_w=m_norm_w, w_in=m_w_in, lru_conv_w=m_lru_conv_w, lru_conv_b=m_lru_conv_b, lru_wa=m_lru_wa,
                 lru_ba=m_lru_ba, lru_wx=m_lru_wx, lru_bx=m_lru_bx, lru_lambda=m_lru_lambda,
                 lru_norm_w=m_lru_norm_w, dn_conv_w=m_dn_conv_w, dn_A_log=m_dn_A_log, dn_dt_bias=m_dn_dt_bias,
                 dn_norm_w=m_dn_norm_w, w_out=m_w_out, final_norm_w=m_final_norm_w)
    mom_v = dict(norm_w=v_norm_w, w_in=v_w_in, lru_conv_w=v_lru_conv_w, lru_conv_b=v_lru_conv_b, lru_wa=v_lru_wa,
                 lru_ba=v_lru_ba, lru_wx=v_lru_wx, lru_bx=v_lru_bx, lru_lambda=v_lru_lambda,
                 lru_norm_w=v_lru_norm_w, dn_conv_w=v_dn_conv_w, dn_A_log=v_dn_A_log, dn_dt_bias=v_dn_dt_bias,
                 dn_norm_w=v_dn_norm_w, w_out=v_w_out, final_norm_w=v_final_norm_w)
    depth = norm_w.shape[0]
    xs = x[0]
    s = xs.shape[0]
    tm = min(1024, s)

    assert depth >= 2, depth

    def row(a):
        return a.reshape(1, -1)

    def pad_row(a):
        return jnp.pad(a, (0, HEAD_DIM - a.shape[0])).reshape(1, HEAD_DIM)

    def full_w_in(g):
        w = jnp.transpose(g, (1, 2, 0, 3)).reshape(g.shape[1], D_MODEL, D_IN)
        return jnp.pad(w, ((0, 0), (0, 0), (0, D_IN_PAD - D_IN)))

    g_win0, g_lcw, g_dcw = _direct_exchange([w_in[:1].astype(BF16), lru_conv_w, dn_conv_w], [False] * 3,
                                            "gather_first")
    rest = _exchange_start([w_in[1:].astype(BF16), w_out.astype(BF16)], [False] * 2, "gather_rest_start")
    win = [full_w_in(g_win0)[0]]
    wout = None
    lcw = jnp.transpose(g_lcw, (1, 2, 0, 3)).reshape(depth, 4, D_MODEL)
    dcw = jnp.transpose(g_dcw, (1, 2, 0, 3)).reshape(depth, 4, 3 * D_MODEL)

    saved = []
    cur = xs
    for l in range(depth):
        nw_row = row(norm_w[l]) + rest["token"] if l == 0 else row(norm_w[l])
        hn = _rmsnorm_fwd(cur, nw_row, f"norm_fwd_{l}")
        proj = _matmul(hn, win[l], "nn", tm, 896, D_MODEL, f"in_proj_{l}")
        y_lru, hs = _lru_fwd(proj, lcw[l], row(lru_conv_b[l]), lru_wa[l], row(lru_ba[l]), lru_wx[l], row(lru_bx[l]),
                             row(lru_lambda[l]), row(lru_norm_w[l]), f"lru_fwd_{l}")
        ycat, o_dn, states = _dn_fwd(proj, y_lru, dcw[l], pad_row(dn_A_log[l]), pad_row(dn_dt_bias[l]),
                                     row(dn_norm_w[l]), f"dn_fwd_{l}")
        if l == 0:
            g_win_rest, g_wout = _exchange_wait(rest, ycat, "gather_rest_wait")
            win += list(full_w_in(g_win_rest))
            wout = jnp.transpose(g_wout, (1, 0, 2, 3)).reshape(depth, 2 * D_MODEL, D_MODEL)
        nxt = _matmul(ycat, wout[l], "nn", tm, D_MODEL, 2 * D_MODEL, f"out_proj_{l}", add=cur)
        saved.append((cur, hn, proj, hs, o_dn, states, ycat))
        cur = nxt
    loss_part, dx, d_final = _final_loss(cur, row(final_norm_w), loss_target[0], "final_loss")

    def win_slots(g):
        return jnp.transpose(g.reshape(D_MODEL, N_DEV, D_IN // N_DEV), (1, 0, 2))

    def wout_slots(g):
        return g.reshape(N_DEV, 2 * D_MODEL // N_DEV, D_MODEL)

    grads = {k: [None] * depth for k in _WEIGHTS if k not in ("final_norm_w", "w_in", "w_out")}
    started = {}
    token = None
    for l in reversed(range(depth)):
        x_in, hn, proj, hs, o_dn, states, ycat = saved[l]
        dy = _matmul(dx, wout[l], "nt", tm, D_MODEL, D_MODEL, f"out_proj_dy_{l}")
        g_wout_l = _matmul(ycat, dx, "tn", D_MODEL, D_MODEL, tm, f"out_proj_dw_{l}", out_dtype=BF16)
        if l == 0:
            started["w_out_0"] = _exchange_start([wout_slots(g_wout_l)], [True], "exchange_w_out_0_start")
            token = token + started["w_out_0"]["token"]
        cb_row = row(lru_conv_b[l]) if token is None else row(lru_conv_b[l]) + token
        (dlx, dlz, g_lcw, g_lcb, g_wa, g_ba, g_wx, g_bx, g_lam, g_lnw) = _lru_bwd(
            proj, hs, dy, lcw[l], cb_row, lru_wa[l], row(lru_ba[l]), lru_wx[l], row(lru_bx[l]),
            row(lru_lambda[l]), row(lru_norm_w[l]), f"lru_bwd_{l}")
        (dq, dk, dv, ddz, dba, g_dcw3, g_al, g_dt, g_dnw) = _dn_bwd(
            proj, o_dn, states, dy, dcw[l], pad_row(dn_A_log[l]), pad_row(dn_dt_bias[l]), row(dn_norm_w[l]),
            f"dn_bwd_{l}")
        dproj = jnp.concatenate([dlx, dlz, dq, dk, dv, ddz, dba.astype(BF16)], axis=1)
        g_win_l = _matmul(hn, dproj, "tn", D_MODEL, 896, tm, f"in_proj_dw_{l}", out_dtype=BF16)[:, :D_IN]
        dep = None
        if l == 0:
            started[0] = _exchange_start([win_slots(g_win_l)], [True], "exchange_0_start")
            dep = started[0]["token_block"]
        dh = _matmul(dproj, win[l], "nt", tm, D_MODEL, 896, f"in_proj_dh_{l}", dep=dep)
        dx, g_nw = _rmsnorm_bwd(x_in, row(norm_w[l]), dh, dx, f"norm_bwd_{l}")
        grads["norm_w"][l] = g_nw.reshape(D_MODEL)
        grads["lru_conv_w"][l] = _heads_to_channels(g_lcw)
        grads["lru_conv_b"][l] = g_lcb.reshape(D_MODEL)
        grads["lru_wa"][l] = g_wa
        grads["lru_ba"][l] = g_ba.reshape(D_MODEL)
        grads["lru_wx"][l] = g_wx
        grads["lru_bx"][l] = g_bx.reshape(D_MODEL)
        grads["lru_lambda"][l] = g_lam.reshape(D_MODEL)
        grads["lru_norm_w"][l] = g_lnw.reshape(D_MODEL)
        g_dcw3 = g_dcw3.reshape(HEADS, 3, 4, HEAD_DIM)
        grads["dn_conv_w"][l] = jnp.concatenate([_heads_to_channels(g_dcw3[:, i]) for i in range(3)], axis=1)
        grads["dn_A_log"][l] = g_al[0, :HEADS]
        grads["dn_dt_bias"][l] = g_dt[0, :HEADS]
        grads["dn_norm_w"][l] = g_dnw.reshape(HEAD_DIM)
        if l > 0:
            tail = (d_final, loss_part) if l == depth - 1 else ()
            started[l] = _exchange_start([win_slots(g_win_l), wout_slots(g_wout_l), _pack_layer(grads, l, tail)],
                                         [True, True, False], f"exchange_{l}_start")
            token = started[l]["token"]

    def conv_slots(a):
        dd, r, cc = a.shape
        return jnp.transpose(a.reshape(dd, r, N_DEV, cc // N_DEV), (2, 0, 1, 3))

    small = _exchange_start(
        [conv_slots(jnp.stack(grads["lru_conv_w"])), conv_slots(jnp.stack(grads["dn_conv_w"])), _pack_layer(grads, 0)],
        [True, True, False], "exchange_small_start")

    new = {}
    flat_in = (depth * D_MODEL, D_IN // N_DEV)
    flat_out = (depth * 2 * D_MODEL // N_DEV, D_MODEL)
    zero_row = jnp.zeros((1, HEAD_DIM), F32)

    def adamw_pack(parts, layer):
        tails = [(t, zero_row) if layer == depth - 1 else () for t in (final_norm_w, m_final_norm_w, v_final_norm_w)]
        return _adamw(parts, _pack_layer(weights, layer, tails[0]), _pack_layer(mom_m, layer, tails[1]),
                      _pack_layer(mom_v, layer, tails[2]), f"adamw_small_{layer}", parts.shape[1])

    def adamw_w_in(parts, layer, into):
        return _adamw(parts, w_in.reshape(flat_in), m_w_in.reshape(flat_in), v_w_in.reshape(flat_in),
                      f"adamw_w_in_{layer}", 256, layer * D_MODEL, into)

    def adamw_w_out(parts, layer, into):
        return _adamw(parts, w_out.reshape(flat_out), m_w_out.reshape(flat_out), v_w_out.reshape(flat_out),
                      f"adamw_w_out_{layer}", 256, layer * flat_out[0] // depth, into)

    acc_in = acc_out = None
    packs = [None] * depth
    after = small["token_block"]
    for l in reversed(range(1, depth)):
        r_win, r_wout, r_pack = _exchange_wait(started[l], after, f"exchange_{l}_wait")
        acc_in = adamw_w_in(r_win, l, acc_in)
        acc_out = adamw_w_out(r_wout, l, acc_out)
        packs[l] = adamw_pack(r_pack, l)
        after = packs[l][0]
    (r_wout,) = _exchange_wait(started["w_out_0"], after, "exchange_w_out_0_wait")
    acc_out = adamw_w_out(r_wout, 0, acc_out)
    (r_win,) = _exchange_wait(started[0], acc_out[0], "exchange_0_wait")
    acc_in = adamw_w_in(r_win, 0, acc_in)
    r_lcw, r_dcw, r_pack = _exchange_wait(small, acc_in[0], "exchange_small_wait")
    for name, parts in (("lru_conv_w", r_lcw), ("dn_conv_w", r_dcw)):
        w = weights[name]
        flat = (-1, w.shape[-1])
        outs = _adamw(parts.reshape((N_DEV,) + (w.size // w.shape[-1], w.shape[-1])), w.reshape(flat),
                      mom_m[name].reshape(flat), mom_v[name].reshape(flat), f"adamw_{name}", 8)
        new[name] = [a.reshape(w.shape) for a in outs]
    packs[0] = adamw_pack(r_pack, 0)
    new["w_in"] = [a.reshape(w_in.shape) for a in acc_in]
    new["w_out"] = [a.reshape(w_out.shape) for a in acc_out]
    for name in _LAYERED:
        new[name] = [jnp.stack([_unpack_layer(packs[l][i], weights)[0][name] for l in range(depth)]) for i in range(4)]
    tail_at = _unpack_layer(packs[depth - 1][0], weights)[1]
    rows_final = D_MODEL // HEAD_DIM
    new["final_norm_w"] = [packs[depth - 1][i][tail_at:tail_at + rows_final].reshape(D_MODEL) for i in range(4)]
    loss = packs[depth - 1][0][tail_at + rows_final, 0]
    out = [loss, dx.reshape(x.shape)]
    for i in range(4):
        out += [new[name][i] for name in _WEIGHTS]
    return tuple(out)
```

```python
import functools

import jax
import jax.numpy as jnp
from jax import lax
from jax.experimental import pallas as pl
from jax.experimental.pallas import tpu as pltpu

F32 = jnp.float32
BF16 = jnp.bfloat16

N_DEV = 8
D_MODEL = 1024
HEADS = 8
HEAD_DIM = 128
CHUNK = 64
D_IN = 6160
D_IN_PAD = 6272
COL_LRU_X, COL_LRU_Z, COL_Q, COL_K, COL_V, COL_DN_Z, COL_BA = 0, 8, 16, 24, 32, 40, 48
LRU_C = 8.0
EPS = 1e-6
ADAM_LR, ADAM_B1, ADAM_B2, ADAM_EPS, ADAM_WD, ADAM_STEP = 0.001, 0.9, 0.999, 1e-08, 0.01, 10
TIME_BLOCK = 1024
VMEM_LIMIT = 56 * 1024 * 1024

NN = (((1,), (0,)), ((), ()))
NT = (((1,), (1,)), ((), ()))
TN = (((0,), (0,)), ((), ()))


B_NN = (((2,), (1,)), ((0,), (0,)))
B_NT = (((2,), (2,)), ((0,), (0,)))
B_TN = (((1,), (1,)), ((0,), (0,)))


def _split_bf16(x):
    hi = x.astype(BF16)
    return hi, (x - hi.astype(F32)).astype(BF16)


def _dot(a, b, dims, prec):
    if prec == "bf16":
        return lax.dot_general(a.astype(BF16), b.astype(BF16), dims, preferred_element_type=F32)
    a1, a2 = _split_bf16(a)
    b1, b2 = _split_bf16(b)
    dg = functools.partial(lax.dot_general, dimension_numbers=dims, preferred_element_type=F32)
    return dg(a1, b1) + (dg(a1, b2) + dg(a2, b1))


def _make_mm(prec, nn_dims, nt_dims, tn_dims):
    @jax.custom_vjp
    def nn(a, b):
        return _dot(a, b, nn_dims, prec)

    @jax.custom_vjp
    def nt(a, b):
        return _dot(a, b, nt_dims, prec)

    @jax.custom_vjp
    def tn(a, b):
        return _dot(a, b, tn_dims, prec)

    nn.defvjp(lambda a, b: (_dot(a, b, nn_dims, prec), (a, b)),
              lambda r, g: (_dot(g, r[1], nt_dims, prec), _dot(r[0], g, tn_dims, prec)))
    nt.defvjp(lambda a, b: (_dot(a, b, nt_dims, prec), (a, b)),
              lambda r, g: (_dot(g, r[1], nn_dims, prec), _dot(g, r[0], tn_dims, prec)))
    tn.defvjp(lambda a, b: (_dot(a, b, tn_dims, prec), (a, b)),
              lambda r, g: (_dot(r[1], g, nt_dims, prec), _dot(r[0], g, nn_dims, prec)))
    return nn, nt, tn


_NN_B, _NT_B, _TN_B = _make_mm("bf16", NN, NT, TN)
_BNN, _BNT, _BTN = _make_mm("bf16", B_NN, B_NT, B_TN)


@jax.custom_vjp
def _unit_lower_inverse(a):
    n = a.shape[-1]
    eye = (lax.broadcasted_iota(jnp.int32, a.shape, 1) == lax.broadcasted_iota(jnp.int32, a.shape, 2)).astype(F32)
    dg = functools.partial(lax.dot_general, dimension_numbers=B_NN, preferred_element_type=F32)
    inv = eye - a
    pw = _dot(a, a, B_NN, "bf16x3")
    steps = n.bit_length() - 2
    for j in range(steps):
        i1, i2 = _split_bf16(inv)
        p1, p2 = _split_bf16(pw)
        square = j + 1 < steps
        by_hi = dg(jnp.concatenate([i1, i2, p1, p2] if square else [i1, i2], axis=1), p1)
        by_lo = dg(jnp.concatenate([i1, p1], axis=1) if square else i1, p2)
        inv = inv + (by_hi[:, :n] + (by_lo[:, :n] + by_hi[:, n:2 * n]))
        if square:
            pw = by_hi[:, 2 * n:3 * n] + (by_lo[:, n:] + by_hi[:, 3 * n:])
    return inv


def _uli_fwd(a):
    inv = _unit_lower_inverse(a)
    return inv, inv


def _uli_bwd(inv, g):
    return (-_dot(_dot(inv, g, B_TN, "bf16"), inv, B_NT, "bf16"),)


_unit_lower_inverse.defvjp(_uli_fwd, _uli_bwd)


def _rows2(y, m):
    return y[:, :m], y[:, m:]


@jax.custom_vjp
def _pair_nn(x1, x2, r):
    return _rows2(_dot(jnp.concatenate([x1, x2], axis=1), r, B_NN, "bf16"), x1.shape[1])


def _pair_nn_bwd(res, g):
    x1, x2, r = res
    g = jnp.concatenate(g, axis=1)
    dx1, dx2 = _rows2(_dot(g, r, B_NT, "bf16"), x1.shape[1])
    return dx1, dx2, _dot(jnp.concatenate([x1, x2], axis=1), g, B_TN, "bf16")


_pair_nn.defvjp(lambda x1, x2, r: (_pair_nn(x1, x2, r), (x1, x2, r)), _pair_nn_bwd)


@jax.custom_vjp
def _pair_nt(x1, x2, r):
    return _rows2(_dot(jnp.concatenate([x1, x2], axis=1), r, B_NT, "bf16"), x1.shape[1])


def _pair_nt_bwd(res, g):
    x1, x2, r = res
    g = jnp.concatenate(g, axis=1)
    dx1, dx2 = _rows2(_dot(g, r, B_NN, "bf16"), x1.shape[1])
    return dx1, dx2, _dot(g, jnp.concatenate([x1, x2], axis=1), B_TN, "bf16")


_pair_nt.defvjp(lambda x1, x2, r: (_pair_nt(x1, x2, r), (x1, x2, r)), _pair_nt_bwd)


@jax.custom_vjp
def _wide_nn(l, r1, r2):
    y = _dot(l, jnp.concatenate([r1, r2], axis=2), B_NN, "bf16")
    return y[:, :, :r1.shape[2]], y[:, :, r1.shape[2]:]


def _wide_nn_bwd(res, g):
    l, r1, r2 = res
    g = jnp.concatenate(g, axis=2)
    dr = _dot(l, g, B_TN, "bf16")
    return (_dot(g, jnp.concatenate([r1, r2], axis=2), B_NT, "bf16"), dr[:, :, :r1.shape[2]], dr[:, :, r1.shape[2]:])


_wide_nn.defvjp(lambda l, r1, r2: (_wide_nn(l, r1, r2), (l, r1, r2)), _wide_nn_bwd)


def _lower_ones(batch, n):
    shape = (batch, n, n)
    return (lax.broadcasted_iota(jnp.int32, shape, 1) >= lax.broadcasted_iota(jnp.int32, shape, 2)).astype(BF16)


@jax.custom_vjp
def _chunk_cumsum(g):
    tri = _lower_ones(g.shape[0], g.shape[1])
    g1, g2 = _split_bf16(g)
    g3 = (g - g1.astype(F32) - g2.astype(F32)).astype(BF16)
    dg = functools.partial(lax.dot_general, dimension_numbers=B_NN, preferred_element_type=F32)
    return dg(tri, g1) + (dg(tri, g2) + dg(tri, g3))


def _chunk_cumsum_bwd(_, ct):
    tri = _lower_ones(ct.shape[0], ct.shape[1])
    c1, c2 = _split_bf16(ct)
    dg = functools.partial(lax.dot_general, dimension_numbers=B_TN, preferred_element_type=F32)
    return (dg(tri, c1) + dg(tri, c2),)


_chunk_cumsum.defvjp(lambda g: (_chunk_cumsum(g), None), _chunk_cumsum_bwd)


def _expm1(x):
    small = x * (1.0 + x * (0.5 + x * (1.0 / 6 + x * (1.0 / 24 + x * (1.0 / 120 + x * (1.0 / 720))))))
    return jnp.where(jnp.abs(x) < 0.2, small, jnp.exp(x) - 1.0)


def _sigmoid(x):
    return 1.0 / (1.0 + jnp.exp(-x))


def _silu(x):
    return x * _sigmoid(x)


def _softplus(x):
    return jnp.maximum(x, 0.0) + jnp.log(1.0 + jnp.exp(-jnp.abs(x)))


def _rmsnorm(x, w):
    return x * lax.rsqrt(jnp.mean(x * x, axis=-1, keepdims=True) + EPS) * w


def _gated_norm(o, z, w):
    return o * lax.rsqrt(jnp.mean(o * o, axis=-1, keepdims=True) + EPS) * w * _silu(z)


def _lru_gates(xc, wa, ba, wx, bx, lam):
    r = _sigmoid(_NN_B(xc, wa) + ba)
    i = _sigmoid(_NN_B(xc, wx) + bx)
    log_a = -LRU_C * r * _softplus(-lam)
    a = jnp.exp(log_a)
    mult = jnp.sqrt(-_expm1(2.0 * log_a))
    return a, mult * (i * xc)


def _scan_forward(a, b, h0):
    rows = a.shape[0]
    row = lax.broadcasted_iota(jnp.int32, a.shape, 0)
    k = 1
    while k < rows:
        seen = row >= k
        b = jnp.where(seen, a * pltpu.roll(b, k, 0) + b, b)
        a = jnp.where(seen, a * pltpu.roll(a, k, 0), a)
        k *= 2
    return b + a * h0


def _scan_reverse(a, d, carry):
    rows = a.shape[0]
    row = lax.broadcasted_iota(jnp.int32, a.shape, 0)
    last = row == rows - 1
    c = jnp.where(last, 0.0, pltpu.roll(a, rows - 1, 0))
    d = d + jnp.where(last, carry, 0.0)
    k = 1
    while k < rows:
        seen = row < rows - k
        d = jnp.where(seen, d + c * pltpu.roll(d, rows - k, 0), d)
        c = jnp.where(seen, c * pltpu.roll(c, rows - k, 0), c)
        k *= 2
    return d


def _lane_pick(row, lane_index):
    lane = lax.broadcasted_iota(jnp.int32, row.shape, 1)
    return jnp.sum(jnp.where(lane == lane_index, row, 0.0), axis=-1, keepdims=True)


def _dn_prep(qc, kc, vc, ba, a_log_row, dt_row, head):
    q = _silu(qc)
    k = _silu(kc)
    v = _silu(vc)
    q = q * lax.rsqrt(jnp.sum(q * q, axis=-1, keepdims=True) + EPS) * (HEAD_DIM ** -0.5)
    k = k * lax.rsqrt(jnp.sum(k * k, axis=-1, keepdims=True) + EPS)
    beta = _sigmoid(_lane_pick(ba, head))
    g = -jnp.exp(_lane_pick(a_log_row, head)) * _softplus(_lane_pick(ba, HEADS + head) + _lane_pick(dt_row, head))
    return q, k, v, g, beta


def _dn_chunks_head(q, k, v, gcol, bcol):
    n, c, d = q.shape
    row = lax.broadcasted_iota(jnp.int32, (n, c, c), 1)
    col = lax.broadcasted_iota(jnp.int32, (n, c, c), 2)
    g_wide = jnp.broadcast_to(gcol, (n, c, d))
    b_wide = jnp.broadcast_to(bcol, (n, c, d))
    gc = _chunk_cumsum(g_wide)
    gc_rows = gc[:, :, :c]
    decay = jnp.exp(jnp.where(row >= col, gc_rows - jnp.swapaxes(gc_rows, 1, 2), -1e30))
    kb = k * b_wide
    eg = jnp.exp(gc)
    kbk, qk = _pair_nt(kb, q, k)
    tinv = _unit_lower_inverse(jnp.where(row > col, kbk * decay, 0.0))
    u, w = _wide_nn(tinv, v * b_wide, kb * eg)
    g_last = jnp.sum(g_wide, axis=1, keepdims=True)
    return u, w, qk * decay, q * eg, k * jnp.exp(g_last - gc), jnp.exp(g_last)


def _dn_chunks(q, k, v, gcol, bcol, states):
    u, w, attn, qe, kdec, eglast = _dn_chunks_head(q, k, v, gcol, bcol)
    w_st, qe_st = _pair_nn(w, qe, states)
    v_new = u - w_st
    o = qe_st + _BNN(attn, v_new)
    return (o, states * eglast + _BTN(kdec, v_new)), (w, attn, qe, kdec, eglast)


def _conv_taps(buf, head, cw, rows):
    acc = cw[0:1, :] * buf[head, pl.ds(5, rows), :]
    for j in range(1, 4):
        acc = acc + cw[j:j + 1, :] * buf[head, pl.ds(5 + j, rows), :]
    return acc


def _conv_backward(dbuf, dhead, xbuf, xhead, cw, dxc, rows):
    dbuf[dhead, pl.ds(0, rows), :] = dxc
    dx = cw[0:1, :] * dbuf[dhead, pl.ds(3, rows), :]
    for j in range(1, 4):
        dx = dx + cw[j:j + 1, :] * dbuf[dhead, pl.ds(3 - j, rows), :]
    dcw = jnp.concatenate(
        [jnp.sum(dxc * xbuf[xhead, pl.ds(5 + j, rows), :], axis=0, keepdims=True) for j in range(4)], axis=0)
    dbuf[dhead, pl.ds(rows, 8), :] = dbuf[dhead, pl.ds(0, 8), :]
    return dx, dcw


def _params(**kw):
    return pltpu.CompilerParams(vmem_limit_bytes=VMEM_LIMIT, **kw)


def _matmul(a, b, form, tm, tn, tk, name, add=None, out_dtype=F32, dep=None):
    if form == "nn":
        (m, kdim), (_, n) = a.shape, b.shape
        a_spec = pl.BlockSpec((tm, tk), lambda j, i, k: (i, k))
        b_spec = pl.BlockSpec((tk, tn), lambda j, i, k: (k, j))
        dims = NN
    elif form == "nt":
        (m, kdim), (n, _) = a.shape, b.shape
        a_spec = pl.BlockSpec((tm, tk), lambda j, i, k: (i, k))
        b_spec = pl.BlockSpec((tn, tk), lambda j, i, k: (j, k))
        dims = NT
    else:
        (kdim, m), (_, n) = a.shape, b.shape
        a_spec = pl.BlockSpec((tk, tm), lambda j, i, k: (k, i))
        b_spec = pl.BlockSpec((tk, tn), lambda j, i, k: (k, j))
        dims = TN
    assert m % tm == 0 and n % tn == 0 and kdim % tk == 0, (name, m, n, kdim, tm, tn, tk)
    ksteps = kdim // tk
    o_spec = pl.BlockSpec((tm, tn), lambda j, i, k: (i, j))
    has_add = add is not None
    extra = [] if dep is None else [dep]

    def body(*refs):
        a_ref, b_ref = refs[:2]
        c_ref = refs[2] if has_add else None
        o_ref, acc = refs[-2:]
        k = pl.program_id(2)

        @pl.when(k == 0)
        def _():
            acc[...] = c_ref[...] if has_add else jnp.zeros_like(acc)

        acc[...] += lax.dot_general(a_ref[...].astype(BF16), b_ref[...].astype(BF16), dims,
                                    preferred_element_type=F32)

        @pl.when(k == ksteps - 1)
        def _():
            o_ref[...] = acc[...].astype(o_ref.dtype)

    in_specs = [a_spec, b_spec] + ([o_spec] if has_add else []) + [pl.BlockSpec((8, HEAD_DIM), lambda j, i, k: (0, 0))
                                                                   for _ in extra]
    args = (a, b) + ((add,) if has_add else ()) + tuple(extra)
    return pl.pallas_call(
        body, name=name, grid=(n // tn, m // tm, ksteps), in_specs=in_specs, out_specs=o_spec,
        out_shape=jax.ShapeDtypeStruct((m, n), out_dtype), scratch_shapes=[pltpu.VMEM((tm, tn), F32)],
        compiler_params=_params(dimension_semantics=("parallel", "parallel", "arbitrary")),
    )(*args)


def _rmsnorm_fwd(x, w_row, name):
    s = x.shape[0]
    tb = min(TIME_BLOCK, s)

    def body(x_ref, w_ref, o_ref):
        o_ref[...] = _rmsnorm(x_ref[...], w_ref[...]).astype(BF16)

    return pl.pallas_call(
        body, name=name, grid=(s // tb,),
        in_specs=[pl.BlockSpec((tb, D_MODEL), lambda i: (i, 0)), pl.BlockSpec((1, D_MODEL), lambda i: (0, 0))],
        out_specs=pl.BlockSpec((tb, D_MODEL), lambda i: (i, 0)),
        out_shape=jax.ShapeDtypeStruct((s, D_MODEL), BF16), compiler_params=_params(),
    )(x, w_row)


def _rmsnorm_bwd(x, w_row, dh, dres, name):
    s = x.shape[0]
    tb = min(TIME_BLOCK, s)

    def body(x_ref, w_ref, dh_ref, dres_ref, dx_ref, dw_ref):
        _, vjp = jax.vjp(_rmsnorm, x_ref[...], w_ref[...])
        dx, dw = vjp(dh_ref[...])
        dx_ref[...] = dres_ref[...] + dx

        @pl.when(pl.program_id(0) == 0)
        def _():
            dw_ref[...] = jnp.zeros_like(dw_ref)

        dw_ref[...] += dw

    row = pl.BlockSpec((tb, D_MODEL), lambda i: (i, 0))
    vec = pl.BlockSpec((1, D_MODEL), lambda i: (0, 0))
    return pl.pallas_call(
        body, name=name, grid=(s // tb,), in_specs=[row, vec, row, row], out_specs=[row, vec],
        out_shape=[jax.ShapeDtypeStruct((s, D_MODEL), F32), jax.ShapeDtypeStruct((1, D_MODEL), F32)],
        compiler_params=_params(),
    )(x, w_row, dh, dres)


def _final_loss(x, w_row, target, name):
    s = x.shape[0]
    tb = min(TIME_BLOCK, s)

    def loss_fn(xv, wv, tv):
        err = _rmsnorm(xv, wv) - tv
        return 0.5 * jnp.sum(jnp.sum(err * err, axis=-1, keepdims=True), axis=0, keepdims=True) * (1.0 / D_MODEL)

    def body(x_ref, w_ref, t_ref, loss_ref, dx_ref, dw_ref):
        tv = t_ref[...]
        loss, vjp = jax.vjp(lambda xv, wv: loss_fn(xv, wv, tv), x_ref[...], w_ref[...])
        dx, dw = vjp(jnp.ones((1, 1), F32))
        dx_ref[...] = dx

        @pl.when(pl.program_id(0) == 0)
        def _():
            dw_ref[...] = jnp.zeros_like(dw_ref)
            loss_ref[...] = jnp.zeros_like(loss_ref)

        dw_ref[...] += dw
        loss_ref[...] += jnp.broadcast_to(loss, loss_ref.shape)

    row = pl.BlockSpec((tb, D_MODEL), lambda i: (i, 0))
    vec = pl.BlockSpec((1, D_MODEL), lambda i: (0, 0))
    return pl.pallas_call(
        body, name=name, grid=(s // tb,), in_specs=[row, vec, row],
        out_specs=[pl.BlockSpec((1, HEAD_DIM), lambda i: (0, 0)), row, vec],
        out_shape=[jax.ShapeDtypeStruct((1, HEAD_DIM), F32), jax.ShapeDtypeStruct((s, D_MODEL), F32),
                   jax.ShapeDtypeStruct((1, D_MODEL), F32)],
        compiler_params=_params(),
    )(x, w_row, target)


def _head_specs(tb, time_of):
    def col(off):
        return pl.BlockSpec((tb, HEAD_DIM), lambda t, h: (time_of(t), off + h))
    return col


def _vec_spec():
    return pl.BlockSpec((1, HEAD_DIM), lambda t, h: (0, h))


def _lru_fwd(proj, conv_w, conv_b, wa, ba, wx, bx, lam, nw, name):
    s = proj.shape[0]
    tb = min(TIME_BLOCK, s)
    nt = s // tb
    col = _head_specs(tb, lambda t: t)

    def body(x_ref, z_ref, cw_ref, cb_ref, wa_ref, ba_ref, wx_ref, bx_ref, lam_ref, nw_ref,
             y_ref, hs_ref, xbuf, hcar):
        t, h = pl.program_id(0), pl.program_id(1)

        @pl.when(t == 0)
        def _():
            xbuf[h, pl.ds(0, 8), :] = jnp.zeros((8, HEAD_DIM), F32)
            hcar[h] = jnp.zeros((8, HEAD_DIM), F32)

        xbuf[h, pl.ds(8, tb), :] = x_ref[...]
        xc = _conv_taps(xbuf, h, cw_ref[...], tb) + cb_ref[...]
        a, b = _lru_gates(xc, wa_ref[...], ba_ref[...], wx_ref[...], bx_ref[...], lam_ref[...])
        hs_ref[...] = _scan_forward(a, b, hcar[h, pl.ds(0, 1), :])
        hcar[h, pl.ds(0, 1), :] = hs_ref[pl.ds(tb - 1, 1), :]
        xbuf[h, pl.ds(0, 8), :] = xbuf[h, pl.ds(tb, 8), :]
        y_ref[...] = _gated_norm(hs_ref[...], z_ref[...], nw_ref[...]).astype(BF16)

    vec = _vec_spec()
    return pl.pallas_call(
        body, name=name, grid=(nt, HEADS),
        in_specs=[col(COL_LRU_X), col(COL_LRU_Z), pl.BlockSpec((4, HEAD_DIM), lambda t, h: (0, h)), vec,
                  pl.BlockSpec((None, HEAD_DIM, HEAD_DIM), lambda t, h: (h, 0, 0)), vec,
                  pl.BlockSpec((None, HEAD_DIM, HEAD_DIM), lambda t, h: (h, 0, 0)), vec, vec, vec],
        out_specs=[col(0), col(0)],
        out_shape=[jax.ShapeDtypeStruct((s, 2 * D_MODEL), BF16), jax.ShapeDtypeStruct((s, D_MODEL), F32)],
        scratch_shapes=[pltpu.VMEM((HEADS, tb + 8, HEAD_DIM), F32), pltpu.VMEM((HEADS, 8, HEAD_DIM), F32)],
        compiler_params=_params(dimension_semantics=("arbitrary", "arbitrary")),
    )(proj, proj, conv_w, conv_b, wa, ba, wx, bx, lam, nw)


def _halo_spec(tb, nt, off):
    per = tb // 8
    return pl.BlockSpec((8, HEAD_DIM), lambda t, h: (jnp.maximum((nt - 1 - t) * per - 1, 0), off + h))


def _lru_bwd(proj, hs, dy, conv_w, conv_b, wa, ba, wx, bx, lam, nw, name):
    s = proj.shape[0]
    tb = min(TIME_BLOCK, s)
    nt = s // tb
    col = _head_specs(tb, lambda t: nt - 1 - t)

    def body(x_ref, xh_ref, z_ref, hs_ref, hh_ref, dy_ref, cw_ref, cb_ref, wa_ref, ba_ref, wx_ref, bx_ref,
             lam_ref, nw_ref, dx_ref, dz_ref, dcw_ref, dcb_ref, dwa_ref, dba_ref, dwx_ref, dbx_ref, dlam_ref,
             dnw_ref, xbuf, hbuf, dbuf, gcar):
        t, h = pl.program_id(0), pl.program_id(1)
        first_block = t == nt - 1

        @pl.when(t == 0)
        def _():
            dbuf[h, pl.ds(tb, 8), :] = jnp.zeros((8, HEAD_DIM), F32)
            gcar[h] = jnp.zeros((8, HEAD_DIM), F32)
            dcw_ref[h] = jnp.zeros((4, HEAD_DIM), F32)
            dwa_ref[h] = jnp.zeros((HEAD_DIM, HEAD_DIM), F32)
            dwx_ref[h] = jnp.zeros((HEAD_DIM, HEAD_DIM), F32)
            for ref in (dcb_ref, dba_ref, dbx_ref, dlam_ref, dnw_ref):
                ref[h] = jnp.zeros((1, HEAD_DIM), F32)

        keep = jnp.where(first_block, 0.0, 1.0)
        xbuf[0, pl.ds(0, 8), :] = xh_ref[...] * keep
        xbuf[0, pl.ds(8, tb), :] = x_ref[...]
        hbuf[pl.ds(0, 8), :] = hh_ref[...] * keep
        hbuf[pl.ds(8, tb), :] = hs_ref[...]
        cw = cw_ref[...]
        xc = _conv_taps(xbuf, 0, cw, tb) + cb_ref[...]
        (a, _), gates_vjp = jax.vjp(_lru_gates, xc, wa_ref[...], ba_ref[...], wx_ref[...], bx_ref[...], lam_ref[...])
        _, norm_vjp = jax.vjp(_gated_norm, hs_ref[...], z_ref[...], nw_ref[...])
        dh, dz, dnw = norm_vjp(dy_ref[...])
        dz_ref[...] = dz.astype(dz_ref.dtype)
        g = _scan_reverse(a, dh, gcar[h, pl.ds(0, 1), :])
        gcar[h, pl.ds(0, 1), :] = a[0:1, :] * g[0:1, :]
        dxc, dwa, dba, dwx, dbx, dlam = gates_vjp((g * hbuf[pl.ds(7, tb), :], g))
        dx, dcw = _conv_backward(dbuf, h, xbuf, 0, cw, dxc, tb)
        dx_ref[...] = dx.astype(dx_ref.dtype)
        dcw_ref[h] += dcw
        dcb_ref[h] += jnp.sum(dxc, axis=0, keepdims=True)
        dwa_ref[h] += dwa
        dwx_ref[h] += dwx
        dba_ref[h] += dba
        dbx_ref[h] += dbx
        dlam_ref[h] += dlam
        dnw_ref[h] += dnw

    vec = _vec_spec()
    mat = pl.BlockSpec((None, HEAD_DIM, HEAD_DIM), lambda t, h: (h, 0, 0))

    def whole(shape):
        return pl.BlockSpec(shape, lambda t, h: (0,) * len(shape))

    head_vec = jax.ShapeDtypeStruct((HEADS, 1, HEAD_DIM), F32)
    head_mat = jax.ShapeDtypeStruct((HEADS, HEAD_DIM, HEAD_DIM), F32)
    return pl.pallas_call(
        body, name=name, grid=(nt, HEADS),
        in_specs=[col(COL_LRU_X), _halo_spec(tb, nt, COL_LRU_X), col(COL_LRU_Z), col(0), _halo_spec(tb, nt, 0), col(0),
                  pl.BlockSpec((4, HEAD_DIM), lambda t, h: (0, h)), vec, mat, vec, mat, vec, vec, vec],
        out_specs=[col(0), col(0), whole((HEADS, 4, HEAD_DIM)), whole((HEADS, 1, HEAD_DIM)),
                   whole((HEADS, HEAD_DIM, HEAD_DIM)), whole((HEADS, 1, HEAD_DIM)),
                   whole((HEADS, HEAD_DIM, HEAD_DIM)), whole((HEADS, 1, HEAD_DIM)), whole((HEADS, 1, HEAD_DIM)),
                   whole((HEADS, 1, HEAD_DIM))],
        out_shape=[jax.ShapeDtypeStruct((s, D_MODEL), BF16), jax.ShapeDtypeStruct((s, D_MODEL), BF16),
                   jax.ShapeDtypeStruct((HEADS, 4, HEAD_DIM), F32), head_vec, head_mat, head_vec, head_mat, head_vec,
                   head_vec, head_vec],
        scratch_shapes=[pltpu.VMEM((1, tb + 8, HEAD_DIM), F32), pltpu.VMEM((tb + 8, HEAD_DIM), F32),
                        pltpu.VMEM((HEADS, tb + 8, HEAD_DIM), F32), pltpu.VMEM((HEADS, 8, HEAD_DIM), F32)],
        compiler_params=_params(dimension_semantics=("arbitrary", "arbitrary")),
    )(proj, proj, proj, hs, hs, dy, conv_w, conv_b, wa, ba, wx, bx, lam, nw)


def _dn_fwd(proj, y, conv_w, a_log_row, dt_row, nw, name):
    s = proj.shape[0]
    tb = min(TIME_BLOCK, s)
    nt = s // tb
    nchunk = tb // CHUNK
    col = _head_specs(tb, lambda t: t)

    def body(q_ref, k_ref, v_ref, z_ref, ba_ref, cwq_ref, cwk_ref, cwv_ref, al_ref, dt_ref, nw_ref, y_in_ref,
             y_ref, o_ref, st_ref, xbuf, state):
        t, h = pl.program_id(0), pl.program_id(1)

        @pl.when(t == 0)
        def _():
            for i in range(3):
                xbuf[3 * h + i, pl.ds(0, 8), :] = jnp.zeros((8, HEAD_DIM), F32)
            state[h] = jnp.zeros((HEAD_DIM, HEAD_DIM), F32)

        conv = []
        for i, (ref, cw_ref) in enumerate(((q_ref, cwq_ref), (k_ref, cwk_ref), (v_ref, cwv_ref))):
            xbuf[3 * h + i, pl.ds(8, tb), :] = ref[...]
            conv.append(_conv_taps(xbuf, 3 * h + i, cw_ref[...], tb))
            xbuf[3 * h + i, pl.ds(0, 8), :] = xbuf[3 * h + i, pl.ds(tb, 8), :]
        q, k, v, g, beta = _dn_prep(conv[0], conv[1], conv[2], ba_ref[...], al_ref[...], dt_ref[...], h)
        def chunks(a):
            return a.reshape(nchunk, CHUNK, a.shape[-1])

        u, w, attn, qe, kdec, eglast = _dn_chunks_head(chunks(q), chunks(k), chunks(v), chunks(g), chunks(beta))
        w_u = jnp.concatenate([w, u], axis=2)
        kdec_w_u = _dot(kdec, w_u, B_TN, "bf16")
        attn_w_u = _dot(attn, w_u, B_NN, "bf16")
        st = state[h]
        for c in range(nchunk):
            st_ref[c] = st
            st = st * eglast[c] - _NN_B(kdec_w_u[c, :, :HEAD_DIM], st) + kdec_w_u[c, :, HEAD_DIM:]
        state[h] = st
        o = _dot(qe - attn_w_u[:, :, :HEAD_DIM], st_ref[...], B_NN, "bf16") + attn_w_u[:, :, HEAD_DIM:]
        o_ref[...] = o.reshape(tb, HEAD_DIM)
        y_ref[...] = _gated_norm(o_ref[...], z_ref[...], nw_ref[...]).astype(BF16)

    def cw_spec(off):
        return pl.BlockSpec((4, HEAD_DIM), lambda t, h: (0, off + h))

    row128 = pl.BlockSpec((1, HEAD_DIM), lambda t, h: (0, 0))
    return pl.pallas_call(
        body, name=name, grid=(nt, HEADS),
        in_specs=[col(COL_Q), col(COL_K), col(COL_V), col(COL_DN_Z),
                  pl.BlockSpec((tb, HEAD_DIM), lambda t, h: (t, COL_BA)),
                  cw_spec(0), cw_spec(HEADS), cw_spec(2 * HEADS), row128, row128, row128,
                  pl.BlockSpec(memory_space=pl.ANY)],
        out_specs=[col(HEADS), col(0), pl.BlockSpec((None, nchunk, HEAD_DIM, HEAD_DIM), lambda t, h: (h, t, 0, 0))],
        out_shape=[jax.ShapeDtypeStruct((s, 2 * D_MODEL), BF16), jax.ShapeDtypeStruct((s, D_MODEL), F32),
                   jax.ShapeDtypeStruct((HEADS, s // CHUNK, HEAD_DIM, HEAD_DIM), F32)],
        input_output_aliases={11: 0},
        scratch_shapes=[pltpu.VMEM((3 * HEADS, tb + 8, HEAD_DIM), F32), pltpu.VMEM((HEADS, HEAD_DIM, HEAD_DIM), F32)],
        compiler_params=_params(dimension_semantics=("arbitrary", "arbitrary")),
    )(proj, proj, proj, proj, proj, conv_w, conv_w, conv_w, a_log_row, dt_row, nw, y)


def _dn_bwd(proj, o, states, dy, conv_w, a_log_row, dt_row, nw, name):
    s = proj.shape[0]
    tb = min(TIME_BLOCK, s)
    nt = s // tb
    nchunk = tb // CHUNK
    col = _head_specs(tb, lambda t: nt - 1 - t)

    def body(q_ref, qh_ref, k_ref, kh_ref, v_ref, vh_ref, z_ref, ba_ref, o_ref, st_ref, dy_ref,
             cwq_ref, cwk_ref, cwv_ref, al_ref, dt_ref, nw_ref,
             dq_ref, dk_ref, dv_ref, dz_ref, dba_ref, dcw_ref, dal_ref, ddt_ref, dnw_ref,
             xbuf, dbuf, dstate, dst_s):
        t, h = pl.program_id(0), pl.program_id(1)
        first_block = t == nt - 1

        @pl.when(t == 0)
        def _():
            for i in range(3):
                dbuf[3 * h + i, pl.ds(tb, 8), :] = jnp.zeros((8, HEAD_DIM), F32)
                dcw_ref[3 * h + i] = jnp.zeros((4, HEAD_DIM), F32)
            dstate[h] = jnp.zeros((HEAD_DIM, HEAD_DIM), F32)

        @pl.when((t == 0) & (h == 0))
        def _():
            for ref in (dal_ref, ddt_ref, dnw_ref):
                ref[...] = jnp.zeros_like(ref)

        keep = jnp.where(first_block, 0.0, 1.0)
        cws = (cwq_ref[...], cwk_ref[...], cwv_ref[...])
        conv = []
        for i, (ref, halo) in enumerate(((q_ref, qh_ref), (k_ref, kh_ref), (v_ref, vh_ref))):
            xbuf[i, pl.ds(0, 8), :] = halo[...] * keep
            xbuf[i, pl.ds(8, tb), :] = ref[...]
            conv.append(_conv_taps(xbuf, i, cws[i], tb))
        (q, k, v, g, beta), prep_vjp = jax.vjp(
            lambda qc, kc, vc, ba, al, dt: _dn_prep(qc, kc, vc, ba, al, dt, h),
            conv[0], conv[1], conv[2], ba_ref[...], al_ref[...], dt_ref[...])
        _, norm_vjp = jax.vjp(_gated_norm, o_ref[...], z_ref[...], nw_ref[...])
        do, dz, dnw = norm_vjp(dy_ref[...])
        dz_ref[...] = dz.astype(dz_ref.dtype)
        dnw_ref[...] += dnw

        def chunks(a):
            return a.reshape(nchunk, CHUNK, a.shape[-1])

        do = chunks(do)
        _, chunks_vjp, (w, attn, qe, kdec, eglast) = jax.vjp(
            _dn_chunks, chunks(q), chunks(k), chunks(v), chunks(g), chunks(beta), st_ref[...], has_aux=True)
        kdec_w = _dot(kdec, w, B_TN, "bf16")
        fixed = _dot(qe, do, B_TN, "bf16") - _dot(w, _dot(attn, do, B_TN, "bf16"), B_TN, "bf16")
        dst = dstate[h]
        for c in reversed(range(nchunk)):
            dst_s[c] = dst
            dst = dst * eglast[c] - _dot(kdec_w[c], dst, TN, "bf16") + fixed[c]
        dstate[h] = dst
        dq, dk, dv, dg, db, _ = chunks_vjp((do, dst_s[...]))

        def rows(a):
            return a.reshape(tb, a.shape[-1])

        dqc, dkc, dvc, dba, dal, ddt = prep_vjp((rows(dq), rows(dk), rows(dv), rows(dg), rows(db)))
        for i, (dxc, out) in enumerate(((dqc, dq_ref), (dkc, dk_ref), (dvc, dv_ref))):
            dx, dcw = _conv_backward(dbuf, 3 * h + i, xbuf, i, cws[i], dxc, tb)
            out[...] = dx.astype(out.dtype)
            dcw_ref[3 * h + i] += dcw
        dal_ref[...] += dal
        ddt_ref[...] += ddt

        @pl.when(h == 0)
        def _():
            dba_ref[...] = dba.astype(dba_ref.dtype)

        @pl.when(h > 0)
        def _():
            dba_ref[...] += dba.astype(dba_ref.dtype)

    def cw_spec(off):
        return pl.BlockSpec((4, HEAD_DIM), lambda t, h: (0, off + h))

    def whole(shape):
        return pl.BlockSpec(shape, lambda t, h: (0,) * len(shape))

    row128 = whole((1, HEAD_DIM))
    blk = (tb, HEAD_DIM)
    act = jax.ShapeDtypeStruct((s, D_MODEL), BF16)
    row_out = jax.ShapeDtypeStruct((1, HEAD_DIM), F32)
    return pl.pallas_call(
        body, name=name, grid=(nt, HEADS),
        in_specs=[col(COL_Q), _halo_spec(tb, nt, COL_Q), col(COL_K), _halo_spec(tb, nt, COL_K),
                  col(COL_V), _halo_spec(tb, nt, COL_V), col(COL_DN_Z),
                  pl.BlockSpec(blk, lambda t, h: (nt - 1 - t, COL_BA)), col(0),
                  pl.BlockSpec((None, nchunk, HEAD_DIM, HEAD_DIM), lambda t, h: (h, nt - 1 - t, 0, 0)), col(HEADS),
                  cw_spec(0), cw_spec(HEADS), cw_spec(2 * HEADS), row128, row128, row128],
        out_specs=[col(0), col(0), col(0), col(0), pl.BlockSpec(blk, lambda t, h: (nt - 1 - t, 0)),
                   whole((3 * HEADS, 4, HEAD_DIM)), row128, row128, row128],
        out_shape=[act, act, act, act, jax.ShapeDtypeStruct((s, HEAD_DIM), F32),
                   jax.ShapeDtypeStruct((3 * HEADS, 4, HEAD_DIM), F32), row_out, row_out, row_out],
        scratch_shapes=[pltpu.VMEM((3, tb + 8, HEAD_DIM), F32), pltpu.VMEM((3 * HEADS, tb + 8, HEAD_DIM), F32),
                        pltpu.VMEM((HEADS, HEAD_DIM, HEAD_DIM), F32), pltpu.VMEM((nchunk, HEAD_DIM, HEAD_DIM), F32)],
        compiler_params=_params(dimension_semantics=("arbitrary", "arbitrary")),
    )(proj, proj, proj, proj, proj, proj, proj, proj, o, states, dy, conv_w, conv_w, conv_w, a_log_row, dt_row, nw)


def _mesh_position():
    x, y, c = lax.axis_index("x"), lax.axis_index("y"), lax.axis_index("c")
    return x, y, c, 4 * x + 2 * y + c


def _peer(k, x, y, c):
    px = 1 - x if k & 4 else x
    py = 1 - y if k & 2 else y
    pc = 1 - c if k & 1 else c
    return (px, py, pc), 4 * px + 2 * py + pc


def _exchange_copies(ins, lands, scatter, send_sems, recv_sems, receives=True):
    x, y, c, me = _mesh_position()
    sends, recvs = [], []
    for i, (src, land) in enumerate(zip(ins, lands)):
        for k in range(1, N_DEV):
            peer, peer_id = _peer(k, x, y, c)
            sem = i * (N_DEV - 1) + k - 1
            for dst, out in ((me, sends), (peer_id, recvs)) if receives else ((me, sends),):
                out.append(pltpu.make_async_remote_copy(
                    src_ref=src.at[peer_id] if scatter[i] else src, dst_ref=land.at[dst],
                    send_sem=send_sems.at[sem], recv_sem=recv_sems.at[sem],
                    device_id=peer, device_id_type=pl.DeviceIdType.MESH))
    return sends, recvs


def _landing_shape(a, scatter):
    return a.shape if scatter else (N_DEV,) + a.shape


def _direct_exchange(arrays, scatter, name):
    n = len(arrays)
    out_shapes = [jax.ShapeDtypeStruct(_landing_shape(a, sc), a.dtype) for a, sc in zip(arrays, scatter)]

    def body(*refs):
        ins, outs = refs[:n], refs[n:2 * n]
        send_sems, recv_sems, local_sems = refs[2 * n:]
        me = _mesh_position()[3]
        local = [pltpu.make_async_copy(ins[i].at[me] if scatter[i] else ins[i], outs[i].at[me], local_sems.at[i])
                 for i in range(n)]
        sends, recvs = _exchange_copies(ins, outs, scatter, send_sems, recv_sems)
        for cp in local + sends:
            cp.start()
        for cp in recvs:
            cp.wait_recv()
        for cp in sends:
            cp.wait_send()
        for cp in local:
            cp.wait()

    hbm = pl.BlockSpec(memory_space=pl.ANY)
    return pl.pallas_call(
        body, name=name, in_specs=[hbm] * n, out_specs=[hbm] * n, out_shape=out_shapes,
        scratch_shapes=[pltpu.SemaphoreType.DMA((n * (N_DEV - 1),)), pltpu.SemaphoreType.DMA((n * (N_DEV - 1),)),
                        pltpu.SemaphoreType.DMA((n,))],
    )(*arrays)


def _two_level_gather(arrays, name):
    n = len(arrays)
    per = N_DEV - 1

    def body(*refs):
        ins, outs = refs[:n], refs[n:2 * n]
        send_sems, recv_sems, local_sems = refs[2 * n:]
        x, y, c, me = _mesh_position()
        sibling = (x, y, 1 - c)
        chips = [(1 - x, y), (x, 1 - y), (1 - x, 1 - y)]

        def copy(i, k, block, to, src=None):
            slot = outs[i].at[4 * block[0] + 2 * block[1] + block[2]]
            return pltpu.make_async_remote_copy(
                src_ref=slot if src is None else src, dst_ref=slot,
                send_sem=send_sems.at[i * per + k], recv_sem=recv_sems.at[i * per + k],
                device_id=to, device_id_type=pl.DeviceIdType.MESH)

        local = [pltpu.make_async_copy(ins[i], outs[i].at[me], local_sems.at[i]) for i in range(n)]
        first = []
        for i in range(n):
            first.append(copy(i, 0, (x, y, c), sibling, src=ins[i]))
            first += [copy(i, 1 + j, (x, y, c), (*chip, c), src=ins[i]) for j, chip in enumerate(chips)]
        for cp in local + first:
            cp.start()
        passed = []
        for i in range(n):
            for j, chip in enumerate(chips):
                copy(i, 1 + j, (*chip, c), (x, y, c)).wait_recv()
                passed.append(copy(i, 4 + j, (*chip, c), sibling))
                passed[-1].start()
        for i in range(n):
            copy(i, 0, sibling, (x, y, c)).wait_recv()
            for j, chip in enumerate(chips):
                copy(i, 4 + j, (*chip, 1 - c), (x, y, c)).wait_recv()
        for cp in first + passed:
            cp.wait_send()
        for cp in local:
            cp.wait()

    hbm = pl.BlockSpec(memory_space=pl.ANY)
    return pl.pallas_call(
        body, name=name, in_specs=[hbm] * n, out_specs=[hbm] * n,
        out_shape=[jax.ShapeDtypeStruct((N_DEV,) + a.shape, a.dtype) for a in arrays],
        scratch_shapes=[pltpu.SemaphoreType.DMA((n * per,)), pltpu.SemaphoreType.DMA((n * per,)),
                        pltpu.SemaphoreType.DMA((n,))],
    )(*arrays)


_HBM = pl.BlockSpec(memory_space=pltpu.HBM)
_SEM = pl.BlockSpec(memory_space=pltpu.SEMAPHORE)
_DATAFLOW = pltpu.SideEffectType.DATAFLOW_SIDE_EFFECTING


def _exchange_start(arrays, scatter, name):
    n = len(arrays)
    srcs = [pltpu.with_memory_space_constraint(a, pltpu.HBM) for a in arrays]
    lands = [pltpu.with_memory_space_constraint(lax.empty(_landing_shape(a, sc), a.dtype), pltpu.HBM)
             for a, sc in zip(arrays, scatter)]
    nsem = n * (N_DEV - 1)

    def body(*refs):
        ins, zones = refs[:n], refs[n:2 * n]
        send_sems, recv_sems = refs[2 * n], refs[2 * n + 1]
        token = refs[-1]
        sends, _ = _exchange_copies(ins, zones, scatter, send_sems, recv_sems, receives=False)
        for cp in sends:
            cp.start()
        token[...] = jnp.zeros_like(token)

    res = pl.pallas_call(
        body, name=name,
        out_shape=(pltpu.SemaphoreType.DMA((nsem,)), pltpu.SemaphoreType.DMA((nsem,)),
                   *[pltpu.HBM(a.shape, a.dtype) for a in srcs + lands], jax.ShapeDtypeStruct((8, HEAD_DIM), F32)),
        in_specs=[_HBM] * (2 * n),
        out_specs=(_SEM, _SEM, *[_HBM] * (2 * n), pl.BlockSpec(memory_space=pltpu.VMEM)),
        input_output_aliases={i: 2 + i for i in range(2 * n)},
        compiler_params=pltpu.CompilerParams(has_side_effects=_DATAFLOW),
    )(*srcs, *lands)
    return dict(sems=res[:2], srcs=res[2:2 + n], lands=res[2 + n:2 + 2 * n], token_block=res[-1],
                token=res[-1][0, 0], scatter=scatter)


def _exchange_wait(started, after, name):
    scatter = started["scatter"]
    n = len(scatter)

    def body(*refs):
        ins, zones = refs[:n], refs[n:2 * n]
        send_sems, recv_sems = refs[2 * n], refs[2 * n + 1]
        sends, recvs = _exchange_copies(ins, zones, scatter, send_sems, recv_sems)
        for cp in sends:
            cp.wait_send()
        for cp in recvs:
            cp.wait_recv()

    thru = list(started["srcs"]) + list(started["lands"])
    res = pl.pallas_call(
        body, name=name, out_shape=[pltpu.HBM(a.shape, a.dtype) for a in thru],
        in_specs=[_HBM] * (2 * n) + [_SEM, _SEM, pl.BlockSpec(memory_space=pl.ANY)], out_specs=[_HBM] * (2 * n),
        input_output_aliases={i: i for i in range(2 * n)},
        compiler_params=pltpu.CompilerParams(has_side_effects=_DATAFLOW),
    )(*thru, *started["sems"], after)
    me = 4 * lax.axis_index("x") + 2 * lax.axis_index("y") + lax.axis_index("c")
    out = []
    for src, got, sc in zip(res[:n], res[n:], scatter):
        own = lax.dynamic_index_in_dim(src, me, 0, keepdims=False) if sc else src
        out.append(lax.dynamic_update_index_in_dim(got, own, me, 0))
    return out


def _adamw(parts, w, m, v, name, rows_per_step, row_offset=0, into=None):
    rows, cols = parts.shape[1:]
    tr = min(rows_per_step, rows)
    assert rows % tr == 0 and row_offset % tr == 0, (name, rows, tr, row_offset)
    first = row_offset // tr
    c1 = 1.0 / (1.0 - ADAM_B1 ** ADAM_STEP)
    c2 = 1.0 / (1.0 - ADAM_B2 ** ADAM_STEP)

    def body(p_ref, w_ref, m_ref, v_ref, *rest):
        g_ref, d_ref, nm_ref, nv_ref = rest[-4:]
        g = p_ref[0].astype(F32)
        for d in range(1, N_DEV):
            g = g + p_ref[d].astype(F32)
        nm = ADAM_B1 * m_ref[...] + (1.0 - ADAM_B1) * g
        nv = ADAM_B2 * v_ref[...] + (1.0 - ADAM_B2) * (g * g)
        g_ref[...] = g
        nm_ref[...] = nm
        nv_ref[...] = nv
        d_ref[...] = -ADAM_LR * ((nm * c1) / (jnp.sqrt(nv * c2) + ADAM_EPS) + ADAM_WD * w_ref[...])

    blk = pl.BlockSpec((tr, cols), lambda i: (i + first, 0))
    shape = jax.ShapeDtypeStruct(w.shape, F32)
    prior = [] if into is None else list(into)
    return pl.pallas_call(
        body, name=name, grid=(rows // tr,),
        in_specs=[pl.BlockSpec((N_DEV, tr, cols), lambda i: (0, i, 0)), blk, blk, blk]
        + [pl.BlockSpec(memory_space=pl.ANY)] * len(prior),
        out_specs=[blk] * 4, out_shape=[shape] * 4,
        input_output_aliases={4 + j: j for j in range(len(prior))}, compiler_params=_params(),
    )(parts, w, m, v, *prior)


_LAYERED = ("norm_w", "lru_conv_b", "lru_wa", "lru_ba", "lru_wx", "lru_bx", "lru_lambda", "lru_norm_w",
            "dn_A_log", "dn_dt_bias", "dn_norm_w")
_WEIGHTS = ("norm_w", "w_in", "lru_conv_w", "lru_conv_b", "lru_wa", "lru_ba", "lru_wx", "lru_bx", "lru_lambda",
            "lru_norm_w", "dn_conv_w", "dn_A_log", "dn_dt_bias", "dn_norm_w", "w_out", "final_norm_w")


def _pack_layer(tree, layer, tail=(), names=_LAYERED):
    rows = []
    for name in names:
        a = tree[name][layer]
        if a.shape[-1] == HEADS:
            a = jnp.pad(a, (0, HEAD_DIM - HEADS))
        rows.append(a.reshape(-1, HEAD_DIM))
    rows += [t.reshape(-1, HEAD_DIM) for t in tail]
    packed = jnp.concatenate(rows, axis=0)
    return jnp.pad(packed, ((0, (-packed.shape[0]) % 8), (0, 0)))


def _unpack_layer(packed, like, names=_LAYERED):
    out, at = {}, 0
    for name in names:
        shape = like[name].shape[1:]
        if shape[-1] == HEADS:
            n = 1
            out[name] = packed[at, :HEADS]
        else:
            n = like[name][0].size // HEAD_DIM
            out[name] = packed[at:at + n].reshape(shape)
        at += n
    return out, at


def _heads_to_channels(a):
    return jnp.transpose(a, (1, 0, 2)).reshape(a.shape[1], HEADS * HEAD_DIM)


def kernel(x, norm_w, w_in, lru_conv_w, lru_conv_b, lru_wa, lru_ba, lru_wx, lru_bx, lru_lambda, lru_norm_w, dn_conv_w, dn_A_log, dn_dt_bias, dn_norm_w, w_out, final_norm_w, loss_target, m_norm_w, m_w_in, m_lru_conv_w, m_lru_conv_b, m_lru_wa, m_lru_ba, m_lru_wx, m_lru_bx, m_lru_lambda, m_lru_norm_w, m_dn_conv_w, m_dn_A_log, m_dn_dt_bias, m_dn_norm_w, m_w_out, m_final_norm_w, v_norm_w, v_w_in, v_lru_conv_w, v_lru_conv_b, v_lru_wa, v_lru_ba, v_lru_wx, v_lru_bx, v_lru_lambda, v_lru_norm_w, v_dn_conv_w, v_dn_A_log, v_dn_dt_bias, v_dn_norm_w, v_w_out, v_final_norm_w):
    weights = dict(norm_w=norm_w, w_in=w_in, lru_conv_w=lru_conv_w, lru_conv_b=lru_conv_b, lru_wa=lru_wa,
                   lru_ba=lru_ba, lru_wx=lru_wx, lru_bx=lru_bx, lru_lambda=lru_lambda, lru_norm_w=lru_norm_w,
                   dn_conv_w=dn_conv_w, dn_A_log=dn_A_log, dn_dt_bias=dn_dt_bias, dn_norm_w=dn_norm_w,
                   w_out=w_out, final_norm_w=final_norm_w)
    mom_m = dict(norm_w=m_norm_w, w_in=m_w_in, lru_conv_w=m_lru_conv_w, lru_conv_b=m_lru_conv_b, lru_wa=m_lru_wa,
                 lru_ba=m_lru_ba, lru_wx=m_lru_wx, lru_bx=m_lru_bx, lru_lambda=m_lru_lambda,
                 lru_norm_w=m_lru_norm_w, dn_conv_w=m_dn_conv_w, dn_A_log=m_dn_A_log, dn_dt_bias=m_dn_dt_bias,
                 dn_norm_w=m_dn_norm_w, w_out=m_w_out, final_norm_w=m_final_norm_w)
    mom_v = dict(norm_w=v_norm_w, w_in=v_w_in, lru_conv_w=v_lru_conv_w, lru_conv_b=v_lru_conv_b, lru_wa=v_lru_wa,
                 lru_ba=v_lru_ba, lru_wx=v_lru_wx, lru_bx=v_lru_bx, lru_lambda=v_lru_lambda,
                 lru_norm_w=v_lru_norm_w, dn_conv_w=v_dn_conv_w, dn_A_log=v_dn_A_log, dn_dt_bias=v_dn_dt_bias,
                 dn_norm_w=v_dn_norm_w, w_out=v_w_out, final_norm_w=v_final_norm_w)
    depth = norm_w.shape[0]
    xs = x[0]
    s = xs.shape[0]
    tm = min(1024, s)

    assert depth >= 2, depth

    def row(a):
        return a.reshape(1, -1)

    def pad_row(a):
        return jnp.pad(a, (0, HEAD_DIM - a.shape[0])).reshape(1, HEAD_DIM)

    def full_w_in(g):
        w = jnp.transpose(g, (1, 2, 0, 3)).reshape(g.shape[1], D_MODEL, D_IN)
        return jnp.pad(w, ((0, 0), (0, 0), (0, D_IN_PAD - D_IN)))

    g_win0, g_lcw, g_dcw = _two_level_gather([w_in[:1].astype(BF16), lru_conv_w, dn_conv_w], "gather_first")
    rest = _exchange_start([w_in[1:].astype(BF16), w_out.astype(BF16)], [False] * 2, "gather_rest_start")
    win = [full_w_in(g_win0)[0]]
    wout = None
    lcw = jnp.transpose(g_lcw, (1, 2, 0, 3)).reshape(depth, 4, D_MODEL)
    dcw = jnp.transpose(g_dcw, (1, 2, 0, 3)).reshape(depth, 4, 3 * D_MODEL)

    saved = []
    cur = xs
    for l in range(depth):
        nw_row = row(norm_w[l]) + rest["token"] if l == 0 else row(norm_w[l])
        hn = _rmsnorm_fwd(cur, nw_row, f"norm_fwd_{l}")
        proj = _matmul(hn, win[l], "nn", tm, 896, D_MODEL, f"in_proj_{l}")
        y_lru, hs = _lru_fwd(proj, lcw[l], row(lru_conv_b[l]), lru_wa[l], row(lru_ba[l]), lru_wx[l], row(lru_bx[l]),
                             row(lru_lambda[l]), row(lru_norm_w[l]), f"lru_fwd_{l}")
        ycat, o_dn, states = _dn_fwd(proj, y_lru, dcw[l], pad_row(dn_A_log[l]), pad_row(dn_dt_bias[l]),
                                     row(dn_norm_w[l]), f"dn_fwd_{l}")
        if l == 0:
            g_win_rest, g_wout = _exchange_wait(rest, ycat, "gather_rest_wait")
            win += list(full_w_in(g_win_rest))
            wout = jnp.transpose(g_wout, (1, 0, 2, 3)).reshape(depth, 2 * D_MODEL, D_MODEL)
        nxt = _matmul(ycat, wout[l], "nn", tm, D_MODEL, 2 * D_MODEL, f"out_proj_{l}", add=cur)
        saved.append((cur, hn, proj, hs, o_dn, states, ycat))
        cur = nxt
    loss_part, dx, d_final = _final_loss(cur, row(final_norm_w), loss_target[0], "final_loss")

    def win_slots(g):
        return jnp.transpose(g.reshape(D_MODEL, N_DEV, D_IN // N_DEV), (1, 0, 2))

    def wout_slots(g):
        return g.reshape(N_DEV, 2 * D_MODEL // N_DEV, D_MODEL)

    grads = {k: [None] * depth for k in _WEIGHTS if k not in ("final_norm_w", "w_in", "w_out")}
    started = {}
    token = None
    for l in reversed(range(depth)):
        x_in, hn, proj, hs, o_dn, states, ycat = saved[l]
        dy = _matmul(dx, wout[l], "nt", tm, D_MODEL, D_MODEL, f"out_proj_dy_{l}")
        g_wout_l = _matmul(ycat, dx, "tn", D_MODEL, D_MODEL, tm, f"out_proj_dw_{l}", out_dtype=BF16)
        if l == 0:
            started["w_out_0"] = _exchange_start([wout_slots(g_wout_l)], [True], "exchange_w_out_0_start")
            token = token + started["w_out_0"]["token"]
        cb_row = row(lru_conv_b[l]) if token is None else row(lru_conv_b[l]) + token
        (dlx, dlz, g_lcw, g_lcb, g_wa, g_ba, g_wx, g_bx, g_lam, g_lnw) = _lru_bwd(
            proj, hs, dy, lcw[l], cb_row, lru_wa[l], row(lru_ba[l]), lru_wx[l], row(lru_bx[l]),
            row(lru_lambda[l]), row(lru_norm_w[l]), f"lru_bwd_{l}")
        (dq, dk, dv, ddz, dba, g_dcw3, g_al, g_dt, g_dnw) = _dn_bwd(
            proj, o_dn, states, dy, dcw[l], pad_row(dn_A_log[l]), pad_row(dn_dt_bias[l]), row(dn_norm_w[l]),
            f"dn_bwd_{l}")
        grads["lru_conv_w"][l] = _heads_to_channels(g_lcw)
        grads["lru_conv_b"][l] = g_lcb.reshape(D_MODEL)
        grads["lru_wa"][l] = g_wa
        grads["lru_ba"][l] = g_ba.reshape(D_MODEL)
        grads["lru_wx"][l] = g_wx
        grads["lru_bx"][l] = g_bx.reshape(D_MODEL)
        grads["lru_lambda"][l] = g_lam.reshape(D_MODEL)
        grads["lru_norm_w"][l] = g_lnw.reshape(D_MODEL)
        g_dcw3 = g_dcw3.reshape(HEADS, 3, 4, HEAD_DIM)
        grads["dn_conv_w"][l] = jnp.concatenate([_heads_to_channels(g_dcw3[:, i]) for i in range(3)], axis=1)
        grads["dn_A_log"][l] = g_al[0, :HEADS]
        grads["dn_dt_bias"][l] = g_dt[0, :HEADS]
        grads["dn_norm_w"][l] = g_dnw.reshape(HEAD_DIM)
        dep = None
        if l == 0:
            started["pack_0"] = _exchange_start([_pack_layer(grads, 0, names=_LAYERED[1:])], [False],
                                                "exchange_pack_0_start")
            dep = started["pack_0"]["token_block"]
        dproj = jnp.concatenate([dlx, dlz, dq, dk, dv, ddz, dba.astype(BF16)], axis=1)
        g_win_l = _matmul(hn, dproj, "tn", D_MODEL, 896, tm, f"in_proj_dw_{l}", out_dtype=BF16, dep=dep)[:, :D_IN]
        if l == 0:
            started[0] = _exchange_start([win_slots(g_win_l)], [True], "exchange_0_start")
            dep = started[0]["token_block"]
        dh = _matmul(dproj, win[l], "nt", tm, D_MODEL, 896, f"in_proj_dh_{l}", dep=dep)
        dx, g_nw = _rmsnorm_bwd(x_in, row(norm_w[l]), dh, dx, f"norm_bwd_{l}")
        grads["norm_w"][l] = g_nw.reshape(D_MODEL)
        if l > 0:
            tail = (d_final, loss_part) if l == depth - 1 else ()
            started[l] = _exchange_start([win_slots(g_win_l), wout_slots(g_wout_l), _pack_layer(grads, l, tail)],
                                         [True, True, False], f"exchange_{l}_start")
            token = started[l]["token"]

    def conv_slots(a):
        dd, r, cc = a.shape
        return jnp.transpose(a.reshape(dd, r, N_DEV, cc // N_DEV), (2, 0, 1, 3))

    small = _exchange_start(
        [conv_slots(jnp.stack(grads["lru_conv_w"])), conv_slots(jnp.stack(grads["dn_conv_w"])),
         _pack_layer(grads, 0, names=_LAYERED[:1])], [True, True, False], "exchange_small_start")

    new = {}
    flat_in = (depth * D_MODEL, D_IN // N_DEV)
    flat_out = (depth * 2 * D_MODEL // N_DEV, D_MODEL)
    zero_row = jnp.zeros((1, HEAD_DIM), F32)

    def adamw_pack(parts, layer, names=_LAYERED, name="adamw_small"):
        tails = [(t, zero_row) if layer == depth - 1 else () for t in (final_norm_w, m_final_norm_w, v_final_norm_w)]
        return _adamw(parts, _pack_layer(weights, layer, tails[0], names), _pack_layer(mom_m, layer, tails[1], names),
                      _pack_layer(mom_v, layer, tails[2], names), f"{name}_{layer}", parts.shape[1])

    def adamw_w_in(parts, layer, into):
        return _adamw(parts, w_in.reshape(flat_in), m_w_in.reshape(flat_in), v_w_in.reshape(flat_in),
                      f"adamw_w_in_{layer}", 256, layer * D_MODEL, into)

    def adamw_w_out(parts, layer, into):
        return _adamw(parts, w_out.reshape(flat_out), m_w_out.reshape(flat_out), v_w_out.reshape(flat_out),
                      f"adamw_w_out_{layer}", 256, layer * flat_out[0] // depth, into)

    acc_in = acc_out = None
    packs = [None] * depth
    after = small["token_block"]
    for l in reversed(range(1, depth)):
        r_win, r_wout, r_pack = _exchange_wait(started[l], after, f"exchange_{l}_wait")
        acc_in = adamw_w_in(r_win, l, acc_in)
        acc_out = adamw_w_out(r_wout, l, acc_out)
        packs[l] = adamw_pack(r_pack, l)
        after = packs[l][0]
    (r_wout,) = _exchange_wait(started["w_out_0"], after, "exchange_w_out_0_wait")
    acc_out = adamw_w_out(r_wout, 0, acc_out)
    (r_win,) = _exchange_wait(started[0], acc_out[0], "exchange_0_wait")
    acc_in = adamw_w_in(r_win, 0, acc_in)
    (r_pack,) = _exchange_wait(started["pack_0"], acc_in[0], "exchange_pack_0_wait")
    packs[0] = adamw_pack(r_pack, 0, _LAYERED[1:])
    r_lcw, r_dcw, r_norm = _exchange_wait(small, packs[0][0], "exchange_small_wait")
    for name, parts in (("lru_conv_w", r_lcw), ("dn_conv_w", r_dcw)):
        w = weights[name]
        flat = (-1, w.shape[-1])
        outs = _adamw(parts.reshape((N_DEV,) + (w.size // w.shape[-1], w.shape[-1])), w.reshape(flat),
                      mom_m[name].reshape(flat), mom_v[name].reshape(flat), f"adamw_{name}", 8)
        new[name] = [a.reshape(w.shape) for a in outs]
    norm_0 = adamw_pack(r_norm, 0, _LAYERED[:1], "adamw_norm")
    new["w_in"] = [a.reshape(w_in.shape) for a in acc_in]
    new["w_out"] = [a.reshape(w_out.shape) for a in acc_out]
    for i in range(4):
        layers = [{**_unpack_layer(packs[0][i], weights, _LAYERED[1:])[0],
                   **_unpack_layer(norm_0[i], weights, _LAYERED[:1])[0]}]
        layers += [_unpack_layer(packs[l][i], weights)[0] for l in range(1, depth)]
        for name in _LAYERED:
            new.setdefault(name, []).append(jnp.stack([layer[name] for layer in layers]))
    tail_at = _unpack_layer(packs[depth - 1][0], weights)[1]
    rows_final = D_MODEL // HEAD_DIM
    new["final_norm_w"] = [packs[depth - 1][i][tail_at:tail_at + rows_final].reshape(D_MODEL) for i in range(4)]
    loss = packs[depth - 1][0][tail_at + rows_final, 0]
    out = [loss, dx.reshape(x.shape)]
    for i in range(4):
        out += [new[name][i] for name in _WEIGHTS]
    return tuple(out)
```

```python
import functools

import jax
import jax.numpy as jnp
from jax import lax
from jax.experimental import pallas as pl
from jax.experimental.pallas import tpu as pltpu

F32 = jnp.float32
BF16 = jnp.bfloat16

N_DEV = 8
D_MODEL = 1024
HEADS = 8
HEAD_DIM = 128
CHUNK = 64
D_IN = 6160
D_IN_PAD = 6272
COL_LRU_X, COL_LRU_Z, COL_Q, COL_K, COL_V, COL_DN_Z, COL_BA = 0, 8, 16, 24, 32, 40, 48
LRU_C = 8.0
EPS = 1e-6
ADAM_LR, ADAM_B1, ADAM_B2, ADAM_EPS, ADAM_WD, ADAM_STEP = 0.001, 0.9, 0.999, 1e-08, 0.01, 10
TIME_BLOCK = 1024
VMEM_LIMIT = 56 * 1024 * 1024

NN = (((1,), (0,)), ((), ()))
NT = (((1,), (1,)), ((), ()))
TN = (((0,), (0,)), ((), ()))


B_NN = (((2,), (1,)), ((0,), (0,)))
B_NT = (((2,), (2,)), ((0,), (0,)))
B_TN = (((1,), (1,)), ((0,), (0,)))


def _split_bf16(x):
    hi = x.astype(BF16)
    return hi, (x - hi.astype(F32)).astype(BF16)


def _dot(a, b, dims, prec):
    if prec == "bf16":
        return lax.dot_general(a.astype(BF16), b.astype(BF16), dims, preferred_element_type=F32)
    a1, a2 = _split_bf16(a)
    b1, b2 = _split_bf16(b)
    dg = functools.partial(lax.dot_general, dimension_numbers=dims, preferred_element_type=F32)
    return dg(a1, b1) + (dg(a1, b2) + dg(a2, b1))


def _make_mm(prec, nn_dims, nt_dims, tn_dims):
    @jax.custom_vjp
    def nn(a, b):
        return _dot(a, b, nn_dims, prec)

    @jax.custom_vjp
    def nt(a, b):
        return _dot(a, b, nt_dims, prec)

    @jax.custom_vjp
    def tn(a, b):
        return _dot(a, b, tn_dims, prec)

    nn.defvjp(lambda a, b: (_dot(a, b, nn_dims, prec), (a, b)),
              lambda r, g: (_dot(g, r[1], nt_dims, prec), _dot(r[0], g, tn_dims, prec)))
    nt.defvjp(lambda a, b: (_dot(a, b, nt_dims, prec), (a, b)),
              lambda r, g: (_dot(g, r[1], nn_dims, prec), _dot(g, r[0], tn_dims, prec)))
    tn.defvjp(lambda a, b: (_dot(a, b, tn_dims, prec), (a, b)),
              lambda r, g: (_dot(r[1], g, nt_dims, prec), _dot(r[0], g, nn_dims, prec)))
    return nn, nt, tn


_NN_B, _NT_B, _TN_B = _make_mm("bf16", NN, NT, TN)
_BNN, _BNT, _BTN = _make_mm("bf16", B_NN, B_NT, B_TN)


@jax.custom_vjp
def _unit_lower_inverse(a):
    n = a.shape[-1]
    eye = (lax.broadcasted_iota(jnp.int32, a.shape, 1) == lax.broadcasted_iota(jnp.int32, a.shape, 2)).astype(F32)
    dg = functools.partial(lax.dot_general, dimension_numbers=B_NN, preferred_element_type=F32)
    inv = eye - a
    pw = _dot(a, a, B_NN, "bf16x3")
    steps = n.bit_length() - 2
    for j in range(steps):
        i1, i2 = _split_bf16(inv)
        p1, p2 = _split_bf16(pw)
        square = j + 1 < steps
        by_hi = dg(jnp.concatenate([i1, i2, p1, p2] if square else [i1, i2], axis=1), p1)
        by_lo = dg(jnp.concatenate([i1, p1], axis=1) if square else i1, p2)
        inv = inv + (by_hi[:, :n] + (by_lo[:, :n] + by_hi[:, n:2 * n]))
        if square:
            pw = by_hi[:, 2 * n:3 * n] + (by_lo[:, n:] + by_hi[:, 3 * n:])
    return inv


def _uli_fwd(a):
    inv = _unit_lower_inverse(a)
    return inv, inv


def _uli_bwd(inv, g):
    return (-_dot(_dot(inv, g, B_TN, "bf16"), inv, B_NT, "bf16"),)


_unit_lower_inverse.defvjp(_uli_fwd, _uli_bwd)


def _rows2(y, m):
    return y[:, :m], y[:, m:]


@jax.custom_vjp
def _pair_nn(x1, x2, r):
    return _rows2(_dot(jnp.concatenate([x1, x2], axis=1), r, B_NN, "bf16"), x1.shape[1])


def _pair_nn_bwd(res, g):
    x1, x2, r = res
    g = jnp.concatenate(g, axis=1)
    dx1, dx2 = _rows2(_dot(g, r, B_NT, "bf16"), x1.shape[1])
    return dx1, dx2, _dot(jnp.concatenate([x1, x2], axis=1), g, B_TN, "bf16")


_pair_nn.defvjp(lambda x1, x2, r: (_pair_nn(x1, x2, r), (x1, x2, r)), _pair_nn_bwd)


@jax.custom_vjp
def _pair_nt(x1, x2, r):
    return _rows2(_dot(jnp.concatenate([x1, x2], axis=1), r, B_NT, "bf16"), x1.shape[1])


def _pair_nt_bwd(res, g):
    x1, x2, r = res
    g = jnp.concatenate(g, axis=1)
    dx1, dx2 = _rows2(_dot(g, r, B_NN, "bf16"), x1.shape[1])
    return dx1, dx2, _dot(g, jnp.concatenate([x1, x2], axis=1), B_TN, "bf16")


_pair_nt.defvjp(lambda x1, x2, r: (_pair_nt(x1, x2, r), (x1, x2, r)), _pair_nt_bwd)


@jax.custom_vjp
def _wide_nn(l, r1, r2):
    y = _dot(l, jnp.concatenate([r1, r2], axis=2), B_NN, "bf16")
    return y[:, :, :r1.shape[2]], y[:, :, r1.shape[2]:]


def _wide_nn_bwd(res, g):
    l, r1, r2 = res
    g = jnp.concatenate(g, axis=2)
    dr = _dot(l, g, B_TN, "bf16")
    return (_dot(g, jnp.concatenate([r1, r2], axis=2), B_NT, "bf16"), dr[:, :, :r1.shape[2]], dr[:, :, r1.shape[2]:])


_wide_nn.defvjp(lambda l, r1, r2: (_wide_nn(l, r1, r2), (l, r1, r2)), _wide_nn_bwd)


def _lower_ones(batch, n):
    shape = (batch, n, n)
    return (lax.broadcasted_iota(jnp.int32, shape, 1) >= lax.broadcasted_iota(jnp.int32, shape, 2)).astype(BF16)


@jax.custom_vjp
def _chunk_cumsum(g):
    tri = _lower_ones(g.shape[0], g.shape[1])
    g1, g2 = _split_bf16(g)
    g3 = (g - g1.astype(F32) - g2.astype(F32)).astype(BF16)
    dg = functools.partial(lax.dot_general, dimension_numbers=B_NN, preferred_element_type=F32)
    return dg(tri, g1) + (dg(tri, g2) + dg(tri, g3))


def _chunk_cumsum_bwd(_, ct):
    tri = _lower_ones(ct.shape[0], ct.shape[1])
    c1, c2 = _split_bf16(ct)
    dg = functools.partial(lax.dot_general, dimension_numbers=B_TN, preferred_element_type=F32)
    return (dg(tri, c1) + dg(tri, c2),)


_chunk_cumsum.defvjp(lambda g: (_chunk_cumsum(g), None), _chunk_cumsum_bwd)


def _expm1(x):
    small = x * (1.0 + x * (0.5 + x * (1.0 / 6 + x * (1.0 / 24 + x * (1.0 / 120 + x * (1.0 / 720))))))
    return jnp.where(jnp.abs(x) < 0.2, small, jnp.exp(x) - 1.0)


def _sigmoid(x):
    return 1.0 / (1.0 + jnp.exp(-x))


def _silu(x):
    return x * _sigmoid(x)


def _softplus(x):
    return jnp.maximum(x, 0.0) + jnp.log(1.0 + jnp.exp(-jnp.abs(x)))


def _rmsnorm(x, w):
    return x * lax.rsqrt(jnp.mean(x * x, axis=-1, keepdims=True) + EPS) * w


def _gated_norm(o, z, w):
    return o * lax.rsqrt(jnp.mean(o * o, axis=-1, keepdims=True) + EPS) * w * _silu(z)


def _lru_gates(xc, wa, ba, wx, bx, lam):
    r = _sigmoid(_NN_B(xc, wa) + ba)
    i = _sigmoid(_NN_B(xc, wx) + bx)
    log_a = -LRU_C * r * _softplus(-lam)
    a = jnp.exp(log_a)
    mult = jnp.sqrt(-_expm1(2.0 * log_a))
    return a, mult * (i * xc)


def _scan_forward(a, b, h0):
    rows = a.shape[0]
    row = lax.broadcasted_iota(jnp.int32, a.shape, 0)
    k = 1
    while k < rows:
        seen = row >= k
        b = jnp.where(seen, a * pltpu.roll(b, k, 0) + b, b)
        a = jnp.where(seen, a * pltpu.roll(a, k, 0), a)
        k *= 2
    return b + a * h0


def _scan_reverse(a, d, carry):
    rows = a.shape[0]
    row = lax.broadcasted_iota(jnp.int32, a.shape, 0)
    last = row == rows - 1
    c = jnp.where(last, 0.0, pltpu.roll(a, rows - 1, 0))
    d = d + jnp.where(last, carry, 0.0)
    k = 1
    while k < rows:
        seen = row < rows - k
        d = jnp.where(seen, d + c * pltpu.roll(d, rows - k, 0), d)
        c = jnp.where(seen, c * pltpu.roll(c, rows - k, 0), c)
        k *= 2
    return d


def _lane_pick(row, lane_index):
    lane = lax.broadcasted_iota(jnp.int32, row.shape, 1)
    return jnp.sum(jnp.where(lane == lane_index, row, 0.0), axis=-1, keepdims=True)


def _dn_prep(qc, kc, vc, ba, a_log_row, dt_row, head):
    q = _silu(qc)
    k = _silu(kc)
    v = _silu(vc)
    q = q * lax.rsqrt(jnp.sum(q * q, axis=-1, keepdims=True) + EPS) * (HEAD_DIM ** -0.5)
    k = k * lax.rsqrt(jnp.sum(k * k, axis=-1, keepdims=True) + EPS)
    beta = _sigmoid(_lane_pick(ba, head))
    g = -jnp.exp(_lane_pick(a_log_row, head)) * _softplus(_lane_pick(ba, HEADS + head) + _lane_pick(dt_row, head))
    return q, k, v, g, beta


def _dn_chunks_head(q, k, v, gcol, bcol):
    n, c, d = q.shape
    row = lax.broadcasted_iota(jnp.int32, (n, c, c), 1)
    col = lax.broadcasted_iota(jnp.int32, (n, c, c), 2)
    g_wide = jnp.broadcast_to(gcol, (n, c, d))
    b_wide = jnp.broadcast_to(bcol, (n, c, d))
    gc = _chunk_cumsum(g_wide)
    gc_rows = gc[:, :, :c]
    decay = jnp.exp(jnp.where(row >= col, gc_rows - jnp.swapaxes(gc_rows, 1, 2), -1e30))
    kb = k * b_wide
    eg = jnp.exp(gc)
    kbk, qk = _pair_nt(kb, q, k)
    tinv = _unit_lower_inverse(jnp.where(row > col, kbk * decay, 0.0))
    u, w = _wide_nn(tinv, v * b_wide, kb * eg)
    g_last = jnp.sum(g_wide, axis=1, keepdims=True)
    return u, w, qk * decay, q * eg, k * jnp.exp(g_last - gc), jnp.exp(g_last)


def _dn_chunks(q, k, v, gcol, bcol, states):
    u, w, attn, qe, kdec, eglast = _dn_chunks_head(q, k, v, gcol, bcol)
    w_st, qe_st = _pair_nn(w, qe, states)
    v_new = u - w_st
    o = qe_st + _BNN(attn, v_new)
    return (o, states * eglast + _BTN(kdec, v_new)), (w, attn, qe, kdec, eglast)


def _conv_taps(buf, head, cw, rows):
    acc = cw[0:1, :] * buf[head, pl.ds(5, rows), :]
    for j in range(1, 4):
        acc = acc + cw[j:j + 1, :] * buf[head, pl.ds(5 + j, rows), :]
    return acc


def _conv_backward(dbuf, dhead, xbuf, xhead, cw, dxc, rows):
    dbuf[dhead, pl.ds(0, rows), :] = dxc
    dx = cw[0:1, :] * dbuf[dhead, pl.ds(3, rows), :]
    for j in range(1, 4):
        dx = dx + cw[j:j + 1, :] * dbuf[dhead, pl.ds(3 - j, rows), :]
    dcw = jnp.concatenate(
        [jnp.sum(dxc * xbuf[xhead, pl.ds(5 + j, rows), :], axis=0, keepdims=True) for j in range(4)], axis=0)
    dbuf[dhead, pl.ds(rows, 8), :] = dbuf[dhead, pl.ds(0, 8), :]
    return dx, dcw


def _params(**kw):
    return pltpu.CompilerParams(vmem_limit_bytes=VMEM_LIMIT, **kw)


def _matmul(a, b, form, tm, tn, tk, name, add=None, out_dtype=F32, dep=None):
    if form == "nn":
        (m, kdim), (_, n) = a.shape, b.shape
        a_spec = pl.BlockSpec((tm, tk), lambda j, i, k: (i, k))
        b_spec = pl.BlockSpec((tk, tn), lambda j, i, k: (k, j))
        dims = NN
    elif form == "nt":
        (m, kdim), (n, _) = a.shape, b.shape
        a_spec = pl.BlockSpec((tm, tk), lambda j, i, k: (i, k))
        b_spec = pl.BlockSpec((tn, tk), lambda j, i, k: (j, k))
        dims = NT
    else:
        (kdim, m), (_, n) = a.shape, b.shape
        a_spec = pl.BlockSpec((tk, tm), lambda j, i, k: (k, i))
        b_spec = pl.BlockSpec((tk, tn), lambda j, i, k: (k, j))
        dims = TN
    assert m % tm == 0 and n % tn == 0 and kdim % tk == 0, (name, m, n, kdim, tm, tn, tk)
    ksteps = kdim // tk
    o_spec = pl.BlockSpec((tm, tn), lambda j, i, k: (i, j))
    has_add = add is not None
    extra = [] if dep is None else [dep]

    def body(*refs):
        a_ref, b_ref = refs[:2]
        c_ref = refs[2] if has_add else None
        o_ref, acc = refs[-2:]
        k = pl.program_id(2)

        @pl.when(k == 0)
        def _():
            acc[...] = c_ref[...] if has_add else jnp.zeros_like(acc)

        acc[...] += lax.dot_general(a_ref[...].astype(BF16), b_ref[...].astype(BF16), dims,
                                    preferred_element_type=F32)

        @pl.when(k == ksteps - 1)
        def _():
            o_ref[...] = acc[...].astype(o_ref.dtype)

    in_specs = [a_spec, b_spec] + ([o_spec] if has_add else []) + [pl.BlockSpec((8, HEAD_DIM), lambda j, i, k: (0, 0))
                                                                   for _ in extra]
    args = (a, b) + ((add,) if has_add else ()) + tuple(extra)
    return pl.pallas_call(
        body, name=name, grid=(n // tn, m // tm, ksteps), in_specs=in_specs, out_specs=o_spec,
        out_shape=jax.ShapeDtypeStruct((m, n), out_dtype), scratch_shapes=[pltpu.VMEM((tm, tn), F32)],
        compiler_params=_params(dimension_semantics=("parallel", "parallel", "arbitrary")),
    )(*args)


def _rmsnorm_fwd(x, w_row, name):
    s = x.shape[0]
    tb = min(TIME_BLOCK, s)

    def body(x_ref, w_ref, o_ref):
        o_ref[...] = _rmsnorm(x_ref[...], w_ref[...]).astype(BF16)

    return pl.pallas_call(
        body, name=name, grid=(s // tb,),
        in_specs=[pl.BlockSpec((tb, D_MODEL), lambda i: (i, 0)), pl.BlockSpec((1, D_MODEL), lambda i: (0, 0))],
        out_specs=pl.BlockSpec((tb, D_MODEL), lambda i: (i, 0)),
        out_shape=jax.ShapeDtypeStruct((s, D_MODEL), BF16), compiler_params=_params(),
    )(x, w_row)


def _rmsnorm_bwd(x, w_row, dh, dres, name):
    s = x.shape[0]
    tb = min(TIME_BLOCK, s)

    def body(x_ref, w_ref, dh_ref, dres_ref, dx_ref, dw_ref):
        _, vjp = jax.vjp(_rmsnorm, x_ref[...], w_ref[...])
        dx, dw = vjp(dh_ref[...])
        dx_ref[...] = dres_ref[...] + dx

        @pl.when(pl.program_id(0) == 0)
        def _():
            dw_ref[...] = jnp.zeros_like(dw_ref)

        dw_ref[...] += dw

    row = pl.BlockSpec((tb, D_MODEL), lambda i: (i, 0))
    vec = pl.BlockSpec((1, D_MODEL), lambda i: (0, 0))
    return pl.pallas_call(
        body, name=name, grid=(s // tb,), in_specs=[row, vec, row, row], out_specs=[row, vec],
        out_shape=[jax.ShapeDtypeStruct((s, D_MODEL), F32), jax.ShapeDtypeStruct((1, D_MODEL), F32)],
        compiler_params=_params(),
    )(x, w_row, dh, dres)


def _final_loss(x, w_row, target, name):
    s = x.shape[0]
    tb = min(TIME_BLOCK, s)

    def loss_fn(xv, wv, tv):
        err = _rmsnorm(xv, wv) - tv
        return 0.5 * jnp.sum(jnp.sum(err * err, axis=-1, keepdims=True), axis=0, keepdims=True) * (1.0 / D_MODEL)

    def body(x_ref, w_ref, t_ref, loss_ref, dx_ref, dw_ref):
        tv = t_ref[...]
        loss, vjp = jax.vjp(lambda xv, wv: loss_fn(xv, wv, tv), x_ref[...], w_ref[...])
        dx, dw = vjp(jnp.ones((1, 1), F32))
        dx_ref[...] = dx

        @pl.when(pl.program_id(0) == 0)
        def _():
            dw_ref[...] = jnp.zeros_like(dw_ref)
            loss_ref[...] = jnp.zeros_like(loss_ref)

        dw_ref[...] += dw
        loss_ref[...] += jnp.broadcast_to(loss, loss_ref.shape)

    row = pl.BlockSpec((tb, D_MODEL), lambda i: (i, 0))
    vec = pl.BlockSpec((1, D_MODEL), lambda i: (0, 0))
    return pl.pallas_call(
        body, name=name, grid=(s // tb,), in_specs=[row, vec, row],
        out_specs=[pl.BlockSpec((1, HEAD_DIM), lambda i: (0, 0)), row, vec],
        out_shape=[jax.ShapeDtypeStruct((1, HEAD_DIM), F32), jax.ShapeDtypeStruct((s, D_MODEL), F32),
                   jax.ShapeDtypeStruct((1, D_MODEL), F32)],
        compiler_params=_params(),
    )(x, w_row, target)


def _head_specs(tb, time_of):
    def col(off):
        return pl.BlockSpec((tb, HEAD_DIM), lambda t, h: (time_of(t), off + h))
    return col


def _vec_spec():
    return pl.BlockSpec((1, HEAD_DIM), lambda t, h: (0, h))


def _lru_fwd(proj, conv_w, conv_b, wa, ba, wx, bx, lam, nw, name):
    s = proj.shape[0]
    tb = min(TIME_BLOCK, s)
    nt = s // tb
    col = _head_specs(tb, lambda t: t)

    def body(x_ref, z_ref, cw_ref, cb_ref, wa_ref, ba_ref, wx_ref, bx_ref, lam_ref, nw_ref,
             y_ref, hs_ref, xbuf, hcar):
        t, h = pl.program_id(0), pl.program_id(1)

        @pl.when(t == 0)
        def _():
            xbuf[h, pl.ds(0, 8), :] = jnp.zeros((8, HEAD_DIM), F32)
            hcar[h] = jnp.zeros((8, HEAD_DIM), F32)

        xbuf[h, pl.ds(8, tb), :] = x_ref[...]
        xc = _conv_taps(xbuf, h, cw_ref[...], tb) + cb_ref[...]
        a, b = _lru_gates(xc, wa_ref[...], ba_ref[...], wx_ref[...], bx_ref[...], lam_ref[...])
        hs_ref[...] = _scan_forward(a, b, hcar[h, pl.ds(0, 1), :])
        hcar[h, pl.ds(0, 1), :] = hs_ref[pl.ds(tb - 1, 1), :]
        xbuf[h, pl.ds(0, 8), :] = xbuf[h, pl.ds(tb, 8), :]
        y_ref[...] = _gated_norm(hs_ref[...], z_ref[...], nw_ref[...]).astype(BF16)

    vec = _vec_spec()
    return pl.pallas_call(
        body, name=name, grid=(nt, HEADS),
        in_specs=[col(COL_LRU_X), col(COL_LRU_Z), pl.BlockSpec((4, HEAD_DIM), lambda t, h: (0, h)), vec,
                  pl.BlockSpec((None, HEAD_DIM, HEAD_DIM), lambda t, h: (h, 0, 0)), vec,
                  pl.BlockSpec((None, HEAD_DIM, HEAD_DIM), lambda t, h: (h, 0, 0)), vec, vec, vec],
        out_specs=[col(0), col(0)],
        out_shape=[jax.ShapeDtypeStruct((s, 2 * D_MODEL), BF16), jax.ShapeDtypeStruct((s, D_MODEL), F32)],
        scratch_shapes=[pltpu.VMEM((HEADS, tb + 8, HEAD_DIM), F32), pltpu.VMEM((HEADS, 8, HEAD_DIM), F32)],
        compiler_params=_params(dimension_semantics=("arbitrary", "arbitrary")),
    )(proj, proj, conv_w, conv_b, wa, ba, wx, bx, lam, nw)


def _halo_spec(tb, nt, off):
    per = tb // 8
    return pl.BlockSpec((8, HEAD_DIM), lambda t, h: (jnp.maximum((nt - 1 - t) * per - 1, 0), off + h))


def _lru_bwd(proj, hs, dy, conv_w, conv_b, wa, ba, wx, bx, lam, nw, name):
    s = proj.shape[0]
    tb = min(TIME_BLOCK, s)
    nt = s // tb
    col = _head_specs(tb, lambda t: nt - 1 - t)

    def body(x_ref, xh_ref, z_ref, hs_ref, hh_ref, dy_ref, cw_ref, cb_ref, wa_ref, ba_ref, wx_ref, bx_ref,
             lam_ref, nw_ref, dx_ref, dz_ref, dcw_ref, dcb_ref, dwa_ref, dba_ref, dwx_ref, dbx_ref, dlam_ref,
             dnw_ref, xbuf, hbuf, dbuf, gcar):
        t, h = pl.program_id(0), pl.program_id(1)
        first_block = t == nt - 1

        @pl.when(t == 0)
        def _():
            dbuf[h, pl.ds(tb, 8), :] = jnp.zeros((8, HEAD_DIM), F32)
            gcar[h] = jnp.zeros((8, HEAD_DIM), F32)
            dcw_ref[h] = jnp.zeros((4, HEAD_DIM), F32)
            dwa_ref[h] = jnp.zeros((HEAD_DIM, HEAD_DIM), F32)
            dwx_ref[h] = jnp.zeros((HEAD_DIM, HEAD_DIM), F32)
            for ref in (dcb_ref, dba_ref, dbx_ref, dlam_ref, dnw_ref):
                ref[h] = jnp.zeros((1, HEAD_DIM), F32)

        keep = jnp.where(first_block, 0.0, 1.0)
        xbuf[0, pl.ds(0, 8), :] = xh_ref[...] * keep
        xbuf[0, pl.ds(8, tb), :] = x_ref[...]
        hbuf[pl.ds(0, 8), :] = hh_ref[...] * keep
        hbuf[pl.ds(8, tb), :] = hs_ref[...]
        cw = cw_ref[...]
        xc = _conv_taps(xbuf, 0, cw, tb) + cb_ref[...]
        (a, _), gates_vjp = jax.vjp(_lru_gates, xc, wa_ref[...], ba_ref[...], wx_ref[...], bx_ref[...], lam_ref[...])
        _, norm_vjp = jax.vjp(_gated_norm, hs_ref[...], z_ref[...], nw_ref[...])
        dh, dz, dnw = norm_vjp(dy_ref[...])
        dz_ref[...] = dz.astype(dz_ref.dtype)
        g = _scan_reverse(a, dh, gcar[h, pl.ds(0, 1), :])
        gcar[h, pl.ds(0, 1), :] = a[0:1, :] * g[0:1, :]
        dxc, dwa, dba, dwx, dbx, dlam = gates_vjp((g * hbuf[pl.ds(7, tb), :], g))
        dx, dcw = _conv_backward(dbuf, h, xbuf, 0, cw, dxc, tb)
        dx_ref[...] = dx.astype(dx_ref.dtype)
        dcw_ref[h] += dcw
        dcb_ref[h] += jnp.sum(dxc, axis=0, keepdims=True)
        dwa_ref[h] += dwa
        dwx_ref[h] += dwx
        dba_ref[h] += dba
        dbx_ref[h] += dbx
        dlam_ref[h] += dlam
        dnw_ref[h] += dnw

    vec = _vec_spec()
    mat = pl.BlockSpec((None, HEAD_DIM, HEAD_DIM), lambda t, h: (h, 0, 0))

    def whole(shape):
        return pl.BlockSpec(shape, lambda t, h: (0,) * len(shape))

    head_vec = jax.ShapeDtypeStruct((HEADS, 1, HEAD_DIM), F32)
    head_mat = jax.ShapeDtypeStruct((HEADS, HEAD_DIM, HEAD_DIM), F32)
    return pl.pallas_call(
        body, name=name, grid=(nt, HEADS),
        in_specs=[col(COL_LRU_X), _halo_spec(tb, nt, COL_LRU_X), col(COL_LRU_Z), col(0), _halo_spec(tb, nt, 0), col(0),
                  pl.BlockSpec((4, HEAD_DIM), lambda t, h: (0, h)), vec, mat, vec, mat, vec, vec, vec],
        out_specs=[col(0), col(0), whole((HEADS, 4, HEAD_DIM)), whole((HEADS, 1, HEAD_DIM)),
                   whole((HEADS, HEAD_DIM, HEAD_DIM)), whole((HEADS, 1, HEAD_DIM)),
                   whole((HEADS, HEAD_DIM, HEAD_DIM)), whole((HEADS, 1, HEAD_DIM)), whole((HEADS, 1, HEAD_DIM)),
                   whole((HEADS, 1, HEAD_DIM))],
        out_shape=[jax.ShapeDtypeStruct((s, D_MODEL), BF16), jax.ShapeDtypeStruct((s, D_MODEL), BF16),
                   jax.ShapeDtypeStruct((HEADS, 4, HEAD_DIM), F32), head_vec, head_mat, head_vec, head_mat, head_vec,
                   head_vec, head_vec],
        scratch_shapes=[pltpu.VMEM((1, tb + 8, HEAD_DIM), F32), pltpu.VMEM((tb + 8, HEAD_DIM), F32),
                        pltpu.VMEM((HEADS, tb + 8, HEAD_DIM), F32), pltpu.VMEM((HEADS, 8, HEAD_DIM), F32)],
        compiler_params=_params(dimension_semantics=("arbitrary", "arbitrary")),
    )(proj, proj, proj, hs, hs, dy, conv_w, conv_b, wa, ba, wx, bx, lam, nw)


def _dn_fwd(proj, y, conv_w, a_log_row, dt_row, nw, name):
    s = proj.shape[0]
    tb = min(TIME_BLOCK, s)
    nt = s // tb
    nchunk = tb // CHUNK
    col = _head_specs(tb, lambda t: t)

    def body(q_ref, k_ref, v_ref, z_ref, ba_ref, cwq_ref, cwk_ref, cwv_ref, al_ref, dt_ref, nw_ref, y_in_ref,
             y_ref, o_ref, st_ref, xbuf, state):
        t, h = pl.program_id(0), pl.program_id(1)

        @pl.when(t == 0)
        def _():
            for i in range(3):
                xbuf[3 * h + i, pl.ds(0, 8), :] = jnp.zeros((8, HEAD_DIM), F32)
            state[h] = jnp.zeros((HEAD_DIM, HEAD_DIM), F32)

        conv = []
        for i, (ref, cw_ref) in enumerate(((q_ref, cwq_ref), (k_ref, cwk_ref), (v_ref, cwv_ref))):
            xbuf[3 * h + i, pl.ds(8, tb), :] = ref[...]
            conv.append(_conv_taps(xbuf, 3 * h + i, cw_ref[...], tb))
            xbuf[3 * h + i, pl.ds(0, 8), :] = xbuf[3 * h + i, pl.ds(tb, 8), :]
        q, k, v, g, beta = _dn_prep(conv[0], conv[1], conv[2], ba_ref[...], al_ref[...], dt_ref[...], h)
        def chunks(a):
            return a.reshape(nchunk, CHUNK, a.shape[-1])

        u, w, attn, qe, kdec, eglast = _dn_chunks_head(chunks(q), chunks(k), chunks(v), chunks(g), chunks(beta))
        w_u = jnp.concatenate([w, u], axis=2)
        kdec_w_u = _dot(kdec, w_u, B_TN, "bf16")
        attn_w_u = _dot(attn, w_u, B_NN, "bf16")
        st = state[h]
        for c in range(nchunk):
            st_ref[c] = st
            st = st * eglast[c] - _NN_B(kdec_w_u[c, :, :HEAD_DIM], st) + kdec_w_u[c, :, HEAD_DIM:]
        state[h] = st
        o = _dot(qe - attn_w_u[:, :, :HEAD_DIM], st_ref[...], B_NN, "bf16") + attn_w_u[:, :, HEAD_DIM:]
        o_ref[...] = o.reshape(tb, HEAD_DIM)
        y_ref[...] = _gated_norm(o_ref[...], z_ref[...], nw_ref[...]).astype(BF16)

    def cw_spec(off):
        return pl.BlockSpec((4, HEAD_DIM), lambda t, h: (0, off + h))

    row128 = pl.BlockSpec((1, HEAD_DIM), lambda t, h: (0, 0))
    return pl.pallas_call(
        body, name=name, grid=(nt, HEADS),
        in_specs=[col(COL_Q), col(COL_K), col(COL_V), col(COL_DN_Z),
                  pl.BlockSpec((tb, HEAD_DIM), lambda t, h: (t, COL_BA)),
                  cw_spec(0), cw_spec(HEADS), cw_spec(2 * HEADS), row128, row128, row128,
                  pl.BlockSpec(memory_space=pl.ANY)],
        out_specs=[col(HEADS), col(0), pl.BlockSpec((None, nchunk, HEAD_DIM, HEAD_DIM), lambda t, h: (h, t, 0, 0))],
        out_shape=[jax.ShapeDtypeStruct((s, 2 * D_MODEL), BF16), jax.ShapeDtypeStruct((s, D_MODEL), F32),
                   jax.ShapeDtypeStruct((HEADS, s // CHUNK, HEAD_DIM, HEAD_DIM), F32)],
        input_output_aliases={11: 0},
        scratch_shapes=[pltpu.VMEM((3 * HEADS, tb + 8, HEAD_DIM), F32), pltpu.VMEM((HEADS, HEAD_DIM, HEAD_DIM), F32)],
        compiler_params=_params(dimension_semantics=("arbitrary", "arbitrary")),
    )(proj, proj, proj, proj, proj, conv_w, conv_w, conv_w, a_log_row, dt_row, nw, y)


def _dn_bwd(proj, o, states, dy, conv_w, a_log_row, dt_row, nw, name):
    s = proj.shape[0]
    tb = min(TIME_BLOCK, s)
    nt = s // tb
    nchunk = tb // CHUNK
    col = _head_specs(tb, lambda t: nt - 1 - t)

    def body(q_ref, qh_ref, k_ref, kh_ref, v_ref, vh_ref, z_ref, ba_ref, o_ref, st_ref, dy_ref,
             cwq_ref, cwk_ref, cwv_ref, al_ref, dt_ref, nw_ref,
             dq_ref, dk_ref, dv_ref, dz_ref, dba_ref, dcw_ref, dal_ref, ddt_ref, dnw_ref,
             xbuf, dbuf, dstate, dst_s):
        t, h = pl.program_id(0), pl.program_id(1)
        first_block = t == nt - 1

        @pl.when(t == 0)
        def _():
            for i in range(3):
                dbuf[3 * h + i, pl.ds(tb, 8), :] = jnp.zeros((8, HEAD_DIM), F32)
                dcw_ref[3 * h + i] = jnp.zeros((4, HEAD_DIM), F32)
            dstate[h] = jnp.zeros((HEAD_DIM, HEAD_DIM), F32)

        @pl.when((t == 0) & (h == 0))
        def _():
            for ref in (dal_ref, ddt_ref, dnw_ref):
                ref[...] = jnp.zeros_like(ref)

        keep = jnp.where(first_block, 0.0, 1.0)
        cws = (cwq_ref[...], cwk_ref[...], cwv_ref[...])
        conv = []
        for i, (ref, halo) in enumerate(((q_ref, qh_ref), (k_ref, kh_ref), (v_ref, vh_ref))):
            xbuf[i, pl.ds(0, 8), :] = halo[...] * keep
            xbuf[i, pl.ds(8, tb), :] = ref[...]
            conv.append(_conv_taps(xbuf, i, cws[i], tb))
        (q, k, v, g, beta), prep_vjp = jax.vjp(
            lambda qc, kc, vc, ba, al, dt: _dn_prep(qc, kc, vc, ba, al, dt, h),
            conv[0], conv[1], conv[2], ba_ref[...], al_ref[...], dt_ref[...])
        _, norm_vjp = jax.vjp(_gated_norm, o_ref[...], z_ref[...], nw_ref[...])
        do, dz, dnw = norm_vjp(dy_ref[...])
        dz_ref[...] = dz.astype(dz_ref.dtype)
        dnw_ref[...] += dnw

        def chunks(a):
            return a.reshape(nchunk, CHUNK, a.shape[-1])

        do = chunks(do)
        _, chunks_vjp, (w, attn, qe, kdec, eglast) = jax.vjp(
            _dn_chunks, chunks(q), chunks(k), chunks(v), chunks(g), chunks(beta), st_ref[...], has_aux=True)
        kdec_w = _dot(kdec, w, B_TN, "bf16")
        fixed = _dot(qe, do, B_TN, "bf16") - _dot(w, _dot(attn, do, B_TN, "bf16"), B_TN, "bf16")
        dst = dstate[h]
        for c in reversed(range(nchunk)):
            dst_s[c] = dst
            dst = dst * eglast[c] - _dot(kdec_w[c], dst, TN, "bf16") + fixed[c]
        dstate[h] = dst
        dq, dk, dv, dg, db, _ = chunks_vjp((do, dst_s[...]))

        def rows(a):
            return a.reshape(tb, a.shape[-1])

        dqc, dkc, dvc, dba, dal, ddt = prep_vjp((rows(dq), rows(dk), rows(dv), rows(dg), rows(db)))
        for i, (dxc, out) in enumerate(((dqc, dq_ref), (dkc, dk_ref), (dvc, dv_ref))):
            dx, dcw = _conv_backward(dbuf, 3 * h + i, xbuf, i, cws[i], dxc, tb)
            out[...] = dx.astype(out.dtype)
            dcw_ref[3 * h + i] += dcw
        dal_ref[...] += dal
        ddt_ref[...] += ddt

        @pl.when(h == 0)
        def _():
            dba_ref[...] = dba.astype(dba_ref.dtype)

        @pl.when(h > 0)
        def _():
            dba_ref[...] += dba.astype(dba_ref.dtype)

    def cw_spec(off):
        return pl.BlockSpec((4, HEAD_DIM), lambda t, h: (0, off + h))

    def whole(shape):
        return pl.BlockSpec(shape, lambda t, h: (0,) * len(shape))

    row128 = whole((1, HEAD_DIM))
    blk = (tb, HEAD_DIM)
    act = jax.ShapeDtypeStruct((s, D_MODEL), BF16)
    row_out = jax.ShapeDtypeStruct((1, HEAD_DIM), F32)
    return pl.pallas_call(
        body, name=name, grid=(nt, HEADS),
        in_specs=[col(COL_Q), _halo_spec(tb, nt, COL_Q), col(COL_K), _halo_spec(tb, nt, COL_K),
                  col(COL_V), _halo_spec(tb, nt, COL_V), col(COL_DN_Z),
                  pl.BlockSpec(blk, lambda t, h: (nt - 1 - t, COL_BA)), col(0),
                  pl.BlockSpec((None, nchunk, HEAD_DIM, HEAD_DIM), lambda t, h: (h, nt - 1 - t, 0, 0)), col(HEADS),
                  cw_spec(0), cw_spec(HEADS), cw_spec(2 * HEADS), row128, row128, row128],
        out_specs=[col(0), col(0), col(0), col(0), pl.BlockSpec(blk, lambda t, h: (nt - 1 - t, 0)),
                   whole((3 * HEADS, 4, HEAD_DIM)), row128, row128, row128],
        out_shape=[act, act, act, act, jax.ShapeDtypeStruct((s, HEAD_DIM), F32),
                   jax.ShapeDtypeStruct((3 * HEADS, 4, HEAD_DIM), F32), row_out, row_out, row_out],
        scratch_shapes=[pltpu.VMEM((3, tb + 8, HEAD_DIM), F32), pltpu.VMEM((3 * HEADS, tb + 8, HEAD_DIM), F32),
                        pltpu.VMEM((HEADS, HEAD_DIM, HEAD_DIM), F32), pltpu.VMEM((nchunk, HEAD_DIM, HEAD_DIM), F32)],
        compiler_params=_params(dimension_semantics=("arbitrary", "arbitrary")),
    )(proj, proj, proj, proj, proj, proj, proj, proj, o, states, dy, conv_w, conv_w, conv_w, a_log_row, dt_row, nw)


def _mesh_position():
    x, y, c = lax.axis_index("x"), lax.axis_index("y"), lax.axis_index("c")
    return x, y, c, 4 * x + 2 * y + c


def _peer(k, x, y, c):
    px = 1 - x if k & 4 else x
    py = 1 - y if k & 2 else y
    pc = 1 - c if k & 1 else c
    return (px, py, pc), 4 * px + 2 * py + pc


def _exchange_copies(ins, lands, scatter, send_sems, recv_sems, receives=True):
    x, y, c, me = _mesh_position()
    sends, recvs = [], []
    for i, (src, land) in enumerate(zip(ins, lands)):
        for k in range(1, N_DEV):
            peer, peer_id = _peer(k, x, y, c)
            sem = i * (N_DEV - 1) + k - 1
            for dst, out in ((me, sends), (peer_id, recvs)) if receives else ((me, sends),):
                out.append(pltpu.make_async_remote_copy(
                    src_ref=src.at[peer_id] if scatter[i] else src, dst_ref=land.at[dst],
                    send_sem=send_sems.at[sem], recv_sem=recv_sems.at[sem],
                    device_id=peer, device_id_type=pl.DeviceIdType.MESH))
    return sends, recvs


def _landing_shape(a, scatter):
    return a.shape if scatter else (N_DEV,) + a.shape


def _direct_exchange(arrays, scatter, name):
    n = len(arrays)
    out_shapes = [jax.ShapeDtypeStruct(_landing_shape(a, sc), a.dtype) for a, sc in zip(arrays, scatter)]

    def body(*refs):
        ins, outs = refs[:n], refs[n:2 * n]
        send_sems, recv_sems, local_sems = refs[2 * n:]
        me = _mesh_position()[3]
        local = [pltpu.make_async_copy(ins[i].at[me] if scatter[i] else ins[i], outs[i].at[me], local_sems.at[i])
                 for i in range(n)]
        sends, recvs = _exchange_copies(ins, outs, scatter, send_sems, recv_sems)
        for cp in local + sends:
            cp.start()
        for cp in recvs:
            cp.wait_recv()
        for cp in sends:
            cp.wait_send()
        for cp in local:
            cp.wait()

    hbm = pl.BlockSpec(memory_space=pl.ANY)
    return pl.pallas_call(
        body, name=name, in_specs=[hbm] * n, out_specs=[hbm] * n, out_shape=out_shapes,
        scratch_shapes=[pltpu.SemaphoreType.DMA((n * (N_DEV - 1),)), pltpu.SemaphoreType.DMA((n * (N_DEV - 1),)),
                        pltpu.SemaphoreType.DMA((n,))],
    )(*arrays)


def _two_level_gather(arrays, name):
    n = len(arrays)
    per = N_DEV - 1

    def body(*refs):
        ins, outs = refs[:n], refs[n:2 * n]
        send_sems, recv_sems, local_sems = refs[2 * n:]
        x, y, c, me = _mesh_position()
        sibling = (x, y, 1 - c)
        chips = [(1 - x, y), (x, 1 - y), (1 - x, 1 - y)]

        def copy(i, k, block, to, src=None):
            slot = outs[i].at[4 * block[0] + 2 * block[1] + block[2]]
            return pltpu.make_async_remote_copy(
                src_ref=slot if src is None else src, dst_ref=slot,
                send_sem=send_sems.at[i * per + k], recv_sem=recv_sems.at[i * per + k],
                device_id=to, device_id_type=pl.DeviceIdType.MESH)

        local = [pltpu.make_async_copy(ins[i], outs[i].at[me], local_sems.at[i]) for i in range(n)]
        first = []
        for i in range(n):
            first.append(copy(i, 0, (x, y, c), sibling, src=ins[i]))
            first += [copy(i, 1 + j, (x, y, c), (*chip, c), src=ins[i]) for j, chip in enumerate(chips)]
        for cp in local + first:
            cp.start()
        passed = []
        for i in range(n):
            for j, chip in enumerate(chips):
                copy(i, 1 + j, (*chip, c), (x, y, c)).wait_recv()
                passed.append(copy(i, 4 + j, (*chip, c), sibling))
                passed[-1].start()
        for i in range(n):
            copy(i, 0, sibling, (x, y, c)).wait_recv()
            for j, chip in enumerate(chips):
                copy(i, 4 + j, (*chip, 1 - c), (x, y, c)).wait_recv()
        for cp in first + passed:
            cp.wait_send()
        for cp in local:
            cp.wait()

    hbm = pl.BlockSpec(memory_space=pl.ANY)
    return pl.pallas_call(
        body, name=name, in_specs=[hbm] * n, out_specs=[hbm] * n,
        out_shape=[jax.ShapeDtypeStruct((N_DEV,) + a.shape, a.dtype) for a in arrays],
        scratch_shapes=[pltpu.SemaphoreType.DMA((n * per,)), pltpu.SemaphoreType.DMA((n * per,)),
                        pltpu.SemaphoreType.DMA((n,))],
    )(*arrays)


_HBM = pl.BlockSpec(memory_space=pltpu.HBM)
_SEM = pl.BlockSpec(memory_space=pltpu.SEMAPHORE)
_DATAFLOW = pltpu.SideEffectType.DATAFLOW_SIDE_EFFECTING


def _exchange_start(arrays, scatter, name):
    n = len(arrays)
    srcs = [pltpu.with_memory_space_constraint(a, pltpu.HBM) for a in arrays]
    lands = [pltpu.with_memory_space_constraint(lax.empty(_landing_shape(a, sc), a.dtype), pltpu.HBM)
             for a, sc in zip(arrays, scatter)]
    nsem = n * (N_DEV - 1)

    def body(*refs):
        ins, zones = refs[:n], refs[n:2 * n]
        send_sems, recv_sems = refs[2 * n], refs[2 * n + 1]
        token = refs[-1]
        sends, _ = _exchange_copies(ins, zones, scatter, send_sems, recv_sems, receives=False)
        for cp in sends:
            cp.start()
        token[...] = jnp.zeros_like(token)

    res = pl.pallas_call(
        body, name=name,
        out_shape=(pltpu.SemaphoreType.DMA((nsem,)), pltpu.SemaphoreType.DMA((nsem,)),
                   *[pltpu.HBM(a.shape, a.dtype) for a in srcs + lands], jax.ShapeDtypeStruct((8, HEAD_DIM), F32)),
        in_specs=[_HBM] * (2 * n),
        out_specs=(_SEM, _SEM, *[_HBM] * (2 * n), pl.BlockSpec(memory_space=pltpu.VMEM)),
        input_output_aliases={i: 2 + i for i in range(2 * n)},
        compiler_params=pltpu.CompilerParams(has_side_effects=_DATAFLOW),
    )(*srcs, *lands)
    return dict(sems=res[:2], srcs=res[2:2 + n], lands=res[2 + n:2 + 2 * n], token_block=res[-1],
                token=res[-1][0, 0], scatter=scatter)


def _exchange_wait(started, after, name):
    scatter = started["scatter"]
    n = len(scatter)

    def body(*refs):
        ins, zones = refs[:n], refs[n:2 * n]
        send_sems, recv_sems = refs[2 * n], refs[2 * n + 1]
        sends, recvs = _exchange_copies(ins, zones, scatter, send_sems, recv_sems)
        for cp in sends:
            cp.wait_send()
        for cp in recvs:
            cp.wait_recv()

    thru = list(started["srcs"]) + list(started["lands"])
    res = pl.pallas_call(
        body, name=name, out_shape=[pltpu.HBM(a.shape, a.dtype) for a in thru],
        in_specs=[_HBM] * (2 * n) + [_SEM, _SEM, pl.BlockSpec(memory_space=pl.ANY)], out_specs=[_HBM] * (2 * n),
        input_output_aliases={i: i for i in range(2 * n)},
        compiler_params=pltpu.CompilerParams(has_side_effects=_DATAFLOW),
    )(*thru, *started["sems"], after)
    me = 4 * lax.axis_index("x") + 2 * lax.axis_index("y") + lax.axis_index("c")
    out = []
    for src, got, sc in zip(res[:n], res[n:], scatter):
        own = lax.dynamic_index_in_dim(src, me, 0, keepdims=False) if sc else src
        out.append(lax.dynamic_update_index_in_dim(got, own, me, 0))
    return out


def _adamw(parts, w, m, v, name, rows_per_step, row_offset=0, into=None):
    rows, cols = parts.shape[1:]
    tr = min(rows_per_step, rows)
    assert rows % tr == 0 and row_offset % tr == 0, (name, rows, tr, row_offset)
    first = row_offset // tr
    c1 = 1.0 / (1.0 - ADAM_B1 ** ADAM_STEP)
    c2 = 1.0 / (1.0 - ADAM_B2 ** ADAM_STEP)

    def body(p_ref, w_ref, m_ref, v_ref, *rest):
        g_ref, d_ref, nm_ref, nv_ref = rest[-4:]
        g = p_ref[0].astype(F32)
        for d in range(1, N_DEV):
            g = g + p_ref[d].astype(F32)
        nm = ADAM_B1 * m_ref[...] + (1.0 - ADAM_B1) * g
        nv = ADAM_B2 * v_ref[...] + (1.0 - ADAM_B2) * (g * g)
        g_ref[...] = g
        nm_ref[...] = nm
        nv_ref[...] = nv
        d_ref[...] = -ADAM_LR * ((nm * c1) / (jnp.sqrt(nv * c2) + ADAM_EPS) + ADAM_WD * w_ref[...])

    blk = pl.BlockSpec((tr, cols), lambda i: (i + first, 0))
    shape = jax.ShapeDtypeStruct(w.shape, F32)
    prior = [] if into is None else list(into)
    return pl.pallas_call(
        body, name=name, grid=(rows // tr,),
        in_specs=[pl.BlockSpec((N_DEV, tr, cols), lambda i: (0, i, 0)), blk, blk, blk]
        + [pl.BlockSpec(memory_space=pl.ANY)] * len(prior),
        out_specs=[blk] * 4, out_shape=[shape] * 4,
        input_output_aliases={4 + j: j for j in range(len(prior))}, compiler_params=_params(),
    )(parts, w, m, v, *prior)


_LAYERED = ("norm_w", "lru_conv_b", "lru_wa", "lru_ba", "lru_wx", "lru_bx", "lru_lambda", "lru_norm_w",
            "dn_A_log", "dn_dt_bias", "dn_norm_w")
_PACK_LRU = _LAYERED[1:8]
_PACK_LAST = _LAYERED[:1] + _LAYERED[8:]
_WEIGHTS = ("norm_w", "w_in", "lru_conv_w", "lru_conv_b", "lru_wa", "lru_ba", "lru_wx", "lru_bx", "lru_lambda",
            "lru_norm_w", "dn_conv_w", "dn_A_log", "dn_dt_bias", "dn_norm_w", "w_out", "final_norm_w")


def _pack_layer(tree, layer, tail=(), names=_LAYERED):
    rows = []
    for name in names:
        a = tree[name][layer]
        if a.shape[-1] == HEADS:
            a = jnp.pad(a, (0, HEAD_DIM - HEADS))
        rows.append(a.reshape(-1, HEAD_DIM))
    rows += [t.reshape(-1, HEAD_DIM) for t in tail]
    packed = jnp.concatenate(rows, axis=0)
    return jnp.pad(packed, ((0, (-packed.shape[0]) % 8), (0, 0)))


def _unpack_layer(packed, like, names=_LAYERED):
    out, at = {}, 0
    for name in names:
        shape = like[name].shape[1:]
        if shape[-1] == HEADS:
            n = 1
            out[name] = packed[at, :HEADS]
        else:
            n = like[name][0].size // HEAD_DIM
            out[name] = packed[at:at + n].reshape(shape)
        at += n
    return out, at


def _heads_to_channels(a):
    return jnp.transpose(a, (1, 0, 2)).reshape(a.shape[1], HEADS * HEAD_DIM)


def kernel(x, norm_w, w_in, lru_conv_w, lru_conv_b, lru_wa, lru_ba, lru_wx, lru_bx, lru_lambda, lru_norm_w, dn_conv_w, dn_A_log, dn_dt_bias, dn_norm_w, w_out, final_norm_w, loss_target, m_norm_w, m_w_in, m_lru_conv_w, m_lru_conv_b, m_lru_wa, m_lru_ba, m_lru_wx, m_lru_bx, m_lru_lambda, m_lru_norm_w, m_dn_conv_w, m_dn_A_log, m_dn_dt_bias, m_dn_norm_w, m_w_out, m_final_norm_w, v_norm_w, v_w_in, v_lru_conv_w, v_lru_conv_b, v_lru_wa, v_lru_ba, v_lru_wx, v_lru_bx, v_lru_lambda, v_lru_norm_w, v_dn_conv_w, v_dn_A_log, v_dn_dt_bias, v_dn_norm_w, v_w_out, v_final_norm_w):
    weights = dict(norm_w=norm_w, w_in=w_in, lru_conv_w=lru_conv_w, lru_conv_b=lru_conv_b, lru_wa=lru_wa,
                   lru_ba=lru_ba, lru_wx=lru_wx, lru_bx=lru_bx, lru_lambda=lru_lambda, lru_norm_w=lru_norm_w,
                   dn_conv_w=dn_conv_w, dn_A_log=dn_A_log, dn_dt_bias=dn_dt_bias, dn_norm_w=dn_norm_w,
                   w_out=w_out, final_norm_w=final_norm_w)
    mom_m = dict(norm_w=m_norm_w, w_in=m_w_in, lru_conv_w=m_lru_conv_w, lru_conv_b=m_lru_conv_b, lru_wa=m_lru_wa,
                 lru_ba=m_lru_ba, lru_wx=m_lru_wx, lru_bx=m_lru_bx, lru_lambda=m_lru_lambda,
                 lru_norm_w=m_lru_norm_w, dn_conv_w=m_dn_conv_w, dn_A_log=m_dn_A_log, dn_dt_bias=m_dn_dt_bias,
                 dn_norm_w=m_dn_norm_w, w_out=m_w_out, final_norm_w=m_final_norm_w)
    mom_v = dict(norm_w=v_norm_w, w_in=v_w_in, lru_conv_w=v_lru_conv_w, lru_conv_b=v_lru_conv_b, lru_wa=v_lru_wa,
                 lru_ba=v_lru_ba, lru_wx=v_lru_wx, lru_bx=v_lru_bx, lru_lambda=v_lru_lambda,
                 lru_norm_w=v_lru_norm_w, dn_conv_w=v_dn_conv_w, dn_A_log=v_dn_A_log, dn_dt_bias=v_dn_dt_bias,
                 dn_norm_w=v_dn_norm_w, w_out=v_w_out, final_norm_w=v_final_norm_w)
    depth = norm_w.shape[0]
    xs = x[0]
    s = xs.shape[0]
    tm = min(1024, s)

    assert depth >= 2, depth

    def row(a):
        return a.reshape(1, -1)

    def pad_row(a):
        return jnp.pad(a, (0, HEAD_DIM - a.shape[0])).reshape(1, HEAD_DIM)

    def full_w_in(g):
        w = jnp.transpose(g, (1, 2, 0, 3)).reshape(g.shape[1], D_MODEL, D_IN)
        return jnp.pad(w, ((0, 0), (0, 0), (0, D_IN_PAD - D_IN)))

    g_win0, g_lcw, g_dcw = _two_level_gather([w_in[:1].astype(BF16), lru_conv_w, dn_conv_w], "gather_first")
    rest = _exchange_start([w_in[1:].astype(BF16), w_out.astype(BF16)], [False] * 2, "gather_rest_start")
    win = [full_w_in(g_win0)[0]]
    wout = None
    lcw = jnp.transpose(g_lcw, (1, 2, 0, 3)).reshape(depth, 4, D_MODEL)
    dcw = jnp.transpose(g_dcw, (1, 2, 0, 3)).reshape(depth, 4, 3 * D_MODEL)

    saved = []
    cur = xs
    for l in range(depth):
        nw_row = row(norm_w[l]) + rest["token"] if l == 0 else row(norm_w[l])
        hn = _rmsnorm_fwd(cur, nw_row, f"norm_fwd_{l}")
        proj = _matmul(hn, win[l], "nn", tm, 896, D_MODEL, f"in_proj_{l}")
        y_lru, hs = _lru_fwd(proj, lcw[l], row(lru_conv_b[l]), lru_wa[l], row(lru_ba[l]), lru_wx[l], row(lru_bx[l]),
                             row(lru_lambda[l]), row(lru_norm_w[l]), f"lru_fwd_{l}")
        ycat, o_dn, states = _dn_fwd(proj, y_lru, dcw[l], pad_row(dn_A_log[l]), pad_row(dn_dt_bias[l]),
                                     row(dn_norm_w[l]), f"dn_fwd_{l}")
        if l == 0:
            g_win_rest, g_wout = _exchange_wait(rest, ycat, "gather_rest_wait")
            win += list(full_w_in(g_win_rest))
            wout = jnp.transpose(g_wout, (1, 0, 2, 3)).reshape(depth, 2 * D_MODEL, D_MODEL)
        nxt = _matmul(ycat, wout[l], "nn", tm, D_MODEL, 2 * D_MODEL, f"out_proj_{l}", add=cur)
        saved.append((cur, hn, proj, hs, o_dn, states, ycat))
        cur = nxt
    loss_part, dx, d_final = _final_loss(cur, row(final_norm_w), loss_target[0], "final_loss")

    def win_slots(g):
        return jnp.transpose(g.reshape(D_MODEL, N_DEV, D_IN // N_DEV), (1, 0, 2))

    def wout_slots(g):
        return g.reshape(N_DEV, 2 * D_MODEL // N_DEV, D_MODEL)

    grads = {k: [None] * depth for k in _WEIGHTS if k not in ("final_norm_w", "w_in", "w_out")}
    started = {}
    token = None
    for l in reversed(range(depth)):
        x_in, hn, proj, hs, o_dn, states, ycat = saved[l]
        dy = _matmul(dx, wout[l], "nt", tm, D_MODEL, D_MODEL, f"out_proj_dy_{l}")
        g_wout_l = _matmul(ycat, dx, "tn", D_MODEL, D_MODEL, tm, f"out_proj_dw_{l}", out_dtype=BF16)
        if l == 0:
            started["w_out_0"] = _exchange_start([wout_slots(g_wout_l)], [True], "exchange_w_out_0_start")
            token = token + started["w_out_0"]["token"]
        cb_row = row(lru_conv_b[l]) if token is None else row(lru_conv_b[l]) + token
        (dlx, dlz, g_lcw, g_lcb, g_wa, g_ba, g_wx, g_bx, g_lam, g_lnw) = _lru_bwd(
            proj, hs, dy, lcw[l], cb_row, lru_wa[l], row(lru_ba[l]), lru_wx[l], row(lru_bx[l]),
            row(lru_lambda[l]), row(lru_norm_w[l]), f"lru_bwd_{l}")
        grads["lru_conv_w"][l] = _heads_to_channels(g_lcw)
        grads["lru_conv_b"][l] = g_lcb.reshape(D_MODEL)
        grads["lru_wa"][l] = g_wa
        grads["lru_ba"][l] = g_ba.reshape(D_MODEL)
        grads["lru_wx"][l] = g_wx
        grads["lru_bx"][l] = g_bx.reshape(D_MODEL)
        grads["lru_lambda"][l] = g_lam.reshape(D_MODEL)
        grads["lru_norm_w"][l] = g_lnw.reshape(D_MODEL)
        al_row = pad_row(dn_A_log[l])
        if l == 0:
            started["pack_0"] = _exchange_start([_pack_layer(grads, 0, names=_PACK_LRU)], [False],
                                                "exchange_pack_0_start")
            al_row = al_row + started["pack_0"]["token"]
        (dq, dk, dv, ddz, dba, g_dcw3, g_al, g_dt, g_dnw) = _dn_bwd(
            proj, o_dn, states, dy, dcw[l], al_row, pad_row(dn_dt_bias[l]), row(dn_norm_w[l]), f"dn_bwd_{l}")
        g_dcw3 = g_dcw3.reshape(HEADS, 3, 4, HEAD_DIM)
        grads["dn_conv_w"][l] = jnp.concatenate([_heads_to_channels(g_dcw3[:, i]) for i in range(3)], axis=1)
        grads["dn_A_log"][l] = g_al[0, :HEADS]
        grads["dn_dt_bias"][l] = g_dt[0, :HEADS]
        grads["dn_norm_w"][l] = g_dnw.reshape(HEAD_DIM)
        dep = None
        dproj = jnp.concatenate([dlx, dlz, dq, dk, dv, ddz, dba.astype(BF16)], axis=1)
        g_win_l = _matmul(hn, dproj, "tn", D_MODEL, 896, tm, f"in_proj_dw_{l}", out_dtype=BF16)[:, :D_IN]
        if l == 0:
            started[0] = _exchange_start([win_slots(g_win_l)], [True], "exchange_0_start")
            dep = started[0]["token_block"]
        dh = _matmul(dproj, win[l], "nt", tm, D_MODEL, 896, f"in_proj_dh_{l}", dep=dep)
        dx, g_nw = _rmsnorm_bwd(x_in, row(norm_w[l]), dh, dx, f"norm_bwd_{l}")
        grads["norm_w"][l] = g_nw.reshape(D_MODEL)
        if l > 0:
            tail = (d_final, loss_part) if l == depth - 1 else ()
            started[l] = _exchange_start([win_slots(g_win_l), wout_slots(g_wout_l), _pack_layer(grads, l, tail)],
                                         [True, True, False], f"exchange_{l}_start")
            token = started[l]["token"]

    def conv_slots(a):
        dd, r, cc = a.shape
        return jnp.transpose(a.reshape(dd, r, N_DEV, cc // N_DEV), (2, 0, 1, 3))

    small = _exchange_start(
        [conv_slots(jnp.stack(grads["lru_conv_w"])), conv_slots(jnp.stack(grads["dn_conv_w"])),
         _pack_layer(grads, 0, names=_PACK_LAST)], [True, True, False], "exchange_small_start")

    new = {}
    flat_in = (depth * D_MODEL, D_IN // N_DEV)
    flat_out = (depth * 2 * D_MODEL // N_DEV, D_MODEL)
    zero_row = jnp.zeros((1, HEAD_DIM), F32)

    def adamw_pack(parts, layer, names=_LAYERED, name="adamw_small"):
        tails = [(t, zero_row) if layer == depth - 1 else () for t in (final_norm_w, m_final_norm_w, v_final_norm_w)]
        return _adamw(parts, _pack_layer(weights, layer, tails[0], names), _pack_layer(mom_m, layer, tails[1], names),
                      _pack_layer(mom_v, layer, tails[2], names), f"{name}_{layer}", parts.shape[1])

    def adamw_w_in(parts, layer, into):
        return _adamw(parts, w_in.reshape(flat_in), m_w_in.reshape(flat_in), v_w_in.reshape(flat_in),
                      f"adamw_w_in_{layer}", 256, layer * D_MODEL, into)

    def adamw_w_out(parts, layer, into):
        return _adamw(parts, w_out.reshape(flat_out), m_w_out.reshape(flat_out), v_w_out.reshape(flat_out),
                      f"adamw_w_out_{layer}", 256, layer * flat_out[0] // depth, into)

    acc_in = acc_out = None
    packs = [None] * depth
    after = small["token_block"]
    for l in reversed(range(1, depth)):
        r_win, r_wout, r_pack = _exchange_wait(started[l], after, f"exchange_{l}_wait")
        acc_in = adamw_w_in(r_win, l, acc_in)
        acc_out = adamw_w_out(r_wout, l, acc_out)
        packs[l] = adamw_pack(r_pack, l)
        after = packs[l][0]
    (r_wout,) = _exchange_wait(started["w_out_0"], after, "exchange_w_out_0_wait")
    acc_out = adamw_w_out(r_wout, 0, acc_out)
    (r_win,) = _exchange_wait(started[0], acc_out[0], "exchange_0_wait")
    acc_in = adamw_w_in(r_win, 0, acc_in)
    (r_pack,) = _exchange_wait(started["pack_0"], acc_in[0], "exchange_pack_0_wait")
    packs[0] = adamw_pack(r_pack, 0, _PACK_LRU)
    r_lcw, r_dcw, r_last = _exchange_wait(small, packs[0][0], "exchange_small_wait")
    for name, parts in (("lru_conv_w", r_lcw), ("dn_conv_w", r_dcw)):
        w = weights[name]
        flat = (-1, w.shape[-1])
        outs = _adamw(parts.reshape((N_DEV,) + (w.size // w.shape[-1], w.shape[-1])), w.reshape(flat),
                      mom_m[name].reshape(flat), mom_v[name].reshape(flat), f"adamw_{name}", 8)
        new[name] = [a.reshape(w.shape) for a in outs]
    last_0 = adamw_pack(r_last, 0, _PACK_LAST, "adamw_last")
    new["w_in"] = [a.reshape(w_in.shape) for a in acc_in]
    new["w_out"] = [a.reshape(w_out.shape) for a in acc_out]
    for i in range(4):
        layers = [{**_unpack_layer(packs[0][i], weights, _PACK_LRU)[0],
                   **_unpack_layer(last_0[i], weights, _PACK_LAST)[0]}]
        layers += [_unpack_layer(packs[l][i], weights)[0] for l in range(1, depth)]
        for name in _LAYERED:
            new.setdefault(name, []).append(jnp.stack([layer[name] for layer in layers]))
    tail_at = _unpack_layer(packs[depth - 1][0], weights)[1]
    rows_final = D_MODEL // HEAD_DIM
    new["final_norm_w"] = [packs[depth - 1][i][tail_at:tail_at + rows_final].reshape(D_MODEL) for i in range(4)]
    loss = packs[depth - 1][0][tail_at + rows_final, 0]
    out = [loss, dx.reshape(x.shape)]
    for i in range(4):
        out += [new[name][i] for name in _WEIGHTS]
    return tuple(out)
```

```python
import functools

import jax
import jax.numpy as jnp
from jax import lax
from jax.experimental import pallas as pl
from jax.experimental.pallas import tpu as pltpu

F32 = jnp.float32
BF16 = jnp.bfloat16

N_DEV = 8
D_MODEL = 1024
HEADS = 8
HEAD_DIM = 128
CHUNK = 64
D_IN = 6160
D_IN_PAD = 6272
COL_LRU_X, COL_LRU_Z, COL_Q, COL_K, COL_V, COL_DN_Z, COL_BA = 0, 8, 16, 24, 32, 40, 48
LRU_C = 8.0
EPS = 1e-6
ADAM_LR, ADAM_B1, ADAM_B2, ADAM_EPS, ADAM_WD, ADAM_STEP = 0.001, 0.9, 0.999, 1e-08, 0.01, 10
TIME_BLOCK = 1024
PART_CHUNKS = 8
PART_CHUNKS_BWD = 16
VMEM_LIMIT = 56 * 1024 * 1024

NN = (((1,), (0,)), ((), ()))
NT = (((1,), (1,)), ((), ()))
TN = (((0,), (0,)), ((), ()))


B_NN = (((2,), (1,)), ((0,), (0,)))
B_NT = (((2,), (2,)), ((0,), (0,)))
B_TN = (((1,), (1,)), ((0,), (0,)))


def _split_bf16(x):
    hi = x.astype(BF16)
    return hi, (x - hi.astype(F32)).astype(BF16)


def _dot(a, b, dims, prec):
    if prec == "bf16":
        return lax.dot_general(a.astype(BF16), b.astype(BF16), dims, preferred_element_type=F32)
    a1, a2 = _split_bf16(a)
    b1, b2 = _split_bf16(b)
    dg = functools.partial(lax.dot_general, dimension_numbers=dims, preferred_element_type=F32)
    return dg(a1, b1) + (dg(a1, b2) + dg(a2, b1))


def _make_mm(prec, nn_dims, nt_dims, tn_dims):
    @jax.custom_vjp
    def nn(a, b):
        return _dot(a, b, nn_dims, prec)

    @jax.custom_vjp
    def nt(a, b):
        return _dot(a, b, nt_dims, prec)

    @jax.custom_vjp
    def tn(a, b):
        return _dot(a, b, tn_dims, prec)

    nn.defvjp(lambda a, b: (_dot(a, b, nn_dims, prec), (a, b)),
              lambda r, g: (_dot(g, r[1], nt_dims, prec), _dot(r[0], g, tn_dims, prec)))
    nt.defvjp(lambda a, b: (_dot(a, b, nt_dims, prec), (a, b)),
              lambda r, g: (_dot(g, r[1], nn_dims, prec), _dot(g, r[0], tn_dims, prec)))
    tn.defvjp(lambda a, b: (_dot(a, b, tn_dims, prec), (a, b)),
              lambda r, g: (_dot(r[1], g, nt_dims, prec), _dot(r[0], g, nn_dims, prec)))
    return nn, nt, tn


_NN_B, _NT_B, _TN_B = _make_mm("bf16", NN, NT, TN)
_BNN, _BNT, _BTN = _make_mm("bf16", B_NN, B_NT, B_TN)


def _unit_lower_inverse_steps(a, tick=lambda: None):
    n = a.shape[-1]
    eye = (lax.broadcasted_iota(jnp.int32, a.shape, 1) == lax.broadcasted_iota(jnp.int32, a.shape, 2)).astype(F32)
    dg = functools.partial(lax.dot_general, dimension_numbers=B_NN, preferred_element_type=F32)
    inv = eye - a
    pw = _dot(a, a, B_NN, "bf16x3")
    steps = n.bit_length() - 2
    for j in range(steps):
        i1, i2 = _split_bf16(inv)
        p1, p2 = _split_bf16(pw)
        square = j + 1 < steps
        by_hi = dg(jnp.concatenate([i1, i2, p1, p2] if square else [i1, i2], axis=1), p1)
        by_lo = dg(jnp.concatenate([i1, p1], axis=1) if square else i1, p2)
        inv = inv + (by_hi[:, :n] + (by_lo[:, :n] + by_hi[:, n:2 * n]))
        if square:
            pw = by_hi[:, 2 * n:3 * n] + (by_lo[:, n:] + by_hi[:, 3 * n:])
        tick()
    return inv


@jax.custom_vjp
def _unit_lower_inverse(a):
    return _unit_lower_inverse_steps(a)


def _uli_fwd(a):
    inv = _unit_lower_inverse(a)
    return inv, inv


def _uli_bwd(inv, g):
    return (-_dot(_dot(inv, g, B_TN, "bf16"), inv, B_NT, "bf16"),)


_unit_lower_inverse.defvjp(_uli_fwd, _uli_bwd)


def _rows2(y, m):
    return y[:, :m], y[:, m:]


@jax.custom_vjp
def _pair_nn(x1, x2, r):
    return _rows2(_dot(jnp.concatenate([x1, x2], axis=1), r, B_NN, "bf16"), x1.shape[1])


def _pair_nn_bwd(res, g):
    x1, x2, r = res
    g = jnp.concatenate(g, axis=1)
    dx1, dx2 = _rows2(_dot(g, r, B_NT, "bf16"), x1.shape[1])
    return dx1, dx2, _dot(jnp.concatenate([x1, x2], axis=1), g, B_TN, "bf16")


_pair_nn.defvjp(lambda x1, x2, r: (_pair_nn(x1, x2, r), (x1, x2, r)), _pair_nn_bwd)


@jax.custom_vjp
def _pair_nt(x1, x2, r):
    return _rows2(_dot(jnp.concatenate([x1, x2], axis=1), r, B_NT, "bf16"), x1.shape[1])


def _pair_nt_bwd(res, g):
    x1, x2, r = res
    g = jnp.concatenate(g, axis=1)
    dx1, dx2 = _rows2(_dot(g, r, B_NN, "bf16"), x1.shape[1])
    return dx1, dx2, _dot(g, jnp.concatenate([x1, x2], axis=1), B_TN, "bf16")


_pair_nt.defvjp(lambda x1, x2, r: (_pair_nt(x1, x2, r), (x1, x2, r)), _pair_nt_bwd)


@jax.custom_vjp
def _wide_nn(l, r1, r2):
    y = _dot(l, jnp.concatenate([r1, r2], axis=2), B_NN, "bf16")
    return y[:, :, :r1.shape[2]], y[:, :, r1.shape[2]:]


def _wide_nn_bwd(res, g):
    l, r1, r2 = res
    g = jnp.concatenate(g, axis=2)
    dr = _dot(l, g, B_TN, "bf16")
    return (_dot(g, jnp.concatenate([r1, r2], axis=2), B_NT, "bf16"), dr[:, :, :r1.shape[2]], dr[:, :, r1.shape[2]:])


_wide_nn.defvjp(lambda l, r1, r2: (_wide_nn(l, r1, r2), (l, r1, r2)), _wide_nn_bwd)


def _lower_ones(batch, n):
    shape = (batch, n, n)
    return (lax.broadcasted_iota(jnp.int32, shape, 1) >= lax.broadcasted_iota(jnp.int32, shape, 2)).astype(BF16)


@jax.custom_vjp
def _chunk_cumsum(g):
    tri = _lower_ones(g.shape[0], g.shape[1])
    g1, g2 = _split_bf16(g)
    g3 = (g - g1.astype(F32) - g2.astype(F32)).astype(BF16)
    dg = functools.partial(lax.dot_general, dimension_numbers=B_NN, preferred_element_type=F32)
    return dg(tri, g1) + (dg(tri, g2) + dg(tri, g3))


def _chunk_cumsum_bwd(_, ct):
    tri = _lower_ones(ct.shape[0], ct.shape[1])
    c1, c2 = _split_bf16(ct)
    dg = functools.partial(lax.dot_general, dimension_numbers=B_TN, preferred_element_type=F32)
    return (dg(tri, c1) + dg(tri, c2),)


_chunk_cumsum.defvjp(lambda g: (_chunk_cumsum(g), None), _chunk_cumsum_bwd)


def _expm1(x):
    small = x * (1.0 + x * (0.5 + x * (1.0 / 6 + x * (1.0 / 24 + x * (1.0 / 120 + x * (1.0 / 720))))))
    return jnp.where(jnp.abs(x) < 0.2, small, jnp.exp(x) - 1.0)


def _sigmoid(x):
    return 1.0 / (1.0 + jnp.exp(-x))


def _silu(x):
    return x * _sigmoid(x)


def _softplus(x):
    return jnp.maximum(x, 0.0) + jnp.log(1.0 + jnp.exp(-jnp.abs(x)))


def _rmsnorm(x, w):
    return x * lax.rsqrt(jnp.mean(x * x, axis=-1, keepdims=True) + EPS) * w


def _gated_norm(o, z, w):
    return o * lax.rsqrt(jnp.mean(o * o, axis=-1, keepdims=True) + EPS) * w * _silu(z)


def _lru_gates(xc, wa, ba, wx, bx, lam):
    r = _sigmoid(_NN_B(xc, wa) + ba)
    i = _sigmoid(_NN_B(xc, wx) + bx)
    log_a = -LRU_C * r * _softplus(-lam)
    a = jnp.exp(log_a)
    mult = jnp.sqrt(-_expm1(2.0 * log_a))
    return a, mult * (i * xc)


def _scan_forward(a, b, h0):
    rows = a.shape[0]
    row = lax.broadcasted_iota(jnp.int32, a.shape, 0)
    k = 1
    while k < rows:
        seen = row >= k
        b = jnp.where(seen, a * pltpu.roll(b, k, 0) + b, b)
        a = jnp.where(seen, a * pltpu.roll(a, k, 0), a)
        k *= 2
    return b + a * h0


def _scan_reverse(a, d, carry):
    rows = a.shape[0]
    row = lax.broadcasted_iota(jnp.int32, a.shape, 0)
    last = row == rows - 1
    c = jnp.where(last, 0.0, pltpu.roll(a, rows - 1, 0))
    d = d + jnp.where(last, carry, 0.0)
    k = 1
    while k < rows:
        seen = row < rows - k
        d = jnp.where(seen, d + c * pltpu.roll(d, rows - k, 0), d)
        c = jnp.where(seen, c * pltpu.roll(c, rows - k, 0), c)
        k *= 2
    return d


def _lane_pick(row, lane_index):
    lane = lax.broadcasted_iota(jnp.int32, row.shape, 1)
    return jnp.sum(jnp.where(lane == lane_index, row, 0.0), axis=-1, keepdims=True)


def _dn_prep(qc, kc, vc, ba, a_log_row, dt_row, head):
    q = _silu(qc)
    k = _silu(kc)
    v = _silu(vc)
    q = q * lax.rsqrt(jnp.sum(q * q, axis=-1, keepdims=True) + EPS) * (HEAD_DIM ** -0.5)
    k = k * lax.rsqrt(jnp.sum(k * k, axis=-1, keepdims=True) + EPS)
    beta = _sigmoid(_lane_pick(ba, head))
    g = -jnp.exp(_lane_pick(a_log_row, head)) * _softplus(_lane_pick(ba, HEADS + head) + _lane_pick(dt_row, head))
    return q, k, v, g, beta


def _dn_chunks_head(q, k, v, gcol, bcol, tick=None):
    n, c, d = q.shape
    row = lax.broadcasted_iota(jnp.int32, (n, c, c), 1)
    col = lax.broadcasted_iota(jnp.int32, (n, c, c), 2)
    g_wide = jnp.broadcast_to(gcol, (n, c, d))
    b_wide = jnp.broadcast_to(bcol, (n, c, d))
    gc = _chunk_cumsum(g_wide)
    gc_rows = gc[:, :, :c]
    decay = jnp.exp(jnp.where(row >= col, gc_rows - jnp.swapaxes(gc_rows, 1, 2), -1e30))
    kb = k * b_wide
    eg = jnp.exp(gc)
    if tick is not None:
        tick()
    kbk, qk = _pair_nt(kb, q, k)
    a = jnp.where(row > col, kbk * decay, 0.0)
    if tick is not None:
        tick()
    tinv = _unit_lower_inverse(a) if tick is None else _unit_lower_inverse_steps(a, tick)
    u, w = _wide_nn(tinv, v * b_wide, kb * eg)
    g_last = jnp.sum(g_wide, axis=1, keepdims=True)
    return u, w, qk * decay, q * eg, k * jnp.exp(g_last - gc), jnp.exp(g_last)


def _dn_chunks(q, k, v, gcol, bcol, states):
    u, w, attn, qe, kdec, eglast = _dn_chunks_head(q, k, v, gcol, bcol)
    w_st, qe_st = _pair_nn(w, qe, states)
    v_new = u - w_st
    o = qe_st + _BNN(attn, v_new)
    return (o, states * eglast + _BTN(kdec, v_new)), (w, attn, qe, kdec, eglast)


def _conv_taps(buf, head, cw, rows):
    acc = cw[0:1, :] * buf[head, pl.ds(5, rows), :]
    for j in range(1, 4):
        acc = acc + cw[j:j + 1, :] * buf[head, pl.ds(5 + j, rows), :]
    return acc


def _conv_backward(dbuf, dhead, xbuf, xhead, cw, dxc, rows):
    dbuf[dhead, pl.ds(0, rows), :] = dxc
    dx = cw[0:1, :] * dbuf[dhead, pl.ds(3, rows), :]
    for j in range(1, 4):
        dx = dx + cw[j:j + 1, :] * dbuf[dhead, pl.ds(3 - j, rows), :]
    dcw = jnp.concatenate(
        [jnp.sum(dxc * xbuf[xhead, pl.ds(5 + j, rows), :], axis=0, keepdims=True) for j in range(4)], axis=0)
    dbuf[dhead, pl.ds(rows, 8), :] = dbuf[dhead, pl.ds(0, 8), :]
    return dx, dcw


def _params(**kw):
    return pltpu.CompilerParams(vmem_limit_bytes=VMEM_LIMIT, **kw)


def _matmul(a, b, form, tm, tn, tk, name, add=None, out_dtype=F32, dep=None):
    if form == "nn":
        (m, kdim), (_, n) = a.shape, b.shape
        a_spec = pl.BlockSpec((tm, tk), lambda j, i, k: (i, k))
        b_spec = pl.BlockSpec((tk, tn), lambda j, i, k: (k, j))
        dims = NN
    elif form == "nt":
        (m, kdim), (n, _) = a.shape, b.shape
        a_spec = pl.BlockSpec((tm, tk), lambda j, i, k: (i, k))
        b_spec = pl.BlockSpec((tn, tk), lambda j, i, k: (j, k))
        dims = NT
    else:
        (kdim, m), (_, n) = a.shape, b.shape
        a_spec = pl.BlockSpec((tk, tm), lambda j, i, k: (k, i))
        b_spec = pl.BlockSpec((tk, tn), lambda j, i, k: (k, j))
        dims = TN
    assert m % tm == 0 and n % tn == 0 and kdim % tk == 0, (name, m, n, kdim, tm, tn, tk)
    ksteps = kdim // tk
    o_spec = pl.BlockSpec((tm, tn), lambda j, i, k: (i, j))
    has_add = add is not None
    extra = [] if dep is None else [dep]

    def body(*refs):
        a_ref, b_ref = refs[:2]
        c_ref = refs[2] if has_add else None
        o_ref, acc = refs[-2:]
        k = pl.program_id(2)

        @pl.when(k == 0)
        def _():
            acc[...] = c_ref[...] if has_add else jnp.zeros_like(acc)

        acc[...] += lax.dot_general(a_ref[...].astype(BF16), b_ref[...].astype(BF16), dims,
                                    preferred_element_type=F32)

        @pl.when(k == ksteps - 1)
        def _():
            o_ref[...] = acc[...].astype(o_ref.dtype)

    in_specs = [a_spec, b_spec] + ([o_spec] if has_add else []) + [pl.BlockSpec((8, HEAD_DIM), lambda j, i, k: (0, 0))
                                                                   for _ in extra]
    args = (a, b) + ((add,) if has_add else ()) + tuple(extra)
    return pl.pallas_call(
        body, name=name, grid=(n // tn, m // tm, ksteps), in_specs=in_specs, out_specs=o_spec,
        out_shape=jax.ShapeDtypeStruct((m, n), out_dtype), scratch_shapes=[pltpu.VMEM((tm, tn), F32)],
        compiler_params=_params(dimension_semantics=("parallel", "parallel", "arbitrary")),
    )(*args)


def _matmul_nt_parts(parts, w, tm, tn, name, add, dep=None):
    m, c = parts[0].shape
    n = w.shape[0]
    count = len(parts)
    assert m % tm == 0 and n % tn == 0 and all(p.shape == (m, c) for p in parts), (name, m, n, c)
    extra = [] if dep is None else [dep]

    def body(*refs):
        part_refs, w_ref, c_ref = refs[:count], refs[count], refs[count + 1]
        o_ref, acc = refs[-2:]
        k = pl.program_id(2)

        @pl.when(k == 0)
        def _():
            acc[...] = c_ref[...]

        for p in range(count):
            @pl.when(k == p)
            def _(p=p):
                acc[...] += lax.dot_general(part_refs[p][...].astype(BF16), w_ref[...].astype(BF16), NT,
                                            preferred_element_type=F32)

        @pl.when(k == count - 1)
        def _():
            o_ref[...] = acc[...]

    o_spec = pl.BlockSpec((tm, tn), lambda j, i, k: (i, j))
    in_specs = ([pl.BlockSpec((tm, c), lambda j, i, k: (i, 0))] * count + [pl.BlockSpec((tn, c), lambda j, i, k: (j, k)), o_spec]
                + [pl.BlockSpec((8, HEAD_DIM), lambda j, i, k: (0, 0)) for _ in extra])
    return pl.pallas_call(
        body, name=name, grid=(n // tn, m // tm, count), in_specs=in_specs, out_specs=o_spec,
        out_shape=jax.ShapeDtypeStruct((m, n), F32), scratch_shapes=[pltpu.VMEM((tm, tn), F32)],
        compiler_params=_params(dimension_semantics=("parallel", "parallel", "arbitrary")),
    )(*parts, w, add, *extra)


def _matmul_tn_parts(a, parts, tm, tn, tk, name, out_dtype):
    kdim, m = a.shape
    c = parts[0].shape[1]
    count = len(parts)
    per = c // tn
    assert m % tm == 0 and c % tn == 0 and kdim % tk == 0 and all(p.shape == (kdim, c) for p in parts), (name, m, c)
    ksteps = kdim // tk

    def body(*refs):
        a_ref, part_refs = refs[0], refs[1:1 + count]
        o_ref, acc = refs[-2:]
        j, k = pl.program_id(0), pl.program_id(2)

        @pl.when(k == 0)
        def _():
            acc[...] = jnp.zeros_like(acc)

        for p in range(count):
            @pl.when(j // per == p)
            def _(p=p):
                acc[...] += lax.dot_general(a_ref[...].astype(BF16), part_refs[p][...].astype(BF16), TN,
                                            preferred_element_type=F32)

        @pl.when(k == ksteps - 1)
        def _():
            o_ref[...] = acc[...].astype(o_ref.dtype)

    def part_spec(p):
        return pl.BlockSpec((tk, tn), lambda j, i, k: (jnp.where(j // per == p, k, 0), jnp.where(j // per == p, j % per, 0)))

    return pl.pallas_call(
        body, name=name, grid=(count * per, m // tm, ksteps),
        in_specs=[pl.BlockSpec((tk, tm), lambda j, i, k: (k, i))] + [part_spec(p) for p in range(count)],
        out_specs=pl.BlockSpec((tm, tn), lambda j, i, k: (i, j)),
        out_shape=jax.ShapeDtypeStruct((m, count * c), out_dtype), scratch_shapes=[pltpu.VMEM((tm, tn), F32)],
        compiler_params=_params(dimension_semantics=("parallel", "parallel", "arbitrary")),
    )(a, *parts)


def _rmsnorm_fwd(x, w_row, name):
    s = x.shape[0]
    tb = min(TIME_BLOCK, s)

    def body(x_ref, w_ref, o_ref):
        o_ref[...] = _rmsnorm(x_ref[...], w_ref[...]).astype(BF16)

    return pl.pallas_call(
        body, name=name, grid=(s // tb,),
        in_specs=[pl.BlockSpec((tb, D_MODEL), lambda i: (i, 0)), pl.BlockSpec((1, D_MODEL), lambda i: (0, 0))],
        out_specs=pl.BlockSpec((tb, D_MODEL), lambda i: (i, 0)),
        out_shape=jax.ShapeDtypeStruct((s, D_MODEL), BF16), compiler_params=_params(),
    )(x, w_row)


def _rmsnorm_bwd(x, w_row, dh, dres, name):
    s = x.shape[0]
    tb = min(TIME_BLOCK, s)

    def body(x_ref, w_ref, dh_ref, dres_ref, dx_ref, dw_ref):
        _, vjp = jax.vjp(_rmsnorm, x_ref[...], w_ref[...])
        dx, dw = vjp(dh_ref[...])
        dx_ref[...] = dres_ref[...] + dx

        @pl.when(pl.program_id(0) == 0)
        def _():
            dw_ref[...] = jnp.zeros_like(dw_ref)

        dw_ref[...] += dw

    row = pl.BlockSpec((tb, D_MODEL), lambda i: (i, 0))
    vec = pl.BlockSpec((1, D_MODEL), lambda i: (0, 0))
    return pl.pallas_call(
        body, name=name, grid=(s // tb,), in_specs=[row, vec, row, row], out_specs=[row, vec],
        out_shape=[jax.ShapeDtypeStruct((s, D_MODEL), F32), jax.ShapeDtypeStruct((1, D_MODEL), F32)],
        compiler_params=_params(),
    )(x, w_row, dh, dres)


def _final_loss(x, w_row, target, name):
    s = x.shape[0]
    tb = min(TIME_BLOCK, s)

    def loss_fn(xv, wv, tv):
        err = _rmsnorm(xv, wv) - tv
        return 0.5 * jnp.sum(jnp.sum(err * err, axis=-1, keepdims=True), axis=0, keepdims=True) * (1.0 / D_MODEL)

    def body(x_ref, w_ref, t_ref, loss_ref, dx_ref, dw_ref):
        tv = t_ref[...]
        loss, vjp = jax.vjp(lambda xv, wv: loss_fn(xv, wv, tv), x_ref[...], w_ref[...])
        dx, dw = vjp(jnp.ones((1, 1), F32))
        dx_ref[...] = dx

        @pl.when(pl.program_id(0) == 0)
        def _():
            dw_ref[...] = jnp.zeros_like(dw_ref)
            loss_ref[...] = jnp.zeros_like(loss_ref)

        dw_ref[...] += dw
        loss_ref[...] += jnp.broadcast_to(loss, loss_ref.shape)

    row = pl.BlockSpec((tb, D_MODEL), lambda i: (i, 0))
    vec = pl.BlockSpec((1, D_MODEL), lambda i: (0, 0))
    return pl.pallas_call(
        body, name=name, grid=(s // tb,), in_specs=[row, vec, row],
        out_specs=[pl.BlockSpec((1, HEAD_DIM), lambda i: (0, 0)), row, vec],
        out_shape=[jax.ShapeDtypeStruct((1, HEAD_DIM), F32), jax.ShapeDtypeStruct((s, D_MODEL), F32),
                   jax.ShapeDtypeStruct((1, D_MODEL), F32)],
        compiler_params=_params(),
    )(x, w_row, target)


def _head_specs(tb, time_of):
    def col(off):
        return pl.BlockSpec((tb, HEAD_DIM), lambda t, h: (time_of(t), off + h))
    return col


def _vec_spec():
    return pl.BlockSpec((1, HEAD_DIM), lambda t, h: (0, h))


def _lru_fwd(proj, conv_w, conv_b, wa, ba, wx, bx, lam, nw, name):
    s = proj.shape[0]
    tb = min(TIME_BLOCK, s)
    nt = s // tb
    col = _head_specs(tb, lambda t: t)

    def body(x_ref, z_ref, cw_ref, cb_ref, wa_ref, ba_ref, wx_ref, bx_ref, lam_ref, nw_ref,
             y_ref, hs_ref, xbuf, hcar):
        t, h = pl.program_id(0), pl.program_id(1)

        @pl.when(t == 0)
        def _():
            xbuf[h, pl.ds(0, 8), :] = jnp.zeros((8, HEAD_DIM), F32)
            hcar[h] = jnp.zeros((8, HEAD_DIM), F32)

        xbuf[h, pl.ds(8, tb), :] = x_ref[...]
        xc = _conv_taps(xbuf, h, cw_ref[...], tb) + cb_ref[...]
        a, b = _lru_gates(xc, wa_ref[...], ba_ref[...], wx_ref[...], bx_ref[...], lam_ref[...])
        hs_ref[...] = _scan_forward(a, b, hcar[h, pl.ds(0, 1), :])
        hcar[h, pl.ds(0, 1), :] = hs_ref[pl.ds(tb - 1, 1), :]
        xbuf[h, pl.ds(0, 8), :] = xbuf[h, pl.ds(tb, 8), :]
        y_ref[...] = _gated_norm(hs_ref[...], z_ref[...], nw_ref[...]).astype(BF16)

    vec = _vec_spec()
    return pl.pallas_call(
        body, name=name, grid=(nt, HEADS),
        in_specs=[col(COL_LRU_X), col(COL_LRU_Z), pl.BlockSpec((4, HEAD_DIM), lambda t, h: (0, h)), vec,
                  pl.BlockSpec((None, HEAD_DIM, HEAD_DIM), lambda t, h: (h, 0, 0)), vec,
                  pl.BlockSpec((None, HEAD_DIM, HEAD_DIM), lambda t, h: (h, 0, 0)), vec, vec, vec],
        out_specs=[col(0), col(0)],
        out_shape=[jax.ShapeDtypeStruct((s, 2 * D_MODEL), BF16), jax.ShapeDtypeStruct((s, D_MODEL), F32)],
        scratch_shapes=[pltpu.VMEM((HEADS, tb + 8, HEAD_DIM), F32), pltpu.VMEM((HEADS, 8, HEAD_DIM), F32)],
        compiler_params=_params(dimension_semantics=("arbitrary", "arbitrary")),
    )(proj, proj, conv_w, conv_b, wa, ba, wx, bx, lam, nw)


def _halo_spec(tb, nt, off):
    per = tb // 8
    return pl.BlockSpec((8, HEAD_DIM), lambda t, h: (jnp.maximum((nt - 1 - t) * per - 1, 0), off + h))


def _lru_bwd(proj, hs, dy, conv_w, conv_b, wa, ba, wx, bx, lam, nw, name):
    s = proj.shape[0]
    tb = min(TIME_BLOCK, s)
    nt = s // tb
    col = _head_specs(tb, lambda t: nt - 1 - t)

    def body(x_ref, xh_ref, z_ref, hs_ref, hh_ref, dy_ref, cw_ref, cb_ref, wa_ref, ba_ref, wx_ref, bx_ref,
             lam_ref, nw_ref, dx_ref, dz_ref, dcw_ref, dcb_ref, dwa_ref, dba_ref, dwx_ref, dbx_ref, dlam_ref,
             dnw_ref, xbuf, hbuf, dbuf, gcar):
        t, h = pl.program_id(0), pl.program_id(1)
        first_block = t == nt - 1

        @pl.when(t == 0)
        def _():
            dbuf[h, pl.ds(tb, 8), :] = jnp.zeros((8, HEAD_DIM), F32)
            gcar[h] = jnp.zeros((8, HEAD_DIM), F32)
            dcw_ref[h] = jnp.zeros((4, HEAD_DIM), F32)
            dwa_ref[h] = jnp.zeros((HEAD_DIM, HEAD_DIM), F32)
            dwx_ref[h] = jnp.zeros((HEAD_DIM, HEAD_DIM), F32)
            for ref in (dcb_ref, dba_ref, dbx_ref, dlam_ref, dnw_ref):
                ref[h] = jnp.zeros((1, HEAD_DIM), F32)

        keep = jnp.where(first_block, 0.0, 1.0)
        xbuf[0, pl.ds(0, 8), :] = xh_ref[...] * keep
        xbuf[0, pl.ds(8, tb), :] = x_ref[...]
        hbuf[pl.ds(0, 8), :] = hh_ref[...] * keep
        hbuf[pl.ds(8, tb), :] = hs_ref[...]
        cw = cw_ref[...]
        xc = _conv_taps(xbuf, 0, cw, tb) + cb_ref[...]
        (a, _), gates_vjp = jax.vjp(_lru_gates, xc, wa_ref[...], ba_ref[...], wx_ref[...], bx_ref[...], lam_ref[...])
        _, norm_vjp = jax.vjp(_gated_norm, hs_ref[...], z_ref[...], nw_ref[...])
        dh, dz, dnw = norm_vjp(dy_ref[...])
        dz_ref[...] = dz.astype(dz_ref.dtype)
        g = _scan_reverse(a, dh, gcar[h, pl.ds(0, 1), :])
        gcar[h, pl.ds(0, 1), :] = a[0:1, :] * g[0:1, :]
        dxc, dwa, dba, dwx, dbx, dlam = gates_vjp((g * hbuf[pl.ds(7, tb), :], g))
        dx, dcw = _conv_backward(dbuf, h, xbuf, 0, cw, dxc, tb)
        dx_ref[...] = dx.astype(dx_ref.dtype)
        dcw_ref[h] += dcw
        dcb_ref[h] += jnp.sum(dxc, axis=0, keepdims=True)
        dwa_ref[h] += dwa
        dwx_ref[h] += dwx
        dba_ref[h] += dba
        dbx_ref[h] += dbx
        dlam_ref[h] += dlam
        dnw_ref[h] += dnw

    vec = _vec_spec()
    mat = pl.BlockSpec((None, HEAD_DIM, HEAD_DIM), lambda t, h: (h, 0, 0))

    def whole(shape):
        return pl.BlockSpec(shape, lambda t, h: (0,) * len(shape))

    head_vec = jax.ShapeDtypeStruct((HEADS, 1, HEAD_DIM), F32)
    head_mat = jax.ShapeDtypeStruct((HEADS, HEAD_DIM, HEAD_DIM), F32)
    return pl.pallas_call(
        body, name=name, grid=(nt, HEADS),
        in_specs=[col(COL_LRU_X), _halo_spec(tb, nt, COL_LRU_X), col(COL_LRU_Z), col(0), _halo_spec(tb, nt, 0), col(0),
                  pl.BlockSpec((4, HEAD_DIM), lambda t, h: (0, h)), vec, mat, vec, mat, vec, vec, vec],
        out_specs=[col(0), col(0), whole((HEADS, 4, HEAD_DIM)), whole((HEADS, 1, HEAD_DIM)),
                   whole((HEADS, HEAD_DIM, HEAD_DIM)), whole((HEADS, 1, HEAD_DIM)),
                   whole((HEADS, HEAD_DIM, HEAD_DIM)), whole((HEADS, 1, HEAD_DIM)), whole((HEADS, 1, HEAD_DIM)),
                   whole((HEADS, 1, HEAD_DIM))],
        out_shape=[jax.ShapeDtypeStruct((s, D_MODEL), BF16), jax.ShapeDtypeStruct((s, D_MODEL), BF16),
                   jax.ShapeDtypeStruct((HEADS, 4, HEAD_DIM), F32), head_vec, head_mat, head_vec, head_mat, head_vec,
                   head_vec, head_vec],
        scratch_shapes=[pltpu.VMEM((1, tb + 8, HEAD_DIM), F32), pltpu.VMEM((tb + 8, HEAD_DIM), F32),
                        pltpu.VMEM((HEADS, tb + 8, HEAD_DIM), F32), pltpu.VMEM((HEADS, 8, HEAD_DIM), F32)],
        compiler_params=_params(dimension_semantics=("arbitrary", "arbitrary")),
    )(proj, proj, proj, hs, hs, dy, conv_w, conv_b, wa, ba, wx, bx, lam, nw)


def _dn_fwd(proj, y, conv_w, a_log_row, dt_row, nw, name):
    s = proj.shape[0]
    tb = min(TIME_BLOCK, s)
    nt = s // tb
    nchunk = tb // CHUNK
    part = min(PART_CHUNKS, nchunk)
    assert nchunk % part == 0, (nchunk, part)
    col = _head_specs(tb, lambda t: t)

    def body(q_ref, k_ref, v_ref, z_ref, ba_ref, cwq_ref, cwk_ref, cwv_ref, al_ref, dt_ref, nw_ref, y_in_ref,
             y_ref, o_ref, st_ref, xbuf, state):
        t, h = pl.program_id(0), pl.program_id(1)

        @pl.when(t == 0)
        def _():
            for i in range(3):
                xbuf[3 * h + i, pl.ds(0, 8), :] = jnp.zeros((8, HEAD_DIM), F32)
            state[h] = jnp.zeros((HEAD_DIM, HEAD_DIM), F32)

        conv = []
        for i, (ref, cw_ref) in enumerate(((q_ref, cwq_ref), (k_ref, cwk_ref), (v_ref, cwv_ref))):
            xbuf[3 * h + i, pl.ds(8, tb), :] = ref[...]
            conv.append(_conv_taps(xbuf, 3 * h + i, cw_ref[...], tb))
            xbuf[3 * h + i, pl.ds(0, 8), :] = xbuf[3 * h + i, pl.ds(tb, 8), :]
        q, k, v, g, beta = _dn_prep(conv[0], conv[1], conv[2], ba_ref[...], al_ref[...], dt_ref[...], h)
        def chunks(a):
            return a.reshape(nchunk, CHUNK, a.shape[-1])

        qs, ks, vs, gs, bs = chunks(q), chunks(k), chunks(v), chunks(g), chunks(beta)
        carried = [state[h]]
        pending = []

        def tick():
            if pending:
                pending.pop(0)()

        def chain_step(lo, c, eglast, kdec_w_u):
            def step():
                st_ref[lo + c] = carried[0]
                carried[0] = (carried[0] * eglast[c] - _NN_B(kdec_w_u[c, :, :HEAD_DIM], carried[0])
                              + kdec_w_u[c, :, HEAD_DIM:])
            return step

        def outputs(lo, qe, attn_w_u):
            def step():
                o = (_dot(qe - attn_w_u[:, :, :HEAD_DIM], st_ref[pl.ds(lo, part)], B_NN, "bf16")
                     + attn_w_u[:, :, HEAD_DIM:])
                o_ref[pl.ds(lo * CHUNK, part * CHUNK), :] = o.reshape(part * CHUNK, HEAD_DIM)
            return step

        for lo in range(0, nchunk, part):
            sl = slice(lo, lo + part)
            u, w, attn, qe, kdec, eglast = _dn_chunks_head(qs[sl], ks[sl], vs[sl], gs[sl], bs[sl], tick)
            w_u = jnp.concatenate([w, u], axis=2)
            kdec_w_u = _dot(kdec, w_u, B_TN, "bf16")
            tick()
            attn_w_u = _dot(attn, w_u, B_NN, "bf16")
            while pending:
                tick()
            pending += [chain_step(lo, c, eglast, kdec_w_u) for c in range(part)] + [outputs(lo, qe, attn_w_u)]
        while pending:
            tick()
        st = carried[0]
        state[h] = st
        y_ref[...] = _gated_norm(o_ref[...], z_ref[...], nw_ref[...]).astype(BF16)

    def cw_spec(off):
        return pl.BlockSpec((4, HEAD_DIM), lambda t, h: (0, off + h))

    row128 = pl.BlockSpec((1, HEAD_DIM), lambda t, h: (0, 0))
    return pl.pallas_call(
        body, name=name, grid=(nt, HEADS),
        in_specs=[col(COL_Q), col(COL_K), col(COL_V), col(COL_DN_Z),
                  pl.BlockSpec((tb, HEAD_DIM), lambda t, h: (t, COL_BA)),
                  cw_spec(0), cw_spec(HEADS), cw_spec(2 * HEADS), row128, row128, row128,
                  pl.BlockSpec(memory_space=pl.ANY)],
        out_specs=[col(HEADS), col(0), pl.BlockSpec((None, nchunk, HEAD_DIM, HEAD_DIM), lambda t, h: (h, t, 0, 0))],
        out_shape=[jax.ShapeDtypeStruct((s, 2 * D_MODEL), BF16), jax.ShapeDtypeStruct((s, D_MODEL), F32),
                   jax.ShapeDtypeStruct((HEADS, s // CHUNK, HEAD_DIM, HEAD_DIM), F32)],
        input_output_aliases={11: 0},
        scratch_shapes=[pltpu.VMEM((3 * HEADS, tb + 8, HEAD_DIM), F32), pltpu.VMEM((HEADS, HEAD_DIM, HEAD_DIM), F32)],
        compiler_params=_params(dimension_semantics=("arbitrary", "arbitrary")),
    )(proj, proj, proj, proj, proj, conv_w, conv_w, conv_w, a_log_row, dt_row, nw, y)


def _dn_bwd(proj, o, states, dy, conv_w, a_log_row, dt_row, nw, name):
    s = proj.shape[0]
    tb = min(TIME_BLOCK, s)
    nt = s // tb
    nchunk = tb // CHUNK
    part = min(PART_CHUNKS_BWD, nchunk)
    assert nchunk % part == 0, (nchunk, part)
    col = _head_specs(tb, lambda t: nt - 1 - t)

    def body(q_ref, qh_ref, k_ref, kh_ref, v_ref, vh_ref, z_ref, ba_ref, o_ref, st_ref, dy_ref,
             cwq_ref, cwk_ref, cwv_ref, al_ref, dt_ref, nw_ref,
             dq_ref, dk_ref, dv_ref, dz_ref, dba_ref, dcw_ref, dal_ref, ddt_ref, dnw_ref,
             xbuf, dbuf, dstate, dst_s):
        t, h = pl.program_id(0), pl.program_id(1)
        first_block = t == nt - 1

        @pl.when(t == 0)
        def _():
            for i in range(3):
                dbuf[3 * h + i, pl.ds(tb, 8), :] = jnp.zeros((8, HEAD_DIM), F32)
                dcw_ref[3 * h + i] = jnp.zeros((4, HEAD_DIM), F32)
            dstate[h] = jnp.zeros((HEAD_DIM, HEAD_DIM), F32)

        @pl.when((t == 0) & (h == 0))
        def _():
            for ref in (dal_ref, ddt_ref, dnw_ref):
                ref[...] = jnp.zeros_like(ref)

        keep = jnp.where(first_block, 0.0, 1.0)
        cws = (cwq_ref[...], cwk_ref[...], cwv_ref[...])
        conv = []
        for i, (ref, halo) in enumerate(((q_ref, qh_ref), (k_ref, kh_ref), (v_ref, vh_ref))):
            xbuf[i, pl.ds(0, 8), :] = halo[...] * keep
            xbuf[i, pl.ds(8, tb), :] = ref[...]
            conv.append(_conv_taps(xbuf, i, cws[i], tb))
        (q, k, v, g, beta), prep_vjp = jax.vjp(
            lambda qc, kc, vc, ba, al, dt: _dn_prep(qc, kc, vc, ba, al, dt, h),
            conv[0], conv[1], conv[2], ba_ref[...], al_ref[...], dt_ref[...])
        _, norm_vjp = jax.vjp(_gated_norm, o_ref[...], z_ref[...], nw_ref[...])
        do, dz, dnw = norm_vjp(dy_ref[...])
        dz_ref[...] = dz.astype(dz_ref.dtype)
        dnw_ref[...] += dnw

        def chunks(a):
            return a.reshape(nchunk, CHUNK, a.shape[-1])

        do, qs, ks, vs, gs, bs = chunks(do), chunks(q), chunks(k), chunks(v), chunks(g), chunks(beta)
        dst = dstate[h]
        cotangents = []
        for lo in reversed(range(0, nchunk, part)):
            sl = slice(lo, lo + part)
            _, chunks_vjp, (w, attn, qe, kdec, eglast) = jax.vjp(
                _dn_chunks, qs[sl], ks[sl], vs[sl], gs[sl], bs[sl], st_ref[pl.ds(lo, part)], has_aux=True)
            kdec_w = _dot(kdec, w, B_TN, "bf16")
            fixed = _dot(qe, do[sl], B_TN, "bf16") - _dot(w, _dot(attn, do[sl], B_TN, "bf16"), B_TN, "bf16")
            for c in reversed(range(part)):
                dst_s[lo + c] = dst
                dst = dst * eglast[c] - _dot(kdec_w[c], dst, TN, "bf16") + fixed[c]
            cotangents.insert(0, chunks_vjp((do[sl], dst_s[pl.ds(lo, part)])))
        dstate[h] = dst
        dq, dk, dv, dg, db = [jnp.concatenate([ct[i] for ct in cotangents], axis=0) for i in range(5)]

        def rows(a):
            return a.reshape(tb, a.shape[-1])

        dqc, dkc, dvc, dba, dal, ddt = prep_vjp((rows(dq), rows(dk), rows(dv), rows(dg), rows(db)))
        for i, (dxc, out) in enumerate(((dqc, dq_ref), (dkc, dk_ref), (dvc, dv_ref))):
            dx, dcw = _conv_backward(dbuf, 3 * h + i, xbuf, i, cws[i], dxc, tb)
            out[...] = dx.astype(out.dtype)
            dcw_ref[3 * h + i] += dcw
        dal_ref[...] += dal
        ddt_ref[...] += ddt

        @pl.when(h == 0)
        def _():
            dba_ref[...] = dba.astype(dba_ref.dtype)

        @pl.when(h > 0)
        def _():
            dba_ref[...] += dba.astype(dba_ref.dtype)

    def cw_spec(off):
        return pl.BlockSpec((4, HEAD_DIM), lambda t, h: (0, off + h))

    def whole(shape):
        return pl.BlockSpec(shape, lambda t, h: (0,) * len(shape))

    row128 = whole((1, HEAD_DIM))
    blk = (tb, HEAD_DIM)
    act = jax.ShapeDtypeStruct((s, D_MODEL), BF16)
    row_out = jax.ShapeDtypeStruct((1, HEAD_DIM), F32)
    return pl.pallas_call(
        body, name=name, grid=(nt, HEADS),
        in_specs=[col(COL_Q), _halo_spec(tb, nt, COL_Q), col(COL_K), _halo_spec(tb, nt, COL_K),
                  col(COL_V), _halo_spec(tb, nt, COL_V), col(COL_DN_Z),
                  pl.BlockSpec(blk, lambda t, h: (nt - 1 - t, COL_BA)), col(0),
                  pl.BlockSpec((None, nchunk, HEAD_DIM, HEAD_DIM), lambda t, h: (h, nt - 1 - t, 0, 0)), col(HEADS),
                  cw_spec(0), cw_spec(HEADS), cw_spec(2 * HEADS), row128, row128, row128],
        out_specs=[col(0), col(0), col(0), col(0), pl.BlockSpec(blk, lambda t, h: (nt - 1 - t, 0)),
                   whole((3 * HEADS, 4, HEAD_DIM)), row128, row128, row128],
        out_shape=[act, act, act, act, jax.ShapeDtypeStruct((s, HEAD_DIM), F32),
                   jax.ShapeDtypeStruct((3 * HEADS, 4, HEAD_DIM), F32), row_out, row_out, row_out],
        scratch_shapes=[pltpu.VMEM((3, tb + 8, HEAD_DIM), F32), pltpu.VMEM((3 * HEADS, tb + 8, HEAD_DIM), F32),
                        pltpu.VMEM((HEADS, HEAD_DIM, HEAD_DIM), F32), pltpu.VMEM((nchunk, HEAD_DIM, HEAD_DIM), F32)],
        compiler_params=_params(dimension_semantics=("arbitrary", "arbitrary")),
    )(proj, proj, proj, proj, proj, proj, proj, proj, o, states, dy, conv_w, conv_w, conv_w, a_log_row, dt_row, nw)


def _mesh_position():
    x, y, c = lax.axis_index("x"), lax.axis_index("y"), lax.axis_index("c")
    return x, y, c, 4 * x + 2 * y + c


def _peer(k, x, y, c):
    px = 1 - x if k & 4 else x
    py = 1 - y if k & 2 else y
    pc = 1 - c if k & 1 else c
    return (px, py, pc), 4 * px + 2 * py + pc


def _exchange_copies(ins, lands, scatter, send_sems, recv_sems, receives=True):
    x, y, c, me = _mesh_position()
    sends, recvs = [], []
    for i, (src, land) in enumerate(zip(ins, lands)):
        for k in range(1, N_DEV):
            peer, peer_id = _peer(k, x, y, c)
            sem = i * (N_DEV - 1) + k - 1
            for dst, out in ((me, sends), (peer_id, recvs)) if receives else ((me, sends),):
                out.append(pltpu.make_async_remote_copy(
                    src_ref=src.at[peer_id] if scatter[i] else src, dst_ref=land.at[dst],
                    send_sem=send_sems.at[sem], recv_sem=recv_sems.at[sem],
                    device_id=peer, device_id_type=pl.DeviceIdType.MESH))
    return sends, recvs


def _landing_shape(a, scatter):
    return a.shape if scatter else (N_DEV,) + a.shape


def _direct_exchange(arrays, scatter, name):
    n = len(arrays)
    out_shapes = [jax.ShapeDtypeStruct(_landing_shape(a, sc), a.dtype) for a, sc in zip(arrays, scatter)]

    def body(*refs):
        ins, outs = refs[:n], refs[n:2 * n]
        send_sems, recv_sems, local_sems = refs[2 * n:]
        me = _mesh_position()[3]
        local = [pltpu.make_async_copy(ins[i].at[me] if scatter[i] else ins[i], outs[i].at[me], local_sems.at[i])
                 for i in range(n)]
        sends, recvs = _exchange_copies(ins, outs, scatter, send_sems, recv_sems)
        for cp in local + sends:
            cp.start()
        for cp in recvs:
            cp.wait_recv()
        for cp in sends:
            cp.wait_send()
        for cp in local:
            cp.wait()

    hbm = pl.BlockSpec(memory_space=pl.ANY)
    return pl.pallas_call(
        body, name=name, in_specs=[hbm] * n, out_specs=[hbm] * n, out_shape=out_shapes,
        scratch_shapes=[pltpu.SemaphoreType.DMA((n * (N_DEV - 1),)), pltpu.SemaphoreType.DMA((n * (N_DEV - 1),)),
                        pltpu.SemaphoreType.DMA((n,))],
    )(*arrays)


def _two_level_gather(arrays, name):
    n = len(arrays)
    per = N_DEV - 1

    def body(*refs):
        ins, outs = refs[:n], refs[n:2 * n]
        send_sems, recv_sems, local_sems = refs[2 * n:]
        x, y, c, me = _mesh_position()
        sibling = (x, y, 1 - c)
        chips = [(1 - x, y), (x, 1 - y), (1 - x, 1 - y)]

        def copy(i, k, block, to, src=None):
            slot = outs[i].at[4 * block[0] + 2 * block[1] + block[2]]
            return pltpu.make_async_remote_copy(
                src_ref=slot if src is None else src, dst_ref=slot,
                send_sem=send_sems.at[i * per + k], recv_sem=recv_sems.at[i * per + k],
                device_id=to, device_id_type=pl.DeviceIdType.MESH)

        local = [pltpu.make_async_copy(ins[i], outs[i].at[me], local_sems.at[i]) for i in range(n)]
        first = []
        for i in range(n):
            first.append(copy(i, 0, (x, y, c), sibling, src=ins[i]))
            first += [copy(i, 1 + j, (x, y, c), (*chip, c), src=ins[i]) for j, chip in enumerate(chips)]
        for cp in local + first:
            cp.start()
        passed = []
        for i in range(n):
            for j, chip in enumerate(chips):
                copy(i, 1 + j, (*chip, c), (x, y, c)).wait_recv()
                passed.append(copy(i, 4 + j, (*chip, c), sibling))
                passed[-1].start()
        for i in range(n):
            copy(i, 0, sibling, (x, y, c)).wait_recv()
            for j, chip in enumerate(chips):
                copy(i, 4 + j, (*chip, 1 - c), (x, y, c)).wait_recv()
        for cp in first + passed:
            cp.wait_send()
        for cp in local:
            cp.wait()

    hbm = pl.BlockSpec(memory_space=pl.ANY)
    return pl.pallas_call(
        body, name=name, in_specs=[hbm] * n, out_specs=[hbm] * n,
        out_shape=[jax.ShapeDtypeStruct((N_DEV,) + a.shape, a.dtype) for a in arrays],
        scratch_shapes=[pltpu.SemaphoreType.DMA((n * per,)), pltpu.SemaphoreType.DMA((n * per,)),
                        pltpu.SemaphoreType.DMA((n,))],
    )(*arrays)


_HBM = pl.BlockSpec(memory_space=pltpu.HBM)
_SEM = pl.BlockSpec(memory_space=pltpu.SEMAPHORE)
_DATAFLOW = pltpu.SideEffectType.DATAFLOW_SIDE_EFFECTING


def _exchange_start(arrays, scatter, name):
    n = len(arrays)
    srcs = [pltpu.with_memory_space_constraint(a, pltpu.HBM) for a in arrays]
    lands = [pltpu.with_memory_space_constraint(lax.empty(_landing_shape(a, sc), a.dtype), pltpu.HBM)
             for a, sc in zip(arrays, scatter)]
    nsem = n * (N_DEV - 1)

    def body(*refs):
        ins, zones = refs[:n], refs[n:2 * n]
        send_sems, recv_sems = refs[2 * n], refs[2 * n + 1]
        token = refs[-1]
        sends, _ = _exchange_copies(ins, zones, scatter, send_sems, recv_sems, receives=False)
        for cp in sends:
            cp.start()
        token[...] = jnp.zeros_like(token)

    res = pl.pallas_call(
        body, name=name,
        out_shape=(pltpu.SemaphoreType.DMA((nsem,)), pltpu.SemaphoreType.DMA((nsem,)),
                   *[pltpu.HBM(a.shape, a.dtype) for a in srcs + lands], jax.ShapeDtypeStruct((8, HEAD_DIM), F32)),
        in_specs=[_HBM] * (2 * n),
        out_specs=(_SEM, _SEM, *[_HBM] * (2 * n), pl.BlockSpec(memory_space=pltpu.VMEM)),
        input_output_aliases={i: 2 + i for i in range(2 * n)},
        compiler_params=pltpu.CompilerParams(has_side_effects=_DATAFLOW),
    )(*srcs, *lands)
    return dict(sems=res[:2], srcs=res[2:2 + n], lands=res[2 + n:2 + 2 * n], token_block=res[-1],
                token=res[-1][0, 0], scatter=scatter)


def _exchange_wait(started, after, name):
    scatter = started["scatter"]
    n = len(scatter)

    def body(*refs):
        ins, zones = refs[:n], refs[n:2 * n]
        send_sems, recv_sems = refs[2 * n], refs[2 * n + 1]
        sends, recvs = _exchange_copies(ins, zones, scatter, send_sems, recv_sems)
        for cp in sends:
            cp.wait_send()
        for cp in recvs:
            cp.wait_recv()

    thru = list(started["srcs"]) + list(started["lands"])
    res = pl.pallas_call(
        body, name=name, out_shape=[pltpu.HBM(a.shape, a.dtype) for a in thru],
        in_specs=[_HBM] * (2 * n) + [_SEM, _SEM, pl.BlockSpec(memory_space=pl.ANY)], out_specs=[_HBM] * (2 * n),
        input_output_aliases={i: i for i in range(2 * n)},
        compiler_params=pltpu.CompilerParams(has_side_effects=_DATAFLOW),
    )(*thru, *started["sems"], after)
    me = 4 * lax.axis_index("x") + 2 * lax.axis_index("y") + lax.axis_index("c")
    out = []
    for src, got, sc in zip(res[:n], res[n:], scatter):
        own = lax.dynamic_index_in_dim(src, me, 0, keepdims=False) if sc else src
        out.append(lax.dynamic_update_index_in_dim(got, own, me, 0))
    return out


def _adamw(parts, w, m, v, name, rows_per_step, row_offset=0, into=None):
    rows, cols = parts.shape[1:]
    tr = min(rows_per_step, rows)
    assert rows % tr == 0 and row_offset % tr == 0, (name, rows, tr, row_offset)
    first = row_offset // tr
    c1 = 1.0 / (1.0 - ADAM_B1 ** ADAM_STEP)
    c2 = 1.0 / (1.0 - ADAM_B2 ** ADAM_STEP)

    def body(p_ref, w_ref, m_ref, v_ref, *rest):
        g_ref, d_ref, nm_ref, nv_ref = rest[-4:]
        g = p_ref[0].astype(F32)
        for d in range(1, N_DEV):
            g = g + p_ref[d].astype(F32)
        nm = ADAM_B1 * m_ref[...] + (1.0 - ADAM_B1) * g
        nv = ADAM_B2 * v_ref[...] + (1.0 - ADAM_B2) * (g * g)
        g_ref[...] = g
        nm_ref[...] = nm
        nv_ref[...] = nv
        d_ref[...] = -ADAM_LR * ((nm * c1) / (jnp.sqrt(nv * c2) + ADAM_EPS) + ADAM_WD * w_ref[...])

    blk = pl.BlockSpec((tr, cols), lambda i: (i + first, 0))
    shape = jax.ShapeDtypeStruct(w.shape, F32)
    prior = [] if into is None else list(into)
    return pl.pallas_call(
        body, name=name, grid=(rows // tr,),
        in_specs=[pl.BlockSpec((N_DEV, tr, cols), lambda i: (0, i, 0)), blk, blk, blk]
        + [pl.BlockSpec(memory_space=pl.ANY)] * len(prior),
        out_specs=[blk] * 4, out_shape=[shape] * 4,
        input_output_aliases={4 + j: j for j in range(len(prior))}, compiler_params=_params(),
    )(parts, w, m, v, *prior)


_LAYERED = ("norm_w", "lru_conv_b", "lru_wa", "lru_ba", "lru_wx", "lru_bx", "lru_lambda", "lru_norm_w",
            "dn_A_log", "dn_dt_bias", "dn_norm_w")
_PACK_LRU = _LAYERED[1:8]
_PACK_LAST = _LAYERED[:1] + _LAYERED[8:]
_WEIGHTS = ("norm_w", "w_in", "lru_conv_w", "lru_conv_b", "lru_wa", "lru_ba", "lru_wx", "lru_bx", "lru_lambda",
            "lru_norm_w", "dn_conv_w", "dn_A_log", "dn_dt_bias", "dn_norm_w", "w_out", "final_norm_w")


def _pack_layer(tree, layer, tail=(), names=_LAYERED):
    rows = []
    for name in names:
        a = tree[name][layer]
        if a.shape[-1] == HEADS:
            a = jnp.pad(a, (0, HEAD_DIM - HEADS))
        rows.append(a.reshape(-1, HEAD_DIM))
    rows += [t.reshape(-1, HEAD_DIM) for t in tail]
    packed = jnp.concatenate(rows, axis=0)
    return jnp.pad(packed, ((0, (-packed.shape[0]) % 8), (0, 0)))


def _unpack_layer(packed, like, names=_LAYERED):
    out, at = {}, 0
    for name in names:
        shape = like[name].shape[1:]
        if shape[-1] == HEADS:
            n = 1
            out[name] = packed[at, :HEADS]
        else:
            n = like[name][0].size // HEAD_DIM
            out[name] = packed[at:at + n].reshape(shape)
        at += n
    return out, at


def _heads_to_channels(a):
    return jnp.transpose(a, (1, 0, 2)).reshape(a.shape[1], HEADS * HEAD_DIM)


def kernel(x, norm_w, w_in, lru_conv_w, lru_conv_b, lru_wa, lru_ba, lru_wx, lru_bx, lru_lambda, lru_norm_w, dn_conv_w, dn_A_log, dn_dt_bias, dn_norm_w, w_out, final_norm_w, loss_target, m_norm_w, m_w_in, m_lru_conv_w, m_lru_conv_b, m_lru_wa, m_lru_ba, m_lru_wx, m_lru_bx, m_lru_lambda, m_lru_norm_w, m_dn_conv_w, m_dn_A_log, m_dn_dt_bias, m_dn_norm_w, m_w_out, m_final_norm_w, v_norm_w, v_w_in, v_lru_conv_w, v_lru_conv_b, v_lru_wa, v_lru_ba, v_lru_wx, v_lru_bx, v_lru_lambda, v_lru_norm_w, v_dn_conv_w, v_dn_A_log, v_dn_dt_bias, v_dn_norm_w, v_w_out, v_final_norm_w):
    weights = dict(norm_w=norm_w, w_in=w_in, lru_conv_w=lru_conv_w, lru_conv_b=lru_conv_b, lru_wa=lru_wa,
                   lru_ba=lru_ba, lru_wx=lru_wx, lru_bx=lru_bx, lru_lambda=lru_lambda, lru_norm_w=lru_norm_w,
                   dn_conv_w=dn_conv_w, dn_A_log=dn_A_log, dn_dt_bias=dn_dt_bias, dn_norm_w=dn_norm_w,
                   w_out=w_out, final_norm_w=final_norm_w)
    mom_m = dict(norm_w=m_norm_w, w_in=m_w_in, lru_conv_w=m_lru_conv_w, lru_conv_b=m_lru_conv_b, lru_wa=m_lru_wa,
                 lru_ba=m_lru_ba, lru_wx=m_lru_wx, lru_bx=m_lru_bx, lru_lambda=m_lru_lambda,
                 lru_norm_w=m_lru_norm_w, dn_conv_w=m_dn_conv_w, dn_A_log=m_dn_A_log, dn_dt_bias=m_dn_dt_bias,
                 dn_norm_w=m_dn_norm_w, w_out=m_w_out, final_norm_w=m_final_norm_w)
    mom_v = dict(norm_w=v_norm_w, w_in=v_w_in, lru_conv_w=v_lru_conv_w, lru_conv_b=v_lru_conv_b, lru_wa=v_lru_wa,
                 lru_ba=v_lru_ba, lru_wx=v_lru_wx, lru_bx=v_lru_bx, lru_lambda=v_lru_lambda,
                 lru_norm_w=v_lru_norm_w, dn_conv_w=v_dn_conv_w, dn_A_log=v_dn_A_log, dn_dt_bias=v_dn_dt_bias,
                 dn_norm_w=v_dn_norm_w, w_out=v_w_out, final_norm_w=v_final_norm_w)
    depth = norm_w.shape[0]
    xs = x[0]
    s = xs.shape[0]
    tm = min(1024, s)

    assert depth >= 2, depth

    def row(a):
        return a.reshape(1, -1)

    def pad_row(a):
        return jnp.pad(a, (0, HEAD_DIM - a.shape[0])).reshape(1, HEAD_DIM)

    def full_w_in(g):
        w = jnp.transpose(g, (1, 2, 0, 3)).reshape(g.shape[1], D_MODEL, D_IN)
        return jnp.pad(w, ((0, 0), (0, 0), (0, D_IN_PAD - D_IN)))

    g_win0, g_lcw, g_dcw = _two_level_gather([w_in[:1].astype(BF16), lru_conv_w, dn_conv_w], "gather_first")
    rest = _exchange_start([w_in[1:].astype(BF16), w_out.astype(BF16)], [False] * 2, "gather_rest_start")
    win = [full_w_in(g_win0)[0]]
    wout = None
    lcw = jnp.transpose(g_lcw, (1, 2, 0, 3)).reshape(depth, 4, D_MODEL)
    dcw = jnp.transpose(g_dcw, (1, 2, 0, 3)).reshape(depth, 4, 3 * D_MODEL)

    saved = []
    cur = xs
    for l in range(depth):
        nw_row = row(norm_w[l]) + rest["token"] if l == 0 else row(norm_w[l])
        hn = _rmsnorm_fwd(cur, nw_row, f"norm_fwd_{l}")
        proj = _matmul(hn, win[l], "nn", tm, 896, D_MODEL, f"in_proj_{l}")
        y_lru, hs = _lru_fwd(proj, lcw[l], row(lru_conv_b[l]), lru_wa[l], row(lru_ba[l]), lru_wx[l], row(lru_bx[l]),
                             row(lru_lambda[l]), row(lru_norm_w[l]), f"lru_fwd_{l}")
        ycat, o_dn, states = _dn_fwd(proj, y_lru, dcw[l], pad_row(dn_A_log[l]), pad_row(dn_dt_bias[l]),
                                     row(dn_norm_w[l]), f"dn_fwd_{l}")
        if l == 0:
            g_win_rest, g_wout = _exchange_wait(rest, ycat, "gather_rest_wait")
            win += list(full_w_in(g_win_rest))
            wout = jnp.transpose(g_wout, (1, 0, 2, 3)).reshape(depth, 2 * D_MODEL, D_MODEL)
        nxt = _matmul(ycat, wout[l], "nn", tm, D_MODEL, 2 * D_MODEL, f"out_proj_{l}", add=cur)
        saved.append((cur, hn, proj, hs, o_dn, states, ycat))
        cur = nxt
    loss_part, dx, d_final = _final_loss(cur, row(final_norm_w), loss_target[0], "final_loss")

    def win_slots(g):
        return jnp.transpose(g.reshape(D_MODEL, N_DEV, D_IN // N_DEV), (1, 0, 2))

    def wout_slots(g):
        return g.reshape(N_DEV, 2 * D_MODEL // N_DEV, D_MODEL)

    grads = {k: [None] * depth for k in _WEIGHTS if k not in ("final_norm_w", "w_in", "w_out")}
    started = {}
    token = None
    for l in reversed(range(depth)):
        x_in, hn, proj, hs, o_dn, states, ycat = saved[l]
        dy = _matmul(dx, wout[l], "nt", tm, D_MODEL, D_MODEL, f"out_proj_dy_{l}")
        g_wout_l = _matmul(ycat, dx, "tn", D_MODEL, D_MODEL, tm, f"out_proj_dw_{l}", out_dtype=BF16)
        if l == 0:
            started["w_out_0"] = _exchange_start([wout_slots(g_wout_l)], [True], "exchange_w_out_0_start")
            token = token + started["w_out_0"]["token"]
        cb_row = row(lru_conv_b[l]) if token is None else row(lru_conv_b[l]) + token
        (dlx, dlz, g_lcw, g_lcb, g_wa, g_ba, g_wx, g_bx, g_lam, g_lnw) = _lru_bwd(
            proj, hs, dy, lcw[l], cb_row, lru_wa[l], row(lru_ba[l]), lru_wx[l], row(lru_bx[l]),
            row(lru_lambda[l]), row(lru_norm_w[l]), f"lru_bwd_{l}")
        grads["lru_conv_w"][l] = _heads_to_channels(g_lcw)
        grads["lru_conv_b"][l] = g_lcb.reshape(D_MODEL)
        grads["lru_wa"][l] = g_wa
        grads["lru_ba"][l] = g_ba.reshape(D_MODEL)
        grads["lru_wx"][l] = g_wx
        grads["lru_bx"][l] = g_bx.reshape(D_MODEL)
        grads["lru_lambda"][l] = g_lam.reshape(D_MODEL)
        grads["lru_norm_w"][l] = g_lnw.reshape(D_MODEL)
        al_row = pad_row(dn_A_log[l])
        if l == 0:
            started["pack_0"] = _exchange_start([_pack_layer(grads, 0, names=_PACK_LRU)], [False],
                                                "exchange_pack_0_start")
            al_row = al_row + started["pack_0"]["token"]
        (dq, dk, dv, ddz, dba, g_dcw3, g_al, g_dt, g_dnw) = _dn_bwd(
            proj, o_dn, states, dy, dcw[l], al_row, pad_row(dn_dt_bias[l]), row(dn_norm_w[l]), f"dn_bwd_{l}")
        g_dcw3 = g_dcw3.reshape(HEADS, 3, 4, HEAD_DIM)
        grads["dn_conv_w"][l] = jnp.concatenate([_heads_to_channels(g_dcw3[:, i]) for i in range(3)], axis=1)
        grads["dn_A_log"][l] = g_al[0, :HEADS]
        grads["dn_dt_bias"][l] = g_dt[0, :HEADS]
        grads["dn_norm_w"][l] = g_dnw.reshape(HEAD_DIM)
        dep = None
        pieces = [dlx, dlz, dq, dk, dv, ddz]
        wide = len(pieces) * D_MODEL
        dba = dba.astype(BF16)
        g_win_l = jnp.concatenate(
            [_matmul_tn_parts(hn, pieces, D_MODEL, D_MODEL, tm, f"in_proj_dw_{l}", BF16),
             _matmul(hn, dba, "tn", D_MODEL, HEAD_DIM, tm, f"in_proj_dw_gates_{l}", out_dtype=BF16)[:, :D_IN - wide]],
            axis=1)
        if l == 0:
            started[0] = _exchange_start([win_slots(g_win_l)], [True], "exchange_0_start")
            dep = started[0]["token_block"]
        dh = _matmul(dba, win[l][:, wide:], "nt", tm, D_MODEL, HEAD_DIM, f"in_proj_dh_gates_{l}", dep=dep)
        dh = _matmul_nt_parts(pieces, win[l], min(tm, 512), D_MODEL, f"in_proj_dh_{l}", add=dh)
        dx, g_nw = _rmsnorm_bwd(x_in, row(norm_w[l]), dh, dx, f"norm_bwd_{l}")
        grads["norm_w"][l] = g_nw.reshape(D_MODEL)
        if l > 0:
            tail = (d_final, loss_part) if l == depth - 1 else ()
            started[l] = _exchange_start([win_slots(g_win_l), wout_slots(g_wout_l), _pack_layer(grads, l, tail)],
                                         [True, True, False], f"exchange_{l}_start")
            token = started[l]["token"]

    def conv_slots(a):
        dd, r, cc = a.shape
        return jnp.transpose(a.reshape(dd, r, N_DEV, cc // N_DEV), (2, 0, 1, 3))

    small = _exchange_start(
        [conv_slots(jnp.stack(grads["lru_conv_w"])), conv_slots(jnp.stack(grads["dn_conv_w"])),
         _pack_layer(grads, 0, names=_PACK_LAST)], [True, True, False], "exchange_small_start")

    new = {}
    flat_in = (depth * D_MODEL, D_IN // N_DEV)
    flat_out = (depth * 2 * D_MODEL // N_DEV, D_MODEL)
    zero_row = jnp.zeros((1, HEAD_DIM), F32)

    def adamw_pack(parts, layer, names=_LAYERED, name="adamw_small"):
        tails = [(t, zero_row) if layer == depth - 1 else () for t in (final_norm_w, m_final_norm_w, v_final_norm_w)]
        return _adamw(parts, _pack_layer(weights, layer, tails[0], names), _pack_layer(mom_m, layer, tails[1], names),
                      _pack_layer(mom_v, layer, tails[2], names), f"{name}_{layer}", parts.shape[1])

    def adamw_w_in(parts, layer, into):
        return _adamw(parts, w_in.reshape(flat_in), m_w_in.reshape(flat_in), v_w_in.reshape(flat_in),
                      f"adamw_w_in_{layer}", 256, layer * D_MODEL, into)

    def adamw_w_out(parts, layer, into):
        return _adamw(parts, w_out.reshape(flat_out), m_w_out.reshape(flat_out), v_w_out.reshape(flat_out),
                      f"adamw_w_out_{layer}", 256, layer * flat_out[0] // depth, into)

    acc_in = acc_out = None
    packs = [None] * depth
    after = small["token_block"]
    for l in reversed(range(1, depth)):
        r_win, r_wout, r_pack = _exchange_wait(started[l], after, f"exchange_{l}_wait")
        acc_in = adamw_w_in(r_win, l, acc_in)
        acc_out = adamw_w_out(r_wout, l, acc_out)
        packs[l] = adamw_pack(r_pack, l)
        after = packs[l][0]
    (r_wout,) = _exchange_wait(started["w_out_0"], after, "exchange_w_out_0_wait")
    acc_out = adamw_w_out(r_wout, 0, acc_out)
    (r_win,) = _exchange_wait(started[0], acc_out[0], "exchange_0_wait")
    acc_in = adamw_w_in(r_win, 0, acc_in)
    (r_pack,) = _exchange_wait(started["pack_0"], acc_in[0], "exchange_pack_0_wait")
    packs[0] = adamw_pack(r_pack, 0, _PACK_LRU)
    r_lcw, r_dcw, r_last = _exchange_wait(small, packs[0][0], "exchange_small_wait")
    for name, parts in (("lru_conv_w", r_lcw), ("dn_conv_w", r_dcw)):
        w = weights[name]
        flat = (-1, w.shape[-1])
        outs = _adamw(parts.reshape((N_DEV,) + (w.size // w.shape[-1], w.shape[-1])), w.reshape(flat),
                      mom_m[name].reshape(flat), mom_v[name].reshape(flat), f"adamw_{name}", 8)
        new[name] = [a.reshape(w.shape) for a in outs]
    last_0 = adamw_pack(r_last, 0, _PACK_LAST, "adamw_last")
    new["w_in"] = [a.reshape(w_in.shape) for a in acc_in]
    new["w_out"] = [a.reshape(w_out.shape) for a in acc_out]
    for i in range(4):
        layers = [{**_unpack_layer(packs[0][i], weights, _PACK_LRU)[0],
                   **_unpack_layer(last_0[i], weights, _PACK_LAST)[0]}]
        layers += [_unpack_layer(packs[l][i], weights)[0] for l in range(1, depth)]
        for name in _LAYERED:
            new.setdefault(name, []).append(jnp.stack([layer[name] for layer in layers]))
    tail_at = _unpack_layer(packs[depth - 1][0], weights)[1]
    rows_final = D_MODEL // HEAD_DIM
    new["final_norm_w"] = [packs[depth - 1][i][tail_at:tail_at + rows_final].reshape(D_MODEL) for i in range(4)]
    loss = packs[depth - 1][0][tail_at + rows_final, 0]
    out = [loss, dx.reshape(x.shape)]
    for i in range(4):
        out += [new[name][i] for name in _WEIGHTS]
    return tuple(out)
```

```python
import functools

import jax
import jax.numpy as jnp
from jax import lax
from jax.experimental import pallas as pl
from jax.experimental.pallas import tpu as pltpu

F32 = jnp.float32
BF16 = jnp.bfloat16

N_DEV = 8
D_MODEL = 1024
HEADS = 8
HEAD_DIM = 128
CHUNK = 64
D_IN = 6160
D_IN_PAD = 6272
COL_LRU_X, COL_LRU_Z, COL_Q, COL_K, COL_V, COL_DN_Z, COL_BA = 0, 8, 16, 24, 32, 40, 48
LRU_C = 8.0
EPS = 1e-6
ADAM_LR, ADAM_B1, ADAM_B2, ADAM_EPS, ADAM_WD, ADAM_STEP = 0.001, 0.9, 0.999, 1e-08, 0.01, 10
TIME_BLOCK = 1024
PART_CHUNKS = 16
PART_CHUNKS_BWD = 16
VMEM_LIMIT = 56 * 1024 * 1024

NN = (((1,), (0,)), ((), ()))
NT = (((1,), (1,)), ((), ()))
TN = (((0,), (0,)), ((), ()))


B_NN = (((2,), (1,)), ((0,), (0,)))
B_NT = (((2,), (2,)), ((0,), (0,)))
B_TN = (((1,), (1,)), ((0,), (0,)))


def _split_bf16(x):
    hi = x.astype(BF16)
    return hi, (x - hi.astype(F32)).astype(BF16)


def _dot(a, b, dims, prec):
    if prec == "bf16":
        return lax.dot_general(a.astype(BF16), b.astype(BF16), dims, preferred_element_type=F32)
    a1, a2 = _split_bf16(a)
    b1, b2 = _split_bf16(b)
    dg = functools.partial(lax.dot_general, dimension_numbers=dims, preferred_element_type=F32)
    return dg(a1, b1) + (dg(a1, b2) + dg(a2, b1))


def _make_mm(prec, nn_dims, nt_dims, tn_dims):
    @jax.custom_vjp
    def nn(a, b):
        return _dot(a, b, nn_dims, prec)

    @jax.custom_vjp
    def nt(a, b):
        return _dot(a, b, nt_dims, prec)

    @jax.custom_vjp
    def tn(a, b):
        return _dot(a, b, tn_dims, prec)

    nn.defvjp(lambda a, b: (_dot(a, b, nn_dims, prec), (a, b)),
              lambda r, g: (_dot(g, r[1], nt_dims, prec), _dot(r[0], g, tn_dims, prec)))
    nt.defvjp(lambda a, b: (_dot(a, b, nt_dims, prec), (a, b)),
              lambda r, g: (_dot(g, r[1], nn_dims, prec), _dot(g, r[0], tn_dims, prec)))
    tn.defvjp(lambda a, b: (_dot(a, b, tn_dims, prec), (a, b)),
              lambda r, g: (_dot(r[1], g, nt_dims, prec), _dot(r[0], g, nn_dims, prec)))
    return nn, nt, tn


_NN_B, _NT_B, _TN_B = _make_mm("bf16", NN, NT, TN)
_BNN, _BNT, _BTN = _make_mm("bf16", B_NN, B_NT, B_TN)


def _unit_lower_inverse_steps(a, tick=lambda: None):
    n = a.shape[-1]
    eye = (lax.broadcasted_iota(jnp.int32, a.shape, 1) == lax.broadcasted_iota(jnp.int32, a.shape, 2)).astype(F32)
    dg = functools.partial(lax.dot_general, dimension_numbers=B_NN, preferred_element_type=F32)
    inv = eye - a
    pw = _dot(a, a, B_NN, "bf16x3")
    steps = n.bit_length() - 2
    for j in range(steps):
        i1, i2 = _split_bf16(inv)
        p1, p2 = _split_bf16(pw)
        square = j + 1 < steps
        by_hi = dg(jnp.concatenate([i1, i2, p1, p2] if square else [i1, i2], axis=1), p1)
        by_lo = dg(jnp.concatenate([i1, p1], axis=1) if square else i1, p2)
        inv = inv + (by_hi[:, :n] + (by_lo[:, :n] + by_hi[:, n:2 * n]))
        if square:
            pw = by_hi[:, 2 * n:3 * n] + (by_lo[:, n:] + by_hi[:, 3 * n:])
        tick()
    return inv


@jax.custom_vjp
def _unit_lower_inverse(a):
    return _unit_lower_inverse_steps(a)


def _uli_fwd(a):
    inv = _unit_lower_inverse(a)
    return inv, inv


def _uli_bwd(inv, g):
    return (-_dot(_dot(inv, g, B_TN, "bf16"), inv, B_NT, "bf16"),)


_unit_lower_inverse.defvjp(_uli_fwd, _uli_bwd)


def _rows2(y, m):
    return y[:, :m], y[:, m:]


@jax.custom_vjp
def _pair_nn(x1, x2, r):
    return _rows2(_dot(jnp.concatenate([x1, x2], axis=1), r, B_NN, "bf16"), x1.shape[1])


def _pair_nn_bwd(res, g):
    x1, x2, r = res
    g = jnp.concatenate(g, axis=1)
    dx1, dx2 = _rows2(_dot(g, r, B_NT, "bf16"), x1.shape[1])
    return dx1, dx2, _dot(jnp.concatenate([x1, x2], axis=1), g, B_TN, "bf16")


_pair_nn.defvjp(lambda x1, x2, r: (_pair_nn(x1, x2, r), (x1, x2, r)), _pair_nn_bwd)


@jax.custom_vjp
def _pair_nt(x1, x2, r):
    return _rows2(_dot(jnp.concatenate([x1, x2], axis=1), r, B_NT, "bf16"), x1.shape[1])


def _pair_nt_bwd(res, g):
    x1, x2, r = res
    g = jnp.concatenate(g, axis=1)
    dx1, dx2 = _rows2(_dot(g, r, B_NN, "bf16"), x1.shape[1])
    return dx1, dx2, _dot(g, jnp.concatenate([x1, x2], axis=1), B_TN, "bf16")


_pair_nt.defvjp(lambda x1, x2, r: (_pair_nt(x1, x2, r), (x1, x2, r)), _pair_nt_bwd)


@jax.custom_vjp
def _wide_nn(l, r1, r2):
    y = _dot(l, jnp.concatenate([r1, r2], axis=2), B_NN, "bf16")
    return y[:, :, :r1.shape[2]], y[:, :, r1.shape[2]:]


def _wide_nn_bwd(res, g):
    l, r1, r2 = res
    g = jnp.concatenate(g, axis=2)
    dr = _dot(l, g, B_TN, "bf16")
    return (_dot(g, jnp.concatenate([r1, r2], axis=2), B_NT, "bf16"), dr[:, :, :r1.shape[2]], dr[:, :, r1.shape[2]:])


_wide_nn.defvjp(lambda l, r1, r2: (_wide_nn(l, r1, r2), (l, r1, r2)), _wide_nn_bwd)


def _lower_ones(batch, n):
    shape = (batch, n, n)
    return (lax.broadcasted_iota(jnp.int32, shape, 1) >= lax.broadcasted_iota(jnp.int32, shape, 2)).astype(BF16)


@jax.custom_vjp
def _chunk_cumsum(g):
    tri = _lower_ones(g.shape[0], g.shape[1])
    g1, g2 = _split_bf16(g)
    g3 = (g - g1.astype(F32) - g2.astype(F32)).astype(BF16)
    dg = functools.partial(lax.dot_general, dimension_numbers=B_NN, preferred_element_type=F32)
    return dg(tri, g1) + (dg(tri, g2) + dg(tri, g3))


def _chunk_cumsum_bwd(_, ct):
    tri = _lower_ones(ct.shape[0], ct.shape[1])
    c1, c2 = _split_bf16(ct)
    dg = functools.partial(lax.dot_general, dimension_numbers=B_TN, preferred_element_type=F32)
    return (dg(tri, c1) + dg(tri, c2),)


_chunk_cumsum.defvjp(lambda g: (_chunk_cumsum(g), None), _chunk_cumsum_bwd)


def _expm1(x):
    small = x * (1.0 + x * (0.5 + x * (1.0 / 6 + x * (1.0 / 24 + x * (1.0 / 120 + x * (1.0 / 720))))))
    return jnp.where(jnp.abs(x) < 0.2, small, jnp.exp(x) - 1.0)


def _sigmoid(x):
    return 1.0 / (1.0 + jnp.exp(-x))


def _silu(x):
    return x * _sigmoid(x)


def _softplus(x):
    return jnp.maximum(x, 0.0) + jnp.log(1.0 + jnp.exp(-jnp.abs(x)))


def _rmsnorm(x, w):
    return x * lax.rsqrt(jnp.mean(x * x, axis=-1, keepdims=True) + EPS) * w


def _gated_norm(o, z, w):
    return o * lax.rsqrt(jnp.mean(o * o, axis=-1, keepdims=True) + EPS) * w * _silu(z)


def _lru_gates(xc, wa, ba, wx, bx, lam):
    r = _sigmoid(_NN_B(xc, wa) + ba)
    i = _sigmoid(_NN_B(xc, wx) + bx)
    log_a = -LRU_C * r * _softplus(-lam)
    a = jnp.exp(log_a)
    mult = jnp.sqrt(-_expm1(2.0 * log_a))
    return a, mult * (i * xc)


def _scan_forward(a, b, h0):
    rows = a.shape[0]
    row = lax.broadcasted_iota(jnp.int32, a.shape, 0)
    k = 1
    while k < rows:
        seen = row >= k
        b = jnp.where(seen, a * pltpu.roll(b, k, 0) + b, b)
        a = jnp.where(seen, a * pltpu.roll(a, k, 0), a)
        k *= 2
    return b + a * h0


def _scan_reverse(a, d, carry):
    rows = a.shape[0]
    row = lax.broadcasted_iota(jnp.int32, a.shape, 0)
    last = row == rows - 1
    c = jnp.where(last, 0.0, pltpu.roll(a, rows - 1, 0))
    d = d + jnp.where(last, carry, 0.0)
    k = 1
    while k < rows:
        seen = row < rows - k
        d = jnp.where(seen, d + c * pltpu.roll(d, rows - k, 0), d)
        c = jnp.where(seen, c * pltpu.roll(c, rows - k, 0), c)
        k *= 2
    return d


def _lane_pick(row, lane_index):
    lane = lax.broadcasted_iota(jnp.int32, row.shape, 1)
    return jnp.sum(jnp.where(lane == lane_index, row, 0.0), axis=-1, keepdims=True)


def _dn_prep(qc, kc, vc, ba, a_log_row, dt_row, head):
    q = _silu(qc)
    k = _silu(kc)
    v = _silu(vc)
    q = q * lax.rsqrt(jnp.sum(q * q, axis=-1, keepdims=True) + EPS) * (HEAD_DIM ** -0.5)
    k = k * lax.rsqrt(jnp.sum(k * k, axis=-1, keepdims=True) + EPS)
    beta = _sigmoid(_lane_pick(ba, head))
    g = -jnp.exp(_lane_pick(a_log_row, head)) * _softplus(_lane_pick(ba, HEADS + head) + _lane_pick(dt_row, head))
    return q, k, v, g, beta


def _dn_chunks_head(q, k, v, gcol, bcol, tick=None):
    n, c, d = q.shape
    row = lax.broadcasted_iota(jnp.int32, (n, c, c), 1)
    col = lax.broadcasted_iota(jnp.int32, (n, c, c), 2)
    g_wide = jnp.broadcast_to(gcol, (n, c, d))
    b_wide = jnp.broadcast_to(bcol, (n, c, d))
    gc = _chunk_cumsum(g_wide)
    gc_rows = gc[:, :, :c]
    decay = jnp.exp(jnp.where(row >= col, gc_rows - jnp.swapaxes(gc_rows, 1, 2), -1e30))
    kb = k * b_wide
    eg = jnp.exp(gc)
    if tick is not None:
        tick()
    kbk, qk = _pair_nt(kb, q, k)
    a = jnp.where(row > col, kbk * decay, 0.0)
    if tick is not None:
        tick()
    tinv = _unit_lower_inverse(a) if tick is None else _unit_lower_inverse_steps(a, tick)
    u, w = _wide_nn(tinv, v * b_wide, kb * eg)
    g_last = jnp.sum(g_wide, axis=1, keepdims=True)
    return u, w, qk * decay, q * eg, k * jnp.exp(g_last - gc), jnp.exp(g_last)


def _dn_chunks(q, k, v, gcol, bcol, states):
    u, w, attn, qe, kdec, eglast = _dn_chunks_head(q, k, v, gcol, bcol)
    w_st, qe_st = _pair_nn(w, qe, states)
    v_new = u - w_st
    o = qe_st + _BNN(attn, v_new)
    return (o, states * eglast + _BTN(kdec, v_new)), (w, attn, qe, kdec, eglast)


def _conv_taps(buf, head, cw, rows):
    acc = cw[0:1, :] * buf[head, pl.ds(5, rows), :]
    for j in range(1, 4):
        acc = acc + cw[j:j + 1, :] * buf[head, pl.ds(5 + j, rows), :]
    return acc


def _conv_backward(dbuf, dhead, xbuf, xhead, cw, dxc, rows):
    dbuf[dhead, pl.ds(0, rows), :] = dxc
    dx = cw[0:1, :] * dbuf[dhead, pl.ds(3, rows), :]
    for j in range(1, 4):
        dx = dx + cw[j:j + 1, :] * dbuf[dhead, pl.ds(3 - j, rows), :]
    dcw = jnp.concatenate(
        [jnp.sum(dxc * xbuf[xhead, pl.ds(5 + j, rows), :], axis=0, keepdims=True) for j in range(4)], axis=0)
    dbuf[dhead, pl.ds(rows, 8), :] = dbuf[dhead, pl.ds(0, 8), :]
    return dx, dcw


def _params(**kw):
    return pltpu.CompilerParams(vmem_limit_bytes=VMEM_LIMIT, **kw)


def _matmul(a, b, form, tm, tn, tk, name, add=None, out_dtype=F32, dep=None):
    if form == "nn":
        (m, kdim), (_, n) = a.shape, b.shape
        a_spec = pl.BlockSpec((tm, tk), lambda j, i, k: (i, k))
        b_spec = pl.BlockSpec((tk, tn), lambda j, i, k: (k, j))
        dims = NN
    elif form == "nt":
        (m, kdim), (n, _) = a.shape, b.shape
        a_spec = pl.BlockSpec((tm, tk), lambda j, i, k: (i, k))
        b_spec = pl.BlockSpec((tn, tk), lambda j, i, k: (j, k))
        dims = NT
    else:
        (kdim, m), (_, n) = a.shape, b.shape
        a_spec = pl.BlockSpec((tk, tm), lambda j, i, k: (k, i))
        b_spec = pl.BlockSpec((tk, tn), lambda j, i, k: (k, j))
        dims = TN
    assert m % tm == 0 and n % tn == 0 and kdim % tk == 0, (name, m, n, kdim, tm, tn, tk)
    ksteps = kdim // tk
    o_spec = pl.BlockSpec((tm, tn), lambda j, i, k: (i, j))
    has_add = add is not None
    extra = [] if dep is None else [dep]

    def body(*refs):
        a_ref, b_ref = refs[:2]
        c_ref = refs[2] if has_add else None
        o_ref, acc = refs[-2:]
        k = pl.program_id(2)

        @pl.when(k == 0)
        def _():
            acc[...] = c_ref[...] if has_add else jnp.zeros_like(acc)

        acc[...] += lax.dot_general(a_ref[...].astype(BF16), b_ref[...].astype(BF16), dims,
                                    preferred_element_type=F32)

        @pl.when(k == ksteps - 1)
        def _():
            o_ref[...] = acc[...].astype(o_ref.dtype)

    in_specs = [a_spec, b_spec] + ([o_spec] if has_add else []) + [pl.BlockSpec((8, HEAD_DIM), lambda j, i, k: (0, 0))
                                                                   for _ in extra]
    args = (a, b) + ((add,) if has_add else ()) + tuple(extra)
    return pl.pallas_call(
        body, name=name, grid=(n // tn, m // tm, ksteps), in_specs=in_specs, out_specs=o_spec,
        out_shape=jax.ShapeDtypeStruct((m, n), out_dtype), scratch_shapes=[pltpu.VMEM((tm, tn), F32)],
        compiler_params=_params(dimension_semantics=("parallel", "parallel", "arbitrary")),
    )(*args)


def _matmul_nt_parts(parts, narrow, w, tm, tn, name, dep=None):
    m, c = parts[0].shape
    c2 = narrow.shape[1]
    n = w.shape[0]
    count = len(parts)
    assert m % tm == 0 and n % tn == 0 and all(p.shape == (m, c) for p in parts) and (count * c) % c2 == 0, (name, m, n)
    extra = [] if dep is None else [dep]

    def body(*refs):
        part_refs, narrow_ref, w_ref, w2_ref = refs[:count], refs[count], refs[count + 1], refs[count + 2]
        o_ref, acc = refs[-2:]
        k = pl.program_id(2)

        @pl.when(k == 0)
        def _():
            acc[...] = jnp.zeros_like(acc)

        for p in range(count):
            @pl.when(k == p)
            def _(p=p):
                acc[...] += lax.dot_general(part_refs[p][...].astype(BF16), w_ref[...].astype(BF16), NT,
                                            preferred_element_type=F32)

        @pl.when(k == count)
        def _():
            o_ref[...] = acc[...] + lax.dot_general(narrow_ref[...].astype(BF16), w2_ref[...].astype(BF16), NT,
                                                    preferred_element_type=F32)

    in_specs = ([pl.BlockSpec((tm, c), lambda j, i, k: (i, 0))] * count
                + [pl.BlockSpec((tm, c2), lambda j, i, k: (i, 0)),
                   pl.BlockSpec((tn, c), lambda j, i, k: (j, jnp.minimum(k, count - 1))),
                   pl.BlockSpec((tn, c2), lambda j, i, k: (j, count * c // c2))]
                + [pl.BlockSpec((8, HEAD_DIM), lambda j, i, k: (0, 0)) for _ in extra])
    return pl.pallas_call(
        body, name=name, grid=(n // tn, m // tm, count + 1), in_specs=in_specs,
        out_specs=pl.BlockSpec((tm, tn), lambda j, i, k: (i, j)),
        out_shape=jax.ShapeDtypeStruct((m, n), F32), scratch_shapes=[pltpu.VMEM((tm, tn), F32)],
        compiler_params=_params(dimension_semantics=("parallel", "parallel", "arbitrary")),
    )(*parts, narrow, w, w, *extra)


def _matmul_tn_parts(a, parts, tm, tn, tk, name, out_dtype):
    kdim, m = a.shape
    c = parts[0].shape[1]
    count = len(parts)
    per = c // tn
    assert m % tm == 0 and c % tn == 0 and kdim % tk == 0 and all(p.shape == (kdim, c) for p in parts), (name, m, c)
    ksteps = kdim // tk

    def body(*refs):
        a_ref, part_refs = refs[0], refs[1:1 + count]
        o_ref, acc = refs[-2:]
        j, k = pl.program_id(0), pl.program_id(2)

        @pl.when(k == 0)
        def _():
            acc[...] = jnp.zeros_like(acc)

        for p in range(count):
            @pl.when(j // per == p)
            def _(p=p):
                acc[...] += lax.dot_general(a_ref[...].astype(BF16), part_refs[p][...].astype(BF16), TN,
                                            preferred_element_type=F32)

        @pl.when(k == ksteps - 1)
        def _():
            o_ref[...] = acc[...].astype(o_ref.dtype)

    def part_spec(p):
        return pl.BlockSpec((tk, tn), lambda j, i, k: (jnp.where(j // per == p, k, 0), jnp.where(j // per == p, j % per, 0)))

    return pl.pallas_call(
        body, name=name, grid=(count * per, m // tm, ksteps),
        in_specs=[pl.BlockSpec((tk, tm), lambda j, i, k: (k, i))] + [part_spec(p) for p in range(count)],
        out_specs=pl.BlockSpec((tm, tn), lambda j, i, k: (i, j)),
        out_shape=jax.ShapeDtypeStruct((m, count * c), out_dtype), scratch_shapes=[pltpu.VMEM((tm, tn), F32)],
        compiler_params=_params(dimension_semantics=("parallel", "parallel", "arbitrary")),
    )(a, *parts)


def _rmsnorm_fwd(x, w_row, name):
    s = x.shape[0]
    tb = min(TIME_BLOCK, s)

    def body(x_ref, w_ref, o_ref):
        o_ref[...] = _rmsnorm(x_ref[...], w_ref[...]).astype(BF16)

    return pl.pallas_call(
        body, name=name, grid=(s // tb,),
        in_specs=[pl.BlockSpec((tb, D_MODEL), lambda i: (i, 0)), pl.BlockSpec((1, D_MODEL), lambda i: (0, 0))],
        out_specs=pl.BlockSpec((tb, D_MODEL), lambda i: (i, 0)),
        out_shape=jax.ShapeDtypeStruct((s, D_MODEL), BF16), compiler_params=_params(),
    )(x, w_row)


def _rmsnorm_bwd(x, w_row, dh, dres, name):
    s = x.shape[0]
    tb = min(TIME_BLOCK, s)

    def body(x_ref, w_ref, dh_ref, dres_ref, dx_ref, dw_ref):
        _, vjp = jax.vjp(_rmsnorm, x_ref[...], w_ref[...])
        dx, dw = vjp(dh_ref[...])
        dx_ref[...] = dres_ref[...] + dx

        @pl.when(pl.program_id(0) == 0)
        def _():
            dw_ref[...] = jnp.zeros_like(dw_ref)

        dw_ref[...] += dw

    row = pl.BlockSpec((tb, D_MODEL), lambda i: (i, 0))
    vec = pl.BlockSpec((1, D_MODEL), lambda i: (0, 0))
    return pl.pallas_call(
        body, name=name, grid=(s // tb,), in_specs=[row, vec, row, row], out_specs=[row, vec],
        out_shape=[jax.ShapeDtypeStruct((s, D_MODEL), F32), jax.ShapeDtypeStruct((1, D_MODEL), F32)],
        compiler_params=_params(),
    )(x, w_row, dh, dres)


def _final_loss(x, w_row, target, name):
    s = x.shape[0]
    tb = min(TIME_BLOCK, s)

    def loss_fn(xv, wv, tv):
        err = _rmsnorm(xv, wv) - tv
        return 0.5 * jnp.sum(jnp.sum(err * err, axis=-1, keepdims=True), axis=0, keepdims=True) * (1.0 / D_MODEL)

    def body(x_ref, w_ref, t_ref, loss_ref, dx_ref, dw_ref):
        tv = t_ref[...]
        loss, vjp = jax.vjp(lambda xv, wv: loss_fn(xv, wv, tv), x_ref[...], w_ref[...])
        dx, dw = vjp(jnp.ones((1, 1), F32))
        dx_ref[...] = dx

        @pl.when(pl.program_id(0) == 0)
        def _():
            dw_ref[...] = jnp.zeros_like(dw_ref)
            loss_ref[...] = jnp.zeros_like(loss_ref)

        dw_ref[...] += dw
        loss_ref[...] += jnp.broadcast_to(loss, loss_ref.shape)

    row = pl.BlockSpec((tb, D_MODEL), lambda i: (i, 0))
    vec = pl.BlockSpec((1, D_MODEL), lambda i: (0, 0))
    return pl.pallas_call(
        body, name=name, grid=(s // tb,), in_specs=[row, vec, row],
        out_specs=[pl.BlockSpec((1, HEAD_DIM), lambda i: (0, 0)), row, vec],
        out_shape=[jax.ShapeDtypeStruct((1, HEAD_DIM), F32), jax.ShapeDtypeStruct((s, D_MODEL), F32),
                   jax.ShapeDtypeStruct((1, D_MODEL), F32)],
        compiler_params=_params(),
    )(x, w_row, target)


def _head_specs(tb, time_of):
    def col(off):
        return pl.BlockSpec((tb, HEAD_DIM), lambda t, h: (time_of(t), off + h))
    return col


def _vec_spec():
    return pl.BlockSpec((1, HEAD_DIM), lambda t, h: (0, h))


def _lru_fwd(proj, conv_w, conv_b, wa, ba, wx, bx, lam, nw, name):
    s = proj.shape[0]
    tb = min(TIME_BLOCK, s)
    nt = s // tb
    col = _head_specs(tb, lambda t: t)

    def body(x_ref, z_ref, cw_ref, cb_ref, wa_ref, ba_ref, wx_ref, bx_ref, lam_ref, nw_ref,
             y_ref, hs_ref, xbuf, hcar):
        t, h = pl.program_id(0), pl.program_id(1)

        @pl.when(t == 0)
        def _():
            xbuf[h, pl.ds(0, 8), :] = jnp.zeros((8, HEAD_DIM), F32)
            hcar[h] = jnp.zeros((8, HEAD_DIM), F32)

        xbuf[h, pl.ds(8, tb), :] = x_ref[...]
        xc = _conv_taps(xbuf, h, cw_ref[...], tb) + cb_ref[...]
        a, b = _lru_gates(xc, wa_ref[...], ba_ref[...], wx_ref[...], bx_ref[...], lam_ref[...])
        hs_ref[...] = _scan_forward(a, b, hcar[h, pl.ds(0, 1), :])
        hcar[h, pl.ds(0, 1), :] = hs_ref[pl.ds(tb - 1, 1), :]
        xbuf[h, pl.ds(0, 8), :] = xbuf[h, pl.ds(tb, 8), :]
        y_ref[...] = _gated_norm(hs_ref[...], z_ref[...], nw_ref[...]).astype(BF16)

    vec = _vec_spec()
    return pl.pallas_call(
        body, name=name, grid=(nt, HEADS),
        in_specs=[col(COL_LRU_X), col(COL_LRU_Z), pl.BlockSpec((4, HEAD_DIM), lambda t, h: (0, h)), vec,
                  pl.BlockSpec((None, HEAD_DIM, HEAD_DIM), lambda t, h: (h, 0, 0)), vec,
                  pl.BlockSpec((None, HEAD_DIM, HEAD_DIM), lambda t, h: (h, 0, 0)), vec, vec, vec],
        out_specs=[col(0), col(0)],
        out_shape=[jax.ShapeDtypeStruct((s, 2 * D_MODEL), BF16), jax.ShapeDtypeStruct((s, D_MODEL), F32)],
        scratch_shapes=[pltpu.VMEM((HEADS, tb + 8, HEAD_DIM), F32), pltpu.VMEM((HEADS, 8, HEAD_DIM), F32)],
        compiler_params=_params(dimension_semantics=("arbitrary", "arbitrary")),
    )(proj, proj, conv_w, conv_b, wa, ba, wx, bx, lam, nw)


def _halo_spec(tb, nt, off):
    per = tb // 8
    return pl.BlockSpec((8, HEAD_DIM), lambda t, h: (jnp.maximum((nt - 1 - t) * per - 1, 0), off + h))


def _lru_bwd(proj, hs, dy, conv_w, conv_b, wa, ba, wx, bx, lam, nw, name):
    s = proj.shape[0]
    tb = min(TIME_BLOCK, s)
    nt = s // tb
    col = _head_specs(tb, lambda t: nt - 1 - t)

    def body(x_ref, xh_ref, z_ref, hs_ref, hh_ref, dy_ref, cw_ref, cb_ref, wa_ref, ba_ref, wx_ref, bx_ref,
             lam_ref, nw_ref, dx_ref, dz_ref, dcw_ref, dcb_ref, dwa_ref, dba_ref, dwx_ref, dbx_ref, dlam_ref,
             dnw_ref, xbuf, hbuf, dbuf, gcar):
        t, h = pl.program_id(0), pl.program_id(1)
        first_block = t == nt - 1

        @pl.when(t == 0)
        def _():
            dbuf[h, pl.ds(tb, 8), :] = jnp.zeros((8, HEAD_DIM), F32)
            gcar[h] = jnp.zeros((8, HEAD_DIM), F32)
            dcw_ref[h] = jnp.zeros((4, HEAD_DIM), F32)
            dwa_ref[h] = jnp.zeros((HEAD_DIM, HEAD_DIM), F32)
            dwx_ref[h] = jnp.zeros((HEAD_DIM, HEAD_DIM), F32)
            for ref in (dcb_ref, dba_ref, dbx_ref, dlam_ref, dnw_ref):
                ref[h] = jnp.zeros((1, HEAD_DIM), F32)

        keep = jnp.where(first_block, 0.0, 1.0)
        xbuf[0, pl.ds(0, 8), :] = xh_ref[...] * keep
        xbuf[0, pl.ds(8, tb), :] = x_ref[...]
        hbuf[pl.ds(0, 8), :] = hh_ref[...] * keep
        hbuf[pl.ds(8, tb), :] = hs_ref[...]
        cw = cw_ref[...]
        xc = _conv_taps(xbuf, 0, cw, tb) + cb_ref[...]
        (a, _), gates_vjp = jax.vjp(_lru_gates, xc, wa_ref[...], ba_ref[...], wx_ref[...], bx_ref[...], lam_ref[...])
        _, norm_vjp = jax.vjp(_gated_norm, hs_ref[...], z_ref[...], nw_ref[...])
        dh, dz, dnw = norm_vjp(dy_ref[...])
        dz_ref[...] = dz.astype(dz_ref.dtype)
        g = _scan_reverse(a, dh, gcar[h, pl.ds(0, 1), :])
        gcar[h, pl.ds(0, 1), :] = a[0:1, :] * g[0:1, :]
        dxc, dwa, dba, dwx, dbx, dlam = gates_vjp((g * hbuf[pl.ds(7, tb), :], g))
        dx, dcw = _conv_backward(dbuf, h, xbuf, 0, cw, dxc, tb)
        dx_ref[...] = dx.astype(dx_ref.dtype)
        dcw_ref[h] += dcw
        dcb_ref[h] += jnp.sum(dxc, axis=0, keepdims=True)
        dwa_ref[h] += dwa
        dwx_ref[h] += dwx
        dba_ref[h] += dba
        dbx_ref[h] += dbx
        dlam_ref[h] += dlam
        dnw_ref[h] += dnw

    vec = _vec_spec()
    mat = pl.BlockSpec((None, HEAD_DIM, HEAD_DIM), lambda t, h: (h, 0, 0))

    def whole(shape):
        return pl.BlockSpec(shape, lambda t, h: (0,) * len(shape))

    head_vec = jax.ShapeDtypeStruct((HEADS, 1, HEAD_DIM), F32)
    head_mat = jax.ShapeDtypeStruct((HEADS, HEAD_DIM, HEAD_DIM), F32)
    return pl.pallas_call(
        body, name=name, grid=(nt, HEADS),
        in_specs=[col(COL_LRU_X), _halo_spec(tb, nt, COL_LRU_X), col(COL_LRU_Z), col(0), _halo_spec(tb, nt, 0), col(0),
                  pl.BlockSpec((4, HEAD_DIM), lambda t, h: (0, h)), vec, mat, vec, mat, vec, vec, vec],
        out_specs=[col(0), col(0), whole((HEADS, 4, HEAD_DIM)), whole((HEADS, 1, HEAD_DIM)),
                   whole((HEADS, HEAD_DIM, HEAD_DIM)), whole((HEADS, 1, HEAD_DIM)),
                   whole((HEADS, HEAD_DIM, HEAD_DIM)), whole((HEADS, 1, HEAD_DIM)), whole((HEADS, 1, HEAD_DIM)),
                   whole((HEADS, 1, HEAD_DIM))],
        out_shape=[jax.ShapeDtypeStruct((s, D_MODEL), BF16), jax.ShapeDtypeStruct((s, D_MODEL), BF16),
                   jax.ShapeDtypeStruct((HEADS, 4, HEAD_DIM), F32), head_vec, head_mat, head_vec, head_mat, head_vec,
                   head_vec, head_vec],
        scratch_shapes=[pltpu.VMEM((1, tb + 8, HEAD_DIM), F32), pltpu.VMEM((tb + 8, HEAD_DIM), F32),
                        pltpu.VMEM((HEADS, tb + 8, HEAD_DIM), F32), pltpu.VMEM((HEADS, 8, HEAD_DIM), F32)],
        compiler_params=_params(dimension_semantics=("arbitrary", "arbitrary")),
    )(proj, proj, proj, hs, hs, dy, conv_w, conv_b, wa, ba, wx, bx, lam, nw)


def _dn_fwd(proj, y, conv_w, a_log_row, dt_row, nw, name):
    s = proj.shape[0]
    tb = min(TIME_BLOCK, s)
    nt = s // tb
    nchunk = tb // CHUNK
    part = min(PART_CHUNKS, nchunk)
    assert nchunk % part == 0, (nchunk, part)
    col = _head_specs(tb, lambda t: t)

    def body(q_ref, k_ref, v_ref, z_ref, ba_ref, cwq_ref, cwk_ref, cwv_ref, al_ref, dt_ref, nw_ref, y_in_ref,
             y_ref, o_ref, st_ref, xbuf, state):
        t, h = pl.program_id(0), pl.program_id(1)

        @pl.when(t == 0)
        def _():
            for i in range(3):
                xbuf[3 * h + i, pl.ds(0, 8), :] = jnp.zeros((8, HEAD_DIM), F32)
            state[h] = jnp.zeros((HEAD_DIM, HEAD_DIM), F32)

        conv = []
        for i, (ref, cw_ref) in enumerate(((q_ref, cwq_ref), (k_ref, cwk_ref), (v_ref, cwv_ref))):
            xbuf[3 * h + i, pl.ds(8, tb), :] = ref[...]
            conv.append(_conv_taps(xbuf, 3 * h + i, cw_ref[...], tb))
            xbuf[3 * h + i, pl.ds(0, 8), :] = xbuf[3 * h + i, pl.ds(tb, 8), :]
        q, k, v, g, beta = _dn_prep(conv[0], conv[1], conv[2], ba_ref[...], al_ref[...], dt_ref[...], h)
        def chunks(a):
            return a.reshape(nchunk, CHUNK, a.shape[-1])

        qs, ks, vs, gs, bs = chunks(q), chunks(k), chunks(v), chunks(g), chunks(beta)
        carried = [state[h]]
        pending = []

        def tick():
            if pending:
                pending.pop(0)()

        def chain_step(lo, c, eglast, kdec_w_u):
            def step():
                st_ref[lo + c] = carried[0]
                carried[0] = (carried[0] * eglast[c] - _NN_B(kdec_w_u[c, :, :HEAD_DIM], carried[0])
                              + kdec_w_u[c, :, HEAD_DIM:])
            return step

        def outputs(lo, qe, attn_w_u):
            def step():
                o = (_dot(qe - attn_w_u[:, :, :HEAD_DIM], st_ref[pl.ds(lo, part)], B_NN, "bf16")
                     + attn_w_u[:, :, HEAD_DIM:])
                o_ref[pl.ds(lo * CHUNK, part * CHUNK), :] = o.reshape(part * CHUNK, HEAD_DIM)
            return step

        for lo in range(0, nchunk, part):
            sl = slice(lo, lo + part)
            u, w, attn, qe, kdec, eglast = _dn_chunks_head(qs[sl], ks[sl], vs[sl], gs[sl], bs[sl], tick)
            w_u = jnp.concatenate([w, u], axis=2)
            kdec_w_u = _dot(kdec, w_u, B_TN, "bf16")
            tick()
            attn_w_u = _dot(attn, w_u, B_NN, "bf16")
            while pending:
                tick()
            pending += [chain_step(lo, c, eglast, kdec_w_u) for c in range(part)] + [outputs(lo, qe, attn_w_u)]
        while pending:
            tick()
        st = carried[0]
        state[h] = st
        y_ref[...] = _gated_norm(o_ref[...], z_ref[...], nw_ref[...]).astype(BF16)

    def cw_spec(off):
        return pl.BlockSpec((4, HEAD_DIM), lambda t, h: (0, off + h))

    row128 = pl.BlockSpec((1, HEAD_DIM), lambda t, h: (0, 0))
    return pl.pallas_call(
        body, name=name, grid=(nt, HEADS),
        in_specs=[col(COL_Q), col(COL_K), col(COL_V), col(COL_DN_Z),
                  pl.BlockSpec((tb, HEAD_DIM), lambda t, h: (t, COL_BA)),
                  cw_spec(0), cw_spec(HEADS), cw_spec(2 * HEADS), row128, row128, row128,
                  pl.BlockSpec(memory_space=pl.ANY)],
        out_specs=[col(HEADS), col(0), pl.BlockSpec((None, nchunk, HEAD_DIM, HEAD_DIM), lambda t, h: (h, t, 0, 0))],
        out_shape=[jax.ShapeDtypeStruct((s, 2 * D_MODEL), BF16), jax.ShapeDtypeStruct((s, D_MODEL), F32),
                   jax.ShapeDtypeStruct((HEADS, s // CHUNK, HEAD_DIM, HEAD_DIM), F32)],
        input_output_aliases={11: 0},
        scratch_shapes=[pltpu.VMEM((3 * HEADS, tb + 8, HEAD_DIM), F32), pltpu.VMEM((HEADS, HEAD_DIM, HEAD_DIM), F32)],
        compiler_params=_params(dimension_semantics=("arbitrary", "arbitrary")),
    )(proj, proj, proj, proj, proj, conv_w, conv_w, conv_w, a_log_row, dt_row, nw, y)


def _dn_bwd(proj, o, states, dy, conv_w, a_log_row, dt_row, nw, name):
    s = proj.shape[0]
    tb = min(TIME_BLOCK, s)
    nt = s // tb
    nchunk = tb // CHUNK
    part = min(PART_CHUNKS_BWD, nchunk)
    assert nchunk % part == 0, (nchunk, part)
    col = _head_specs(tb, lambda t: nt - 1 - t)

    def body(q_ref, qh_ref, k_ref, kh_ref, v_ref, vh_ref, z_ref, ba_ref, o_ref, st_ref, dy_ref,
             cwq_ref, cwk_ref, cwv_ref, al_ref, dt_ref, nw_ref,
             dq_ref, dk_ref, dv_ref, dz_ref, dba_ref, dcw_ref, dal_ref, ddt_ref, dnw_ref,
             xbuf, dbuf, dstate, dst_s):
        t, h = pl.program_id(0), pl.program_id(1)
        first_block = t == nt - 1

        @pl.when(t == 0)
        def _():
            for i in range(3):
                dbuf[3 * h + i, pl.ds(tb, 8), :] = jnp.zeros((8, HEAD_DIM), F32)
                dcw_ref[3 * h + i] = jnp.zeros((4, HEAD_DIM), F32)
            dstate[h] = jnp.zeros((HEAD_DIM, HEAD_DIM), F32)

        @pl.when((t == 0) & (h == 0))
        def _():
            for ref in (dal_ref, ddt_ref, dnw_ref):
                ref[...] = jnp.zeros_like(ref)

        keep = jnp.where(first_block, 0.0, 1.0)
        cws = (cwq_ref[...], cwk_ref[...], cwv_ref[...])
        conv = []
        for i, (ref, halo) in enumerate(((q_ref, qh_ref), (k_ref, kh_ref), (v_ref, vh_ref))):
            xbuf[i, pl.ds(0, 8), :] = halo[...] * keep
            xbuf[i, pl.ds(8, tb), :] = ref[...]
            conv.append(_conv_taps(xbuf, i, cws[i], tb))
        (q, k, v, g, beta), prep_vjp = jax.vjp(
            lambda qc, kc, vc, ba, al, dt: _dn_prep(qc, kc, vc, ba, al, dt, h),
            conv[0], conv[1], conv[2], ba_ref[...], al_ref[...], dt_ref[...])
        _, norm_vjp = jax.vjp(_gated_norm, o_ref[...], z_ref[...], nw_ref[...])
        do, dz, dnw = norm_vjp(dy_ref[...])
        dz_ref[...] = dz.astype(dz_ref.dtype)
        dnw_ref[...] += dnw

        def chunks(a):
            return a.reshape(nchunk, CHUNK, a.shape[-1])

        do, qs, ks, vs, gs, bs = chunks(do), chunks(q), chunks(k), chunks(v), chunks(g), chunks(beta)
        dst = dstate[h]
        cotangents = []
        for lo in reversed(range(0, nchunk, part)):
            sl = slice(lo, lo + part)
            _, chunks_vjp, (w, attn, qe, kdec, eglast) = jax.vjp(
                _dn_chunks, qs[sl], ks[sl], vs[sl], gs[sl], bs[sl], st_ref[pl.ds(lo, part)], has_aux=True)
            kdec_w = _dot(kdec, w, B_TN, "bf16")
            fixed = _dot(qe, do[sl], B_TN, "bf16") - _dot(w, _dot(attn, do[sl], B_TN, "bf16"), B_TN, "bf16")
            for c in reversed(range(part)):
                dst_s[lo + c] = dst
                dst = dst * eglast[c] - _dot(kdec_w[c], dst, TN, "bf16") + fixed[c]
            cotangents.insert(0, chunks_vjp((do[sl], dst_s[pl.ds(lo, part)])))
        dstate[h] = dst
        dq, dk, dv, dg, db = [jnp.concatenate([ct[i] for ct in cotangents], axis=0) for i in range(5)]

        def rows(a):
            return a.reshape(tb, a.shape[-1])

        dqc, dkc, dvc, dba, dal, ddt = prep_vjp((rows(dq), rows(dk), rows(dv), rows(dg), rows(db)))
        for i, (dxc, out) in enumerate(((dqc, dq_ref), (dkc, dk_ref), (dvc, dv_ref))):
            dx, dcw = _conv_backward(dbuf, 3 * h + i, xbuf, i, cws[i], dxc, tb)
            out[...] = dx.astype(out.dtype)
            dcw_ref[3 * h + i] += dcw
        dal_ref[...] += dal
        ddt_ref[...] += ddt

        @pl.when(h == 0)
        def _():
            dba_ref[...] = dba.astype(dba_ref.dtype)

        @pl.when(h > 0)
        def _():
            dba_ref[...] += dba.astype(dba_ref.dtype)

    def cw_spec(off):
        return pl.BlockSpec((4, HEAD_DIM), lambda t, h: (0, off + h))

    def whole(shape):
        return pl.BlockSpec(shape, lambda t, h: (0,) * len(shape))

    row128 = whole((1, HEAD_DIM))
    blk = (tb, HEAD_DIM)
    act = jax.ShapeDtypeStruct((s, D_MODEL), BF16)
    row_out = jax.ShapeDtypeStruct((1, HEAD_DIM), F32)
    return pl.pallas_call(
        body, name=name, grid=(nt, HEADS),
        in_specs=[col(COL_Q), _halo_spec(tb, nt, COL_Q), col(COL_K), _halo_spec(tb, nt, COL_K),
                  col(COL_V), _halo_spec(tb, nt, COL_V), col(COL_DN_Z),
                  pl.BlockSpec(blk, lambda t, h: (nt - 1 - t, COL_BA)), col(0),
                  pl.BlockSpec((None, nchunk, HEAD_DIM, HEAD_DIM), lambda t, h: (h, nt - 1 - t, 0, 0)), col(HEADS),
                  cw_spec(0), cw_spec(HEADS), cw_spec(2 * HEADS), row128, row128, row128],
        out_specs=[col(0), col(0), col(0), col(0), pl.BlockSpec(blk, lambda t, h: (nt - 1 - t, 0)),
                   whole((3 * HEADS, 4, HEAD_DIM)), row128, row128, row128],
        out_shape=[act, act, act, act, jax.ShapeDtypeStruct((s, HEAD_DIM), F32),
                   jax.ShapeDtypeStruct((3 * HEADS, 4, HEAD_DIM), F32), row_out, row_out, row_out],
        scratch_shapes=[pltpu.VMEM((3, tb + 8, HEAD_DIM), F32), pltpu.VMEM((3 * HEADS, tb + 8, HEAD_DIM), F32),
                        pltpu.VMEM((HEADS, HEAD_DIM, HEAD_DIM), F32), pltpu.VMEM((nchunk, HEAD_DIM, HEAD_DIM), F32)],
        compiler_params=_params(dimension_semantics=("arbitrary", "arbitrary")),
    )(proj, proj, proj, proj, proj, proj, proj, proj, o, states, dy, conv_w, conv_w, conv_w, a_log_row, dt_row, nw)


def _mesh_position():
    x, y, c = lax.axis_index("x"), lax.axis_index("y"), lax.axis_index("c")
    return x, y, c, 4 * x + 2 * y + c


def _peer(k, x, y, c):
    px = 1 - x if k & 4 else x
    py = 1 - y if k & 2 else y
    pc = 1 - c if k & 1 else c
    return (px, py, pc), 4 * px + 2 * py + pc


def _exchange_copies(ins, lands, scatter, send_sems, recv_sems, receives=True):
    x, y, c, me = _mesh_position()
    sends, recvs = [], []
    for i, (src, land) in enumerate(zip(ins, lands)):
        for k in range(1, N_DEV):
            peer, peer_id = _peer(k, x, y, c)
            sem = i * (N_DEV - 1) + k - 1
            for dst, out in ((me, sends), (peer_id, recvs)) if receives else ((me, sends),):
                out.append(pltpu.make_async_remote_copy(
                    src_ref=src.at[peer_id] if scatter[i] else src, dst_ref=land.at[dst],
                    send_sem=send_sems.at[sem], recv_sem=recv_sems.at[sem],
                    device_id=peer, device_id_type=pl.DeviceIdType.MESH))
    return sends, recvs


def _landing_shape(a, scatter):
    return a.shape if scatter else (N_DEV,) + a.shape


def _direct_exchange(arrays, scatter, name):
    n = len(arrays)
    out_shapes = [jax.ShapeDtypeStruct(_landing_shape(a, sc), a.dtype) for a, sc in zip(arrays, scatter)]

    def body(*refs):
        ins, outs = refs[:n], refs[n:2 * n]
        send_sems, recv_sems, local_sems = refs[2 * n:]
        me = _mesh_position()[3]
        local = [pltpu.make_async_copy(ins[i].at[me] if scatter[i] else ins[i], outs[i].at[me], local_sems.at[i])
                 for i in range(n)]
        sends, recvs = _exchange_copies(ins, outs, scatter, send_sems, recv_sems)
        for cp in local + sends:
            cp.start()
        for cp in recvs:
            cp.wait_recv()
        for cp in sends:
            cp.wait_send()
        for cp in local:
            cp.wait()

    hbm = pl.BlockSpec(memory_space=pl.ANY)
    return pl.pallas_call(
        body, name=name, in_specs=[hbm] * n, out_specs=[hbm] * n, out_shape=out_shapes,
        scratch_shapes=[pltpu.SemaphoreType.DMA((n * (N_DEV - 1),)), pltpu.SemaphoreType.DMA((n * (N_DEV - 1),)),
                        pltpu.SemaphoreType.DMA((n,))],
    )(*arrays)


def _two_level_gather(arrays, name):
    n = len(arrays)
    per = N_DEV - 1

    def body(*refs):
        ins, outs = refs[:n], refs[n:2 * n]
        send_sems, recv_sems, local_sems = refs[2 * n:]
        x, y, c, me = _mesh_position()
        sibling = (x, y, 1 - c)
        chips = [(1 - x, y), (x, 1 - y), (1 - x, 1 - y)]

        def copy(i, k, block, to, src=None):
            slot = outs[i].at[4 * block[0] + 2 * block[1] + block[2]]
            return pltpu.make_async_remote_copy(
                src_ref=slot if src is None else src, dst_ref=slot,
                send_sem=send_sems.at[i * per + k], recv_sem=recv_sems.at[i * per + k],
                device_id=to, device_id_type=pl.DeviceIdType.MESH)

        local = [pltpu.make_async_copy(ins[i], outs[i].at[me], local_sems.at[i]) for i in range(n)]
        first = []
        for i in range(n):
            first.append(copy(i, 0, (x, y, c), sibling, src=ins[i]))
            first += [copy(i, 1 + j, (x, y, c), (*chip, c), src=ins[i]) for j, chip in enumerate(chips)]
        for cp in local + first:
            cp.start()
        passed = []
        for i in range(n):
            for j, chip in enumerate(chips):
                copy(i, 1 + j, (*chip, c), (x, y, c)).wait_recv()
                passed.append(copy(i, 4 + j, (*chip, c), sibling))
                passed[-1].start()
        for i in range(n):
            copy(i, 0, sibling, (x, y, c)).wait_recv()
            for j, chip in enumerate(chips):
                copy(i, 4 + j, (*chip, 1 - c), (x, y, c)).wait_recv()
        for cp in first + passed:
            cp.wait_send()
        for cp in local:
            cp.wait()

    hbm = pl.BlockSpec(memory_space=pl.ANY)
    return pl.pallas_call(
        body, name=name, in_specs=[hbm] * n, out_specs=[hbm] * n,
        out_shape=[jax.ShapeDtypeStruct((N_DEV,) + a.shape, a.dtype) for a in arrays],
        scratch_shapes=[pltpu.SemaphoreType.DMA((n * per,)), pltpu.SemaphoreType.DMA((n * per,)),
                        pltpu.SemaphoreType.DMA((n,))],
    )(*arrays)


_HBM = pl.BlockSpec(memory_space=pltpu.HBM)
_SEM = pl.BlockSpec(memory_space=pltpu.SEMAPHORE)
_DATAFLOW = pltpu.SideEffectType.DATAFLOW_SIDE_EFFECTING


def _exchange_start(arrays, scatter, name):
    n = len(arrays)
    srcs = [pltpu.with_memory_space_constraint(a, pltpu.HBM) for a in arrays]
    lands = [pltpu.with_memory_space_constraint(lax.empty(_landing_shape(a, sc), a.dtype), pltpu.HBM)
             for a, sc in zip(arrays, scatter)]
    nsem = n * (N_DEV - 1)

    def body(*refs):
        ins, zones = refs[:n], refs[n:2 * n]
        send_sems, recv_sems = refs[2 * n], refs[2 * n + 1]
        token = refs[-1]
        sends, _ = _exchange_copies(ins, zones, scatter, send_sems, recv_sems, receives=False)
        for cp in sends:
            cp.start()
        token[...] = jnp.zeros_like(token)

    res = pl.pallas_call(
        body, name=name,
        out_shape=(pltpu.SemaphoreType.DMA((nsem,)), pltpu.SemaphoreType.DMA((nsem,)),
                   *[pltpu.HBM(a.shape, a.dtype) for a in srcs + lands], jax.ShapeDtypeStruct((8, HEAD_DIM), F32)),
        in_specs=[_HBM] * (2 * n),
        out_specs=(_SEM, _SEM, *[_HBM] * (2 * n), pl.BlockSpec(memory_space=pltpu.VMEM)),
        input_output_aliases={i: 2 + i for i in range(2 * n)},
        compiler_params=pltpu.CompilerParams(has_side_effects=_DATAFLOW),
    )(*srcs, *lands)
    return dict(sems=res[:2], srcs=res[2:2 + n], lands=res[2 + n:2 + 2 * n], token_block=res[-1],
                token=res[-1][0, 0], scatter=scatter)


def _exchange_wait(started, after, name):
    scatter = started["scatter"]
    n = len(scatter)

    def body(*refs):
        ins, zones = refs[:n], refs[n:2 * n]
        send_sems, recv_sems = refs[2 * n], refs[2 * n + 1]
        sends, recvs = _exchange_copies(ins, zones, scatter, send_sems, recv_sems)
        for cp in sends:
            cp.wait_send()
        for cp in recvs:
            cp.wait_recv()

    thru = list(started["srcs"]) + list(started["lands"])
    res = pl.pallas_call(
        body, name=name, out_shape=[pltpu.HBM(a.shape, a.dtype) for a in thru],
        in_specs=[_HBM] * (2 * n) + [_SEM, _SEM, pl.BlockSpec(memory_space=pl.ANY)], out_specs=[_HBM] * (2 * n),
        input_output_aliases={i: i for i in range(2 * n)},
        compiler_params=pltpu.CompilerParams(has_side_effects=_DATAFLOW),
    )(*thru, *started["sems"], after)
    me = 4 * lax.axis_index("x") + 2 * lax.axis_index("y") + lax.axis_index("c")
    out = []
    for src, got, sc in zip(res[:n], res[n:], scatter):
        own = lax.dynamic_index_in_dim(src, me, 0, keepdims=False) if sc else src
        out.append(lax.dynamic_update_index_in_dim(got, own, me, 0))
    return out


def _adamw(parts, w, m, v, name, rows_per_step, layer=None, into=None):
    rows, cols = parts.shape[1:]
    tr = min(rows_per_step, rows)
    assert rows % tr == 0 and w.shape[-2:] == (rows, cols), (name, rows, tr, w.shape)
    c1 = 1.0 / (1.0 - ADAM_B1 ** ADAM_STEP)
    c2 = 1.0 / (1.0 - ADAM_B2 ** ADAM_STEP)

    def body(p_ref, w_ref, m_ref, v_ref, *rest):
        g_ref, d_ref, nm_ref, nv_ref = rest[-4:]
        g = p_ref[0].astype(F32)
        for d in range(1, N_DEV):
            g = g + p_ref[d].astype(F32)
        nm = ADAM_B1 * m_ref[...] + (1.0 - ADAM_B1) * g
        nv = ADAM_B2 * v_ref[...] + (1.0 - ADAM_B2) * (g * g)
        g_ref[...] = g
        nm_ref[...] = nm
        nv_ref[...] = nv
        d_ref[...] = -ADAM_LR * ((nm * c1) / (jnp.sqrt(nv * c2) + ADAM_EPS) + ADAM_WD * w_ref[...])

    if layer is None:
        blk = pl.BlockSpec((tr, cols), lambda i: (i, 0))
    else:
        blk = pl.BlockSpec((None, tr, cols), lambda i: (layer, i, 0))
    shape = jax.ShapeDtypeStruct(w.shape, F32)
    prior = [] if into is None else list(into)
    return pl.pallas_call(
        body, name=name, grid=(rows // tr,),
        in_specs=[pl.BlockSpec((N_DEV, tr, cols), lambda i: (0, i, 0)), blk, blk, blk]
        + [pl.BlockSpec(memory_space=pl.ANY)] * len(prior),
        out_specs=[blk] * 4, out_shape=[shape] * 4,
        input_output_aliases={4 + j: j for j in range(len(prior))}, compiler_params=_params(),
    )(parts, w, m, v, *prior)


_LAYERED = ("norm_w", "lru_conv_b", "lru_wa", "lru_ba", "lru_wx", "lru_bx", "lru_lambda", "lru_norm_w",
            "dn_A_log", "dn_dt_bias", "dn_norm_w")
_PACK_LRU = _LAYERED[1:8]
_PACK_LAST = _LAYERED[:1] + _LAYERED[8:]
_WEIGHTS = ("norm_w", "w_in", "lru_conv_w", "lru_conv_b", "lru_wa", "lru_ba", "lru_wx", "lru_bx", "lru_lambda",
            "lru_norm_w", "dn_conv_w", "dn_A_log", "dn_dt_bias", "dn_norm_w", "w_out", "final_norm_w")


def _pack_layer(tree, layer, tail=(), names=_LAYERED):
    rows = []
    for name in names:
        a = tree[name][layer]
        if a.shape[-1] == HEADS:
            a = jnp.pad(a, (0, HEAD_DIM - HEADS))
        rows.append(a.reshape(-1, HEAD_DIM))
    rows += [t.reshape(-1, HEAD_DIM) for t in tail]
    packed = jnp.concatenate(rows, axis=0)
    return jnp.pad(packed, ((0, (-packed.shape[0]) % 8), (0, 0)))


def _unpack_layer(packed, like, names=_LAYERED):
    out, at = {}, 0
    for name in names:
        shape = like[name].shape[1:]
        if shape[-1] == HEADS:
            n = 1
            out[name] = packed[at, :HEADS]
        else:
            n = like[name][0].size // HEAD_DIM
            out[name] = packed[at:at + n].reshape(shape)
        at += n
    return out, at


def _heads_to_channels(a):
    return jnp.transpose(a, (1, 0, 2)).reshape(a.shape[1], HEADS * HEAD_DIM)


def kernel(x, norm_w, w_in, lru_conv_w, lru_conv_b, lru_wa, lru_ba, lru_wx, lru_bx, lru_lambda, lru_norm_w, dn_conv_w, dn_A_log, dn_dt_bias, dn_norm_w, w_out, final_norm_w, loss_target, m_norm_w, m_w_in, m_lru_conv_w, m_lru_conv_b, m_lru_wa, m_lru_ba, m_lru_wx, m_lru_bx, m_lru_lambda, m_lru_norm_w, m_dn_conv_w, m_dn_A_log, m_dn_dt_bias, m_dn_norm_w, m_w_out, m_final_norm_w, v_norm_w, v_w_in, v_lru_conv_w, v_lru_conv_b, v_lru_wa, v_lru_ba, v_lru_wx, v_lru_bx, v_lru_lambda, v_lru_norm_w, v_dn_conv_w, v_dn_A_log, v_dn_dt_bias, v_dn_norm_w, v_w_out, v_final_norm_w):
    weights = dict(norm_w=norm_w, w_in=w_in, lru_conv_w=lru_conv_w, lru_conv_b=lru_conv_b, lru_wa=lru_wa,
                   lru_ba=lru_ba, lru_wx=lru_wx, lru_bx=lru_bx, lru_lambda=lru_lambda, lru_norm_w=lru_norm_w,
                   dn_conv_w=dn_conv_w, dn_A_log=dn_A_log, dn_dt_bias=dn_dt_bias, dn_norm_w=dn_norm_w,
                   w_out=w_out, final_norm_w=final_norm_w)
    mom_m = dict(norm_w=m_norm_w, w_in=m_w_in, lru_conv_w=m_lru_conv_w, lru_conv_b=m_lru_conv_b, lru_wa=m_lru_wa,
                 lru_ba=m_lru_ba, lru_wx=m_lru_wx, lru_bx=m_lru_bx, lru_lambda=m_lru_lambda,
                 lru_norm_w=m_lru_norm_w, dn_conv_w=m_dn_conv_w, dn_A_log=m_dn_A_log, dn_dt_bias=m_dn_dt_bias,
                 dn_norm_w=m_dn_norm_w, w_out=m_w_out, final_norm_w=m_final_norm_w)
    mom_v = dict(norm_w=v_norm_w, w_in=v_w_in, lru_conv_w=v_lru_conv_w, lru_conv_b=v_lru_conv_b, lru_wa=v_lru_wa,
                 lru_ba=v_lru_ba, lru_wx=v_lru_wx, lru_bx=v_lru_bx, lru_lambda=v_lru_lambda,
                 lru_norm_w=v_lru_norm_w, dn_conv_w=v_dn_conv_w, dn_A_log=v_dn_A_log, dn_dt_bias=v_dn_dt_bias,
                 dn_norm_w=v_dn_norm_w, w_out=v_w_out, final_norm_w=v_final_norm_w)
    depth = norm_w.shape[0]
    xs = x[0]
    s = xs.shape[0]
    tm = min(1024, s)

    assert depth >= 2, depth

    def row(a):
        return a.reshape(1, -1)

    def pad_row(a):
        return jnp.pad(a, (0, HEAD_DIM - a.shape[0])).reshape(1, HEAD_DIM)

    def full_w_in(g):
        w = jnp.transpose(g, (1, 2, 0, 3)).reshape(g.shape[1], D_MODEL, D_IN)
        return jnp.pad(w, ((0, 0), (0, 0), (0, D_IN_PAD - D_IN)))

    g_win0, g_lcw, g_dcw = _two_level_gather([w_in[:1].astype(BF16), lru_conv_w, dn_conv_w], "gather_first")
    rest = _exchange_start([w_in[1:].astype(BF16), w_out.astype(BF16)], [False] * 2, "gather_rest_start")
    win = [full_w_in(g_win0)[0]]
    wout = None
    lcw = jnp.transpose(g_lcw, (1, 2, 0, 3)).reshape(depth, 4, D_MODEL)
    dcw = jnp.transpose(g_dcw, (1, 2, 0, 3)).reshape(depth, 4, 3 * D_MODEL)

    saved = []
    cur = xs
    for l in range(depth):
        nw_row = row(norm_w[l]) + rest["token"] if l == 0 else row(norm_w[l])
        hn = _rmsnorm_fwd(cur, nw_row, f"norm_fwd_{l}")
        proj = _matmul(hn, win[l], "nn", tm, 896, D_MODEL, f"in_proj_{l}")
        y_lru, hs = _lru_fwd(proj, lcw[l], row(lru_conv_b[l]), lru_wa[l], row(lru_ba[l]), lru_wx[l], row(lru_bx[l]),
                             row(lru_lambda[l]), row(lru_norm_w[l]), f"lru_fwd_{l}")
        ycat, o_dn, states = _dn_fwd(proj, y_lru, dcw[l], pad_row(dn_A_log[l]), pad_row(dn_dt_bias[l]),
                                     row(dn_norm_w[l]), f"dn_fwd_{l}")
        if l == 0:
            g_win_rest, g_wout = _exchange_wait(rest, ycat, "gather_rest_wait")
            win += list(full_w_in(g_win_rest))
            wout = jnp.transpose(g_wout, (1, 0, 2, 3)).reshape(depth, 2 * D_MODEL, D_MODEL)
        nxt = _matmul(ycat, wout[l], "nn", tm, D_MODEL, 2 * D_MODEL, f"out_proj_{l}", add=cur)
        saved.append((cur, hn, proj, hs, o_dn, states, ycat))
        cur = nxt
    loss_part, dx, d_final = _final_loss(cur, row(final_norm_w), loss_target[0], "final_loss")

    def win_slots(g):
        return jnp.transpose(g.reshape(D_MODEL, N_DEV, D_IN // N_DEV), (1, 0, 2))

    def wout_slots(g):
        return g.reshape(N_DEV, 2 * D_MODEL // N_DEV, D_MODEL)

    grads = {k: [None] * depth for k in _WEIGHTS if k not in ("final_norm_w", "w_in", "w_out")}
    started = {}
    token = None
    for l in reversed(range(depth)):
        x_in, hn, proj, hs, o_dn, states, ycat = saved[l]
        dy = _matmul(dx, wout[l], "nt", tm, D_MODEL, D_MODEL, f"out_proj_dy_{l}")
        g_wout_l = _matmul(ycat, dx, "tn", D_MODEL, D_MODEL, tm, f"out_proj_dw_{l}", out_dtype=BF16)
        if l == 0:
            started["w_out_0"] = _exchange_start([wout_slots(g_wout_l)], [True], "exchange_w_out_0_start")
            token = token + started["w_out_0"]["token"]
        cb_row = row(lru_conv_b[l]) if token is None else row(lru_conv_b[l]) + token
        (dlx, dlz, g_lcw, g_lcb, g_wa, g_ba, g_wx, g_bx, g_lam, g_lnw) = _lru_bwd(
            proj, hs, dy, lcw[l], cb_row, lru_wa[l], row(lru_ba[l]), lru_wx[l], row(lru_bx[l]),
            row(lru_lambda[l]), row(lru_norm_w[l]), f"lru_bwd_{l}")
        grads["lru_conv_w"][l] = _heads_to_channels(g_lcw)
        grads["lru_conv_b"][l] = g_lcb.reshape(D_MODEL)
        grads["lru_wa"][l] = g_wa
        grads["lru_ba"][l] = g_ba.reshape(D_MODEL)
        grads["lru_wx"][l] = g_wx
        grads["lru_bx"][l] = g_bx.reshape(D_MODEL)
        grads["lru_lambda"][l] = g_lam.reshape(D_MODEL)
        grads["lru_norm_w"][l] = g_lnw.reshape(D_MODEL)
        al_row = pad_row(dn_A_log[l])
        if l == 0:
            started["pack_0"] = _exchange_start([_pack_layer(grads, 0, names=_PACK_LRU)], [False],
                                                "exchange_pack_0_start")
            al_row = al_row + started["pack_0"]["token"]
        (dq, dk, dv, ddz, dba, g_dcw3, g_al, g_dt, g_dnw) = _dn_bwd(
            proj, o_dn, states, dy, dcw[l], al_row, pad_row(dn_dt_bias[l]), row(dn_norm_w[l]), f"dn_bwd_{l}")
        g_dcw3 = g_dcw3.reshape(HEADS, 3, 4, HEAD_DIM)
        grads["dn_conv_w"][l] = jnp.concatenate([_heads_to_channels(g_dcw3[:, i]) for i in range(3)], axis=1)
        grads["dn_A_log"][l] = g_al[0, :HEADS]
        grads["dn_dt_bias"][l] = g_dt[0, :HEADS]
        grads["dn_norm_w"][l] = g_dnw.reshape(HEAD_DIM)
        dep = None
        pieces = [dlx, dlz, dq, dk, dv, ddz]
        wide = len(pieces) * D_MODEL
        dba = dba.astype(BF16)
        g_win_l = jnp.concatenate(
            [_matmul_tn_parts(hn, pieces, D_MODEL, D_MODEL, tm, f"in_proj_dw_{l}", BF16),
             _matmul(hn, dba, "tn", D_MODEL, HEAD_DIM, tm, f"in_proj_dw_gates_{l}", out_dtype=BF16)[:, :D_IN - wide]],
            axis=1)
        if l == 0:
            started[0] = _exchange_start([win_slots(g_win_l)], [True], "exchange_0_start")
            dep = started[0]["token_block"]
        dh = _matmul_nt_parts(pieces, dba, win[l], tm, D_MODEL, f"in_proj_dh_{l}", dep=dep)
        dx, g_nw = _rmsnorm_bwd(x_in, row(norm_w[l]), dh, dx, f"norm_bwd_{l}")
        grads["norm_w"][l] = g_nw.reshape(D_MODEL)
        if l > 0:
            tail = (d_final, loss_part) if l == depth - 1 else ()
            started[l] = _exchange_start([win_slots(g_win_l), wout_slots(g_wout_l), _pack_layer(grads, l, tail)],
                                         [True, True, False], f"exchange_{l}_start")
            token = started[l]["token"]

    def conv_slots(a):
        dd, r, cc = a.shape
        return jnp.transpose(a.reshape(dd, r, N_DEV, cc // N_DEV), (2, 0, 1, 3))

    small = _exchange_start(
        [conv_slots(jnp.stack(grads["lru_conv_w"])), conv_slots(jnp.stack(grads["dn_conv_w"])),
         _pack_layer(grads, 0, names=_PACK_LAST)], [True, True, False], "exchange_small_start")

    new = {}
    zero_row = jnp.zeros((1, HEAD_DIM), F32)

    def adamw_pack(parts, layer, names=_LAYERED, name="adamw_small"):
        tails = [(t, zero_row) if layer == depth - 1 else () for t in (final_norm_w, m_final_norm_w, v_final_norm_w)]
        return _adamw(parts, _pack_layer(weights, layer, tails[0], names), _pack_layer(mom_m, layer, tails[1], names),
                      _pack_layer(mom_v, layer, tails[2], names), f"{name}_{layer}", parts.shape[1])

    def adamw_w_in(parts, layer, into):
        return _adamw(parts, w_in, m_w_in, v_w_in, f"adamw_w_in_{layer}", 256, layer, into)

    def adamw_w_out(parts, layer, into):
        return _adamw(parts, w_out, m_w_out, v_w_out, f"adamw_w_out_{layer}", 256, layer, into)

    acc_in = acc_out = None
    packs = [None] * depth
    after = small["token_block"]
    for l in reversed(range(1, depth)):
        r_win, r_wout, r_pack = _exchange_wait(started[l], after, f"exchange_{l}_wait")
        acc_in = adamw_w_in(r_win, l, acc_in)
        acc_out = adamw_w_out(r_wout, l, acc_out)
        packs[l] = adamw_pack(r_pack, l)
        after = packs[l][0]
    (r_wout,) = _exchange_wait(started["w_out_0"], after, "exchange_w_out_0_wait")
    acc_out = adamw_w_out(r_wout, 0, acc_out)
    (r_win,) = _exchange_wait(started[0], acc_out[0], "exchange_0_wait")
    acc_in = adamw_w_in(r_win, 0, acc_in)
    (r_pack,) = _exchange_wait(started["pack_0"], acc_in[0], "exchange_pack_0_wait")
    packs[0] = adamw_pack(r_pack, 0, _PACK_LRU)
    r_lcw, r_dcw, r_last = _exchange_wait(small, packs[0][0], "exchange_small_wait")
    for name, parts in (("lru_conv_w", r_lcw), ("dn_conv_w", r_dcw)):
        w = weights[name]
        flat = (-1, w.shape[-1])
        outs = _adamw(parts.reshape((N_DEV,) + (w.size // w.shape[-1], w.shape[-1])), w.reshape(flat),
                      mom_m[name].reshape(flat), mom_v[name].reshape(flat), f"adamw_{name}", 8)
        new[name] = [a.reshape(w.shape) for a in outs]
    last_0 = adamw_pack(r_last, 0, _PACK_LAST, "adamw_last")
    new["w_in"] = list(acc_in)
    new["w_out"] = list(acc_out)
    for i in range(4):
        layers = [{**_unpack_layer(packs[0][i], weights, _PACK_LRU)[0],
                   **_unpack_layer(last_0[i], weights, _PACK_LAST)[0]}]
        layers += [_unpack_layer(packs[l][i], weights)[0] for l in range(1, depth)]
        for name in _LAYERED:
            new.setdefault(name, []).append(jnp.stack([layer[name] for layer in layers]))
    tail_at = _unpack_layer(packs[depth - 1][0], weights)[1]
    rows_final = D_MODEL // HEAD_DIM
    new["final_norm_w"] = [packs[depth - 1][i][tail_at:tail_at + rows_final].reshape(D_MODEL) for i in range(4)]
    loss = packs[depth - 1][0][tail_at + rows_final, 0]
    out = [loss, dx.reshape(x.shape)]
    for i in range(4):
        out += [new[name][i] for name in _WEIGHTS]
    return tuple(out)
```

```python
import functools

import jax
import jax.numpy as jnp
from jax import lax
from jax.experimental import pallas as pl
from jax.experimental.pallas import tpu as pltpu

F32 = jnp.float32
BF16 = jnp.bfloat16

N_DEV = 8
D_MODEL = 1024
HEADS = 8
HEAD_DIM = 128
CHUNK = 64
D_IN = 6160
D_IN_PAD = 6272
COL_LRU_X, COL_LRU_Z, COL_Q, COL_K, COL_V, COL_DN_Z, COL_BA = 0, 8, 16, 24, 32, 40, 48
LRU_C = 8.0
EPS = 1e-6
ADAM_LR, ADAM_B1, ADAM_B2, ADAM_EPS, ADAM_WD, ADAM_STEP = 0.001, 0.9, 0.999, 1e-08, 0.01, 10
TIME_BLOCK = 1024
PART_CHUNKS = 16
PART_CHUNKS_BWD = 16
VMEM_LIMIT = 56 * 1024 * 1024

NN = (((1,), (0,)), ((), ()))
NT = (((1,), (1,)), ((), ()))
TN = (((0,), (0,)), ((), ()))


B_NN = (((2,), (1,)), ((0,), (0,)))
B_NT = (((2,), (2,)), ((0,), (0,)))
B_TN = (((1,), (1,)), ((0,), (0,)))


def _split_bf16(x):
    hi = x.astype(BF16)
    return hi, (x - hi.astype(F32)).astype(BF16)


def _dot(a, b, dims, prec):
    if prec == "bf16":
        return lax.dot_general(a.astype(BF16), b.astype(BF16), dims, preferred_element_type=F32)
    a1, a2 = _split_bf16(a)
    b1, b2 = _split_bf16(b)
    dg = functools.partial(lax.dot_general, dimension_numbers=dims, preferred_element_type=F32)
    return dg(a1, b1) + (dg(a1, b2) + dg(a2, b1))


def _make_mm(prec, nn_dims, nt_dims, tn_dims):
    @jax.custom_vjp
    def nn(a, b):
        return _dot(a, b, nn_dims, prec)

    @jax.custom_vjp
    def nt(a, b):
        return _dot(a, b, nt_dims, prec)

    @jax.custom_vjp
    def tn(a, b):
        return _dot(a, b, tn_dims, prec)

    nn.defvjp(lambda a, b: (_dot(a, b, nn_dims, prec), (a, b)),
              lambda r, g: (_dot(g, r[1], nt_dims, prec), _dot(r[0], g, tn_dims, prec)))
    nt.defvjp(lambda a, b: (_dot(a, b, nt_dims, prec), (a, b)),
              lambda r, g: (_dot(g, r[1], nn_dims, prec), _dot(g, r[0], tn_dims, prec)))
    tn.defvjp(lambda a, b: (_dot(a, b, tn_dims, prec), (a, b)),
              lambda r, g: (_dot(r[1], g, nt_dims, prec), _dot(r[0], g, nn_dims, prec)))
    return nn, nt, tn


_NN_B, _NT_B, _TN_B = _make_mm("bf16", NN, NT, TN)
_BNN, _BNT, _BTN = _make_mm("bf16", B_NN, B_NT, B_TN)


def _unit_lower_inverse_steps(a, tick=lambda: None):
    n = a.shape[-1]
    eye = (lax.broadcasted_iota(jnp.int32, a.shape, 1) == lax.broadcasted_iota(jnp.int32, a.shape, 2)).astype(F32)
    dg = functools.partial(lax.dot_general, dimension_numbers=B_NN, preferred_element_type=F32)
    inv = eye - a
    pw = _dot(a, a, B_NN, "bf16x3")
    steps = n.bit_length() - 2
    for j in range(steps):
        i1, i2 = _split_bf16(inv)
        p1, p2 = _split_bf16(pw)
        square = j + 1 < steps
        by_hi = dg(jnp.concatenate([i1, i2, p1, p2] if square else [i1, i2], axis=1), p1)
        by_lo = dg(jnp.concatenate([i1, p1], axis=1) if square else i1, p2)
        inv = inv + (by_hi[:, :n] + (by_lo[:, :n] + by_hi[:, n:2 * n]))
        if square:
            pw = by_hi[:, 2 * n:3 * n] + (by_lo[:, n:] + by_hi[:, 3 * n:])
        tick()
    return inv


@jax.custom_vjp
def _unit_lower_inverse(a):
    return _unit_lower_inverse_steps(a)


def _uli_fwd(a):
    inv = _unit_lower_inverse(a)
    return inv, inv


def _uli_bwd(inv, g):
    return (-_dot(_dot(inv, g, B_TN, "bf16"), inv, B_NT, "bf16"),)


_unit_lower_inverse.defvjp(_uli_fwd, _uli_bwd)


def _rows2(y, m):
    return y[:, :m], y[:, m:]


@jax.custom_vjp
def _pair_nn(x1, x2, r):
    return _rows2(_dot(jnp.concatenate([x1, x2], axis=1), r, B_NN, "bf16"), x1.shape[1])


def _pair_nn_bwd(res, g):
    x1, x2, r = res
    g = jnp.concatenate(g, axis=1)
    dx1, dx2 = _rows2(_dot(g, r, B_NT, "bf16"), x1.shape[1])
    return dx1, dx2, _dot(jnp.concatenate([x1, x2], axis=1), g, B_TN, "bf16")


_pair_nn.defvjp(lambda x1, x2, r: (_pair_nn(x1, x2, r), (x1, x2, r)), _pair_nn_bwd)


@jax.custom_vjp
def _pair_nt(x1, x2, r):
    return _rows2(_dot(jnp.concatenate([x1, x2], axis=1), r, B_NT, "bf16"), x1.shape[1])


def _pair_nt_bwd(res, g):
    x1, x2, r = res
    g = jnp.concatenate(g, axis=1)
    dx1, dx2 = _rows2(_dot(g, r, B_NN, "bf16"), x1.shape[1])
    return dx1, dx2, _dot(g, jnp.concatenate([x1, x2], axis=1), B_TN, "bf16")


_pair_nt.defvjp(lambda x1, x2, r: (_pair_nt(x1, x2, r), (x1, x2, r)), _pair_nt_bwd)


@jax.custom_vjp
def _wide_nn(l, r1, r2):
    y = _dot(l, jnp.concatenate([r1, r2], axis=2), B_NN, "bf16")
    return y[:, :, :r1.shape[2]], y[:, :, r1.shape[2]:]


def _wide_nn_bwd(res, g):
    l, r1, r2 = res
    g = jnp.concatenate(g, axis=2)
    dr = _dot(l, g, B_TN, "bf16")
    return (_dot(g, jnp.concatenate([r1, r2], axis=2), B_NT, "bf16"), dr[:, :, :r1.shape[2]], dr[:, :, r1.shape[2]:])


_wide_nn.defvjp(lambda l, r1, r2: (_wide_nn(l, r1, r2), (l, r1, r2)), _wide_nn_bwd)


def _lower_ones(batch, n):
    shape = (batch, n, n)
    return (lax.broadcasted_iota(jnp.int32, shape, 1) >= lax.broadcasted_iota(jnp.int32, shape, 2)).astype(BF16)


@jax.custom_vjp
def _chunk_cumsum(g):
    tri = _lower_ones(g.shape[0], g.shape[1])
    g1, g2 = _split_bf16(g)
    g3 = (g - g1.astype(F32) - g2.astype(F32)).astype(BF16)
    dg = functools.partial(lax.dot_general, dimension_numbers=B_NN, preferred_element_type=F32)
    return dg(tri, g1) + (dg(tri, g2) + dg(tri, g3))


def _chunk_cumsum_bwd(_, ct):
    tri = _lower_ones(ct.shape[0], ct.shape[1])
    c1, c2 = _split_bf16(ct)
    dg = functools.partial(lax.dot_general, dimension_numbers=B_TN, preferred_element_type=F32)
    return (dg(tri, c1) + dg(tri, c2),)


_chunk_cumsum.defvjp(lambda g: (_chunk_cumsum(g), None), _chunk_cumsum_bwd)


def _expm1(x):
    small = x * (1.0 + x * (0.5 + x * (1.0 / 6 + x * (1.0 / 24 + x * (1.0 / 120 + x * (1.0 / 720))))))
    return jnp.where(jnp.abs(x) < 0.2, small, jnp.exp(x) - 1.0)


def _sigmoid(x):
    return 1.0 / (1.0 + jnp.exp(-x))


def _silu(x):
    return x * _sigmoid(x)


def _softplus(x):
    return jnp.maximum(x, 0.0) + jnp.log(1.0 + jnp.exp(-jnp.abs(x)))


def _rmsnorm(x, w):
    return x * lax.rsqrt(jnp.mean(x * x, axis=-1, keepdims=True) + EPS) * w


def _gated_norm(o, z, w):
    return o * lax.rsqrt(jnp.mean(o * o, axis=-1, keepdims=True) + EPS) * w * _silu(z)


def _lru_gates(xc, wa, ba, wx, bx, lam):
    r = _sigmoid(_NN_B(xc, wa) + ba)
    i = _sigmoid(_NN_B(xc, wx) + bx)
    log_a = -LRU_C * r * _softplus(-lam)
    a = jnp.exp(log_a)
    mult = jnp.sqrt(-_expm1(2.0 * log_a))
    return a, mult * (i * xc)


def _scan_forward(a, b, h0):
    rows = a.shape[0]
    row = lax.broadcasted_iota(jnp.int32, a.shape, 0)
    k = 1
    while k < rows:
        seen = row >= k
        b = jnp.where(seen, a * pltpu.roll(b, k, 0) + b, b)
        a = jnp.where(seen, a * pltpu.roll(a, k, 0), a)
        k *= 2
    return b + a * h0


def _scan_reverse(a, d, carry):
    rows = a.shape[0]
    row = lax.broadcasted_iota(jnp.int32, a.shape, 0)
    last = row == rows - 1
    c = jnp.where(last, 0.0, pltpu.roll(a, rows - 1, 0))
    d = d + jnp.where(last, carry, 0.0)
    k = 1
    while k < rows:
        seen = row < rows - k
        d = jnp.where(seen, d + c * pltpu.roll(d, rows - k, 0), d)
        c = jnp.where(seen, c * pltpu.roll(c, rows - k, 0), c)
        k *= 2
    return d


def _lane_pick(row, lane_index):
    lane = lax.broadcasted_iota(jnp.int32, row.shape, 1)
    return jnp.sum(jnp.where(lane == lane_index, row, 0.0), axis=-1, keepdims=True)


def _dn_prep(qc, kc, vc, ba, a_log_row, dt_row, head):
    q = _silu(qc)
    k = _silu(kc)
    v = _silu(vc)
    q = q * lax.rsqrt(jnp.sum(q * q, axis=-1, keepdims=True) + EPS) * (HEAD_DIM ** -0.5)
    k = k * lax.rsqrt(jnp.sum(k * k, axis=-1, keepdims=True) + EPS)
    beta = _sigmoid(_lane_pick(ba, head))
    g = -jnp.exp(_lane_pick(a_log_row, head)) * _softplus(_lane_pick(ba, HEADS + head) + _lane_pick(dt_row, head))
    return q, k, v, g, beta


def _dn_chunks_head(q, k, v, gcol, bcol, tick=None):
    n, c, d = q.shape
    row = lax.broadcasted_iota(jnp.int32, (n, c, c), 1)
    col = lax.broadcasted_iota(jnp.int32, (n, c, c), 2)
    g_wide = jnp.broadcast_to(gcol, (n, c, d))
    b_wide = jnp.broadcast_to(bcol, (n, c, d))
    gc = _chunk_cumsum(g_wide)
    gc_rows = gc[:, :, :c]
    decay = jnp.exp(jnp.where(row >= col, gc_rows - jnp.swapaxes(gc_rows, 1, 2), -1e30))
    kb = k * b_wide
    eg = jnp.exp(gc)
    if tick is not None:
        tick()
    kbk, qk = _pair_nt(kb, q, k)
    a = jnp.where(row > col, kbk * decay, 0.0)
    if tick is not None:
        tick()
    tinv = _unit_lower_inverse(a) if tick is None else _unit_lower_inverse_steps(a, tick)
    u, w = _wide_nn(tinv, v * b_wide, kb * eg)
    g_last = jnp.sum(g_wide, axis=1, keepdims=True)
    return u, w, qk * decay, q * eg, k * jnp.exp(g_last - gc), jnp.exp(g_last)


def _dn_chunks(q, k, v, gcol, bcol, states):
    u, w, attn, qe, kdec, eglast = _dn_chunks_head(q, k, v, gcol, bcol)
    w_st, qe_st = _pair_nn(w, qe, states)
    v_new = u - w_st
    o = qe_st + _BNN(attn, v_new)
    return (o, states * eglast + _BTN(kdec, v_new)), (w, attn, qe, kdec, eglast)


def _conv_taps(buf, head, cw, rows):
    acc = cw[0:1, :] * buf[head, pl.ds(5, rows), :]
    for j in range(1, 4):
        acc = acc + cw[j:j + 1, :] * buf[head, pl.ds(5 + j, rows), :]
    return acc


def _conv_backward(dbuf, dhead, xbuf, xhead, cw, dxc, rows):
    dbuf[dhead, pl.ds(0, rows), :] = dxc
    dx = cw[0:1, :] * dbuf[dhead, pl.ds(3, rows), :]
    for j in range(1, 4):
        dx = dx + cw[j:j + 1, :] * dbuf[dhead, pl.ds(3 - j, rows), :]
    dcw = jnp.concatenate(
        [jnp.sum(dxc * xbuf[xhead, pl.ds(5 + j, rows), :], axis=0, keepdims=True) for j in range(4)], axis=0)
    dbuf[dhead, pl.ds(rows, 8), :] = dbuf[dhead, pl.ds(0, 8), :]
    return dx, dcw


def _params(**kw):
    return pltpu.CompilerParams(vmem_limit_bytes=VMEM_LIMIT, **kw)


def _matmul(a, b, form, tm, tn, tk, name, add=None, out_dtype=F32, dep=None):
    if form == "nn":
        (m, kdim), (_, n) = a.shape, b.shape
        a_spec = pl.BlockSpec((tm, tk), lambda j, i, k: (i, k))
        b_spec = pl.BlockSpec((tk, tn), lambda j, i, k: (k, j))
        dims = NN
    elif form == "nt":
        (m, kdim), (n, _) = a.shape, b.shape
        a_spec = pl.BlockSpec((tm, tk), lambda j, i, k: (i, k))
        b_spec = pl.BlockSpec((tn, tk), lambda j, i, k: (j, k))
        dims = NT
    else:
        (kdim, m), (_, n) = a.shape, b.shape
        a_spec = pl.BlockSpec((tk, tm), lambda j, i, k: (k, i))
        b_spec = pl.BlockSpec((tk, tn), lambda j, i, k: (k, j))
        dims = TN
    assert m % tm == 0 and n % tn == 0 and kdim % tk == 0, (name, m, n, kdim, tm, tn, tk)
    ksteps = kdim // tk
    o_spec = pl.BlockSpec((tm, tn), lambda j, i, k: (i, j))
    has_add = add is not None
    extra = [] if dep is None else [dep]

    def body(*refs):
        a_ref, b_ref = refs[:2]
        c_ref = refs[2] if has_add else None
        o_ref, acc = refs[-2:]
        k = pl.program_id(2)

        @pl.when(k == 0)
        def _():
            acc[...] = c_ref[...] if has_add else jnp.zeros_like(acc)

        acc[...] += lax.dot_general(a_ref[...].astype(BF16), b_ref[...].astype(BF16), dims,
                                    preferred_element_type=F32)

        @pl.when(k == ksteps - 1)
        def _():
            o_ref[...] = acc[...].astype(o_ref.dtype)

    in_specs = [a_spec, b_spec] + ([o_spec] if has_add else []) + [pl.BlockSpec((8, HEAD_DIM), lambda j, i, k: (0, 0))
                                                                   for _ in extra]
    args = (a, b) + ((add,) if has_add else ()) + tuple(extra)
    return pl.pallas_call(
        body, name=name, grid=(n // tn, m // tm, ksteps), in_specs=in_specs, out_specs=o_spec,
        out_shape=jax.ShapeDtypeStruct((m, n), out_dtype), scratch_shapes=[pltpu.VMEM((tm, tn), F32)],
        compiler_params=_params(dimension_semantics=("parallel", "parallel", "arbitrary")),
    )(*args)


def _matmul_nt_parts(parts, narrow, w, tm, tn, name, dep=None):
    m, c = parts[0].shape
    c2 = narrow.shape[1]
    n = w.shape[0]
    count = len(parts)
    assert m % tm == 0 and n % tn == 0 and all(p.shape == (m, c) for p in parts) and (count * c) % c2 == 0, (name, m, n)
    extra = [] if dep is None else [dep]

    def body(*refs):
        part_refs, narrow_ref, w_ref, w2_ref = refs[:count], refs[count], refs[count + 1], refs[count + 2]
        o_ref, acc = refs[-2:]
        k = pl.program_id(2)

        @pl.when(k == 0)
        def _():
            acc[...] = jnp.zeros_like(acc)

        for p in range(count):
            @pl.when(k == p)
            def _(p=p):
                acc[...] += lax.dot_general(part_refs[p][...].astype(BF16), w_ref[...].astype(BF16), NT,
                                            preferred_element_type=F32)

        @pl.when(k == count)
        def _():
            o_ref[...] = acc[...] + lax.dot_general(narrow_ref[...].astype(BF16), w2_ref[...].astype(BF16), NT,
                                                    preferred_element_type=F32)

    in_specs = ([pl.BlockSpec((tm, c), lambda j, i, k: (i, 0))] * count
                + [pl.BlockSpec((tm, c2), lambda j, i, k: (i, 0)),
                   pl.BlockSpec((tn, c), lambda j, i, k: (j, jnp.minimum(k, count - 1))),
                   pl.BlockSpec((tn, c2), lambda j, i, k: (j, count * c // c2))]
                + [pl.BlockSpec((8, HEAD_DIM), lambda j, i, k: (0, 0)) for _ in extra])
    return pl.pallas_call(
        body, name=name, grid=(n // tn, m // tm, count + 1), in_specs=in_specs,
        out_specs=pl.BlockSpec((tm, tn), lambda j, i, k: (i, j)),
        out_shape=jax.ShapeDtypeStruct((m, n), F32), scratch_shapes=[pltpu.VMEM((tm, tn), F32)],
        compiler_params=_params(dimension_semantics=("parallel", "parallel", "arbitrary")),
    )(*parts, narrow, w, w, *extra)


def _matmul_tn_parts(a, parts, tm, tn, tk, name, out_dtype):
    kdim, m = a.shape
    c = parts[0].shape[1]
    count = len(parts)
    per = c // tn
    assert m % tm == 0 and c % tn == 0 and kdim % tk == 0 and all(p.shape == (kdim, c) for p in parts), (name, m, c)
    ksteps = kdim // tk

    def body(*refs):
        a_ref, part_refs = refs[0], refs[1:1 + count]
        o_ref, acc = refs[-2:]
        j, k = pl.program_id(0), pl.program_id(2)

        @pl.when(k == 0)
        def _():
            acc[...] = jnp.zeros_like(acc)

        for p in range(count):
            @pl.when(j // per == p)
            def _(p=p):
                acc[...] += lax.dot_general(a_ref[...].astype(BF16), part_refs[p][...].astype(BF16), TN,
                                            preferred_element_type=F32)

        @pl.when(k == ksteps - 1)
        def _():
            o_ref[...] = acc[...].astype(o_ref.dtype)

    def part_spec(p):
        return pl.BlockSpec((tk, tn), lambda j, i, k: (jnp.where(j // per == p, k, 0), jnp.where(j // per == p, j % per, 0)))

    return pl.pallas_call(
        body, name=name, grid=(count * per, m // tm, ksteps),
        in_specs=[pl.BlockSpec((tk, tm), lambda j, i, k: (k, i))] + [part_spec(p) for p in range(count)],
        out_specs=pl.BlockSpec((tm, tn), lambda j, i, k: (i, j)),
        out_shape=jax.ShapeDtypeStruct((m, count * c), out_dtype), scratch_shapes=[pltpu.VMEM((tm, tn), F32)],
        compiler_params=_params(dimension_semantics=("parallel", "parallel", "arbitrary")),
    )(a, *parts)


def _rmsnorm_fwd(x, w_row, name):
    s = x.shape[0]
    tb = min(TIME_BLOCK, s)

    def body(x_ref, w_ref, o_ref):
        o_ref[...] = _rmsnorm(x_ref[...], w_ref[...]).astype(BF16)

    return pl.pallas_call(
        body, name=name, grid=(s // tb,),
        in_specs=[pl.BlockSpec((tb, D_MODEL), lambda i: (i, 0)), pl.BlockSpec((1, D_MODEL), lambda i: (0, 0))],
        out_specs=pl.BlockSpec((tb, D_MODEL), lambda i: (i, 0)),
        out_shape=jax.ShapeDtypeStruct((s, D_MODEL), BF16), compiler_params=_params(),
    )(x, w_row)


def _rmsnorm_bwd(x, w_row, dh, dres, name):
    s = x.shape[0]
    tb = min(TIME_BLOCK, s)

    def body(x_ref, w_ref, dh_ref, dres_ref, dx_ref, dw_ref):
        _, vjp = jax.vjp(_rmsnorm, x_ref[...], w_ref[...])
        dx, dw = vjp(dh_ref[...])
        dx_ref[...] = dres_ref[...] + dx

        @pl.when(pl.program_id(0) == 0)
        def _():
            dw_ref[...] = jnp.zeros_like(dw_ref)

        dw_ref[...] += dw

    row = pl.BlockSpec((tb, D_MODEL), lambda i: (i, 0))
    vec = pl.BlockSpec((1, D_MODEL), lambda i: (0, 0))
    return pl.pallas_call(
        body, name=name, grid=(s // tb,), in_specs=[row, vec, row, row], out_specs=[row, vec],
        out_shape=[jax.ShapeDtypeStruct((s, D_MODEL), F32), jax.ShapeDtypeStruct((1, D_MODEL), F32)],
        compiler_params=_params(),
    )(x, w_row, dh, dres)


def _final_loss(x, w_row, target, name):
    s = x.shape[0]
    tb = min(TIME_BLOCK, s)

    def loss_fn(xv, wv, tv):
        err = _rmsnorm(xv, wv) - tv
        return 0.5 * jnp.sum(jnp.sum(err * err, axis=-1, keepdims=True), axis=0, keepdims=True) * (1.0 / D_MODEL)

    def body(x_ref, w_ref, t_ref, loss_ref, dx_ref, dw_ref):
        tv = t_ref[...]
        loss, vjp = jax.vjp(lambda xv, wv: loss_fn(xv, wv, tv), x_ref[...], w_ref[...])
        dx, dw = vjp(jnp.ones((1, 1), F32))
        dx_ref[...] = dx

        @pl.when(pl.program_id(0) == 0)
        def _():
            dw_ref[...] = jnp.zeros_like(dw_ref)
            loss_ref[...] = jnp.zeros_like(loss_ref)

        dw_ref[...] += dw
        loss_ref[...] += jnp.broadcast_to(loss, loss_ref.shape)

    row = pl.BlockSpec((tb, D_MODEL), lambda i: (i, 0))
    vec = pl.BlockSpec((1, D_MODEL), lambda i: (0, 0))
    return pl.pallas_call(
        body, name=name, grid=(s // tb,), in_specs=[row, vec, row],
        out_specs=[pl.BlockSpec((1, HEAD_DIM), lambda i: (0, 0)), row, vec],
        out_shape=[jax.ShapeDtypeStruct((1, HEAD_DIM), F32), jax.ShapeDtypeStruct((s, D_MODEL), F32),
                   jax.ShapeDtypeStruct((1, D_MODEL), F32)],
        compiler_params=_params(),
    )(x, w_row, target)


def _head_specs(tb, time_of):
    def col(off):
        return pl.BlockSpec((tb, HEAD_DIM), lambda t, h: (time_of(t), off + h))
    return col


def _vec_spec():
    return pl.BlockSpec((1, HEAD_DIM), lambda t, h: (0, h))


def _lru_fwd(proj, conv_w, conv_b, wa, ba, wx, bx, lam, nw, name):
    s = proj.shape[0]
    tb = min(TIME_BLOCK, s)
    nt = s // tb
    col = _head_specs(tb, lambda t: t)

    def body(x_ref, z_ref, cw_ref, cb_ref, wa_ref, ba_ref, wx_ref, bx_ref, lam_ref, nw_ref,
             y_ref, hs_ref, xbuf, hcar):
        t, h = pl.program_id(0), pl.program_id(1)

        @pl.when(t == 0)
        def _():
            xbuf[h, pl.ds(0, 8), :] = jnp.zeros((8, HEAD_DIM), F32)
            hcar[h] = jnp.zeros((8, HEAD_DIM), F32)

        xbuf[h, pl.ds(8, tb), :] = x_ref[...]
        xc = _conv_taps(xbuf, h, cw_ref[...], tb) + cb_ref[...]
        a, b = _lru_gates(xc, wa_ref[...], ba_ref[...], wx_ref[...], bx_ref[...], lam_ref[...])
        hs_ref[...] = _scan_forward(a, b, hcar[h, pl.ds(0, 1), :])
        hcar[h, pl.ds(0, 1), :] = hs_ref[pl.ds(tb - 1, 1), :]
        xbuf[h, pl.ds(0, 8), :] = xbuf[h, pl.ds(tb, 8), :]
        y_ref[...] = _gated_norm(hs_ref[...], z_ref[...], nw_ref[...]).astype(BF16)

    vec = _vec_spec()
    return pl.pallas_call(
        body, name=name, grid=(nt, HEADS),
        in_specs=[col(COL_LRU_X), col(COL_LRU_Z), pl.BlockSpec((4, HEAD_DIM), lambda t, h: (0, h)), vec,
                  pl.BlockSpec((None, HEAD_DIM, HEAD_DIM), lambda t, h: (h, 0, 0)), vec,
                  pl.BlockSpec((None, HEAD_DIM, HEAD_DIM), lambda t, h: (h, 0, 0)), vec, vec, vec],
        out_specs=[col(0), col(0)],
        out_shape=[jax.ShapeDtypeStruct((s, 2 * D_MODEL), BF16), jax.ShapeDtypeStruct((s, D_MODEL), F32)],
        scratch_shapes=[pltpu.VMEM((HEADS, tb + 8, HEAD_DIM), F32), pltpu.VMEM((HEADS, 8, HEAD_DIM), F32)],
        compiler_params=_params(dimension_semantics=("arbitrary", "arbitrary")),
    )(proj, proj, conv_w, conv_b, wa, ba, wx, bx, lam, nw)


def _halo_spec(tb, nt, off):
    per = tb // 8
    return pl.BlockSpec((8, HEAD_DIM), lambda t, h: (jnp.maximum((nt - 1 - t) * per - 1, 0), off + h))


def _lru_bwd(proj, hs, dy, conv_w, conv_b, wa, ba, wx, bx, lam, nw, name):
    s = proj.shape[0]
    tb = min(TIME_BLOCK, s)
    nt = s // tb
    col = _head_specs(tb, lambda t: nt - 1 - t)

    def body(x_ref, xh_ref, z_ref, hs_ref, hh_ref, dy_ref, cw_ref, cb_ref, wa_ref, ba_ref, wx_ref, bx_ref,
             lam_ref, nw_ref, dx_ref, dz_ref, dcw_ref, dcb_ref, dwa_ref, dba_ref, dwx_ref, dbx_ref, dlam_ref,
             dnw_ref, xbuf, hbuf, dbuf, gcar):
        t, h = pl.program_id(0), pl.program_id(1)
        first_block = t == nt - 1

        @pl.when(t == 0)
        def _():
            dbuf[h, pl.ds(tb, 8), :] = jnp.zeros((8, HEAD_DIM), F32)
            gcar[h] = jnp.zeros((8, HEAD_DIM), F32)
            dcw_ref[h] = jnp.zeros((4, HEAD_DIM), F32)
            dwa_ref[h] = jnp.zeros((HEAD_DIM, HEAD_DIM), F32)
            dwx_ref[h] = jnp.zeros((HEAD_DIM, HEAD_DIM), F32)
            for ref in (dcb_ref, dba_ref, dbx_ref, dlam_ref, dnw_ref):
                ref[h] = jnp.zeros((1, HEAD_DIM), F32)

        keep = jnp.where(first_block, 0.0, 1.0)
        xbuf[0, pl.ds(0, 8), :] = xh_ref[...] * keep
        xbuf[0, pl.ds(8, tb), :] = x_ref[...]
        hbuf[pl.ds(0, 8), :] = hh_ref[...] * keep
        hbuf[pl.ds(8, tb), :] = hs_ref[...]
        cw = cw_ref[...]
        xc = _conv_taps(xbuf, 0, cw, tb) + cb_ref[...]
        (a, _), gates_vjp = jax.vjp(_lru_gates, xc, wa_ref[...], ba_ref[...], wx_ref[...], bx_ref[...], lam_ref[...])
        _, norm_vjp = jax.vjp(_gated_norm, hs_ref[...], z_ref[...], nw_ref[...])
        dh, dz, dnw = norm_vjp(dy_ref[...])
        dz_ref[...] = dz.astype(dz_ref.dtype)
        g = _scan_reverse(a, dh, gcar[h, pl.ds(0, 1), :])
        gcar[h, pl.ds(0, 1), :] = a[0:1, :] * g[0:1, :]
        dxc, dwa, dba, dwx, dbx, dlam = gates_vjp((g * hbuf[pl.ds(7, tb), :], g))
        dx, dcw = _conv_backward(dbuf, h, xbuf, 0, cw, dxc, tb)
        dx_ref[...] = dx.astype(dx_ref.dtype)
        dcw_ref[h] += dcw
        dcb_ref[h] += jnp.sum(dxc, axis=0, keepdims=True)
        dwa_ref[h] += dwa
        dwx_ref[h] += dwx
        dba_ref[h] += dba
        dbx_ref[h] += dbx
        dlam_ref[h] += dlam
        dnw_ref[h] += dnw

    vec = _vec_spec()
    mat = pl.BlockSpec((None, HEAD_DIM, HEAD_DIM), lambda t, h: (h, 0, 0))

    def whole(shape):
        return pl.BlockSpec(shape, lambda t, h: (0,) * len(shape))

    head_vec = jax.ShapeDtypeStruct((HEADS, 1, HEAD_DIM), F32)
    head_mat = jax.ShapeDtypeStruct((HEADS, HEAD_DIM, HEAD_DIM), F32)
    return pl.pallas_call(
        body, name=name, grid=(nt, HEADS),
        in_specs=[col(COL_LRU_X), _halo_spec(tb, nt, COL_LRU_X), col(COL_LRU_Z), col(0), _halo_spec(tb, nt, 0), col(0),
                  pl.BlockSpec((4, HEAD_DIM), lambda t, h: (0, h)), vec, mat, vec, mat, vec, vec, vec],
        out_specs=[col(0), col(0), whole((HEADS, 4, HEAD_DIM)), whole((HEADS, 1, HEAD_DIM)),
                   whole((HEADS, HEAD_DIM, HEAD_DIM)), whole((HEADS, 1, HEAD_DIM)),
                   whole((HEADS, HEAD_DIM, HEAD_DIM)), whole((HEADS, 1, HEAD_DIM)), whole((HEADS, 1, HEAD_DIM)),
                   whole((HEADS, 1, HEAD_DIM))],
        out_shape=[jax.ShapeDtypeStruct((s, D_MODEL), BF16), jax.ShapeDtypeStruct((s, D_MODEL), BF16),
                   jax.ShapeDtypeStruct((HEADS, 4, HEAD_DIM), F32), head_vec, head_mat, head_vec, head_mat, head_vec,
                   head_vec, head_vec],
        scratch_shapes=[pltpu.VMEM((1, tb + 8, HEAD_DIM), F32), pltpu.VMEM((tb + 8, HEAD_DIM), F32),
                        pltpu.VMEM((HEADS, tb + 8, HEAD_DIM), F32), pltpu.VMEM((HEADS, 8, HEAD_DIM), F32)],
        compiler_params=_params(dimension_semantics=("arbitrary", "arbitrary")),
    )(proj, proj, proj, hs, hs, dy, conv_w, conv_b, wa, ba, wx, bx, lam, nw)


def _dn_fwd(proj, y, conv_w, a_log_row, dt_row, nw, name):
    s = proj.shape[0]
    tb = min(TIME_BLOCK, s)
    nt = s // tb
    nchunk = tb // CHUNK
    part = min(PART_CHUNKS, nchunk)
    assert nchunk % part == 0, (nchunk, part)
    col = _head_specs(tb, lambda t: t)

    def body(q_ref, k_ref, v_ref, z_ref, ba_ref, cwq_ref, cwk_ref, cwv_ref, al_ref, dt_ref, nw_ref, y_in_ref,
             y_ref, o_ref, st_ref, xbuf, state):
        t, h = pl.program_id(0), pl.program_id(1)

        @pl.when(t == 0)
        def _():
            for i in range(3):
                xbuf[3 * h + i, pl.ds(0, 8), :] = jnp.zeros((8, HEAD_DIM), F32)
            state[h] = jnp.zeros((HEAD_DIM, HEAD_DIM), F32)

        conv = []
        for i, (ref, cw_ref) in enumerate(((q_ref, cwq_ref), (k_ref, cwk_ref), (v_ref, cwv_ref))):
            xbuf[3 * h + i, pl.ds(8, tb), :] = ref[...]
            conv.append(_conv_taps(xbuf, 3 * h + i, cw_ref[...], tb))
            xbuf[3 * h + i, pl.ds(0, 8), :] = xbuf[3 * h + i, pl.ds(tb, 8), :]
        q, k, v, g, beta = _dn_prep(conv[0], conv[1], conv[2], ba_ref[...], al_ref[...], dt_ref[...], h)
        def chunks(a):
            return a.reshape(nchunk, CHUNK, a.shape[-1])

        qs, ks, vs, gs, bs = chunks(q), chunks(k), chunks(v), chunks(g), chunks(beta)
        carried = [state[h]]
        pending = []

        def tick():
            if pending:
                pending.pop(0)()

        def chain_step(lo, c, eglast, kdec_w_u):
            def step():
                st_ref[lo + c] = carried[0]
                carried[0] = (carried[0] * eglast[c] - _NN_B(kdec_w_u[c, :, :HEAD_DIM], carried[0])
                              + kdec_w_u[c, :, HEAD_DIM:])
            return step

        def outputs(lo, qe, attn_w_u):
            def step():
                o = (_dot(qe - attn_w_u[:, :, :HEAD_DIM], st_ref[pl.ds(lo, part)], B_NN, "bf16")
                     + attn_w_u[:, :, HEAD_DIM:])
                o_ref[pl.ds(lo * CHUNK, part * CHUNK), :] = o.reshape(part * CHUNK, HEAD_DIM)
            return step

        for lo in range(0, nchunk, part):
            sl = slice(lo, lo + part)
            u, w, attn, qe, kdec, eglast = _dn_chunks_head(qs[sl], ks[sl], vs[sl], gs[sl], bs[sl], tick)
            w_u = jnp.concatenate([w, u], axis=2)
            kdec_w_u = _dot(kdec, w_u, B_TN, "bf16")
            tick()
            attn_w_u = _dot(attn, w_u, B_NN, "bf16")
            while pending:
                tick()
            pending += [chain_step(lo, c, eglast, kdec_w_u) for c in range(part)] + [outputs(lo, qe, attn_w_u)]
        while pending:
            tick()
        st = carried[0]
        state[h] = st
        y_ref[...] = _gated_norm(o_ref[...], z_ref[...], nw_ref[...]).astype(BF16)

    def cw_spec(off):
        return pl.BlockSpec((4, HEAD_DIM), lambda t, h: (0, off + h))

    row128 = pl.BlockSpec((1, HEAD_DIM), lambda t, h: (0, 0))
    return pl.pallas_call(
        body, name=name, grid=(nt, HEADS),
        in_specs=[col(COL_Q), col(COL_K), col(COL_V), col(COL_DN_Z),
                  pl.BlockSpec((tb, HEAD_DIM), lambda t, h: (t, COL_BA)),
                  cw_spec(0), cw_spec(HEADS), cw_spec(2 * HEADS), row128, row128, row128,
                  pl.BlockSpec(memory_space=pl.ANY)],
        out_specs=[col(HEADS), col(0), pl.BlockSpec((None, nchunk, HEAD_DIM, HEAD_DIM), lambda t, h: (h, t, 0, 0))],
        out_shape=[jax.ShapeDtypeStruct((s, 2 * D_MODEL), BF16), jax.ShapeDtypeStruct((s, D_MODEL), F32),
                   jax.ShapeDtypeStruct((HEADS, s // CHUNK, HEAD_DIM, HEAD_DIM), F32)],
        input_output_aliases={11: 0},
        scratch_shapes=[pltpu.VMEM((3 * HEADS, tb + 8, HEAD_DIM), F32), pltpu.VMEM((HEADS, HEAD_DIM, HEAD_DIM), F32)],
        compiler_params=_params(dimension_semantics=("arbitrary", "arbitrary")),
    )(proj, proj, proj, proj, proj, conv_w, conv_w, conv_w, a_log_row, dt_row, nw, y)


def _dn_bwd(proj, o, states, dy, conv_w, a_log_row, dt_row, nw, name):
    s = proj.shape[0]
    tb = min(TIME_BLOCK, s)
    nt = s // tb
    nchunk = tb // CHUNK
    part = min(PART_CHUNKS_BWD, nchunk)
    assert nchunk % part == 0, (nchunk, part)
    col = _head_specs(tb, lambda t: nt - 1 - t)

    def body(q_ref, qh_ref, k_ref, kh_ref, v_ref, vh_ref, z_ref, ba_ref, o_ref, st_ref, dy_ref,
             cwq_ref, cwk_ref, cwv_ref, al_ref, dt_ref, nw_ref,
             dq_ref, dk_ref, dv_ref, dz_ref, dba_ref, dcw_ref, dal_ref, ddt_ref, dnw_ref,
             xbuf, dbuf, dstate, dst_s):
        t, h = pl.program_id(0), pl.program_id(1)
        first_block = t == nt - 1

        @pl.when(t == 0)
        def _():
            for i in range(3):
                dbuf[3 * h + i, pl.ds(tb, 8), :] = jnp.zeros((8, HEAD_DIM), F32)
                dcw_ref[3 * h + i] = jnp.zeros((4, HEAD_DIM), F32)
            dstate[h] = jnp.zeros((HEAD_DIM, HEAD_DIM), F32)

        @pl.when((t == 0) & (h == 0))
        def _():
            for ref in (dal_ref, ddt_ref, dnw_ref):
                ref[...] = jnp.zeros_like(ref)

        keep = jnp.where(first_block, 0.0, 1.0)
        cws = (cwq_ref[...], cwk_ref[...], cwv_ref[...])
        conv = []
        for i, (ref, halo) in enumerate(((q_ref, qh_ref), (k_ref, kh_ref), (v_ref, vh_ref))):
            xbuf[i, pl.ds(0, 8), :] = halo[...] * keep
            xbuf[i, pl.ds(8, tb), :] = ref[...]
            conv.append(_conv_taps(xbuf, i, cws[i], tb))
        (q, k, v, g, beta), prep_vjp = jax.vjp(
            lambda qc, kc, vc, ba, al, dt: _dn_prep(qc, kc, vc, ba, al, dt, h),
            conv[0], conv[1], conv[2], ba_ref[...], al_ref[...], dt_ref[...])
        _, norm_vjp = jax.vjp(_gated_norm, o_ref[...], z_ref[...], nw_ref[...])
        do, dz, dnw = norm_vjp(dy_ref[...])
        dz_ref[...] = dz.astype(dz_ref.dtype)
        dnw_ref[...] += dnw

        def chunks(a):
            return a.reshape(nchunk, CHUNK, a.shape[-1])

        do, qs, ks, vs, gs, bs = chunks(do), chunks(q), chunks(k), chunks(v), chunks(g), chunks(beta)
        dst = dstate[h]
        cotangents = []
        for lo in reversed(range(0, nchunk, part)):
            sl = slice(lo, lo + part)
            _, chunks_vjp, (w, attn, qe, kdec, eglast) = jax.vjp(
                _dn_chunks, qs[sl], ks[sl], vs[sl], gs[sl], bs[sl], st_ref[pl.ds(lo, part)], has_aux=True)
            kdec_w = _dot(kdec, w, B_TN, "bf16")
            fixed = _dot(qe, do[sl], B_TN, "bf16") - _dot(w, _dot(attn, do[sl], B_TN, "bf16"), B_TN, "bf16")
            for c in reversed(range(part)):
                dst_s[lo + c] = dst
                dst = dst * eglast[c] - _dot(kdec_w[c], dst, TN, "bf16") + fixed[c]
            cotangents.insert(0, chunks_vjp((do[sl], dst_s[pl.ds(lo, part)])))
        dstate[h] = dst
        dq, dk, dv, dg, db = [jnp.concatenate([ct[i] for ct in cotangents], axis=0) for i in range(5)]

        def rows(a):
            return a.reshape(tb, a.shape[-1])

        dqc, dkc, dvc, dba, dal, ddt = prep_vjp((rows(dq), rows(dk), rows(dv), rows(dg), rows(db)))
        for i, (dxc, out) in enumerate(((dqc, dq_ref), (dkc, dk_ref), (dvc, dv_ref))):
            dx, dcw = _conv_backward(dbuf, 3 * h + i, xbuf, i, cws[i], dxc, tb)
            out[...] = dx.astype(out.dtype)
            dcw_ref[3 * h + i] += dcw
        dal_ref[...] += dal
        ddt_ref[...] += ddt

        @pl.when(h == 0)
        def _():
            dba_ref[...] = dba.astype(dba_ref.dtype)

        @pl.when(h > 0)
        def _():
            dba_ref[...] += dba.astype(dba_ref.dtype)

    def cw_spec(off):
        return pl.BlockSpec((4, HEAD_DIM), lambda t, h: (0, off + h))

    def whole(shape):
        return pl.BlockSpec(shape, lambda t, h: (0,) * len(shape))

    row128 = whole((1, HEAD_DIM))
    blk = (tb, HEAD_DIM)
    act = jax.ShapeDtypeStruct((s, D_MODEL), BF16)
    row_out = jax.ShapeDtypeStruct((1, HEAD_DIM), F32)
    return pl.pallas_call(
        body, name=name, grid=(nt, HEADS),
        in_specs=[col(COL_Q), _halo_spec(tb, nt, COL_Q), col(COL_K), _halo_spec(tb, nt, COL_K),
                  col(COL_V), _halo_spec(tb, nt, COL_V), col(COL_DN_Z),
                  pl.BlockSpec(blk, lambda t, h: (nt - 1 - t, COL_BA)), col(0),
                  pl.BlockSpec((None, nchunk, HEAD_DIM, HEAD_DIM), lambda t, h: (h, nt - 1 - t, 0, 0)), col(HEADS),
                  cw_spec(0), cw_spec(HEADS), cw_spec(2 * HEADS), row128, row128, row128],
        out_specs=[col(0), col(0), col(0), col(0), pl.BlockSpec(blk, lambda t, h: (nt - 1 - t, 0)),
                   whole((3 * HEADS, 4, HEAD_DIM)), row128, row128, row128],
        out_shape=[act, act, act, act, jax.ShapeDtypeStruct((s, HEAD_DIM), F32),
                   jax.ShapeDtypeStruct((3 * HEADS, 4, HEAD_DIM), F32), row_out, row_out, row_out],
        scratch_shapes=[pltpu.VMEM((3, tb + 8, HEAD_DIM), F32), pltpu.VMEM((3 * HEADS, tb + 8, HEAD_DIM), F32),
                        pltpu.VMEM((HEADS, HEAD_DIM, HEAD_DIM), F32), pltpu.VMEM((nchunk, HEAD_DIM, HEAD_DIM), F32)],
        compiler_params=_params(dimension_semantics=("arbitrary", "arbitrary")),
    )(proj, proj, proj, proj, proj, proj, proj, proj, o, states, dy, conv_w, conv_w, conv_w, a_log_row, dt_row, nw)


def _mesh_position():
    x, y, c = lax.axis_index("x"), lax.axis_index("y"), lax.axis_index("c")
    return x, y, c, 4 * x + 2 * y + c


def _peer(k, x, y, c):
    px = 1 - x if k & 4 else x
    py = 1 - y if k & 2 else y
    pc = 1 - c if k & 1 else c
    return (px, py, pc), 4 * px + 2 * py + pc


def _exchange_copies(ins, lands, scatter, send_sems, recv_sems, receives=True):
    x, y, c, me = _mesh_position()
    sends, recvs = [], []
    for i, (src, land) in enumerate(zip(ins, lands)):
        for k in range(1, N_DEV):
            peer, peer_id = _peer(k, x, y, c)
            sem = i * (N_DEV - 1) + k - 1
            for dst, out in ((me, sends), (peer_id, recvs)) if receives else ((me, sends),):
                out.append(pltpu.make_async_remote_copy(
                    src_ref=src.at[peer_id] if scatter[i] else src, dst_ref=land.at[dst],
                    send_sem=send_sems.at[sem], recv_sem=recv_sems.at[sem],
                    device_id=peer, device_id_type=pl.DeviceIdType.MESH))
    return sends, recvs


def _landing_shape(a, scatter):
    return a.shape if scatter else (N_DEV,) + a.shape


def _direct_exchange(arrays, scatter, name):
    n = len(arrays)
    out_shapes = [jax.ShapeDtypeStruct(_landing_shape(a, sc), a.dtype) for a, sc in zip(arrays, scatter)]

    def body(*refs):
        ins, outs = refs[:n], refs[n:2 * n]
        send_sems, recv_sems, local_sems = refs[2 * n:]
        me = _mesh_position()[3]
        local = [pltpu.make_async_copy(ins[i].at[me] if scatter[i] else ins[i], outs[i].at[me], local_sems.at[i])
                 for i in range(n)]
        sends, recvs = _exchange_copies(ins, outs, scatter, send_sems, recv_sems)
        for cp in local + sends:
            cp.start()
        for cp in recvs:
            cp.wait_recv()
        for cp in sends:
            cp.wait_send()
        for cp in local:
            cp.wait()

    hbm = pl.BlockSpec(memory_space=pl.ANY)
    return pl.pallas_call(
        body, name=name, in_specs=[hbm] * n, out_specs=[hbm] * n, out_shape=out_shapes,
        scratch_shapes=[pltpu.SemaphoreType.DMA((n * (N_DEV - 1),)), pltpu.SemaphoreType.DMA((n * (N_DEV - 1),)),
                        pltpu.SemaphoreType.DMA((n,))],
    )(*arrays)


def _two_level_gather(arrays, name):
    n = len(arrays)
    per = N_DEV - 1

    def body(*refs):
        ins, outs = refs[:n], refs[n:2 * n]
        send_sems, recv_sems, local_sems = refs[2 * n:]
        x, y, c, me = _mesh_position()
        sibling = (x, y, 1 - c)
        chips = [(1 - x, y), (x, 1 - y), (1 - x, 1 - y)]

        def copy(i, k, block, to, src=None):
            slot = outs[i].at[4 * block[0] + 2 * block[1] + block[2]]
            return pltpu.make_async_remote_copy(
                src_ref=slot if src is None else src, dst_ref=slot,
                send_sem=send_sems.at[i * per + k], recv_sem=recv_sems.at[i * per + k],
                device_id=to, device_id_type=pl.DeviceIdType.MESH)

        local = [pltpu.make_async_copy(ins[i], outs[i].at[me], local_sems.at[i]) for i in range(n)]
        first = []
        for i in range(n):
            first.append(copy(i, 0, (x, y, c), sibling, src=ins[i]))
            first += [copy(i, 1 + j, (x, y, c), (*chip, c), src=ins[i]) for j, chip in enumerate(chips)]
        for cp in local + first:
            cp.start()
        passed = []
        for i in range(n):
            for j, chip in enumerate(chips):
                copy(i, 1 + j, (*chip, c), (x, y, c)).wait_recv()
                passed.append(copy(i, 4 + j, (*chip, c), sibling))
                passed[-1].start()
        for i in range(n):
            copy(i, 0, sibling, (x, y, c)).wait_recv()
            for j, chip in enumerate(chips):
                copy(i, 4 + j, (*chip, 1 - c), (x, y, c)).wait_recv()
        for cp in first + passed:
            cp.wait_send()
        for cp in local:
            cp.wait()

    hbm = pl.BlockSpec(memory_space=pl.ANY)
    return pl.pallas_call(
        body, name=name, in_specs=[hbm] * n, out_specs=[hbm] * n,
        out_shape=[jax.ShapeDtypeStruct((N_DEV,) + a.shape, a.dtype) for a in arrays],
        scratch_shapes=[pltpu.SemaphoreType.DMA((n * per,)), pltpu.SemaphoreType.DMA((n * per,)),
                        pltpu.SemaphoreType.DMA((n,))],
    )(*arrays)


_HBM = pl.BlockSpec(memory_space=pltpu.HBM)
_SEM = pl.BlockSpec(memory_space=pltpu.SEMAPHORE)
_DATAFLOW = pltpu.SideEffectType.DATAFLOW_SIDE_EFFECTING


def _exchange_start(arrays, scatter, name):
    n = len(arrays)
    srcs = [pltpu.with_memory_space_constraint(a, pltpu.HBM) for a in arrays]
    lands = [pltpu.with_memory_space_constraint(lax.empty(_landing_shape(a, sc), a.dtype), pltpu.HBM)
             for a, sc in zip(arrays, scatter)]
    nsem = n * (N_DEV - 1)

    def body(*refs):
        ins, zones = refs[:n], refs[n:2 * n]
        send_sems, recv_sems = refs[2 * n], refs[2 * n + 1]
        token = refs[-1]
        sends, _ = _exchange_copies(ins, zones, scatter, send_sems, recv_sems, receives=False)
        for cp in sends:
            cp.start()
        token[...] = jnp.zeros_like(token)

    res = pl.pallas_call(
        body, name=name,
        out_shape=(pltpu.SemaphoreType.DMA((nsem,)), pltpu.SemaphoreType.DMA((nsem,)),
                   *[pltpu.HBM(a.shape, a.dtype) for a in srcs + lands], jax.ShapeDtypeStruct((8, HEAD_DIM), F32)),
        in_specs=[_HBM] * (2 * n),
        out_specs=(_SEM, _SEM, *[_HBM] * (2 * n), pl.BlockSpec(memory_space=pltpu.VMEM)),
        input_output_aliases={i: 2 + i for i in range(2 * n)},
        compiler_params=pltpu.CompilerParams(has_side_effects=_DATAFLOW),
    )(*srcs, *lands)
    return dict(sems=res[:2], srcs=res[2:2 + n], lands=res[2 + n:2 + 2 * n], token_block=res[-1],
                token=res[-1][0, 0], scatter=scatter)


def _exchange_wait(started, after, name):
    scatter = started["scatter"]
    n = len(scatter)

    def body(*refs):
        ins, zones = refs[:n], refs[n:2 * n]
        send_sems, recv_sems = refs[2 * n], refs[2 * n + 1]
        sends, recvs = _exchange_copies(ins, zones, scatter, send_sems, recv_sems)
        for cp in sends:
            cp.wait_send()
        for cp in recvs:
            cp.wait_recv()

    thru = list(started["srcs"]) + list(started["lands"])
    res = pl.pallas_call(
        body, name=name, out_shape=[pltpu.HBM(a.shape, a.dtype) for a in thru],
        in_specs=[_HBM] * (2 * n) + [_SEM, _SEM, pl.BlockSpec(memory_space=pl.ANY)], out_specs=[_HBM] * (2 * n),
        input_output_aliases={i: i for i in range(2 * n)},
        compiler_params=pltpu.CompilerParams(has_side_effects=_DATAFLOW),
    )(*thru, *started["sems"], after)
    me = 4 * lax.axis_index("x") + 2 * lax.axis_index("y") + lax.axis_index("c")
    out = []
    for src, got, sc in zip(res[:n], res[n:], scatter):
        own = lax.dynamic_index_in_dim(src, me, 0, keepdims=False) if sc else src
        out.append(lax.dynamic_update_index_in_dim(got, own, me, 0))
    return out


def _adamw(parts, w, m, v, name, rows_per_step, row_offset=0, into=None):
    rows, cols = parts.shape[1:]
    tr = min(rows_per_step, rows)
    assert rows % tr == 0 and row_offset % tr == 0, (name, rows, tr, row_offset)
    first = row_offset // tr
    c1 = 1.0 / (1.0 - ADAM_B1 ** ADAM_STEP)
    c2 = 1.0 / (1.0 - ADAM_B2 ** ADAM_STEP)

    def body(p_ref, w_ref, m_ref, v_ref, *rest):
        g_ref, d_ref, nm_ref, nv_ref = rest[-4:]
        g = p_ref[0].astype(F32)
        for d in range(1, N_DEV):
            g = g + p_ref[d].astype(F32)
        nm = ADAM_B1 * m_ref[...] + (1.0 - ADAM_B1) * g
        nv = ADAM_B2 * v_ref[...] + (1.0 - ADAM_B2) * (g * g)
        g_ref[...] = g
        nm_ref[...] = nm
        nv_ref[...] = nv
        d_ref[...] = -ADAM_LR * ((nm * c1) / (jnp.sqrt(nv * c2) + ADAM_EPS) + ADAM_WD * w_ref[...])

    blk = pl.BlockSpec((tr, cols), lambda i: (i + first, 0))
    shape = jax.ShapeDtypeStruct(w.shape, F32)
    prior = [] if into is None else list(into)
    return pl.pallas_call(
        body, name=name, grid=(rows // tr,),
        in_specs=[pl.BlockSpec((N_DEV, tr, cols), lambda i: (0, i, 0)), blk, blk, blk]
        + [pl.BlockSpec(memory_space=pl.ANY)] * len(prior),
        out_specs=[blk] * 4, out_shape=[shape] * 4,
        input_output_aliases={4 + j: j for j in range(len(prior))}, compiler_params=_params(),
    )(parts, w, m, v, *prior)


_LAYERED = ("norm_w", "lru_conv_b", "lru_wa", "lru_ba", "lru_wx", "lru_bx", "lru_lambda", "lru_norm_w",
            "dn_A_log", "dn_dt_bias", "dn_norm_w")
_PACK_LRU = _LAYERED[1:8]
_PACK_LAST = _LAYERED[:1] + _LAYERED[8:]
_WEIGHTS = ("norm_w", "w_in", "lru_conv_w", "lru_conv_b", "lru_wa", "lru_ba", "lru_wx", "lru_bx", "lru_lambda",
            "lru_norm_w", "dn_conv_w", "dn_A_log", "dn_dt_bias", "dn_norm_w", "w_out", "final_norm_w")


def _pack_layer(tree, layer, tail=(), names=_LAYERED):
    rows = []
    for name in names:
        a = tree[name][layer]
        if a.shape[-1] == HEADS:
            a = jnp.pad(a, (0, HEAD_DIM - HEADS))
        rows.append(a.reshape(-1, HEAD_DIM))
    rows += [t.reshape(-1, HEAD_DIM) for t in tail]
    packed = jnp.concatenate(rows, axis=0)
    return jnp.pad(packed, ((0, (-packed.shape[0]) % 8), (0, 0)))


def _unpack_layer(packed, like, names=_LAYERED):
    out, at = {}, 0
    for name in names:
        shape = like[name].shape[1:]
        if shape[-1] == HEADS:
            n = 1
            out[name] = packed[at, :HEADS]
        else:
            n = like[name][0].size // HEAD_DIM
            out[name] = packed[at:at + n].reshape(shape)
        at += n
    return out, at


def _heads_to_channels(a):
    return jnp.transpose(a, (1, 0, 2)).reshape(a.shape[1], HEADS * HEAD_DIM)


def kernel(x, norm_w, w_in, lru_conv_w, lru_conv_b, lru_wa, lru_ba, lru_wx, lru_bx, lru_lambda, lru_norm_w, dn_conv_w, dn_A_log, dn_dt_bias, dn_norm_w, w_out, final_norm_w, loss_target, m_norm_w, m_w_in, m_lru_conv_w, m_lru_conv_b, m_lru_wa, m_lru_ba, m_lru_wx, m_lru_bx, m_lru_lambda, m_lru_norm_w, m_dn_conv_w, m_dn_A_log, m_dn_dt_bias, m_dn_norm_w, m_w_out, m_final_norm_w, v_norm_w, v_w_in, v_lru_conv_w, v_lru_conv_b, v_lru_wa, v_lru_ba, v_lru_wx, v_lru_bx, v_lru_lambda, v_lru_norm_w, v_dn_conv_w, v_dn_A_log, v_dn_dt_bias, v_dn_norm_w, v_w_out, v_final_norm_w):
    weights = dict(norm_w=norm_w, w_in=w_in, lru_conv_w=lru_conv_w, lru_conv_b=lru_conv_b, lru_wa=lru_wa,
                   lru_ba=lru_ba, lru_wx=lru_wx, lru_bx=lru_bx, lru_lambda=lru_lambda, lru_norm_w=lru_norm_w,
                   dn_conv_w=dn_conv_w, dn_A_log=dn_A_log, dn_dt_bias=dn_dt_bias, dn_norm_w=dn_norm_w,
                   w_out=w_out, final_norm_w=final_norm_w)
    mom_m = dict(norm_w=m_norm_w, w_in=m_w_in, lru_conv_w=m_lru_conv_w, lru_conv_b=m_lru_conv_b, lru_wa=m_lru_wa,
                 lru_ba=m_lru_ba, lru_wx=m_lru_wx, lru_bx=m_lru_bx, lru_lambda=m_lru_lambda,
                 lru_norm_w=m_lru_norm_w, dn_conv_w=m_dn_conv_w, dn_A_log=m_dn_A_log, dn_dt_bias=m_dn_dt_bias,
                 dn_norm_w=m_dn_norm_w, w_out=m_w_out, final_norm_w=m_final_norm_w)
    mom_v = dict(norm_w=v_norm_w, w_in=v_w_in, lru_conv_w=v_lru_conv_w, lru_conv_b=v_lru_conv_b, lru_wa=v_lru_wa,
                 lru_ba=v_lru_ba, lru_wx=v_lru_wx, lru_bx=v_lru_bx, lru_lambda=v_lru_lambda,
                 lru_norm_w=v_lru_norm_w, dn_conv_w=v_dn_conv_w, dn_A_log=v_dn_A_log, dn_dt_bias=v_dn_dt_bias,
                 dn_norm_w=v_dn_norm_w, w_out=v_w_out, final_norm_w=v_final_norm_w)
    depth = norm_w.shape[0]
    xs = x[0]
    s = xs.shape[0]
    tm = min(1024, s)

    assert depth >= 2, depth

    def row(a):
        return a.reshape(1, -1)

    def pad_row(a):
        return jnp.pad(a, (0, HEAD_DIM - a.shape[0])).reshape(1, HEAD_DIM)

    def full_w_in(g):
        w = jnp.transpose(g, (1, 2, 0, 3)).reshape(g.shape[1], D_MODEL, D_IN)
        return jnp.pad(w, ((0, 0), (0, 0), (0, D_IN_PAD - D_IN)))

    g_win0, g_lcw, g_dcw = _two_level_gather([w_in[:1].astype(BF16), lru_conv_w, dn_conv_w], "gather_first")
    rest = _exchange_start([w_in[1:].astype(BF16), w_out.astype(BF16)], [False] * 2, "gather_rest_start")
    win = [full_w_in(g_win0)[0]]
    wout = None
    lcw = jnp.transpose(g_lcw, (1, 2, 0, 3)).reshape(depth, 4, D_MODEL)
    dcw = jnp.transpose(g_dcw, (1, 2, 0, 3)).reshape(depth, 4, 3 * D_MODEL)

    saved = []
    cur = xs
    for l in range(depth):
        nw_row = row(norm_w[l]) + rest["token"] if l == 0 else row(norm_w[l])
        hn = _rmsnorm_fwd(cur, nw_row, f"norm_fwd_{l}")
        proj = _matmul(hn, win[l], "nn", tm, 896, D_MODEL, f"in_proj_{l}")
        y_lru, hs = _lru_fwd(proj, lcw[l], row(lru_conv_b[l]), lru_wa[l], row(lru_ba[l]), lru_wx[l], row(lru_bx[l]),
                             row(lru_lambda[l]), row(lru_norm_w[l]), f"lru_fwd_{l}")
        ycat, o_dn, states = _dn_fwd(proj, y_lru, dcw[l], pad_row(dn_A_log[l]), pad_row(dn_dt_bias[l]),
                                     row(dn_norm_w[l]), f"dn_fwd_{l}")
        if l == 0:
            g_win_rest, g_wout = _exchange_wait(rest, ycat, "gather_rest_wait")
            win += list(full_w_in(g_win_rest))
            wout = jnp.transpose(g_wout, (1, 0, 2, 3)).reshape(depth, 2 * D_MODEL, D_MODEL)
        nxt = _matmul(ycat, wout[l], "nn", tm, D_MODEL, 2 * D_MODEL, f"out_proj_{l}", add=cur)
        saved.append((cur, hn, proj, hs, o_dn, states, ycat))
        cur = nxt
    loss_part, dx, d_final = _final_loss(cur, row(final_norm_w), loss_target[0], "final_loss")

    def win_slots(g):
        return jnp.transpose(g.reshape(D_MODEL, N_DEV, D_IN // N_DEV), (1, 0, 2))

    def wout_slots(g):
        return g.reshape(N_DEV, 2 * D_MODEL // N_DEV, D_MODEL)

    grads = {k: [None] * depth for k in _WEIGHTS if k not in ("final_norm_w", "w_in", "w_out")}
    started = {}
    token = None
    for l in reversed(range(depth)):
        x_in, hn, proj, hs, o_dn, states, ycat = saved[l]
        dy = _matmul(dx, wout[l], "nt", tm, D_MODEL, D_MODEL, f"out_proj_dy_{l}")
        g_wout_l = _matmul(ycat, dx, "tn", D_MODEL, D_MODEL, tm, f"out_proj_dw_{l}", out_dtype=BF16)
        if l == 0:
            started["w_out_0"] = _exchange_start([wout_slots(g_wout_l)], [True], "exchange_w_out_0_start")
            token = token + started["w_out_0"]["token"]
        cb_row = row(lru_conv_b[l]) if token is None else row(lru_conv_b[l]) + token
        (dlx, dlz, g_lcw, g_lcb, g_wa, g_ba, g_wx, g_bx, g_lam, g_lnw) = _lru_bwd(
            proj, hs, dy, lcw[l], cb_row, lru_wa[l], row(lru_ba[l]), lru_wx[l], row(lru_bx[l]),
            row(lru_lambda[l]), row(lru_norm_w[l]), f"lru_bwd_{l}")
        grads["lru_conv_w"][l] = _heads_to_channels(g_lcw)
        grads["lru_conv_b"][l] = g_lcb.reshape(D_MODEL)
        grads["lru_wa"][l] = g_wa
        grads["lru_ba"][l] = g_ba.reshape(D_MODEL)
        grads["lru_wx"][l] = g_wx
        grads["lru_bx"][l] = g_bx.reshape(D_MODEL)
        grads["lru_lambda"][l] = g_lam.reshape(D_MODEL)
        grads["lru_norm_w"][l] = g_lnw.reshape(D_MODEL)
        al_row = pad_row(dn_A_log[l])
        if l == 0:
            started["pack_0"] = _exchange_start([_pack_layer(grads, 0, names=_PACK_LRU)], [False],
                                                "exchange_pack_0_start")
            al_row = al_row + started["pack_0"]["token"]
        (dq, dk, dv, ddz, dba, g_dcw3, g_al, g_dt, g_dnw) = _dn_bwd(
            proj, o_dn, states, dy, dcw[l], al_row, pad_row(dn_dt_bias[l]), row(dn_norm_w[l]), f"dn_bwd_{l}")
        g_dcw3 = g_dcw3.reshape(HEADS, 3, 4, HEAD_DIM)
        grads["dn_conv_w"][l] = jnp.concatenate([_heads_to_channels(g_dcw3[:, i]) for i in range(3)], axis=1)
        grads["dn_A_log"][l] = g_al[0, :HEADS]
        grads["dn_dt_bias"][l] = g_dt[0, :HEADS]
        grads["dn_norm_w"][l] = g_dnw.reshape(HEAD_DIM)
        dep = None
        pieces = [dlx, dlz, dq, dk, dv, ddz]
        wide = len(pieces) * D_MODEL
        dba = dba.astype(BF16)
        g_win_l = jnp.concatenate(
            [_matmul_tn_parts(hn, pieces, D_MODEL, D_MODEL, tm, f"in_proj_dw_{l}", BF16),
             _matmul(hn, dba, "tn", D_MODEL, HEAD_DIM, tm, f"in_proj_dw_gates_{l}", out_dtype=BF16)[:, :D_IN - wide]],
            axis=1)
        if l == 0:
            started[0] = _exchange_start([win_slots(g_win_l)], [True], "exchange_0_start")
            dep = started[0]["token_block"]
        dh = _matmul_nt_parts(pieces, dba, win[l], tm, D_MODEL, f"in_proj_dh_{l}", dep=dep)
        dx, g_nw = _rmsnorm_bwd(x_in, row(norm_w[l]), dh, dx, f"norm_bwd_{l}")
        grads["norm_w"][l] = g_nw.reshape(D_MODEL)
        if l > 0:
            tail = (d_final, loss_part) if l == depth - 1 else ()
            started[l] = _exchange_start([win_slots(g_win_l), wout_slots(g_wout_l), _pack_layer(grads, l, tail)],
                                         [True, True, False], f"exchange_{l}_start")
            token = started[l]["token"]

    def conv_slots(a):
        dd, r, cc = a.shape
        return jnp.transpose(a.reshape(dd, r, N_DEV, cc // N_DEV), (2, 0, 1, 3))

    small = _exchange_start(
        [conv_slots(jnp.stack(grads["lru_conv_w"])), conv_slots(jnp.stack(grads["dn_conv_w"])),
         _pack_layer(grads, 0, names=_PACK_LAST)], [True, True, False], "exchange_small_start")

    new = {}
    flat_in = (depth * D_MODEL, D_IN // N_DEV)
    flat_out = (depth * 2 * D_MODEL // N_DEV, D_MODEL)
    zero_row = jnp.zeros((1, HEAD_DIM), F32)

    def adamw_pack(parts, layer, names=_LAYERED, name="adamw_small"):
        tails = [(t, zero_row) if layer == depth - 1 else () for t in (final_norm_w, m_final_norm_w, v_final_norm_w)]
        return _adamw(parts, _pack_layer(weights, layer, tails[0], names), _pack_layer(mom_m, layer, tails[1], names),
                      _pack_layer(mom_v, layer, tails[2], names), f"{name}_{layer}", parts.shape[1])

    def adamw_w_in(parts, layer, into):
        return _adamw(parts, w_in.reshape(flat_in), m_w_in.reshape(flat_in), v_w_in.reshape(flat_in),
                      f"adamw_w_in_{layer}", 256, layer * D_MODEL, into)

    def adamw_w_out(parts, layer, into):
        return _adamw(parts, w_out.reshape(flat_out), m_w_out.reshape(flat_out), v_w_out.reshape(flat_out),
                      f"adamw_w_out_{layer}", 256, layer * flat_out[0] // depth, into)

    acc_in = acc_out = None
    packs = [None] * depth
    after = small["token_block"]
    for l in reversed(range(1, depth)):
        r_win, r_wout, r_pack = _exchange_wait(started[l], after, f"exchange_{l}_wait")
        acc_in = adamw_w_in(r_win, l, acc_in)
        acc_out = adamw_w_out(r_wout, l, acc_out)
        packs[l] = adamw_pack(r_pack, l)
        after = packs[l][0]
    (r_wout,) = _exchange_wait(started["w_out_0"], after, "exchange_w_out_0_wait")
    acc_out = adamw_w_out(r_wout, 0, acc_out)
    (r_win,) = _exchange_wait(started[0], acc_out[0], "exchange_0_wait")
    acc_in = adamw_w_in(r_win, 0, acc_in)
    (r_pack,) = _exchange_wait(started["pack_0"], acc_in[0], "exchange_pack_0_wait")
    packs[0] = adamw_pack(r_pack, 0, _PACK_LRU)
    r_lcw, r_dcw, r_last = _exchange_wait(small, packs[0][0], "exchange_small_wait")
    for name, parts in (("lru_conv_w", r_lcw), ("dn_conv_w", r_dcw)):
        w = weights[name]
        flat = (-1, w.shape[-1])
        outs = _adamw(parts.reshape((N_DEV,) + (w.size // w.shape[-1], w.shape[-1])), w.reshape(flat),
                      mom_m[name].reshape(flat), mom_v[name].reshape(flat), f"adamw_{name}", 8)
        new[name] = [a.reshape(w.shape) for a in outs]
    last_0 = adamw_pack(r_last, 0, _PACK_LAST, "adamw_last")
    new["w_in"] = [a.reshape(w_in.shape) for a in acc_in]
    new["w_out"] = [a.reshape(w_out.shape) for a in acc_out]
    for i in range(4):
        layers = [{**_unpack_layer(packs[0][i], weights, _PACK_LRU)[0],
                   **_unpack_layer(last_0[i], weights, _PACK_LAST)[0]}]
        layers += [_unpack_layer(packs[l][i], weights)[0] for l in range(1, depth)]
        for name in _LAYERED:
            new.setdefault(name, []).append(jnp.stack([layer[name] for layer in layers]))
    tail_at = _unpack_layer(packs[depth - 1][0], weights)[1]
    rows_final = D_MODEL // HEAD_DIM
    new["final_norm_w"] = [packs[depth - 1][i][tail_at:tail_at + rows_final].reshape(D_MODEL) for i in range(4)]
    loss = packs[depth - 1][0][tail_at + rows_final, 0]
    out = [loss, dx.reshape(x.shape)]
    for i in range(4):
        out += [new[name][i] for name in _WEIGHTS]
    return tuple(out)
```

```python
import functools

import jax
import jax.numpy as jnp
from jax import lax
from jax.experimental import pallas as pl
from jax.experimental.pallas import tpu as pltpu

F32 = jnp.float32
BF16 = jnp.bfloat16

N_DEV = 8
D_MODEL = 1024
HEADS = 8
HEAD_DIM = 128
CHUNK = 64
D_IN = 6160
D_IN_PAD = 6272
COL_LRU_X, COL_LRU_Z, COL_Q, COL_K, COL_V, COL_DN_Z, COL_BA = 0, 8, 16, 24, 32, 40, 48
LRU_C = 8.0
EPS = 1e-6
ADAM_LR, ADAM_B1, ADAM_B2, ADAM_EPS, ADAM_WD, ADAM_STEP = 0.001, 0.9, 0.999, 1e-08, 0.01, 10
TIME_BLOCK = 1024
DN_TIME_BLOCK = 128
DN_GROUP = 8
PART_CHUNKS = 16
PART_CHUNKS_BWD = 16
VMEM_LIMIT = 56 * 1024 * 1024

NN = (((1,), (0,)), ((), ()))
NT = (((1,), (1,)), ((), ()))
TN = (((0,), (0,)), ((), ()))


B_NN = (((2,), (1,)), ((0,), (0,)))
B_NT = (((2,), (2,)), ((0,), (0,)))
B_TN = (((1,), (1,)), ((0,), (0,)))


def _split_bf16(x):
    hi = x.astype(BF16)
    return hi, (x - hi.astype(F32)).astype(BF16)


def _dot(a, b, dims, prec):
    if prec == "bf16":
        return lax.dot_general(a.astype(BF16), b.astype(BF16), dims, preferred_element_type=F32)
    a1, a2 = _split_bf16(a)
    b1, b2 = _split_bf16(b)
    dg = functools.partial(lax.dot_general, dimension_numbers=dims, preferred_element_type=F32)
    return dg(a1, b1) + (dg(a1, b2) + dg(a2, b1))


def _make_mm(prec, nn_dims, nt_dims, tn_dims):
    @jax.custom_vjp
    def nn(a, b):
        return _dot(a, b, nn_dims, prec)

    @jax.custom_vjp
    def nt(a, b):
        return _dot(a, b, nt_dims, prec)

    @jax.custom_vjp
    def tn(a, b):
        return _dot(a, b, tn_dims, prec)

    nn.defvjp(lambda a, b: (_dot(a, b, nn_dims, prec), (a, b)),
              lambda r, g: (_dot(g, r[1], nt_dims, prec), _dot(r[0], g, tn_dims, prec)))
    nt.defvjp(lambda a, b: (_dot(a, b, nt_dims, prec), (a, b)),
              lambda r, g: (_dot(g, r[1], nn_dims, prec), _dot(g, r[0], tn_dims, prec)))
    tn.defvjp(lambda a, b: (_dot(a, b, tn_dims, prec), (a, b)),
              lambda r, g: (_dot(r[1], g, nt_dims, prec), _dot(r[0], g, nn_dims, prec)))
    return nn, nt, tn


_NN_B, _NT_B, _TN_B = _make_mm("bf16", NN, NT, TN)
_BNN, _BNT, _BTN = _make_mm("bf16", B_NN, B_NT, B_TN)


def _unit_lower_inverse_steps(a, tick=lambda: None):
    n = a.shape[-1]
    eye = (lax.broadcasted_iota(jnp.int32, a.shape, 1) == lax.broadcasted_iota(jnp.int32, a.shape, 2)).astype(F32)
    dg = functools.partial(lax.dot_general, dimension_numbers=B_NN, preferred_element_type=F32)
    inv = eye - a
    pw = _dot(a, a, B_NN, "bf16x3")
    steps = n.bit_length() - 2
    for j in range(steps):
        i1, i2 = _split_bf16(inv)
        p1, p2 = _split_bf16(pw)
        square = j + 1 < steps
        by_hi = dg(jnp.concatenate([i1, i2, p1, p2] if square else [i1, i2], axis=1), p1)
        by_lo = dg(jnp.concatenate([i1, p1], axis=1) if square else i1, p2)
        inv = inv + (by_hi[:, :n] + (by_lo[:, :n] + by_hi[:, n:2 * n]))
        if square:
            pw = by_hi[:, 2 * n:3 * n] + (by_lo[:, n:] + by_hi[:, 3 * n:])
        tick()
    return inv


@jax.custom_vjp
def _unit_lower_inverse(a):
    return _unit_lower_inverse_steps(a)


def _uli_fwd(a):
    inv = _unit_lower_inverse(a)
    return inv, inv


def _uli_bwd(inv, g):
    return (-_dot(_dot(inv, g, B_TN, "bf16"), inv, B_NT, "bf16"),)


_unit_lower_inverse.defvjp(_uli_fwd, _uli_bwd)


def _rows2(y, m):
    return y[:, :m], y[:, m:]


@jax.custom_vjp
def _pair_nn(x1, x2, r):
    return _rows2(_dot(jnp.concatenate([x1, x2], axis=1), r, B_NN, "bf16"), x1.shape[1])


def _pair_nn_bwd(res, g):
    x1, x2, r = res
    g = jnp.concatenate(g, axis=1)
    dx1, dx2 = _rows2(_dot(g, r, B_NT, "bf16"), x1.shape[1])
    return dx1, dx2, _dot(jnp.concatenate([x1, x2], axis=1), g, B_TN, "bf16")


_pair_nn.defvjp(lambda x1, x2, r: (_pair_nn(x1, x2, r), (x1, x2, r)), _pair_nn_bwd)


@jax.custom_vjp
def _pair_nt(x1, x2, r):
    return _rows2(_dot(jnp.concatenate([x1, x2], axis=1), r, B_NT, "bf16"), x1.shape[1])


def _pair_nt_bwd(res, g):
    x1, x2, r = res
    g = jnp.concatenate(g, axis=1)
    dx1, dx2 = _rows2(_dot(g, r, B_NN, "bf16"), x1.shape[1])
    return dx1, dx2, _dot(g, jnp.concatenate([x1, x2], axis=1), B_TN, "bf16")


_pair_nt.defvjp(lambda x1, x2, r: (_pair_nt(x1, x2, r), (x1, x2, r)), _pair_nt_bwd)


@jax.custom_vjp
def _wide_nn(l, r1, r2):
    y = _dot(l, jnp.concatenate([r1, r2], axis=2), B_NN, "bf16")
    return y[:, :, :r1.shape[2]], y[:, :, r1.shape[2]:]


def _wide_nn_bwd(res, g):
    l, r1, r2 = res
    g = jnp.concatenate(g, axis=2)
    dr = _dot(l, g, B_TN, "bf16")
    return (_dot(g, jnp.concatenate([r1, r2], axis=2), B_NT, "bf16"), dr[:, :, :r1.shape[2]], dr[:, :, r1.shape[2]:])


_wide_nn.defvjp(lambda l, r1, r2: (_wide_nn(l, r1, r2), (l, r1, r2)), _wide_nn_bwd)


def _lower_ones(batch, n):
    shape = (batch, n, n)
    return (lax.broadcasted_iota(jnp.int32, shape, 1) >= lax.broadcasted_iota(jnp.int32, shape, 2)).astype(BF16)


@jax.custom_vjp
def _chunk_cumsum(g):
    tri = _lower_ones(g.shape[0], g.shape[1])
    g1, g2 = _split_bf16(g)
    g3 = (g - g1.astype(F32) - g2.astype(F32)).astype(BF16)
    dg = functools.partial(lax.dot_general, dimension_numbers=B_NN, preferred_element_type=F32)
    return dg(tri, g1) + (dg(tri, g2) + dg(tri, g3))


def _chunk_cumsum_bwd(_, ct):
    tri = _lower_ones(ct.shape[0], ct.shape[1])
    c1, c2 = _split_bf16(ct)
    dg = functools.partial(lax.dot_general, dimension_numbers=B_TN, preferred_element_type=F32)
    return (dg(tri, c1) + dg(tri, c2),)


_chunk_cumsum.defvjp(lambda g: (_chunk_cumsum(g), None), _chunk_cumsum_bwd)


def _expm1(x):
    small = x * (1.0 + x * (0.5 + x * (1.0 / 6 + x * (1.0 / 24 + x * (1.0 / 120 + x * (1.0 / 720))))))
    return jnp.where(jnp.abs(x) < 0.2, small, jnp.exp(x) - 1.0)


def _sigmoid(x):
    return 1.0 / (1.0 + jnp.exp(-x))


def _silu(x):
    return x * _sigmoid(x)


def _softplus(x):
    return jnp.maximum(x, 0.0) + jnp.log(1.0 + jnp.exp(-jnp.abs(x)))


def _rmsnorm(x, w):
    return x * lax.rsqrt(jnp.mean(x * x, axis=-1, keepdims=True) + EPS) * w


def _gated_norm(o, z, w):
    return o * lax.rsqrt(jnp.mean(o * o, axis=-1, keepdims=True) + EPS) * w * _silu(z)


def _lru_gates(xc, wa, ba, wx, bx, lam):
    r = _sigmoid(_NN_B(xc, wa) + ba)
    i = _sigmoid(_NN_B(xc, wx) + bx)
    log_a = -LRU_C * r * _softplus(-lam)
    a = jnp.exp(log_a)
    mult = jnp.sqrt(-_expm1(2.0 * log_a))
    return a, mult * (i * xc)


def _scan_forward(a, b, h0):
    rows = a.shape[0]
    row = lax.broadcasted_iota(jnp.int32, a.shape, 0)
    k = 1
    while k < rows:
        seen = row >= k
        b = jnp.where(seen, a * pltpu.roll(b, k, 0) + b, b)
        a = jnp.where(seen, a * pltpu.roll(a, k, 0), a)
        k *= 2
    return b + a * h0


def _scan_reverse(a, d, carry):
    rows = a.shape[0]
    row = lax.broadcasted_iota(jnp.int32, a.shape, 0)
    last = row == rows - 1
    c = jnp.where(last, 0.0, pltpu.roll(a, rows - 1, 0))
    d = d + jnp.where(last, carry, 0.0)
    k = 1
    while k < rows:
        seen = row < rows - k
        d = jnp.where(seen, d + c * pltpu.roll(d, rows - k, 0), d)
        c = jnp.where(seen, c * pltpu.roll(c, rows - k, 0), c)
        k *= 2
    return d


def _lane_pick(row, lane_index):
    lane = lax.broadcasted_iota(jnp.int32, row.shape, 1)
    return jnp.sum(jnp.where(lane == lane_index, row, 0.0), axis=-1, keepdims=True)


def _dn_prep(qc, kc, vc, ba, a_log_row, dt_row, head):
    q = _silu(qc)
    k = _silu(kc)
    v = _silu(vc)
    q = q * lax.rsqrt(jnp.sum(q * q, axis=-1, keepdims=True) + EPS) * (HEAD_DIM ** -0.5)
    k = k * lax.rsqrt(jnp.sum(k * k, axis=-1, keepdims=True) + EPS)
    beta = _sigmoid(_lane_pick(ba, head))
    g = -jnp.exp(_lane_pick(a_log_row, head)) * _softplus(_lane_pick(ba, HEADS + head) + _lane_pick(dt_row, head))
    return q, k, v, g, beta


def _dn_chunks_head(q, k, v, gcol, bcol, tick=None):
    n, c, d = q.shape
    row = lax.broadcasted_iota(jnp.int32, (n, c, c), 1)
    col = lax.broadcasted_iota(jnp.int32, (n, c, c), 2)
    g_wide = jnp.broadcast_to(gcol, (n, c, d))
    b_wide = jnp.broadcast_to(bcol, (n, c, d))
    gc = _chunk_cumsum(g_wide)
    gc_rows = gc[:, :, :c]
    decay = jnp.exp(jnp.where(row >= col, gc_rows - jnp.swapaxes(gc_rows, 1, 2), -1e30))
    kb = k * b_wide
    eg = jnp.exp(gc)
    if tick is not None:
        tick()
    kbk, qk = _pair_nt(kb, q, k)
    a = jnp.where(row > col, kbk * decay, 0.0)
    if tick is not None:
        tick()
    tinv = _unit_lower_inverse(a) if tick is None else _unit_lower_inverse_steps(a, tick)
    u, w = _wide_nn(tinv, v * b_wide, kb * eg)
    g_last = jnp.sum(g_wide, axis=1, keepdims=True)
    return u, w, qk * decay, q * eg, k * jnp.exp(g_last - gc), jnp.exp(g_last)


def _dn_chunks(q, k, v, gcol, bcol, states):
    u, w, attn, qe, kdec, eglast = _dn_chunks_head(q, k, v, gcol, bcol)
    w_st, qe_st = _pair_nn(w, qe, states)
    v_new = u - w_st
    o = qe_st + _BNN(attn, v_new)
    return (o, states * eglast + _BTN(kdec, v_new)), (w, attn, qe, kdec, eglast)


def _conv_taps(buf, head, cw, rows):
    acc = cw[0:1, :] * buf[head, pl.ds(5, rows), :]
    for j in range(1, 4):
        acc = acc + cw[j:j + 1, :] * buf[head, pl.ds(5 + j, rows), :]
    return acc


def _conv_backward(dbuf, dhead, xbuf, xhead, cw, dxc, rows):
    dbuf[dhead, pl.ds(0, rows), :] = dxc
    dx = cw[0:1, :] * dbuf[dhead, pl.ds(3, rows), :]
    for j in range(1, 4):
        dx = dx + cw[j:j + 1, :] * dbuf[dhead, pl.ds(3 - j, rows), :]
    dcw = jnp.concatenate(
        [jnp.sum(dxc * xbuf[xhead, pl.ds(5 + j, rows), :], axis=0, keepdims=True) for j in range(4)], axis=0)
    dbuf[dhead, pl.ds(rows, 8), :] = dbuf[dhead, pl.ds(0, 8), :]
    return dx, dcw


def _params(**kw):
    return pltpu.CompilerParams(vmem_limit_bytes=VMEM_LIMIT, **kw)


def _matmul(a, b, form, tm, tn, tk, name, add=None, out_dtype=F32, dep=None):
    if form == "nn":
        (m, kdim), (_, n) = a.shape, b.shape
        a_spec = pl.BlockSpec((tm, tk), lambda j, i, k: (i, k))
        b_spec = pl.BlockSpec((tk, tn), lambda j, i, k: (k, j))
        dims = NN
    elif form == "nt":
        (m, kdim), (n, _) = a.shape, b.shape
        a_spec = pl.BlockSpec((tm, tk), lambda j, i, k: (i, k))
        b_spec = pl.BlockSpec((tn, tk), lambda j, i, k: (j, k))
        dims = NT
    else:
        (kdim, m), (_, n) = a.shape, b.shape
        a_spec = pl.BlockSpec((tk, tm), lambda j, i, k: (k, i))
        b_spec = pl.BlockSpec((tk, tn), lambda j, i, k: (k, j))
        dims = TN
    assert m % tm == 0 and n % tn == 0 and kdim % tk == 0, (name, m, n, kdim, tm, tn, tk)
    ksteps = kdim // tk
    o_spec = pl.BlockSpec((tm, tn), lambda j, i, k: (i, j))
    has_add = add is not None
    extra = [] if dep is None else [dep]

    def body(*refs):
        a_ref, b_ref = refs[:2]
        c_ref = refs[2] if has_add else None
        o_ref, acc = refs[-2:]
        k = pl.program_id(2)

        @pl.when(k == 0)
        def _():
            acc[...] = c_ref[...] if has_add else jnp.zeros_like(acc)

        acc[...] += lax.dot_general(a_ref[...].astype(BF16), b_ref[...].astype(BF16), dims,
                                    preferred_element_type=F32)

        @pl.when(k == ksteps - 1)
        def _():
            o_ref[...] = acc[...].astype(o_ref.dtype)

    in_specs = [a_spec, b_spec] + ([o_spec] if has_add else []) + [pl.BlockSpec((8, HEAD_DIM), lambda j, i, k: (0, 0))
                                                                   for _ in extra]
    args = (a, b) + ((add,) if has_add else ()) + tuple(extra)
    return pl.pallas_call(
        body, name=name, grid=(n // tn, m // tm, ksteps), in_specs=in_specs, out_specs=o_spec,
        out_shape=jax.ShapeDtypeStruct((m, n), out_dtype), scratch_shapes=[pltpu.VMEM((tm, tn), F32)],
        compiler_params=_params(dimension_semantics=("parallel", "parallel", "arbitrary")),
    )(*args)


def _matmul_nt_parts(parts, narrow, w, tm, tn, name, dep=None):
    m, c = parts[0].shape
    c2 = narrow.shape[1]
    n = w.shape[0]
    count = len(parts)
    assert m % tm == 0 and n % tn == 0 and all(p.shape == (m, c) for p in parts) and (count * c) % c2 == 0, (name, m, n)
    extra = [] if dep is None else [dep]

    def body(*refs):
        part_refs, narrow_ref, w_ref, w2_ref = refs[:count], refs[count], refs[count + 1], refs[count + 2]
        o_ref, acc = refs[-2:]
        k = pl.program_id(2)

        @pl.when(k == 0)
        def _():
            acc[...] = jnp.zeros_like(acc)

        for p in range(count):
            @pl.when(k == p)
            def _(p=p):
                acc[...] += lax.dot_general(part_refs[p][...].astype(BF16), w_ref[...].astype(BF16), NT,
                                            preferred_element_type=F32)

        @pl.when(k == count)
        def _():
            o_ref[...] = acc[...] + lax.dot_general(narrow_ref[...].astype(BF16), w2_ref[...].astype(BF16), NT,
                                                    preferred_element_type=F32)

    in_specs = ([pl.BlockSpec((tm, c), lambda j, i, k: (i, 0))] * count
                + [pl.BlockSpec((tm, c2), lambda j, i, k: (i, 0)),
                   pl.BlockSpec((tn, c), lambda j, i, k: (j, jnp.minimum(k, count - 1))),
                   pl.BlockSpec((tn, c2), lambda j, i, k: (j, count * c // c2))]
                + [pl.BlockSpec((8, HEAD_DIM), lambda j, i, k: (0, 0)) for _ in extra])
    return pl.pallas_call(
        body, name=name, grid=(n // tn, m // tm, count + 1), in_specs=in_specs,
        out_specs=pl.BlockSpec((tm, tn), lambda j, i, k: (i, j)),
        out_shape=jax.ShapeDtypeStruct((m, n), F32), scratch_shapes=[pltpu.VMEM((tm, tn), F32)],
        compiler_params=_params(dimension_semantics=("parallel", "parallel", "arbitrary")),
    )(*parts, narrow, w, w, *extra)


def _matmul_tn_parts(a, parts, tm, tn, tk, name, out_dtype):
    kdim, m = a.shape
    c = parts[0].shape[1]
    count = len(parts)
    per = c // tn
    assert m % tm == 0 and c % tn == 0 and kdim % tk == 0 and all(p.shape == (kdim, c) for p in parts), (name, m, c)
    ksteps = kdim // tk

    def body(*refs):
        a_ref, part_refs = refs[0], refs[1:1 + count]
        o_ref, acc = refs[-2:]
        j, k = pl.program_id(0), pl.program_id(2)

        @pl.when(k == 0)
        def _():
            acc[...] = jnp.zeros_like(acc)

        for p in range(count):
            @pl.when(j // per == p)
            def _(p=p):
                acc[...] += lax.dot_general(a_ref[...].astype(BF16), part_refs[p][...].astype(BF16), TN,
                                            preferred_element_type=F32)

        @pl.when(k == ksteps - 1)
        def _():
            o_ref[...] = acc[...].astype(o_ref.dtype)

    def part_spec(p):
        return pl.BlockSpec((tk, tn), lambda j, i, k: (jnp.where(j // per == p, k, 0), jnp.where(j // per == p, j % per, 0)))

    return pl.pallas_call(
        body, name=name, grid=(count * per, m // tm, ksteps),
        in_specs=[pl.BlockSpec((tk, tm), lambda j, i, k: (k, i))] + [part_spec(p) for p in range(count)],
        out_specs=pl.BlockSpec((tm, tn), lambda j, i, k: (i, j)),
        out_shape=jax.ShapeDtypeStruct((m, count * c), out_dtype), scratch_shapes=[pltpu.VMEM((tm, tn), F32)],
        compiler_params=_params(dimension_semantics=("parallel", "parallel", "arbitrary")),
    )(a, *parts)


def _rmsnorm_fwd(x, w_row, name):
    s = x.shape[0]
    tb = min(TIME_BLOCK, s)

    def body(x_ref, w_ref, o_ref):
        o_ref[...] = _rmsnorm(x_ref[...], w_ref[...]).astype(BF16)

    return pl.pallas_call(
        body, name=name, grid=(s // tb,),
        in_specs=[pl.BlockSpec((tb, D_MODEL), lambda i: (i, 0)), pl.BlockSpec((1, D_MODEL), lambda i: (0, 0))],
        out_specs=pl.BlockSpec((tb, D_MODEL), lambda i: (i, 0)),
        out_shape=jax.ShapeDtypeStruct((s, D_MODEL), BF16), compiler_params=_params(),
    )(x, w_row)


def _rmsnorm_bwd(x, w_row, dh, dres, name):
    s = x.shape[0]
    tb = min(TIME_BLOCK, s)

    def body(x_ref, w_ref, dh_ref, dres_ref, dx_ref, dw_ref):
        _, vjp = jax.vjp(_rmsnorm, x_ref[...], w_ref[...])
        dx, dw = vjp(dh_ref[...])
        dx_ref[...] = dres_ref[...] + dx

        @pl.when(pl.program_id(0) == 0)
        def _():
            dw_ref[...] = jnp.zeros_like(dw_ref)

        dw_ref[...] += dw

    row = pl.BlockSpec((tb, D_MODEL), lambda i: (i, 0))
    vec = pl.BlockSpec((1, D_MODEL), lambda i: (0, 0))
    return pl.pallas_call(
        body, name=name, grid=(s // tb,), in_specs=[row, vec, row, row], out_specs=[row, vec],
        out_shape=[jax.ShapeDtypeStruct((s, D_MODEL), F32), jax.ShapeDtypeStruct((1, D_MODEL), F32)],
        compiler_params=_params(),
    )(x, w_row, dh, dres)


def _final_loss(x, w_row, target, name):
    s = x.shape[0]
    tb = min(TIME_BLOCK, s)

    def loss_fn(xv, wv, tv):
        err = _rmsnorm(xv, wv) - tv
        return 0.5 * jnp.sum(jnp.sum(err * err, axis=-1, keepdims=True), axis=0, keepdims=True) * (1.0 / D_MODEL)

    def body(x_ref, w_ref, t_ref, loss_ref, dx_ref, dw_ref):
        tv = t_ref[...]
        loss, vjp = jax.vjp(lambda xv, wv: loss_fn(xv, wv, tv), x_ref[...], w_ref[...])
        dx, dw = vjp(jnp.ones((1, 1), F32))
        dx_ref[...] = dx

        @pl.when(pl.program_id(0) == 0)
        def _():
            dw_ref[...] = jnp.zeros_like(dw_ref)
            loss_ref[...] = jnp.zeros_like(loss_ref)

        dw_ref[...] += dw
        loss_ref[...] += jnp.broadcast_to(loss, loss_ref.shape)

    row = pl.BlockSpec((tb, D_MODEL), lambda i: (i, 0))
    vec = pl.BlockSpec((1, D_MODEL), lambda i: (0, 0))
    return pl.pallas_call(
        body, name=name, grid=(s // tb,), in_specs=[row, vec, row],
        out_specs=[pl.BlockSpec((1, HEAD_DIM), lambda i: (0, 0)), row, vec],
        out_shape=[jax.ShapeDtypeStruct((1, HEAD_DIM), F32), jax.ShapeDtypeStruct((s, D_MODEL), F32),
                   jax.ShapeDtypeStruct((1, D_MODEL), F32)],
        compiler_params=_params(),
    )(x, w_row, target)


def _head_specs(tb, time_of):
    def col(off):
        return pl.BlockSpec((tb, HEAD_DIM), lambda t, h: (time_of(t), off + h))
    return col


def _vec_spec():
    return pl.BlockSpec((1, HEAD_DIM), lambda t, h: (0, h))


def _lru_fwd(proj, conv_w, conv_b, wa, ba, wx, bx, lam, nw, name):
    s = proj.shape[0]
    tb = min(TIME_BLOCK, s)
    nt = s // tb
    col = _head_specs(tb, lambda t: t)

    def body(x_ref, z_ref, cw_ref, cb_ref, wa_ref, ba_ref, wx_ref, bx_ref, lam_ref, nw_ref,
             y_ref, hs_ref, xbuf, hcar):
        t, h = pl.program_id(0), pl.program_id(1)

        @pl.when(t == 0)
        def _():
            xbuf[h, pl.ds(0, 8), :] = jnp.zeros((8, HEAD_DIM), F32)
            hcar[h] = jnp.zeros((8, HEAD_DIM), F32)

        xbuf[h, pl.ds(8, tb), :] = x_ref[...]
        xc = _conv_taps(xbuf, h, cw_ref[...], tb) + cb_ref[...]
        a, b = _lru_gates(xc, wa_ref[...], ba_ref[...], wx_ref[...], bx_ref[...], lam_ref[...])
        hs_ref[...] = _scan_forward(a, b, hcar[h, pl.ds(0, 1), :])
        hcar[h, pl.ds(0, 1), :] = hs_ref[pl.ds(tb - 1, 1), :]
        xbuf[h, pl.ds(0, 8), :] = xbuf[h, pl.ds(tb, 8), :]
        y_ref[...] = _gated_norm(hs_ref[...], z_ref[...], nw_ref[...]).astype(BF16)

    vec = _vec_spec()
    return pl.pallas_call(
        body, name=name, grid=(nt, HEADS),
        in_specs=[col(COL_LRU_X), col(COL_LRU_Z), pl.BlockSpec((4, HEAD_DIM), lambda t, h: (0, h)), vec,
                  pl.BlockSpec((None, HEAD_DIM, HEAD_DIM), lambda t, h: (h, 0, 0)), vec,
                  pl.BlockSpec((None, HEAD_DIM, HEAD_DIM), lambda t, h: (h, 0, 0)), vec, vec, vec],
        out_specs=[col(0), col(0)],
        out_shape=[jax.ShapeDtypeStruct((s, 2 * D_MODEL), BF16), jax.ShapeDtypeStruct((s, D_MODEL), F32)],
        scratch_shapes=[pltpu.VMEM((HEADS, tb + 8, HEAD_DIM), F32), pltpu.VMEM((HEADS, 8, HEAD_DIM), F32)],
        compiler_params=_params(dimension_semantics=("arbitrary", "arbitrary")),
    )(proj, proj, conv_w, conv_b, wa, ba, wx, bx, lam, nw)


def _halo_spec(tb, nt, off):
    per = tb // 8
    return pl.BlockSpec((8, HEAD_DIM), lambda t, h: (jnp.maximum((nt - 1 - t) * per - 1, 0), off + h))


def _lru_bwd(proj, hs, dy, conv_w, conv_b, wa, ba, wx, bx, lam, nw, name):
    s = proj.shape[0]
    tb = min(TIME_BLOCK, s)
    nt = s // tb
    col = _head_specs(tb, lambda t: nt - 1 - t)

    def body(x_ref, xh_ref, z_ref, hs_ref, hh_ref, dy_ref, cw_ref, cb_ref, wa_ref, ba_ref, wx_ref, bx_ref,
             lam_ref, nw_ref, dx_ref, dz_ref, dcw_ref, dcb_ref, dwa_ref, dba_ref, dwx_ref, dbx_ref, dlam_ref,
             dnw_ref, xbuf, hbuf, dbuf, gcar):
        t, h = pl.program_id(0), pl.program_id(1)
        first_block = t == nt - 1

        @pl.when(t == 0)
        def _():
            dbuf[h, pl.ds(tb, 8), :] = jnp.zeros((8, HEAD_DIM), F32)
            gcar[h] = jnp.zeros((8, HEAD_DIM), F32)
            dcw_ref[h] = jnp.zeros((4, HEAD_DIM), F32)
            dwa_ref[h] = jnp.zeros((HEAD_DIM, HEAD_DIM), F32)
            dwx_ref[h] = jnp.zeros((HEAD_DIM, HEAD_DIM), F32)
            for ref in (dcb_ref, dba_ref, dbx_ref, dlam_ref, dnw_ref):
                ref[h] = jnp.zeros((1, HEAD_DIM), F32)

        keep = jnp.where(first_block, 0.0, 1.0)
        xbuf[0, pl.ds(0, 8), :] = xh_ref[...] * keep
        xbuf[0, pl.ds(8, tb), :] = x_ref[...]
        hbuf[pl.ds(0, 8), :] = hh_ref[...] * keep
        hbuf[pl.ds(8, tb), :] = hs_ref[...]
        cw = cw_ref[...]
        xc = _conv_taps(xbuf, 0, cw, tb) + cb_ref[...]
        (a, _), gates_vjp = jax.vjp(_lru_gates, xc, wa_ref[...], ba_ref[...], wx_ref[...], bx_ref[...], lam_ref[...])
        _, norm_vjp = jax.vjp(_gated_norm, hs_ref[...], z_ref[...], nw_ref[...])
        dh, dz, dnw = norm_vjp(dy_ref[...])
        dz_ref[...] = dz.astype(dz_ref.dtype)
        g = _scan_reverse(a, dh, gcar[h, pl.ds(0, 1), :])
        gcar[h, pl.ds(0, 1), :] = a[0:1, :] * g[0:1, :]
        dxc, dwa, dba, dwx, dbx, dlam = gates_vjp((g * hbuf[pl.ds(7, tb), :], g))
        dx, dcw = _conv_backward(dbuf, h, xbuf, 0, cw, dxc, tb)
        dx_ref[...] = dx.astype(dx_ref.dtype)
        dcw_ref[h] += dcw
        dcb_ref[h] += jnp.sum(dxc, axis=0, keepdims=True)
        dwa_ref[h] += dwa
        dwx_ref[h] += dwx
        dba_ref[h] += dba
        dbx_ref[h] += dbx
        dlam_ref[h] += dlam
        dnw_ref[h] += dnw

    vec = _vec_spec()
    mat = pl.BlockSpec((None, HEAD_DIM, HEAD_DIM), lambda t, h: (h, 0, 0))

    def whole(shape):
        return pl.BlockSpec(shape, lambda t, h: (0,) * len(shape))

    head_vec = jax.ShapeDtypeStruct((HEADS, 1, HEAD_DIM), F32)
    head_mat = jax.ShapeDtypeStruct((HEADS, HEAD_DIM, HEAD_DIM), F32)
    return pl.pallas_call(
        body, name=name, grid=(nt, HEADS),
        in_specs=[col(COL_LRU_X), _halo_spec(tb, nt, COL_LRU_X), col(COL_LRU_Z), col(0), _halo_spec(tb, nt, 0), col(0),
                  pl.BlockSpec((4, HEAD_DIM), lambda t, h: (0, h)), vec, mat, vec, mat, vec, vec, vec],
        out_specs=[col(0), col(0), whole((HEADS, 4, HEAD_DIM)), whole((HEADS, 1, HEAD_DIM)),
                   whole((HEADS, HEAD_DIM, HEAD_DIM)), whole((HEADS, 1, HEAD_DIM)),
                   whole((HEADS, HEAD_DIM, HEAD_DIM)), whole((HEADS, 1, HEAD_DIM)), whole((HEADS, 1, HEAD_DIM)),
                   whole((HEADS, 1, HEAD_DIM))],
        out_shape=[jax.ShapeDtypeStruct((s, D_MODEL), BF16), jax.ShapeDtypeStruct((s, D_MODEL), BF16),
                   jax.ShapeDtypeStruct((HEADS, 4, HEAD_DIM), F32), head_vec, head_mat, head_vec, head_mat, head_vec,
                   head_vec, head_vec],
        scratch_shapes=[pltpu.VMEM((1, tb + 8, HEAD_DIM), F32), pltpu.VMEM((tb + 8, HEAD_DIM), F32),
                        pltpu.VMEM((HEADS, tb + 8, HEAD_DIM), F32), pltpu.VMEM((HEADS, 8, HEAD_DIM), F32)],
        compiler_params=_params(dimension_semantics=("arbitrary", "arbitrary")),
    )(proj, proj, proj, hs, hs, dy, conv_w, conv_b, wa, ba, wx, bx, lam, nw)


def _group_col(tb, time_of):
    def col(off):
        return pl.BlockSpec((tb, DN_GROUP * HEAD_DIM), lambda t, hg: (time_of(t), off // DN_GROUP + hg))
    return col


def _dn_fwd(proj, y, conv_w, a_log_row, dt_row, nw, name):
    s = proj.shape[0]
    tb = min(DN_TIME_BLOCK, s)
    nt = s // tb
    nchunk = tb // CHUNK
    grp = DN_GROUP
    col = _group_col(tb, lambda t: t)

    def body(q_ref, k_ref, v_ref, z_ref, ba_ref, cwq_ref, cwk_ref, cwv_ref, al_ref, dt_ref, nw_ref, y_in_ref,
             y_ref, o_ref, st_ref, xbuf, state):
        t, hg = pl.program_id(0), pl.program_id(1)

        def chunks(a):
            return a.reshape(nchunk, CHUNK, a.shape[-1])

        prepared = []
        for gi in range(grp):
            h = hg * grp + gi
            lanes = slice(gi * HEAD_DIM, (gi + 1) * HEAD_DIM)

            @pl.when(t == 0)
            def _(h=h):
                for i in range(3):
                    xbuf[3 * h + i, pl.ds(0, 8), :] = jnp.zeros((8, HEAD_DIM), F32)
                state[h] = jnp.zeros((HEAD_DIM, HEAD_DIM), F32)

            conv = []
            for i, (ref, cw_ref) in enumerate(((q_ref, cwq_ref), (k_ref, cwk_ref), (v_ref, cwv_ref))):
                xbuf[3 * h + i, pl.ds(8, tb), :] = ref[:, lanes]
                conv.append(_conv_taps(xbuf, 3 * h + i, cw_ref[:, lanes], tb))
                xbuf[3 * h + i, pl.ds(0, 8), :] = xbuf[3 * h + i, pl.ds(tb, 8), :]
            prepared.append([chunks(a) for a in
                             _dn_prep(conv[0], conv[1], conv[2], ba_ref[...], al_ref[...], dt_ref[...], h)])
        qs, ks, vs, gs, bs = [jnp.concatenate([p[i] for p in prepared], axis=0) for i in range(5)]
        u, w, attn, qe, kdec, eglast = _dn_chunks_head(qs, ks, vs, gs, bs)
        w_u = jnp.concatenate([w, u], axis=2)
        kdec_w_u = _dot(kdec, w_u, B_TN, "bf16")
        attn_w_u = _dot(attn, w_u, B_NN, "bf16")
        st = [state[hg * grp + gi] for gi in range(grp)]
        for c in range(nchunk):
            for gi in range(grp):
                n = gi * nchunk + c
                st_ref[gi, c] = st[gi]
                st[gi] = st[gi] * eglast[n] - _NN_B(kdec_w_u[n, :, :HEAD_DIM], st[gi]) + kdec_w_u[n, :, HEAD_DIM:]
        for gi in range(grp):
            state[hg * grp + gi] = st[gi]
        states = st_ref[...].reshape(grp * nchunk, HEAD_DIM, HEAD_DIM)
        o = _dot(qe - attn_w_u[:, :, :HEAD_DIM], states, B_NN, "bf16") + attn_w_u[:, :, HEAD_DIM:]
        for gi in range(grp):
            lanes = slice(gi * HEAD_DIM, (gi + 1) * HEAD_DIM)
            o_head = o[gi * nchunk:(gi + 1) * nchunk].reshape(tb, HEAD_DIM)
            o_ref[:, lanes] = o_head
            y_ref[:, lanes] = _gated_norm(o_head, z_ref[:, lanes], nw_ref[...]).astype(BF16)

    def cw_spec(off):
        return pl.BlockSpec((4, grp * HEAD_DIM), lambda t, hg: (0, off // grp + hg))

    row128 = pl.BlockSpec((1, HEAD_DIM), lambda t, hg: (0, 0))
    return pl.pallas_call(
        body, name=name, grid=(nt, HEADS // grp),
        in_specs=[col(COL_Q), col(COL_K), col(COL_V), col(COL_DN_Z),
                  pl.BlockSpec((tb, HEAD_DIM), lambda t, hg: (t, COL_BA)),
                  cw_spec(0), cw_spec(HEADS), cw_spec(2 * HEADS), row128, row128, row128,
                  pl.BlockSpec(memory_space=pl.ANY)],
        out_specs=[col(HEADS), col(0),
                   pl.BlockSpec((grp, nchunk, HEAD_DIM, HEAD_DIM), lambda t, hg: (hg, t, 0, 0))],
        out_shape=[jax.ShapeDtypeStruct((s, 2 * D_MODEL), BF16), jax.ShapeDtypeStruct((s, D_MODEL), F32),
                   jax.ShapeDtypeStruct((HEADS, s // CHUNK, HEAD_DIM, HEAD_DIM), F32)],
        input_output_aliases={11: 0},
        scratch_shapes=[pltpu.VMEM((3 * HEADS, tb + 8, HEAD_DIM), F32), pltpu.VMEM((HEADS, HEAD_DIM, HEAD_DIM), F32)],
        compiler_params=_params(dimension_semantics=("arbitrary", "arbitrary")),
    )(proj, proj, proj, proj, proj, conv_w, conv_w, conv_w, a_log_row, dt_row, nw, y)


def _dn_bwd(proj, o, states, dy, conv_w, a_log_row, dt_row, nw, name):
    s = proj.shape[0]
    tb = min(DN_TIME_BLOCK, s)
    nt = s // tb
    nchunk = tb // CHUNK
    grp = DN_GROUP
    col = _group_col(tb, lambda t: nt - 1 - t)

    def body(q_ref, qh_ref, k_ref, kh_ref, v_ref, vh_ref, z_ref, ba_ref, o_ref, st_ref, dy_ref,
             cwq_ref, cwk_ref, cwv_ref, al_ref, dt_ref, nw_ref,
             dq_ref, dk_ref, dv_ref, dz_ref, dba_ref, dcw_ref, dal_ref, ddt_ref, dnw_ref,
             xbuf, dbuf, dstate, dst_s):
        t, hg = pl.program_id(0), pl.program_id(1)
        keep = jnp.where(t == nt - 1, 0.0, 1.0)

        @pl.when((t == 0) & (hg == 0))
        def _():
            for ref in (dal_ref, ddt_ref, dnw_ref):
                ref[...] = jnp.zeros_like(ref)

        def chunks(a):
            return a.reshape(nchunk, CHUNK, a.shape[-1])

        prepared, prep_vjps, dos = [], [], []
        for gi in range(grp):
            h = hg * grp + gi
            lanes = slice(gi * HEAD_DIM, (gi + 1) * HEAD_DIM)

            @pl.when(t == 0)
            def _(h=h):
                for i in range(3):
                    dbuf[3 * h + i, pl.ds(tb, 8), :] = jnp.zeros((8, HEAD_DIM), F32)
                    dcw_ref[3 * h + i] = jnp.zeros((4, HEAD_DIM), F32)
                dstate[h] = jnp.zeros((HEAD_DIM, HEAD_DIM), F32)

            conv = []
            for i, (ref, halo, cw_ref) in enumerate(((q_ref, qh_ref, cwq_ref), (k_ref, kh_ref, cwk_ref),
                                                     (v_ref, vh_ref, cwv_ref))):
                xbuf[3 * gi + i, pl.ds(0, 8), :] = halo[:, lanes] * keep
                xbuf[3 * gi + i, pl.ds(8, tb), :] = ref[:, lanes]
                conv.append(_conv_taps(xbuf, 3 * gi + i, cw_ref[:, lanes], tb))
            outs, prep_vjp = jax.vjp(
                lambda qc, kc, vc, ba, al, dt, h=h: _dn_prep(qc, kc, vc, ba, al, dt, h),
                conv[0], conv[1], conv[2], ba_ref[...], al_ref[...], dt_ref[...])
            prepared.append([chunks(a) for a in outs])
            prep_vjps.append(prep_vjp)
            _, norm_vjp = jax.vjp(_gated_norm, o_ref[:, lanes], z_ref[:, lanes], nw_ref[...])
            do, dz, dnw = norm_vjp(dy_ref[:, lanes])
            dz_ref[:, lanes] = dz.astype(dz_ref.dtype)
            dnw_ref[...] += dnw
            dos.append(chunks(do))
        qs, ks, vs, gs, bs = [jnp.concatenate([p[i] for p in prepared], axis=0) for i in range(5)]
        do = jnp.concatenate(dos, axis=0)
        states_in = st_ref[...].reshape(grp * nchunk, HEAD_DIM, HEAD_DIM)
        _, chunks_vjp, (w, attn, qe, kdec, eglast) = jax.vjp(_dn_chunks, qs, ks, vs, gs, bs, states_in, has_aux=True)
        kdec_w = _dot(kdec, w, B_TN, "bf16")
        fixed = _dot(qe, do, B_TN, "bf16") - _dot(w, _dot(attn, do, B_TN, "bf16"), B_TN, "bf16")
        dst = [dstate[hg * grp + gi] for gi in range(grp)]
        for c in reversed(range(nchunk)):
            for gi in range(grp):
                n = gi * nchunk + c
                dst_s[n] = dst[gi]
                dst[gi] = dst[gi] * eglast[n] - _dot(kdec_w[n], dst[gi], TN, "bf16") + fixed[n]
        for gi in range(grp):
            dstate[hg * grp + gi] = dst[gi]
        cts = chunks_vjp((do, dst_s[...]))[:5]

        dba_sum = None
        for gi in range(grp):
            h = hg * grp + gi
            lanes = slice(gi * HEAD_DIM, (gi + 1) * HEAD_DIM)
            per_head = [ct[gi * nchunk:(gi + 1) * nchunk].reshape(tb, ct.shape[-1]) for ct in cts]
            dqc, dkc, dvc, dba, dal, ddt = prep_vjps[gi](tuple(per_head))
            for i, (dxc, out, cw_ref) in enumerate(((dqc, dq_ref, cwq_ref), (dkc, dk_ref, cwk_ref),
                                                    (dvc, dv_ref, cwv_ref))):
                dx, dcw = _conv_backward(dbuf, 3 * h + i, xbuf, 3 * gi + i, cw_ref[:, lanes], dxc, tb)
                out[:, lanes] = dx.astype(out.dtype)
                dcw_ref[3 * h + i] += dcw
            dal_ref[...] += dal
            ddt_ref[...] += ddt
            dba_sum = dba if dba_sum is None else dba_sum + dba

        @pl.when(hg == 0)
        def _():
            dba_ref[...] = dba_sum.astype(dba_ref.dtype)

        @pl.when(hg > 0)
        def _():
            dba_ref[...] += dba_sum.astype(dba_ref.dtype)

    def cw_spec(off):
        return pl.BlockSpec((4, grp * HEAD_DIM), lambda t, hg: (0, off // grp + hg))

    def halo(off):
        per = tb // 8
        return pl.BlockSpec((8, grp * HEAD_DIM),
                            lambda t, hg: (jnp.maximum((nt - 1 - t) * per - 1, 0), off // grp + hg))

    def whole(shape):
        return pl.BlockSpec(shape, lambda t, hg: (0,) * len(shape))

    row128 = whole((1, HEAD_DIM))
    blk = (tb, HEAD_DIM)
    act = jax.ShapeDtypeStruct((s, D_MODEL), BF16)
    row_out = jax.ShapeDtypeStruct((1, HEAD_DIM), F32)
    return pl.pallas_call(
        body, name=name, grid=(nt, HEADS // grp),
        in_specs=[col(COL_Q), halo(COL_Q), col(COL_K), halo(COL_K), col(COL_V), halo(COL_V), col(COL_DN_Z),
                  pl.BlockSpec(blk, lambda t, hg: (nt - 1 - t, COL_BA)), col(0),
                  pl.BlockSpec((grp, nchunk, HEAD_DIM, HEAD_DIM), lambda t, hg: (hg, nt - 1 - t, 0, 0)), col(HEADS),
                  cw_spec(0), cw_spec(HEADS), cw_spec(2 * HEADS), row128, row128, row128],
        out_specs=[col(0), col(0), col(0), col(0), pl.BlockSpec(blk, lambda t, hg: (nt - 1 - t, 0)),
                   whole((3 * HEADS, 4, HEAD_DIM)), row128, row128, row128],
        out_shape=[act, act, act, act, jax.ShapeDtypeStruct((s, HEAD_DIM), F32),
                   jax.ShapeDtypeStruct((3 * HEADS, 4, HEAD_DIM), F32), row_out, row_out, row_out],
        scratch_shapes=[pltpu.VMEM((3 * grp, tb + 8, HEAD_DIM), F32), pltpu.VMEM((3 * HEADS, tb + 8, HEAD_DIM), F32),
                        pltpu.VMEM((HEADS, HEAD_DIM, HEAD_DIM), F32),
                        pltpu.VMEM((grp * nchunk, HEAD_DIM, HEAD_DIM), F32)],
        compiler_params=_params(dimension_semantics=("arbitrary", "arbitrary")),
    )(proj, proj, proj, proj, proj, proj, proj, proj, o, states, dy, conv_w, conv_w, conv_w, a_log_row, dt_row, nw)


def _mesh_position():
    x, y, c = lax.axis_index("x"), lax.axis_index("y"), lax.axis_index("c")
    return x, y, c, 4 * x + 2 * y + c


def _peer(k, x, y, c):
    px = 1 - x if k & 4 else x
    py = 1 - y if k & 2 else y
    pc = 1 - c if k & 1 else c
    return (px, py, pc), 4 * px + 2 * py + pc


def _exchange_copies(ins, lands, scatter, send_sems, recv_sems, receives=True):
    x, y, c, me = _mesh_position()
    sends, recvs = [], []
    for i, (src, land) in enumerate(zip(ins, lands)):
        for k in range(1, N_DEV):
            peer, peer_id = _peer(k, x, y, c)
            sem = i * (N_DEV - 1) + k - 1
            for dst, out in ((me, sends), (peer_id, recvs)) if receives else ((me, sends),):
                out.append(pltpu.make_async_remote_copy(
                    src_ref=src.at[peer_id] if scatter[i] else src, dst_ref=land.at[dst],
                    send_sem=send_sems.at[sem], recv_sem=recv_sems.at[sem],
                    device_id=peer, device_id_type=pl.DeviceIdType.MESH))
    return sends, recvs


def _landing_shape(a, scatter):
    return a.shape if scatter else (N_DEV,) + a.shape


def _direct_exchange(arrays, scatter, name):
    n = len(arrays)
    out_shapes = [jax.ShapeDtypeStruct(_landing_shape(a, sc), a.dtype) for a, sc in zip(arrays, scatter)]

    def body(*refs):
        ins, outs = refs[:n], refs[n:2 * n]
        send_sems, recv_sems, local_sems = refs[2 * n:]
        me = _mesh_position()[3]
        local = [pltpu.make_async_copy(ins[i].at[me] if scatter[i] else ins[i], outs[i].at[me], local_sems.at[i])
                 for i in range(n)]
        sends, recvs = _exchange_copies(ins, outs, scatter, send_sems, recv_sems)
        for cp in local + sends:
            cp.start()
        for cp in recvs:
            cp.wait_recv()
        for cp in sends:
            cp.wait_send()
        for cp in local:
            cp.wait()

    hbm = pl.BlockSpec(memory_space=pl.ANY)
    return pl.pallas_call(
        body, name=name, in_specs=[hbm] * n, out_specs=[hbm] * n, out_shape=out_shapes,
        scratch_shapes=[pltpu.SemaphoreType.DMA((n * (N_DEV - 1),)), pltpu.SemaphoreType.DMA((n * (N_DEV - 1),)),
                        pltpu.SemaphoreType.DMA((n,))],
    )(*arrays)


def _two_level_gather(arrays, name):
    n = len(arrays)
    per = N_DEV - 1

    def body(*refs):
        ins, outs = refs[:n], refs[n:2 * n]
        send_sems, recv_sems, local_sems = refs[2 * n:]
        x, y, c, me = _mesh_position()
        sibling = (x, y, 1 - c)
        chips = [(1 - x, y), (x, 1 - y), (1 - x, 1 - y)]

        def copy(i, k, block, to, src=None):
            slot = outs[i].at[4 * block[0] + 2 * block[1] + block[2]]
            return pltpu.make_async_remote_copy(
                src_ref=slot if src is None else src, dst_ref=slot,
                send_sem=send_sems.at[i * per + k], recv_sem=recv_sems.at[i * per + k],
                device_id=to, device_id_type=pl.DeviceIdType.MESH)

        local = [pltpu.make_async_copy(ins[i], outs[i].at[me], local_sems.at[i]) for i in range(n)]
        first = []
        for i in range(n):
            first.append(copy(i, 0, (x, y, c), sibling, src=ins[i]))
            first += [copy(i, 1 + j, (x, y, c), (*chip, c), src=ins[i]) for j, chip in enumerate(chips)]
        for cp in local + first:
            cp.start()
        passed = []
        for i in range(n):
            for j, chip in enumerate(chips):
                copy(i, 1 + j, (*chip, c), (x, y, c)).wait_recv()
                passed.append(copy(i, 4 + j, (*chip, c), sibling))
                passed[-1].start()
        for i in range(n):
            copy(i, 0, sibling, (x, y, c)).wait_recv()
            for j, chip in enumerate(chips):
                copy(i, 4 + j, (*chip, 1 - c), (x, y, c)).wait_recv()
        for cp in first + passed:
            cp.wait_send()
        for cp in local:
            cp.wait()

    hbm = pl.BlockSpec(memory_space=pl.ANY)
    return pl.pallas_call(
        body, name=name, in_specs=[hbm] * n, out_specs=[hbm] * n,
        out_shape=[jax.ShapeDtypeStruct((N_DEV,) + a.shape, a.dtype) for a in arrays],
        scratch_shapes=[pltpu.SemaphoreType.DMA((n * per,)), pltpu.SemaphoreType.DMA((n * per,)),
                        pltpu.SemaphoreType.DMA((n,))],
    )(*arrays)


_HBM = pl.BlockSpec(memory_space=pltpu.HBM)
_SEM = pl.BlockSpec(memory_space=pltpu.SEMAPHORE)
_DATAFLOW = pltpu.SideEffectType.DATAFLOW_SIDE_EFFECTING


def _exchange_start(arrays, scatter, name):
    n = len(arrays)
    srcs = [pltpu.with_memory_space_constraint(a, pltpu.HBM) for a in arrays]
    lands = [pltpu.with_memory_space_constraint(lax.empty(_landing_shape(a, sc), a.dtype), pltpu.HBM)
             for a, sc in zip(arrays, scatter)]
    nsem = n * (N_DEV - 1)

    def body(*refs):
        ins, zones = refs[:n], refs[n:2 * n]
        send_sems, recv_sems = refs[2 * n], refs[2 * n + 1]
        token = refs[-1]
        sends, _ = _exchange_copies(ins, zones, scatter, send_sems, recv_sems, receives=False)
        for cp in sends:
            cp.start()
        token[...] = jnp.zeros_like(token)

    res = pl.pallas_call(
        body, name=name,
        out_shape=(pltpu.SemaphoreType.DMA((nsem,)), pltpu.SemaphoreType.DMA((nsem,)),
                   *[pltpu.HBM(a.shape, a.dtype) for a in srcs + lands], jax.ShapeDtypeStruct((8, HEAD_DIM), F32)),
        in_specs=[_HBM] * (2 * n),
        out_specs=(_SEM, _SEM, *[_HBM] * (2 * n), pl.BlockSpec(memory_space=pltpu.VMEM)),
        input_output_aliases={i: 2 + i for i in range(2 * n)},
        compiler_params=pltpu.CompilerParams(has_side_effects=_DATAFLOW),
    )(*srcs, *lands)
    return dict(sems=res[:2], srcs=res[2:2 + n], lands=res[2 + n:2 + 2 * n], token_block=res[-1],
                token=res[-1][0, 0], scatter=scatter)


def _exchange_wait(started, after, name):
    scatter = started["scatter"]
    n = len(scatter)

    def body(*refs):
        ins, zones = refs[:n], refs[n:2 * n]
        send_sems, recv_sems = refs[2 * n], refs[2 * n + 1]
        sends, recvs = _exchange_copies(ins, zones, scatter, send_sems, recv_sems)
        for cp in sends:
            cp.wait_send()
        for cp in recvs:
            cp.wait_recv()

    thru = list(started["srcs"]) + list(started["lands"])
    res = pl.pallas_call(
        body, name=name, out_shape=[pltpu.HBM(a.shape, a.dtype) for a in thru],
        in_specs=[_HBM] * (2 * n) + [_SEM, _SEM, pl.BlockSpec(memory_space=pl.ANY)], out_specs=[_HBM] * (2 * n),
        input_output_aliases={i: i for i in range(2 * n)},
        compiler_params=pltpu.CompilerParams(has_side_effects=_DATAFLOW),
    )(*thru, *started["sems"], after)
    me = 4 * lax.axis_index("x") + 2 * lax.axis_index("y") + lax.axis_index("c")
    out = []
    for src, got, sc in zip(res[:n], res[n:], scatter):
        own = lax.dynamic_index_in_dim(src, me, 0, keepdims=False) if sc else src
        out.append(lax.dynamic_update_index_in_dim(got, own, me, 0))
    return out


def _adamw(parts, w, m, v, name, rows_per_step, row_offset=0, into=None):
    rows, cols = parts.shape[1:]
    tr = min(rows_per_step, rows)
    assert rows % tr == 0 and row_offset % tr == 0, (name, rows, tr, row_offset)
    first = row_offset // tr
    c1 = 1.0 / (1.0 - ADAM_B1 ** ADAM_STEP)
    c2 = 1.0 / (1.0 - ADAM_B2 ** ADAM_STEP)

    def body(p_ref, w_ref, m_ref, v_ref, *rest):
        g_ref, d_ref, nm_ref, nv_ref = rest[-4:]
        g = p_ref[0].astype(F32)
        for d in range(1, N_DEV):
            g = g + p_ref[d].astype(F32)
        nm = ADAM_B1 * m_ref[...] + (1.0 - ADAM_B1) * g
        nv = ADAM_B2 * v_ref[...] + (1.0 - ADAM_B2) * (g * g)
        g_ref[...] = g
        nm_ref[...] = nm
        nv_ref[...] = nv
        d_ref[...] = -ADAM_LR * ((nm * c1) / (jnp.sqrt(nv * c2) + ADAM_EPS) + ADAM_WD * w_ref[...])

    blk = pl.BlockSpec((tr, cols), lambda i: (i + first, 0))
    shape = jax.ShapeDtypeStruct(w.shape, F32)
    prior = [] if into is None else list(into)
    return pl.pallas_call(
        body, name=name, grid=(rows // tr,),
        in_specs=[pl.BlockSpec((N_DEV, tr, cols), lambda i: (0, i, 0)), blk, blk, blk]
        + [pl.BlockSpec(memory_space=pl.ANY)] * len(prior),
        out_specs=[blk] * 4, out_shape=[shape] * 4,
        input_output_aliases={4 + j: j for j in range(len(prior))}, compiler_params=_params(),
    )(parts, w, m, v, *prior)


_LAYERED = ("norm_w", "lru_conv_b", "lru_wa", "lru_ba", "lru_wx", "lru_bx", "lru_lambda", "lru_norm_w",
            "dn_A_log", "dn_dt_bias", "dn_norm_w")
_PACK_LRU = _LAYERED[1:8]
_PACK_LAST = _LAYERED[:1] + _LAYERED[8:]
_WEIGHTS = ("norm_w", "w_in", "lru_conv_w", "lru_conv_b", "lru_wa", "lru_ba", "lru_wx", "lru_bx", "lru_lambda",
            "lru_norm_w", "dn_conv_w", "dn_A_log", "dn_dt_bias", "dn_norm_w", "w_out", "final_norm_w")


def _pack_layer(tree, layer, tail=(), names=_LAYERED):
    rows = []
    for name in names:
        a = tree[name][layer]
        if a.shape[-1] == HEADS:
            a = jnp.pad(a, (0, HEAD_DIM - HEADS))
        rows.append(a.reshape(-1, HEAD_DIM))
    rows += [t.reshape(-1, HEAD_DIM) for t in tail]
    packed = jnp.concatenate(rows, axis=0)
    return jnp.pad(packed, ((0, (-packed.shape[0]) % 8), (0, 0)))


def _unpack_layer(packed, like, names=_LAYERED):
    out, at = {}, 0
    for name in names:
        shape = like[name].shape[1:]
        if shape[-1] == HEADS:
            n = 1
            out[name] = packed[at, :HEADS]
        else:
            n = like[name][0].size // HEAD_DIM
            out[name] = packed[at:at + n].reshape(shape)
        at += n
    return out, at


def _heads_to_channels(a):
    return jnp.transpose(a, (1, 0, 2)).reshape(a.shape[1], HEADS * HEAD_DIM)


def kernel(x, norm_w, w_in, lru_conv_w, lru_conv_b, lru_wa, lru_ba, lru_wx, lru_bx, lru_lambda, lru_norm_w, dn_conv_w, dn_A_log, dn_dt_bias, dn_norm_w, w_out, final_norm_w, loss_target, m_norm_w, m_w_in, m_lru_conv_w, m_lru_conv_b, m_lru_wa, m_lru_ba, m_lru_wx, m_lru_bx, m_lru_lambda, m_lru_norm_w, m_dn_conv_w, m_dn_A_log, m_dn_dt_bias, m_dn_norm_w, m_w_out, m_final_norm_w, v_norm_w, v_w_in, v_lru_conv_w, v_lru_conv_b, v_lru_wa, v_lru_ba, v_lru_wx, v_lru_bx, v_lru_lambda, v_lru_norm_w, v_dn_conv_w, v_dn_A_log, v_dn_dt_bias, v_dn_norm_w, v_w_out, v_final_norm_w):
    weights = dict(norm_w=norm_w, w_in=w_in, lru_conv_w=lru_conv_w, lru_conv_b=lru_conv_b, lru_wa=lru_wa,
                   lru_ba=lru_ba, lru_wx=lru_wx, lru_bx=lru_bx, lru_lambda=lru_lambda, lru_norm_w=lru_norm_w,
                   dn_conv_w=dn_conv_w, dn_A_log=dn_A_log, dn_dt_bias=dn_dt_bias, dn_norm_w=dn_norm_w,
                   w_out=w_out, final_norm_w=final_norm_w)
    mom_m = dict(norm_w=m_norm_w, w_in=m_w_in, lru_conv_w=m_lru_conv_w, lru_conv_b=m_lru_conv_b, lru_wa=m_lru_wa,
                 lru_ba=m_lru_ba, lru_wx=m_lru_wx, lru_bx=m_lru_bx, lru_lambda=m_lru_lambda,
                 lru_norm_w=m_lru_norm_w, dn_conv_w=m_dn_conv_w, dn_A_log=m_dn_A_log, dn_dt_bias=m_dn_dt_bias,
                 dn_norm_w=m_dn_norm_w, w_out=m_w_out, final_norm_w=m_final_norm_w)
    mom_v = dict(norm_w=v_norm_w, w_in=v_w_in, lru_conv_w=v_lru_conv_w, lru_conv_b=v_lru_conv_b, lru_wa=v_lru_wa,
                 lru_ba=v_lru_ba, lru_wx=v_lru_wx, lru_bx=v_lru_bx, lru_lambda=v_lru_lambda,
                 lru_norm_w=v_lru_norm_w, dn_conv_w=v_dn_conv_w, dn_A_log=v_dn_A_log, dn_dt_bias=v_dn_dt_bias,
                 dn_norm_w=v_dn_norm_w, w_out=v_w_out, final_norm_w=v_final_norm_w)
    depth = norm_w.shape[0]
    xs = x[0]
    s = xs.shape[0]
    tm = min(1024, s)

    assert depth >= 2, depth

    def row(a):
        return a.reshape(1, -1)

    def pad_row(a):
        return jnp.pad(a, (0, HEAD_DIM - a.shape[0])).reshape(1, HEAD_DIM)

    def full_w_in(g):
        w = jnp.transpose(g, (1, 2, 0, 3)).reshape(g.shape[1], D_MODEL, D_IN)
        return jnp.pad(w, ((0, 0), (0, 0), (0, D_IN_PAD - D_IN)))

    g_win0, g_lcw, g_dcw = _two_level_gather([w_in[:1].astype(BF16), lru_conv_w, dn_conv_w], "gather_first")
    rest = _exchange_start([w_in[1:].astype(BF16), w_out.astype(BF16)], [False] * 2, "gather_rest_start")
    win = [full_w_in(g_win0)[0]]
    wout = None
    lcw = jnp.transpose(g_lcw, (1, 2, 0, 3)).reshape(depth, 4, D_MODEL)
    dcw = jnp.transpose(g_dcw, (1, 2, 0, 3)).reshape(depth, 4, 3 * D_MODEL)

    saved = []
    cur = xs
    for l in range(depth):
        nw_row = row(norm_w[l]) + rest["token"] if l == 0 else row(norm_w[l])
        hn = _rmsnorm_fwd(cur, nw_row, f"norm_fwd_{l}")
        proj = _matmul(hn, win[l], "nn", tm, 896, D_MODEL, f"in_proj_{l}")
        y_lru, hs = _lru_fwd(proj, lcw[l], row(lru_conv_b[l]), lru_wa[l], row(lru_ba[l]), lru_wx[l], row(lru_bx[l]),
                             row(lru_lambda[l]), row(lru_norm_w[l]), f"lru_fwd_{l}")
        ycat, o_dn, states = _dn_fwd(proj, y_lru, dcw[l], pad_row(dn_A_log[l]), pad_row(dn_dt_bias[l]),
                                     row(dn_norm_w[l]), f"dn_fwd_{l}")
        if l == 0:
            g_win_rest, g_wout = _exchange_wait(rest, ycat, "gather_rest_wait")
            win += list(full_w_in(g_win_rest))
            wout = jnp.transpose(g_wout, (1, 0, 2, 3)).reshape(depth, 2 * D_MODEL, D_MODEL)
        nxt = _matmul(ycat, wout[l], "nn", tm, D_MODEL, 2 * D_MODEL, f"out_proj_{l}", add=cur)
        saved.append((cur, hn, proj, hs, o_dn, states, ycat))
        cur = nxt
    loss_part, dx, d_final = _final_loss(cur, row(final_norm_w), loss_target[0], "final_loss")

    def win_slots(g):
        return jnp.transpose(g.reshape(D_MODEL, N_DEV, D_IN // N_DEV), (1, 0, 2))

    def wout_slots(g):
        return g.reshape(N_DEV, 2 * D_MODEL // N_DEV, D_MODEL)

    grads = {k: [None] * depth for k in _WEIGHTS if k not in ("final_norm_w", "w_in", "w_out")}
    started = {}
    token = None
    for l in reversed(range(depth)):
        x_in, hn, proj, hs, o_dn, states, ycat = saved[l]
        dy = _matmul(dx, wout[l], "nt", tm, D_MODEL, D_MODEL, f"out_proj_dy_{l}")
        g_wout_l = _matmul(ycat, dx, "tn", D_MODEL, D_MODEL, tm, f"out_proj_dw_{l}", out_dtype=BF16)
        if l == 0:
            started["w_out_0"] = _exchange_start([wout_slots(g_wout_l)], [True], "exchange_w_out_0_start")
            token = token + started["w_out_0"]["token"]
        cb_row = row(lru_conv_b[l]) if token is None else row(lru_conv_b[l]) + token
        (dlx, dlz, g_lcw, g_lcb, g_wa, g_ba, g_wx, g_bx, g_lam, g_lnw) = _lru_bwd(
            proj, hs, dy, lcw[l], cb_row, lru_wa[l], row(lru_ba[l]), lru_wx[l], row(lru_bx[l]),
            row(lru_lambda[l]), row(lru_norm_w[l]), f"lru_bwd_{l}")
        grads["lru_conv_w"][l] = _heads_to_channels(g_lcw)
        grads["lru_conv_b"][l] = g_lcb.reshape(D_MODEL)
        grads["lru_wa"][l] = g_wa
        grads["lru_ba"][l] = g_ba.reshape(D_MODEL)
        grads["lru_wx"][l] = g_wx
        grads["lru_bx"][l] = g_bx.reshape(D_MODEL)
        grads["lru_lambda"][l] = g_lam.reshape(D_MODEL)
        grads["lru_norm_w"][l] = g_lnw.reshape(D_MODEL)
        al_row = pad_row(dn_A_log[l])
        if l == 0:
            started["pack_0"] = _exchange_start([_pack_layer(grads, 0, names=_PACK_LRU)], [False],
                                                "exchange_pack_0_start")
            al_row = al_row + started["pack_0"]["token"]
        (dq, dk, dv, ddz, dba, g_dcw3, g_al, g_dt, g_dnw) = _dn_bwd(
            proj, o_dn, states, dy, dcw[l], al_row, pad_row(dn_dt_bias[l]), row(dn_norm_w[l]), f"dn_bwd_{l}")
        g_dcw3 = g_dcw3.reshape(HEADS, 3, 4, HEAD_DIM)
        grads["dn_conv_w"][l] = jnp.concatenate([_heads_to_channels(g_dcw3[:, i]) for i in range(3)], axis=1)
        grads["dn_A_log"][l] = g_al[0, :HEADS]
        grads["dn_dt_bias"][l] = g_dt[0, :HEADS]
        grads["dn_norm_w"][l] = g_dnw.reshape(HEAD_DIM)
        dep = None
        pieces = [dlx, dlz, dq, dk, dv, ddz]
        wide = len(pieces) * D_MODEL
        dba = dba.astype(BF16)
        g_win_l = jnp.concatenate(
            [_matmul_tn_parts(hn, pieces, D_MODEL, D_MODEL, tm, f"in_proj_dw_{l}", BF16),
             _matmul(hn, dba, "tn", D_MODEL, HEAD_DIM, tm, f"in_proj_dw_gates_{l}", out_dtype=BF16)[:, :D_IN - wide]],
            axis=1)
        if l == 0:
            started[0] = _exchange_start([win_slots(g_win_l)], [True], "exchange_0_start")
            dep = started[0]["token_block"]
        dh = _matmul_nt_parts(pieces, dba, win[l], tm, D_MODEL, f"in_proj_dh_{l}", dep=dep)
        dx, g_nw = _rmsnorm_bwd(x_in, row(norm_w[l]), dh, dx, f"norm_bwd_{l}")
        grads["norm_w"][l] = g_nw.reshape(D_MODEL)
        if l > 0:
            tail = (d_final, loss_part) if l == depth - 1 else ()
            started[l] = _exchange_start([win_slots(g_win_l), wout_slots(g_wout_l), _pack_layer(grads, l, tail)],
                                         [True, True, False], f"exchange_{l}_start")
            token = started[l]["token"]

    def conv_slots(a):
        dd, r, cc = a.shape
        return jnp.transpose(a.reshape(dd, r, N_DEV, cc // N_DEV), (2, 0, 1, 3))

    small = _exchange_start(
        [conv_slots(jnp.stack(grads["lru_conv_w"])), conv_slots(jnp.stack(grads["dn_conv_w"])),
         _pack_layer(grads, 0, names=_PACK_LAST)], [True, True, False], "exchange_small_start")

    new = {}
    flat_in = (depth * D_MODEL, D_IN // N_DEV)
    flat_out = (depth * 2 * D_MODEL // N_DEV, D_MODEL)
    zero_row = jnp.zeros((1, HEAD_DIM), F32)

    def adamw_pack(parts, layer, names=_LAYERED, name="adamw_small"):
        tails = [(t, zero_row) if layer == depth - 1 else () for t in (final_norm_w, m_final_norm_w, v_final_norm_w)]
        return _adamw(parts, _pack_layer(weights, layer, tails[0], names), _pack_layer(mom_m, layer, tails[1], names),
                      _pack_layer(mom_v, layer, tails[2], names), f"{name}_{layer}", parts.shape[1])

    def adamw_w_in(parts, layer, into):
        return _adamw(parts, w_in.reshape(flat_in), m_w_in.reshape(flat_in), v_w_in.reshape(flat_in),
                      f"adamw_w_in_{layer}", 256, layer * D_MODEL, into)

    def adamw_w_out(parts, layer, into):
        return _adamw(parts, w_out.reshape(flat_out), m_w_out.reshape(flat_out), v_w_out.reshape(flat_out),
                      f"adamw_w_out_{layer}", 256, layer * flat_out[0] // depth, into)

    acc_in = acc_out = None
    packs = [None] * depth
    after = small["token_block"]
    for l in reversed(range(1, depth)):
        r_win, r_wout, r_pack = _exchange_wait(started[l], after, f"exchange_{l}_wait")
        acc_in = adamw_w_in(r_win, l, acc_in)
        acc_out = adamw_w_out(r_wout, l, acc_out)
        packs[l] = adamw_pack(r_pack, l)
        after = packs[l][0]
    (r_wout,) = _exchange_wait(started["w_out_0"], after, "exchange_w_out_0_wait")
    acc_out = adamw_w_out(r_wout, 0, acc_out)
    (r_win,) = _exchange_wait(started[0], acc_out[0], "exchange_0_wait")
    acc_in = adamw_w_in(r_win, 0, acc_in)
    (r_pack,) = _exchange_wait(started["pack_0"], acc_in[0], "exchange_pack_0_wait")
    packs[0] = adamw_pack(r_pack, 0, _PACK_LRU)
    r_lcw, r_dcw, r_last = _exchange_wait(small, packs[0][0], "exchange_small_wait")
    for name, parts in (("lru_conv_w", r_lcw), ("dn_conv_w", r_dcw)):
        w = weights[name]
        flat = (-1, w.shape[-1])
        outs = _adamw(parts.reshape((N_DEV,) + (w.size // w.shape[-1], w.shape[-1])), w.reshape(flat),
                      mom_m[name].reshape(flat), mom_v[name].reshape(flat), f"adamw_{name}", 8)
        new[name] = [a.reshape(w.shape) for a in outs]
    last_0 = adamw_pack(r_last, 0, _PACK_LAST, "adamw_last")
    new["w_in"] = [a.reshape(w_in.shape) for a in acc_in]
    new["w_out"] = [a.reshape(w_out.shape) for a in acc_out]
    for i in range(4):
        layers = [{**_unpack_layer(packs[0][i], weights, _PACK_LRU)[0],
                   **_unpack_layer(last_0[i], weights, _PACK_LAST)[0]}]
        layers += [_unpack_layer(packs[l][i], weights)[0] for l in range(1, depth)]
        for name in _LAYERED:
            new.setdefault(name, []).append(jnp.stack([layer[name] for layer in layers]))
    tail_at = _unpack_layer(packs[depth - 1][0], weights)[1]
    rows_final = D_MODEL // HEAD_DIM
    new["final_norm_w"] = [packs[depth - 1][i][tail_at:tail_at + rows_final].reshape(D_MODEL) for i in range(4)]
    loss = packs[depth - 1][0][tail_at + rows_final, 0]
    out = [loss, dx.reshape(x.shape)]
    for i in range(4):
        out += [new[name][i] for name in _WEIGHTS]
    return tuple(out)
```

```python
import functools

import jax
import jax.numpy as jnp
from jax import lax
from jax.experimental import pallas as pl
from jax.experimental.pallas import tpu as pltpu

F32 = jnp.float32
BF16 = jnp.bfloat16

N_DEV = 8
D_MODEL = 1024
HEADS = 8
HEAD_DIM = 128
CHUNK = 64
D_IN = 6160
D_IN_PAD = 6272
COL_LRU_X, COL_LRU_Z, COL_Q, COL_K, COL_V, COL_DN_Z, COL_BA = 0, 8, 16, 24, 32, 40, 48
LRU_C = 8.0
EPS = 1e-6
ADAM_LR, ADAM_B1, ADAM_B2, ADAM_EPS, ADAM_WD, ADAM_STEP = 0.001, 0.9, 0.999, 1e-08, 0.01, 10
TIME_BLOCK = 1024
DN_TIME_BLOCK = 128
DN_GROUP = 8
VMEM_LIMIT = 56 * 1024 * 1024

NN = (((1,), (0,)), ((), ()))
NT = (((1,), (1,)), ((), ()))
TN = (((0,), (0,)), ((), ()))


B_NN = (((2,), (1,)), ((0,), (0,)))
B_NT = (((2,), (2,)), ((0,), (0,)))
B_TN = (((1,), (1,)), ((0,), (0,)))


def _split_bf16(x):
    hi = x.astype(BF16)
    return hi, (x - hi.astype(F32)).astype(BF16)


def _dot(a, b, dims, prec):
    if prec == "bf16":
        return lax.dot_general(a.astype(BF16), b.astype(BF16), dims, preferred_element_type=F32)
    a1, a2 = _split_bf16(a)
    b1, b2 = _split_bf16(b)
    dg = functools.partial(lax.dot_general, dimension_numbers=dims, preferred_element_type=F32)
    return dg(a1, b1) + (dg(a1, b2) + dg(a2, b1))


def _make_mm(prec, nn_dims, nt_dims, tn_dims):
    @jax.custom_vjp
    def nn(a, b):
        return _dot(a, b, nn_dims, prec)

    @jax.custom_vjp
    def nt(a, b):
        return _dot(a, b, nt_dims, prec)

    @jax.custom_vjp
    def tn(a, b):
        return _dot(a, b, tn_dims, prec)

    nn.defvjp(lambda a, b: (_dot(a, b, nn_dims, prec), (a, b)),
              lambda r, g: (_dot(g, r[1], nt_dims, prec), _dot(r[0], g, tn_dims, prec)))
    nt.defvjp(lambda a, b: (_dot(a, b, nt_dims, prec), (a, b)),
              lambda r, g: (_dot(g, r[1], nn_dims, prec), _dot(g, r[0], tn_dims, prec)))
    tn.defvjp(lambda a, b: (_dot(a, b, tn_dims, prec), (a, b)),
              lambda r, g: (_dot(r[1], g, nt_dims, prec), _dot(r[0], g, nn_dims, prec)))
    return nn, nt, tn


_NN_B, _NT_B, _TN_B = _make_mm("bf16", NN, NT, TN)
_BNN, _BNT, _BTN = _make_mm("bf16", B_NN, B_NT, B_TN)


@jax.custom_vjp
def _unit_lower_inverse(a):
    n = a.shape[-1]
    eye = (lax.broadcasted_iota(jnp.int32, a.shape, 1) == lax.broadcasted_iota(jnp.int32, a.shape, 2)).astype(F32)
    dg = functools.partial(lax.dot_general, dimension_numbers=B_NN, preferred_element_type=F32)
    inv = eye - a
    pw = _dot(a, a, B_NN, "bf16x3")
    steps = n.bit_length() - 2
    for j in range(steps):
        i1, i2 = _split_bf16(inv)
        p1, p2 = _split_bf16(pw)
        square = j + 1 < steps
        by_hi = dg(jnp.concatenate([i1, i2, p1, p2] if square else [i1, i2], axis=1), p1)
        by_lo = dg(jnp.concatenate([i1, p1], axis=1) if square else i1, p2)
        inv = inv + (by_hi[:, :n] + (by_lo[:, :n] + by_hi[:, n:2 * n]))
        if square:
            pw = by_hi[:, 2 * n:3 * n] + (by_lo[:, n:] + by_hi[:, 3 * n:])
    return inv


def _uli_fwd(a):
    inv = _unit_lower_inverse(a)
    return inv, inv


def _uli_bwd(inv, g):
    return (-_dot(_dot(inv, g, B_TN, "bf16"), inv, B_NT, "bf16"),)


_unit_lower_inverse.defvjp(_uli_fwd, _uli_bwd)


@jax.custom_vjp
def _known_inverse(a, inv):
    return inv


_known_inverse.defvjp(lambda a, inv: (inv, inv), lambda inv, g: (_uli_bwd(inv, g)[0], jnp.zeros_like(inv)))


def _rows2(y, m):
    return y[:, :m], y[:, m:]


@jax.custom_vjp
def _pair_nn(x1, x2, r):
    return _rows2(_dot(jnp.concatenate([x1, x2], axis=1), r, B_NN, "bf16"), x1.shape[1])


def _pair_nn_bwd(res, g):
    x1, x2, r = res
    g = jnp.concatenate(g, axis=1)
    dx1, dx2 = _rows2(_dot(g, r, B_NT, "bf16"), x1.shape[1])
    return dx1, dx2, _dot(jnp.concatenate([x1, x2], axis=1), g, B_TN, "bf16")


_pair_nn.defvjp(lambda x1, x2, r: (_pair_nn(x1, x2, r), (x1, x2, r)), _pair_nn_bwd)


@jax.custom_vjp
def _pair_nt(x1, x2, r):
    return _rows2(_dot(jnp.concatenate([x1, x2], axis=1), r, B_NT, "bf16"), x1.shape[1])


def _pair_nt_bwd(res, g):
    x1, x2, r = res
    g = jnp.concatenate(g, axis=1)
    dx1, dx2 = _rows2(_dot(g, r, B_NN, "bf16"), x1.shape[1])
    return dx1, dx2, _dot(g, jnp.concatenate([x1, x2], axis=1), B_TN, "bf16")


_pair_nt.defvjp(lambda x1, x2, r: (_pair_nt(x1, x2, r), (x1, x2, r)), _pair_nt_bwd)


@jax.custom_vjp
def _wide_nn(l, r1, r2):
    y = _dot(l, jnp.concatenate([r1, r2], axis=2), B_NN, "bf16")
    return y[:, :, :r1.shape[2]], y[:, :, r1.shape[2]:]


def _wide_nn_bwd(res, g):
    l, r1, r2 = res
    g = jnp.concatenate(g, axis=2)
    dr = _dot(l, g, B_TN, "bf16")
    return (_dot(g, jnp.concatenate([r1, r2], axis=2), B_NT, "bf16"), dr[:, :, :r1.shape[2]], dr[:, :, r1.shape[2]:])


_wide_nn.defvjp(lambda l, r1, r2: (_wide_nn(l, r1, r2), (l, r1, r2)), _wide_nn_bwd)


def _lower_ones(batch, n):
    shape = (batch, n, n)
    return (lax.broadcasted_iota(jnp.int32, shape, 1) >= lax.broadcasted_iota(jnp.int32, shape, 2)).astype(BF16)


@jax.custom_vjp
def _chunk_cumsum(g):
    tri = _lower_ones(g.shape[0], g.shape[1])
    g1, g2 = _split_bf16(g)
    g3 = (g - g1.astype(F32) - g2.astype(F32)).astype(BF16)
    dg = functools.partial(lax.dot_general, dimension_numbers=B_NN, preferred_element_type=F32)
    return dg(tri, g1) + (dg(tri, g2) + dg(tri, g3))


def _chunk_cumsum_bwd(_, ct):
    tri = _lower_ones(ct.shape[0], ct.shape[1])
    c1, c2 = _split_bf16(ct)
    dg = functools.partial(lax.dot_general, dimension_numbers=B_TN, preferred_element_type=F32)
    return (dg(tri, c1) + dg(tri, c2),)


_chunk_cumsum.defvjp(lambda g: (_chunk_cumsum(g), None), _chunk_cumsum_bwd)


def _expm1(x):
    small = x * (1.0 + x * (0.5 + x * (1.0 / 6 + x * (1.0 / 24 + x * (1.0 / 120 + x * (1.0 / 720))))))
    return jnp.where(jnp.abs(x) < 0.2, small, jnp.exp(x) - 1.0)


def _sigmoid(x):
    return 0.5 * jnp.tanh(0.5 * x) + 0.5


def _silu(x):
    return x * _sigmoid(x)


def _softplus(x):
    return jnp.maximum(x, 0.0) + jnp.log(1.0 + jnp.exp(-jnp.abs(x)))


def _rmsnorm(x, w):
    return x * lax.rsqrt(jnp.mean(x * x, axis=-1, keepdims=True) + EPS) * w


def _gated_norm(o, z, w):
    return o * lax.rsqrt(jnp.mean(o * o, axis=-1, keepdims=True) + EPS) * w * _silu(z)


def _lru_gates(xc, wa, ba, wx, bx, lam):
    r = _sigmoid(_NN_B(xc, wa) + ba)
    i = _sigmoid(_NN_B(xc, wx) + bx)
    log_a = -LRU_C * r * _softplus(-lam)
    a = jnp.exp(log_a)
    mult = jnp.sqrt(-_expm1(2.0 * log_a))
    return a, mult * (i * xc)


SCAN_ROWS = 32


def _scan_forward(a, b, h0):
    rows = a.shape[0]
    piece = min(SCAN_ROWS, rows)
    pos = lax.broadcasted_iota(jnp.int32, a.shape, 0) % piece
    k = 1
    while k < piece:
        seen = pos >= k
        b = jnp.where(seen, a * pltpu.roll(b, k, 0) + b, b)
        a = jnp.where(seen, a * pltpu.roll(a, k, 0), a)
        k *= 2
    out, entering = [], h0
    for lo in range(0, rows, piece):
        out.append(b[lo:lo + piece] + a[lo:lo + piece] * entering)
        entering = out[-1][piece - 1:piece, :]
    return jnp.concatenate(out, axis=0)


def _scan_reverse(a, d, carry):
    rows = a.shape[0]
    piece = min(SCAN_ROWS, rows)
    row = lax.broadcasted_iota(jnp.int32, a.shape, 0)
    pos = row % piece
    last = row == rows - 1
    c = jnp.where(last, 0.0, pltpu.roll(a, rows - 1, 0))
    d = d + jnp.where(last, carry, 0.0)
    k = 1
    while k < piece:
        seen = pos < piece - k
        d = jnp.where(seen, d + c * pltpu.roll(d, rows - k, 0), d)
        c = jnp.where(seen, c * pltpu.roll(c, rows - k, 0), c)
        k *= 2
    out, following = [], jnp.zeros_like(carry)
    for lo in reversed(range(0, rows, piece)):
        out.insert(0, d[lo:lo + piece] + c[lo:lo + piece] * following)
        following = out[0][0:1, :]
    return jnp.concatenate(out, axis=0)


def _lane_pick(row, lane_index):
    lane = lax.broadcasted_iota(jnp.int32, row.shape, 1)
    return jnp.sum(jnp.where(lane == lane_index, row, 0.0), axis=-1, keepdims=True)


def _dn_prep(qc, kc, vc, ba, a_log_row, dt_row, head):
    q = _silu(qc)
    k = _silu(kc)
    v = _silu(vc)
    q = q * lax.rsqrt(jnp.sum(q * q, axis=-1, keepdims=True) + EPS) * (HEAD_DIM ** -0.5)
    k = k * lax.rsqrt(jnp.sum(k * k, axis=-1, keepdims=True) + EPS)
    beta = _sigmoid(_lane_pick(ba, head))
    g = -jnp.exp(_lane_pick(a_log_row, head)) * _softplus(_lane_pick(ba, HEADS + head) + _lane_pick(dt_row, head))
    return q, k, v, g, beta


def _dn_chunks_head(q, k, v, gcol, bcol, inverse=None):
    n, c, d = q.shape
    row = lax.broadcasted_iota(jnp.int32, (n, c, c), 1)
    col = lax.broadcasted_iota(jnp.int32, (n, c, c), 2)
    g_wide = jnp.broadcast_to(gcol, (n, c, d))
    b_wide = jnp.broadcast_to(bcol, (n, c, d))
    gc = _chunk_cumsum(g_wide)
    gc_rows = gc[:, :, :c]
    decay = jnp.exp(jnp.where(row >= col, gc_rows - jnp.swapaxes(gc_rows, 1, 2), -1e30))
    kb = k * b_wide
    eg = jnp.exp(gc)
    kbk, qk = _pair_nt(kb, q, k)
    a = jnp.where(row > col, kbk * decay, 0.0)
    tinv = _unit_lower_inverse(a) if inverse is None else _known_inverse(a, inverse)
    u, w = _wide_nn(tinv, v * b_wide, kb * eg)
    g_last = jnp.sum(g_wide, axis=1, keepdims=True)
    return u, w, qk * decay, q * eg, k * jnp.exp(g_last - gc), jnp.exp(g_last), tinv


def _dn_chunks(inverse, q, k, v, gcol, bcol, states):
    u, w, attn, qe, kdec, eglast, _ = _dn_chunks_head(q, k, v, gcol, bcol, inverse)
    w_st, qe_st = _pair_nn(w, qe, states)
    v_new = u - w_st
    o = qe_st + _BNN(attn, v_new)
    return (o, states * eglast + _BTN(kdec, v_new)), (w, attn, qe, kdec, eglast)


def _conv_taps(buf, head, cw, rows):
    acc = cw[0:1, :] * buf[head, pl.ds(5, rows), :]
    for j in range(1, 4):
        acc = acc + cw[j:j + 1, :] * buf[head, pl.ds(5 + j, rows), :]
    return acc


def _conv_backward(dbuf, dhead, xbuf, xhead, cw, dxc, rows):
    dbuf[dhead, pl.ds(0, rows), :] = dxc
    dx = cw[0:1, :] * dbuf[dhead, pl.ds(3, rows), :]
    for j in range(1, 4):
        dx = dx + cw[j:j + 1, :] * dbuf[dhead, pl.ds(3 - j, rows), :]
    dcw = jnp.concatenate(
        [jnp.sum(dxc * xbuf[xhead, pl.ds(5 + j, rows), :], axis=0, keepdims=True) for j in range(4)], axis=0)
    dbuf[dhead, pl.ds(rows, 8), :] = dbuf[dhead, pl.ds(0, 8), :]
    return dx, dcw


def _params(**kw):
    return pltpu.CompilerParams(vmem_limit_bytes=VMEM_LIMIT, **kw)


def _matmul(a, b, form, tm, tn, tk, name, add=None, out_dtype=F32, dep=None):
    if form == "nn":
        (m, kdim), (_, n) = a.shape, b.shape
        a_spec = pl.BlockSpec((tm, tk), lambda j, i, k: (i, k))
        b_spec = pl.BlockSpec((tk, tn), lambda j, i, k: (k, j))
        dims = NN
    elif form == "nt":
        (m, kdim), (n, _) = a.shape, b.shape
        a_spec = pl.BlockSpec((tm, tk), lambda j, i, k: (i, k))
        b_spec = pl.BlockSpec((tn, tk), lambda j, i, k: (j, k))
        dims = NT
    else:
        (kdim, m), (_, n) = a.shape, b.shape
        a_spec = pl.BlockSpec((tk, tm), lambda j, i, k: (k, i))
        b_spec = pl.BlockSpec((tk, tn), lambda j, i, k: (k, j))
        dims = TN
    assert m % tm == 0 and n % tn == 0 and kdim % tk == 0, (name, m, n, kdim, tm, tn, tk)
    ksteps = kdim // tk
    o_spec = pl.BlockSpec((tm, tn), lambda j, i, k: (i, j))
    has_add = add is not None
    extra = [] if dep is None else [dep]

    def body(*refs):
        a_ref, b_ref = refs[:2]
        c_ref = refs[2] if has_add else None
        o_ref, acc = refs[-2:]
        k = pl.program_id(2)

        @pl.when(k == 0)
        def _():
            acc[...] = c_ref[...] if has_add else jnp.zeros_like(acc)

        acc[...] += lax.dot_general(a_ref[...].astype(BF16), b_ref[...].astype(BF16), dims,
                                    preferred_element_type=F32)

        @pl.when(k == ksteps - 1)
        def _():
            o_ref[...] = acc[...].astype(o_ref.dtype)

    in_specs = [a_spec, b_spec] + ([o_spec] if has_add else []) + [pl.BlockSpec((8, HEAD_DIM), lambda j, i, k: (0, 0))
                                                                   for _ in extra]
    args = (a, b) + ((add,) if has_add else ()) + tuple(extra)
    return pl.pallas_call(
        body, name=name, grid=(n // tn, m // tm, ksteps), in_specs=in_specs, out_specs=o_spec,
        out_shape=jax.ShapeDtypeStruct((m, n), out_dtype), scratch_shapes=[pltpu.VMEM((tm, tn), F32)],
        compiler_params=_params(dimension_semantics=("parallel", "parallel", "arbitrary")),
    )(*args)


def _matmul_nt_parts(parts, narrow, w, tm, tn, name, dep=None):
    m, c = parts[0].shape
    c2 = narrow.shape[1]
    n = w.shape[0]
    count = len(parts)
    assert m % tm == 0 and n % tn == 0 and all(p.shape == (m, c) for p in parts) and (count * c) % c2 == 0, (name, m, n)
    extra = [] if dep is None else [dep]

    def body(*refs):
        part_refs, narrow_ref, w_ref, w2_ref = refs[:count], refs[count], refs[count + 1], refs[count + 2]
        o_ref, acc = refs[-2:]
        k = pl.program_id(2)

        @pl.when(k == 0)
        def _():
            acc[...] = jnp.zeros_like(acc)

        for p in range(count):
            @pl.when(k == p)
            def _(p=p):
                acc[...] += lax.dot_general(part_refs[p][...].astype(BF16), w_ref[...].astype(BF16), NT,
                                            preferred_element_type=F32)

        @pl.when(k == count)
        def _():
            o_ref[...] = acc[...] + lax.dot_general(narrow_ref[...].astype(BF16), w2_ref[...].astype(BF16), NT,
                                                    preferred_element_type=F32)

    in_specs = ([pl.BlockSpec((tm, c), lambda j, i, k: (i, 0))] * count
                + [pl.BlockSpec((tm, c2), lambda j, i, k: (i, 0)),
                   pl.BlockSpec((tn, c), lambda j, i, k: (j, jnp.minimum(k, count - 1))),
                   pl.BlockSpec((tn, c2), lambda j, i, k: (j, count * c // c2))]
                + [pl.BlockSpec((8, HEAD_DIM), lambda j, i, k: (0, 0)) for _ in extra])
    return pl.pallas_call(
        body, name=name, grid=(n // tn, m // tm, count + 1), in_specs=in_specs,
        out_specs=pl.BlockSpec((tm, tn), lambda j, i, k: (i, j)),
        out_shape=jax.ShapeDtypeStruct((m, n), F32), scratch_shapes=[pltpu.VMEM((tm, tn), F32)],
        compiler_params=_params(dimension_semantics=("parallel", "parallel", "arbitrary")),
    )(*parts, narrow, w, w, *extra)


def _matmul_tn_parts(a, parts, tm, tn, tk, name, out_dtype):
    kdim, m = a.shape
    c = parts[0].shape[1]
    count = len(parts)
    per = c // tn
    assert m % tm == 0 and c % tn == 0 and kdim % tk == 0 and all(p.shape == (kdim, c) for p in parts), (name, m, c)
    ksteps = kdim // tk

    def body(*refs):
        a_ref, part_refs = refs[0], refs[1:1 + count]
        o_ref, acc = refs[-2:]
        j, k = pl.program_id(0), pl.program_id(2)

        @pl.when(k == 0)
        def _():
            acc[...] = jnp.zeros_like(acc)

        for p in range(count):
            @pl.when(j // per == p)
            def _(p=p):
                acc[...] += lax.dot_general(a_ref[...].astype(BF16), part_refs[p][...].astype(BF16), TN,
                                            preferred_element_type=F32)

        @pl.when(k == ksteps - 1)
        def _():
            o_ref[...] = acc[...].astype(o_ref.dtype)

    def part_spec(p):
        return pl.BlockSpec((tk, tn), lambda j, i, k: (jnp.where(j // per == p, k, 0), jnp.where(j // per == p, j % per, 0)))

    return pl.pallas_call(
        body, name=name, grid=(count * per, m // tm, ksteps),
        in_specs=[pl.BlockSpec((tk, tm), lambda j, i, k: (k, i))] + [part_spec(p) for p in range(count)],
        out_specs=pl.BlockSpec((tm, tn), lambda j, i, k: (i, j)),
        out_shape=jax.ShapeDtypeStruct((m, count * c), out_dtype), scratch_shapes=[pltpu.VMEM((tm, tn), F32)],
        compiler_params=_params(dimension_semantics=("parallel", "parallel", "arbitrary")),
    )(a, *parts)


def _rmsnorm_fwd(x, w_row, name):
    s = x.shape[0]
    tb = min(TIME_BLOCK, s)

    def body(x_ref, w_ref, o_ref):
        o_ref[...] = _rmsnorm(x_ref[...], w_ref[...]).astype(BF16)

    return pl.pallas_call(
        body, name=name, grid=(s // tb,),
        in_specs=[pl.BlockSpec((tb, D_MODEL), lambda i: (i, 0)), pl.BlockSpec((1, D_MODEL), lambda i: (0, 0))],
        out_specs=pl.BlockSpec((tb, D_MODEL), lambda i: (i, 0)),
        out_shape=jax.ShapeDtypeStruct((s, D_MODEL), BF16), compiler_params=_params(),
    )(x, w_row)


def _rmsnorm_bwd(x, w_row, dh, dres, name):
    s = x.shape[0]
    tb = min(TIME_BLOCK, s)

    def body(x_ref, w_ref, dh_ref, dres_ref, dx_ref, dw_ref):
        _, vjp = jax.vjp(_rmsnorm, x_ref[...], w_ref[...])
        dx, dw = vjp(dh_ref[...])
        dx_ref[...] = dres_ref[...] + dx

        @pl.when(pl.program_id(0) == 0)
        def _():
            dw_ref[...] = jnp.zeros_like(dw_ref)

        dw_ref[...] += dw

    row = pl.BlockSpec((tb, D_MODEL), lambda i: (i, 0))
    vec = pl.BlockSpec((1, D_MODEL), lambda i: (0, 0))
    return pl.pallas_call(
        body, name=name, grid=(s // tb,), in_specs=[row, vec, row, row], out_specs=[row, vec],
        out_shape=[jax.ShapeDtypeStruct((s, D_MODEL), F32), jax.ShapeDtypeStruct((1, D_MODEL), F32)],
        compiler_params=_params(),
    )(x, w_row, dh, dres)


def _final_loss(x, w_row, target, name):
    s = x.shape[0]
    tb = min(TIME_BLOCK, s)

    def loss_fn(xv, wv, tv):
        err = _rmsnorm(xv, wv) - tv
        return 0.5 * jnp.sum(jnp.sum(err * err, axis=-1, keepdims=True), axis=0, keepdims=True) * (1.0 / D_MODEL)

    def body(x_ref, w_ref, t_ref, loss_ref, dx_ref, dw_ref):
        tv = t_ref[...]
        loss, vjp = jax.vjp(lambda xv, wv: loss_fn(xv, wv, tv), x_ref[...], w_ref[...])
        dx, dw = vjp(jnp.ones((1, 1), F32))
        dx_ref[...] = dx

        @pl.when(pl.program_id(0) == 0)
        def _():
            dw_ref[...] = jnp.zeros_like(dw_ref)
            loss_ref[...] = jnp.zeros_like(loss_ref)

        dw_ref[...] += dw
        loss_ref[...] += jnp.broadcast_to(loss, loss_ref.shape)

    row = pl.BlockSpec((tb, D_MODEL), lambda i: (i, 0))
    vec = pl.BlockSpec((1, D_MODEL), lambda i: (0, 0))
    return pl.pallas_call(
        body, name=name, grid=(s // tb,), in_specs=[row, vec, row],
        out_specs=[pl.BlockSpec((1, HEAD_DIM), lambda i: (0, 0)), row, vec],
        out_shape=[jax.ShapeDtypeStruct((1, HEAD_DIM), F32), jax.ShapeDtypeStruct((s, D_MODEL), F32),
                   jax.ShapeDtypeStruct((1, D_MODEL), F32)],
        compiler_params=_params(),
    )(x, w_row, target)


def _head_specs(tb, time_of):
    def col(off):
        return pl.BlockSpec((tb, HEAD_DIM), lambda t, h: (time_of(t), off + h))
    return col


def _vec_spec():
    return pl.BlockSpec((1, HEAD_DIM), lambda t, h: (0, h))


def _lru_fwd(proj, conv_w, conv_b, wa, ba, wx, bx, lam, nw, name):
    s = proj.shape[0]
    tb = min(TIME_BLOCK, s)
    nt = s // tb
    col = _head_specs(tb, lambda t: t)

    def body(x_ref, z_ref, cw_ref, cb_ref, wa_ref, ba_ref, wx_ref, bx_ref, lam_ref, nw_ref,
             y_ref, hs_ref, xbuf, hcar):
        t, h = pl.program_id(0), pl.program_id(1)

        @pl.when(t == 0)
        def _():
            xbuf[h, pl.ds(0, 8), :] = jnp.zeros((8, HEAD_DIM), F32)
            hcar[h] = jnp.zeros((8, HEAD_DIM), F32)

        xbuf[h, pl.ds(8, tb), :] = x_ref[...]
        xc = _conv_taps(xbuf, h, cw_ref[...], tb) + cb_ref[...]
        a, b = _lru_gates(xc, wa_ref[...], ba_ref[...], wx_ref[...], bx_ref[...], lam_ref[...])
        hs_ref[...] = _scan_forward(a, b, hcar[h, pl.ds(0, 1), :])
        hcar[h, pl.ds(0, 1), :] = hs_ref[pl.ds(tb - 1, 1), :]
        xbuf[h, pl.ds(0, 8), :] = xbuf[h, pl.ds(tb, 8), :]
        y_ref[...] = _gated_norm(hs_ref[...], z_ref[...], nw_ref[...]).astype(BF16)

    vec = _vec_spec()
    return pl.pallas_call(
        body, name=name, grid=(nt, HEADS),
        in_specs=[col(COL_LRU_X), col(COL_LRU_Z), pl.BlockSpec((4, HEAD_DIM), lambda t, h: (0, h)), vec,
                  pl.BlockSpec((None, HEAD_DIM, HEAD_DIM), lambda t, h: (h, 0, 0)), vec,
                  pl.BlockSpec((None, HEAD_DIM, HEAD_DIM), lambda t, h: (h, 0, 0)), vec, vec, vec],
        out_specs=[col(0), col(0)],
        out_shape=[jax.ShapeDtypeStruct((s, 2 * D_MODEL), BF16), jax.ShapeDtypeStruct((s, D_MODEL), F32)],
        scratch_shapes=[pltpu.VMEM((HEADS, tb + 8, HEAD_DIM), F32), pltpu.VMEM((HEADS, 8, HEAD_DIM), F32)],
        compiler_params=_params(dimension_semantics=("arbitrary", "arbitrary")),
    )(proj, proj, conv_w, conv_b, wa, ba, wx, bx, lam, nw)


def _halo_spec(tb, nt, off):
    per = tb // 8
    return pl.BlockSpec((8, HEAD_DIM), lambda t, h: (jnp.maximum((nt - 1 - t) * per - 1, 0), off + h))


def _lru_bwd(proj, hs, dy, conv_w, conv_b, wa, ba, wx, bx, lam, nw, name):
    s = proj.shape[0]
    tb = min(TIME_BLOCK, s)
    nt = s // tb
    col = _head_specs(tb, lambda t: nt - 1 - t)

    def body(x_ref, xh_ref, z_ref, hs_ref, hh_ref, dy_ref, cw_ref, cb_ref, wa_ref, ba_ref, wx_ref, bx_ref,
             lam_ref, nw_ref, dx_ref, dz_ref, dcw_ref, dcb_ref, dwa_ref, dba_ref, dwx_ref, dbx_ref, dlam_ref,
             dnw_ref, xbuf, hbuf, dbuf, gcar):
        t, h = pl.program_id(0), pl.program_id(1)
        first_block = t == nt - 1

        @pl.when(t == 0)
        def _():
            dbuf[h, pl.ds(tb, 8), :] = jnp.zeros((8, HEAD_DIM), F32)
            gcar[h] = jnp.zeros((8, HEAD_DIM), F32)
            dcw_ref[h] = jnp.zeros((4, HEAD_DIM), F32)
            dwa_ref[h] = jnp.zeros((HEAD_DIM, HEAD_DIM), F32)
            dwx_ref[h] = jnp.zeros((HEAD_DIM, HEAD_DIM), F32)
            for ref in (dcb_ref, dba_ref, dbx_ref, dlam_ref, dnw_ref):
                ref[h] = jnp.zeros((1, HEAD_DIM), F32)

        keep = jnp.where(first_block, 0.0, 1.0)
        xbuf[0, pl.ds(0, 8), :] = xh_ref[...] * keep
        xbuf[0, pl.ds(8, tb), :] = x_ref[...]
        hbuf[pl.ds(0, 8), :] = hh_ref[...] * keep
        hbuf[pl.ds(8, tb), :] = hs_ref[...]
        cw = cw_ref[...]
        xc = _conv_taps(xbuf, 0, cw, tb) + cb_ref[...]
        (a, _), gates_vjp = jax.vjp(_lru_gates, xc, wa_ref[...], ba_ref[...], wx_ref[...], bx_ref[...], lam_ref[...])
        _, norm_vjp = jax.vjp(_gated_norm, hs_ref[...], z_ref[...], nw_ref[...])
        dh, dz, dnw = norm_vjp(dy_ref[...])
        dz_ref[...] = dz.astype(dz_ref.dtype)
        g = _scan_reverse(a, dh, gcar[h, pl.ds(0, 1), :])
        gcar[h, pl.ds(0, 1), :] = a[0:1, :] * g[0:1, :]
        dxc, dwa, dba, dwx, dbx, dlam = gates_vjp((g * hbuf[pl.ds(7, tb), :], g))
        dx, dcw = _conv_backward(dbuf, h, xbuf, 0, cw, dxc, tb)
        dx_ref[...] = dx.astype(dx_ref.dtype)
        dcw_ref[h] += dcw
        dcb_ref[h] += jnp.sum(dxc, axis=0, keepdims=True)
        dwa_ref[h] += dwa
        dwx_ref[h] += dwx
        dba_ref[h] += dba
        dbx_ref[h] += dbx
        dlam_ref[h] += dlam
        dnw_ref[h] += dnw

    vec = _vec_spec()
    mat = pl.BlockSpec((None, HEAD_DIM, HEAD_DIM), lambda t, h: (h, 0, 0))

    def whole(shape):
        return pl.BlockSpec(shape, lambda t, h: (0,) * len(shape))

    head_vec = jax.ShapeDtypeStruct((HEADS, 1, HEAD_DIM), F32)
    head_mat = jax.ShapeDtypeStruct((HEADS, HEAD_DIM, HEAD_DIM), F32)
    return pl.pallas_call(
        body, name=name, grid=(nt, HEADS),
        in_specs=[col(COL_LRU_X), _halo_spec(tb, nt, COL_LRU_X), col(COL_LRU_Z), col(0), _halo_spec(tb, nt, 0), col(0),
                  pl.BlockSpec((4, HEAD_DIM), lambda t, h: (0, h)), vec, mat, vec, mat, vec, vec, vec],
        out_specs=[col(0), col(0), whole((HEADS, 4, HEAD_DIM)), whole((HEADS, 1, HEAD_DIM)),
                   whole((HEADS, HEAD_DIM, HEAD_DIM)), whole((HEADS, 1, HEAD_DIM)),
                   whole((HEADS, HEAD_DIM, HEAD_DIM)), whole((HEADS, 1, HEAD_DIM)), whole((HEADS, 1, HEAD_DIM)),
                   whole((HEADS, 1, HEAD_DIM))],
        out_shape=[jax.ShapeDtypeStruct((s, D_MODEL), BF16), jax.ShapeDtypeStruct((s, D_MODEL), BF16),
                   jax.ShapeDtypeStruct((HEADS, 4, HEAD_DIM), F32), head_vec, head_mat, head_vec, head_mat, head_vec,
                   head_vec, head_vec],
        scratch_shapes=[pltpu.VMEM((1, tb + 8, HEAD_DIM), F32), pltpu.VMEM((tb + 8, HEAD_DIM), F32),
                        pltpu.VMEM((HEADS, tb + 8, HEAD_DIM), F32), pltpu.VMEM((HEADS, 8, HEAD_DIM), F32)],
        compiler_params=_params(dimension_semantics=("arbitrary", "arbitrary")),
    )(proj, proj, proj, hs, hs, dy, conv_w, conv_b, wa, ba, wx, bx, lam, nw)


def _group_col(tb, time_of):
    def col(off):
        return pl.BlockSpec((tb, DN_GROUP * HEAD_DIM), lambda t, hg: (time_of(t), off // DN_GROUP + hg))
    return col


def _dn_fwd(proj, y, conv_w, a_log_row, dt_row, nw, name):
    s = proj.shape[0]
    tb = min(DN_TIME_BLOCK, s)
    nt = s // tb
    nchunk = tb // CHUNK
    grp = DN_GROUP
    col = _group_col(tb, lambda t: t)

    def body(q_ref, k_ref, v_ref, z_ref, ba_ref, cwq_ref, cwk_ref, cwv_ref, al_ref, dt_ref, nw_ref, y_in_ref,
             y_ref, o_ref, st_ref, inv_ref, xbuf, state):
        t, hg = pl.program_id(0), pl.program_id(1)

        def chunks(a):
            return a.reshape(nchunk, CHUNK, a.shape[-1])

        prepared = []
        for gi in range(grp):
            h = hg * grp + gi
            lanes = slice(gi * HEAD_DIM, (gi + 1) * HEAD_DIM)

            @pl.when(t == 0)
            def _(h=h):
                for i in range(3):
                    xbuf[3 * h + i, pl.ds(0, 8), :] = jnp.zeros((8, HEAD_DIM), F32)
                state[h] = jnp.zeros((HEAD_DIM, HEAD_DIM), F32)

            conv = []
            for i, (ref, cw_ref) in enumerate(((q_ref, cwq_ref), (k_ref, cwk_ref), (v_ref, cwv_ref))):
                xbuf[3 * h + i, pl.ds(8, tb), :] = ref[:, lanes]
                conv.append(_conv_taps(xbuf, 3 * h + i, cw_ref[:, lanes], tb))
                xbuf[3 * h + i, pl.ds(0, 8), :] = xbuf[3 * h + i, pl.ds(tb, 8), :]
            prepared.append([chunks(a) for a in
                             _dn_prep(conv[0], conv[1], conv[2], ba_ref[...], al_ref[...], dt_ref[...], h)])
        qs, ks, vs, gs, bs = [jnp.concatenate([p[i] for p in prepared], axis=0) for i in range(5)]
        u, w, attn, qe, kdec, eglast, tinv = _dn_chunks_head(qs, ks, vs, gs, bs)
        inv_ref[...] = tinv.reshape(grp, nchunk, CHUNK, CHUNK)
        w_u = jnp.concatenate([w, u], axis=2)
        kdec_w_u = _dot(kdec, w_u, B_TN, "bf16")
        attn_w_u = _dot(attn, w_u, B_NN, "bf16")
        st = [state[hg * grp + gi] for gi in range(grp)]
        for c in range(nchunk):
            for gi in range(grp):
                n = gi * nchunk + c
                st_ref[gi, c] = st[gi]
                st[gi] = st[gi] * eglast[n] - _NN_B(kdec_w_u[n, :, :HEAD_DIM], st[gi]) + kdec_w_u[n, :, HEAD_DIM:]
        for gi in range(grp):
            state[hg * grp + gi] = st[gi]
        states = st_ref[...].reshape(grp * nchunk, HEAD_DIM, HEAD_DIM)
        o = _dot(qe - attn_w_u[:, :, :HEAD_DIM], states, B_NN, "bf16") + attn_w_u[:, :, HEAD_DIM:]
        for gi in range(grp):
            lanes = slice(gi * HEAD_DIM, (gi + 1) * HEAD_DIM)
            o_head = o[gi * nchunk:(gi + 1) * nchunk].reshape(tb, HEAD_DIM)
            o_ref[:, lanes] = o_head
            y_ref[:, lanes] = _gated_norm(o_head, z_ref[:, lanes], nw_ref[...]).astype(BF16)

    def cw_spec(off):
        return pl.BlockSpec((4, grp * HEAD_DIM), lambda t, hg: (0, off // grp + hg))

    row128 = pl.BlockSpec((1, HEAD_DIM), lambda t, hg: (0, 0))
    return pl.pallas_call(
        body, name=name, grid=(nt, HEADS // grp),
        in_specs=[col(COL_Q), col(COL_K), col(COL_V), col(COL_DN_Z),
                  pl.BlockSpec((tb, HEAD_DIM), lambda t, hg: (t, COL_BA)),
                  cw_spec(0), cw_spec(HEADS), cw_spec(2 * HEADS), row128, row128, row128,
                  pl.BlockSpec(memory_space=pl.ANY)],
        out_specs=[col(HEADS), col(0),
                   pl.BlockSpec((grp, nchunk, HEAD_DIM, HEAD_DIM), lambda t, hg: (hg, t, 0, 0)),
                   pl.BlockSpec((grp, nchunk, CHUNK, CHUNK), lambda t, hg: (hg, t, 0, 0))],
        out_shape=[jax.ShapeDtypeStruct((s, 2 * D_MODEL), BF16), jax.ShapeDtypeStruct((s, D_MODEL), F32),
                   jax.ShapeDtypeStruct((HEADS, s // CHUNK, HEAD_DIM, HEAD_DIM), F32),
                   jax.ShapeDtypeStruct((HEADS, s // CHUNK, CHUNK, CHUNK), F32)],
        input_output_aliases={11: 0},
        scratch_shapes=[pltpu.VMEM((3 * HEADS, tb + 8, HEAD_DIM), F32), pltpu.VMEM((HEADS, HEAD_DIM, HEAD_DIM), F32)],
        compiler_params=_params(dimension_semantics=("arbitrary", "arbitrary")),
    )(proj, proj, proj, proj, proj, conv_w, conv_w, conv_w, a_log_row, dt_row, nw, y)


def _dn_bwd(proj, o, states, inverses, dy, conv_w, a_log_row, dt_row, nw, name):
    s = proj.shape[0]
    tb = min(DN_TIME_BLOCK, s)
    nt = s // tb
    nchunk = tb // CHUNK
    grp = DN_GROUP
    col = _group_col(tb, lambda t: nt - 1 - t)

    def body(q_ref, qh_ref, k_ref, kh_ref, v_ref, vh_ref, z_ref, ba_ref, o_ref, st_ref, inv_ref, dy_ref,
             cwq_ref, cwk_ref, cwv_ref, al_ref, dt_ref, nw_ref,
             dq_ref, dk_ref, dv_ref, dz_ref, dba_ref, dcw_ref, dal_ref, ddt_ref, dnw_ref,
             xbuf, dbuf, dstate, dst_s):
        t, hg = pl.program_id(0), pl.program_id(1)
        keep = jnp.where(t == nt - 1, 0.0, 1.0)

        @pl.when((t == 0) & (hg == 0))
        def _():
            for ref in (dal_ref, ddt_ref, dnw_ref):
                ref[...] = jnp.zeros_like(ref)

        def chunks(a):
            return a.reshape(nchunk, CHUNK, a.shape[-1])

        prepared, prep_vjps, dos = [], [], []
        for gi in range(grp):
            h = hg * grp + gi
            lanes = slice(gi * HEAD_DIM, (gi + 1) * HEAD_DIM)

            @pl.when(t == 0)
            def _(h=h):
                for i in range(3):
                    dbuf[3 * h + i, pl.ds(tb, 8), :] = jnp.zeros((8, HEAD_DIM), F32)
                    dcw_ref[3 * h + i] = jnp.zeros((4, HEAD_DIM), F32)
                dstate[h] = jnp.zeros((HEAD_DIM, HEAD_DIM), F32)

            conv = []
            for i, (ref, halo, cw_ref) in enumerate(((q_ref, qh_ref, cwq_ref), (k_ref, kh_ref, cwk_ref),
                                                     (v_ref, vh_ref, cwv_ref))):
                xbuf[3 * gi + i, pl.ds(0, 8), :] = halo[:, lanes] * keep
                xbuf[3 * gi + i, pl.ds(8, tb), :] = ref[:, lanes]
                conv.append(_conv_taps(xbuf, 3 * gi + i, cw_ref[:, lanes], tb))
            outs, prep_vjp = jax.vjp(
                lambda qc, kc, vc, ba, al, dt, h=h: _dn_prep(qc, kc, vc, ba, al, dt, h),
                conv[0], conv[1], conv[2], ba_ref[...], al_ref[...], dt_ref[...])
            prepared.append([chunks(a) for a in outs])
            prep_vjps.append(prep_vjp)
            _, norm_vjp = jax.vjp(_gated_norm, o_ref[:, lanes], z_ref[:, lanes], nw_ref[...])
            do, dz, dnw = norm_vjp(dy_ref[:, lanes])
            dz_ref[:, lanes] = dz.astype(dz_ref.dtype)
            dnw_ref[...] += dnw
            dos.append(chunks(do))
        qs, ks, vs, gs, bs = [jnp.concatenate([p[i] for p in prepared], axis=0) for i in range(5)]
        do = jnp.concatenate(dos, axis=0)
        states_in = st_ref[...].reshape(grp * nchunk, HEAD_DIM, HEAD_DIM)
        kept = inv_ref[...].reshape(grp * nchunk, CHUNK, CHUNK)
        _, chunks_vjp, (w, attn, qe, kdec, eglast) = jax.vjp(
            functools.partial(_dn_chunks, kept), qs, ks, vs, gs, bs, states_in, has_aux=True)
        kdec_w = _dot(kdec, w, B_TN, "bf16")
        fixed = _dot(qe, do, B_TN, "bf16") - _dot(w, _dot(attn, do, B_TN, "bf16"), B_TN, "bf16")
        dst = [dstate[hg * grp + gi] for gi in range(grp)]
        for c in reversed(range(nchunk)):
            for gi in range(grp):
                n = gi * nchunk + c
                dst_s[n] = dst[gi]
                dst[gi] = dst[gi] * eglast[n] - _dot(kdec_w[n], dst[gi], TN, "bf16") + fixed[n]
        for gi in range(grp):
            dstate[hg * grp + gi] = dst[gi]
        cts = chunks_vjp((do, dst_s[...]))[:5]

        dba_sum = None
        for gi in range(grp):
            h = hg * grp + gi
            lanes = slice(gi * HEAD_DIM, (gi + 1) * HEAD_DIM)
            per_head = [ct[gi * nchunk:(gi + 1) * nchunk].reshape(tb, ct.shape[-1]) for ct in cts]
            dqc, dkc, dvc, dba, dal, ddt = prep_vjps[gi](tuple(per_head))
            for i, (dxc, out, cw_ref) in enumerate(((dqc, dq_ref, cwq_ref), (dkc, dk_ref, cwk_ref),
                                                    (dvc, dv_ref, cwv_ref))):
                dx, dcw = _conv_backward(dbuf, 3 * h + i, xbuf, 3 * gi + i, cw_ref[:, lanes], dxc, tb)
                out[:, lanes] = dx.astype(out.dtype)
                dcw_ref[3 * h + i] += dcw
            dal_ref[...] += dal
            ddt_ref[...] += ddt
            dba_sum = dba if dba_sum is None else dba_sum + dba

        @pl.when(hg == 0)
        def _():
            dba_ref[...] = dba_sum.astype(dba_ref.dtype)

        @pl.when(hg > 0)
        def _():
            dba_ref[...] += dba_sum.astype(dba_ref.dtype)

    def cw_spec(off):
        return pl.BlockSpec((4, grp * HEAD_DIM), lambda t, hg: (0, off // grp + hg))

    def halo(off):
        per = tb // 8
        return pl.BlockSpec((8, grp * HEAD_DIM),
                            lambda t, hg: (jnp.maximum((nt - 1 - t) * per - 1, 0), off // grp + hg))

    def whole(shape):
        return pl.BlockSpec(shape, lambda t, hg: (0,) * len(shape))

    row128 = whole((1, HEAD_DIM))
    blk = (tb, HEAD_DIM)
    act = jax.ShapeDtypeStruct((s, D_MODEL), BF16)
    row_out = jax.ShapeDtypeStruct((1, HEAD_DIM), F32)
    return pl.pallas_call(
        body, name=name, grid=(nt, HEADS // grp),
        in_specs=[col(COL_Q), halo(COL_Q), col(COL_K), halo(COL_K), col(COL_V), halo(COL_V), col(COL_DN_Z),
                  pl.BlockSpec(blk, lambda t, hg: (nt - 1 - t, COL_BA)), col(0),
                  pl.BlockSpec((grp, nchunk, HEAD_DIM, HEAD_DIM), lambda t, hg: (hg, nt - 1 - t, 0, 0)),
                  pl.BlockSpec((grp, nchunk, CHUNK, CHUNK), lambda t, hg: (hg, nt - 1 - t, 0, 0)), col(HEADS),
                  cw_spec(0), cw_spec(HEADS), cw_spec(2 * HEADS), row128, row128, row128],
        out_specs=[col(0), col(0), col(0), col(0), pl.BlockSpec(blk, lambda t, hg: (nt - 1 - t, 0)),
                   whole((3 * HEADS, 4, HEAD_DIM)), row128, row128, row128],
        out_shape=[act, act, act, act, jax.ShapeDtypeStruct((s, HEAD_DIM), F32),
                   jax.ShapeDtypeStruct((3 * HEADS, 4, HEAD_DIM), F32), row_out, row_out, row_out],
        scratch_shapes=[pltpu.VMEM((3 * grp, tb + 8, HEAD_DIM), F32), pltpu.VMEM((3 * HEADS, tb + 8, HEAD_DIM), F32),
                        pltpu.VMEM((HEADS, HEAD_DIM, HEAD_DIM), F32),
                        pltpu.VMEM((grp * nchunk, HEAD_DIM, HEAD_DIM), F32)],
        compiler_params=_params(dimension_semantics=("arbitrary", "arbitrary")),
    )(proj, proj, proj, proj, proj, proj, proj, proj, o, states, inverses, dy, conv_w, conv_w, conv_w, a_log_row, dt_row,
      nw)


def _mesh_position():
    x, y, c = lax.axis_index("x"), lax.axis_index("y"), lax.axis_index("c")
    return x, y, c, 4 * x + 2 * y + c


def _peer(k, x, y, c):
    px = 1 - x if k & 4 else x
    py = 1 - y if k & 2 else y
    pc = 1 - c if k & 1 else c
    return (px, py, pc), 4 * px + 2 * py + pc


def _exchange_copies(ins, lands, scatter, send_sems, recv_sems, receives=True):
    x, y, c, me = _mesh_position()
    sends, recvs = [], []
    for i, (src, land) in enumerate(zip(ins, lands)):
        for k in range(1, N_DEV):
            peer, peer_id = _peer(k, x, y, c)
            sem = i * (N_DEV - 1) + k - 1
            for dst, out in ((me, sends), (peer_id, recvs)) if receives else ((me, sends),):
                out.append(pltpu.make_async_remote_copy(
                    src_ref=src.at[peer_id] if scatter[i] else src, dst_ref=land.at[dst],
                    send_sem=send_sems.at[sem], recv_sem=recv_sems.at[sem],
                    device_id=peer, device_id_type=pl.DeviceIdType.MESH))
    return sends, recvs


def _landing_shape(a, scatter):
    return a.shape if scatter else (N_DEV,) + a.shape


def _direct_exchange(arrays, scatter, name):
    n = len(arrays)
    out_shapes = [jax.ShapeDtypeStruct(_landing_shape(a, sc), a.dtype) for a, sc in zip(arrays, scatter)]

    def body(*refs):
        ins, outs = refs[:n], refs[n:2 * n]
        send_sems, recv_sems, local_sems = refs[2 * n:]
        me = _mesh_position()[3]
        local = [pltpu.make_async_copy(ins[i].at[me] if scatter[i] else ins[i], outs[i].at[me], local_sems.at[i])
                 for i in range(n)]
        sends, recvs = _exchange_copies(ins, outs, scatter, send_sems, recv_sems)
        for cp in local + sends:
            cp.start()
        for cp in recvs:
            cp.wait_recv()
        for cp in sends:
            cp.wait_send()
        for cp in local:
            cp.wait()

    hbm = pl.BlockSpec(memory_space=pl.ANY)
    return pl.pallas_call(
        body, name=name, in_specs=[hbm] * n, out_specs=[hbm] * n, out_shape=out_shapes,
        scratch_shapes=[pltpu.SemaphoreType.DMA((n * (N_DEV - 1),)), pltpu.SemaphoreType.DMA((n * (N_DEV - 1),)),
                        pltpu.SemaphoreType.DMA((n,))],
    )(*arrays)


def _two_level_gather(arrays, name):
    n = len(arrays)
    per = N_DEV - 1

    def body(*refs):
        ins, outs = refs[:n], refs[n:2 * n]
        send_sems, recv_sems, local_sems = refs[2 * n:]
        x, y, c, me = _mesh_position()
        sibling = (x, y, 1 - c)
        chips = [(1 - x, y), (x, 1 - y), (1 - x, 1 - y)]

        def copy(i, k, block, to, src=None):
            slot = outs[i].at[4 * block[0] + 2 * block[1] + block[2]]
            return pltpu.make_async_remote_copy(
                src_ref=slot if src is None else src, dst_ref=slot,
                send_sem=send_sems.at[i * per + k], recv_sem=recv_sems.at[i * per + k],
                device_id=to, device_id_type=pl.DeviceIdType.MESH)

        local = [pltpu.make_async_copy(ins[i], outs[i].at[me], local_sems.at[i]) for i in range(n)]
        first = []
        for i in range(n):
            first.append(copy(i, 0, (x, y, c), sibling, src=ins[i]))
            first += [copy(i, 1 + j, (x, y, c), (*chip, c), src=ins[i]) for j, chip in enumerate(chips)]
        for cp in local + first:
            cp.start()
        passed = []
        for i in range(n):
            for j, chip in enumerate(chips):
                copy(i, 1 + j, (*chip, c), (x, y, c)).wait_recv()
                passed.append(copy(i, 4 + j, (*chip, c), sibling))
                passed[-1].start()
        for i in range(n):
            copy(i, 0, sibling, (x, y, c)).wait_recv()
            for j, chip in enumerate(chips):
                copy(i, 4 + j, (*chip, 1 - c), (x, y, c)).wait_recv()
        for cp in first + passed:
            cp.wait_send()
        for cp in local:
            cp.wait()

    hbm = pl.BlockSpec(memory_space=pl.ANY)
    return pl.pallas_call(
        body, name=name, in_specs=[hbm] * n, out_specs=[hbm] * n,
        out_shape=[jax.ShapeDtypeStruct((N_DEV,) + a.shape, a.dtype) for a in arrays],
        scratch_shapes=[pltpu.SemaphoreType.DMA((n * per,)), pltpu.SemaphoreType.DMA((n * per,)),
                        pltpu.SemaphoreType.DMA((n,))],
    )(*arrays)


_HBM = pl.BlockSpec(memory_space=pltpu.HBM)
_SEM = pl.BlockSpec(memory_space=pltpu.SEMAPHORE)
_DATAFLOW = pltpu.SideEffectType.DATAFLOW_SIDE_EFFECTING


def _exchange_start(arrays, scatter, name):
    n = len(arrays)
    srcs = [pltpu.with_memory_space_constraint(a, pltpu.HBM) for a in arrays]
    lands = [pltpu.with_memory_space_constraint(lax.empty(_landing_shape(a, sc), a.dtype), pltpu.HBM)
             for a, sc in zip(arrays, scatter)]
    nsem = n * (N_DEV - 1)

    def body(*refs):
        ins, zones = refs[:n], refs[n:2 * n]
        send_sems, recv_sems = refs[2 * n], refs[2 * n + 1]
        token = refs[-1]
        sends, _ = _exchange_copies(ins, zones, scatter, send_sems, recv_sems, receives=False)
        for cp in sends:
            cp.start()
        token[...] = jnp.zeros_like(token)

    res = pl.pallas_call(
        body, name=name,
        out_shape=(pltpu.SemaphoreType.DMA((nsem,)), pltpu.SemaphoreType.DMA((nsem,)),
                   *[pltpu.HBM(a.shape, a.dtype) for a in srcs + lands], jax.ShapeDtypeStruct((8, HEAD_DIM), F32)),
        in_specs=[_HBM] * (2 * n),
        out_specs=(_SEM, _SEM, *[_HBM] * (2 * n), pl.BlockSpec(memory_space=pltpu.VMEM)),
        input_output_aliases={i: 2 + i for i in range(2 * n)},
        compiler_params=pltpu.CompilerParams(has_side_effects=_DATAFLOW),
    )(*srcs, *lands)
    return dict(sems=res[:2], srcs=res[2:2 + n], lands=res[2 + n:2 + 2 * n], token_block=res[-1],
                token=res[-1][0, 0], scatter=scatter)


def _exchange_wait(started, after, name):
    scatter = started["scatter"]
    n = len(scatter)

    def body(*refs):
        ins, zones = refs[:n], refs[n:2 * n]
        send_sems, recv_sems = refs[2 * n], refs[2 * n + 1]
        sends, recvs = _exchange_copies(ins, zones, scatter, send_sems, recv_sems)
        for cp in sends:
            cp.wait_send()
        for cp in recvs:
            cp.wait_recv()

    thru = list(started["srcs"]) + list(started["lands"])
    res = pl.pallas_call(
        body, name=name, out_shape=[pltpu.HBM(a.shape, a.dtype) for a in thru],
        in_specs=[_HBM] * (2 * n) + [_SEM, _SEM, pl.BlockSpec(memory_space=pl.ANY)], out_specs=[_HBM] * (2 * n),
        input_output_aliases={i: i for i in range(2 * n)},
        compiler_params=pltpu.CompilerParams(has_side_effects=_DATAFLOW),
    )(*thru, *started["sems"], after)
    me = 4 * lax.axis_index("x") + 2 * lax.axis_index("y") + lax.axis_index("c")
    out = []
    for src, got, sc in zip(res[:n], res[n:], scatter):
        own = lax.dynamic_index_in_dim(src, me, 0, keepdims=False) if sc else src
        out.append(lax.dynamic_update_index_in_dim(got, own, me, 0))
    return out


def _adamw(parts, w, m, v, name, rows_per_step, row_offset=0, into=None):
    rows, cols = parts.shape[1:]
    tr = min(rows_per_step, rows)
    assert rows % tr == 0 and row_offset % tr == 0, (name, rows, tr, row_offset)
    first = row_offset // tr
    c1 = 1.0 / (1.0 - ADAM_B1 ** ADAM_STEP)
    c2 = 1.0 / (1.0 - ADAM_B2 ** ADAM_STEP)

    def body(p_ref, w_ref, m_ref, v_ref, *rest):
        g_ref, d_ref, nm_ref, nv_ref = rest[-4:]
        g = p_ref[0].astype(F32)
        for d in range(1, N_DEV):
            g = g + p_ref[d].astype(F32)
        nm = ADAM_B1 * m_ref[...] + (1.0 - ADAM_B1) * g
        nv = ADAM_B2 * v_ref[...] + (1.0 - ADAM_B2) * (g * g)
        g_ref[...] = g
        nm_ref[...] = nm
        nv_ref[...] = nv
        d_ref[...] = -ADAM_LR * ((nm * c1) / (jnp.sqrt(nv * c2) + ADAM_EPS) + ADAM_WD * w_ref[...])

    blk = pl.BlockSpec((tr, cols), lambda i: (i + first, 0))
    shape = jax.ShapeDtypeStruct(w.shape, F32)
    prior = [] if into is None else list(into)
    return pl.pallas_call(
        body, name=name, grid=(rows // tr,),
        in_specs=[pl.BlockSpec((N_DEV, tr, cols), lambda i: (0, i, 0)), blk, blk, blk]
        + [pl.BlockSpec(memory_space=pl.ANY)] * len(prior),
        out_specs=[blk] * 4, out_shape=[shape] * 4,
        input_output_aliases={4 + j: j for j in range(len(prior))}, compiler_params=_params(),
    )(parts, w, m, v, *prior)


_LAYERED = ("norm_w", "lru_conv_b", "lru_wa", "lru_ba", "lru_wx", "lru_bx", "lru_lambda", "lru_norm_w",
            "dn_A_log", "dn_dt_bias", "dn_norm_w")
_PACK_LRU = _LAYERED[1:8]
_PACK_LAST = _LAYERED[:1] + _LAYERED[8:]
_WEIGHTS = ("norm_w", "w_in", "lru_conv_w", "lru_conv_b", "lru_wa", "lru_ba", "lru_wx", "lru_bx", "lru_lambda",
            "lru_norm_w", "dn_conv_w", "dn_A_log", "dn_dt_bias", "dn_norm_w", "w_out", "final_norm_w")


def _pack_layer(tree, layer, tail=(), names=_LAYERED):
    rows = []
    for name in names:
        a = tree[name][layer]
        if a.shape[-1] == HEADS:
            a = jnp.pad(a, (0, HEAD_DIM - HEADS))
        rows.append(a.reshape(-1, HEAD_DIM))
    rows += [t.reshape(-1, HEAD_DIM) for t in tail]
    packed = jnp.concatenate(rows, axis=0)
    return jnp.pad(packed, ((0, (-packed.shape[0]) % 8), (0, 0)))


def _unpack_layer(packed, like, names=_LAYERED):
    out, at = {}, 0
    for name in names:
        shape = like[name].shape[1:]
        if shape[-1] == HEADS:
            n = 1
            out[name] = packed[at, :HEADS]
        else:
            n = like[name][0].size // HEAD_DIM
            out[name] = packed[at:at + n].reshape(shape)
        at += n
    return out, at


def _heads_to_channels(a):
    return jnp.transpose(a, (1, 0, 2)).reshape(a.shape[1], HEADS * HEAD_DIM)


def kernel(x, norm_w, w_in, lru_conv_w, lru_conv_b, lru_wa, lru_ba, lru_wx, lru_bx, lru_lambda, lru_norm_w, dn_conv_w, dn_A_log, dn_dt_bias, dn_norm_w, w_out, final_norm_w, loss_target, m_norm_w, m_w_in, m_lru_conv_w, m_lru_conv_b, m_lru_wa, m_lru_ba, m_lru_wx, m_lru_bx, m_lru_lambda, m_lru_norm_w, m_dn_conv_w, m_dn_A_log, m_dn_dt_bias, m_dn_norm_w, m_w_out, m_final_norm_w, v_norm_w, v_w_in, v_lru_conv_w, v_lru_conv_b, v_lru_wa, v_lru_ba, v_lru_wx, v_lru_bx, v_lru_lambda, v_lru_norm_w, v_dn_conv_w, v_dn_A_log, v_dn_dt_bias, v_dn_norm_w, v_w_out, v_final_norm_w):
    weights = dict(norm_w=norm_w, w_in=w_in, lru_conv_w=lru_conv_w, lru_conv_b=lru_conv_b, lru_wa=lru_wa,
                   lru_ba=lru_ba, lru_wx=lru_wx, lru_bx=lru_bx, lru_lambda=lru_lambda, lru_norm_w=lru_norm_w,
                   dn_conv_w=dn_conv_w, dn_A_log=dn_A_log, dn_dt_bias=dn_dt_bias, dn_norm_w=dn_norm_w,
                   w_out=w_out, final_norm_w=final_norm_w)
    mom_m = dict(norm_w=m_norm_w, w_in=m_w_in, lru_conv_w=m_lru_conv_w, lru_conv_b=m_lru_conv_b, lru_wa=m_lru_wa,
                 lru_ba=m_lru_ba, lru_wx=m_lru_wx, lru_bx=m_lru_bx, lru_lambda=m_lru_lambda,
                 lru_norm_w=m_lru_norm_w, dn_conv_w=m_dn_conv_w, dn_A_log=m_dn_A_log, dn_dt_bias=m_dn_dt_bias,
                 dn_norm_w=m_dn_norm_w, w_out=m_w_out, final_norm_w=m_final_norm_w)
    mom_v = dict(norm_w=v_norm_w, w_in=v_w_in, lru_conv_w=v_lru_conv_w, lru_conv_b=v_lru_conv_b, lru_wa=v_lru_wa,
                 lru_ba=v_lru_ba, lru_wx=v_lru_wx, lru_bx=v_lru_bx, lru_lambda=v_lru_lambda,
                 lru_norm_w=v_lru_norm_w, dn_conv_w=v_dn_conv_w, dn_A_log=v_dn_A_log, dn_dt_bias=v_dn_dt_bias,
                 dn_norm_w=v_dn_norm_w, w_out=v_w_out, final_norm_w=v_final_norm_w)
    depth = norm_w.shape[0]
    xs = x[0]
    s = xs.shape[0]
    tm = min(1024, s)

    assert depth >= 2, depth

    def row(a):
        return a.reshape(1, -1)

    def pad_row(a):
        return jnp.pad(a, (0, HEAD_DIM - a.shape[0])).reshape(1, HEAD_DIM)

    def full_w_in(g):
        w = jnp.transpose(g, (1, 2, 0, 3)).reshape(g.shape[1], D_MODEL, D_IN)
        return jnp.pad(w, ((0, 0), (0, 0), (0, D_IN_PAD - D_IN)))

    g_win0, g_lcw, g_dcw = _two_level_gather([w_in[:1].astype(BF16), lru_conv_w, dn_conv_w], "gather_first")
    rest = _exchange_start([w_in[1:].astype(BF16), w_out.astype(BF16)], [False] * 2, "gather_rest_start")
    win = [full_w_in(g_win0)[0]]
    wout = None
    lcw = jnp.transpose(g_lcw, (1, 2, 0, 3)).reshape(depth, 4, D_MODEL)
    dcw = jnp.transpose(g_dcw, (1, 2, 0, 3)).reshape(depth, 4, 3 * D_MODEL)

    saved = []
    cur = xs
    for l in range(depth):
        nw_row = row(norm_w[l]) + rest["token"] if l == 0 else row(norm_w[l])
        hn = _rmsnorm_fwd(cur, nw_row, f"norm_fwd_{l}")
        proj = _matmul(hn, win[l], "nn", tm, 896, D_MODEL, f"in_proj_{l}")
        y_lru, hs = _lru_fwd(proj, lcw[l], row(lru_conv_b[l]), lru_wa[l], row(lru_ba[l]), lru_wx[l], row(lru_bx[l]),
                             row(lru_lambda[l]), row(lru_norm_w[l]), f"lru_fwd_{l}")
        ycat, o_dn, states, inverses = _dn_fwd(proj, y_lru, dcw[l], pad_row(dn_A_log[l]), pad_row(dn_dt_bias[l]),
                                     row(dn_norm_w[l]), f"dn_fwd_{l}")
        if l == 0:
            g_win_rest, g_wout = _exchange_wait(rest, ycat, "gather_rest_wait")
            win += list(full_w_in(g_win_rest))
            wout = jnp.transpose(g_wout, (1, 0, 2, 3)).reshape(depth, 2 * D_MODEL, D_MODEL)
        nxt = _matmul(ycat, wout[l], "nn", tm, D_MODEL, 2 * D_MODEL, f"out_proj_{l}", add=cur)
        saved.append((cur, hn, proj, hs, o_dn, states, inverses, ycat))
        cur = nxt
    loss_part, dx, d_final = _final_loss(cur, row(final_norm_w), loss_target[0], "final_loss")

    def win_slots(g):
        return jnp.transpose(g.reshape(D_MODEL, N_DEV, D_IN // N_DEV), (1, 0, 2))

    def wout_slots(g):
        return g.reshape(N_DEV, 2 * D_MODEL // N_DEV, D_MODEL)

    grads = {k: [None] * depth for k in _WEIGHTS if k not in ("final_norm_w", "w_in", "w_out")}
    started = {}
    token = None
    for l in reversed(range(depth)):
        x_in, hn, proj, hs, o_dn, states, inverses, ycat = saved[l]
        dy = _matmul(dx, wout[l], "nt", tm, D_MODEL, D_MODEL, f"out_proj_dy_{l}")
        g_wout_l = _matmul(ycat, dx, "tn", D_MODEL, D_MODEL, tm, f"out_proj_dw_{l}", out_dtype=BF16)
        if l == 0:
            started["w_out_0"] = _exchange_start([wout_slots(g_wout_l)], [True], "exchange_w_out_0_start")
            token = token + started["w_out_0"]["token"]
        cb_row = row(lru_conv_b[l]) if token is None else row(lru_conv_b[l]) + token
        (dlx, dlz, g_lcw, g_lcb, g_wa, g_ba, g_wx, g_bx, g_lam, g_lnw) = _lru_bwd(
            proj, hs, dy, lcw[l], cb_row, lru_wa[l], row(lru_ba[l]), lru_wx[l], row(lru_bx[l]),
            row(lru_lambda[l]), row(lru_norm_w[l]), f"lru_bwd_{l}")
        grads["lru_conv_w"][l] = _heads_to_channels(g_lcw)
        grads["lru_conv_b"][l] = g_lcb.reshape(D_MODEL)
        grads["lru_wa"][l] = g_wa
        grads["lru_ba"][l] = g_ba.reshape(D_MODEL)
        grads["lru_wx"][l] = g_wx
        grads["lru_bx"][l] = g_bx.reshape(D_MODEL)
        grads["lru_lambda"][l] = g_lam.reshape(D_MODEL)
        grads["lru_norm_w"][l] = g_lnw.reshape(D_MODEL)
        al_row = pad_row(dn_A_log[l])
        if l == 0:
            started["pack_0"] = _exchange_start([_pack_layer(grads, 0, names=_PACK_LRU)], [False],
                                                "exchange_pack_0_start")
            al_row = al_row + started["pack_0"]["token"]
        (dq, dk, dv, ddz, dba, g_dcw3, g_al, g_dt, g_dnw) = _dn_bwd(
            proj, o_dn, states, inverses, dy, dcw[l], al_row, pad_row(dn_dt_bias[l]), row(dn_norm_w[l]), f"dn_bwd_{l}")
        g_dcw3 = g_dcw3.reshape(HEADS, 3, 4, HEAD_DIM)
        grads["dn_conv_w"][l] = jnp.concatenate([_heads_to_channels(g_dcw3[:, i]) for i in range(3)], axis=1)
        grads["dn_A_log"][l] = g_al[0, :HEADS]
        grads["dn_dt_bias"][l] = g_dt[0, :HEADS]
        grads["dn_norm_w"][l] = g_dnw.reshape(HEAD_DIM)
        dep = None
        pieces = [dlx, dlz, dq, dk, dv, ddz]
        wide = len(pieces) * D_MODEL
        dba = dba.astype(BF16)
        g_win_l = jnp.concatenate(
            [_matmul_tn_parts(hn, pieces, D_MODEL, D_MODEL, tm, f"in_proj_dw_{l}", BF16),
             _matmul(hn, dba, "tn", D_MODEL, HEAD_DIM, tm, f"in_proj_dw_gates_{l}", out_dtype=BF16)[:, :D_IN - wide]],
            axis=1)
        if l == 0:
            started[0] = _exchange_start([win_slots(g_win_l)], [True], "exchange_0_start")
            dep = started[0]["token_block"]
        dh = _matmul_nt_parts(pieces, dba, win[l], tm, D_MODEL, f"in_proj_dh_{l}", dep=dep)
        dx, g_nw = _rmsnorm_bwd(x_in, row(norm_w[l]), dh, dx, f"norm_bwd_{l}")
        grads["norm_w"][l] = g_nw.reshape(D_MODEL)
        if l > 0:
            tail = (d_final, loss_part) if l == depth - 1 else ()
            started[l] = _exchange_start([win_slots(g_win_l), wout_slots(g_wout_l), _pack_layer(grads, l, tail)],
                                         [True, True, False], f"exchange_{l}_start")
            token = started[l]["token"]

    def conv_slots(a):
        dd, r, cc = a.shape
        return jnp.transpose(a.reshape(dd, r, N_DEV, cc // N_DEV), (2, 0, 1, 3))

    small = _exchange_start(
        [conv_slots(jnp.stack(grads["lru_conv_w"])), conv_slots(jnp.stack(grads["dn_conv_w"])),
         _pack_layer(grads, 0, names=_PACK_LAST)], [True, True, False], "exchange_small_start")

    new = {}
    flat_in = (depth * D_MODEL, D_IN // N_DEV)
    flat_out = (depth * 2 * D_MODEL // N_DEV, D_MODEL)
    zero_row = jnp.zeros((1, HEAD_DIM), F32)

    def adamw_pack(parts, layer, names=_LAYERED, name="adamw_small"):
        tails = [(t, zero_row) if layer == depth - 1 else () for t in (final_norm_w, m_final_norm_w, v_final_norm_w)]
        return _adamw(parts, _pack_layer(weights, layer, tails[0], names), _pack_layer(mom_m, layer, tails[1], names),
                      _pack_layer(mom_v, layer, tails[2], names), f"{name}_{layer}", parts.shape[1])

    def adamw_w_in(parts, layer, into):
        return _adamw(parts, w_in.reshape(flat_in), m_w_in.reshape(flat_in), v_w_in.reshape(flat_in),
                      f"adamw_w_in_{layer}", 256, layer * D_MODEL, into)

    def adamw_w_out(parts, layer, into):
        return _adamw(parts, w_out.reshape(flat_out), m_w_out.reshape(flat_out), v_w_out.reshape(flat_out),
                      f"adamw_w_out_{layer}", 256, layer * flat_out[0] // depth, into)

    acc_in = acc_out = None
    packs = [None] * depth
    after = small["token_block"]
    for l in reversed(range(1, depth)):
        r_win, r_wout, r_pack = _exchange_wait(started[l], after, f"exchange_{l}_wait")
        acc_in = adamw_w_in(r_win, l, acc_in)
        acc_out = adamw_w_out(r_wout, l, acc_out)
        packs[l] = adamw_pack(r_pack, l)
        after = packs[l][0]
    (r_wout,) = _exchange_wait(started["w_out_0"], after, "exchange_w_out_0_wait")
    acc_out = adamw_w_out(r_wout, 0, acc_out)
    (r_win,) = _exchange_wait(started[0], acc_out[0], "exchange_0_wait")
    acc_in = adamw_w_in(r_win, 0, acc_in)
    (r_pack,) = _exchange_wait(started["pack_0"], acc_in[0], "exchange_pack_0_wait")
    packs[0] = adamw_pack(r_pack, 0, _PACK_LRU)
    r_lcw, r_dcw, r_last = _exchange_wait(small, packs[0][0], "exchange_small_wait")
    for name, parts in (("lru_conv_w", r_lcw), ("dn_conv_w", r_dcw)):
        w = weights[name]
        flat = (-1, w.shape[-1])
        outs = _adamw(parts.reshape((N_DEV,) + (w.size // w.shape[-1], w.shape[-1])), w.reshape(flat),
                      mom_m[name].reshape(flat), mom_v[name].reshape(flat), f"adamw_{name}", 8)
        new[name] = [a.reshape(w.shape) for a in outs]
    last_0 = adamw_pack(r_last, 0, _PACK_LAST, "adamw_last")
    new["w_in"] = [a.reshape(w_in.shape) for a in acc_in]
    new["w_out"] = [a.reshape(w_out.shape) for a in acc_out]
    for i in range(4):
        layers = [{**_unpack_layer(packs[0][i], weights, _PACK_LRU)[0],
                   **_unpack_layer(last_0[i], weights, _PACK_LAST)[0]}]
        layers += [_unpack_layer(packs[l][i], weights)[0] for l in range(1, depth)]
        for name in _LAYERED:
            new.setdefault(name, []).append(jnp.stack([layer[name] for layer in layers]))
    tail_at = _unpack_layer(packs[depth - 1][0], weights)[1]
    rows_final = D_MODEL // HEAD_DIM
    new["final_norm_w"] = [packs[depth - 1][i][tail_at:tail_at + rows_final].reshape(D_MODEL) for i in range(4)]
    loss = packs[depth - 1][0][tail_at + rows_final, 0]
    out = [loss, dx.reshape(x.shape)]
    for i in range(4):
        out += [new[name][i] for name in _WEIGHTS]
    return tuple(out)
```

```python
import functools

import jax
import jax.numpy as jnp
from jax import lax
from jax.experimental import pallas as pl
from jax.experimental.pallas import tpu as pltpu

F32 = jnp.float32
BF16 = jnp.bfloat16

N_DEV = 8
D_MODEL = 1024
HEADS = 8
HEAD_DIM = 128
CHUNK = 64
D_IN = 6160
D_IN_PAD = 6272
COL_LRU_X, COL_LRU_Z, COL_Q, COL_K, COL_V, COL_DN_Z, COL_BA = 0, 8, 16, 24, 32, 40, 48
LRU_C = 8.0
EPS = 1e-6
ADAM_LR, ADAM_B1, ADAM_B2, ADAM_EPS, ADAM_WD, ADAM_STEP = 0.001, 0.9, 0.999, 1e-08, 0.01, 10
TIME_BLOCK = 1024
DN_TIME_BLOCK = 128
DN_GROUP = 8
VMEM_LIMIT = 56 * 1024 * 1024

NN = (((1,), (0,)), ((), ()))
NT = (((1,), (1,)), ((), ()))
TN = (((0,), (0,)), ((), ()))


B_NN = (((2,), (1,)), ((0,), (0,)))
B_NT = (((2,), (2,)), ((0,), (0,)))
B_TN = (((1,), (1,)), ((0,), (0,)))


def _split_bf16(x):
    hi = x.astype(BF16)
    return hi, (x - hi.astype(F32)).astype(BF16)


def _dot(a, b, dims, prec):
    if prec == "bf16":
        return lax.dot_general(a.astype(BF16), b.astype(BF16), dims, preferred_element_type=F32)
    a1, a2 = _split_bf16(a)
    b1, b2 = _split_bf16(b)
    dg = functools.partial(lax.dot_general, dimension_numbers=dims, preferred_element_type=F32)
    return dg(a1, b1) + (dg(a1, b2) + dg(a2, b1))


def _make_mm(prec, nn_dims, nt_dims, tn_dims):
    @jax.custom_vjp
    def nn(a, b):
        return _dot(a, b, nn_dims, prec)

    @jax.custom_vjp
    def nt(a, b):
        return _dot(a, b, nt_dims, prec)

    @jax.custom_vjp
    def tn(a, b):
        return _dot(a, b, tn_dims, prec)

    nn.defvjp(lambda a, b: (_dot(a, b, nn_dims, prec), (a, b)),
              lambda r, g: (_dot(g, r[1], nt_dims, prec), _dot(r[0], g, tn_dims, prec)))
    nt.defvjp(lambda a, b: (_dot(a, b, nt_dims, prec), (a, b)),
              lambda r, g: (_dot(g, r[1], nn_dims, prec), _dot(g, r[0], tn_dims, prec)))
    tn.defvjp(lambda a, b: (_dot(a, b, tn_dims, prec), (a, b)),
              lambda r, g: (_dot(r[1], g, nt_dims, prec), _dot(r[0], g, nn_dims, prec)))
    return nn, nt, tn


_NN_B, _NT_B, _TN_B = _make_mm("bf16", NN, NT, TN)
_BNN, _BNT, _BTN = _make_mm("bf16", B_NN, B_NT, B_TN)


@jax.custom_vjp
def _unit_lower_inverse(a):
    n = a.shape[-1]
    eye = (lax.broadcasted_iota(jnp.int32, a.shape, 1) == lax.broadcasted_iota(jnp.int32, a.shape, 2)).astype(F32)
    dg = functools.partial(lax.dot_general, dimension_numbers=B_NN, preferred_element_type=F32)
    inv = eye - a
    pw = _dot(a, a, B_NN, "bf16x3")
    steps = n.bit_length() - 2
    for j in range(steps):
        i1, i2 = _split_bf16(inv)
        p1, p2 = _split_bf16(pw)
        square = j + 1 < steps
        by_hi = dg(jnp.concatenate([i1, i2, p1, p2] if square else [i1, i2], axis=1), p1)
        by_lo = dg(jnp.concatenate([i1, p1], axis=1) if square else i1, p2)
        inv = inv + (by_hi[:, :n] + (by_lo[:, :n] + by_hi[:, n:2 * n]))
        if square:
            pw = by_hi[:, 2 * n:3 * n] + (by_lo[:, n:] + by_hi[:, 3 * n:])
    return inv


def _uli_fwd(a):
    inv = _unit_lower_inverse(a)
    return inv, inv


def _uli_bwd(inv, g):
    return (-_dot(_dot(inv, g, B_TN, "bf16"), inv, B_NT, "bf16"),)


_unit_lower_inverse.defvjp(_uli_fwd, _uli_bwd)


@jax.custom_vjp
def _known_inverse(a, inv):
    return inv


_known_inverse.defvjp(lambda a, inv: (inv, inv), lambda inv, g: (_uli_bwd(inv, g)[0], jnp.zeros_like(inv)))


def _rows2(y, m):
    return y[:, :m], y[:, m:]


@jax.custom_vjp
def _pair_nn(x1, x2, r):
    return _rows2(_dot(jnp.concatenate([x1, x2], axis=1), r, B_NN, "bf16"), x1.shape[1])


def _pair_nn_bwd(res, g):
    x1, x2, r = res
    g = jnp.concatenate(g, axis=1)
    dx1, dx2 = _rows2(_dot(g, r, B_NT, "bf16"), x1.shape[1])
    return dx1, dx2, _dot(jnp.concatenate([x1, x2], axis=1), g, B_TN, "bf16")


_pair_nn.defvjp(lambda x1, x2, r: (_pair_nn(x1, x2, r), (x1, x2, r)), _pair_nn_bwd)


@jax.custom_vjp
def _pair_nt(x1, x2, r):
    return _rows2(_dot(jnp.concatenate([x1, x2], axis=1), r, B_NT, "bf16"), x1.shape[1])


def _pair_nt_bwd(res, g):
    x1, x2, r = res
    g = jnp.concatenate(g, axis=1)
    dx1, dx2 = _rows2(_dot(g, r, B_NN, "bf16"), x1.shape[1])
    return dx1, dx2, _dot(g, jnp.concatenate([x1, x2], axis=1), B_TN, "bf16")


_pair_nt.defvjp(lambda x1, x2, r: (_pair_nt(x1, x2, r), (x1, x2, r)), _pair_nt_bwd)


@jax.custom_vjp
def _wide_nn(l, r1, r2):
    y = _dot(l, jnp.concatenate([r1, r2], axis=2), B_NN, "bf16")
    return y[:, :, :r1.shape[2]], y[:, :, r1.shape[2]:]


def _wide_nn_bwd(res, g):
    l, r1, r2 = res
    g = jnp.concatenate(g, axis=2)
    dr = _dot(l, g, B_TN, "bf16")
    return (_dot(g, jnp.concatenate([r1, r2], axis=2), B_NT, "bf16"), dr[:, :, :r1.shape[2]], dr[:, :, r1.shape[2]:])


_wide_nn.defvjp(lambda l, r1, r2: (_wide_nn(l, r1, r2), (l, r1, r2)), _wide_nn_bwd)


def _lower_ones(batch, n):
    shape = (batch, n, n)
    return (lax.broadcasted_iota(jnp.int32, shape, 1) >= lax.broadcasted_iota(jnp.int32, shape, 2)).astype(BF16)


@jax.custom_vjp
def _chunk_cumsum(g):
    tri = _lower_ones(g.shape[0], g.shape[1])
    g1, g2 = _split_bf16(g)
    g3 = (g - g1.astype(F32) - g2.astype(F32)).astype(BF16)
    dg = functools.partial(lax.dot_general, dimension_numbers=B_NN, preferred_element_type=F32)
    return dg(tri, g1) + (dg(tri, g2) + dg(tri, g3))


def _chunk_cumsum_bwd(_, ct):
    tri = _lower_ones(ct.shape[0], ct.shape[1])
    c1, c2 = _split_bf16(ct)
    dg = functools.partial(lax.dot_general, dimension_numbers=B_TN, preferred_element_type=F32)
    return (dg(tri, c1) + dg(tri, c2),)


_chunk_cumsum.defvjp(lambda g: (_chunk_cumsum(g), None), _chunk_cumsum_bwd)


def _expm1(x):
    small = x * (1.0 + x * (0.5 + x * (1.0 / 6 + x * (1.0 / 24 + x * (1.0 / 120 + x * (1.0 / 720))))))
    return jnp.where(jnp.abs(x) < 0.2, small, jnp.exp(x) - 1.0)


def _sigmoid(x):
    return 1.0 / (1.0 + jnp.exp(-x))


def _silu(x):
    return x * _sigmoid(x)


def _softplus(x):
    return jnp.maximum(x, 0.0) + jnp.log(1.0 + jnp.exp(-jnp.abs(x)))


def _rmsnorm(x, w):
    return x * lax.rsqrt(jnp.mean(x * x, axis=-1, keepdims=True) + EPS) * w


def _gated_norm(o, z, w):
    return o * lax.rsqrt(jnp.mean(o * o, axis=-1, keepdims=True) + EPS) * w * _silu(z)


def _lru_gates(xc, wa, ba, wx, bx, lam):
    r = _sigmoid(_NN_B(xc, wa) + ba)
    i = _sigmoid(_NN_B(xc, wx) + bx)
    log_a = -LRU_C * r * _softplus(-lam)
    a = jnp.exp(log_a)
    mult = jnp.sqrt(-_expm1(2.0 * log_a))
    return a, mult * (i * xc)


SCAN_ROWS = 32


def _scan_forward(a, b, h0):
    rows = a.shape[0]
    piece = min(SCAN_ROWS, rows)
    pos = lax.broadcasted_iota(jnp.int32, a.shape, 0) % piece
    k = 1
    while k < piece:
        seen = pos >= k
        b = jnp.where(seen, a * pltpu.roll(b, k, 0) + b, b)
        a = jnp.where(seen, a * pltpu.roll(a, k, 0), a)
        k *= 2
    out, entering = [], h0
    for lo in range(0, rows, piece):
        out.append(b[lo:lo + piece] + a[lo:lo + piece] * entering)
        entering = out[-1][piece - 1:piece, :]
    return jnp.concatenate(out, axis=0)


def _scan_reverse(a, d, carry):
    rows = a.shape[0]
    piece = min(SCAN_ROWS, rows)
    row = lax.broadcasted_iota(jnp.int32, a.shape, 0)
    pos = row % piece
    last = row == rows - 1
    c = jnp.where(last, 0.0, pltpu.roll(a, rows - 1, 0))
    d = d + jnp.where(last, carry, 0.0)
    k = 1
    while k < piece:
        seen = pos < piece - k
        d = jnp.where(seen, d + c * pltpu.roll(d, rows - k, 0), d)
        c = jnp.where(seen, c * pltpu.roll(c, rows - k, 0), c)
        k *= 2
    out, following = [], jnp.zeros_like(carry)
    for lo in reversed(range(0, rows, piece)):
        out.insert(0, d[lo:lo + piece] + c[lo:lo + piece] * following)
        following = out[0][0:1, :]
    return jnp.concatenate(out, axis=0)


def _lane_pick(row, lane_index):
    lane = lax.broadcasted_iota(jnp.int32, row.shape, 1)
    return jnp.sum(jnp.where(lane == lane_index, row, 0.0), axis=-1, keepdims=True)


def _dn_prep(qc, kc, vc, ba, a_log_row, dt_row, head):
    q = _silu(qc)
    k = _silu(kc)
    v = _silu(vc)
    q = q * lax.rsqrt(jnp.sum(q * q, axis=-1, keepdims=True) + EPS) * (HEAD_DIM ** -0.5)
    k = k * lax.rsqrt(jnp.sum(k * k, axis=-1, keepdims=True) + EPS)
    beta = _sigmoid(_lane_pick(ba, head))
    g = -jnp.exp(_lane_pick(a_log_row, head)) * _softplus(_lane_pick(ba, HEADS + head) + _lane_pick(dt_row, head))
    return q, k, v, g, beta


def _dn_chunks_head(q, k, v, gcol, bcol, inverse=None):
    n, c, d = q.shape
    row = lax.broadcasted_iota(jnp.int32, (n, c, c), 1)
    col = lax.broadcasted_iota(jnp.int32, (n, c, c), 2)
    g_wide = jnp.broadcast_to(gcol, (n, c, d))
    b_wide = jnp.broadcast_to(bcol, (n, c, d))
    gc = _chunk_cumsum(g_wide)
    gc_rows = gc[:, :, :c]
    decay = jnp.exp(jnp.where(row >= col, gc_rows - jnp.swapaxes(gc_rows, 1, 2), -1e30))
    kb = k * b_wide
    eg = jnp.exp(gc)
    kbk, qk = _pair_nt(kb, q, k)
    a = jnp.where(row > col, kbk * decay, 0.0)
    tinv = _unit_lower_inverse(a) if inverse is None else _known_inverse(a, inverse)
    u, w = _wide_nn(tinv, v * b_wide, kb * eg)
    g_last = jnp.sum(g_wide, axis=1, keepdims=True)
    return u, w, qk * decay, q * eg, k * jnp.exp(g_last - gc), jnp.exp(g_last), tinv


def _dn_chunks(inverse, q, k, v, gcol, bcol, states):
    u, w, attn, qe, kdec, eglast, _ = _dn_chunks_head(q, k, v, gcol, bcol, inverse)
    w_st, qe_st = _pair_nn(w, qe, states)
    v_new = u - w_st
    o = qe_st + _BNN(attn, v_new)
    return (o, states * eglast + _BTN(kdec, v_new)), (w, attn, qe, kdec, eglast)


def _conv_taps(buf, head, cw, rows):
    acc = cw[0:1, :] * buf[head, pl.ds(5, rows), :]
    for j in range(1, 4):
        acc = acc + cw[j:j + 1, :] * buf[head, pl.ds(5 + j, rows), :]
    return acc


def _conv_backward(dbuf, dhead, xbuf, xhead, cw, dxc, rows):
    dbuf[dhead, pl.ds(0, rows), :] = dxc
    dx = cw[0:1, :] * dbuf[dhead, pl.ds(3, rows), :]
    for j in range(1, 4):
        dx = dx + cw[j:j + 1, :] * dbuf[dhead, pl.ds(3 - j, rows), :]
    dcw = jnp.concatenate(
        [jnp.sum(dxc * xbuf[xhead, pl.ds(5 + j, rows), :], axis=0, keepdims=True) for j in range(4)], axis=0)
    dbuf[dhead, pl.ds(rows, 8), :] = dbuf[dhead, pl.ds(0, 8), :]
    return dx, dcw


def _params(**kw):
    return pltpu.CompilerParams(vmem_limit_bytes=VMEM_LIMIT, **kw)


def _matmul(a, b, form, tm, tn, tk, name, add=None, out_dtype=F32, dep=None):
    if form == "nn":
        (m, kdim), (_, n) = a.shape, b.shape
        a_spec = pl.BlockSpec((tm, tk), lambda j, i, k: (i, k))
        b_spec = pl.BlockSpec((tk, tn), lambda j, i, k: (k, j))
        dims = NN
    elif form == "nt":
        (m, kdim), (n, _) = a.shape, b.shape
        a_spec = pl.BlockSpec((tm, tk), lambda j, i, k: (i, k))
        b_spec = pl.BlockSpec((tn, tk), lambda j, i, k: (j, k))
        dims = NT
    else:
        (kdim, m), (_, n) = a.shape, b.shape
        a_spec = pl.BlockSpec((tk, tm), lambda j, i, k: (k, i))
        b_spec = pl.BlockSpec((tk, tn), lambda j, i, k: (k, j))
        dims = TN
    assert m % tm == 0 and n % tn == 0 and kdim % tk == 0, (name, m, n, kdim, tm, tn, tk)
    ksteps = kdim // tk
    o_spec = pl.BlockSpec((tm, tn), lambda j, i, k: (i, j))
    has_add = add is not None
    extra = [] if dep is None else [dep]

    def body(*refs):
        a_ref, b_ref = refs[:2]
        c_ref = refs[2] if has_add else None
        o_ref, acc = refs[-2:]
        k = pl.program_id(2)

        @pl.when(k == 0)
        def _():
            acc[...] = c_ref[...] if has_add else jnp.zeros_like(acc)

        acc[...] += lax.dot_general(a_ref[...].astype(BF16), b_ref[...].astype(BF16), dims,
                                    preferred_element_type=F32)

        @pl.when(k == ksteps - 1)
        def _():
            o_ref[...] = acc[...].astype(o_ref.dtype)

    in_specs = [a_spec, b_spec] + ([o_spec] if has_add else []) + [pl.BlockSpec((8, HEAD_DIM), lambda j, i, k: (0, 0))
                                                                   for _ in extra]
    args = (a, b) + ((add,) if has_add else ()) + tuple(extra)
    return pl.pallas_call(
        body, name=name, grid=(n // tn, m // tm, ksteps), in_specs=in_specs, out_specs=o_spec,
        out_shape=jax.ShapeDtypeStruct((m, n), out_dtype), scratch_shapes=[pltpu.VMEM((tm, tn), F32)],
        compiler_params=_params(dimension_semantics=("parallel", "parallel", "arbitrary")),
    )(*args)


def _matmul_nt_parts(parts, narrow, w, tm, tn, name, dep=None):
    m, c = parts[0].shape
    c2 = narrow.shape[1]
    n = w.shape[0]
    count = len(parts)
    assert m % tm == 0 and n % tn == 0 and all(p.shape == (m, c) for p in parts) and (count * c) % c2 == 0, (name, m, n)
    extra = [] if dep is None else [dep]

    def body(*refs):
        part_refs, narrow_ref, w_ref, w2_ref = refs[:count], refs[count], refs[count + 1], refs[count + 2]
        o_ref, acc = refs[-2:]
        k = pl.program_id(2)

        @pl.when(k == 0)
        def _():
            acc[...] = jnp.zeros_like(acc)

        for p in range(count):
            @pl.when(k == p)
            def _(p=p):
                acc[...] += lax.dot_general(part_refs[p][...].astype(BF16), w_ref[...].astype(BF16), NT,
                                            preferred_element_type=F32)

        @pl.when(k == count)
        def _():
            o_ref[...] = acc[...] + lax.dot_general(narrow_ref[...].astype(BF16), w2_ref[...].astype(BF16), NT,
                                                    preferred_element_type=F32)

    in_specs = ([pl.BlockSpec((tm, c), lambda j, i, k: (i, 0))] * count
                + [pl.BlockSpec((tm, c2), lambda j, i, k: (i, 0)),
                   pl.BlockSpec((tn, c), lambda j, i, k: (j, jnp.minimum(k, count - 1))),
                   pl.BlockSpec((tn, c2), lambda j, i, k: (j, count * c // c2))]
                + [pl.BlockSpec((8, HEAD_DIM), lambda j, i, k: (0, 0)) for _ in extra])
    return pl.pallas_call(
        body, name=name, grid=(n // tn, m // tm, count + 1), in_specs=in_specs,
        out_specs=pl.BlockSpec((tm, tn), lambda j, i, k: (i, j)),
        out_shape=jax.ShapeDtypeStruct((m, n), F32), scratch_shapes=[pltpu.VMEM((tm, tn), F32)],
        compiler_params=_params(dimension_semantics=("parallel", "parallel", "arbitrary")),
    )(*parts, narrow, w, w, *extra)


def _matmul_tn_parts(a, parts, tm, tn, tk, name, out_dtype):
    kdim, m = a.shape
    c = parts[0].shape[1]
    count = len(parts)
    per = c // tn
    assert m % tm == 0 and c % tn == 0 and kdim % tk == 0 and all(p.shape == (kdim, c) for p in parts), (name, m, c)
    ksteps = kdim // tk

    def body(*refs):
        a_ref, part_refs = refs[0], refs[1:1 + count]
        o_ref, acc = refs[-2:]
        j, k = pl.program_id(0), pl.program_id(2)

        @pl.when(k == 0)
        def _():
            acc[...] = jnp.zeros_like(acc)

        for p in range(count):
            @pl.when(j // per == p)
            def _(p=p):
                acc[...] += lax.dot_general(a_ref[...].astype(BF16), part_refs[p][...].astype(BF16), TN,
                                            preferred_element_type=F32)

        @pl.when(k == ksteps - 1)
        def _():
            o_ref[...] = acc[...].astype(o_ref.dtype)

    def part_spec(p):
        return pl.BlockSpec((tk, tn), lambda j, i, k: (jnp.where(j // per == p, k, 0), jnp.where(j // per == p, j % per, 0)))

    return pl.pallas_call(
        body, name=name, grid=(count * per, m // tm, ksteps),
        in_specs=[pl.BlockSpec((tk, tm), lambda j, i, k: (k, i))] + [part_spec(p) for p in range(count)],
        out_specs=pl.BlockSpec((tm, tn), lambda j, i, k: (i, j)),
        out_shape=jax.ShapeDtypeStruct((m, count * c), out_dtype), scratch_shapes=[pltpu.VMEM((tm, tn), F32)],
        compiler_params=_params(dimension_semantics=("parallel", "parallel", "arbitrary")),
    )(a, *parts)


def _rmsnorm_fwd(x, w_row, name):
    s = x.shape[0]
    tb = min(TIME_BLOCK, s)

    def body(x_ref, w_ref, o_ref):
        o_ref[...] = _rmsnorm(x_ref[...], w_ref[...]).astype(BF16)

    return pl.pallas_call(
        body, name=name, grid=(s // tb,),
        in_specs=[pl.BlockSpec((tb, D_MODEL), lambda i: (i, 0)), pl.BlockSpec((1, D_MODEL), lambda i: (0, 0))],
        out_specs=pl.BlockSpec((tb, D_MODEL), lambda i: (i, 0)),
        out_shape=jax.ShapeDtypeStruct((s, D_MODEL), BF16), compiler_params=_params(),
    )(x, w_row)


def _rmsnorm_bwd(x, w_row, dh, dres, name):
    s = x.shape[0]
    tb = min(TIME_BLOCK, s)

    def body(x_ref, w_ref, dh_ref, dres_ref, dx_ref, dw_ref):
        _, vjp = jax.vjp(_rmsnorm, x_ref[...], w_ref[...])
        dx, dw = vjp(dh_ref[...])
        dx_ref[...] = dres_ref[...] + dx

        @pl.when(pl.program_id(0) == 0)
        def _():
            dw_ref[...] = jnp.zeros_like(dw_ref)

        dw_ref[...] += dw

    row = pl.BlockSpec((tb, D_MODEL), lambda i: (i, 0))
    vec = pl.BlockSpec((1, D_MODEL), lambda i: (0, 0))
    return pl.pallas_call(
        body, name=name, grid=(s // tb,), in_specs=[row, vec, row, row], out_specs=[row, vec],
        out_shape=[jax.ShapeDtypeStruct((s, D_MODEL), F32), jax.ShapeDtypeStruct((1, D_MODEL), F32)],
        compiler_params=_params(),
    )(x, w_row, dh, dres)


def _final_loss(x, w_row, target, name):
    s = x.shape[0]
    tb = min(TIME_BLOCK, s)

    def loss_fn(xv, wv, tv):
        err = _rmsnorm(xv, wv) - tv
        return 0.5 * jnp.sum(jnp.sum(err * err, axis=-1, keepdims=True), axis=0, keepdims=True) * (1.0 / D_MODEL)

    def body(x_ref, w_ref, t_ref, loss_ref, dx_ref, dw_ref):
        tv = t_ref[...]
        loss, vjp = jax.vjp(lambda xv, wv: loss_fn(xv, wv, tv), x_ref[...], w_ref[...])
        dx, dw = vjp(jnp.ones((1, 1), F32))
        dx_ref[...] = dx

        @pl.when(pl.program_id(0) == 0)
        def _():
            dw_ref[...] = jnp.zeros_like(dw_ref)
            loss_ref[...] = jnp.zeros_like(loss_ref)

        dw_ref[...] += dw
        loss_ref[...] += jnp.broadcast_to(loss, loss_ref.shape)

    row = pl.BlockSpec((tb, D_MODEL), lambda i: (i, 0))
    vec = pl.BlockSpec((1, D_MODEL), lambda i: (0, 0))
    return pl.pallas_call(
        body, name=name, grid=(s // tb,), in_specs=[row, vec, row],
        out_specs=[pl.BlockSpec((1, HEAD_DIM), lambda i: (0, 0)), row, vec],
        out_shape=[jax.ShapeDtypeStruct((1, HEAD_DIM), F32), jax.ShapeDtypeStruct((s, D_MODEL), F32),
                   jax.ShapeDtypeStruct((1, D_MODEL), F32)],
        compiler_params=_params(),
    )(x, w_row, target)


def _head_specs(tb, time_of):
    def col(off):
        return pl.BlockSpec((tb, HEAD_DIM), lambda t, h: (time_of(t), off + h))
    return col


def _vec_spec():
    return pl.BlockSpec((1, HEAD_DIM), lambda t, h: (0, h))


def _lru_fwd(proj, conv_w, conv_b, wa, ba, wx, bx, lam, nw, name):
    s = proj.shape[0]
    tb = min(TIME_BLOCK, s)
    nt = s // tb
    col = _head_specs(tb, lambda t: t)

    def body(x_ref, z_ref, cw_ref, cb_ref, wa_ref, ba_ref, wx_ref, bx_ref, lam_ref, nw_ref,
             y_ref, hs_ref, xbuf, hcar):
        t, h = pl.program_id(0), pl.program_id(1)

        @pl.when(t == 0)
        def _():
            xbuf[h, pl.ds(0, 8), :] = jnp.zeros((8, HEAD_DIM), F32)
            hcar[h] = jnp.zeros((8, HEAD_DIM), F32)

        xbuf[h, pl.ds(8, tb), :] = x_ref[...]
        xc = _conv_taps(xbuf, h, cw_ref[...], tb) + cb_ref[...]
        a, b = _lru_gates(xc, wa_ref[...], ba_ref[...], wx_ref[...], bx_ref[...], lam_ref[...])
        hs_ref[...] = _scan_forward(a, b, hcar[h, pl.ds(0, 1), :])
        hcar[h, pl.ds(0, 1), :] = hs_ref[pl.ds(tb - 1, 1), :]
        xbuf[h, pl.ds(0, 8), :] = xbuf[h, pl.ds(tb, 8), :]
        y_ref[...] = _gated_norm(hs_ref[...], z_ref[...], nw_ref[...]).astype(BF16)

    vec = _vec_spec()
    return pl.pallas_call(
        body, name=name, grid=(nt, HEADS),
        in_specs=[col(COL_LRU_X), col(COL_LRU_Z), pl.BlockSpec((4, HEAD_DIM), lambda t, h: (0, h)), vec,
                  pl.BlockSpec((None, HEAD_DIM, HEAD_DIM), lambda t, h: (h, 0, 0)), vec,
                  pl.BlockSpec((None, HEAD_DIM, HEAD_DIM), lambda t, h: (h, 0, 0)), vec, vec, vec],
        out_specs=[col(0), col(0)],
        out_shape=[jax.ShapeDtypeStruct((s, 2 * D_MODEL), BF16), jax.ShapeDtypeStruct((s, D_MODEL), F32)],
        scratch_shapes=[pltpu.VMEM((HEADS, tb + 8, HEAD_DIM), F32), pltpu.VMEM((HEADS, 8, HEAD_DIM), F32)],
        compiler_params=_params(dimension_semantics=("arbitrary", "arbitrary")),
    )(proj, proj, conv_w, conv_b, wa, ba, wx, bx, lam, nw)


def _halo_spec(tb, nt, off):
    per = tb // 8
    return pl.BlockSpec((8, HEAD_DIM), lambda t, h: (jnp.maximum((nt - 1 - t) * per - 1, 0), off + h))


def _lru_bwd(proj, hs, dy, conv_w, conv_b, wa, ba, wx, bx, lam, nw, name):
    s = proj.shape[0]
    tb = min(TIME_BLOCK, s)
    nt = s // tb
    col = _head_specs(tb, lambda t: nt - 1 - t)

    def body(x_ref, xh_ref, z_ref, hs_ref, hh_ref, dy_ref, cw_ref, cb_ref, wa_ref, ba_ref, wx_ref, bx_ref,
             lam_ref, nw_ref, dx_ref, dz_ref, dcw_ref, dcb_ref, dwa_ref, dba_ref, dwx_ref, dbx_ref, dlam_ref,
             dnw_ref, xbuf, hbuf, dbuf, gcar):
        t, h = pl.program_id(0), pl.program_id(1)
        first_block = t == nt - 1

        @pl.when(t == 0)
        def _():
            dbuf[h, pl.ds(tb, 8), :] = jnp.zeros((8, HEAD_DIM), F32)
            gcar[h] = jnp.zeros((8, HEAD_DIM), F32)
            dcw_ref[h] = jnp.zeros((4, HEAD_DIM), F32)
            dwa_ref[h] = jnp.zeros((HEAD_DIM, HEAD_DIM), F32)
            dwx_ref[h] = jnp.zeros((HEAD_DIM, HEAD_DIM), F32)
            for ref in (dcb_ref, dba_ref, dbx_ref, dlam_ref, dnw_ref):
                ref[h] = jnp.zeros((1, HEAD_DIM), F32)

        keep = jnp.where(first_block, 0.0, 1.0)
        xbuf[0, pl.ds(0, 8), :] = xh_ref[...] * keep
        xbuf[0, pl.ds(8, tb), :] = x_ref[...]
        hbuf[pl.ds(0, 8), :] = hh_ref[...] * keep
        hbuf[pl.ds(8, tb), :] = hs_ref[...]
        cw = cw_ref[...]
        xc = _conv_taps(xbuf, 0, cw, tb) + cb_ref[...]
        (a, _), gates_vjp = jax.vjp(_lru_gates, xc, wa_ref[...], ba_ref[...], wx_ref[...], bx_ref[...], lam_ref[...])
        _, norm_vjp = jax.vjp(_gated_norm, hs_ref[...], z_ref[...], nw_ref[...])
        dh, dz, dnw = norm_vjp(dy_ref[...])
        dz_ref[...] = dz.astype(dz_ref.dtype)
        g = _scan_reverse(a, dh, gcar[h, pl.ds(0, 1), :])
        gcar[h, pl.ds(0, 1), :] = a[0:1, :] * g[0:1, :]
        dxc, dwa, dba, dwx, dbx, dlam = gates_vjp((g * hbuf[pl.ds(7, tb), :], g))
        dx, dcw = _conv_backward(dbuf, h, xbuf, 0, cw, dxc, tb)
        dx_ref[...] = dx.astype(dx_ref.dtype)
        dcw_ref[h] += dcw
        dcb_ref[h] += jnp.sum(dxc, axis=0, keepdims=True)
        dwa_ref[h] += dwa
        dwx_ref[h] += dwx
        dba_ref[h] += dba
        dbx_ref[h] += dbx
        dlam_ref[h] += dlam
        dnw_ref[h] += dnw

    vec = _vec_spec()
    mat = pl.BlockSpec((None, HEAD_DIM, HEAD_DIM), lambda t, h: (h, 0, 0))

    def whole(shape):
        return pl.BlockSpec(shape, lambda t, h: (0,) * len(shape))

    head_vec = jax.ShapeDtypeStruct((HEADS, 1, HEAD_DIM), F32)
    head_mat = jax.ShapeDtypeStruct((HEADS, HEAD_DIM, HEAD_DIM), F32)
    return pl.pallas_call(
        body, name=name, grid=(nt, HEADS),
        in_specs=[col(COL_LRU_X), _halo_spec(tb, nt, COL_LRU_X), col(COL_LRU_Z), col(0), _halo_spec(tb, nt, 0), col(0),
                  pl.BlockSpec((4, HEAD_DIM), lambda t, h: (0, h)), vec, mat, vec, mat, vec, vec, vec],
        out_specs=[col(0), col(0), whole((HEADS, 4, HEAD_DIM)), whole((HEADS, 1, HEAD_DIM)),
                   whole((HEADS, HEAD_DIM, HEAD_DIM)), whole((HEADS, 1, HEAD_DIM)),
                   whole((HEADS, HEAD_DIM, HEAD_DIM)), whole((HEADS, 1, HEAD_DIM)), whole((HEADS, 1, HEAD_DIM)),
                   whole((HEADS, 1, HEAD_DIM))],
        out_shape=[jax.ShapeDtypeStruct((s, D_MODEL), BF16), jax.ShapeDtypeStruct((s, D_MODEL), BF16),
                   jax.ShapeDtypeStruct((HEADS, 4, HEAD_DIM), F32), head_vec, head_mat, head_vec, head_mat, head_vec,
                   head_vec, head_vec],
        scratch_shapes=[pltpu.VMEM((1, tb + 8, HEAD_DIM), F32), pltpu.VMEM((tb + 8, HEAD_DIM), F32),
                        pltpu.VMEM((HEADS, tb + 8, HEAD_DIM), F32), pltpu.VMEM((HEADS, 8, HEAD_DIM), F32)],
        compiler_params=_params(dimension_semantics=("arbitrary", "arbitrary")),
    )(proj, proj, proj, hs, hs, dy, conv_w, conv_b, wa, ba, wx, bx, lam, nw)


def _group_col(tb, time_of):
    def col(off):
        return pl.BlockSpec((tb, DN_GROUP * HEAD_DIM), lambda t, hg: (time_of(t), off // DN_GROUP + hg))
    return col


def _dn_fwd(proj, y, conv_w, a_log_row, dt_row, nw, name):
    s = proj.shape[0]
    tb = min(DN_TIME_BLOCK, s)
    nt = s // tb
    nchunk = tb // CHUNK
    grp = DN_GROUP
    col = _group_col(tb, lambda t: t)

    def body(q_ref, k_ref, v_ref, z_ref, ba_ref, cwq_ref, cwk_ref, cwv_ref, al_ref, dt_ref, nw_ref, y_in_ref,
             y_ref, o_ref, st_ref, inv_ref, xbuf, state):
        t, hg = pl.program_id(0), pl.program_id(1)

        def chunks(a):
            return a.reshape(nchunk, CHUNK, a.shape[-1])

        prepared = []
        for gi in range(grp):
            h = hg * grp + gi
            lanes = slice(gi * HEAD_DIM, (gi + 1) * HEAD_DIM)

            @pl.when(t == 0)
            def _(h=h):
                for i in range(3):
                    xbuf[3 * h + i, pl.ds(0, 8), :] = jnp.zeros((8, HEAD_DIM), F32)
                state[h] = jnp.zeros((HEAD_DIM, HEAD_DIM), F32)

            conv = []
            for i, (ref, cw_ref) in enumerate(((q_ref, cwq_ref), (k_ref, cwk_ref), (v_ref, cwv_ref))):
                xbuf[3 * h + i, pl.ds(8, tb), :] = ref[:, lanes]
                conv.append(_conv_taps(xbuf, 3 * h + i, cw_ref[:, lanes], tb))
                xbuf[3 * h + i, pl.ds(0, 8), :] = xbuf[3 * h + i, pl.ds(tb, 8), :]
            prepared.append([chunks(a) for a in
                             _dn_prep(conv[0], conv[1], conv[2], ba_ref[...], al_ref[...], dt_ref[...], h)])
        qs, ks, vs, gs, bs = [jnp.concatenate([p[i] for p in prepared], axis=0) for i in range(5)]
        u, w, attn, qe, kdec, eglast, tinv = _dn_chunks_head(qs, ks, vs, gs, bs)
        inv_ref[...] = tinv.reshape(grp, nchunk, CHUNK, CHUNK)
        w_u = jnp.concatenate([w, u], axis=2)
        kdec_w_u = _dot(kdec, w_u, B_TN, "bf16")
        attn_w_u = _dot(attn, w_u, B_NN, "bf16")
        st = [state[hg * grp + gi] for gi in range(grp)]
        for c in range(nchunk):
            for gi in range(grp):
                n = gi * nchunk + c
                st_ref[gi, c] = st[gi]
                st[gi] = st[gi] * eglast[n] - _NN_B(kdec_w_u[n, :, :HEAD_DIM], st[gi]) + kdec_w_u[n, :, HEAD_DIM:]
        for gi in range(grp):
            state[hg * grp + gi] = st[gi]
        states = st_ref[...].reshape(grp * nchunk, HEAD_DIM, HEAD_DIM)
        o = _dot(qe - attn_w_u[:, :, :HEAD_DIM], states, B_NN, "bf16") + attn_w_u[:, :, HEAD_DIM:]
        for gi in range(grp):
            lanes = slice(gi * HEAD_DIM, (gi + 1) * HEAD_DIM)
            o_head = o[gi * nchunk:(gi + 1) * nchunk].reshape(tb, HEAD_DIM)
            o_ref[:, lanes] = o_head
            y_ref[:, lanes] = _gated_norm(o_head, z_ref[:, lanes], nw_ref[...]).astype(BF16)

    def cw_spec(off):
        return pl.BlockSpec((4, grp * HEAD_DIM), lambda t, hg: (0, off // grp + hg))

    row128 = pl.BlockSpec((1, HEAD_DIM), lambda t, hg: (0, 0))
    return pl.pallas_call(
        body, name=name, grid=(nt, HEADS // grp),
        in_specs=[col(COL_Q), col(COL_K), col(COL_V), col(COL_DN_Z),
                  pl.BlockSpec((tb, HEAD_DIM), lambda t, hg: (t, COL_BA)),
                  cw_spec(0), cw_spec(HEADS), cw_spec(2 * HEADS), row128, row128, row128,
                  pl.BlockSpec(memory_space=pl.ANY)],
        out_specs=[col(HEADS), col(0),
                   pl.BlockSpec((grp, nchunk, HEAD_DIM, HEAD_DIM), lambda t, hg: (hg, t, 0, 0)),
                   pl.BlockSpec((grp, nchunk, CHUNK, CHUNK), lambda t, hg: (hg, t, 0, 0))],
        out_shape=[jax.ShapeDtypeStruct((s, 2 * D_MODEL), BF16), jax.ShapeDtypeStruct((s, D_MODEL), F32),
                   jax.ShapeDtypeStruct((HEADS, s // CHUNK, HEAD_DIM, HEAD_DIM), F32),
                   jax.ShapeDtypeStruct((HEADS, s // CHUNK, CHUNK, CHUNK), F32)],
        input_output_aliases={11: 0},
        scratch_shapes=[pltpu.VMEM((3 * HEADS, tb + 8, HEAD_DIM), F32), pltpu.VMEM((HEADS, HEAD_DIM, HEAD_DIM), F32)],
        compiler_params=_params(dimension_semantics=("arbitrary", "arbitrary")),
    )(proj, proj, proj, proj, proj, conv_w, conv_w, conv_w, a_log_row, dt_row, nw, y)


def _dn_bwd(proj, o, states, inverses, dy, conv_w, a_log_row, dt_row, nw, name):
    s = proj.shape[0]
    tb = min(DN_TIME_BLOCK, s)
    nt = s // tb
    nchunk = tb // CHUNK
    grp = DN_GROUP
    col = _group_col(tb, lambda t: nt - 1 - t)

    def body(q_ref, qh_ref, k_ref, kh_ref, v_ref, vh_ref, z_ref, ba_ref, o_ref, st_ref, inv_ref, dy_ref,
             cwq_ref, cwk_ref, cwv_ref, al_ref, dt_ref, nw_ref,
             dq_ref, dk_ref, dv_ref, dz_ref, dba_ref, dcw_ref, dal_ref, ddt_ref, dnw_ref,
             xbuf, dbuf, dstate, dst_s):
        t, hg = pl.program_id(0), pl.program_id(1)
        keep = jnp.where(t == nt - 1, 0.0, 1.0)

        @pl.when((t == 0) & (hg == 0))
        def _():
            for ref in (dal_ref, ddt_ref, dnw_ref):
                ref[...] = jnp.zeros_like(ref)

        def chunks(a):
            return a.reshape(nchunk, CHUNK, a.shape[-1])

        prepared, prep_vjps, dos = [], [], []
        for gi in range(grp):
            h = hg * grp + gi
            lanes = slice(gi * HEAD_DIM, (gi + 1) * HEAD_DIM)

            @pl.when(t == 0)
            def _(h=h):
                for i in range(3):
                    dbuf[3 * h + i, pl.ds(tb, 8), :] = jnp.zeros((8, HEAD_DIM), F32)
                    dcw_ref[3 * h + i] = jnp.zeros((4, HEAD_DIM), F32)
                dstate[h] = jnp.zeros((HEAD_DIM, HEAD_DIM), F32)

            conv = []
            for i, (ref, halo, cw_ref) in enumerate(((q_ref, qh_ref, cwq_ref), (k_ref, kh_ref, cwk_ref),
                                                     (v_ref, vh_ref, cwv_ref))):
                xbuf[3 * gi + i, pl.ds(0, 8), :] = halo[:, lanes] * keep
                xbuf[3 * gi + i, pl.ds(8, tb), :] = ref[:, lanes]
                conv.append(_conv_taps(xbuf, 3 * gi + i, cw_ref[:, lanes], tb))
            outs, prep_vjp = jax.vjp(
                lambda qc, kc, vc, ba, al, dt, h=h: _dn_prep(qc, kc, vc, ba, al, dt, h),
                conv[0], conv[1], conv[2], ba_ref[...], al_ref[...], dt_ref[...])
            prepared.append([chunks(a) for a in outs])
            prep_vjps.append(prep_vjp)
            _, norm_vjp = jax.vjp(_gated_norm, o_ref[:, lanes], z_ref[:, lanes], nw_ref[...])
            do, dz, dnw = norm_vjp(dy_ref[:, lanes])
            dz_ref[:, lanes] = dz.astype(dz_ref.dtype)
            dnw_ref[...] += dnw
            dos.append(chunks(do))
        qs, ks, vs, gs, bs = [jnp.concatenate([p[i] for p in prepared], axis=0) for i in range(5)]
        do = jnp.concatenate(dos, axis=0)
        states_in = st_ref[...].reshape(grp * nchunk, HEAD_DIM, HEAD_DIM)
        kept = inv_ref[...].reshape(grp * nchunk, CHUNK, CHUNK)
        _, chunks_vjp, (w, attn, qe, kdec, eglast) = jax.vjp(
            functools.partial(_dn_chunks, kept), qs, ks, vs, gs, bs, states_in, has_aux=True)
        kdec_w = _dot(kdec, w, B_TN, "bf16")
        fixed = _dot(qe, do, B_TN, "bf16") - _dot(w, _dot(attn, do, B_TN, "bf16"), B_TN, "bf16")
        dst = [dstate[hg * grp + gi] for gi in range(grp)]
        for c in reversed(range(nchunk)):
            for gi in range(grp):
                n = gi * nchunk + c
                dst_s[n] = dst[gi]
                dst[gi] = dst[gi] * eglast[n] - _dot(kdec_w[n], dst[gi], TN, "bf16") + fixed[n]
        for gi in range(grp):
            dstate[hg * grp + gi] = dst[gi]
        cts = chunks_vjp((do, dst_s[...]))[:5]

        dba_sum = None
        for gi in range(grp):
            h = hg * grp + gi
            lanes = slice(gi * HEAD_DIM, (gi + 1) * HEAD_DIM)
            per_head = [ct[gi * nchunk:(gi + 1) * nchunk].reshape(tb, ct.shape[-1]) for ct in cts]
            dqc, dkc, dvc, dba, dal, ddt = prep_vjps[gi](tuple(per_head))
            for i, (dxc, out, cw_ref) in enumerate(((dqc, dq_ref, cwq_ref), (dkc, dk_ref, cwk_ref),
                                                    (dvc, dv_ref, cwv_ref))):
                dx, dcw = _conv_backward(dbuf, 3 * h + i, xbuf, 3 * gi + i, cw_ref[:, lanes], dxc, tb)
                out[:, lanes] = dx.astype(out.dtype)
                dcw_ref[3 * h + i] += dcw
            dal_ref[...] += dal
            ddt_ref[...] += ddt
            dba_sum = dba if dba_sum is None else dba_sum + dba

        @pl.when(hg == 0)
        def _():
            dba_ref[...] = dba_sum.astype(dba_ref.dtype)

        @pl.when(hg > 0)
        def _():
            dba_ref[...] += dba_sum.astype(dba_ref.dtype)

    def cw_spec(off):
        return pl.BlockSpec((4, grp * HEAD_DIM), lambda t, hg: (0, off // grp + hg))

    def halo(off):
        per = tb // 8
        return pl.BlockSpec((8, grp * HEAD_DIM),
                            lambda t, hg: (jnp.maximum((nt - 1 - t) * per - 1, 0), off // grp + hg))

    def whole(shape):
        return pl.BlockSpec(shape, lambda t, hg: (0,) * len(shape))

    row128 = whole((1, HEAD_DIM))
    blk = (tb, HEAD_DIM)
    act = jax.ShapeDtypeStruct((s, D_MODEL), BF16)
    row_out = jax.ShapeDtypeStruct((1, HEAD_DIM), F32)
    return pl.pallas_call(
        body, name=name, grid=(nt, HEADS // grp),
        in_specs=[col(COL_Q), halo(COL_Q), col(COL_K), halo(COL_K), col(COL_V), halo(COL_V), col(COL_DN_Z),
                  pl.BlockSpec(blk, lambda t, hg: (nt - 1 - t, COL_BA)), col(0),
                  pl.BlockSpec((grp, nchunk, HEAD_DIM, HEAD_DIM), lambda t, hg: (hg, nt - 1 - t, 0, 0)),
                  pl.BlockSpec((grp, nchunk, CHUNK, CHUNK), lambda t, hg: (hg, nt - 1 - t, 0, 0)), col(HEADS),
                  cw_spec(0), cw_spec(HEADS), cw_spec(2 * HEADS), row128, row128, row128],
        out_specs=[col(0), col(0), col(0), col(0), pl.BlockSpec(blk, lambda t, hg: (nt - 1 - t, 0)),
                   whole((3 * HEADS, 4, HEAD_DIM)), row128, row128, row128],
        out_shape=[act, act, act, act, jax.ShapeDtypeStruct((s, HEAD_DIM), F32),
                   jax.ShapeDtypeStruct((3 * HEADS, 4, HEAD_DIM), F32), row_out, row_out, row_out],
        scratch_shapes=[pltpu.VMEM((3 * grp, tb + 8, HEAD_DIM), F32), pltpu.VMEM((3 * HEADS, tb + 8, HEAD_DIM), F32),
                        pltpu.VMEM((HEADS, HEAD_DIM, HEAD_DIM), F32),
                        pltpu.VMEM((grp * nchunk, HEAD_DIM, HEAD_DIM), F32)],
        compiler_params=_params(dimension_semantics=("arbitrary", "arbitrary")),
    )(proj, proj, proj, proj, proj, proj, proj, proj, o, states, inverses, dy, conv_w, conv_w, conv_w, a_log_row, dt_row,
      nw)


def _mesh_position():
    x, y, c = lax.axis_index("x"), lax.axis_index("y"), lax.axis_index("c")
    return x, y, c, 4 * x + 2 * y + c


def _peer(k, x, y, c):
    px = 1 - x if k & 4 else x
    py = 1 - y if k & 2 else y
    pc = 1 - c if k & 1 else c
    return (px, py, pc), 4 * px + 2 * py + pc


def _exchange_copies(ins, lands, scatter, send_sems, recv_sems, receives=True):
    x, y, c, me = _mesh_position()
    sends, recvs = [], []
    for i, (src, land) in enumerate(zip(ins, lands)):
        for k in range(1, N_DEV):
            peer, peer_id = _peer(k, x, y, c)
            sem = i * (N_DEV - 1) + k - 1
            for dst, out in ((me, sends), (peer_id, recvs)) if receives else ((me, sends),):
                out.append(pltpu.make_async_remote_copy(
                    src_ref=src.at[peer_id] if scatter[i] else src, dst_ref=land.at[dst],
                    send_sem=send_sems.at[sem], recv_sem=recv_sems.at[sem],
                    device_id=peer, device_id_type=pl.DeviceIdType.MESH))
    return sends, recvs


def _landing_shape(a, scatter):
    return a.shape if scatter else (N_DEV,) + a.shape


def _two_level_gather(arrays, name):
    n = len(arrays)
    per = N_DEV - 1

    def body(*refs):
        ins, outs = refs[:n], refs[n:2 * n]
        send_sems, recv_sems, local_sems = refs[2 * n:]
        x, y, c, me = _mesh_position()
        sibling = (x, y, 1 - c)
        chips = [(1 - x, y), (x, 1 - y), (1 - x, 1 - y)]

        def copy(i, k, block, to, src=None):
            slot = outs[i].at[4 * block[0] + 2 * block[1] + block[2]]
            return pltpu.make_async_remote_copy(
                src_ref=slot if src is None else src, dst_ref=slot,
                send_sem=send_sems.at[i * per + k], recv_sem=recv_sems.at[i * per + k],
                device_id=to, device_id_type=pl.DeviceIdType.MESH)

        local = [pltpu.make_async_copy(ins[i], outs[i].at[me], local_sems.at[i]) for i in range(n)]
        first = []
        for i in range(n):
            first.append(copy(i, 0, (x, y, c), sibling, src=ins[i]))
            first += [copy(i, 1 + j, (x, y, c), (*chip, c), src=ins[i]) for j, chip in enumerate(chips)]
        for cp in local + first:
            cp.start()
        passed = []
        for i in range(n):
            for j, chip in enumerate(chips):
                copy(i, 1 + j, (*chip, c), (x, y, c)).wait_recv()
                passed.append(copy(i, 4 + j, (*chip, c), sibling))
                passed[-1].start()
        for i in range(n):
            copy(i, 0, sibling, (x, y, c)).wait_recv()
            for j, chip in enumerate(chips):
                copy(i, 4 + j, (*chip, 1 - c), (x, y, c)).wait_recv()
        for cp in first + passed:
            cp.wait_send()
        for cp in local:
            cp.wait()

    hbm = pl.BlockSpec(memory_space=pl.ANY)
    return pl.pallas_call(
        body, name=name, in_specs=[hbm] * n, out_specs=[hbm] * n,
        out_shape=[jax.ShapeDtypeStruct((N_DEV,) + a.shape, a.dtype) for a in arrays],
        scratch_shapes=[pltpu.SemaphoreType.DMA((n * per,)), pltpu.SemaphoreType.DMA((n * per,)),
                        pltpu.SemaphoreType.DMA((n,))],
    )(*arrays)


_HBM = pl.BlockSpec(memory_space=pltpu.HBM)
_SEM = pl.BlockSpec(memory_space=pltpu.SEMAPHORE)
_DATAFLOW = pltpu.SideEffectType.DATAFLOW_SIDE_EFFECTING


def _exchange_start(arrays, scatter, name):
    n = len(arrays)
    srcs = [pltpu.with_memory_space_constraint(a, pltpu.HBM) for a in arrays]
    lands = [pltpu.with_memory_space_constraint(lax.empty(_landing_shape(a, sc), a.dtype), pltpu.HBM)
             for a, sc in zip(arrays, scatter)]
    nsem = n * (N_DEV - 1)

    def body(*refs):
        ins, zones = refs[:n], refs[n:2 * n]
        send_sems, recv_sems = refs[2 * n], refs[2 * n + 1]
        token = refs[-1]
        sends, _ = _exchange_copies(ins, zones, scatter, send_sems, recv_sems, receives=False)
        for cp in sends:
            cp.start()
        token[...] = jnp.zeros_like(token)

    res = pl.pallas_call(
        body, name=name,
        out_shape=(pltpu.SemaphoreType.DMA((nsem,)), pltpu.SemaphoreType.DMA((nsem,)),
                   *[pltpu.HBM(a.shape, a.dtype) for a in srcs + lands], jax.ShapeDtypeStruct((8, HEAD_DIM), F32)),
        in_specs=[_HBM] * (2 * n),
        out_specs=(_SEM, _SEM, *[_HBM] * (2 * n), pl.BlockSpec(memory_space=pltpu.VMEM)),
        input_output_aliases={i: 2 + i for i in range(2 * n)},
        compiler_params=pltpu.CompilerParams(has_side_effects=_DATAFLOW),
    )(*srcs, *lands)
    return dict(sems=res[:2], srcs=res[2:2 + n], lands=res[2 + n:2 + 2 * n], token_block=res[-1],
                token=res[-1][0, 0], scatter=scatter)


def _exchange_wait(started, after, name):
    scatter = started["scatter"]
    n = len(scatter)

    def body(*refs):
        ins, zones = refs[:n], refs[n:2 * n]
        send_sems, recv_sems = refs[2 * n], refs[2 * n + 1]
        sends, recvs = _exchange_copies(ins, zones, scatter, send_sems, recv_sems)
        for cp in sends:
            cp.wait_send()
        for cp in recvs:
            cp.wait_recv()

    thru = list(started["srcs"]) + list(started["lands"])
    res = pl.pallas_call(
        body, name=name, out_shape=[pltpu.HBM(a.shape, a.dtype) for a in thru],
        in_specs=[_HBM] * (2 * n) + [_SEM, _SEM, pl.BlockSpec(memory_space=pl.ANY)], out_specs=[_HBM] * (2 * n),
        input_output_aliases={i: i for i in range(2 * n)},
        compiler_params=pltpu.CompilerParams(has_side_effects=_DATAFLOW),
    )(*thru, *started["sems"], after)
    me = 4 * lax.axis_index("x") + 2 * lax.axis_index("y") + lax.axis_index("c")
    out = []
    for src, got, sc in zip(res[:n], res[n:], scatter):
        own = lax.dynamic_index_in_dim(src, me, 0, keepdims=False) if sc else src
        out.append(lax.dynamic_update_index_in_dim(got, own, me, 0))
    return out


def _adamw(parts, w, m, v, name, rows_per_step, row_offset=0, into=None):
    rows, cols = parts.shape[1:]
    tr = min(rows_per_step, rows)
    assert rows % tr == 0 and row_offset % tr == 0, (name, rows, tr, row_offset)
    first = row_offset // tr
    c1 = 1.0 / (1.0 - ADAM_B1 ** ADAM_STEP)
    c2 = 1.0 / (1.0 - ADAM_B2 ** ADAM_STEP)

    def body(p_ref, w_ref, m_ref, v_ref, *rest):
        g_ref, d_ref, nm_ref, nv_ref = rest[-4:]
        g = p_ref[0].astype(F32)
        for d in range(1, N_DEV):
            g = g + p_ref[d].astype(F32)
        nm = ADAM_B1 * m_ref[...] + (1.0 - ADAM_B1) * g
        nv = ADAM_B2 * v_ref[...] + (1.0 - ADAM_B2) * (g * g)
        g_ref[...] = g
        nm_ref[...] = nm
        nv_ref[...] = nv
        d_ref[...] = -ADAM_LR * ((nm * c1) / (jnp.sqrt(nv * c2) + ADAM_EPS) + ADAM_WD * w_ref[...])

    blk = pl.BlockSpec((tr, cols), lambda i: (i + first, 0))
    shape = jax.ShapeDtypeStruct(w.shape, F32)
    prior = [] if into is None else list(into)
    return pl.pallas_call(
        body, name=name, grid=(rows // tr,),
        in_specs=[pl.BlockSpec((N_DEV, tr, cols), lambda i: (0, i, 0)), blk, blk, blk]
        + [pl.BlockSpec(memory_space=pl.ANY)] * len(prior),
        out_specs=[blk] * 4, out_shape=[shape] * 4,
        input_output_aliases={4 + j: j for j in range(len(prior))}, compiler_params=_params(),
    )(parts, w, m, v, *prior)


_LAYERED = ("norm_w", "lru_conv_b", "lru_wa", "lru_ba", "lru_wx", "lru_bx", "lru_lambda", "lru_norm_w",
            "dn_A_log", "dn_dt_bias", "dn_norm_w")
_PACK_LRU = _LAYERED[1:8]
_PACK_LAST = _LAYERED[:1] + _LAYERED[8:]
_WEIGHTS = ("norm_w", "w_in", "lru_conv_w", "lru_conv_b", "lru_wa", "lru_ba", "lru_wx", "lru_bx", "lru_lambda",
            "lru_norm_w", "dn_conv_w", "dn_A_log", "dn_dt_bias", "dn_norm_w", "w_out", "final_norm_w")


def _pack_layer(tree, layer, tail=(), names=_LAYERED):
    rows = []
    for name in names:
        a = tree[name][layer]
        if a.shape[-1] == HEADS:
            a = jnp.pad(a, (0, HEAD_DIM - HEADS))
        rows.append(a.reshape(-1, HEAD_DIM))
    rows += [t.reshape(-1, HEAD_DIM) for t in tail]
    packed = jnp.concatenate(rows, axis=0)
    return jnp.pad(packed, ((0, (-packed.shape[0]) % 8), (0, 0)))


def _unpack_layer(packed, like, names=_LAYERED):
    out, at = {}, 0
    for name in names:
        shape = like[name].shape[1:]
        if shape[-1] == HEADS:
            n = 1
            out[name] = packed[at, :HEADS]
        else:
            n = like[name][0].size // HEAD_DIM
            out[name] = packed[at:at + n].reshape(shape)
        at += n
    return out, at


def _heads_to_channels(a):
    return jnp.transpose(a, (1, 0, 2)).reshape(a.shape[1], HEADS * HEAD_DIM)


def kernel(x, norm_w, w_in, lru_conv_w, lru_conv_b, lru_wa, lru_ba, lru_wx, lru_bx, lru_lambda, lru_norm_w, dn_conv_w, dn_A_log, dn_dt_bias, dn_norm_w, w_out, final_norm_w, loss_target, m_norm_w, m_w_in, m_lru_conv_w, m_lru_conv_b, m_lru_wa, m_lru_ba, m_lru_wx, m_lru_bx, m_lru_lambda, m_lru_norm_w, m_dn_conv_w, m_dn_A_log, m_dn_dt_bias, m_dn_norm_w, m_w_out, m_final_norm_w, v_norm_w, v_w_in, v_lru_conv_w, v_lru_conv_b, v_lru_wa, v_lru_ba, v_lru_wx, v_lru_bx, v_lru_lambda, v_lru_norm_w, v_dn_conv_w, v_dn_A_log, v_dn_dt_bias, v_dn_norm_w, v_w_out, v_final_norm_w):
    weights = dict(norm_w=norm_w, w_in=w_in, lru_conv_w=lru_conv_w, lru_conv_b=lru_conv_b, lru_wa=lru_wa,
                   lru_ba=lru_ba, lru_wx=lru_wx, lru_bx=lru_bx, lru_lambda=lru_lambda, lru_norm_w=lru_norm_w,
                   dn_conv_w=dn_conv_w, dn_A_log=dn_A_log, dn_dt_bias=dn_dt_bias, dn_norm_w=dn_norm_w,
                   w_out=w_out, final_norm_w=final_norm_w)
    mom_m = dict(norm_w=m_norm_w, w_in=m_w_in, lru_conv_w=m_lru_conv_w, lru_conv_b=m_lru_conv_b, lru_wa=m_lru_wa,
                 lru_ba=m_lru_ba, lru_wx=m_lru_wx, lru_bx=m_lru_bx, lru_lambda=m_lru_lambda,
                 lru_norm_w=m_lru_norm_w, dn_conv_w=m_dn_conv_w, dn_A_log=m_dn_A_log, dn_dt_bias=m_dn_dt_bias,
                 dn_norm_w=m_dn_norm_w, w_out=m_w_out, final_norm_w=m_final_norm_w)
    mom_v = dict(norm_w=v_norm_w, w_in=v_w_in, lru_conv_w=v_lru_conv_w, lru_conv_b=v_lru_conv_b, lru_wa=v_lru_wa,
                 lru_ba=v_lru_ba, lru_wx=v_lru_wx, lru_bx=v_lru_bx, lru_lambda=v_lru_lambda,
                 lru_norm_w=v_lru_norm_w, dn_conv_w=v_dn_conv_w, dn_A_log=v_dn_A_log, dn_dt_bias=v_dn_dt_bias,
                 dn_norm_w=v_dn_norm_w, w_out=v_w_out, final_norm_w=v_final_norm_w)
    depth = norm_w.shape[0]
    xs = x[0]
    s = xs.shape[0]
    tm = min(1024, s)

    assert depth >= 2, depth

    def row(a):
        return a.reshape(1, -1)

    def pad_row(a):
        return jnp.pad(a, (0, HEAD_DIM - a.shape[0])).reshape(1, HEAD_DIM)

    def full_w_in(g):
        w = jnp.transpose(g, (1, 2, 0, 3)).reshape(g.shape[1], D_MODEL, D_IN)
        return jnp.pad(w, ((0, 0), (0, 0), (0, D_IN_PAD - D_IN)))

    g_win0, g_lcw, g_dcw = _two_level_gather([w_in[:1].astype(BF16), lru_conv_w, dn_conv_w], "gather_first")
    rest = _exchange_start([w_in[1:].astype(BF16), w_out.astype(BF16)], [False] * 2, "gather_rest_start")
    win = [full_w_in(g_win0)[0]]
    wout = None
    lcw = jnp.transpose(g_lcw, (1, 2, 0, 3)).reshape(depth, 4, D_MODEL)
    dcw = jnp.transpose(g_dcw, (1, 2, 0, 3)).reshape(depth, 4, 3 * D_MODEL)

    saved = []
    cur = xs
    for l in range(depth):
        nw_row = row(norm_w[l]) + rest["token"] if l == 0 else row(norm_w[l])
        hn = _rmsnorm_fwd(cur, nw_row, f"norm_fwd_{l}")
        proj = _matmul(hn, win[l], "nn", tm, 896, D_MODEL, f"in_proj_{l}")
        y_lru, hs = _lru_fwd(proj, lcw[l], row(lru_conv_b[l]), lru_wa[l], row(lru_ba[l]), lru_wx[l], row(lru_bx[l]),
                             row(lru_lambda[l]), row(lru_norm_w[l]), f"lru_fwd_{l}")
        ycat, o_dn, states, inverses = _dn_fwd(proj, y_lru, dcw[l], pad_row(dn_A_log[l]), pad_row(dn_dt_bias[l]),
                                     row(dn_norm_w[l]), f"dn_fwd_{l}")
        if l == 0:
            g_win_rest, g_wout = _exchange_wait(rest, ycat, "gather_rest_wait")
            win += list(full_w_in(g_win_rest))
            wout = jnp.transpose(g_wout, (1, 0, 2, 3)).reshape(depth, 2 * D_MODEL, D_MODEL)
        nxt = _matmul(ycat, wout[l], "nn", tm, D_MODEL, 2 * D_MODEL, f"out_proj_{l}", add=cur)
        saved.append((cur, hn, proj, hs, o_dn, states, inverses, ycat))
        cur = nxt
    loss_part, dx, d_final = _final_loss(cur, row(final_norm_w), loss_target[0], "final_loss")

    def win_slots(g):
        return jnp.transpose(g.reshape(D_MODEL, N_DEV, D_IN // N_DEV), (1, 0, 2))

    def wout_slots(g):
        return g.reshape(N_DEV, 2 * D_MODEL // N_DEV, D_MODEL)

    grads = {k: [None] * depth for k in _WEIGHTS if k not in ("final_norm_w", "w_in", "w_out")}
    started = {}
    token = None
    for l in reversed(range(depth)):
        x_in, hn, proj, hs, o_dn, states, inverses, ycat = saved[l]
        dy = _matmul(dx, wout[l], "nt", tm, D_MODEL, D_MODEL, f"out_proj_dy_{l}")
        g_wout_l = _matmul(ycat, dx, "tn", D_MODEL, D_MODEL, tm, f"out_proj_dw_{l}", out_dtype=BF16)
        if l == 0:
            started["w_out_0"] = _exchange_start([wout_slots(g_wout_l)], [True], "exchange_w_out_0_start")
            token = token + started["w_out_0"]["token"]
        cb_row = row(lru_conv_b[l]) if token is None else row(lru_conv_b[l]) + token
        (dlx, dlz, g_lcw, g_lcb, g_wa, g_ba, g_wx, g_bx, g_lam, g_lnw) = _lru_bwd(
            proj, hs, dy, lcw[l], cb_row, lru_wa[l], row(lru_ba[l]), lru_wx[l], row(lru_bx[l]),
            row(lru_lambda[l]), row(lru_norm_w[l]), f"lru_bwd_{l}")
        grads["lru_conv_w"][l] = _heads_to_channels(g_lcw)
        grads["lru_conv_b"][l] = g_lcb.reshape(D_MODEL)
        grads["lru_wa"][l] = g_wa
        grads["lru_ba"][l] = g_ba.reshape(D_MODEL)
        grads["lru_wx"][l] = g_wx
        grads["lru_bx"][l] = g_bx.reshape(D_MODEL)
        grads["lru_lambda"][l] = g_lam.reshape(D_MODEL)
        grads["lru_norm_w"][l] = g_lnw.reshape(D_MODEL)
        al_row = pad_row(dn_A_log[l])
        if l == 0:
            started["pack_0"] = _exchange_start([_pack_layer(grads, 0, names=_PACK_LRU)], [False],
                                                "exchange_pack_0_start")
            al_row = al_row + started["pack_0"]["token"]
        (dq, dk, dv, ddz, dba, g_dcw3, g_al, g_dt, g_dnw) = _dn_bwd(
            proj, o_dn, states, inverses, dy, dcw[l], al_row, pad_row(dn_dt_bias[l]), row(dn_norm_w[l]), f"dn_bwd_{l}")
        g_dcw3 = g_dcw3.reshape(HEADS, 3, 4, HEAD_DIM)
        grads["dn_conv_w"][l] = jnp.concatenate([_heads_to_channels(g_dcw3[:, i]) for i in range(3)], axis=1)
        grads["dn_A_log"][l] = g_al[0, :HEADS]
        grads["dn_dt_bias"][l] = g_dt[0, :HEADS]
        grads["dn_norm_w"][l] = g_dnw.reshape(HEAD_DIM)
        dep = None
        pieces = [dlx, dlz, dq, dk, dv, ddz]
        wide = len(pieces) * D_MODEL
        dba = dba.astype(BF16)
        g_win_l = jnp.concatenate(
            [_matmul_tn_parts(hn, pieces, D_MODEL, D_MODEL, tm, f"in_proj_dw_{l}", BF16),
             _matmul(hn, dba, "tn", D_MODEL, HEAD_DIM, tm, f"in_proj_dw_gates_{l}", out_dtype=BF16)[:, :D_IN - wide]],
            axis=1)
        if l == 0:
            started[0] = _exchange_start([win_slots(g_win_l)], [True], "exchange_0_start")
            dep = started[0]["token_block"]
        dh = _matmul_nt_parts(pieces, dba, win[l], tm, D_MODEL, f"in_proj_dh_{l}", dep=dep)
        dx, g_nw = _rmsnorm_bwd(x_in, row(norm_w[l]), dh, dx, f"norm_bwd_{l}")
        grads["norm_w"][l] = g_nw.reshape(D_MODEL)
        if l > 0:
            tail = (d_final, loss_part) if l == depth - 1 else ()
            started[l] = _exchange_start([win_slots(g_win_l), wout_slots(g_wout_l), _pack_layer(grads, l, tail)],
                                         [True, True, False], f"exchange_{l}_start")
            token = started[l]["token"]

    def conv_slots(a):
        dd, r, cc = a.shape
        return jnp.transpose(a.reshape(dd, r, N_DEV, cc // N_DEV), (2, 0, 1, 3))

    small = _exchange_start(
        [conv_slots(jnp.stack(grads["lru_conv_w"])), conv_slots(jnp.stack(grads["dn_conv_w"])),
         _pack_layer(grads, 0, names=_PACK_LAST)], [True, True, False], "exchange_small_start")

    new = {}
    flat_in = (depth * D_MODEL, D_IN // N_DEV)
    flat_out = (depth * 2 * D_MODEL // N_DEV, D_MODEL)
    zero_row = jnp.zeros((1, HEAD_DIM), F32)

    def adamw_pack(parts, layer, names=_LAYERED, name="adamw_small"):
        tails = [(t, zero_row) if layer == depth - 1 else () for t in (final_norm_w, m_final_norm_w, v_final_norm_w)]
        return _adamw(parts, _pack_layer(weights, layer, tails[0], names), _pack_layer(mom_m, layer, tails[1], names),
                      _pack_layer(mom_v, layer, tails[2], names), f"{name}_{layer}", parts.shape[1])

    def adamw_w_in(parts, layer, into):
        return _adamw(parts, w_in.reshape(flat_in), m_w_in.reshape(flat_in), v_w_in.reshape(flat_in),
                      f"adamw_w_in_{layer}", 256, layer * D_MODEL, into)

    def adamw_w_out(parts, layer, into):
        return _adamw(parts, w_out.reshape(flat_out), m_w_out.reshape(flat_out), v_w_out.reshape(flat_out),
                      f"adamw_w_out_{layer}", 256, layer * flat_out[0] // depth, into)

    acc_in = acc_out = None
    packs = [None] * depth
    after = small["token_block"]
    for l in reversed(range(1, depth)):
        r_win, r_wout, r_pack = _exchange_wait(started[l], after, f"exchange_{l}_wait")
        acc_in = adamw_w_in(r_win, l, acc_in)
        acc_out = adamw_w_out(r_wout, l, acc_out)
        packs[l] = adamw_pack(r_pack, l)
        after = packs[l][0]
    (r_wout,) = _exchange_wait(started["w_out_0"], after, "exchange_w_out_0_wait")
    acc_out = adamw_w_out(r_wout, 0, acc_out)
    (r_win,) = _exchange_wait(started[0], acc_out[0], "exchange_0_wait")
    acc_in = adamw_w_in(r_win, 0, acc_in)
    (r_pack,) = _exchange_wait(started["pack_0"], acc_in[0], "exchange_pack_0_wait")
    packs[0] = adamw_pack(r_pack, 0, _PACK_LRU)
    r_lcw, r_dcw, r_last = _exchange_wait(small, packs[0][0], "exchange_small_wait")
    for name, parts in (("lru_conv_w", r_lcw), ("dn_conv_w", r_dcw)):
        w = weights[name]
        flat = (-1, w.shape[-1])
        outs = _adamw(parts.reshape((N_DEV,) + (w.size // w.shape[-1], w.shape[-1])), w.reshape(flat),
                      mom_m[name].reshape(flat), mom_v[name].reshape(flat), f"adamw_{name}", 8)
        new[name] = [a.reshape(w.shape) for a in outs]
    last_0 = adamw_pack(r_last, 0, _PACK_LAST, "adamw_last")
    new["w_in"] = [a.reshape(w_in.shape) for a in acc_in]
    new["w_out"] = [a.reshape(w_out.shape) for a in acc_out]
    for i in range(4):
        layers = [{**_unpack_layer(packs[0][i], weights, _PACK_LRU)[0],
                   **_unpack_layer(last_0[i], weights, _PACK_LAST)[0]}]
        layers += [_unpack_layer(packs[l][i], weights)[0] for l in range(1, depth)]
        for name in _LAYERED:
            new.setdefault(name, []).append(jnp.stack([layer[name] for layer in layers]))
    tail_at = _unpack_layer(packs[depth - 1][0], weights)[1]
    rows_final = D_MODEL // HEAD_DIM
    new["final_norm_w"] = [packs[depth - 1][i][tail_at:tail_at + rows_final].reshape(D_MODEL) for i in range(4)]
    loss = packs[depth - 1][0][tail_at + rows_final, 0]
    out = [loss, dx.reshape(x.shape)]
    for i in range(4):
        out += [new[name][i] for name in _WEIGHTS]
    return tuple(out)
```

```python
import functools

import jax
import jax.numpy as jnp
from jax import lax
from jax.experimental import pallas as pl
from jax.experimental.pallas import tpu as pltpu

F32 = jnp.float32
BF16 = jnp.bfloat16

N_DEV = 8
D_MODEL = 1024
HEADS = 8
HEAD_DIM = 128
CHUNK = 64
D_IN = 6160
D_IN_PAD = 6272
COL_LRU_X, COL_LRU_Z, COL_Q, COL_K, COL_V, COL_DN_Z, COL_BA = 0, 8, 16, 24, 32, 40, 48
LRU_C = 8.0
EPS = 1e-6
ADAM_LR, ADAM_B1, ADAM_B2, ADAM_EPS, ADAM_WD, ADAM_STEP = 0.001, 0.9, 0.999, 1e-08, 0.01, 10
TIME_BLOCK = 1024
DN_TIME_BLOCK = 128
DN_GROUP = 8
VMEM_LIMIT = 56 * 1024 * 1024

NN = (((1,), (0,)), ((), ()))
NT = (((1,), (1,)), ((), ()))
TN = (((0,), (0,)), ((), ()))


B_NN = (((2,), (1,)), ((0,), (0,)))
B_NT = (((2,), (2,)), ((0,), (0,)))
B_TN = (((1,), (1,)), ((0,), (0,)))


def _split_bf16(x):
    hi = x.astype(BF16)
    return hi, (x - hi.astype(F32)).astype(BF16)


def _dot(a, b, dims, prec):
    if prec == "bf16":
        return lax.dot_general(a.astype(BF16), b.astype(BF16), dims, preferred_element_type=F32)
    a1, a2 = _split_bf16(a)
    b1, b2 = _split_bf16(b)
    dg = functools.partial(lax.dot_general, dimension_numbers=dims, preferred_element_type=F32)
    return dg(a1, b1) + (dg(a1, b2) + dg(a2, b1))


def _make_mm(prec, nn_dims, nt_dims, tn_dims):
    @jax.custom_vjp
    def nn(a, b):
        return _dot(a, b, nn_dims, prec)

    @jax.custom_vjp
    def nt(a, b):
        return _dot(a, b, nt_dims, prec)

    @jax.custom_vjp
    def tn(a, b):
        return _dot(a, b, tn_dims, prec)

    nn.defvjp(lambda a, b: (_dot(a, b, nn_dims, prec), (a, b)),
              lambda r, g: (_dot(g, r[1], nt_dims, prec), _dot(r[0], g, tn_dims, prec)))
    nt.defvjp(lambda a, b: (_dot(a, b, nt_dims, prec), (a, b)),
              lambda r, g: (_dot(g, r[1], nn_dims, prec), _dot(g, r[0], tn_dims, prec)))
    tn.defvjp(lambda a, b: (_dot(a, b, tn_dims, prec), (a, b)),
              lambda r, g: (_dot(r[1], g, nt_dims, prec), _dot(r[0], g, nn_dims, prec)))
    return nn, nt, tn


_NN_B, _NT_B, _TN_B = _make_mm("bf16", NN, NT, TN)
_BNN, _BNT, _BTN = _make_mm("bf16", B_NN, B_NT, B_TN)


@jax.custom_vjp
def _unit_lower_inverse(a):
    n = a.shape[-1]
    eye = (lax.broadcasted_iota(jnp.int32, a.shape, 1) == lax.broadcasted_iota(jnp.int32, a.shape, 2)).astype(F32)
    dg = functools.partial(lax.dot_general, dimension_numbers=B_NN, preferred_element_type=F32)
    inv = eye - a
    pw = _dot(a, a, B_NN, "bf16x3")
    steps = n.bit_length() - 2
    for j in range(steps):
        i1, i2 = _split_bf16(inv)
        p1, p2 = _split_bf16(pw)
        square = j + 1 < steps
        by_hi = dg(jnp.concatenate([i1, i2, p1, p2] if square else [i1, i2], axis=1), p1)
        by_lo = dg(jnp.concatenate([i1, p1], axis=1) if square else i1, p2)
        inv = inv + (by_hi[:, :n] + (by_lo[:, :n] + by_hi[:, n:2 * n]))
        if square:
            pw = by_hi[:, 2 * n:3 * n] + (by_lo[:, n:] + by_hi[:, 3 * n:])
    return inv


def _uli_fwd(a):
    inv = _unit_lower_inverse(a)
    return inv, inv


def _uli_bwd(inv, g):
    return (-_dot(_dot(inv, g, B_TN, "bf16"), inv, B_NT, "bf16"),)


_unit_lower_inverse.defvjp(_uli_fwd, _uli_bwd)


@jax.custom_vjp
def _known_inverse(a, inv):
    return inv


_known_inverse.defvjp(lambda a, inv: (inv, inv), lambda inv, g: (_uli_bwd(inv, g)[0], jnp.zeros_like(inv)))


def _rows2(y, m):
    return y[:, :m], y[:, m:]


@jax.custom_vjp
def _pair_nn(x1, x2, r):
    return _rows2(_dot(jnp.concatenate([x1, x2], axis=1), r, B_NN, "bf16"), x1.shape[1])


def _pair_nn_bwd(res, g):
    x1, x2, r = res
    g = jnp.concatenate(g, axis=1)
    dx1, dx2 = _rows2(_dot(g, r, B_NT, "bf16"), x1.shape[1])
    return dx1, dx2, _dot(jnp.concatenate([x1, x2], axis=1), g, B_TN, "bf16")


_pair_nn.defvjp(lambda x1, x2, r: (_pair_nn(x1, x2, r), (x1, x2, r)), _pair_nn_bwd)


@jax.custom_vjp
def _pair_nt(x1, x2, r):
    return _rows2(_dot(jnp.concatenate([x1, x2], axis=1), r, B_NT, "bf16"), x1.shape[1])


def _pair_nt_bwd(res, g):
    x1, x2, r = res
    g = jnp.concatenate(g, axis=1)
    dx1, dx2 = _rows2(_dot(g, r, B_NN, "bf16"), x1.shape[1])
    return dx1, dx2, _dot(g, jnp.concatenate([x1, x2], axis=1), B_TN, "bf16")


_pair_nt.defvjp(lambda x1, x2, r: (_pair_nt(x1, x2, r), (x1, x2, r)), _pair_nt_bwd)


@jax.custom_vjp
def _wide_nn(l, r1, r2):
    y = _dot(l, jnp.concatenate([r1, r2], axis=2), B_NN, "bf16")
    return y[:, :, :r1.shape[2]], y[:, :, r1.shape[2]:]


def _wide_nn_bwd(res, g):
    l, r1, r2 = res
    g = jnp.concatenate(g, axis=2)
    dr = _dot(l, g, B_TN, "bf16")
    return (_dot(g, jnp.concatenate([r1, r2], axis=2), B_NT, "bf16"), dr[:, :, :r1.shape[2]], dr[:, :, r1.shape[2]:])


_wide_nn.defvjp(lambda l, r1, r2: (_wide_nn(l, r1, r2), (l, r1, r2)), _wide_nn_bwd)


def _lower_ones(batch, n):
    shape = (batch, n, n)
    return (lax.broadcasted_iota(jnp.int32, shape, 1) >= lax.broadcasted_iota(jnp.int32, shape, 2)).astype(BF16)


@jax.custom_vjp
def _chunk_cumsum(g):
    tri = _lower_ones(g.shape[0], g.shape[1])
    g1, g2 = _split_bf16(g)
    g3 = (g - g1.astype(F32) - g2.astype(F32)).astype(BF16)
    dg = functools.partial(lax.dot_general, dimension_numbers=B_NN, preferred_element_type=F32)
    return dg(tri, g1) + (dg(tri, g2) + dg(tri, g3))


def _chunk_cumsum_bwd(_, ct):
    tri = _lower_ones(ct.shape[0], ct.shape[1])
    c1, c2 = _split_bf16(ct)
    dg = functools.partial(lax.dot_general, dimension_numbers=B_TN, preferred_element_type=F32)
    return (dg(tri, c1) + dg(tri, c2),)


_chunk_cumsum.defvjp(lambda g: (_chunk_cumsum(g), None), _chunk_cumsum_bwd)


def _expm1(x):
    small = x * (1.0 + x * (0.5 + x * (1.0 / 6 + x * (1.0 / 24 + x * (1.0 / 120 + x * (1.0 / 720))))))
    return jnp.where(jnp.abs(x) < 0.2, small, jnp.exp(x) - 1.0)


def _sigmoid(x):
    return 0.5 * jnp.tanh(0.5 * x) + 0.5


def _silu(x):
    return x * _sigmoid(x)


def _softplus(x):
    return jnp.maximum(x, 0.0) + jnp.log(1.0 + jnp.exp(-jnp.abs(x)))


def _rmsnorm(x, w):
    return x * lax.rsqrt(jnp.mean(x * x, axis=-1, keepdims=True) + EPS) * w


def _gated_norm(o, z, w):
    return o * lax.rsqrt(jnp.mean(o * o, axis=-1, keepdims=True) + EPS) * w * _silu(z)


def _lru_gates(xc, wa, ba, wx, bx, lam):
    r = 1.0 / (1.0 + jnp.exp(-(_NN_B(xc, wa) + ba)))
    i = _sigmoid(_NN_B(xc, wx) + bx)
    log_a = -LRU_C * r * _softplus(-lam)
    a = jnp.exp(log_a)
    mult = jnp.sqrt(-_expm1(2.0 * log_a))
    return a, mult * (i * xc)


SCAN_ROWS = 32


def _scan_forward(a, b, h0):
    rows = a.shape[0]
    piece = min(SCAN_ROWS, rows)
    pos = lax.broadcasted_iota(jnp.int32, a.shape, 0) % piece
    k = 1
    while k < piece:
        seen = pos >= k
        b = jnp.where(seen, a * pltpu.roll(b, k, 0) + b, b)
        a = jnp.where(seen, a * pltpu.roll(a, k, 0), a)
        k *= 2
    out, entering = [], h0
    for lo in range(0, rows, piece):
        out.append(b[lo:lo + piece] + a[lo:lo + piece] * entering)
        entering = out[-1][piece - 1:piece, :]
    return jnp.concatenate(out, axis=0)


def _scan_reverse(a, d, carry):
    rows = a.shape[0]
    piece = min(SCAN_ROWS, rows)
    row = lax.broadcasted_iota(jnp.int32, a.shape, 0)
    pos = row % piece
    last = row == rows - 1
    c = jnp.where(last, 0.0, pltpu.roll(a, rows - 1, 0))
    d = d + jnp.where(last, carry, 0.0)
    k = 1
    while k < piece:
        seen = pos < piece - k
        d = jnp.where(seen, d + c * pltpu.roll(d, rows - k, 0), d)
        c = jnp.where(seen, c * pltpu.roll(c, rows - k, 0), c)
        k *= 2
    out, following = [], jnp.zeros_like(carry)
    for lo in reversed(range(0, rows, piece)):
        out.insert(0, d[lo:lo + piece] + c[lo:lo + piece] * following)
        following = out[0][0:1, :]
    return jnp.concatenate(out, axis=0)


def _lane_pick(row, lane_index):
    lane = lax.broadcasted_iota(jnp.int32, row.shape, 1)
    return jnp.sum(jnp.where(lane == lane_index, row, 0.0), axis=-1, keepdims=True)


def _dn_prep(qc, kc, vc, ba, a_log_row, dt_row, head):
    q = _silu(qc)
    k = _silu(kc)
    v = _silu(vc)
    q = q * lax.rsqrt(jnp.sum(q * q, axis=-1, keepdims=True) + EPS) * (HEAD_DIM ** -0.5)
    k = k * lax.rsqrt(jnp.sum(k * k, axis=-1, keepdims=True) + EPS)
    beta = _sigmoid(_lane_pick(ba, head))
    g = -jnp.exp(_lane_pick(a_log_row, head)) * _softplus(_lane_pick(ba, HEADS + head) + _lane_pick(dt_row, head))
    return q, k, v, g, beta


def _dn_chunks_head(q, k, v, gcol, bcol, inverse=None):
    n, c, d = q.shape
    row = lax.broadcasted_iota(jnp.int32, (n, c, c), 1)
    col = lax.broadcasted_iota(jnp.int32, (n, c, c), 2)
    g_wide = jnp.broadcast_to(gcol, (n, c, d))
    b_wide = jnp.broadcast_to(bcol, (n, c, d))
    gc = _chunk_cumsum(g_wide)
    gc_rows = gc[:, :, :c]
    decay = jnp.exp(jnp.where(row >= col, gc_rows - jnp.swapaxes(gc_rows, 1, 2), -1e30))
    kb = k * b_wide
    eg = jnp.exp(gc)
    kbk, qk = _pair_nt(kb, q, k)
    a = jnp.where(row > col, kbk * decay, 0.0)
    tinv = _unit_lower_inverse(a) if inverse is None else _known_inverse(a, inverse)
    u, w = _wide_nn(tinv, v * b_wide, kb * eg)
    g_last = jnp.sum(g_wide, axis=1, keepdims=True)
    return u, w, qk * decay, q * eg, k * jnp.exp(g_last - gc), jnp.exp(g_last), tinv


def _dn_chunks(inverse, q, k, v, gcol, bcol, states):
    u, w, attn, qe, kdec, eglast, _ = _dn_chunks_head(q, k, v, gcol, bcol, inverse)
    w_st, qe_st = _pair_nn(w, qe, states)
    v_new = u - w_st
    o = qe_st + _BNN(attn, v_new)
    return (o, states * eglast + _BTN(kdec, v_new)), (w, attn, qe, kdec, eglast)


def _conv_taps(buf, head, cw, rows):
    acc = cw[0:1, :] * buf[head, pl.ds(5, rows), :]
    for j in range(1, 4):
        acc = acc + cw[j:j + 1, :] * buf[head, pl.ds(5 + j, rows), :]
    return acc


def _conv_backward(dbuf, dhead, xbuf, xhead, cw, dxc, rows):
    dbuf[dhead, pl.ds(0, rows), :] = dxc
    dx = cw[0:1, :] * dbuf[dhead, pl.ds(3, rows), :]
    for j in range(1, 4):
        dx = dx + cw[j:j + 1, :] * dbuf[dhead, pl.ds(3 - j, rows), :]
    dcw = jnp.concatenate(
        [jnp.sum(dxc * xbuf[xhead, pl.ds(5 + j, rows), :], axis=0, keepdims=True) for j in range(4)], axis=0)
    dbuf[dhead, pl.ds(rows, 8), :] = dbuf[dhead, pl.ds(0, 8), :]
    return dx, dcw


def _params(**kw):
    return pltpu.CompilerParams(vmem_limit_bytes=VMEM_LIMIT, **kw)


def _matmul(a, b, form, tm, tn, tk, name, add=None, out_dtype=F32, dep=None):
    if form == "nn":
        (m, kdim), (_, n) = a.shape, b.shape
        a_spec = pl.BlockSpec((tm, tk), lambda j, i, k: (i, k))
        b_spec = pl.BlockSpec((tk, tn), lambda j, i, k: (k, j))
        dims = NN
    elif form == "nt":
        (m, kdim), (n, _) = a.shape, b.shape
        a_spec = pl.BlockSpec((tm, tk), lambda j, i, k: (i, k))
        b_spec = pl.BlockSpec((tn, tk), lambda j, i, k: (j, k))
        dims = NT
    else:
        (kdim, m), (_, n) = a.shape, b.shape
        a_spec = pl.BlockSpec((tk, tm), lambda j, i, k: (k, i))
        b_spec = pl.BlockSpec((tk, tn), lambda j, i, k: (k, j))
        dims = TN
    assert m % tm == 0 and n % tn == 0 and kdim % tk == 0, (name, m, n, kdim, tm, tn, tk)
    ksteps = kdim // tk
    o_spec = pl.BlockSpec((tm, tn), lambda j, i, k: (i, j))
    has_add = add is not None
    extra = [] if dep is None else [dep]

    def body(*refs):
        a_ref, b_ref = refs[:2]
        c_ref = refs[2] if has_add else None
        o_ref, acc = refs[-2:]
        k = pl.program_id(2)

        @pl.when(k == 0)
        def _():
            acc[...] = c_ref[...] if has_add else jnp.zeros_like(acc)

        acc[...] += lax.dot_general(a_ref[...].astype(BF16), b_ref[...].astype(BF16), dims,
                                    preferred_element_type=F32)

        @pl.when(k == ksteps - 1)
        def _():
            o_ref[...] = acc[...].astype(o_ref.dtype)

    in_specs = [a_spec, b_spec] + ([o_spec] if has_add else []) + [pl.BlockSpec((8, HEAD_DIM), lambda j, i, k: (0, 0))
                                                                   for _ in extra]
    args = (a, b) + ((add,) if has_add else ()) + tuple(extra)
    return pl.pallas_call(
        body, name=name, grid=(n // tn, m // tm, ksteps), in_specs=in_specs, out_specs=o_spec,
        out_shape=jax.ShapeDtypeStruct((m, n), out_dtype), scratch_shapes=[pltpu.VMEM((tm, tn), F32)],
        compiler_params=_params(dimension_semantics=("parallel", "parallel", "arbitrary")),
    )(*args)


def _matmul_nt_parts(parts, narrow, w, tm, tn, name, dep=None):
    m, c = parts[0].shape
    c2 = narrow.shape[1]
    n = w.shape[0]
    count = len(parts)
    assert m % tm == 0 and n % tn == 0 and all(p.shape == (m, c) for p in parts) and (count * c) % c2 == 0, (name, m, n)
    extra = [] if dep is None else [dep]

    def body(*refs):
        part_refs, narrow_ref, w_ref, w2_ref = refs[:count], refs[count], refs[count + 1], refs[count + 2]
        o_ref, acc = refs[-2:]
        k = pl.program_id(2)

        @pl.when(k == 0)
        def _():
            acc[...] = jnp.zeros_like(acc)

        for p in range(count):
            @pl.when(k == p)
            def _(p=p):
                acc[...] += lax.dot_general(part_refs[p][...].astype(BF16), w_ref[...].astype(BF16), NT,
                                            preferred_element_type=F32)

        @pl.when(k == count)
        def _():
            o_ref[...] = acc[...] + lax.dot_general(narrow_ref[...].astype(BF16), w2_ref[...].astype(BF16), NT,
                                                    preferred_element_type=F32)

    in_specs = ([pl.BlockSpec((tm, c), lambda j, i, k: (i, 0))] * count
                + [pl.BlockSpec((tm, c2), lambda j, i, k: (i, 0)),
                   pl.BlockSpec((tn, c), lambda j, i, k: (j, jnp.minimum(k, count - 1))),
                   pl.BlockSpec((tn, c2), lambda j, i, k: (j, count * c // c2))]
                + [pl.BlockSpec((8, HEAD_DIM), lambda j, i, k: (0, 0)) for _ in extra])
    return pl.pallas_call(
        body, name=name, grid=(n // tn, m // tm, count + 1), in_specs=in_specs,
        out_specs=pl.BlockSpec((tm, tn), lambda j, i, k: (i, j)),
        out_shape=jax.ShapeDtypeStruct((m, n), F32), scratch_shapes=[pltpu.VMEM((tm, tn), F32)],
        compiler_params=_params(dimension_semantics=("parallel", "parallel", "arbitrary")),
    )(*parts, narrow, w, w, *extra)


def _matmul_tn_parts(a, parts, tm, tn, tk, name, out_dtype):
    kdim, m = a.shape
    c = parts[0].shape[1]
    count = len(parts)
    per = c // tn
    assert m % tm == 0 and c % tn == 0 and kdim % tk == 0 and all(p.shape == (kdim, c) for p in parts), (name, m, c)
    ksteps = kdim // tk

    def body(*refs):
        a_ref, part_refs = refs[0], refs[1:1 + count]
        o_ref, acc = refs[-2:]
        j, k = pl.program_id(0), pl.program_id(2)

        @pl.when(k == 0)
        def _():
            acc[...] = jnp.zeros_like(acc)

        for p in range(count):
            @pl.when(j // per == p)
            def _(p=p):
                acc[...] += lax.dot_general(a_ref[...].astype(BF16), part_refs[p][...].astype(BF16), TN,
                                            preferred_element_type=F32)

        @pl.when(k == ksteps - 1)
        def _():
            o_ref[...] = acc[...].astype(o_ref.dtype)

    def part_spec(p):
        return pl.BlockSpec((tk, tn), lambda j, i, k: (jnp.where(j // per == p, k, 0), jnp.where(j // per == p, j % per, 0)))

    return pl.pallas_call(
        body, name=name, grid=(count * per, m // tm, ksteps),
        in_specs=[pl.BlockSpec((tk, tm), lambda j, i, k: (k, i))] + [part_spec(p) for p in range(count)],
        out_specs=pl.BlockSpec((tm, tn), lambda j, i, k: (i, j)),
        out_shape=jax.ShapeDtypeStruct((m, count * c), out_dtype), scratch_shapes=[pltpu.VMEM((tm, tn), F32)],
        compiler_params=_params(dimension_semantics=("parallel", "parallel", "arbitrary")),
    )(a, *parts)


def _rmsnorm_fwd(x, w_row, name):
    s = x.shape[0]
    tb = min(TIME_BLOCK, s)

    def body(x_ref, w_ref, o_ref):
        o_ref[...] = _rmsnorm(x_ref[...], w_ref[...]).astype(BF16)

    return pl.pallas_call(
        body, name=name, grid=(s // tb,),
        in_specs=[pl.BlockSpec((tb, D_MODEL), lambda i: (i, 0)), pl.BlockSpec((1, D_MODEL), lambda i: (0, 0))],
        out_specs=pl.BlockSpec((tb, D_MODEL), lambda i: (i, 0)),
        out_shape=jax.ShapeDtypeStruct((s, D_MODEL), BF16), compiler_params=_params(),
    )(x, w_row)


def _rmsnorm_bwd(x, w_row, dh, dres, name):
    s = x.shape[0]
    tb = min(TIME_BLOCK, s)

    def body(x_ref, w_ref, dh_ref, dres_ref, dx_ref, dw_ref):
        _, vjp = jax.vjp(_rmsnorm, x_ref[...], w_ref[...])
        dx, dw = vjp(dh_ref[...])
        dx_ref[...] = dres_ref[...] + dx

        @pl.when(pl.program_id(0) == 0)
        def _():
            dw_ref[...] = jnp.zeros_like(dw_ref)

        dw_ref[...] += dw

    row = pl.BlockSpec((tb, D_MODEL), lambda i: (i, 0))
    vec = pl.BlockSpec((1, D_MODEL), lambda i: (0, 0))
    return pl.pallas_call(
        body, name=name, grid=(s // tb,), in_specs=[row, vec, row, row], out_specs=[row, vec],
        out_shape=[jax.ShapeDtypeStruct((s, D_MODEL), F32), jax.ShapeDtypeStruct((1, D_MODEL), F32)],
        compiler_params=_params(),
    )(x, w_row, dh, dres)


def _final_loss(x, w_row, target, name):
    s = x.shape[0]
    tb = min(TIME_BLOCK, s)

    def loss_fn(xv, wv, tv):
        err = _rmsnorm(xv, wv) - tv
        return 0.5 * jnp.sum(jnp.sum(err * err, axis=-1, keepdims=True), axis=0, keepdims=True) * (1.0 / D_MODEL)

    def body(x_ref, w_ref, t_ref, loss_ref, dx_ref, dw_ref):
        tv = t_ref[...]
        loss, vjp = jax.vjp(lambda xv, wv: loss_fn(xv, wv, tv), x_ref[...], w_ref[...])
        dx, dw = vjp(jnp.ones((1, 1), F32))
        dx_ref[...] = dx

        @pl.when(pl.program_id(0) == 0)
        def _():
            dw_ref[...] = jnp.zeros_like(dw_ref)
            loss_ref[...] = jnp.zeros_like(loss_ref)

        dw_ref[...] += dw
        loss_ref[...] += jnp.broadcast_to(loss, loss_ref.shape)

    row = pl.BlockSpec((tb, D_MODEL), lambda i: (i, 0))
    vec = pl.BlockSpec((1, D_MODEL), lambda i: (0, 0))
    return pl.pallas_call(
        body, name=name, grid=(s // tb,), in_specs=[row, vec, row],
        out_specs=[pl.BlockSpec((1, HEAD_DIM), lambda i: (0, 0)), row, vec],
        out_shape=[jax.ShapeDtypeStruct((1, HEAD_DIM), F32), jax.ShapeDtypeStruct((s, D_MODEL), F32),
                   jax.ShapeDtypeStruct((1, D_MODEL), F32)],
        compiler_params=_params(),
    )(x, w_row, target)


def _head_specs(tb, time_of):
    def col(off):
        return pl.BlockSpec((tb, HEAD_DIM), lambda t, h: (time_of(t), off + h))
    return col


def _vec_spec():
    return pl.BlockSpec((1, HEAD_DIM), lambda t, h: (0, h))


def _lru_fwd(proj, conv_w, conv_b, wa, ba, wx, bx, lam, nw, name):
    s = proj.shape[0]
    tb = min(TIME_BLOCK, s)
    nt = s // tb
    col = _head_specs(tb, lambda t: t)

    def body(x_ref, z_ref, cw_ref, cb_ref, wa_ref, ba_ref, wx_ref, bx_ref, lam_ref, nw_ref,
             y_ref, hs_ref, xbuf, hcar):
        t, h = pl.program_id(0), pl.program_id(1)

        @pl.when(t == 0)
        def _():
            xbuf[h, pl.ds(0, 8), :] = jnp.zeros((8, HEAD_DIM), F32)
            hcar[h] = jnp.zeros((8, HEAD_DIM), F32)

        xbuf[h, pl.ds(8, tb), :] = x_ref[...]
        xc = _conv_taps(xbuf, h, cw_ref[...], tb) + cb_ref[...]
        a, b = _lru_gates(xc, wa_ref[...], ba_ref[...], wx_ref[...], bx_ref[...], lam_ref[...])
        hs_ref[...] = _scan_forward(a, b, hcar[h, pl.ds(0, 1), :])
        hcar[h, pl.ds(0, 1), :] = hs_ref[pl.ds(tb - 1, 1), :]
        xbuf[h, pl.ds(0, 8), :] = xbuf[h, pl.ds(tb, 8), :]
        y_ref[...] = _gated_norm(hs_ref[...], z_ref[...], nw_ref[...]).astype(BF16)

    vec = _vec_spec()
    return pl.pallas_call(
        body, name=name, grid=(nt, HEADS),
        in_specs=[col(COL_LRU_X), col(COL_LRU_Z), pl.BlockSpec((4, HEAD_DIM), lambda t, h: (0, h)), vec,
                  pl.BlockSpec((None, HEAD_DIM, HEAD_DIM), lambda t, h: (h, 0, 0)), vec,
                  pl.BlockSpec((None, HEAD_DIM, HEAD_DIM), lambda t, h: (h, 0, 0)), vec, vec, vec],
        out_specs=[col(0), col(0)],
        out_shape=[jax.ShapeDtypeStruct((s, 2 * D_MODEL), BF16), jax.ShapeDtypeStruct((s, D_MODEL), F32)],
        scratch_shapes=[pltpu.VMEM((HEADS, tb + 8, HEAD_DIM), F32), pltpu.VMEM((HEADS, 8, HEAD_DIM), F32)],
        compiler_params=_params(dimension_semantics=("arbitrary", "arbitrary")),
    )(proj, proj, conv_w, conv_b, wa, ba, wx, bx, lam, nw)


def _halo_spec(tb, nt, off):
    per = tb // 8
    return pl.BlockSpec((8, HEAD_DIM), lambda t, h: (jnp.maximum((nt - 1 - t) * per - 1, 0), off + h))


def _lru_bwd(proj, hs, dy, conv_w, conv_b, wa, ba, wx, bx, lam, nw, name):
    s = proj.shape[0]
    tb = min(TIME_BLOCK, s)
    nt = s // tb
    col = _head_specs(tb, lambda t: nt - 1 - t)

    def body(x_ref, xh_ref, z_ref, hs_ref, hh_ref, dy_ref, cw_ref, cb_ref, wa_ref, ba_ref, wx_ref, bx_ref,
             lam_ref, nw_ref, dx_ref, dz_ref, dcw_ref, dcb_ref, dwa_ref, dba_ref, dwx_ref, dbx_ref, dlam_ref,
             dnw_ref, xbuf, hbuf, dbuf, gcar):
        t, h = pl.program_id(0), pl.program_id(1)
        first_block = t == nt - 1

        @pl.when(t == 0)
        def _():
            dbuf[h, pl.ds(tb, 8), :] = jnp.zeros((8, HEAD_DIM), F32)
            gcar[h] = jnp.zeros((8, HEAD_DIM), F32)
            dcw_ref[h] = jnp.zeros((4, HEAD_DIM), F32)
            dwa_ref[h] = jnp.zeros((HEAD_DIM, HEAD_DIM), F32)
            dwx_ref[h] = jnp.zeros((HEAD_DIM, HEAD_DIM), F32)
            for ref in (dcb_ref, dba_ref, dbx_ref, dlam_ref, dnw_ref):
                ref[h] = jnp.zeros((1, HEAD_DIM), F32)

        keep = jnp.where(first_block, 0.0, 1.0)
        xbuf[0, pl.ds(0, 8), :] = xh_ref[...] * keep
        xbuf[0, pl.ds(8, tb), :] = x_ref[...]
        hbuf[pl.ds(0, 8), :] = hh_ref[...] * keep
        hbuf[pl.ds(8, tb), :] = hs_ref[...]
        cw = cw_ref[...]
        xc = _conv_taps(xbuf, 0, cw, tb) + cb_ref[...]
        (a, _), gates_vjp = jax.vjp(_lru_gates, xc, wa_ref[...], ba_ref[...], wx_ref[...], bx_ref[...], lam_ref[...])
        _, norm_vjp = jax.vjp(_gated_norm, hs_ref[...], z_ref[...], nw_ref[...])
        dh, dz, dnw = norm_vjp(dy_ref[...])
        dz_ref[...] = dz.astype(dz_ref.dtype)
        g = _scan_reverse(a, dh, gcar[h, pl.ds(0, 1), :])
        gcar[h, pl.ds(0, 1), :] = a[0:1, :] * g[0:1, :]
        dxc, dwa, dba, dwx, dbx, dlam = gates_vjp((g * hbuf[pl.ds(7, tb), :], g))
        dx, dcw = _conv_backward(dbuf, h, xbuf, 0, cw, dxc, tb)
        dx_ref[...] = dx.astype(dx_ref.dtype)
        dcw_ref[h] += dcw
        dcb_ref[h] += jnp.sum(dxc, axis=0, keepdims=True)
        dwa_ref[h] += dwa
        dwx_ref[h] += dwx
        dba_ref[h] += dba
        dbx_ref[h] += dbx
        dlam_ref[h] += dlam
        dnw_ref[h] += dnw

    vec = _vec_spec()
    mat = pl.BlockSpec((None, HEAD_DIM, HEAD_DIM), lambda t, h: (h, 0, 0))

    def whole(shape):
        return pl.BlockSpec(shape, lambda t, h: (0,) * len(shape))

    head_vec = jax.ShapeDtypeStruct((HEADS, 1, HEAD_DIM), F32)
    head_mat = jax.ShapeDtypeStruct((HEADS, HEAD_DIM, HEAD_DIM), F32)
    return pl.pallas_call(
        body, name=name, grid=(nt, HEADS),
        in_specs=[col(COL_LRU_X), _halo_spec(tb, nt, COL_LRU_X), col(COL_LRU_Z), col(0), _halo_spec(tb, nt, 0), col(0),
                  pl.BlockSpec((4, HEAD_DIM), lambda t, h: (0, h)), vec, mat, vec, mat, vec, vec, vec],
        out_specs=[col(0), col(0), whole((HEADS, 4, HEAD_DIM)), whole((HEADS, 1, HEAD_DIM)),
                   whole((HEADS, HEAD_DIM, HEAD_DIM)), whole((HEADS, 1, HEAD_DIM)),
                   whole((HEADS, HEAD_DIM, HEAD_DIM)), whole((HEADS, 1, HEAD_DIM)), whole((HEADS, 1, HEAD_DIM)),
                   whole((HEADS, 1, HEAD_DIM))],
        out_shape=[jax.ShapeDtypeStruct((s, D_MODEL), BF16), jax.ShapeDtypeStruct((s, D_MODEL), BF16),
                   jax.ShapeDtypeStruct((HEADS, 4, HEAD_DIM), F32), head_vec, head_mat, head_vec, head_mat, head_vec,
                   head_vec, head_vec],
        scratch_shapes=[pltpu.VMEM((1, tb + 8, HEAD_DIM), F32), pltpu.VMEM((tb + 8, HEAD_DIM), F32),
                        pltpu.VMEM((HEADS, tb + 8, HEAD_DIM), F32), pltpu.VMEM((HEADS, 8, HEAD_DIM), F32)],
        compiler_params=_params(dimension_semantics=("arbitrary", "arbitrary")),
    )(proj, proj, proj, hs, hs, dy, conv_w, conv_b, wa, ba, wx, bx, lam, nw)


def _group_col(tb, time_of):
    def col(off):
        return pl.BlockSpec((tb, DN_GROUP * HEAD_DIM), lambda t, hg: (time_of(t), off // DN_GROUP + hg))
    return col


def _dn_fwd(proj, y, conv_w, a_log_row, dt_row, nw, name):
    s = proj.shape[0]
    tb = min(DN_TIME_BLOCK, s)
    nt = s // tb
    nchunk = tb // CHUNK
    grp = DN_GROUP
    col = _group_col(tb, lambda t: t)

    def body(q_ref, k_ref, v_ref, z_ref, ba_ref, cwq_ref, cwk_ref, cwv_ref, al_ref, dt_ref, nw_ref, y_in_ref,
             y_ref, o_ref, st_ref, inv_ref, xbuf, state):
        t, hg = pl.program_id(0), pl.program_id(1)

        def chunks(a):
            return a.reshape(nchunk, CHUNK, a.shape[-1])

        prepared = []
        for gi in range(grp):
            h = hg * grp + gi
            lanes = slice(gi * HEAD_DIM, (gi + 1) * HEAD_DIM)

            @pl.when(t == 0)
            def _(h=h):
                for i in range(3):
                    xbuf[3 * h + i, pl.ds(0, 8), :] = jnp.zeros((8, HEAD_DIM), F32)
                state[h] = jnp.zeros((HEAD_DIM, HEAD_DIM), F32)

            conv = []
            for i, (ref, cw_ref) in enumerate(((q_ref, cwq_ref), (k_ref, cwk_ref), (v_ref, cwv_ref))):
                xbuf[3 * h + i, pl.ds(8, tb), :] = ref[:, lanes]
                conv.append(_conv_taps(xbuf, 3 * h + i, cw_ref[:, lanes], tb))
                xbuf[3 * h + i, pl.ds(0, 8), :] = xbuf[3 * h + i, pl.ds(tb, 8), :]
            prepared.append([chunks(a) for a in
                             _dn_prep(conv[0], conv[1], conv[2], ba_ref[...], al_ref[...], dt_ref[...], h)])
        qs, ks, vs, gs, bs = [jnp.concatenate([p[i] for p in prepared], axis=0) for i in range(5)]
        u, w, attn, qe, kdec, eglast, tinv = _dn_chunks_head(qs, ks, vs, gs, bs)
        inv_ref[...] = tinv.reshape(grp, nchunk, CHUNK, CHUNK)
        w_u = jnp.concatenate([w, u], axis=2)
        kdec_w_u = _dot(kdec, w_u, B_TN, "bf16")
        attn_w_u = _dot(attn, w_u, B_NN, "bf16")
        st = [state[hg * grp + gi] for gi in range(grp)]
        for c in range(nchunk):
            for gi in range(grp):
                n = gi * nchunk + c
                st_ref[gi, c] = st[gi]
                st[gi] = st[gi] * eglast[n] - _NN_B(kdec_w_u[n, :, :HEAD_DIM], st[gi]) + kdec_w_u[n, :, HEAD_DIM:]
        for gi in range(grp):
            state[hg * grp + gi] = st[gi]
        states = st_ref[...].reshape(grp * nchunk, HEAD_DIM, HEAD_DIM)
        o = _dot(qe - attn_w_u[:, :, :HEAD_DIM], states, B_NN, "bf16") + attn_w_u[:, :, HEAD_DIM:]
        for gi in range(grp):
            lanes = slice(gi * HEAD_DIM, (gi + 1) * HEAD_DIM)
            o_head = o[gi * nchunk:(gi + 1) * nchunk].reshape(tb, HEAD_DIM)
            o_ref[:, lanes] = o_head
            y_ref[:, lanes] = _gated_norm(o_head, z_ref[:, lanes], nw_ref[...]).astype(BF16)

    def cw_spec(off):
        return pl.BlockSpec((4, grp * HEAD_DIM), lambda t, hg: (0, off // grp + hg))

    row128 = pl.BlockSpec((1, HEAD_DIM), lambda t, hg: (0, 0))
    return pl.pallas_call(
        body, name=name, grid=(nt, HEADS // grp),
        in_specs=[col(COL_Q), col(COL_K), col(COL_V), col(COL_DN_Z),
                  pl.BlockSpec((tb, HEAD_DIM), lambda t, hg: (t, COL_BA)),
                  cw_spec(0), cw_spec(HEADS), cw_spec(2 * HEADS), row128, row128, row128,
                  pl.BlockSpec(memory_space=pl.ANY)],
        out_specs=[col(HEADS), col(0),
                   pl.BlockSpec((grp, nchunk, HEAD_DIM, HEAD_DIM), lambda t, hg: (hg, t, 0, 0)),
                   pl.BlockSpec((grp, nchunk, CHUNK, CHUNK), lambda t, hg: (hg, t, 0, 0))],
        out_shape=[jax.ShapeDtypeStruct((s, 2 * D_MODEL), BF16), jax.ShapeDtypeStruct((s, D_MODEL), F32),
                   jax.ShapeDtypeStruct((HEADS, s // CHUNK, HEAD_DIM, HEAD_DIM), F32),
                   jax.ShapeDtypeStruct((HEADS, s // CHUNK, CHUNK, CHUNK), F32)],
        input_output_aliases={11: 0},
        scratch_shapes=[pltpu.VMEM((3 * HEADS, tb + 8, HEAD_DIM), F32), pltpu.VMEM((HEADS, HEAD_DIM, HEAD_DIM), F32)],
        compiler_params=_params(dimension_semantics=("arbitrary", "arbitrary")),
    )(proj, proj, proj, proj, proj, conv_w, conv_w, conv_w, a_log_row, dt_row, nw, y)


def _dn_bwd(proj, o, states, inverses, dy, conv_w, a_log_row, dt_row, nw, name):
    s = proj.shape[0]
    tb = min(DN_TIME_BLOCK, s)
    nt = s // tb
    nchunk = tb // CHUNK
    grp = DN_GROUP
    col = _group_col(tb, lambda t: nt - 1 - t)

    def body(q_ref, qh_ref, k_ref, kh_ref, v_ref, vh_ref, z_ref, ba_ref, o_ref, st_ref, inv_ref, dy_ref,
             cwq_ref, cwk_ref, cwv_ref, al_ref, dt_ref, nw_ref,
             dq_ref, dk_ref, dv_ref, dz_ref, dba_ref, dcw_ref, dal_ref, ddt_ref, dnw_ref,
             xbuf, dbuf, dstate, dst_s):
        t, hg = pl.program_id(0), pl.program_id(1)
        keep = jnp.where(t == nt - 1, 0.0, 1.0)

        @pl.when((t == 0) & (hg == 0))
        def _():
            for ref in (dal_ref, ddt_ref, dnw_ref):
                ref[...] = jnp.zeros_like(ref)

        def chunks(a):
            return a.reshape(nchunk, CHUNK, a.shape[-1])

        prepared, prep_vjps, dos = [], [], []
        for gi in range(grp):
            h = hg * grp + gi
            lanes = slice(gi * HEAD_DIM, (gi + 1) * HEAD_DIM)

            @pl.when(t == 0)
            def _(h=h):
                for i in range(3):
                    dbuf[3 * h + i, pl.ds(tb, 8), :] = jnp.zeros((8, HEAD_DIM), F32)
                    dcw_ref[3 * h + i] = jnp.zeros((4, HEAD_DIM), F32)
                dstate[h] = jnp.zeros((HEAD_DIM, HEAD_DIM), F32)

            conv = []
            for i, (ref, halo, cw_ref) in enumerate(((q_ref, qh_ref, cwq_ref), (k_ref, kh_ref, cwk_ref),
                                                     (v_ref, vh_ref, cwv_ref))):
                xbuf[3 * gi + i, pl.ds(0, 8), :] = halo[:, lanes] * keep
                xbuf[3 * gi + i, pl.ds(8, tb), :] = ref[:, lanes]
                conv.append(_conv_taps(xbuf, 3 * gi + i, cw_ref[:, lanes], tb))
            outs, prep_vjp = jax.vjp(
                lambda qc, kc, vc, ba, al, dt, h=h: _dn_prep(qc, kc, vc, ba, al, dt, h),
                conv[0], conv[1], conv[2], ba_ref[...], al_ref[...], dt_ref[...])
            prepared.append([chunks(a) for a in outs])
            prep_vjps.append(prep_vjp)
            _, norm_vjp = jax.vjp(_gated_norm, o_ref[:, lanes], z_ref[:, lanes], nw_ref[...])
            do, dz, dnw = norm_vjp(dy_ref[:, lanes])
            dz_ref[:, lanes] = dz.astype(dz_ref.dtype)
            dnw_ref[...] += dnw
            dos.append(chunks(do))
        qs, ks, vs, gs, bs = [jnp.concatenate([p[i] for p in prepared], axis=0) for i in range(5)]
        do = jnp.concatenate(dos, axis=0)
        states_in = st_ref[...].reshape(grp * nchunk, HEAD_DIM, HEAD_DIM)
        kept = inv_ref[...].reshape(grp * nchunk, CHUNK, CHUNK)
        _, chunks_vjp, (w, attn, qe, kdec, eglast) = jax.vjp(
            functools.partial(_dn_chunks, kept), qs, ks, vs, gs, bs, states_in, has_aux=True)
        kdec_w = _dot(kdec, w, B_TN, "bf16")
        fixed = _dot(qe, do, B_TN, "bf16") - _dot(w, _dot(attn, do, B_TN, "bf16"), B_TN, "bf16")
        dst = [dstate[hg * grp + gi] for gi in range(grp)]
        for c in reversed(range(nchunk)):
            for gi in range(grp):
                n = gi * nchunk + c
                dst_s[n] = dst[gi]
                dst[gi] = dst[gi] * eglast[n] - _dot(kdec_w[n], dst[gi], TN, "bf16") + fixed[n]
        for gi in range(grp):
            dstate[hg * grp + gi] = dst[gi]
        cts = chunks_vjp((do, dst_s[...]))[:5]

        dba_sum = None
        for gi in range(grp):
            h = hg * grp + gi
            lanes = slice(gi * HEAD_DIM, (gi + 1) * HEAD_DIM)
            per_head = [ct[gi * nchunk:(gi + 1) * nchunk].reshape(tb, ct.shape[-1]) for ct in cts]
            dqc, dkc, dvc, dba, dal, ddt = prep_vjps[gi](tuple(per_head))
            for i, (dxc, out, cw_ref) in enumerate(((dqc, dq_ref, cwq_ref), (dkc, dk_ref, cwk_ref),
                                                    (dvc, dv_ref, cwv_ref))):
                dx, dcw = _conv_backward(dbuf, 3 * h + i, xbuf, 3 * gi + i, cw_ref[:, lanes], dxc, tb)
                out[:, lanes] = dx.astype(out.dtype)
                dcw_ref[3 * h + i] += dcw
            dal_ref[...] += dal
            ddt_ref[...] += ddt
            dba_sum = dba if dba_sum is None else dba_sum + dba

        @pl.when(hg == 0)
        def _():
            dba_ref[...] = dba_sum.astype(dba_ref.dtype)

        @pl.when(hg > 0)
        def _():
            dba_ref[...] += dba_sum.astype(dba_ref.dtype)

    def cw_spec(off):
        return pl.BlockSpec((4, grp * HEAD_DIM), lambda t, hg: (0, off // grp + hg))

    def halo(off):
        per = tb // 8
        return pl.BlockSpec((8, grp * HEAD_DIM),
                            lambda t, hg: (jnp.maximum((nt - 1 - t) * per - 1, 0), off // grp + hg))

    def whole(shape):
        return pl.BlockSpec(shape, lambda t, hg: (0,) * len(shape))

    row128 = whole((1, HEAD_DIM))
    blk = (tb, HEAD_DIM)
    act = jax.ShapeDtypeStruct((s, D_MODEL), BF16)
    row_out = jax.ShapeDtypeStruct((1, HEAD_DIM), F32)
    return pl.pallas_call(
        body, name=name, grid=(nt, HEADS // grp),
        in_specs=[col(COL_Q), halo(COL_Q), col(COL_K), halo(COL_K), col(COL_V), halo(COL_V), col(COL_DN_Z),
                  pl.BlockSpec(blk, lambda t, hg: (nt - 1 - t, COL_BA)), col(0),
                  pl.BlockSpec((grp, nchunk, HEAD_DIM, HEAD_DIM), lambda t, hg: (hg, nt - 1 - t, 0, 0)),
                  pl.BlockSpec((grp, nchunk, CHUNK, CHUNK), lambda t, hg: (hg, nt - 1 - t, 0, 0)), col(HEADS),
                  cw_spec(0), cw_spec(HEADS), cw_spec(2 * HEADS), row128, row128, row128],
        out_specs=[col(0), col(0), col(0), col(0), pl.BlockSpec(blk, lambda t, hg: (nt - 1 - t, 0)),
                   whole((3 * HEADS, 4, HEAD_DIM)), row128, row128, row128],
        out_shape=[act, act, act, act, jax.ShapeDtypeStruct((s, HEAD_DIM), F32),
                   jax.ShapeDtypeStruct((3 * HEADS, 4, HEAD_DIM), F32), row_out, row_out, row_out],
        scratch_shapes=[pltpu.VMEM((3 * grp, tb + 8, HEAD_DIM), F32), pltpu.VMEM((3 * HEADS, tb + 8, HEAD_DIM), F32),
                        pltpu.VMEM((HEADS, HEAD_DIM, HEAD_DIM), F32),
                        pltpu.VMEM((grp * nchunk, HEAD_DIM, HEAD_DIM), F32)],
        compiler_params=_params(dimension_semantics=("arbitrary", "arbitrary")),
    )(proj, proj, proj, proj, proj, proj, proj, proj, o, states, inverses, dy, conv_w, conv_w, conv_w, a_log_row, dt_row,
      nw)


def _mesh_position():
    x, y, c = lax.axis_index("x"), lax.axis_index("y"), lax.axis_index("c")
    return x, y, c, 4 * x + 2 * y + c


def _peer(k, x, y, c):
    px = 1 - x if k & 4 else x
    py = 1 - y if k & 2 else y
    pc = 1 - c if k & 1 else c
    return (px, py, pc), 4 * px + 2 * py + pc


def _exchange_copies(ins, lands, scatter, send_sems, recv_sems, receives=True):
    x, y, c, me = _mesh_position()
    sends, recvs = [], []
    for i, (src, land) in enumerate(zip(ins, lands)):
        for k in range(1, N_DEV):
            peer, peer_id = _peer(k, x, y, c)
            sem = i * (N_DEV - 1) + k - 1
            for dst, out in ((me, sends), (peer_id, recvs)) if receives else ((me, sends),):
                out.append(pltpu.make_async_remote_copy(
                    src_ref=src.at[peer_id] if scatter[i] else src, dst_ref=land.at[dst],
                    send_sem=send_sems.at[sem], recv_sem=recv_sems.at[sem],
                    device_id=peer, device_id_type=pl.DeviceIdType.MESH))
    return sends, recvs


def _landing_shape(a, scatter):
    return a.shape if scatter else (N_DEV,) + a.shape


def _two_level_gather(arrays, name):
    n = len(arrays)
    per = N_DEV - 1

    def body(*refs):
        ins, outs = refs[:n], refs[n:2 * n]
        send_sems, recv_sems, local_sems = refs[2 * n:]
        x, y, c, me = _mesh_position()
        sibling = (x, y, 1 - c)
        chips = [(1 - x, y), (x, 1 - y), (1 - x, 1 - y)]

        def copy(i, k, block, to, src=None):
            slot = outs[i].at[4 * block[0] + 2 * block[1] + block[2]]
            return pltpu.make_async_remote_copy(
                src_ref=slot if src is None else src, dst_ref=slot,
                send_sem=send_sems.at[i * per + k], recv_sem=recv_sems.at[i * per + k],
                device_id=to, device_id_type=pl.DeviceIdType.MESH)

        local = [pltpu.make_async_copy(ins[i], outs[i].at[me], local_sems.at[i]) for i in range(n)]
        first = []
        for i in range(n):
            first.append(copy(i, 0, (x, y, c), sibling, src=ins[i]))
            first += [copy(i, 1 + j, (x, y, c), (*chip, c), src=ins[i]) for j, chip in enumerate(chips)]
        for cp in local + first:
            cp.start()
        passed = []
        for i in range(n):
            for j, chip in enumerate(chips):
                copy(i, 1 + j, (*chip, c), (x, y, c)).wait_recv()
                passed.append(copy(i, 4 + j, (*chip, c), sibling))
                passed[-1].start()
        for i in range(n):
            copy(i, 0, sibling, (x, y, c)).wait_recv()
            for j, chip in enumerate(chips):
                copy(i, 4 + j, (*chip, 1 - c), (x, y, c)).wait_recv()
        for cp in first + passed:
            cp.wait_send()
        for cp in local:
            cp.wait()

    hbm = pl.BlockSpec(memory_space=pl.ANY)
    return pl.pallas_call(
        body, name=name, in_specs=[hbm] * n, out_specs=[hbm] * n,
        out_shape=[jax.ShapeDtypeStruct((N_DEV,) + a.shape, a.dtype) for a in arrays],
        scratch_shapes=[pltpu.SemaphoreType.DMA((n * per,)), pltpu.SemaphoreType.DMA((n * per,)),
                        pltpu.SemaphoreType.DMA((n,))],
    )(*arrays)


_HBM = pl.BlockSpec(memory_space=pltpu.HBM)
_SEM = pl.BlockSpec(memory_space=pltpu.SEMAPHORE)
_DATAFLOW = pltpu.SideEffectType.DATAFLOW_SIDE_EFFECTING


def _exchange_start(arrays, scatter, name):
    n = len(arrays)
    srcs = [pltpu.with_memory_space_constraint(a, pltpu.HBM) for a in arrays]
    lands = [pltpu.with_memory_space_constraint(lax.empty(_landing_shape(a, sc), a.dtype), pltpu.HBM)
             for a, sc in zip(arrays, scatter)]
    nsem = n * (N_DEV - 1)

    def body(*refs):
        ins, zones = refs[:n], refs[n:2 * n]
        send_sems, recv_sems = refs[2 * n], refs[2 * n + 1]
        token = refs[-1]
        sends, _ = _exchange_copies(ins, zones, scatter, send_sems, recv_sems, receives=False)
        for cp in sends:
            cp.start()
        token[...] = jnp.zeros_like(token)

    res = pl.pallas_call(
        body, name=name,
        out_shape=(pltpu.SemaphoreType.DMA((nsem,)), pltpu.SemaphoreType.DMA((nsem,)),
                   *[pltpu.HBM(a.shape, a.dtype) for a in srcs + lands], jax.ShapeDtypeStruct((8, HEAD_DIM), F32)),
        in_specs=[_HBM] * (2 * n),
        out_specs=(_SEM, _SEM, *[_HBM] * (2 * n), pl.BlockSpec(memory_space=pltpu.VMEM)),
        input_output_aliases={i: 2 + i for i in range(2 * n)},
        compiler_params=pltpu.CompilerParams(has_side_effects=_DATAFLOW),
    )(*srcs, *lands)
    return dict(sems=res[:2], srcs=res[2:2 + n], lands=res[2 + n:2 + 2 * n], token_block=res[-1],
                token=res[-1][0, 0], scatter=scatter)


def _exchange_wait(started, after, name):
    scatter = started["scatter"]
    n = len(scatter)

    def body(*refs):
        ins, zones = refs[:n], refs[n:2 * n]
        send_sems, recv_sems = refs[2 * n], refs[2 * n + 1]
        sends, recvs = _exchange_copies(ins, zones, scatter, send_sems, recv_sems)
        for cp in sends:
            cp.wait_send()
        for cp in recvs:
            cp.wait_recv()

    thru = list(started["srcs"]) + list(started["lands"])
    res = pl.pallas_call(
        body, name=name, out_shape=[pltpu.HBM(a.shape, a.dtype) for a in thru],
        in_specs=[_HBM] * (2 * n) + [_SEM, _SEM, pl.BlockSpec(memory_space=pl.ANY)], out_specs=[_HBM] * (2 * n),
        input_output_aliases={i: i for i in range(2 * n)},
        compiler_params=pltpu.CompilerParams(has_side_effects=_DATAFLOW),
    )(*thru, *started["sems"], after)
    me = 4 * lax.axis_index("x") + 2 * lax.axis_index("y") + lax.axis_index("c")
    out = []
    for src, got, sc in zip(res[:n], res[n:], scatter):
        own = lax.dynamic_index_in_dim(src, me, 0, keepdims=False) if sc else src
        out.append(lax.dynamic_update_index_in_dim(got, own, me, 0))
    return out


def _adamw(parts, w, m, v, name, rows_per_step, row_offset=0, into=None):
    rows, cols = parts.shape[1:]
    tr = min(rows_per_step, rows)
    assert rows % tr == 0 and row_offset % tr == 0, (name, rows, tr, row_offset)
    first = row_offset // tr
    c1 = 1.0 / (1.0 - ADAM_B1 ** ADAM_STEP)
    c2 = 1.0 / (1.0 - ADAM_B2 ** ADAM_STEP)

    def body(p_ref, w_ref, m_ref, v_ref, *rest):
        g_ref, d_ref, nm_ref, nv_ref = rest[-4:]
        g = p_ref[0].astype(F32)
        for d in range(1, N_DEV):
            g = g + p_ref[d].astype(F32)
        nm = ADAM_B1 * m_ref[...] + (1.0 - ADAM_B1) * g
        nv = ADAM_B2 * v_ref[...] + (1.0 - ADAM_B2) * (g * g)
        g_ref[...] = g
        nm_ref[...] = nm
        nv_ref[...] = nv
        d_ref[...] = -ADAM_LR * ((nm * c1) / (jnp.sqrt(nv * c2) + ADAM_EPS) + ADAM_WD * w_ref[...])

    blk = pl.BlockSpec((tr, cols), lambda i: (i + first, 0))
    shape = jax.ShapeDtypeStruct(w.shape, F32)
    prior = [] if into is None else list(into)
    return pl.pallas_call(
        body, name=name, grid=(rows // tr,),
        in_specs=[pl.BlockSpec((N_DEV, tr, cols), lambda i: (0, i, 0)), blk, blk, blk]
        + [pl.BlockSpec(memory_space=pl.ANY)] * len(prior),
        out_specs=[blk] * 4, out_shape=[shape] * 4,
        input_output_aliases={4 + j: j for j in range(len(prior))}, compiler_params=_params(),
    )(parts, w, m, v, *prior)


_LAYERED = ("norm_w", "lru_conv_b", "lru_wa", "lru_ba", "lru_wx", "lru_bx", "lru_lambda", "lru_norm_w",
            "dn_A_log", "dn_dt_bias", "dn_norm_w")
_PACK_LRU = _LAYERED[1:8]
_PACK_LAST = _LAYERED[:1] + _LAYERED[8:]
_WEIGHTS = ("norm_w", "w_in", "lru_conv_w", "lru_conv_b", "lru_wa", "lru_ba", "lru_wx", "lru_bx", "lru_lambda",
            "lru_norm_w", "dn_conv_w", "dn_A_log", "dn_dt_bias", "dn_norm_w", "w_out", "final_norm_w")


def _pack_layer(tree, layer, tail=(), names=_LAYERED):
    rows = []
    for name in names:
        a = tree[name][layer]
        if a.shape[-1] == HEADS:
            a = jnp.pad(a, (0, HEAD_DIM - HEADS))
        rows.append(a.reshape(-1, HEAD_DIM))
    rows += [t.reshape(-1, HEAD_DIM) for t in tail]
    packed = jnp.concatenate(rows, axis=0)
    return jnp.pad(packed, ((0, (-packed.shape[0]) % 8), (0, 0)))


def _unpack_layer(packed, like, names=_LAYERED):
    out, at = {}, 0
    for name in names:
        shape = like[name].shape[1:]
        if shape[-1] == HEADS:
            n = 1
            out[name] = packed[at, :HEADS]
        else:
            n = like[name][0].size // HEAD_DIM
            out[name] = packed[at:at + n].reshape(shape)
        at += n
    return out, at


def _heads_to_channels(a):
    return jnp.transpose(a, (1, 0, 2)).reshape(a.shape[1], HEADS * HEAD_DIM)


def kernel(x, norm_w, w_in, lru_conv_w, lru_conv_b, lru_wa, lru_ba, lru_wx, lru_bx, lru_lambda, lru_norm_w, dn_conv_w, dn_A_log, dn_dt_bias, dn_norm_w, w_out, final_norm_w, loss_target, m_norm_w, m_w_in, m_lru_conv_w, m_lru_conv_b, m_lru_wa, m_lru_ba, m_lru_wx, m_lru_bx, m_lru_lambda, m_lru_norm_w, m_dn_conv_w, m_dn_A_log, m_dn_dt_bias, m_dn_norm_w, m_w_out, m_final_norm_w, v_norm_w, v_w_in, v_lru_conv_w, v_lru_conv_b, v_lru_wa, v_lru_ba, v_lru_wx, v_lru_bx, v_lru_lambda, v_lru_norm_w, v_dn_conv_w, v_dn_A_log, v_dn_dt_bias, v_dn_norm_w, v_w_out, v_final_norm_w):
    weights = dict(norm_w=norm_w, w_in=w_in, lru_conv_w=lru_conv_w, lru_conv_b=lru_conv_b, lru_wa=lru_wa,
                   lru_ba=lru_ba, lru_wx=lru_wx, lru_bx=lru_bx, lru_lambda=lru_lambda, lru_norm_w=lru_norm_w,
                   dn_conv_w=dn_conv_w, dn_A_log=dn_A_log, dn_dt_bias=dn_dt_bias, dn_norm_w=dn_norm_w,
                   w_out=w_out, final_norm_w=final_norm_w)
    mom_m = dict(norm_w=m_norm_w, w_in=m_w_in, lru_conv_w=m_lru_conv_w, lru_conv_b=m_lru_conv_b, lru_wa=m_lru_wa,
                 lru_ba=m_lru_ba, lru_wx=m_lru_wx, lru_bx=m_lru_bx, lru_lambda=m_lru_lambda,
                 lru_norm_w=m_lru_norm_w, dn_conv_w=m_dn_conv_w, dn_A_log=m_dn_A_log, dn_dt_bias=m_dn_dt_bias,
                 dn_norm_w=m_dn_norm_w, w_out=m_w_out, final_norm_w=m_final_norm_w)
    mom_v = dict(norm_w=v_norm_w, w_in=v_w_in, lru_conv_w=v_lru_conv_w, lru_conv_b=v_lru_conv_b, lru_wa=v_lru_wa,
                 lru_ba=v_lru_ba, lru_wx=v_lru_wx, lru_bx=v_lru_bx, lru_lambda=v_lru_lambda,
                 lru_norm_w=v_lru_norm_w, dn_conv_w=v_dn_conv_w, dn_A_log=v_dn_A_log, dn_dt_bias=v_dn_dt_bias,
                 dn_norm_w=v_dn_norm_w, w_out=v_w_out, final_norm_w=v_final_norm_w)
    depth = norm_w.shape[0]
    xs = x[0]
    s = xs.shape[0]
    tm = min(1024, s)

    assert depth >= 2, depth

    def row(a):
        return a.reshape(1, -1)

    def pad_row(a):
        return jnp.pad(a, (0, HEAD_DIM - a.shape[0])).reshape(1, HEAD_DIM)

    def full_w_in(g):
        w = jnp.transpose(g, (1, 2, 0, 3)).reshape(g.shape[1], D_MODEL, D_IN)
        return jnp.pad(w, ((0, 0), (0, 0), (0, D_IN_PAD - D_IN)))

    g_win0, g_lcw, g_dcw = _two_level_gather([w_in[:1].astype(BF16), lru_conv_w, dn_conv_w], "gather_first")
    rest = _exchange_start([w_in[1:].astype(BF16), w_out.astype(BF16)], [False] * 2, "gather_rest_start")
    win = [full_w_in(g_win0)[0]]
    wout = None
    lcw = jnp.transpose(g_lcw, (1, 2, 0, 3)).reshape(depth, 4, D_MODEL)
    dcw = jnp.transpose(g_dcw, (1, 2, 0, 3)).reshape(depth, 4, 3 * D_MODEL)

    saved = []
    cur = xs
    for l in range(depth):
        nw_row = row(norm_w[l]) + rest["token"] if l == 0 else row(norm_w[l])
        hn = _rmsnorm_fwd(cur, nw_row, f"norm_fwd_{l}")
        proj = _matmul(hn, win[l], "nn", tm, 896, D_MODEL, f"in_proj_{l}")
        y_lru, hs = _lru_fwd(proj, lcw[l], row(lru_conv_b[l]), lru_wa[l], row(lru_ba[l]), lru_wx[l], row(lru_bx[l]),
                             row(lru_lambda[l]), row(lru_norm_w[l]), f"lru_fwd_{l}")
        ycat, o_dn, states, inverses = _dn_fwd(proj, y_lru, dcw[l], pad_row(dn_A_log[l]), pad_row(dn_dt_bias[l]),
                                     row(dn_norm_w[l]), f"dn_fwd_{l}")
        if l == 0:
            g_win_rest, g_wout = _exchange_wait(rest, ycat, "gather_rest_wait")
            win += list(full_w_in(g_win_rest))
            wout = jnp.transpose(g_wout, (1, 0, 2, 3)).reshape(depth, 2 * D_MODEL, D_MODEL)
        nxt = _matmul(ycat, wout[l], "nn", tm, D_MODEL, 2 * D_MODEL, f"out_proj_{l}", add=cur)
        saved.append((cur, hn, proj, hs, o_dn, states, inverses, ycat))
        cur = nxt
    loss_part, dx, d_final = _final_loss(cur, row(final_norm_w), loss_target[0], "final_loss")

    def win_slots(g):
        return jnp.transpose(g.reshape(D_MODEL, N_DEV, D_IN // N_DEV), (1, 0, 2))

    def wout_slots(g):
        return g.reshape(N_DEV, 2 * D_MODEL // N_DEV, D_MODEL)

    grads = {k: [None] * depth for k in _WEIGHTS if k not in ("final_norm_w", "w_in", "w_out")}
    started = {}
    token = None
    for l in reversed(range(depth)):
        x_in, hn, proj, hs, o_dn, states, inverses, ycat = saved[l]
        dy = _matmul(dx, wout[l], "nt", tm, D_MODEL, D_MODEL, f"out_proj_dy_{l}")
        g_wout_l = _matmul(ycat, dx, "tn", D_MODEL, D_MODEL, tm, f"out_proj_dw_{l}", out_dtype=BF16)
        if l == 0:
            started["w_out_0"] = _exchange_start([wout_slots(g_wout_l)], [True], "exchange_w_out_0_start")
            token = token + started["w_out_0"]["token"]
        cb_row = row(lru_conv_b[l]) if token is None else row(lru_conv_b[l]) + token
        (dlx, dlz, g_lcw, g_lcb, g_wa, g_ba, g_wx, g_bx, g_lam, g_lnw) = _lru_bwd(
            proj, hs, dy, lcw[l], cb_row, lru_wa[l], row(lru_ba[l]), lru_wx[l], row(lru_bx[l]),
            row(lru_lambda[l]), row(lru_norm_w[l]), f"lru_bwd_{l}")
        grads["lru_conv_w"][l] = _heads_to_channels(g_lcw)
        grads["lru_conv_b"][l] = g_lcb.reshape(D_MODEL)
        grads["lru_wa"][l] = g_wa
        grads["lru_ba"][l] = g_ba.reshape(D_MODEL)
        grads["lru_wx"][l] = g_wx
        grads["lru_bx"][l] = g_bx.reshape(D_MODEL)
        grads["lru_lambda"][l] = g_lam.reshape(D_MODEL)
        grads["lru_norm_w"][l] = g_lnw.reshape(D_MODEL)
        al_row = pad_row(dn_A_log[l])
        if l == 0:
            started["pack_0"] = _exchange_start([_pack_layer(grads, 0, names=_PACK_LRU)], [False],
                                                "exchange_pack_0_start")
            al_row = al_row + started["pack_0"]["token"]
        (dq, dk, dv, ddz, dba, g_dcw3, g_al, g_dt, g_dnw) = _dn_bwd(
            proj, o_dn, states, inverses, dy, dcw[l], al_row, pad_row(dn_dt_bias[l]), row(dn_norm_w[l]), f"dn_bwd_{l}")
        g_dcw3 = g_dcw3.reshape(HEADS, 3, 4, HEAD_DIM)
        grads["dn_conv_w"][l] = jnp.concatenate([_heads_to_channels(g_dcw3[:, i]) for i in range(3)], axis=1)
        grads["dn_A_log"][l] = g_al[0, :HEADS]
        grads["dn_dt_bias"][l] = g_dt[0, :HEADS]
        grads["dn_norm_w"][l] = g_dnw.reshape(HEAD_DIM)
        dep = None
        pieces = [dlx, dlz, dq, dk, dv, ddz]
        wide = len(pieces) * D_MODEL
        dba = dba.astype(BF16)
        g_win_l = jnp.concatenate(
            [_matmul_tn_parts(hn, pieces, D_MODEL, D_MODEL, tm, f"in_proj_dw_{l}", BF16),
             _matmul(hn, dba, "tn", D_MODEL, HEAD_DIM, tm, f"in_proj_dw_gates_{l}", out_dtype=BF16)[:, :D_IN - wide]],
            axis=1)
        if l == 0:
            started[0] = _exchange_start([win_slots(g_win_l)], [True], "exchange_0_start")
            dep = started[0]["token_block"]
        dh = _matmul_nt_parts(pieces, dba, win[l], tm, D_MODEL, f"in_proj_dh_{l}", dep=dep)
        dx, g_nw = _rmsnorm_bwd(x_in, row(norm_w[l]), dh, dx, f"norm_bwd_{l}")
        grads["norm_w"][l] = g_nw.reshape(D_MODEL)
        if l > 0:
            tail = (d_final, loss_part) if l == depth - 1 else ()
            started[l] = _exchange_start([win_slots(g_win_l), wout_slots(g_wout_l), _pack_layer(grads, l, tail)],
                                         [True, True, False], f"exchange_{l}_start")
            token = started[l]["token"]

    def conv_slots(a):
        dd, r, cc = a.shape
        return jnp.transpose(a.reshape(dd, r, N_DEV, cc // N_DEV), (2, 0, 1, 3))

    small = _exchange_start(
        [conv_slots(jnp.stack(grads["lru_conv_w"])), conv_slots(jnp.stack(grads["dn_conv_w"])),
         _pack_layer(grads, 0, names=_PACK_LAST)], [True, True, False], "exchange_small_start")

    new = {}
    flat_in = (depth * D_MODEL, D_IN // N_DEV)
    flat_out = (depth * 2 * D_MODEL // N_DEV, D_MODEL)
    zero_row = jnp.zeros((1, HEAD_DIM), F32)

    def adamw_pack(parts, layer, names=_LAYERED, name="adamw_small"):
        tails = [(t, zero_row) if layer == depth - 1 else () for t in (final_norm_w, m_final_norm_w, v_final_norm_w)]
        return _adamw(parts, _pack_layer(weights, layer, tails[0], names), _pack_layer(mom_m, layer, tails[1], names),
                      _pack_layer(mom_v, layer, tails[2], names), f"{name}_{layer}", parts.shape[1])

    def adamw_w_in(parts, layer, into):
        return _adamw(parts, w_in.reshape(flat_in), m_w_in.reshape(flat_in), v_w_in.reshape(flat_in),
                      f"adamw_w_in_{layer}", 256, layer * D_MODEL, into)

    def adamw_w_out(parts, layer, into):
        return _adamw(parts, w_out.reshape(flat_out), m_w_out.reshape(flat_out), v_w_out.reshape(flat_out),
                      f"adamw_w_out_{layer}", 256, layer * flat_out[0] // depth, into)

    acc_in = acc_out = None
    packs = [None] * depth
    after = small["token_block"]
    for l in reversed(range(1, depth)):
        r_win, r_wout, r_pack = _exchange_wait(started[l], after, f"exchange_{l}_wait")
        acc_in = adamw_w_in(r_win, l, acc_in)
        acc_out = adamw_w_out(r_wout, l, acc_out)
        packs[l] = adamw_pack(r_pack, l)
        after = packs[l][0]
    (r_wout,) = _exchange_wait(started["w_out_0"], after, "exchange_w_out_0_wait")
    acc_out = adamw_w_out(r_wout, 0, acc_out)
    (r_win,) = _exchange_wait(started[0], acc_out[0], "exchange_0_wait")
    acc_in = adamw_w_in(r_win, 0, acc_in)
    (r_pack,) = _exchange_wait(started["pack_0"], acc_in[0], "exchange_pack_0_wait")
    packs[0] = adamw_pack(r_pack, 0, _PACK_LRU)
    r_lcw, r_dcw, r_last = _exchange_wait(small, packs[0][0], "exchange_small_wait")
    for name, parts in (("lru_conv_w", r_lcw), ("dn_conv_w", r_dcw)):
        w = weights[name]
        flat = (-1, w.shape[-1])
        outs = _adamw(parts.reshape((N_DEV,) + (w.size // w.shape[-1], w.shape[-1])), w.reshape(flat),
                      mom_m[name].reshape(flat), mom_v[name].reshape(flat), f"adamw_{name}", 8)
        new[name] = [a.reshape(w.shape) for a in outs]
    last_0 = adamw_pack(r_last, 0, _PACK_LAST, "adamw_last")
    new["w_in"] = [a.reshape(w_in.shape) for a in acc_in]
    new["w_out"] = [a.reshape(w_out.shape) for a in acc_out]
    for i in range(4):
        layers = [{**_unpack_layer(packs[0][i], weights, _PACK_LRU)[0],
                   **_unpack_layer(last_0[i], weights, _PACK_LAST)[0]}]
        layers += [_unpack_layer(packs[l][i], weights)[0] for l in range(1, depth)]
        for name in _LAYERED:
            new.setdefault(name, []).append(jnp.stack([layer[name] for layer in layers]))
    tail_at = _unpack_layer(packs[depth - 1][0], weights)[1]
    rows_final = D_MODEL // HEAD_DIM
    new["final_norm_w"] = [packs[depth - 1][i][tail_at:tail_at + rows_final].reshape(D_MODEL) for i in range(4)]
    loss = packs[depth - 1][0][tail_at + rows_final, 0]
    out = [loss, dx.reshape(x.shape)]
    for i in range(4):
        out += [new[name][i] for name in _WEIGHTS]
    return tuple(out)
```

```python
import functools

import jax
import jax.numpy as jnp
from jax import lax
from jax.experimental import pallas as pl
from jax.experimental.pallas import tpu as pltpu

F32 = jnp.float32
BF16 = jnp.bfloat16

N_DEV = 8
D_MODEL = 1024
HEADS = 8
HEAD_DIM = 128
CHUNK = 64
D_IN = 6160
D_IN_PAD = 6272
COL_LRU_X, COL_LRU_Z, COL_Q, COL_K, COL_V, COL_DN_Z, COL_BA = 0, 8, 16, 24, 32, 40, 48
LRU_C = 8.0
EPS = 1e-6
ADAM_LR, ADAM_B1, ADAM_B2, ADAM_EPS, ADAM_WD, ADAM_STEP = 0.001, 0.9, 0.999, 1e-08, 0.01, 10
TIME_BLOCK = 1024
DN_TIME_BLOCK = 128
DN_GROUP = 8
VMEM_LIMIT = 56 * 1024 * 1024

NN = (((1,), (0,)), ((), ()))
NT = (((1,), (1,)), ((), ()))
TN = (((0,), (0,)), ((), ()))


B_NN = (((2,), (1,)), ((0,), (0,)))
B_NT = (((2,), (2,)), ((0,), (0,)))
B_TN = (((1,), (1,)), ((0,), (0,)))


def _split_bf16(x):
    hi = x.astype(BF16)
    return hi, (x - hi.astype(F32)).astype(BF16)


def _dot(a, b, dims, prec):
    if prec == "bf16":
        return lax.dot_general(a.astype(BF16), b.astype(BF16), dims, preferred_element_type=F32)
    a1, a2 = _split_bf16(a)
    b1, b2 = _split_bf16(b)
    dg = functools.partial(lax.dot_general, dimension_numbers=dims, preferred_element_type=F32)
    return dg(a1, b1) + (dg(a1, b2) + dg(a2, b1))


def _make_mm(prec, nn_dims, nt_dims, tn_dims):
    @jax.custom_vjp
    def nn(a, b):
        return _dot(a, b, nn_dims, prec)

    @jax.custom_vjp
    def nt(a, b):
        return _dot(a, b, nt_dims, prec)

    @jax.custom_vjp
    def tn(a, b):
        return _dot(a, b, tn_dims, prec)

    nn.defvjp(lambda a, b: (_dot(a, b, nn_dims, prec), (a, b)),
              lambda r, g: (_dot(g, r[1], nt_dims, prec), _dot(r[0], g, tn_dims, prec)))
    nt.defvjp(lambda a, b: (_dot(a, b, nt_dims, prec), (a, b)),
              lambda r, g: (_dot(g, r[1], nn_dims, prec), _dot(g, r[0], tn_dims, prec)))
    tn.defvjp(lambda a, b: (_dot(a, b, tn_dims, prec), (a, b)),
              lambda r, g: (_dot(r[1], g, nt_dims, prec), _dot(r[0], g, nn_dims, prec)))
    return nn, nt, tn


_NN_B, _NT_B, _TN_B = _make_mm("bf16", NN, NT, TN)
_BNN, _BNT, _BTN = _make_mm("bf16", B_NN, B_NT, B_TN)


@jax.custom_vjp
def _unit_lower_inverse(a):
    n = a.shape[-1]
    eye = (lax.broadcasted_iota(jnp.int32, a.shape, 1) == lax.broadcasted_iota(jnp.int32, a.shape, 2)).astype(F32)
    dg = functools.partial(lax.dot_general, dimension_numbers=B_NN, preferred_element_type=F32)
    inv = eye - a
    pw = _dot(a, a, B_NN, "bf16x3")
    steps = n.bit_length() - 2
    for j in range(steps):
        i1, i2 = _split_bf16(inv)
        p1, p2 = _split_bf16(pw)
        square = j + 1 < steps
        by_hi = dg(jnp.concatenate([i1, i2, p1, p2] if square else [i1, i2], axis=1), p1)
        by_lo = dg(jnp.concatenate([i1, p1], axis=1) if square else i1, p2)
        inv = inv + (by_hi[:, :n] + (by_lo[:, :n] + by_hi[:, n:2 * n]))
        if square:
            pw = by_hi[:, 2 * n:3 * n] + (by_lo[:, n:] + by_hi[:, 3 * n:])
    return inv


def _uli_fwd(a):
    inv = _unit_lower_inverse(a)
    return inv, inv


def _uli_bwd(inv, g):
    return (-_dot(_dot(inv, g, B_TN, "bf16"), inv, B_NT, "bf16"),)


_unit_lower_inverse.defvjp(_uli_fwd, _uli_bwd)


@jax.custom_vjp
def _known_inverse(a, inv):
    return inv


_known_inverse.defvjp(lambda a, inv: (inv, inv), lambda inv, g: (_uli_bwd(inv, g)[0], jnp.zeros_like(inv)))


def _rows2(y, m):
    return y[:, :m], y[:, m:]


@jax.custom_vjp
def _pair_nn(x1, x2, r):
    return _rows2(_dot(jnp.concatenate([x1, x2], axis=1), r, B_NN, "bf16"), x1.shape[1])


def _pair_nn_bwd(res, g):
    x1, x2, r = res
    g = jnp.concatenate(g, axis=1)
    dx1, dx2 = _rows2(_dot(g, r, B_NT, "bf16"), x1.shape[1])
    return dx1, dx2, _dot(jnp.concatenate([x1, x2], axis=1), g, B_TN, "bf16")


_pair_nn.defvjp(lambda x1, x2, r: (_pair_nn(x1, x2, r), (x1, x2, r)), _pair_nn_bwd)


@jax.custom_vjp
def _pair_nt(x1, x2, r):
    return _rows2(_dot(jnp.concatenate([x1, x2], axis=1), r, B_NT, "bf16"), x1.shape[1])


def _pair_nt_bwd(res, g):
    x1, x2, r = res
    g = jnp.concatenate(g, axis=1)
    dx1, dx2 = _rows2(_dot(g, r, B_NN, "bf16"), x1.shape[1])
    return dx1, dx2, _dot(g, jnp.concatenate([x1, x2], axis=1), B_TN, "bf16")


_pair_nt.defvjp(lambda x1, x2, r: (_pair_nt(x1, x2, r), (x1, x2, r)), _pair_nt_bwd)


@jax.custom_vjp
def _wide_nn(l, r1, r2):
    y = _dot(l, jnp.concatenate([r1, r2], axis=2), B_NN, "bf16")
    return y[:, :, :r1.shape[2]], y[:, :, r1.shape[2]:]


def _wide_nn_bwd(res, g):
    l, r1, r2 = res
    g = jnp.concatenate(g, axis=2)
    dr = _dot(l, g, B_TN, "bf16")
    return (_dot(g, jnp.concatenate([r1, r2], axis=2), B_NT, "bf16"), dr[:, :, :r1.shape[2]], dr[:, :, r1.shape[2]:])


_wide_nn.defvjp(lambda l, r1, r2: (_wide_nn(l, r1, r2), (l, r1, r2)), _wide_nn_bwd)


def _lower_ones(batch, n):
    shape = (batch, n, n)
    return (lax.broadcasted_iota(jnp.int32, shape, 1) >= lax.broadcasted_iota(jnp.int32, shape, 2)).astype(BF16)


@jax.custom_vjp
def _chunk_cumsum(g):
    tri = _lower_ones(g.shape[0], g.shape[1])
    g1, g2 = _split_bf16(g)
    g3 = (g - g1.astype(F32) - g2.astype(F32)).astype(BF16)
    dg = functools.partial(lax.dot_general, dimension_numbers=B_NN, preferred_element_type=F32)
    return dg(tri, g1) + (dg(tri, g2) + dg(tri, g3))


def _chunk_cumsum_bwd(_, ct):
    tri = _lower_ones(ct.shape[0], ct.shape[1])
    c1, c2 = _split_bf16(ct)
    dg = functools.partial(lax.dot_general, dimension_numbers=B_TN, preferred_element_type=F32)
    return (dg(tri, c1) + dg(tri, c2),)


_chunk_cumsum.defvjp(lambda g: (_chunk_cumsum(g), None), _chunk_cumsum_bwd)


def _expm1(x):
    small = x * (1.0 + x * (0.5 + x * (1.0 / 6 + x * (1.0 / 24 + x * (1.0 / 120 + x * (1.0 / 720))))))
    return jnp.where(jnp.abs(x) < 0.2, small, jnp.exp(x) - 1.0)


def _sigmoid(x):
    return 0.5 * jnp.tanh(0.5 * x) + 0.5


def _silu(x):
    return x * _sigmoid(x)


def _softplus(x):
    return jnp.maximum(x, 0.0) + jnp.log1p(jnp.exp(-jnp.abs(x)))


def _rmsnorm(x, w):
    return x * lax.rsqrt(jnp.mean(x * x, axis=-1, keepdims=True) + EPS) * w


def _gated_norm(o, z, w):
    return o * lax.rsqrt(jnp.mean(o * o, axis=-1, keepdims=True) + EPS) * w * _silu(z)


def _lru_gates(xc, wa, ba, wx, bx, lam):
    r = 1.0 / (1.0 + jnp.exp(-(_NN_B(xc, wa) + ba)))
    i = _sigmoid(_NN_B(xc, wx) + bx)
    log_a = -LRU_C * r * _softplus(-lam)
    a = jnp.exp(log_a)
    mult = jnp.sqrt(-_expm1(2.0 * log_a))
    return a, mult * (i * xc)


SCAN_ROWS = 32


def _scan_forward(a, b, h0):
    rows = a.shape[0]
    piece = min(SCAN_ROWS, rows)
    pos = lax.broadcasted_iota(jnp.int32, a.shape, 0) % piece
    k = 1
    while k < piece:
        seen = pos >= k
        b = jnp.where(seen, a * pltpu.roll(b, k, 0) + b, b)
        a = jnp.where(seen, a * pltpu.roll(a, k, 0), a)
        k *= 2
    out, entering = [], h0
    for lo in range(0, rows, piece):
        out.append(b[lo:lo + piece] + a[lo:lo + piece] * entering)
        entering = out[-1][piece - 1:piece, :]
    return jnp.concatenate(out, axis=0)


def _scan_reverse(a, d, carry):
    rows = a.shape[0]
    piece = min(SCAN_ROWS, rows)
    row = lax.broadcasted_iota(jnp.int32, a.shape, 0)
    pos = row % piece
    last = row == rows - 1
    c = jnp.where(last, 0.0, pltpu.roll(a, rows - 1, 0))
    d = d + jnp.where(last, carry, 0.0)
    k = 1
    while k < piece:
        seen = pos < piece - k
        d = jnp.where(seen, d + c * pltpu.roll(d, rows - k, 0), d)
        c = jnp.where(seen, c * pltpu.roll(c, rows - k, 0), c)
        k *= 2
    out, following = [], jnp.zeros_like(carry)
    for lo in reversed(range(0, rows, piece)):
        out.insert(0, d[lo:lo + piece] + c[lo:lo + piece] * following)
        following = out[0][0:1, :]
    return jnp.concatenate(out, axis=0)


def _lane_pick(row, lane_index):
    lane = lax.broadcasted_iota(jnp.int32, row.shape, 1)
    return jnp.sum(jnp.where(lane == lane_index, row, 0.0), axis=-1, keepdims=True)


def _dn_prep(qc, kc, vc, ba, a_log_row, dt_row, head):
    q = _silu(qc)
    k = _silu(kc)
    v = _silu(vc)
    q = q * lax.rsqrt(jnp.sum(q * q, axis=-1, keepdims=True) + EPS) * (HEAD_DIM ** -0.5)
    k = k * lax.rsqrt(jnp.sum(k * k, axis=-1, keepdims=True) + EPS)
    beta = _sigmoid(_lane_pick(ba, head))
    g = -jnp.exp(_lane_pick(a_log_row, head)) * _softplus(_lane_pick(ba, HEADS + head) + _lane_pick(dt_row, head))
    return q, k, v, g, beta


def _dn_chunks_head(q, k, v, gcol, bcol, inverse=None):
    n, c, d = q.shape
    row = lax.broadcasted_iota(jnp.int32, (n, c, c), 1)
    col = lax.broadcasted_iota(jnp.int32, (n, c, c), 2)
    g_wide = jnp.broadcast_to(gcol, (n, c, d))
    b_wide = jnp.broadcast_to(bcol, (n, c, d))
    gc = _chunk_cumsum(g_wide)
    gc_rows = gc[:, :, :c]
    decay = jnp.exp(jnp.where(row >= col, gc_rows - jnp.swapaxes(gc_rows, 1, 2), -1e30))
    kb = k * b_wide
    eg = jnp.exp(gc)
    kbk, qk = _pair_nt(kb, q, k)
    a = jnp.where(row > col, kbk * decay, 0.0)
    tinv = _unit_lower_inverse(a) if inverse is None else _known_inverse(a, inverse)
    u, w = _wide_nn(tinv, v * b_wide, kb * eg)
    g_last = jnp.sum(g_wide, axis=1, keepdims=True)
    return u, w, qk * decay, q * eg, k * jnp.exp(g_last - gc), jnp.exp(g_last), tinv


def _dn_chunks(inverse, q, k, v, gcol, bcol, states):
    u, w, attn, qe, kdec, eglast, _ = _dn_chunks_head(q, k, v, gcol, bcol, inverse)
    w_st, qe_st = _pair_nn(w, qe, states)
    v_new = u - w_st
    o = qe_st + _BNN(attn, v_new)
    return (o, states * eglast + _BTN(kdec, v_new)), (w, attn, qe, kdec, eglast)


def _conv_taps(buf, head, cw, rows):
    acc = cw[0:1, :] * buf[head, pl.ds(5, rows), :]
    for j in range(1, 4):
        acc = acc + cw[j:j + 1, :] * buf[head, pl.ds(5 + j, rows), :]
    return acc


def _conv_backward(dbuf, dhead, xbuf, xhead, cw, dxc, rows):
    dbuf[dhead, pl.ds(0, rows), :] = dxc
    dx = cw[0:1, :] * dbuf[dhead, pl.ds(3, rows), :]
    for j in range(1, 4):
        dx = dx + cw[j:j + 1, :] * dbuf[dhead, pl.ds(3 - j, rows), :]
    dcw = jnp.concatenate(
        [jnp.sum(dxc * xbuf[xhead, pl.ds(5 + j, rows), :], axis=0, keepdims=True) for j in range(4)], axis=0)
    dbuf[dhead, pl.ds(rows, 8), :] = dbuf[dhead, pl.ds(0, 8), :]
    return dx, dcw


def _params(**kw):
    return pltpu.CompilerParams(vmem_limit_bytes=VMEM_LIMIT, **kw)


def _matmul(a, b, form, tm, tn, tk, name, add=None, out_dtype=F32, dep=None):
    if form == "nn":
        (m, kdim), (_, n) = a.shape, b.shape
        a_spec = pl.BlockSpec((tm, tk), lambda j, i, k: (i, k))
        b_spec = pl.BlockSpec((tk, tn), lambda j, i, k: (k, j))
        dims = NN
    elif form == "nt":
        (m, kdim), (n, _) = a.shape, b.shape
        a_spec = pl.BlockSpec((tm, tk), lambda j, i, k: (i, k))
        b_spec = pl.BlockSpec((tn, tk), lambda j, i, k: (j, k))
        dims = NT
    else:
        (kdim, m), (_, n) = a.shape, b.shape
        a_spec = pl.BlockSpec((tk, tm), lambda j, i, k: (k, i))
        b_spec = pl.BlockSpec((tk, tn), lambda j, i, k: (k, j))
        dims = TN
    assert m % tm == 0 and n % tn == 0 and kdim % tk == 0, (name, m, n, kdim, tm, tn, tk)
    ksteps = kdim // tk
    o_spec = pl.BlockSpec((tm, tn), lambda j, i, k: (i, j))
    has_add = add is not None
    extra = [] if dep is None else [dep]

    def body(*refs):
        a_ref, b_ref = refs[:2]
        c_ref = refs[2] if has_add else None
        o_ref, acc = refs[-2:]
        k = pl.program_id(2)

        @pl.when(k == 0)
        def _():
            acc[...] = c_ref[...] if has_add else jnp.zeros_like(acc)

        acc[...] += lax.dot_general(a_ref[...].astype(BF16), b_ref[...].astype(BF16), dims,
                                    preferred_element_type=F32)

        @pl.when(k == ksteps - 1)
        def _():
            o_ref[...] = acc[...].astype(o_ref.dtype)

    in_specs = [a_spec, b_spec] + ([o_spec] if has_add else []) + [pl.BlockSpec((8, HEAD_DIM), lambda j, i, k: (0, 0))
                                                                   for _ in extra]
    args = (a, b) + ((add,) if has_add else ()) + tuple(extra)
    return pl.pallas_call(
        body, name=name, grid=(n // tn, m // tm, ksteps), in_specs=in_specs, out_specs=o_spec,
        out_shape=jax.ShapeDtypeStruct((m, n), out_dtype), scratch_shapes=[pltpu.VMEM((tm, tn), F32)],
        compiler_params=_params(dimension_semantics=("parallel", "parallel", "arbitrary")),
    )(*args)


def _matmul_nt_parts(parts, narrow, w, tm, tn, name, dep=None):
    m, c = parts[0].shape
    c2 = narrow.shape[1]
    n = w.shape[0]
    count = len(parts)
    assert m % tm == 0 and n % tn == 0 and all(p.shape == (m, c) for p in parts) and (count * c) % c2 == 0, (name, m, n)
    extra = [] if dep is None else [dep]

    def body(*refs):
        part_refs, narrow_ref, w_ref, w2_ref = refs[:count], refs[count], refs[count + 1], refs[count + 2]
        o_ref, acc = refs[-2:]
        k = pl.program_id(2)

        @pl.when(k == 0)
        def _():
            acc[...] = jnp.zeros_like(acc)

        for p in range(count):
            @pl.when(k == p)
            def _(p=p):
                acc[...] += lax.dot_general(part_refs[p][...].astype(BF16), w_ref[...].astype(BF16), NT,
                                            preferred_element_type=F32)

        @pl.when(k == count)
        def _():
            o_ref[...] = acc[...] + lax.dot_general(narrow_ref[...].astype(BF16), w2_ref[...].astype(BF16), NT,
                                                    preferred_element_type=F32)

    in_specs = ([pl.BlockSpec((tm, c), lambda j, i, k: (i, 0))] * count
                + [pl.BlockSpec((tm, c2), lambda j, i, k: (i, 0)),
                   pl.BlockSpec((tn, c), lambda j, i, k: (j, jnp.minimum(k, count - 1))),
                   pl.BlockSpec((tn, c2), lambda j, i, k: (j, count * c // c2))]
                + [pl.BlockSpec((8, HEAD_DIM), lambda j, i, k: (0, 0)) for _ in extra])
    return pl.pallas_call(
        body, name=name, grid=(n // tn, m // tm, count + 1), in_specs=in_specs,
        out_specs=pl.BlockSpec((tm, tn), lambda j, i, k: (i, j)),
        out_shape=jax.ShapeDtypeStruct((m, n), F32), scratch_shapes=[pltpu.VMEM((tm, tn), F32)],
        compiler_params=_params(dimension_semantics=("parallel", "parallel", "arbitrary")),
    )(*parts, narrow, w, w, *extra)


def _matmul_tn_parts(a, parts, tm, tn, tk, name, out_dtype):
    kdim, m = a.shape
    c = parts[0].shape[1]
    count = len(parts)
    per = c // tn
    assert m % tm == 0 and c % tn == 0 and kdim % tk == 0 and all(p.shape == (kdim, c) for p in parts), (name, m, c)
    ksteps = kdim // tk

    def body(*refs):
        a_ref, part_refs = refs[0], refs[1:1 + count]
        o_ref, acc = refs[-2:]
        j, k = pl.program_id(0), pl.program_id(2)

        @pl.when(k == 0)
        def _():
            acc[...] = jnp.zeros_like(acc)

        for p in range(count):
            @pl.when(j // per == p)
            def _(p=p):
                acc[...] += lax.dot_general(a_ref[...].astype(BF16), part_refs[p][...].astype(BF16), TN,
                                            preferred_element_type=F32)

        @pl.when(k == ksteps - 1)
        def _():
            o_ref[...] = acc[...].astype(o_ref.dtype)

    def part_spec(p):
        return pl.BlockSpec((tk, tn), lambda j, i, k: (jnp.where(j // per == p, k, 0), jnp.where(j // per == p, j % per, 0)))

    return pl.pallas_call(
        body, name=name, grid=(count * per, m // tm, ksteps),
        in_specs=[pl.BlockSpec((tk, tm), lambda j, i, k: (k, i))] + [part_spec(p) for p in range(count)],
        out_specs=pl.BlockSpec((tm, tn), lambda j, i, k: (i, j)),
        out_shape=jax.ShapeDtypeStruct((m, count * c), out_dtype), scratch_shapes=[pltpu.VMEM((tm, tn), F32)],
        compiler_params=_params(dimension_semantics=("parallel", "parallel", "arbitrary")),
    )(a, *parts)


def _rmsnorm_fwd(x, w_row, name):
    s = x.shape[0]
    tb = min(TIME_BLOCK, s)

    def body(x_ref, w_ref, o_ref):
        o_ref[...] = _rmsnorm(x_ref[...], w_ref[...]).astype(BF16)

    return pl.pallas_call(
        body, name=name, grid=(s // tb,),
        in_specs=[pl.BlockSpec((tb, D_MODEL), lambda i: (i, 0)), pl.BlockSpec((1, D_MODEL), lambda i: (0, 0))],
        out_specs=pl.BlockSpec((tb, D_MODEL), lambda i: (i, 0)),
        out_shape=jax.ShapeDtypeStruct((s, D_MODEL), BF16), compiler_params=_params(),
    )(x, w_row)


def _rmsnorm_bwd(x, w_row, dh, dres, name):
    s = x.shape[0]
    tb = min(TIME_BLOCK, s)

    def body(x_ref, w_ref, dh_ref, dres_ref, dx_ref, dw_ref):
        _, vjp = jax.vjp(_rmsnorm, x_ref[...], w_ref[...])
        dx, dw = vjp(dh_ref[...])
        dx_ref[...] = dres_ref[...] + dx

        @pl.when(pl.program_id(0) == 0)
        def _():
            dw_ref[...] = jnp.zeros_like(dw_ref)

        dw_ref[...] += dw

    row = pl.BlockSpec((tb, D_MODEL), lambda i: (i, 0))
    vec = pl.BlockSpec((1, D_MODEL), lambda i: (0, 0))
    return pl.pallas_call(
        body, name=name, grid=(s // tb,), in_specs=[row, vec, row, row], out_specs=[row, vec],
        out_shape=[jax.ShapeDtypeStruct((s, D_MODEL), F32), jax.ShapeDtypeStruct((1, D_MODEL), F32)],
        compiler_params=_params(),
    )(x, w_row, dh, dres)


def _final_loss(x, w_row, target, name):
    s = x.shape[0]
    tb = min(TIME_BLOCK, s)

    def loss_fn(xv, wv, tv):
        err = _rmsnorm(xv, wv) - tv
        return 0.5 * jnp.sum(jnp.sum(err * err, axis=-1, keepdims=True), axis=0, keepdims=True) * (1.0 / D_MODEL)

    def body(x_ref, w_ref, t_ref, loss_ref, dx_ref, dw_ref):
        tv = t_ref[...]
        loss, vjp = jax.vjp(lambda xv, wv: loss_fn(xv, wv, tv), x_ref[...], w_ref[...])
        dx, dw = vjp(jnp.ones((1, 1), F32))
        dx_ref[...] = dx

        @pl.when(pl.program_id(0) == 0)
        def _():
            dw_ref[...] = jnp.zeros_like(dw_ref)
            loss_ref[...] = jnp.zeros_like(loss_ref)

        dw_ref[...] += dw
        loss_ref[...] += jnp.broadcast_to(loss, loss_ref.shape)

    row = pl.BlockSpec((tb, D_MODEL), lambda i: (i, 0))
    vec = pl.BlockSpec((1, D_MODEL), lambda i: (0, 0))
    return pl.pallas_call(
        body, name=name, grid=(s // tb,), in_specs=[row, vec, row],
        out_specs=[pl.BlockSpec((1, HEAD_DIM), lambda i: (0, 0)), row, vec],
        out_shape=[jax.ShapeDtypeStruct((1, HEAD_DIM), F32), jax.ShapeDtypeStruct((s, D_MODEL), F32),
                   jax.ShapeDtypeStruct((1, D_MODEL), F32)],
        compiler_params=_params(),
    )(x, w_row, target)


def _head_specs(tb, time_of):
    def col(off):
        return pl.BlockSpec((tb, HEAD_DIM), lambda t, h: (time_of(t), off + h))
    return col


def _vec_spec():
    return pl.BlockSpec((1, HEAD_DIM), lambda t, h: (0, h))


def _lru_fwd(proj, conv_w, conv_b, wa, ba, wx, bx, lam, nw, name):
    s = proj.shape[0]
    tb = min(TIME_BLOCK, s)
    nt = s // tb
    col = _head_specs(tb, lambda t: t)

    def body(x_ref, z_ref, cw_ref, cb_ref, wa_ref, ba_ref, wx_ref, bx_ref, lam_ref, nw_ref,
             y_ref, hs_ref, xbuf, hcar):
        t, h = pl.program_id(0), pl.program_id(1)

        @pl.when(t == 0)
        def _():
            xbuf[h, pl.ds(0, 8), :] = jnp.zeros((8, HEAD_DIM), F32)
            hcar[h] = jnp.zeros((8, HEAD_DIM), F32)

        xbuf[h, pl.ds(8, tb), :] = x_ref[...]
        xc = _conv_taps(xbuf, h, cw_ref[...], tb) + cb_ref[...]
        a, b = _lru_gates(xc, wa_ref[...], ba_ref[...], wx_ref[...], bx_ref[...], lam_ref[...])
        hs_ref[...] = _scan_forward(a, b, hcar[h, pl.ds(0, 1), :])
        hcar[h, pl.ds(0, 1), :] = hs_ref[pl.ds(tb - 1, 1), :]
        xbuf[h, pl.ds(0, 8), :] = xbuf[h, pl.ds(tb, 8), :]
        y_ref[...] = _gated_norm(hs_ref[...], z_ref[...], nw_ref[...]).astype(BF16)

    vec = _vec_spec()
    return pl.pallas_call(
        body, name=name, grid=(nt, HEADS),
        in_specs=[col(COL_LRU_X), col(COL_LRU_Z), pl.BlockSpec((4, HEAD_DIM), lambda t, h: (0, h)), vec,
                  pl.BlockSpec((None, HEAD_DIM, HEAD_DIM), lambda t, h: (h, 0, 0)), vec,
                  pl.BlockSpec((None, HEAD_DIM, HEAD_DIM), lambda t, h: (h, 0, 0)), vec, vec, vec],
        out_specs=[col(0), col(0)],
        out_shape=[jax.ShapeDtypeStruct((s, 2 * D_MODEL), BF16), jax.ShapeDtypeStruct((s, D_MODEL), F32)],
        scratch_shapes=[pltpu.VMEM((HEADS, tb + 8, HEAD_DIM), F32), pltpu.VMEM((HEADS, 8, HEAD_DIM), F32)],
        compiler_params=_params(dimension_semantics=("arbitrary", "arbitrary")),
    )(proj, proj, conv_w, conv_b, wa, ba, wx, bx, lam, nw)


def _halo_spec(tb, nt, off):
    per = tb // 8
    return pl.BlockSpec((8, HEAD_DIM), lambda t, h: (jnp.maximum((nt - 1 - t) * per - 1, 0), off + h))


def _lru_bwd(proj, hs, dy, conv_w, conv_b, wa, ba, wx, bx, lam, nw, name):
    s = proj.shape[0]
    tb = min(TIME_BLOCK, s)
    nt = s // tb
    col = _head_specs(tb, lambda t: nt - 1 - t)

    def body(x_ref, xh_ref, z_ref, hs_ref, hh_ref, dy_ref, cw_ref, cb_ref, wa_ref, ba_ref, wx_ref, bx_ref,
             lam_ref, nw_ref, dx_ref, dz_ref, dcw_ref, dcb_ref, dwa_ref, dba_ref, dwx_ref, dbx_ref, dlam_ref,
             dnw_ref, xbuf, hbuf, dbuf, gcar):
        t, h = pl.program_id(0), pl.program_id(1)
        first_block = t == nt - 1

        @pl.when(t == 0)
        def _():
            dbuf[h, pl.ds(tb, 8), :] = jnp.zeros((8, HEAD_DIM), F32)
            gcar[h] = jnp.zeros((8, HEAD_DIM), F32)
            dcw_ref[h] = jnp.zeros((4, HEAD_DIM), F32)
            dwa_ref[h] = jnp.zeros((HEAD_DIM, HEAD_DIM), F32)
            dwx_ref[h] = jnp.zeros((HEAD_DIM, HEAD_DIM), F32)
            for ref in (dcb_ref, dba_ref, dbx_ref, dlam_ref, dnw_ref):
                ref[h] = jnp.zeros((1, HEAD_DIM), F32)

        keep = jnp.where(first_block, 0.0, 1.0)
        xbuf[0, pl.ds(0, 8), :] = xh_ref[...] * keep
        xbuf[0, pl.ds(8, tb), :] = x_ref[...]
        hbuf[pl.ds(0, 8), :] = hh_ref[...] * keep
        hbuf[pl.ds(8, tb), :] = hs_ref[...]
        cw = cw_ref[...]
        xc = _conv_taps(xbuf, 0, cw, tb) + cb_ref[...]
        (a, _), gates_vjp = jax.vjp(_lru_gates, xc, wa_ref[...], ba_ref[...], wx_ref[...], bx_ref[...], lam_ref[...])
        _, norm_vjp = jax.vjp(_gated_norm, hs_ref[...], z_ref[...], nw_ref[...])
        dh, dz, dnw = norm_vjp(dy_ref[...])
        dz_ref[...] = dz.astype(dz_ref.dtype)
        g = _scan_reverse(a, dh, gcar[h, pl.ds(0, 1), :])
        gcar[h, pl.ds(0, 1), :] = a[0:1, :] * g[0:1, :]
        dxc, dwa, dba, dwx, dbx, dlam = gates_vjp((g * hbuf[pl.ds(7, tb), :], g))
        dx, dcw = _conv_backward(dbuf, h, xbuf, 0, cw, dxc, tb)
        dx_ref[...] = dx.astype(dx_ref.dtype)
        dcw_ref[h] += dcw
        dcb_ref[h] += jnp.sum(dxc, axis=0, keepdims=True)
        dwa_ref[h] += dwa
        dwx_ref[h] += dwx
        dba_ref[h] += dba
        dbx_ref[h] += dbx
        dlam_ref[h] += dlam
        dnw_ref[h] += dnw

    vec = _vec_spec()
    mat = pl.BlockSpec((None, HEAD_DIM, HEAD_DIM), lambda t, h: (h, 0, 0))

    def whole(shape):
        return pl.BlockSpec(shape, lambda t, h: (0,) * len(shape))

    head_vec = jax.ShapeDtypeStruct((HEADS, 1, HEAD_DIM), F32)
    head_mat = jax.ShapeDtypeStruct((HEADS, HEAD_DIM, HEAD_DIM), F32)
    return pl.pallas_call(
        body, name=name, grid=(nt, HEADS),
        in_specs=[col(COL_LRU_X), _halo_spec(tb, nt, COL_LRU_X), col(COL_LRU_Z), col(0), _halo_spec(tb, nt, 0), col(0),
                  pl.BlockSpec((4, HEAD_DIM), lambda t, h: (0, h)), vec, mat, vec, mat, vec, vec, vec],
        out_specs=[col(0), col(0), whole((HEADS, 4, HEAD_DIM)), whole((HEADS, 1, HEAD_DIM)),
                   whole((HEADS, HEAD_DIM, HEAD_DIM)), whole((HEADS, 1, HEAD_DIM)),
                   whole((HEADS, HEAD_DIM, HEAD_DIM)), whole((HEADS, 1, HEAD_DIM)), whole((HEADS, 1, HEAD_DIM)),
                   whole((HEADS, 1, HEAD_DIM))],
        out_shape=[jax.ShapeDtypeStruct((s, D_MODEL), BF16), jax.ShapeDtypeStruct((s, D_MODEL), BF16),
                   jax.ShapeDtypeStruct((HEADS, 4, HEAD_DIM), F32), head_vec, head_mat, head_vec, head_mat, head_vec,
                   head_vec, head_vec],
        scratch_shapes=[pltpu.VMEM((1, tb + 8, HEAD_DIM), F32), pltpu.VMEM((tb + 8, HEAD_DIM), F32),
                        pltpu.VMEM((HEADS, tb + 8, HEAD_DIM), F32), pltpu.VMEM((HEADS, 8, HEAD_DIM), F32)],
        compiler_params=_params(dimension_semantics=("arbitrary", "arbitrary")),
    )(proj, proj, proj, hs, hs, dy, conv_w, conv_b, wa, ba, wx, bx, lam, nw)


def _group_col(tb, time_of):
    def col(off):
        return pl.BlockSpec((tb, DN_GROUP * HEAD_DIM), lambda t, hg: (time_of(t), off // DN_GROUP + hg))
    return col


def _dn_fwd(proj, y, conv_w, a_log_row, dt_row, nw, name):
    s = proj.shape[0]
    tb = min(DN_TIME_BLOCK, s)
    nt = s // tb
    nchunk = tb // CHUNK
    grp = DN_GROUP
    col = _group_col(tb, lambda t: t)

    def body(q_ref, k_ref, v_ref, z_ref, ba_ref, cwq_ref, cwk_ref, cwv_ref, al_ref, dt_ref, nw_ref, y_in_ref,
             y_ref, o_ref, st_ref, inv_ref, xbuf, state):
        t, hg = pl.program_id(0), pl.program_id(1)

        def chunks(a):
            return a.reshape(nchunk, CHUNK, a.shape[-1])

        prepared = []
        for gi in range(grp):
            h = hg * grp + gi
            lanes = slice(gi * HEAD_DIM, (gi + 1) * HEAD_DIM)

            @pl.when(t == 0)
            def _(h=h):
                for i in range(3):
                    xbuf[3 * h + i, pl.ds(0, 8), :] = jnp.zeros((8, HEAD_DIM), F32)
                state[h] = jnp.zeros((HEAD_DIM, HEAD_DIM), F32)

            conv = []
            for i, (ref, cw_ref) in enumerate(((q_ref, cwq_ref), (k_ref, cwk_ref), (v_ref, cwv_ref))):
                xbuf[3 * h + i, pl.ds(8, tb), :] = ref[:, lanes]
                conv.append(_conv_taps(xbuf, 3 * h + i, cw_ref[:, lanes], tb))
                xbuf[3 * h + i, pl.ds(0, 8), :] = xbuf[3 * h + i, pl.ds(tb, 8), :]
            prepared.append([chunks(a) for a in
                             _dn_prep(conv[0], conv[1], conv[2], ba_ref[...], al_ref[...], dt_ref[...], h)])
        qs, ks, vs, gs, bs = [jnp.concatenate([p[i] for p in prepared], axis=0) for i in range(5)]
        u, w, attn, qe, kdec, eglast, tinv = _dn_chunks_head(qs, ks, vs, gs, bs)
        inv_ref[...] = tinv.reshape(grp, nchunk, CHUNK, CHUNK)
        w_u = jnp.concatenate([w, u], axis=2)
        kdec_w_u = _dot(kdec, w_u, B_TN, "bf16")
        attn_w_u = _dot(attn, w_u, B_NN, "bf16")
        st = [state[hg * grp + gi] for gi in range(grp)]
        for c in range(nchunk):
            for gi in range(grp):
                n = gi * nchunk + c
                st_ref[gi, c] = st[gi]
                st[gi] = st[gi] * eglast[n] - _NN_B(kdec_w_u[n, :, :HEAD_DIM], st[gi]) + kdec_w_u[n, :, HEAD_DIM:]
        for gi in range(grp):
            state[hg * grp + gi] = st[gi]
        states = st_ref[...].reshape(grp * nchunk, HEAD_DIM, HEAD_DIM)
        o = _dot(qe - attn_w_u[:, :, :HEAD_DIM], states, B_NN, "bf16") + attn_w_u[:, :, HEAD_DIM:]
        for gi in range(grp):
            lanes = slice(gi * HEAD_DIM, (gi + 1) * HEAD_DIM)
            o_head = o[gi * nchunk:(gi + 1) * nchunk].reshape(tb, HEAD_DIM)
            o_ref[:, lanes] = o_head
            y_ref[:, lanes] = _gated_norm(o_head, z_ref[:, lanes], nw_ref[...]).astype(BF16)

    def cw_spec(off):
        return pl.BlockSpec((4, grp * HEAD_DIM), lambda t, hg: (0, off // grp + hg))

    row128 = pl.BlockSpec((1, HEAD_DIM), lambda t, hg: (0, 0))
    return pl.pallas_call(
        body, name=name, grid=(nt, HEADS // grp),
        in_specs=[col(COL_Q), col(COL_K), col(COL_V), col(COL_DN_Z),
                  pl.BlockSpec((tb, HEAD_DIM), lambda t, hg: (t, COL_BA)),
                  cw_spec(0), cw_spec(HEADS), cw_spec(2 * HEADS), row128, row128, row128,
                  pl.BlockSpec(memory_space=pl.ANY)],
        out_specs=[col(HEADS), col(0),
                   pl.BlockSpec((grp, nchunk, HEAD_DIM, HEAD_DIM), lambda t, hg: (hg, t, 0, 0)),
                   pl.BlockSpec((grp, nchunk, CHUNK, CHUNK), lambda t, hg: (hg, t, 0, 0))],
        out_shape=[jax.ShapeDtypeStruct((s, 2 * D_MODEL), BF16), jax.ShapeDtypeStruct((s, D_MODEL), F32),
                   jax.ShapeDtypeStruct((HEADS, s // CHUNK, HEAD_DIM, HEAD_DIM), F32),
                   jax.ShapeDtypeStruct((HEADS, s // CHUNK, CHUNK, CHUNK), F32)],
        input_output_aliases={11: 0},
        scratch_shapes=[pltpu.VMEM((3 * HEADS, tb + 8, HEAD_DIM), F32), pltpu.VMEM((HEADS, HEAD_DIM, HEAD_DIM), F32)],
        compiler_params=_params(dimension_semantics=("arbitrary", "arbitrary")),
    )(proj, proj, proj, proj, proj, conv_w, conv_w, conv_w, a_log_row, dt_row, nw, y)


def _dn_bwd(proj, o, states, inverses, dy, conv_w, a_log_row, dt_row, nw, name):
    s = proj.shape[0]
    tb = min(DN_TIME_BLOCK, s)
    nt = s // tb
    nchunk = tb // CHUNK
    grp = DN_GROUP
    col = _group_col(tb, lambda t: nt - 1 - t)

    def body(q_ref, qh_ref, k_ref, kh_ref, v_ref, vh_ref, z_ref, ba_ref, o_ref, st_ref, inv_ref, dy_ref,
             cwq_ref, cwk_ref, cwv_ref, al_ref, dt_ref, nw_ref,
             dq_ref, dk_ref, dv_ref, dz_ref, dba_ref, dcw_ref, dal_ref, ddt_ref, dnw_ref,
             xbuf, dbuf, dstate, dst_s):
        t, hg = pl.program_id(0), pl.program_id(1)
        keep = jnp.where(t == nt - 1, 0.0, 1.0)

        @pl.when((t == 0) & (hg == 0))
        def _():
            for ref in (dal_ref, ddt_ref, dnw_ref):
                ref[...] = jnp.zeros_like(ref)

        def chunks(a):
            return a.reshape(nchunk, CHUNK, a.shape[-1])

        prepared, prep_vjps, dos = [], [], []
        for gi in range(grp):
            h = hg * grp + gi
            lanes = slice(gi * HEAD_DIM, (gi + 1) * HEAD_DIM)

            @pl.when(t == 0)
            def _(h=h):
                for i in range(3):
                    dbuf[3 * h + i, pl.ds(tb, 8), :] = jnp.zeros((8, HEAD_DIM), F32)
                    dcw_ref[3 * h + i] = jnp.zeros((4, HEAD_DIM), F32)
                dstate[h] = jnp.zeros((HEAD_DIM, HEAD_DIM), F32)

            conv = []
            for i, (ref, halo, cw_ref) in enumerate(((q_ref, qh_ref, cwq_ref), (k_ref, kh_ref, cwk_ref),
                                                     (v_ref, vh_ref, cwv_ref))):
                xbuf[3 * gi + i, pl.ds(0, 8), :] = halo[:, lanes] * keep
                xbuf[3 * gi + i, pl.ds(8, tb), :] = ref[:, lanes]
                conv.append(_conv_taps(xbuf, 3 * gi + i, cw_ref[:, lanes], tb))
            outs, prep_vjp = jax.vjp(
                lambda qc, kc, vc, ba, al, dt, h=h: _dn_prep(qc, kc, vc, ba, al, dt, h),
                conv[0], conv[1], conv[2], ba_ref[...], al_ref[...], dt_ref[...])
            prepared.append([chunks(a) for a in outs])
            prep_vjps.append(prep_vjp)
            _, norm_vjp = jax.vjp(_gated_norm, o_ref[:, lanes], z_ref[:, lanes], nw_ref[...])
            do, dz, dnw = norm_vjp(dy_ref[:, lanes])
            dz_ref[:, lanes] = dz.astype(dz_ref.dtype)
            dnw_ref[...] += dnw
            dos.append(chunks(do))
        qs, ks, vs, gs, bs = [jnp.concatenate([p[i] for p in prepared], axis=0) for i in range(5)]
        do = jnp.concatenate(dos, axis=0)
        states_in = st_ref[...].reshape(grp * nchunk, HEAD_DIM, HEAD_DIM)
        kept = inv_ref[...].reshape(grp * nchunk, CHUNK, CHUNK)
        _, chunks_vjp, (w, attn, qe, kdec, eglast) = jax.vjp(
            functools.partial(_dn_chunks, kept), qs, ks, vs, gs, bs, states_in, has_aux=True)
        kdec_w = _dot(kdec, w, B_TN, "bf16")
        fixed = _dot(qe, do, B_TN, "bf16") - _dot(w, _dot(attn, do, B_TN, "bf16"), B_TN, "bf16")
        dst = [dstate[hg * grp + gi] for gi in range(grp)]
        for c in reversed(range(nchunk)):
            for gi in range(grp):
                n = gi * nchunk + c
                dst_s[n] = dst[gi]
                dst[gi] = dst[gi] * eglast[n] - _dot(kdec_w[n], dst[gi], TN, "bf16") + fixed[n]
        for gi in range(grp):
            dstate[hg * grp + gi] = dst[gi]
        cts = chunks_vjp((do, dst_s[...]))[:5]

        dba_sum = None
        for gi in range(grp):
            h = hg * grp + gi
            lanes = slice(gi * HEAD_DIM, (gi + 1) * HEAD_DIM)
            per_head = [ct[gi * nchunk:(gi + 1) * nchunk].reshape(tb, ct.shape[-1]) for ct in cts]
            dqc, dkc, dvc, dba, dal, ddt = prep_vjps[gi](tuple(per_head))
            for i, (dxc, out, cw_ref) in enumerate(((dqc, dq_ref, cwq_ref), (dkc, dk_ref, cwk_ref),
                                                    (dvc, dv_ref, cwv_ref))):
                dx, dcw = _conv_backward(dbuf, 3 * h + i, xbuf, 3 * gi + i, cw_ref[:, lanes], dxc, tb)
                out[:, lanes] = dx.astype(out.dtype)
                dcw_ref[3 * h + i] += dcw
            dal_ref[...] += dal
            ddt_ref[...] += ddt
            dba_sum = dba if dba_sum is None else dba_sum + dba

        @pl.when(hg == 0)
        def _():
            dba_ref[...] = dba_sum.astype(dba_ref.dtype)

        @pl.when(hg > 0)
        def _():
            dba_ref[...] += dba_sum.astype(dba_ref.dtype)

    def cw_spec(off):
        return pl.BlockSpec((4, grp * HEAD_DIM), lambda t, hg: (0, off // grp + hg))

    def halo(off):
        per = tb // 8
        return pl.BlockSpec((8, grp * HEAD_DIM),
                            lambda t, hg: (jnp.maximum((nt - 1 - t) * per - 1, 0), off // grp + hg))

    def whole(shape):
        return pl.BlockSpec(shape, lambda t, hg: (0,) * len(shape))

    row128 = whole((1, HEAD_DIM))
    blk = (tb, HEAD_DIM)
    act = jax.ShapeDtypeStruct((s, D_MODEL), BF16)
    row_out = jax.ShapeDtypeStruct((1, HEAD_DIM), F32)
    return pl.pallas_call(
        body, name=name, grid=(nt, HEADS // grp),
        in_specs=[col(COL_Q), halo(COL_Q), col(COL_K), halo(COL_K), col(COL_V), halo(COL_V), col(COL_DN_Z),
                  pl.BlockSpec(blk, lambda t, hg: (nt - 1 - t, COL_BA)), col(0),
                  pl.BlockSpec((grp, nchunk, HEAD_DIM, HEAD_DIM), lambda t, hg: (hg, nt - 1 - t, 0, 0)),
                  pl.BlockSpec((grp, nchunk, CHUNK, CHUNK), lambda t, hg: (hg, nt - 1 - t, 0, 0)), col(HEADS),
                  cw_spec(0), cw_spec(HEADS), cw_spec(2 * HEADS), row128, row128, row128],
        out_specs=[col(0), col(0), col(0), col(0), pl.BlockSpec(blk, lambda t, hg: (nt - 1 - t, 0)),
                   whole((3 * HEADS, 4, HEAD_DIM)), row128, row128, row128],
        out_shape=[act, act, act, act, jax.ShapeDtypeStruct((s, HEAD_DIM), F32),
                   jax.ShapeDtypeStruct((3 * HEADS, 4, HEAD_DIM), F32), row_out, row_out, row_out],
        scratch_shapes=[pltpu.VMEM((3 * grp, tb + 8, HEAD_DIM), F32), pltpu.VMEM((3 * HEADS, tb + 8, HEAD_DIM), F32),
                        pltpu.VMEM((HEADS, HEAD_DIM, HEAD_DIM), F32),
                        pltpu.VMEM((grp * nchunk, HEAD_DIM, HEAD_DIM), F32)],
        compiler_params=_params(dimension_semantics=("arbitrary", "arbitrary")),
    )(proj, proj, proj, proj, proj, proj, proj, proj, o, states, inverses, dy, conv_w, conv_w, conv_w, a_log_row, dt_row,
      nw)


def _mesh_position():
    x, y, c = lax.axis_index("x"), lax.axis_index("y"), lax.axis_index("c")
    return x, y, c, 4 * x + 2 * y + c


def _peer(k, x, y, c):
    px = 1 - x if k & 4 else x
    py = 1 - y if k & 2 else y
    pc = 1 - c if k & 1 else c
    return (px, py, pc), 4 * px + 2 * py + pc


def _exchange_copies(ins, lands, scatter, send_sems, recv_sems, receives=True):
    x, y, c, me = _mesh_position()
    sends, recvs = [], []
    for i, (src, land) in enumerate(zip(ins, lands)):
        for k in range(1, N_DEV):
            peer, peer_id = _peer(k, x, y, c)
            sem = i * (N_DEV - 1) + k - 1
            for dst, out in ((me, sends), (peer_id, recvs)) if receives else ((me, sends),):
                out.append(pltpu.make_async_remote_copy(
                    src_ref=src.at[peer_id] if scatter[i] else src, dst_ref=land.at[dst],
                    send_sem=send_sems.at[sem], recv_sem=recv_sems.at[sem],
                    device_id=peer, device_id_type=pl.DeviceIdType.MESH))
    return sends, recvs


def _landing_shape(a, scatter):
    return a.shape if scatter else (N_DEV,) + a.shape


def _two_level_gather(arrays, name):
    n = len(arrays)
    per = N_DEV - 1

    def body(*refs):
        ins, outs = refs[:n], refs[n:2 * n]
        send_sems, recv_sems, local_sems = refs[2 * n:]
        x, y, c, me = _mesh_position()
        sibling = (x, y, 1 - c)
        chips = [(1 - x, y), (x, 1 - y), (1 - x, 1 - y)]

        def copy(i, k, block, to, src=None):
            slot = outs[i].at[4 * block[0] + 2 * block[1] + block[2]]
            return pltpu.make_async_remote_copy(
                src_ref=slot if src is None else src, dst_ref=slot,
                send_sem=send_sems.at[i * per + k], recv_sem=recv_sems.at[i * per + k],
                device_id=to, device_id_type=pl.DeviceIdType.MESH)

        local = [pltpu.make_async_copy(ins[i], outs[i].at[me], local_sems.at[i]) for i in range(n)]
        first = []
        for i in range(n):
            first.append(copy(i, 0, (x, y, c), sibling, src=ins[i]))
            first += [copy(i, 1 + j, (x, y, c), (*chip, c), src=ins[i]) for j, chip in enumerate(chips)]
        for cp in local + first:
            cp.start()
        passed = []
        for i in range(n):
            for j, chip in enumerate(chips):
                copy(i, 1 + j, (*chip, c), (x, y, c)).wait_recv()
                passed.append(copy(i, 4 + j, (*chip, c), sibling))
                passed[-1].start()
        for i in range(n):
            copy(i, 0, sibling, (x, y, c)).wait_recv()
            for j, chip in enumerate(chips):
                copy(i, 4 + j, (*chip, 1 - c), (x, y, c)).wait_recv()
        for cp in first + passed:
            cp.wait_send()
        for cp in local:
            cp.wait()

    hbm = pl.BlockSpec(memory_space=pl.ANY)
    return pl.pallas_call(
        body, name=name, in_specs=[hbm] * n, out_specs=[hbm] * n,
        out_shape=[jax.ShapeDtypeStruct((N_DEV,) + a.shape, a.dtype) for a in arrays],
        scratch_shapes=[pltpu.SemaphoreType.DMA((n * per,)), pltpu.SemaphoreType.DMA((n * per,)),
                        pltpu.SemaphoreType.DMA((n,))],
    )(*arrays)


_HBM = pl.BlockSpec(memory_space=pltpu.HBM)
_SEM = pl.BlockSpec(memory_space=pltpu.SEMAPHORE)
_DATAFLOW = pltpu.SideEffectType.DATAFLOW_SIDE_EFFECTING


def _exchange_start(arrays, scatter, name):
    n = len(arrays)
    srcs = [pltpu.with_memory_space_constraint(a, pltpu.HBM) for a in arrays]
    lands = [pltpu.with_memory_space_constraint(lax.empty(_landing_shape(a, sc), a.dtype), pltpu.HBM)
             for a, sc in zip(arrays, scatter)]
    nsem = n * (N_DEV - 1)

    def body(*refs):
        ins, zones = refs[:n], refs[n:2 * n]
        send_sems, recv_sems = refs[2 * n], refs[2 * n + 1]
        token = refs[-1]
        sends, _ = _exchange_copies(ins, zones, scatter, send_sems, recv_sems, receives=False)
        for cp in sends:
            cp.start()
        token[...] = jnp.zeros_like(token)

    res = pl.pallas_call(
        body, name=name,
        out_shape=(pltpu.SemaphoreType.DMA((nsem,)), pltpu.SemaphoreType.DMA((nsem,)),
                   *[pltpu.HBM(a.shape, a.dtype) for a in srcs + lands], jax.ShapeDtypeStruct((8, HEAD_DIM), F32)),
        in_specs=[_HBM] * (2 * n),
        out_specs=(_SEM, _SEM, *[_HBM] * (2 * n), pl.BlockSpec(memory_space=pltpu.VMEM)),
        input_output_aliases={i: 2 + i for i in range(2 * n)},
        compiler_params=pltpu.CompilerParams(has_side_effects=_DATAFLOW),
    )(*srcs, *lands)
    return dict(sems=res[:2], srcs=res[2:2 + n], lands=res[2 + n:2 + 2 * n], token_block=res[-1],
                token=res[-1][0, 0], scatter=scatter)


def _exchange_wait(started, after, name):
    scatter = started["scatter"]
    n = len(scatter)

    def body(*refs):
        ins, zones = refs[:n], refs[n:2 * n]
        send_sems, recv_sems = refs[2 * n], refs[2 * n + 1]
        sends, recvs = _exchange_copies(ins, zones, scatter, send_sems, recv_sems)
        for cp in sends:
            cp.wait_send()
        for cp in recvs:
            cp.wait_recv()

    thru = list(started["srcs"]) + list(started["lands"])
    res = pl.pallas_call(
        body, name=name, out_shape=[pltpu.HBM(a.shape, a.dtype) for a in thru],
        in_specs=[_HBM] * (2 * n) + [_SEM, _SEM, pl.BlockSpec(memory_space=pl.ANY)], out_specs=[_HBM] * (2 * n),
        input_output_aliases={i: i for i in range(2 * n)},
        compiler_params=pltpu.CompilerParams(has_side_effects=_DATAFLOW),
    )(*thru, *started["sems"], after)
    me = 4 * lax.axis_index("x") + 2 * lax.axis_index("y") + lax.axis_index("c")
    out = []
    for src, got, sc in zip(res[:n], res[n:], scatter):
        own = lax.dynamic_index_in_dim(src, me, 0, keepdims=False) if sc else src
        out.append(lax.dynamic_update_index_in_dim(got, own, me, 0))
    return out


def _adamw(parts, w, m, v, name, rows_per_step, row_offset=0, into=None):
    rows, cols = parts.shape[1:]
    tr = min(rows_per_step, rows)
    assert rows % tr == 0 and row_offset % tr == 0, (name, rows, tr, row_offset)
    first = row_offset // tr
    c1 = 1.0 / (1.0 - ADAM_B1 ** ADAM_STEP)
    c2 = 1.0 / (1.0 - ADAM_B2 ** ADAM_STEP)

    def body(p_ref, w_ref, m_ref, v_ref, *rest):
        g_ref, d_ref, nm_ref, nv_ref = rest[-4:]
        g = p_ref[0].astype(F32)
        for d in range(1, N_DEV):
            g = g + p_ref[d].astype(F32)
        nm = ADAM_B1 * m_ref[...] + (1.0 - ADAM_B1) * g
        nv = ADAM_B2 * v_ref[...] + (1.0 - ADAM_B2) * (g * g)
        g_ref[...] = g
        nm_ref[...] = nm
        nv_ref[...] = nv
        d_ref[...] = -ADAM_LR * ((nm * c1) / (jnp.sqrt(nv * c2) + ADAM_EPS) + ADAM_WD * w_ref[...])

    blk = pl.BlockSpec((tr, cols), lambda i: (i + first, 0))
    shape = jax.ShapeDtypeStruct(w.shape, F32)
    prior = [] if into is None else list(into)
    return pl.pallas_call(
        body, name=name, grid=(rows // tr,),
        in_specs=[pl.BlockSpec((N_DEV, tr, cols), lambda i: (0, i, 0)), blk, blk, blk]
        + [pl.BlockSpec(memory_space=pl.ANY)] * len(prior),
        out_specs=[blk] * 4, out_shape=[shape] * 4,
        input_output_aliases={4 + j: j for j in range(len(prior))}, compiler_params=_params(),
    )(parts, w, m, v, *prior)


_LAYERED = ("norm_w", "lru_conv_b", "lru_wa", "lru_ba", "lru_wx", "lru_bx", "lru_lambda", "lru_norm_w",
            "dn_A_log", "dn_dt_bias", "dn_norm_w")
_PACK_LRU = _LAYERED[1:8]
_PACK_LAST = _LAYERED[:1] + _LAYERED[8:]
_WEIGHTS = ("norm_w", "w_in", "lru_conv_w", "lru_conv_b", "lru_wa", "lru_ba", "lru_wx", "lru_bx", "lru_lambda",
            "lru_norm_w", "dn_conv_w", "dn_A_log", "dn_dt_bias", "dn_norm_w", "w_out", "final_norm_w")


def _pack_layer(tree, layer, tail=(), names=_LAYERED):
    rows = []
    for name in names:
        a = tree[name][layer]
        if a.shape[-1] == HEADS:
            a = jnp.pad(a, (0, HEAD_DIM - HEADS))
        rows.append(a.reshape(-1, HEAD_DIM))
    rows += [t.reshape(-1, HEAD_DIM) for t in tail]
    packed = jnp.concatenate(rows, axis=0)
    return jnp.pad(packed, ((0, (-packed.shape[0]) % 8), (0, 0)))


def _unpack_layer(packed, like, names=_LAYERED):
    out, at = {}, 0
    for name in names:
        shape = like[name].shape[1:]
        if shape[-1] == HEADS:
            n = 1
            out[name] = packed[at, :HEADS]
        else:
            n = like[name][0].size // HEAD_DIM
            out[name] = packed[at:at + n].reshape(shape)
        at += n
    return out, at


def _heads_to_channels(a):
    return jnp.transpose(a, (1, 0, 2)).reshape(a.shape[1], HEADS * HEAD_DIM)


def kernel(x, norm_w, w_in, lru_conv_w, lru_conv_b, lru_wa, lru_ba, lru_wx, lru_bx, lru_lambda, lru_norm_w, dn_conv_w, dn_A_log, dn_dt_bias, dn_norm_w, w_out, final_norm_w, loss_target, m_norm_w, m_w_in, m_lru_conv_w, m_lru_conv_b, m_lru_wa, m_lru_ba, m_lru_wx, m_lru_bx, m_lru_lambda, m_lru_norm_w, m_dn_conv_w, m_dn_A_log, m_dn_dt_bias, m_dn_norm_w, m_w_out, m_final_norm_w, v_norm_w, v_w_in, v_lru_conv_w, v_lru_conv_b, v_lru_wa, v_lru_ba, v_lru_wx, v_lru_bx, v_lru_lambda, v_lru_norm_w, v_dn_conv_w, v_dn_A_log, v_dn_dt_bias, v_dn_norm_w, v_w_out, v_final_norm_w):
    weights = dict(norm_w=norm_w, w_in=w_in, lru_conv_w=lru_conv_w, lru_conv_b=lru_conv_b, lru_wa=lru_wa,
                   lru_ba=lru_ba, lru_wx=lru_wx, lru_bx=lru_bx, lru_lambda=lru_lambda, lru_norm_w=lru_norm_w,
                   dn_conv_w=dn_conv_w, dn_A_log=dn_A_log, dn_dt_bias=dn_dt_bias, dn_norm_w=dn_norm_w,
                   w_out=w_out, final_norm_w=final_norm_w)
    mom_m = dict(norm_w=m_norm_w, w_in=m_w_in, lru_conv_w=m_lru_conv_w, lru_conv_b=m_lru_conv_b, lru_wa=m_lru_wa,
                 lru_ba=m_lru_ba, lru_wx=m_lru_wx, lru_bx=m_lru_bx, lru_lambda=m_lru_lambda,
                 lru_norm_w=m_lru_norm_w, dn_conv_w=m_dn_conv_w, dn_A_log=m_dn_A_log, dn_dt_bias=m_dn_dt_bias,
                 dn_norm_w=m_dn_norm_w, w_out=m_w_out, final_norm_w=m_final_norm_w)
    mom_v = dict(norm_w=v_norm_w, w_in=v_w_in, lru_conv_w=v_lru_conv_w, lru_conv_b=v_lru_conv_b, lru_wa=v_lru_wa,
                 lru_ba=v_lru_ba, lru_wx=v_lru_wx, lru_bx=v_lru_bx, lru_lambda=v_lru_lambda,
                 lru_norm_w=v_lru_norm_w, dn_conv_w=v_dn_conv_w, dn_A_log=v_dn_A_log, dn_dt_bias=v_dn_dt_bias,
                 dn_norm_w=v_dn_norm_w, w_out=v_w_out, final_norm_w=v_final_norm_w)
    depth = norm_w.shape[0]
    xs = x[0]
    s = xs.shape[0]
    tm = min(1024, s)

    assert depth >= 2, depth

    def row(a):
        return a.reshape(1, -1)

    def pad_row(a):
        return jnp.pad(a, (0, HEAD_DIM - a.shape[0])).reshape(1, HEAD_DIM)

    def full_w_in(g):
        w = jnp.transpose(g, (1, 2, 0, 3)).reshape(g.shape[1], D_MODEL, D_IN)
        return jnp.pad(w, ((0, 0), (0, 0), (0, D_IN_PAD - D_IN)))

    g_win0, g_lcw, g_dcw = _two_level_gather([w_in[:1].astype(BF16), lru_conv_w, dn_conv_w], "gather_first")
    rest = _exchange_start([w_in[1:].astype(BF16), w_out.astype(BF16)], [False] * 2, "gather_rest_start")
    win = [full_w_in(g_win0)[0]]
    wout = None
    lcw = jnp.transpose(g_lcw, (1, 2, 0, 3)).reshape(depth, 4, D_MODEL)
    dcw = jnp.transpose(g_dcw, (1, 2, 0, 3)).reshape(depth, 4, 3 * D_MODEL)

    saved = []
    cur = xs
    for l in range(depth):
        nw_row = row(norm_w[l]) + rest["token"] if l == 0 else row(norm_w[l])
        hn = _rmsnorm_fwd(cur, nw_row, f"norm_fwd_{l}")
        proj = _matmul(hn, win[l], "nn", tm, 896, D_MODEL, f"in_proj_{l}")
        y_lru, hs = _lru_fwd(proj, lcw[l], row(lru_conv_b[l]), lru_wa[l], row(lru_ba[l]), lru_wx[l], row(lru_bx[l]),
                             row(lru_lambda[l]), row(lru_norm_w[l]), f"lru_fwd_{l}")
        ycat, o_dn, states, inverses = _dn_fwd(proj, y_lru, dcw[l], pad_row(dn_A_log[l]), pad_row(dn_dt_bias[l]),
                                     row(dn_norm_w[l]), f"dn_fwd_{l}")
        if l == 0:
            g_win_rest, g_wout = _exchange_wait(rest, ycat, "gather_rest_wait")
            win += list(full_w_in(g_win_rest))
            wout = jnp.transpose(g_wout, (1, 0, 2, 3)).reshape(depth, 2 * D_MODEL, D_MODEL)
        nxt = _matmul(ycat, wout[l], "nn", tm, D_MODEL, 2 * D_MODEL, f"out_proj_{l}", add=cur)
        saved.append((cur, hn, proj, hs, o_dn, states, inverses, ycat))
        cur = nxt
    loss_part, dx, d_final = _final_loss(cur, row(final_norm_w), loss_target[0], "final_loss")

    def win_slots(g):
        return jnp.transpose(g.reshape(D_MODEL, N_DEV, D_IN // N_DEV), (1, 0, 2))

    def wout_slots(g):
        return g.reshape(N_DEV, 2 * D_MODEL // N_DEV, D_MODEL)

    grads = {k: [None] * depth for k in _WEIGHTS if k not in ("final_norm_w", "w_in", "w_out")}
    started = {}
    token = None
    for l in reversed(range(depth)):
        x_in, hn, proj, hs, o_dn, states, inverses, ycat = saved[l]
        dy = _matmul(dx, wout[l], "nt", tm, D_MODEL, D_MODEL, f"out_proj_dy_{l}")
        g_wout_l = _matmul(ycat, dx, "tn", D_MODEL, D_MODEL, tm, f"out_proj_dw_{l}", out_dtype=BF16)
        if l == 0:
            started["w_out_0"] = _exchange_start([wout_slots(g_wout_l)], [True], "exchange_w_out_0_start")
            token = token + started["w_out_0"]["token"]
        cb_row = row(lru_conv_b[l]) if token is None else row(lru_conv_b[l]) + token
        (dlx, dlz, g_lcw, g_lcb, g_wa, g_ba, g_wx, g_bx, g_lam, g_lnw) = _lru_bwd(
            proj, hs, dy, lcw[l], cb_row, lru_wa[l], row(lru_ba[l]), lru_wx[l], row(lru_bx[l]),
            row(lru_lambda[l]), row(lru_norm_w[l]), f"lru_bwd_{l}")
        grads["lru_conv_w"][l] = _heads_to_channels(g_lcw)
        grads["lru_conv_b"][l] = g_lcb.reshape(D_MODEL)
        grads["lru_wa"][l] = g_wa
        grads["lru_ba"][l] = g_ba.reshape(D_MODEL)
        grads["lru_wx"][l] = g_wx
        grads["lru_bx"][l] = g_bx.reshape(D_MODEL)
        grads["lru_lambda"][l] = g_lam.reshape(D_MODEL)
        grads["lru_norm_w"][l] = g_lnw.reshape(D_MODEL)
        al_row = pad_row(dn_A_log[l])
        if l == 0:
            started["pack_0"] = _exchange_start([_pack_layer(grads, 0, names=_PACK_LRU)], [False],
                                                "exchange_pack_0_start")
            al_row = al_row + started["pack_0"]["token"]
        (dq, dk, dv, ddz, dba, g_dcw3, g_al, g_dt, g_dnw) = _dn_bwd(
            proj, o_dn, states, inverses, dy, dcw[l], al_row, pad_row(dn_dt_bias[l]), row(dn_norm_w[l]), f"dn_bwd_{l}")
        g_dcw3 = g_dcw3.reshape(HEADS, 3, 4, HEAD_DIM)
        grads["dn_conv_w"][l] = jnp.concatenate([_heads_to_channels(g_dcw3[:, i]) for i in range(3)], axis=1)
        grads["dn_A_log"][l] = g_al[0, :HEADS]
        grads["dn_dt_bias"][l] = g_dt[0, :HEADS]
        grads["dn_norm_w"][l] = g_dnw.reshape(HEAD_DIM)
        dep = None
        pieces = [dlx, dlz, dq, dk, dv, ddz]
        wide = len(pieces) * D_MODEL
        dba = dba.astype(BF16)
        g_win_l = jnp.concatenate(
            [_matmul_tn_parts(hn, pieces, D_MODEL, D_MODEL, tm, f"in_proj_dw_{l}", BF16),
             _matmul(hn, dba, "tn", D_MODEL, HEAD_DIM, tm, f"in_proj_dw_gates_{l}", out_dtype=BF16)[:, :D_IN - wide]],
            axis=1)
        if l == 0:
            started[0] = _exchange_start([win_slots(g_win_l)], [True], "exchange_0_start")
            dep = started[0]["token_block"]
        dh = _matmul_nt_parts(pieces, dba, win[l], tm, D_MODEL, f"in_proj_dh_{l}", dep=dep)
        dx, g_nw = _rmsnorm_bwd(x_in, row(norm_w[l]), dh, dx, f"norm_bwd_{l}")
        grads["norm_w"][l] = g_nw.reshape(D_MODEL)
        if l > 0:
            tail = (d_final, loss_part) if l == depth - 1 else ()
            started[l] = _exchange_start([win_slots(g_win_l), wout_slots(g_wout_l), _pack_layer(grads, l, tail)],
                                         [True, True, False], f"exchange_{l}_start")
            token = started[l]["token"]

    def conv_slots(a):
        dd, r, cc = a.shape
        return jnp.transpose(a.reshape(dd, r, N_DEV, cc // N_DEV), (2, 0, 1, 3))

    small = _exchange_start(
        [conv_slots(jnp.stack(grads["lru_conv_w"])), conv_slots(jnp.stack(grads["dn_conv_w"])),
         _pack_layer(grads, 0, names=_PACK_LAST)], [True, True, False], "exchange_small_start")

    new = {}
    flat_in = (depth * D_MODEL, D_IN // N_DEV)
    flat_out = (depth * 2 * D_MODEL // N_DEV, D_MODEL)
    zero_row = jnp.zeros((1, HEAD_DIM), F32)

    def adamw_pack(parts, layer, names=_LAYERED, name="adamw_small"):
        tails = [(t, zero_row) if layer == depth - 1 else () for t in (final_norm_w, m_final_norm_w, v_final_norm_w)]
        return _adamw(parts, _pack_layer(weights, layer, tails[0], names), _pack_layer(mom_m, layer, tails[1], names),
                      _pack_layer(mom_v, layer, tails[2], names), f"{name}_{layer}", parts.shape[1])

    def adamw_w_in(parts, layer, into):
        return _adamw(parts, w_in.reshape(flat_in), m_w_in.reshape(flat_in), v_w_in.reshape(flat_in),
                      f"adamw_w_in_{layer}", 256, layer * D_MODEL, into)

    def adamw_w_out(parts, layer, into):
        return _adamw(parts, w_out.reshape(flat_out), m_w_out.reshape(flat_out), v_w_out.reshape(flat_out),
                      f"adamw_w_out_{layer}", 256, layer * flat_out[0] // depth, into)

    acc_in = acc_out = None
    packs = [None] * depth
    after = small["token_block"]
    for l in reversed(range(1, depth)):
        r_win, r_wout, r_pack = _exchange_wait(started[l], after, f"exchange_{l}_wait")
        acc_in = adamw_w_in(r_win, l, acc_in)
        acc_out = adamw_w_out(r_wout, l, acc_out)
        packs[l] = adamw_pack(r_pack, l)
        after = packs[l][0]
    (r_wout,) = _exchange_wait(started["w_out_0"], after, "exchange_w_out_0_wait")
    acc_out = adamw_w_out(r_wout, 0, acc_out)
    (r_win,) = _exchange_wait(started[0], acc_out[0], "exchange_0_wait")
    acc_in = adamw_w_in(r_win, 0, acc_in)
    (r_pack,) = _exchange_wait(started["pack_0"], acc_in[0], "exchange_pack_0_wait")
    packs[0] = adamw_pack(r_pack, 0, _PACK_LRU)
    r_lcw, r_dcw, r_last = _exchange_wait(small, packs[0][0], "exchange_small_wait")
    for name, parts in (("lru_conv_w", r_lcw), ("dn_conv_w", r_dcw)):
        w = weights[name]
        flat = (-1, w.shape[-1])
        outs = _adamw(parts.reshape((N_DEV,) + (w.size // w.shape[-1], w.shape[-1])), w.reshape(flat),
                      mom_m[name].reshape(flat), mom_v[name].reshape(flat), f"adamw_{name}", 8)
        new[name] = [a.reshape(w.shape) for a in outs]
    last_0 = adamw_pack(r_last, 0, _PACK_LAST, "adamw_last")
    new["w_in"] = [a.reshape(w_in.shape) for a in acc_in]
    new["w_out"] = [a.reshape(w_out.shape) for a in acc_out]
    for i in range(4):
        layers = [{**_unpack_layer(packs[0][i], weights, _PACK_LRU)[0],
                   **_unpack_layer(last_0[i], weights, _PACK_LAST)[0]}]
        layers += [_unpack_layer(packs[l][i], weights)[0] for l in range(1, depth)]
        for name in _LAYERED:
            new.setdefault(name, []).append(jnp.stack([layer[name] for layer in layers]))
    tail_at = _unpack_layer(packs[depth - 1][0], weights)[1]
    rows_final = D_MODEL // HEAD_DIM
    new["final_norm_w"] = [packs[depth - 1][i][tail_at:tail_at + rows_final].reshape(D_MODEL) for i in range(4)]
    loss = packs[depth - 1][0][tail_at + rows_final, 0]
    out = [loss, dx.reshape(x.shape)]
    for i in range(4):
        out += [new[name][i] for name in _WEIGHTS]
    return tuple(out)
```

```python
import functools

import jax
import jax.numpy as jnp
from jax import lax
from jax.experimental import pallas as pl
from jax.experimental.pallas import tpu as pltpu

F32 = jnp.float32
BF16 = jnp.bfloat16

N_DEV = 8
D_MODEL = 1024
HEADS = 8
HEAD_DIM = 128
CHUNK = 64
D_IN = 6160
D_IN_PAD = 6272
COL_LRU_X, COL_LRU_Z, COL_Q, COL_K, COL_V, COL_DN_Z, COL_BA = 0, 8, 16, 24, 32, 40, 48
LRU_C = 8.0
EPS = 1e-6
ADAM_LR, ADAM_B1, ADAM_B2, ADAM_EPS, ADAM_WD, ADAM_STEP = 0.001, 0.9, 0.999, 1e-08, 0.01, 10
TIME_BLOCK = 1024
DN_TIME_BLOCK = 128
DN_GROUP = 8
VMEM_LIMIT = 56 * 1024 * 1024

NN = (((1,), (0,)), ((), ()))
NT = (((1,), (1,)), ((), ()))
TN = (((0,), (0,)), ((), ()))


B_NN = (((2,), (1,)), ((0,), (0,)))
B_NT = (((2,), (2,)), ((0,), (0,)))
B_TN = (((1,), (1,)), ((0,), (0,)))


def _split_bf16(x):
    hi = x.astype(BF16)
    return hi, (x - hi.astype(F32)).astype(BF16)


def _dot(a, b, dims, prec):
    if prec == "bf16":
        return lax.dot_general(a.astype(BF16), b.astype(BF16), dims, preferred_element_type=F32)
    a1, a2 = _split_bf16(a)
    b1, b2 = _split_bf16(b)
    dg = functools.partial(lax.dot_general, dimension_numbers=dims, preferred_element_type=F32)
    return dg(a1, b1) + (dg(a1, b2) + dg(a2, b1))


def _make_mm(prec, nn_dims, nt_dims, tn_dims):
    @jax.custom_vjp
    def nn(a, b):
        return _dot(a, b, nn_dims, prec)

    @jax.custom_vjp
    def nt(a, b):
        return _dot(a, b, nt_dims, prec)

    @jax.custom_vjp
    def tn(a, b):
        return _dot(a, b, tn_dims, prec)

    nn.defvjp(lambda a, b: (_dot(a, b, nn_dims, prec), (a, b)),
              lambda r, g: (_dot(g, r[1], nt_dims, prec), _dot(r[0], g, tn_dims, prec)))
    nt.defvjp(lambda a, b: (_dot(a, b, nt_dims, prec), (a, b)),
              lambda r, g: (_dot(g, r[1], nn_dims, prec), _dot(g, r[0], tn_dims, prec)))
    tn.defvjp(lambda a, b: (_dot(a, b, tn_dims, prec), (a, b)),
              lambda r, g: (_dot(r[1], g, nt_dims, prec), _dot(r[0], g, nn_dims, prec)))
    return nn, nt, tn


_NN_B, _NT_B, _TN_B = _make_mm("bf16", NN, NT, TN)
_BNN, _BNT, _BTN = _make_mm("bf16", B_NN, B_NT, B_TN)


@jax.custom_vjp
def _unit_lower_inverse(a):
    n = a.shape[-1]
    eye = (lax.broadcasted_iota(jnp.int32, a.shape, 1) == lax.broadcasted_iota(jnp.int32, a.shape, 2)).astype(F32)
    dg = functools.partial(lax.dot_general, dimension_numbers=B_NN, preferred_element_type=F32)
    inv = eye - a
    pw = _dot(a, a, B_NN, "bf16x3")
    steps = n.bit_length() - 2
    for j in range(steps):
        i1, i2 = _split_bf16(inv)
        p1, p2 = _split_bf16(pw)
        square = j + 1 < steps
        by_hi = dg(jnp.concatenate([i1, i2, p1, p2] if square else [i1, i2], axis=1), p1)
        by_lo = dg(jnp.concatenate([i1, p1], axis=1) if square else i1, p2)
        inv = inv + (by_hi[:, :n] + (by_lo[:, :n] + by_hi[:, n:2 * n]))
        if square:
            pw = by_hi[:, 2 * n:3 * n] + (by_lo[:, n:] + by_hi[:, 3 * n:])
    return inv


def _uli_fwd(a):
    inv = _unit_lower_inverse(a)
    return inv, inv


def _uli_bwd(inv, g):
    return (-_dot(_dot(inv, g, B_TN, "bf16"), inv, B_NT, "bf16"),)


_unit_lower_inverse.defvjp(_uli_fwd, _uli_bwd)


@jax.custom_vjp
def _known_inverse(a, inv):
    return inv


_known_inverse.defvjp(lambda a, inv: (inv, inv), lambda inv, g: (_uli_bwd(inv, g)[0], jnp.zeros_like(inv)))


def _rows2(y, m):
    return y[:, :m], y[:, m:]


@jax.custom_vjp
def _pair_nn(x1, x2, r):
    return _rows2(_dot(jnp.concatenate([x1, x2], axis=1), r, B_NN, "bf16"), x1.shape[1])


def _pair_nn_bwd(res, g):
    x1, x2, r = res
    g = jnp.concatenate(g, axis=1)
    dx1, dx2 = _rows2(_dot(g, r, B_NT, "bf16"), x1.shape[1])
    return dx1, dx2, _dot(jnp.concatenate([x1, x2], axis=1), g, B_TN, "bf16")


_pair_nn.defvjp(lambda x1, x2, r: (_pair_nn(x1, x2, r), (x1, x2, r)), _pair_nn_bwd)


@jax.custom_vjp
def _pair_nt(x1, x2, r):
    return _rows2(_dot(jnp.concatenate([x1, x2], axis=1), r, B_NT, "bf16"), x1.shape[1])


def _pair_nt_bwd(res, g):
    x1, x2, r = res
    g = jnp.concatenate(g, axis=1)
    dx1, dx2 = _rows2(_dot(g, r, B_NN, "bf16"), x1.shape[1])
    return dx1, dx2, _dot(g, jnp.concatenate([x1, x2], axis=1), B_TN, "bf16")


_pair_nt.defvjp(lambda x1, x2, r: (_pair_nt(x1, x2, r), (x1, x2, r)), _pair_nt_bwd)


@jax.custom_vjp
def _wide_nn(l, r1, r2):
    y = _dot(l, jnp.concatenate([r1, r2], axis=2), B_NN, "bf16")
    return y[:, :, :r1.shape[2]], y[:, :, r1.shape[2]:]


def _wide_nn_bwd(res, g):
    l, r1, r2 = res
    g = jnp.concatenate(g, axis=2)
    dr = _dot(l, g, B_TN, "bf16")
    return (_dot(g, jnp.concatenate([r1, r2], axis=2), B_NT, "bf16"), dr[:, :, :r1.shape[2]], dr[:, :, r1.shape[2]:])


_wide_nn.defvjp(lambda l, r1, r2: (_wide_nn(l, r1, r2), (l, r1, r2)), _wide_nn_bwd)


def _lower_ones(batch, n):
    shape = (batch, n, n)
    return (lax.broadcasted_iota(jnp.int32, shape, 1) >= lax.broadcasted_iota(jnp.int32, shape, 2)).astype(BF16)


@jax.custom_vjp
def _chunk_cumsum(g):
    tri = _lower_ones(g.shape[0], g.shape[1])
    g1, g2 = _split_bf16(g)
    g3 = (g - g1.astype(F32) - g2.astype(F32)).astype(BF16)
    dg = functools.partial(lax.dot_general, dimension_numbers=B_NN, preferred_element_type=F32)
    return dg(tri, g1) + (dg(tri, g2) + dg(tri, g3))


def _chunk_cumsum_bwd(_, ct):
    tri = _lower_ones(ct.shape[0], ct.shape[1])
    c1, c2 = _split_bf16(ct)
    dg = functools.partial(lax.dot_general, dimension_numbers=B_TN, preferred_element_type=F32)
    return (dg(tri, c1) + dg(tri, c2),)


_chunk_cumsum.defvjp(lambda g: (_chunk_cumsum(g), None), _chunk_cumsum_bwd)


def _expm1(x):
    small = x * (1.0 + x * (0.5 + x * (1.0 / 6 + x * (1.0 / 24 + x * (1.0 / 120 + x * (1.0 / 720))))))
    return jnp.where(jnp.abs(x) < 0.2, small, jnp.exp(x) - 1.0)


def _sigmoid(x):
    return 0.5 * jnp.tanh(0.5 * x) + 0.5


def _silu(x):
    return x * _sigmoid(x)


def _softplus(x):
    return jnp.maximum(x, 0.0) + jnp.log1p(jnp.exp(-jnp.abs(x)))


def _rmsnorm(x, w):
    return x * lax.rsqrt(jnp.mean(x * x, axis=-1, keepdims=True) + EPS) * w


def _gated_norm(o, z, w):
    return o * lax.rsqrt(jnp.mean(o * o, axis=-1, keepdims=True) + EPS) * w * _silu(z)


def _lru_gates(xc, wa, ba, wx, bx, lam):
    r = 1.0 / (1.0 + jnp.exp(-(_NN_B(xc, wa) + ba)))
    i = _sigmoid(_NN_B(xc, wx) + bx)
    log_a = -LRU_C * r * _softplus(-lam)
    a = jnp.exp(log_a)
    mult = jnp.sqrt(-_expm1(2.0 * log_a))
    return a, mult * (i * xc)


SCAN_ROWS = 32


def _scan_forward(a, b, h0):
    rows = a.shape[0]
    piece = min(SCAN_ROWS, rows)
    pos = lax.broadcasted_iota(jnp.int32, a.shape, 0) % piece
    k = 1
    while k < piece:
        seen = pos >= k
        b = jnp.where(seen, a * pltpu.roll(b, k, 0) + b, b)
        a = jnp.where(seen, a * pltpu.roll(a, k, 0), a)
        k *= 2
    out, entering = [], h0
    for lo in range(0, rows, piece):
        out.append(b[lo:lo + piece] + a[lo:lo + piece] * entering)
        entering = out[-1][piece - 1:piece, :]
    return jnp.concatenate(out, axis=0)


def _scan_reverse(a, d, carry):
    rows = a.shape[0]
    piece = min(SCAN_ROWS, rows)
    row = lax.broadcasted_iota(jnp.int32, a.shape, 0)
    pos = row % piece
    last = row == rows - 1
    c = jnp.where(last, 0.0, pltpu.roll(a, rows - 1, 0))
    d = d + jnp.where(last, carry, 0.0)
    k = 1
    while k < piece:
        seen = pos < piece - k
        d = jnp.where(seen, d + c * pltpu.roll(d, rows - k, 0), d)
        c = jnp.where(seen, c * pltpu.roll(c, rows - k, 0), c)
        k *= 2
    out, following = [], jnp.zeros_like(carry)
    for lo in reversed(range(0, rows, piece)):
        out.insert(0, d[lo:lo + piece] + c[lo:lo + piece] * following)
        following = out[0][0:1, :]
    return jnp.concatenate(out, axis=0)


def _lane_pick(row, lane_index):
    lane = lax.broadcasted_iota(jnp.int32, row.shape, 1)
    return jnp.sum(jnp.where(lane == lane_index, row, 0.0), axis=-1, keepdims=True)


def _dn_prep(qc, kc, vc, ba, a_log_row, dt_row, head):
    q = _silu(qc)
    k = _silu(kc)
    v = _silu(vc)
    q = q * lax.rsqrt(jnp.sum(q * q, axis=-1, keepdims=True) + EPS) * (HEAD_DIM ** -0.5)
    k = k * lax.rsqrt(jnp.sum(k * k, axis=-1, keepdims=True) + EPS)
    beta = _sigmoid(_lane_pick(ba, head))
    g = -jnp.exp(_lane_pick(a_log_row, head)) * _softplus(_lane_pick(ba, HEADS + head) + _lane_pick(dt_row, head))
    return q, k, v, g, beta


def _dn_chunks_head(q, k, v, gcol, bcol, inverse=None):
    n, c, d = q.shape
    row = lax.broadcasted_iota(jnp.int32, (n, c, c), 1)
    col = lax.broadcasted_iota(jnp.int32, (n, c, c), 2)
    g_wide = jnp.broadcast_to(gcol, (n, c, d))
    b_wide = jnp.broadcast_to(bcol, (n, c, d))
    gc = _chunk_cumsum(g_wide)
    gc_rows = gc[:, :, :c]
    decay = jnp.exp(jnp.where(row >= col, gc_rows - jnp.swapaxes(gc_rows, 1, 2), -1e30))
    kb = k * b_wide
    eg = jnp.exp(gc)
    kbk, qk = _pair_nt(kb, q, k)
    a = jnp.where(row > col, kbk * decay, 0.0)
    tinv = _unit_lower_inverse(a) if inverse is None else _known_inverse(a, inverse)
    u, w = _wide_nn(tinv, v * b_wide, kb * eg)
    g_last = jnp.sum(g_wide, axis=1, keepdims=True)
    return u, w, qk * decay, q * eg, k * jnp.exp(g_last - gc), jnp.exp(g_last), tinv


def _dn_chunks(inverse, q, k, v, gcol, bcol, states):
    u, w, attn, qe, kdec, eglast, _ = _dn_chunks_head(q, k, v, gcol, bcol, inverse)
    w_st, qe_st = _pair_nn(w, qe, states)
    v_new = u - w_st
    o = qe_st + _BNN(attn, v_new)
    return (o, states * eglast + _BTN(kdec, v_new)), (w, attn, qe, kdec, eglast)


def _conv_taps(buf, head, cw, rows):
    acc = cw[0:1, :] * buf[head, pl.ds(5, rows), :]
    for j in range(1, 4):
        acc = acc + cw[j:j + 1, :] * buf[head, pl.ds(5 + j, rows), :]
    return acc


def _conv_backward(dbuf, dhead, xbuf, xhead, cw, dxc, rows):
    dbuf[dhead, pl.ds(0, rows), :] = dxc
    dx = cw[0:1, :] * dbuf[dhead, pl.ds(3, rows), :]
    for j in range(1, 4):
        dx = dx + cw[j:j + 1, :] * dbuf[dhead, pl.ds(3 - j, rows), :]
    dcw = jnp.concatenate(
        [jnp.sum(dxc * xbuf[xhead, pl.ds(5 + j, rows), :], axis=0, keepdims=True) for j in range(4)], axis=0)
    dbuf[dhead, pl.ds(rows, 8), :] = dbuf[dhead, pl.ds(0, 8), :]
    return dx, dcw


def _params(**kw):
    return pltpu.CompilerParams(vmem_limit_bytes=VMEM_LIMIT, **kw)


def _matmul(a, b, form, tm, tn, tk, name, add=None, out_dtype=F32, dep=None, norm_w=None):
    if form == "nn":
        (m, kdim), (_, n) = a.shape, b.shape
        a_spec = pl.BlockSpec((tm, tk), lambda j, i, k: (i, k))
        b_spec = pl.BlockSpec((tk, tn), lambda j, i, k: (k, j))
        dims = NN
    elif form == "nt":
        (m, kdim), (n, _) = a.shape, b.shape
        a_spec = pl.BlockSpec((tm, tk), lambda j, i, k: (i, k))
        b_spec = pl.BlockSpec((tn, tk), lambda j, i, k: (j, k))
        dims = NT
    else:
        (kdim, m), (_, n) = a.shape, b.shape
        a_spec = pl.BlockSpec((tk, tm), lambda j, i, k: (k, i))
        b_spec = pl.BlockSpec((tk, tn), lambda j, i, k: (k, j))
        dims = TN
    assert m % tm == 0 and n % tn == 0 and kdim % tk == 0, (name, m, n, kdim, tm, tn, tk)
    ksteps = kdim // tk
    o_spec = pl.BlockSpec((tm, tn), lambda j, i, k: (i, j))
    has_add = add is not None
    extra = [] if dep is None else [dep]
    has_norm = norm_w is not None
    assert not has_norm or tn == n, (name, tn, n)

    def body(*refs):
        a_ref, b_ref = refs[:2]
        c_ref = refs[2] if has_add else None
        o_ref, acc = (refs[-3], refs[-1]) if has_norm else refs[-2:]
        k = pl.program_id(2)

        @pl.when(k == 0)
        def _():
            acc[...] = c_ref[...] if has_add else jnp.zeros_like(acc)

        acc[...] += lax.dot_general(a_ref[...].astype(BF16), b_ref[...].astype(BF16), dims,
                                    preferred_element_type=F32)

        @pl.when(k == ksteps - 1)
        def _():
            o_ref[...] = acc[...].astype(o_ref.dtype)
            if has_norm:
                refs[-2][...] = _rmsnorm(acc[...], refs[-4][...]).astype(BF16)

    in_specs = [a_spec, b_spec] + ([o_spec] if has_add else []) + [pl.BlockSpec((8, HEAD_DIM), lambda j, i, k: (0, 0))
                                                                   for _ in extra]
    args = (a, b) + ((add,) if has_add else ()) + tuple(extra)
    out_specs, out_shape = o_spec, jax.ShapeDtypeStruct((m, n), out_dtype)
    if has_norm:
        in_specs, args = in_specs + [pl.BlockSpec((1, tn), lambda j, i, k: (0, j))], args + (norm_w,)
        out_specs, out_shape = [o_spec, o_spec], [out_shape, jax.ShapeDtypeStruct((m, n), BF16)]
    return pl.pallas_call(
        body, name=name, grid=(n // tn, m // tm, ksteps), in_specs=in_specs, out_specs=out_specs,
        out_shape=out_shape, scratch_shapes=[pltpu.VMEM((tm, tn), F32)],
        compiler_params=_params(dimension_semantics=("parallel", "parallel", "arbitrary")),
    )(*args)


def _matmul_nt_parts(parts, narrow, w, tm, tn, name, dep=None):
    m, c = parts[0].shape
    c2 = narrow.shape[1]
    n = w.shape[0]
    count = len(parts)
    assert m % tm == 0 and n % tn == 0 and all(p.shape == (m, c) for p in parts) and (count * c) % c2 == 0, (name, m, n)
    extra = [] if dep is None else [dep]

    def body(*refs):
        part_refs, narrow_ref, w_ref, w2_ref = refs[:count], refs[count], refs[count + 1], refs[count + 2]
        o_ref, acc = refs[-2:]
        k = pl.program_id(2)

        @pl.when(k == 0)
        def _():
            acc[...] = jnp.zeros_like(acc)

        for p in range(count):
            @pl.when(k == p)
            def _(p=p):
                acc[...] += lax.dot_general(part_refs[p][...].astype(BF16), w_ref[...].astype(BF16), NT,
                                            preferred_element_type=F32)

        @pl.when(k == count)
        def _():
            o_ref[...] = acc[...] + lax.dot_general(narrow_ref[...].astype(BF16), w2_ref[...].astype(BF16), NT,
                                                    preferred_element_type=F32)

    in_specs = ([pl.BlockSpec((tm, c), lambda j, i, k: (i, 0))] * count
                + [pl.BlockSpec((tm, c2), lambda j, i, k: (i, 0)),
                   pl.BlockSpec((tn, c), lambda j, i, k: (j, jnp.minimum(k, count - 1))),
                   pl.BlockSpec((tn, c2), lambda j, i, k: (j, count * c // c2))]
                + [pl.BlockSpec((8, HEAD_DIM), lambda j, i, k: (0, 0)) for _ in extra])
    return pl.pallas_call(
        body, name=name, grid=(n // tn, m // tm, count + 1), in_specs=in_specs,
        out_specs=pl.BlockSpec((tm, tn), lambda j, i, k: (i, j)),
        out_shape=jax.ShapeDtypeStruct((m, n), F32), scratch_shapes=[pltpu.VMEM((tm, tn), F32)],
        compiler_params=_params(dimension_semantics=("parallel", "parallel", "arbitrary")),
    )(*parts, narrow, w, w, *extra)


def _matmul_tn_parts(a, parts, tm, tn, tk, name, out_dtype):
    kdim, m = a.shape
    c = parts[0].shape[1]
    count = len(parts)
    per = c // tn
    assert m % tm == 0 and c % tn == 0 and kdim % tk == 0 and all(p.shape == (kdim, c) for p in parts), (name, m, c)
    ksteps = kdim // tk

    def body(*refs):
        a_ref, part_refs = refs[0], refs[1:1 + count]
        o_ref, acc = refs[-2:]
        j, k = pl.program_id(0), pl.program_id(2)

        @pl.when(k == 0)
        def _():
            acc[...] = jnp.zeros_like(acc)

        for p in range(count):
            @pl.when(j // per == p)
            def _(p=p):
                acc[...] += lax.dot_general(a_ref[...].astype(BF16), part_refs[p][...].astype(BF16), TN,
                                            preferred_element_type=F32)

        @pl.when(k == ksteps - 1)
        def _():
            o_ref[...] = acc[...].astype(o_ref.dtype)

    def part_spec(p):
        return pl.BlockSpec((tk, tn), lambda j, i, k: (jnp.where(j // per == p, k, 0), jnp.where(j // per == p, j % per, 0)))

    return pl.pallas_call(
        body, name=name, grid=(count * per, m // tm, ksteps),
        in_specs=[pl.BlockSpec((tk, tm), lambda j, i, k: (k, i))] + [part_spec(p) for p in range(count)],
        out_specs=pl.BlockSpec((tm, tn), lambda j, i, k: (i, j)),
        out_shape=jax.ShapeDtypeStruct((m, count * c), out_dtype), scratch_shapes=[pltpu.VMEM((tm, tn), F32)],
        compiler_params=_params(dimension_semantics=("parallel", "parallel", "arbitrary")),
    )(a, *parts)


def _rmsnorm_fwd(x, w_row, name):
    s = x.shape[0]
    tb = min(TIME_BLOCK, s)

    def body(x_ref, w_ref, o_ref):
        o_ref[...] = _rmsnorm(x_ref[...], w_ref[...]).astype(BF16)

    return pl.pallas_call(
        body, name=name, grid=(s // tb,),
        in_specs=[pl.BlockSpec((tb, D_MODEL), lambda i: (i, 0)), pl.BlockSpec((1, D_MODEL), lambda i: (0, 0))],
        out_specs=pl.BlockSpec((tb, D_MODEL), lambda i: (i, 0)),
        out_shape=jax.ShapeDtypeStruct((s, D_MODEL), BF16), compiler_params=_params(),
    )(x, w_row)


def _rmsnorm_bwd(x, w_row, dh, dres, name):
    s = x.shape[0]
    tb = min(TIME_BLOCK, s)

    def body(x_ref, w_ref, dh_ref, dres_ref, dx_ref, dw_ref):
        _, vjp = jax.vjp(_rmsnorm, x_ref[...], w_ref[...])
        dx, dw = vjp(dh_ref[...])
        dx_ref[...] = dres_ref[...] + dx

        @pl.when(pl.program_id(0) == 0)
        def _():
            dw_ref[...] = jnp.zeros_like(dw_ref)

        dw_ref[...] += dw

    row = pl.BlockSpec((tb, D_MODEL), lambda i: (i, 0))
    vec = pl.BlockSpec((1, D_MODEL), lambda i: (0, 0))
    return pl.pallas_call(
        body, name=name, grid=(s // tb,), in_specs=[row, vec, row, row], out_specs=[row, vec],
        out_shape=[jax.ShapeDtypeStruct((s, D_MODEL), F32), jax.ShapeDtypeStruct((1, D_MODEL), F32)],
        compiler_params=_params(),
    )(x, w_row, dh, dres)


def _final_loss(x, w_row, target, name):
    s = x.shape[0]
    tb = min(TIME_BLOCK, s)

    def loss_fn(xv, wv, tv):
        err = _rmsnorm(xv, wv) - tv
        return 0.5 * jnp.sum(jnp.sum(err * err, axis=-1, keepdims=True), axis=0, keepdims=True) * (1.0 / D_MODEL)

    def body(x_ref, w_ref, t_ref, loss_ref, dx_ref, dw_ref):
        tv = t_ref[...]
        loss, vjp = jax.vjp(lambda xv, wv: loss_fn(xv, wv, tv), x_ref[...], w_ref[...])
        dx, dw = vjp(jnp.ones((1, 1), F32))
        dx_ref[...] = dx

        @pl.when(pl.program_id(0) == 0)
        def _():
            dw_ref[...] = jnp.zeros_like(dw_ref)
            loss_ref[...] = jnp.zeros_like(loss_ref)

        dw_ref[...] += dw
        loss_ref[...] += jnp.broadcast_to(loss, loss_ref.shape)

    row = pl.BlockSpec((tb, D_MODEL), lambda i: (i, 0))
    vec = pl.BlockSpec((1, D_MODEL), lambda i: (0, 0))
    return pl.pallas_call(
        body, name=name, grid=(s // tb,), in_specs=[row, vec, row],
        out_specs=[pl.BlockSpec((1, HEAD_DIM), lambda i: (0, 0)), row, vec],
        out_shape=[jax.ShapeDtypeStruct((1, HEAD_DIM), F32), jax.ShapeDtypeStruct((s, D_MODEL), F32),
                   jax.ShapeDtypeStruct((1, D_MODEL), F32)],
        compiler_params=_params(),
    )(x, w_row, target)


def _head_specs(tb, time_of):
    def col(off):
        return pl.BlockSpec((tb, HEAD_DIM), lambda t, h: (time_of(t), off + h))
    return col


def _vec_spec():
    return pl.BlockSpec((1, HEAD_DIM), lambda t, h: (0, h))


def _lru_fwd(proj, conv_w, conv_b, wa, ba, wx, bx, lam, nw, name):
    s = proj.shape[0]
    tb = min(TIME_BLOCK, s)
    nt = s // tb
    col = _head_specs(tb, lambda t: t)

    def body(x_ref, z_ref, cw_ref, cb_ref, wa_ref, ba_ref, wx_ref, bx_ref, lam_ref, nw_ref,
             y_ref, hs_ref, xbuf, hcar):
        t, h = pl.program_id(0), pl.program_id(1)

        @pl.when(t == 0)
        def _():
            xbuf[h, pl.ds(0, 8), :] = jnp.zeros((8, HEAD_DIM), F32)
            hcar[h] = jnp.zeros((8, HEAD_DIM), F32)

        xbuf[h, pl.ds(8, tb), :] = x_ref[...]
        xc = _conv_taps(xbuf, h, cw_ref[...], tb) + cb_ref[...]
        a, b = _lru_gates(xc, wa_ref[...], ba_ref[...], wx_ref[...], bx_ref[...], lam_ref[...])
        hs_ref[...] = _scan_forward(a, b, hcar[h, pl.ds(0, 1), :])
        hcar[h, pl.ds(0, 1), :] = hs_ref[pl.ds(tb - 1, 1), :]
        xbuf[h, pl.ds(0, 8), :] = xbuf[h, pl.ds(tb, 8), :]
        y_ref[...] = _gated_norm(hs_ref[...], z_ref[...], nw_ref[...]).astype(BF16)

    vec = _vec_spec()
    return pl.pallas_call(
        body, name=name, grid=(nt, HEADS),
        in_specs=[col(COL_LRU_X), col(COL_LRU_Z), pl.BlockSpec((4, HEAD_DIM), lambda t, h: (0, h)), vec,
                  pl.BlockSpec((None, HEAD_DIM, HEAD_DIM), lambda t, h: (h, 0, 0)), vec,
                  pl.BlockSpec((None, HEAD_DIM, HEAD_DIM), lambda t, h: (h, 0, 0)), vec, vec, vec],
        out_specs=[col(0), col(0)],
        out_shape=[jax.ShapeDtypeStruct((s, 2 * D_MODEL), BF16), jax.ShapeDtypeStruct((s, D_MODEL), F32)],
        scratch_shapes=[pltpu.VMEM((HEADS, tb + 8, HEAD_DIM), F32), pltpu.VMEM((HEADS, 8, HEAD_DIM), F32)],
        compiler_params=_params(dimension_semantics=("arbitrary", "arbitrary")),
    )(proj, proj, conv_w, conv_b, wa, ba, wx, bx, lam, nw)


def _halo_spec(tb, nt, off):
    per = tb // 8
    return pl.BlockSpec((8, HEAD_DIM), lambda t, h: (jnp.maximum((nt - 1 - t) * per - 1, 0), off + h))


def _lru_bwd(proj, hs, dy, conv_w, conv_b, wa, ba, wx, bx, lam, nw, name):
    s = proj.shape[0]
    tb = min(TIME_BLOCK, s)
    nt = s // tb
    col = _head_specs(tb, lambda t: nt - 1 - t)

    def body(x_ref, xh_ref, z_ref, hs_ref, hh_ref, dy_ref, cw_ref, cb_ref, wa_ref, ba_ref, wx_ref, bx_ref,
             lam_ref, nw_ref, dx_ref, dz_ref, dcw_ref, dcb_ref, dwa_ref, dba_ref, dwx_ref, dbx_ref, dlam_ref,
             dnw_ref, xbuf, hbuf, dbuf, gcar):
        t, h = pl.program_id(0), pl.program_id(1)
        first_block = t == nt - 1

        @pl.when(t == 0)
        def _():
            dbuf[h, pl.ds(tb, 8), :] = jnp.zeros((8, HEAD_DIM), F32)
            gcar[h] = jnp.zeros((8, HEAD_DIM), F32)
            dcw_ref[h] = jnp.zeros((4, HEAD_DIM), F32)
            dwa_ref[h] = jnp.zeros((HEAD_DIM, HEAD_DIM), F32)
            dwx_ref[h] = jnp.zeros((HEAD_DIM, HEAD_DIM), F32)
            for ref in (dcb_ref, dba_ref, dbx_ref, dlam_ref, dnw_ref):
                ref[h] = jnp.zeros((1, HEAD_DIM), F32)

        keep = jnp.where(first_block, 0.0, 1.0)
        xbuf[0, pl.ds(0, 8), :] = xh_ref[...] * keep
        xbuf[0, pl.ds(8, tb), :] = x_ref[...]
        hbuf[pl.ds(0, 8), :] = hh_ref[...] * keep
        hbuf[pl.ds(8, tb), :] = hs_ref[...]
        cw = cw_ref[...]
        xc = _conv_taps(xbuf, 0, cw, tb) + cb_ref[...]
        (a, _), gates_vjp = jax.vjp(_lru_gates, xc, wa_ref[...], ba_ref[...], wx_ref[...], bx_ref[...], lam_ref[...])
        _, norm_vjp = jax.vjp(_gated_norm, hs_ref[...], z_ref[...], nw_ref[...])
        dh, dz, dnw = norm_vjp(dy_ref[...])
        dz_ref[...] = dz.astype(dz_ref.dtype)
        g = _scan_reverse(a, dh, gcar[h, pl.ds(0, 1), :])
        gcar[h, pl.ds(0, 1), :] = a[0:1, :] * g[0:1, :]
        dxc, dwa, dba, dwx, dbx, dlam = gates_vjp((g * hbuf[pl.ds(7, tb), :], g))
        dx, dcw = _conv_backward(dbuf, h, xbuf, 0, cw, dxc, tb)
        dx_ref[...] = dx.astype(dx_ref.dtype)
        dcw_ref[h] += dcw
        dcb_ref[h] += jnp.sum(dxc, axis=0, keepdims=True)
        dwa_ref[h] += dwa
        dwx_ref[h] += dwx
        dba_ref[h] += dba
        dbx_ref[h] += dbx
        dlam_ref[h] += dlam
        dnw_ref[h] += dnw

    vec = _vec_spec()
    mat = pl.BlockSpec((None, HEAD_DIM, HEAD_DIM), lambda t, h: (h, 0, 0))

    def whole(shape):
        return pl.BlockSpec(shape, lambda t, h: (0,) * len(shape))

    head_vec = jax.ShapeDtypeStruct((HEADS, 1, HEAD_DIM), F32)
    head_mat = jax.ShapeDtypeStruct((HEADS, HEAD_DIM, HEAD_DIM), F32)
    return pl.pallas_call(
        body, name=name, grid=(nt, HEADS),
        in_specs=[col(COL_LRU_X), _halo_spec(tb, nt, COL_LRU_X), col(COL_LRU_Z), col(0), _halo_spec(tb, nt, 0), col(0),
                  pl.BlockSpec((4, HEAD_DIM), lambda t, h: (0, h)), vec, mat, vec, mat, vec, vec, vec],
        out_specs=[col(0), col(0), whole((HEADS, 4, HEAD_DIM)), whole((HEADS, 1, HEAD_DIM)),
                   whole((HEADS, HEAD_DIM, HEAD_DIM)), whole((HEADS, 1, HEAD_DIM)),
                   whole((HEADS, HEAD_DIM, HEAD_DIM)), whole((HEADS, 1, HEAD_DIM)), whole((HEADS, 1, HEAD_DIM)),
                   whole((HEADS, 1, HEAD_DIM))],
        out_shape=[jax.ShapeDtypeStruct((s, D_MODEL), BF16), jax.ShapeDtypeStruct((s, D_MODEL), BF16),
                   jax.ShapeDtypeStruct((HEADS, 4, HEAD_DIM), F32), head_vec, head_mat, head_vec, head_mat, head_vec,
                   head_vec, head_vec],
        scratch_shapes=[pltpu.VMEM((1, tb + 8, HEAD_DIM), F32), pltpu.VMEM((tb + 8, HEAD_DIM), F32),
                        pltpu.VMEM((HEADS, tb + 8, HEAD_DIM), F32), pltpu.VMEM((HEADS, 8, HEAD_DIM), F32)],
        compiler_params=_params(dimension_semantics=("arbitrary", "arbitrary")),
    )(proj, proj, proj, hs, hs, dy, conv_w, conv_b, wa, ba, wx, bx, lam, nw)


def _group_col(tb, time_of):
    def col(off):
        return pl.BlockSpec((tb, DN_GROUP * HEAD_DIM), lambda t, hg: (time_of(t), off // DN_GROUP + hg))
    return col


def _dn_fwd(proj, y, conv_w, a_log_row, dt_row, nw, name):
    s = proj.shape[0]
    tb = min(DN_TIME_BLOCK, s)
    nt = s // tb
    nchunk = tb // CHUNK
    grp = DN_GROUP
    col = _group_col(tb, lambda t: t)

    def body(q_ref, k_ref, v_ref, z_ref, ba_ref, cwq_ref, cwk_ref, cwv_ref, al_ref, dt_ref, nw_ref, y_in_ref,
             y_ref, o_ref, st_ref, inv_ref, xbuf, state):
        t, hg = pl.program_id(0), pl.program_id(1)

        def chunks(a):
            return a.reshape(nchunk, CHUNK, a.shape[-1])

        prepared = []
        for gi in range(grp):
            h = hg * grp + gi
            lanes = slice(gi * HEAD_DIM, (gi + 1) * HEAD_DIM)

            @pl.when(t == 0)
            def _(h=h):
                for i in range(3):
                    xbuf[3 * h + i, pl.ds(0, 8), :] = jnp.zeros((8, HEAD_DIM), F32)
                state[h] = jnp.zeros((HEAD_DIM, HEAD_DIM), F32)

            conv = []
            for i, (ref, cw_ref) in enumerate(((q_ref, cwq_ref), (k_ref, cwk_ref), (v_ref, cwv_ref))):
                xbuf[3 * h + i, pl.ds(8, tb), :] = ref[:, lanes]
                conv.append(_conv_taps(xbuf, 3 * h + i, cw_ref[:, lanes], tb))
                xbuf[3 * h + i, pl.ds(0, 8), :] = xbuf[3 * h + i, pl.ds(tb, 8), :]
            prepared.append([chunks(a) for a in
                             _dn_prep(conv[0], conv[1], conv[2], ba_ref[...], al_ref[...], dt_ref[...], h)])
        qs, ks, vs, gs, bs = [jnp.concatenate([p[i] for p in prepared], axis=0) for i in range(5)]
        u, w, attn, qe, kdec, eglast, tinv = _dn_chunks_head(qs, ks, vs, gs, bs)
        inv_ref[...] = tinv.reshape(grp, nchunk, CHUNK, CHUNK)
        w_u = jnp.concatenate([w, u], axis=2)
        kdec_w_u = _dot(kdec, w_u, B_TN, "bf16")
        attn_w_u = _dot(attn, w_u, B_NN, "bf16")
        st = [state[hg * grp + gi] for gi in range(grp)]
        for c in range(nchunk):
            for gi in range(grp):
                n = gi * nchunk + c
                st_ref[gi, c] = st[gi]
                st[gi] = st[gi] * eglast[n] - _NN_B(kdec_w_u[n, :, :HEAD_DIM], st[gi]) + kdec_w_u[n, :, HEAD_DIM:]
        for gi in range(grp):
            state[hg * grp + gi] = st[gi]
        states = st_ref[...].reshape(grp * nchunk, HEAD_DIM, HEAD_DIM)
        o = _dot(qe - attn_w_u[:, :, :HEAD_DIM], states, B_NN, "bf16") + attn_w_u[:, :, HEAD_DIM:]
        for gi in range(grp):
            lanes = slice(gi * HEAD_DIM, (gi + 1) * HEAD_DIM)
            o_head = o[gi * nchunk:(gi + 1) * nchunk].reshape(tb, HEAD_DIM)
            o_ref[:, lanes] = o_head
            y_ref[:, lanes] = _gated_norm(o_head, z_ref[:, lanes], nw_ref[...]).astype(BF16)

    def cw_spec(off):
        return pl.BlockSpec((4, grp * HEAD_DIM), lambda t, hg: (0, off // grp + hg))

    row128 = pl.BlockSpec((1, HEAD_DIM), lambda t, hg: (0, 0))
    return pl.pallas_call(
        body, name=name, grid=(nt, HEADS // grp),
        in_specs=[col(COL_Q), col(COL_K), col(COL_V), col(COL_DN_Z),
                  pl.BlockSpec((tb, HEAD_DIM), lambda t, hg: (t, COL_BA)),
                  cw_spec(0), cw_spec(HEADS), cw_spec(2 * HEADS), row128, row128, row128,
                  pl.BlockSpec(memory_space=pl.ANY)],
        out_specs=[col(HEADS), col(0),
                   pl.BlockSpec((grp, nchunk, HEAD_DIM, HEAD_DIM), lambda t, hg: (hg, t, 0, 0)),
                   pl.BlockSpec((grp, nchunk, CHUNK, CHUNK), lambda t, hg: (hg, t, 0, 0))],
        out_shape=[jax.ShapeDtypeStruct((s, 2 * D_MODEL), BF16), jax.ShapeDtypeStruct((s, D_MODEL), F32),
                   jax.ShapeDtypeStruct((HEADS, s // CHUNK, HEAD_DIM, HEAD_DIM), F32),
                   jax.ShapeDtypeStruct((HEADS, s // CHUNK, CHUNK, CHUNK), F32)],
        input_output_aliases={11: 0},
        scratch_shapes=[pltpu.VMEM((3 * HEADS, tb + 8, HEAD_DIM), F32), pltpu.VMEM((HEADS, HEAD_DIM, HEAD_DIM), F32)],
        compiler_params=_params(dimension_semantics=("arbitrary", "arbitrary")),
    )(proj, proj, proj, proj, proj, conv_w, conv_w, conv_w, a_log_row, dt_row, nw, y)


def _dn_bwd(proj, o, states, inverses, dy, conv_w, a_log_row, dt_row, nw, name):
    s = proj.shape[0]
    tb = min(DN_TIME_BLOCK, s)
    nt = s // tb
    nchunk = tb // CHUNK
    grp = DN_GROUP
    col = _group_col(tb, lambda t: nt - 1 - t)

    def body(q_ref, qh_ref, k_ref, kh_ref, v_ref, vh_ref, z_ref, ba_ref, o_ref, st_ref, inv_ref, dy_ref,
             cwq_ref, cwk_ref, cwv_ref, al_ref, dt_ref, nw_ref,
             dq_ref, dk_ref, dv_ref, dz_ref, dba_ref, dcw_ref, dal_ref, ddt_ref, dnw_ref,
             xbuf, dbuf, dstate, dst_s):
        t, hg = pl.program_id(0), pl.program_id(1)
        keep = jnp.where(t == nt - 1, 0.0, 1.0)

        @pl.when((t == 0) & (hg == 0))
        def _():
            for ref in (dal_ref, ddt_ref, dnw_ref):
                ref[...] = jnp.zeros_like(ref)

        def chunks(a):
            return a.reshape(nchunk, CHUNK, a.shape[-1])

        prepared, prep_vjps, dos = [], [], []
        for gi in range(grp):
            h = hg * grp + gi
            lanes = slice(gi * HEAD_DIM, (gi + 1) * HEAD_DIM)

            @pl.when(t == 0)
            def _(h=h):
                for i in range(3):
                    dbuf[3 * h + i, pl.ds(tb, 8), :] = jnp.zeros((8, HEAD_DIM), F32)
                    dcw_ref[3 * h + i] = jnp.zeros((4, HEAD_DIM), F32)
                dstate[h] = jnp.zeros((HEAD_DIM, HEAD_DIM), F32)

            conv = []
            for i, (ref, halo, cw_ref) in enumerate(((q_ref, qh_ref, cwq_ref), (k_ref, kh_ref, cwk_ref),
                                                     (v_ref, vh_ref, cwv_ref))):
                xbuf[3 * gi + i, pl.ds(0, 8), :] = halo[:, lanes] * keep
                xbuf[3 * gi + i, pl.ds(8, tb), :] = ref[:, lanes]
                conv.append(_conv_taps(xbuf, 3 * gi + i, cw_ref[:, lanes], tb))
            outs, prep_vjp = jax.vjp(
                lambda qc, kc, vc, ba, al, dt, h=h: _dn_prep(qc, kc, vc, ba, al, dt, h),
                conv[0], conv[1], conv[2], ba_ref[...], al_ref[...], dt_ref[...])
            prepared.append([chunks(a) for a in outs])
            prep_vjps.append(prep_vjp)
            _, norm_vjp = jax.vjp(_gated_norm, o_ref[:, lanes], z_ref[:, lanes], nw_ref[...])
            do, dz, dnw = norm_vjp(dy_ref[:, lanes])
            dz_ref[:, lanes] = dz.astype(dz_ref.dtype)
            dnw_ref[...] += dnw
            dos.append(chunks(do))
        qs, ks, vs, gs, bs = [jnp.concatenate([p[i] for p in prepared], axis=0) for i in range(5)]
        do = jnp.concatenate(dos, axis=0)
        states_in = st_ref[...].reshape(grp * nchunk, HEAD_DIM, HEAD_DIM)
        kept = inv_ref[...].reshape(grp * nchunk, CHUNK, CHUNK)
        _, chunks_vjp, (w, attn, qe, kdec, eglast) = jax.vjp(
            functools.partial(_dn_chunks, kept), qs, ks, vs, gs, bs, states_in, has_aux=True)
        kdec_w = _dot(kdec, w, B_TN, "bf16")
        fixed = _dot(qe, do, B_TN, "bf16") - _dot(w, _dot(attn, do, B_TN, "bf16"), B_TN, "bf16")
        dst = [dstate[hg * grp + gi] for gi in range(grp)]
        for c in reversed(range(nchunk)):
            for gi in range(grp):
                n = gi * nchunk + c
                dst_s[n] = dst[gi]
                dst[gi] = dst[gi] * eglast[n] - _dot(kdec_w[n], dst[gi], TN, "bf16") + fixed[n]
        for gi in range(grp):
            dstate[hg * grp + gi] = dst[gi]
        cts = chunks_vjp((do, dst_s[...]))[:5]

        dba_sum = None
        for gi in range(grp):
            h = hg * grp + gi
            lanes = slice(gi * HEAD_DIM, (gi + 1) * HEAD_DIM)
            per_head = [ct[gi * nchunk:(gi + 1) * nchunk].reshape(tb, ct.shape[-1]) for ct in cts]
            dqc, dkc, dvc, dba, dal, ddt = prep_vjps[gi](tuple(per_head))
            for i, (dxc, out, cw_ref) in enumerate(((dqc, dq_ref, cwq_ref), (dkc, dk_ref, cwk_ref),
                                                    (dvc, dv_ref, cwv_ref))):
                dx, dcw = _conv_backward(dbuf, 3 * h + i, xbuf, 3 * gi + i, cw_ref[:, lanes], dxc, tb)
                out[:, lanes] = dx.astype(out.dtype)
                dcw_ref[3 * h + i] += dcw
            dal_ref[...] += dal
            ddt_ref[...] += ddt
            dba_sum = dba if dba_sum is None else dba_sum + dba

        @pl.when(hg == 0)
        def _():
            dba_ref[...] = dba_sum.astype(dba_ref.dtype)

        @pl.when(hg > 0)
        def _():
            dba_ref[...] += dba_sum.astype(dba_ref.dtype)

    def cw_spec(off):
        return pl.BlockSpec((4, grp * HEAD_DIM), lambda t, hg: (0, off // grp + hg))

    def halo(off):
        per = tb // 8
        return pl.BlockSpec((8, grp * HEAD_DIM),
                            lambda t, hg: (jnp.maximum((nt - 1 - t) * per - 1, 0), off // grp + hg))

    def whole(shape):
        return pl.BlockSpec(shape, lambda t, hg: (0,) * len(shape))

    row128 = whole((1, HEAD_DIM))
    blk = (tb, HEAD_DIM)
    act = jax.ShapeDtypeStruct((s, D_MODEL), BF16)
    row_out = jax.ShapeDtypeStruct((1, HEAD_DIM), F32)
    return pl.pallas_call(
        body, name=name, grid=(nt, HEADS // grp),
        in_specs=[col(COL_Q), halo(COL_Q), col(COL_K), halo(COL_K), col(COL_V), halo(COL_V), col(COL_DN_Z),
                  pl.BlockSpec(blk, lambda t, hg: (nt - 1 - t, COL_BA)), col(0),
                  pl.BlockSpec((grp, nchunk, HEAD_DIM, HEAD_DIM), lambda t, hg: (hg, nt - 1 - t, 0, 0)),
                  pl.BlockSpec((grp, nchunk, CHUNK, CHUNK), lambda t, hg: (hg, nt - 1 - t, 0, 0)), col(HEADS),
                  cw_spec(0), cw_spec(HEADS), cw_spec(2 * HEADS), row128, row128, row128],
        out_specs=[col(0), col(0), col(0), col(0), pl.BlockSpec(blk, lambda t, hg: (nt - 1 - t, 0)),
                   whole((3 * HEADS, 4, HEAD_DIM)), row128, row128, row128],
        out_shape=[act, act, act, act, jax.ShapeDtypeStruct((s, HEAD_DIM), F32),
                   jax.ShapeDtypeStruct((3 * HEADS, 4, HEAD_DIM), F32), row_out, row_out, row_out],
        scratch_shapes=[pltpu.VMEM((3 * grp, tb + 8, HEAD_DIM), F32), pltpu.VMEM((3 * HEADS, tb + 8, HEAD_DIM), F32),
                        pltpu.VMEM((HEADS, HEAD_DIM, HEAD_DIM), F32),
                        pltpu.VMEM((grp * nchunk, HEAD_DIM, HEAD_DIM), F32)],
        compiler_params=_params(dimension_semantics=("arbitrary", "arbitrary")),
    )(proj, proj, proj, proj, proj, proj, proj, proj, o, states, inverses, dy, conv_w, conv_w, conv_w, a_log_row, dt_row,
      nw)


def _mesh_position():
    x, y, c = lax.axis_index("x"), lax.axis_index("y"), lax.axis_index("c")
    return x, y, c, 4 * x + 2 * y + c


def _peer(k, x, y, c):
    px = 1 - x if k & 4 else x
    py = 1 - y if k & 2 else y
    pc = 1 - c if k & 1 else c
    return (px, py, pc), 4 * px + 2 * py + pc


def _exchange_copies(ins, lands, scatter, send_sems, recv_sems, receives=True):
    x, y, c, me = _mesh_position()
    sends, recvs = [], []
    for i, (src, land) in enumerate(zip(ins, lands)):
        for k in range(1, N_DEV):
            peer, peer_id = _peer(k, x, y, c)
            sem = i * (N_DEV - 1) + k - 1
            for dst, out in ((me, sends), (peer_id, recvs)) if receives else ((me, sends),):
                out.append(pltpu.make_async_remote_copy(
                    src_ref=src.at[peer_id] if scatter[i] else src, dst_ref=land.at[dst],
                    send_sem=send_sems.at[sem], recv_sem=recv_sems.at[sem],
                    device_id=peer, device_id_type=pl.DeviceIdType.MESH))
    return sends, recvs


def _landing_shape(a, scatter):
    return a.shape if scatter else (N_DEV,) + a.shape


def _two_level_gather(arrays, name):
    n = len(arrays)
    per = N_DEV - 1

    def body(*refs):
        ins, outs = refs[:n], refs[n:2 * n]
        send_sems, recv_sems, local_sems = refs[2 * n:]
        x, y, c, me = _mesh_position()
        sibling = (x, y, 1 - c)
        chips = [(1 - x, y), (x, 1 - y), (1 - x, 1 - y)]

        def copy(i, k, block, to, src=None):
            slot = outs[i].at[4 * block[0] + 2 * block[1] + block[2]]
            return pltpu.make_async_remote_copy(
                src_ref=slot if src is None else src, dst_ref=slot,
                send_sem=send_sems.at[i * per + k], recv_sem=recv_sems.at[i * per + k],
                device_id=to, device_id_type=pl.DeviceIdType.MESH)

        local = [pltpu.make_async_copy(ins[i], outs[i].at[me], local_sems.at[i]) for i in range(n)]
        first = []
        for i in range(n):
            first.append(copy(i, 0, (x, y, c), sibling, src=ins[i]))
            first += [copy(i, 1 + j, (x, y, c), (*chip, c), src=ins[i]) for j, chip in enumerate(chips)]
        for cp in local + first:
            cp.start()
        passed = []
        for i in range(n):
            for j, chip in enumerate(chips):
                copy(i, 1 + j, (*chip, c), (x, y, c)).wait_recv()
                passed.append(copy(i, 4 + j, (*chip, c), sibling))
                passed[-1].start()
        for i in range(n):
            copy(i, 0, sibling, (x, y, c)).wait_recv()
            for j, chip in enumerate(chips):
                copy(i, 4 + j, (*chip, 1 - c), (x, y, c)).wait_recv()
        for cp in first + passed:
            cp.wait_send()
        for cp in local:
            cp.wait()

    hbm = pl.BlockSpec(memory_space=pl.ANY)
    return pl.pallas_call(
        body, name=name, in_specs=[hbm] * n, out_specs=[hbm] * n,
        out_shape=[jax.ShapeDtypeStruct((N_DEV,) + a.shape, a.dtype) for a in arrays],
        scratch_shapes=[pltpu.SemaphoreType.DMA((n * per,)), pltpu.SemaphoreType.DMA((n * per,)),
                        pltpu.SemaphoreType.DMA((n,))],
    )(*arrays)


_HBM = pl.BlockSpec(memory_space=pltpu.HBM)
_SEM = pl.BlockSpec(memory_space=pltpu.SEMAPHORE)
_DATAFLOW = pltpu.SideEffectType.DATAFLOW_SIDE_EFFECTING


def _exchange_start(arrays, scatter, name):
    n = len(arrays)
    srcs = [pltpu.with_memory_space_constraint(a, pltpu.HBM) for a in arrays]
    lands = [pltpu.with_memory_space_constraint(lax.empty(_landing_shape(a, sc), a.dtype), pltpu.HBM)
             for a, sc in zip(arrays, scatter)]
    nsem = n * (N_DEV - 1)

    def body(*refs):
        ins, zones = refs[:n], refs[n:2 * n]
        send_sems, recv_sems = refs[2 * n], refs[2 * n + 1]
        token = refs[-1]
        sends, _ = _exchange_copies(ins, zones, scatter, send_sems, recv_sems, receives=False)
        for cp in sends:
            cp.start()
        token[...] = jnp.zeros_like(token)

    res = pl.pallas_call(
        body, name=name,
        out_shape=(pltpu.SemaphoreType.DMA((nsem,)), pltpu.SemaphoreType.DMA((nsem,)),
                   *[pltpu.HBM(a.shape, a.dtype) for a in srcs + lands], jax.ShapeDtypeStruct((8, HEAD_DIM), F32)),
        in_specs=[_HBM] * (2 * n),
        out_specs=(_SEM, _SEM, *[_HBM] * (2 * n), pl.BlockSpec(memory_space=pltpu.VMEM)),
        input_output_aliases={i: 2 + i for i in range(2 * n)},
        compiler_params=pltpu.CompilerParams(has_side_effects=_DATAFLOW),
    )(*srcs, *lands)
    return dict(sems=res[:2], srcs=res[2:2 + n], lands=res[2 + n:2 + 2 * n], token_block=res[-1],
                token=res[-1][0, 0], scatter=scatter)


def _exchange_wait(started, after, name):
    scatter = started["scatter"]
    n = len(scatter)

    def body(*refs):
        ins, zones = refs[:n], refs[n:2 * n]
        send_sems, recv_sems = refs[2 * n], refs[2 * n + 1]
        sends, recvs = _exchange_copies(ins, zones, scatter, send_sems, recv_sems)
        for cp in sends:
            cp.wait_send()
        for cp in recvs:
            cp.wait_recv()

    thru = list(started["srcs"]) + list(started["lands"])
    res = pl.pallas_call(
        body, name=name, out_shape=[pltpu.HBM(a.shape, a.dtype) for a in thru],
        in_specs=[_HBM] * (2 * n) + [_SEM, _SEM, pl.BlockSpec(memory_space=pl.ANY)], out_specs=[_HBM] * (2 * n),
        input_output_aliases={i: i for i in range(2 * n)},
        compiler_params=pltpu.CompilerParams(has_side_effects=_DATAFLOW),
    )(*thru, *started["sems"], after)
    me = 4 * lax.axis_index("x") + 2 * lax.axis_index("y") + lax.axis_index("c")
    out = []
    for src, got, sc in zip(res[:n], res[n:], scatter):
        own = lax.dynamic_index_in_dim(src, me, 0, keepdims=False) if sc else src
        out.append(lax.dynamic_update_index_in_dim(got, own, me, 0))
    return out


def _adamw(parts, w, m, v, name, rows_per_step, row_offset=0, into=None):
    rows, cols = parts.shape[1:]
    tr = min(rows_per_step, rows)
    assert rows % tr == 0 and row_offset % tr == 0, (name, rows, tr, row_offset)
    first = row_offset // tr
    c1 = 1.0 / (1.0 - ADAM_B1 ** ADAM_STEP)
    c2 = 1.0 / (1.0 - ADAM_B2 ** ADAM_STEP)

    def body(p_ref, w_ref, m_ref, v_ref, *rest):
        g_ref, d_ref, nm_ref, nv_ref = rest[-4:]
        g = p_ref[0].astype(F32)
        for d in range(1, N_DEV):
            g = g + p_ref[d].astype(F32)
        nm = ADAM_B1 * m_ref[...] + (1.0 - ADAM_B1) * g
        nv = ADAM_B2 * v_ref[...] + (1.0 - ADAM_B2) * (g * g)
        g_ref[...] = g
        nm_ref[...] = nm
        nv_ref[...] = nv
        d_ref[...] = -ADAM_LR * ((nm * c1) / (jnp.sqrt(nv * c2) + ADAM_EPS) + ADAM_WD * w_ref[...])

    blk = pl.BlockSpec((tr, cols), lambda i: (i + first, 0))
    shape = jax.ShapeDtypeStruct(w.shape, F32)
    prior = [] if into is None else list(into)
    return pl.pallas_call(
        body, name=name, grid=(rows // tr,),
        in_specs=[pl.BlockSpec((N_DEV, tr, cols), lambda i: (0, i, 0)), blk, blk, blk]
        + [pl.BlockSpec(memory_space=pl.ANY)] * len(prior),
        out_specs=[blk] * 4, out_shape=[shape] * 4,
        input_output_aliases={4 + j: j for j in range(len(prior))}, compiler_params=_params(),
    )(parts, w, m, v, *prior)


_LAYERED = ("norm_w", "lru_conv_b", "lru_wa", "lru_ba", "lru_wx", "lru_bx", "lru_lambda", "lru_norm_w",
            "dn_A_log", "dn_dt_bias", "dn_norm_w")
_PACK_LRU = _LAYERED[1:8]
_PACK_LAST = _LAYERED[:1] + _LAYERED[8:]
_WEIGHTS = ("norm_w", "w_in", "lru_conv_w", "lru_conv_b", "lru_wa", "lru_ba", "lru_wx", "lru_bx", "lru_lambda",
            "lru_norm_w", "dn_conv_w", "dn_A_log", "dn_dt_bias", "dn_norm_w", "w_out", "final_norm_w")


def _pack_layer(tree, layer, tail=(), names=_LAYERED):
    rows = []
    for name in names:
        a = tree[name][layer]
        if a.shape[-1] == HEADS:
            a = jnp.pad(a, (0, HEAD_DIM - HEADS))
        rows.append(a.reshape(-1, HEAD_DIM))
    rows += [t.reshape(-1, HEAD_DIM) for t in tail]
    packed = jnp.concatenate(rows, axis=0)
    return jnp.pad(packed, ((0, (-packed.shape[0]) % 8), (0, 0)))


def _unpack_layer(packed, like, names=_LAYERED):
    out, at = {}, 0
    for name in names:
        shape = like[name].shape[1:]
        if shape[-1] == HEADS:
            n = 1
            out[name] = packed[at, :HEADS]
        else:
            n = like[name][0].size // HEAD_DIM
            out[name] = packed[at:at + n].reshape(shape)
        at += n
    return out, at


def _heads_to_channels(a):
    return jnp.transpose(a, (1, 0, 2)).reshape(a.shape[1], HEADS * HEAD_DIM)


def kernel(x, norm_w, w_in, lru_conv_w, lru_conv_b, lru_wa, lru_ba, lru_wx, lru_bx, lru_lambda, lru_norm_w, dn_conv_w, dn_A_log, dn_dt_bias, dn_norm_w, w_out, final_norm_w, loss_target, m_norm_w, m_w_in, m_lru_conv_w, m_lru_conv_b, m_lru_wa, m_lru_ba, m_lru_wx, m_lru_bx, m_lru_lambda, m_lru_norm_w, m_dn_conv_w, m_dn_A_log, m_dn_dt_bias, m_dn_norm_w, m_w_out, m_final_norm_w, v_norm_w, v_w_in, v_lru_conv_w, v_lru_conv_b, v_lru_wa, v_lru_ba, v_lru_wx, v_lru_bx, v_lru_lambda, v_lru_norm_w, v_dn_conv_w, v_dn_A_log, v_dn_dt_bias, v_dn_norm_w, v_w_out, v_final_norm_w):
    weights = dict(norm_w=norm_w, w_in=w_in, lru_conv_w=lru_conv_w, lru_conv_b=lru_conv_b, lru_wa=lru_wa,
                   lru_ba=lru_ba, lru_wx=lru_wx, lru_bx=lru_bx, lru_lambda=lru_lambda, lru_norm_w=lru_norm_w,
                   dn_conv_w=dn_conv_w, dn_A_log=dn_A_log, dn_dt_bias=dn_dt_bias, dn_norm_w=dn_norm_w,
                   w_out=w_out, final_norm_w=final_norm_w)
    mom_m = dict(norm_w=m_norm_w, w_in=m_w_in, lru_conv_w=m_lru_conv_w, lru_conv_b=m_lru_conv_b, lru_wa=m_lru_wa,
                 lru_ba=m_lru_ba, lru_wx=m_lru_wx, lru_bx=m_lru_bx, lru_lambda=m_lru_lambda,
                 lru_norm_w=m_lru_norm_w, dn_conv_w=m_dn_conv_w, dn_A_log=m_dn_A_log, dn_dt_bias=m_dn_dt_bias,
                 dn_norm_w=m_dn_norm_w, w_out=m_w_out, final_norm_w=m_final_norm_w)
    mom_v = dict(norm_w=v_norm_w, w_in=v_w_in, lru_conv_w=v_lru_conv_w, lru_conv_b=v_lru_conv_b, lru_wa=v_lru_wa,
                 lru_ba=v_lru_ba, lru_wx=v_lru_wx, lru_bx=v_lru_bx, lru_lambda=v_lru_lambda,
                 lru_norm_w=v_lru_norm_w, dn_conv_w=v_dn_conv_w, dn_A_log=v_dn_A_log, dn_dt_bias=v_dn_dt_bias,
                 dn_norm_w=v_dn_norm_w, w_out=v_w_out, final_norm_w=v_final_norm_w)
    depth = norm_w.shape[0]
    xs = x[0]
    s = xs.shape[0]
    tm = min(1024, s)

    assert depth >= 2, depth

    def row(a):
        return a.reshape(1, -1)

    def pad_row(a):
        return jnp.pad(a, (0, HEAD_DIM - a.shape[0])).reshape(1, HEAD_DIM)

    def full_w_in(g):
        w = jnp.transpose(g, (1, 2, 0, 3)).reshape(g.shape[1], D_MODEL, D_IN)
        return jnp.pad(w, ((0, 0), (0, 0), (0, D_IN_PAD - D_IN)))

    g_win0, g_lcw, g_dcw = _two_level_gather([w_in[:1].astype(BF16), lru_conv_w, dn_conv_w], "gather_first")
    rest = _exchange_start([w_in[1:].astype(BF16), w_out.astype(BF16)], [False] * 2, "gather_rest_start")
    win = [full_w_in(g_win0)[0]]
    wout = None
    lcw = jnp.transpose(g_lcw, (1, 2, 0, 3)).reshape(depth, 4, D_MODEL)
    dcw = jnp.transpose(g_dcw, (1, 2, 0, 3)).reshape(depth, 4, 3 * D_MODEL)

    saved = []
    cur = xs
    for l in range(depth):
        if l == 0:
            hn = _rmsnorm_fwd(cur, row(norm_w[l]) + rest["token"], f"norm_fwd_{l}")
        proj = _matmul(hn, win[l], "nn", tm, 896, D_MODEL, f"in_proj_{l}")
        y_lru, hs = _lru_fwd(proj, lcw[l], row(lru_conv_b[l]), lru_wa[l], row(lru_ba[l]), lru_wx[l], row(lru_bx[l]),
                             row(lru_lambda[l]), row(lru_norm_w[l]), f"lru_fwd_{l}")
        ycat, o_dn, states, inverses = _dn_fwd(proj, y_lru, dcw[l], pad_row(dn_A_log[l]), pad_row(dn_dt_bias[l]),
                                     row(dn_norm_w[l]), f"dn_fwd_{l}")
        if l == 0:
            g_win_rest, g_wout = _exchange_wait(rest, ycat, "gather_rest_wait")
            win += list(full_w_in(g_win_rest))
            wout = jnp.transpose(g_wout, (1, 0, 2, 3)).reshape(depth, 2 * D_MODEL, D_MODEL)
        if l + 1 < depth:
            nxt, hn_next = _matmul(ycat, wout[l], "nn", tm, D_MODEL, 2 * D_MODEL, f"out_proj_{l}", add=cur,
                                   norm_w=row(norm_w[l + 1]))
        else:
            nxt, hn_next = _matmul(ycat, wout[l], "nn", tm, D_MODEL, 2 * D_MODEL, f"out_proj_{l}", add=cur), None
        saved.append((cur, hn, proj, hs, o_dn, states, inverses, ycat))
        cur, hn = nxt, hn_next
    loss_part, dx, d_final = _final_loss(cur, row(final_norm_w), loss_target[0], "final_loss")

    def win_slots(g):
        return jnp.transpose(g.reshape(D_MODEL, N_DEV, D_IN // N_DEV), (1, 0, 2))

    def wout_slots(g):
        return g.reshape(N_DEV, 2 * D_MODEL // N_DEV, D_MODEL)

    grads = {k: [None] * depth for k in _WEIGHTS if k not in ("final_norm_w", "w_in", "w_out")}
    started = {}
    token = None
    for l in reversed(range(depth)):
        x_in, hn, proj, hs, o_dn, states, inverses, ycat = saved[l]
        dy = _matmul(dx, wout[l], "nt", tm, D_MODEL, D_MODEL, f"out_proj_dy_{l}")
        g_wout_l = _matmul(ycat, dx, "tn", D_MODEL, D_MODEL, tm, f"out_proj_dw_{l}", out_dtype=BF16)
        if l == 0:
            started["w_out_0"] = _exchange_start([wout_slots(g_wout_l)], [True], "exchange_w_out_0_start")
            token = token + started["w_out_0"]["token"]
        cb_row = row(lru_conv_b[l]) if token is None else row(lru_conv_b[l]) + token
        (dlx, dlz, g_lcw, g_lcb, g_wa, g_ba, g_wx, g_bx, g_lam, g_lnw) = _lru_bwd(
            proj, hs, dy, lcw[l], cb_row, lru_wa[l], row(lru_ba[l]), lru_wx[l], row(lru_bx[l]),
            row(lru_lambda[l]), row(lru_norm_w[l]), f"lru_bwd_{l}")
        grads["lru_conv_w"][l] = _heads_to_channels(g_lcw)
        grads["lru_conv_b"][l] = g_lcb.reshape(D_MODEL)
        grads["lru_wa"][l] = g_wa
        grads["lru_ba"][l] = g_ba.reshape(D_MODEL)
        grads["lru_wx"][l] = g_wx
        grads["lru_bx"][l] = g_bx.reshape(D_MODEL)
        grads["lru_lambda"][l] = g_lam.reshape(D_MODEL)
        grads["lru_norm_w"][l] = g_lnw.reshape(D_MODEL)
        al_row = pad_row(dn_A_log[l])
        if l == 0:
            started["pack_0"] = _exchange_start([_pack_layer(grads, 0, names=_PACK_LRU)], [False],
                                                "exchange_pack_0_start")
            al_row = al_row + started["pack_0"]["token"]
        (dq, dk, dv, ddz, dba, g_dcw3, g_al, g_dt, g_dnw) = _dn_bwd(
            proj, o_dn, states, inverses, dy, dcw[l], al_row, pad_row(dn_dt_bias[l]), row(dn_norm_w[l]), f"dn_bwd_{l}")
        g_dcw3 = g_dcw3.reshape(HEADS, 3, 4, HEAD_DIM)
        grads["dn_conv_w"][l] = jnp.concatenate([_heads_to_channels(g_dcw3[:, i]) for i in range(3)], axis=1)
        grads["dn_A_log"][l] = g_al[0, :HEADS]
        grads["dn_dt_bias"][l] = g_dt[0, :HEADS]
        grads["dn_norm_w"][l] = g_dnw.reshape(HEAD_DIM)
        dep = None
        pieces = [dlx, dlz, dq, dk, dv, ddz]
        wide = len(pieces) * D_MODEL
        dba = dba.astype(BF16)
        g_win_l = jnp.concatenate(
            [_matmul_tn_parts(hn, pieces, D_MODEL, D_MODEL, tm, f"in_proj_dw_{l}", BF16),
             _matmul(hn, dba, "tn", D_MODEL, HEAD_DIM, tm, f"in_proj_dw_gates_{l}", out_dtype=BF16)[:, :D_IN - wide]],
            axis=1)
        if l == 0:
            started[0] = _exchange_start([win_slots(g_win_l)], [True], "exchange_0_start")
            dep = started[0]["token_block"]
        dh = _matmul_nt_parts(pieces, dba, win[l], tm, D_MODEL, f"in_proj_dh_{l}", dep=dep)
        dx, g_nw = _rmsnorm_bwd(x_in, row(norm_w[l]), dh, dx, f"norm_bwd_{l}")
        grads["norm_w"][l] = g_nw.reshape(D_MODEL)
        if l > 0:
            tail = (d_final, loss_part) if l == depth - 1 else ()
            started[l] = _exchange_start([win_slots(g_win_l), wout_slots(g_wout_l), _pack_layer(grads, l, tail)],
                                         [True, True, False], f"exchange_{l}_start")
            token = started[l]["token"]

    def conv_slots(a):
        dd, r, cc = a.shape
        return jnp.transpose(a.reshape(dd, r, N_DEV, cc // N_DEV), (2, 0, 1, 3))

    small = _exchange_start(
        [conv_slots(jnp.stack(grads["lru_conv_w"])), conv_slots(jnp.stack(grads["dn_conv_w"])),
         _pack_layer(grads, 0, names=_PACK_LAST)], [True, True, False], "exchange_small_start")

    new = {}
    flat_in = (depth * D_MODEL, D_IN // N_DEV)
    flat_out = (depth * 2 * D_MODEL // N_DEV, D_MODEL)
    zero_row = jnp.zeros((1, HEAD_DIM), F32)

    def adamw_pack(parts, layer, names=_LAYERED, name="adamw_small"):
        tails = [(t, zero_row) if layer == depth - 1 else () for t in (final_norm_w, m_final_norm_w, v_final_norm_w)]
        return _adamw(parts, _pack_layer(weights, layer, tails[0], names), _pack_layer(mom_m, layer, tails[1], names),
                      _pack_layer(mom_v, layer, tails[2], names), f"{name}_{layer}", parts.shape[1])

    def adamw_w_in(parts, layer, into):
        return _adamw(parts, w_in.reshape(flat_in), m_w_in.reshape(flat_in), v_w_in.reshape(flat_in),
                      f"adamw_w_in_{layer}", 256, layer * D_MODEL, into)

    def adamw_w_out(parts, layer, into):
        return _adamw(parts, w_out.reshape(flat_out), m_w_out.reshape(flat_out), v_w_out.reshape(flat_out),
                      f"adamw_w_out_{layer}", 256, layer * flat_out[0] // depth, into)

    acc_in = acc_out = None
    packs = [None] * depth
    after = small["token_block"]
    for l in reversed(range(1, depth)):
        r_win, r_wout, r_pack = _exchange_wait(started[l], after, f"exchange_{l}_wait")
        acc_in = adamw_w_in(r_win, l, acc_in)
        acc_out = adamw_w_out(r_wout, l, acc_out)
        packs[l] = adamw_pack(r_pack, l)
        after = packs[l][0]
    (r_wout,) = _exchange_wait(started["w_out_0"], after, "exchange_w_out_0_wait")
    acc_out = adamw_w_out(r_wout, 0, acc_out)
    (r_win,) = _exchange_wait(started[0], acc_out[0], "exchange_0_wait")
    acc_in = adamw_w_in(r_win, 0, acc_in)
    (r_pack,) = _exchange_wait(started["pack_0"], acc_in[0], "exchange_pack_0_wait")
    packs[0] = adamw_pack(r_pack, 0, _PACK_LRU)
    r_lcw, r_dcw, r_last = _exchange_wait(small, packs[0][0], "exchange_small_wait")
    for name, parts in (("lru_conv_w", r_lcw), ("dn_conv_w", r_dcw)):
        w = weights[name]
        flat = (-1, w.shape[-1])
        outs = _adamw(parts.reshape((N_DEV,) + (w.size // w.shape[-1], w.shape[-1])), w.reshape(flat),
                      mom_m[name].reshape(flat), mom_v[name].reshape(flat), f"adamw_{name}", 8)
        new[name] = [a.reshape(w.shape) for a in outs]
    last_0 = adamw_pack(r_last, 0, _PACK_LAST, "adamw_last")
    new["w_in"] = [a.reshape(w_in.shape) for a in acc_in]
    new["w_out"] = [a.reshape(w_out.shape) for a in acc_out]
    for i in range(4):
        layers = [{**_unpack_layer(packs[0][i], weights, _PACK_LRU)[0],
                   **_unpack_layer(last_0[i], weights, _PACK_LAST)[0]}]
        layers += [_unpack_layer(packs[l][i], weights)[0] for l in range(1, depth)]
        for name in _LAYERED:
            new.setdefault(name, []).append(jnp.stack([layer[name] for layer in layers]))
    tail_at = _unpack_layer(packs[depth - 1][0], weights)[1]
    rows_final = D_MODEL // HEAD_DIM
    new["final_norm_w"] = [packs[depth - 1][i][tail_at:tail_at + rows_final].reshape(D_MODEL) for i in range(4)]
    loss = packs[depth - 1][0][tail_at + rows_final, 0]
    out = [loss, dx.reshape(x.shape)]
    for i in range(4):
        out += [new[name][i] for name in _WEIGHTS]
    return tuple(out)
```
